```python
import jax, jax.numpy as jnp
from jax import lax
import numpy as np

D_MODEL = 1024
BATCH = 8
SEQ = 4096
DEPTH = 1

PLE_DIM = 256
ROPE_THETA = 10000.0
RMS_EPS = 1e-6
BLOCK = 128

SWA_WINDOW = 128
A_HEADS = 8
A_KV_HEADS = 2
A_HEAD_DIM = 64
A_WIDTH = A_HEADS * A_HEAD_DIM

B_HEADS = 8
Q_LORA = 256
KV_LORA = 128
NOPE_DIM = 64
ROPE_DIM = 32
V_DIM = 64
B_WIDTH = B_HEADS * V_DIM

SPLIT_SIZES = (A_HEADS * A_HEAD_DIM,
               A_KV_HEADS * A_HEAD_DIM,
               A_KV_HEADS * A_HEAD_DIM,
               Q_LORA,
               KV_LORA,
               ROPE_DIM,
               2 * D_MODEL)
IN_COLS = 768 + 416 + 2 * D_MODEL

D_FF = 2816
CONV_W = 3

kernel_name = "hybrid_swa_mla_gated_block"


def split_cols(t, sizes):
    idx = np.cumsum(np.array(sizes))[:-1].tolist()
    return jnp.split(t, idx, axis=-1)


def rmsnorm(t, g):
    tf = t.astype(jnp.float32)
    y = tf * lax.rsqrt(jnp.mean(tf * tf, axis=-1, keepdims=True) + RMS_EPS)
    return (y * g.astype(jnp.float32)).astype(t.dtype)


def rope_tables(positions, dim):
    inv = ROPE_THETA ** (-(jnp.arange(0, dim, 2, dtype=jnp.float32) / dim))
    ang = positions.astype(jnp.float32)[..., None] * inv
    return jnp.cos(ang), jnp.sin(ang)


def apply_rope(t, cos, sin):
    tf = t.astype(jnp.float32)
    t1, t2 = jnp.split(tf, 2, axis=-1)
    c, s = cos[:, :, None, :], sin[:, :, None, :]
    return jnp.concatenate([t1 * c - t2 * s, t2 * c + t1 * s], axis=-1).astype(t.dtype)


def swa_attention(q, k, v, sinks):
    B_, S_, H, d = q.shape
    G = H // A_KV_HEADS
    nblk = S_ // BLOCK
    qb = q.reshape(B_, nblk, BLOCK, A_KV_HEADS, G, d)
    pad = ((0, 0), (BLOCK, 0), (0, 0), (0, 0))
    kp = jnp.pad(k, pad).reshape(B_, nblk + 1, BLOCK, A_KV_HEADS, d)
    vp = jnp.pad(v, pad).reshape(B_, nblk + 1, BLOCK, A_KV_HEADS, d)
    kb = jnp.concatenate([kp[:, :-1], kp[:, 1:]], axis=2)
    vb = jnp.concatenate([vp[:, :-1], vp[:, 1:]], axis=2)
    s = jnp.einsum('bnqkgd,bnskd->bnkgqs', qb, kb).astype(jnp.float32) * (d ** -0.5)
    qi = jnp.arange(BLOCK)[:, None]
    kj = jnp.arange(2 * BLOCK)[None, :]
    rel = qi + BLOCK - kj
    band = (rel >= 0) & (rel < SWA_WINDOW)
    key_abs = jnp.arange(nblk)[:, None, None] * BLOCK + kj[None] - BLOCK
    valid = band[None] & (key_abs >= 0)
    s = jnp.where(valid[None, :, None, None], s, -jnp.inf)
    sink = sinks.astype(jnp.float32).reshape(1, 1, A_KV_HEADS, G, 1, 1)
    m = jnp.maximum(jnp.max(s, axis=-1, keepdims=True), sink)
    e = jnp.exp(s - m)
    pr = e / (jnp.sum(e, axis=-1, keepdims=True) + jnp.exp(sink - m))
    out = jnp.einsum('bnkgqs,bnskd->bnqkgd', pr.astype(v.dtype), vb)
    return out.reshape(B_, S_, H * d)


def mla_attention(q, k, v):
    B_, S_, H, dqk = q.shape
    dv = v.shape[-1]
    nblk = S_ // BLOCK
    scale = dqk ** -0.5
    qb = q.reshape(B_, nblk, BLOCK, H, dqk).transpose(1, 0, 2, 3, 4)
    key_pos = jnp.arange(S_)

    def one_block(args):
        qblk, n = args
        s = jnp.einsum('bqhd,bshd->bhqs', qblk, k).astype(jnp.float32) * scale
        q_pos = n * BLOCK + jnp.arange(BLOCK)
        causal = key_pos[None, :] <= q_pos[:, None]
        pr = jax.nn.softmax(jnp.where(causal, s, -jnp.inf), axis=-1)
        return jnp.einsum('bhqs,bshd->bqhd', pr.astype(v.dtype), v)

    out = lax.map(one_block, (qb, jnp.arange(nblk)))
    return out.transpose(1, 0, 2, 3, 4).reshape(B_, S_, H * dv)


def causal_dwconv(u, w, b):
    C = u.shape[-1]
    y = lax.conv_general_dilated(u, w[:, None, :].astype(u.dtype), window_strides=(1,),
                                 padding=[(CONV_W - 1, 0)],
                                 dimension_numbers=('NWC', 'WIO', 'NWC'),
                                 feature_group_count=C)
    return y + b


def _fwd_setup_inputs(seed: int = 0) -> dict:
    key = jax.random.key(seed)
    ks = jax.random.split(key, 24)
    f32 = jnp.float32

    def w(k, shape, fan_in):
        return jax.random.normal(k, shape, f32) * (fan_in ** -0.5)

    def gain(k, dim):
        return 1.0 + 0.02 * jax.random.normal(k, (DEPTH, dim), f32)

    x = jax.random.normal(ks[0], (BATCH, SEQ, D_MODEL), f32)
    p = jax.random.normal(ks[1], (DEPTH, BATCH, SEQ, PLE_DIM), f32)
    start = jax.random.randint(ks[2], (BATCH, 1), 0, 1024, dtype=jnp.int32)
    positions = (start + jnp.arange(SEQ, dtype=jnp.int32)[None, :]).astype(jnp.int32)
    return {
        "x": x,
        "p": p,
        "positions": positions,
        "attn_pre_norm": gain(ks[3], D_MODEL),
        "attn_post_norm": gain(ks[4], D_MODEL),
        "w_in": w(ks[5], (DEPTH, D_MODEL, IN_COLS), D_MODEL),
        "b_gate": 0.01 * jax.random.normal(ks[6], (DEPTH, 2 * D_MODEL), f32),
        "sinks": 0.5 * jax.random.normal(ks[7], (DEPTH, A_HEADS), f32),
        "q_a_norm": gain(ks[8], Q_LORA),
        "w_uq": w(ks[9], (DEPTH, Q_LORA, B_HEADS * (NOPE_DIM + ROPE_DIM)), Q_LORA),
        "kv_a_norm": gain(ks[10], KV_LORA),
        "w_ukv": w(ks[11], (DEPTH, KV_LORA, B_HEADS * (NOPE_DIM + V_DIM)), KV_LORA),
        "w_branch_a": w(ks[12], (DEPTH, A_WIDTH, D_MODEL), A_WIDTH),
        "w_branch_b": w(ks[13], (DEPTH, B_WIDTH, D_MODEL), B_WIDTH),
        "w_out": w(ks[14], (DEPTH, D_MODEL, D_MODEL), D_MODEL),
        "mlp_pre_norm": gain(ks[15], D_MODEL),
        "mlp_post_norm": gain(ks[16], D_MODEL),
        "w_up": w(ks[17], (DEPTH, D_MODEL, 2 * D_FF), D_MODEL),
        "conv_w": w(ks[18], (DEPTH, CONV_W, 2 * D_FF), CONV_W),
        "conv_b": 0.01 * jax.random.normal(ks[19], (DEPTH, 2 * D_FF), f32),
        "w_down": w(ks[20], (DEPTH, D_FF, D_MODEL), D_FF),
        "ple_norm": gain(ks[21], D_MODEL),
        "w_ple_gate": w(ks[22], (DEPTH, D_MODEL, D_MODEL), D_MODEL),
        "w_ple": w(ks[23], (DEPTH, PLE_DIM, D_MODEL), PLE_DIM),
    }


def _fwd_reference(x, p, positions, attn_pre_norm, attn_post_norm, w_in, b_gate, sinks,
              q_a_norm, w_uq, kv_a_norm, w_ukv, w_branch_a, w_branch_b, w_out,
              mlp_pre_norm, mlp_post_norm, w_up, conv_w, conv_b, w_down,
              ple_norm, w_ple_gate, w_ple):
    B_, S_, _ = x.shape
    cos_a, sin_a = rope_tables(positions, A_HEAD_DIM)
    cos_b, sin_b = rope_tables(positions, ROPE_DIM)
    for i in range(DEPTH):
        h = rmsnorm(x, attn_pre_norm[i])
        qa, ka, va, cq, ckv, kr, gates = split_cols(h @ w_in[i], SPLIT_SIZES)

        qa = apply_rope(qa.reshape(B_, S_, A_HEADS, A_HEAD_DIM), cos_a, sin_a)
        ka = apply_rope(ka.reshape(B_, S_, A_KV_HEADS, A_HEAD_DIM), cos_a, sin_a)
        va = va.reshape(B_, S_, A_KV_HEADS, A_HEAD_DIM)
        ya = swa_attention(qa, ka, va, sinks[i])

        qb = (rmsnorm(cq, q_a_norm[i]) @ w_uq[i]).reshape(B_, S_, B_HEADS, NOPE_DIM + ROPE_DIM)
        q_nope, q_pe = split_cols(qb, (NOPE_DIM, ROPE_DIM))
        q_pe = apply_rope(q_pe, cos_b, sin_b)
        kvb = (rmsnorm(ckv, kv_a_norm[i]) @ w_ukv[i]).reshape(B_, S_, B_HEADS, NOPE_DIM + V_DIM)
        k_nope, vb = split_cols(kvb, (NOPE_DIM, V_DIM))
        k_pe = apply_rope(kr[:, :, None, :], cos_b, sin_b)
        qb = jnp.concatenate([q_nope, q_pe], axis=-1)
        kb = jnp.concatenate([k_nope, jnp.broadcast_to(k_pe, (B_, S_, B_HEADS, ROPE_DIM))], axis=-1)
        yb = mla_attention(qb, kb, vb)

        gate_a, gate_b = jnp.split(jax.nn.sigmoid(gates + b_gate[i]), 2, axis=-1)
        mixed = gate_a * (ya @ w_branch_a[i]) + gate_b * (yb @ w_branch_b[i])
        x = x + rmsnorm(mixed @ w_out[i], attn_post_norm[i])

        h = rmsnorm(x, mlp_pre_norm[i])
        u = causal_dwconv(h @ w_up[i], conv_w[i], conv_b[i])
        u_gate, u_val = jnp.split(u, 2, axis=-1)
        ff = (jax.nn.gelu(u_gate, approximate=True) * u_val) @ w_down[i]
        x = x + rmsnorm(ff, mlp_post_norm[i])

        e = p[i] @ w_ple[i]
        x = x + jax.nn.sigmoid(rmsnorm(x, ple_norm[i]) @ w_ple_gate[i]) * e
    return x


import jax as _jax
import jax.numpy as _jnp

TWIN_FORMAT = 'train_step'
FWD_PARAMS = ['x', 'p', 'positions', 'attn_pre_norm', 'attn_post_norm', 'w_in', 'b_gate', 'sinks', 'q_a_norm', 'w_uq', 'kv_a_norm', 'w_ukv', 'w_branch_a', 'w_branch_b', 'w_out', 'mlp_pre_norm', 'mlp_post_norm', 'w_up', 'conv_w', 'conv_b', 'w_down', 'ple_norm', 'w_ple_gate', 'w_ple']
TWIN_WEIGHTS = ['attn_pre_norm', 'attn_post_norm', 'w_in', 'b_gate', 'sinks', 'q_a_norm', 'w_uq', 'kv_a_norm', 'w_ukv', 'w_branch_a', 'w_branch_b', 'w_out', 'mlp_pre_norm', 'mlp_post_norm', 'w_up', 'conv_w', 'conv_b', 'w_down', 'ple_norm', 'w_ple_gate', 'w_ple']
TWIN_DIFF_INPUT = 'x'
TWIN_INPUTS = ['x', 'p', 'positions', 'attn_pre_norm', 'attn_post_norm', 'w_in', 'b_gate', 'sinks', 'q_a_norm', 'w_uq', 'kv_a_norm', 'w_ukv', 'w_branch_a', 'w_branch_b', 'w_out', 'mlp_pre_norm', 'mlp_post_norm', 'w_up', 'conv_w', 'conv_b', 'w_down', 'ple_norm', 'w_ple_gate', 'w_ple', 'loss_target', 'm_attn_pre_norm', 'm_attn_post_norm', 'm_w_in', 'm_b_gate', 'm_sinks', 'm_q_a_norm', 'm_w_uq', 'm_kv_a_norm', 'm_w_ukv', 'm_w_branch_a', 'm_w_branch_b', 'm_w_out', 'm_mlp_pre_norm', 'm_mlp_post_norm', 'm_w_up', 'm_conv_w', 'm_conv_b', 'm_w_down', 'm_ple_norm', 'm_w_ple_gate', 'm_w_ple', 'v_attn_pre_norm', 'v_attn_post_norm', 'v_w_in', 'v_b_gate', 'v_sinks', 'v_q_a_norm', 'v_w_uq', 'v_kv_a_norm', 'v_w_ukv', 'v_w_branch_a', 'v_w_branch_b', 'v_w_out', 'v_mlp_pre_norm', 'v_mlp_post_norm', 'v_w_up', 'v_conv_w', 'v_conv_b', 'v_w_down', 'v_ple_norm', 'v_w_ple_gate', 'v_w_ple']
TWIN_OUTPUTS = ['loss', 'grad_x', 'grad_attn_pre_norm', 'grad_attn_post_norm', 'grad_w_in', 'grad_b_gate', 'grad_sinks', 'grad_q_a_norm', 'grad_w_uq', 'grad_kv_a_norm', 'grad_w_ukv', 'grad_w_branch_a', 'grad_w_branch_b', 'grad_w_out', 'grad_mlp_pre_norm', 'grad_mlp_post_norm', 'grad_w_up', 'grad_conv_w', 'grad_conv_b', 'grad_w_down', 'grad_ple_norm', 'grad_w_ple_gate', 'grad_w_ple', 'delta_attn_pre_norm', 'delta_attn_post_norm', 'delta_w_in', 'delta_b_gate', 'delta_sinks', 'delta_q_a_norm', 'delta_w_uq', 'delta_kv_a_norm', 'delta_w_ukv', 'delta_w_branch_a', 'delta_w_branch_b', 'delta_w_out', 'delta_mlp_pre_norm', 'delta_mlp_post_norm', 'delta_w_up', 'delta_conv_w', 'delta_conv_b', 'delta_w_down', 'delta_ple_norm', 'delta_w_ple_gate', 'delta_w_ple', 'new_m_attn_pre_norm', 'new_m_attn_post_norm', 'new_m_w_in', 'new_m_b_gate', 'new_m_sinks', 'new_m_q_a_norm', 'new_m_w_uq', 'new_m_kv_a_norm', 'new_m_w_ukv', 'new_m_w_branch_a', 'new_m_w_branch_b', 'new_m_w_out', 'new_m_mlp_pre_norm', 'new_m_mlp_post_norm', 'new_m_w_up', 'new_m_conv_w', 'new_m_conv_b', 'new_m_w_down', 'new_m_ple_norm', 'new_m_w_ple_gate', 'new_m_w_ple', 'new_v_attn_pre_norm', 'new_v_attn_post_norm', 'new_v_w_in', 'new_v_b_gate', 'new_v_sinks', 'new_v_q_a_norm', 'new_v_w_uq', 'new_v_kv_a_norm', 'new_v_w_ukv', 'new_v_w_branch_a', 'new_v_w_branch_b', 'new_v_w_out', 'new_v_mlp_pre_norm', 'new_v_mlp_post_norm', 'new_v_w_up', 'new_v_conv_w', 'new_v_conv_b', 'new_v_w_down', 'new_v_ple_norm', 'new_v_w_ple_gate', 'new_v_w_ple']
TWIN_LEAF_KINDS = {'loss': 'loss', 'grad_x': 'grad_x', 'grad_attn_pre_norm': 'grad_w', 'grad_attn_post_norm': 'grad_w', 'grad_w_in': 'grad_w', 'grad_b_gate': 'grad_w', 'grad_sinks': 'grad_w', 'grad_q_a_norm': 'grad_w', 'grad_w_uq': 'grad_w', 'grad_kv_a_norm': 'grad_w', 'grad_w_ukv': 'grad_w', 'grad_w_branch_a': 'grad_w', 'grad_w_branch_b': 'grad_w', 'grad_w_out': 'grad_w', 'grad_mlp_pre_norm': 'grad_w', 'grad_mlp_post_norm': 'grad_w', 'grad_w_up': 'grad_w', 'grad_conv_w': 'grad_w', 'grad_conv_b': 'grad_w', 'grad_w_down': 'grad_w', 'grad_ple_norm': 'grad_w', 'grad_w_ple_gate': 'grad_w', 'grad_w_ple': 'grad_w', 'delta_attn_pre_norm': 'delta_w', 'delta_attn_post_norm': 'delta_w', 'delta_w_in': 'delta_w', 'delta_b_gate': 'delta_w', 'delta_sinks': 'delta_w', 'delta_q_a_norm': 'delta_w', 'delta_w_uq': 'delta_w', 'delta_kv_a_norm': 'delta_w', 'delta_w_ukv': 'delta_w', 'delta_w_branch_a': 'delta_w', 'delta_w_branch_b': 'delta_w', 'delta_w_out': 'delta_w', 'delta_mlp_pre_norm': 'delta_w', 'delta_mlp_post_norm': 'delta_w', 'delta_w_up': 'delta_w', 'delta_conv_w': 'delta_w', 'delta_conv_b': 'delta_w', 'delta_w_down': 'delta_w', 'delta_ple_norm': 'delta_w', 'delta_w_ple_gate': 'delta_w', 'delta_w_ple': 'delta_w', 'new_m_attn_pre_norm': 'new_m', 'new_m_attn_post_norm': 'new_m', 'new_m_w_in': 'new_m', 'new_m_b_gate': 'new_m', 'new_m_sinks': 'new_m', 'new_m_q_a_norm': 'new_m', 'new_m_w_uq': 'new_m', 'new_m_kv_a_norm': 'new_m', 'new_m_w_ukv': 'new_m', 'new_m_w_branch_a': 'new_m', 'new_m_w_branch_b': 'new_m', 'new_m_w_out': 'new_m', 'new_m_mlp_pre_norm': 'new_m', 'new_m_mlp_post_norm': 'new_m', 'new_m_w_up': 'new_m', 'new_m_conv_w': 'new_m', 'new_m_conv_b': 'new_m', 'new_m_w_down': 'new_m', 'new_m_ple_norm': 'new_m', 'new_m_w_ple_gate': 'new_m', 'new_m_w_ple': 'new_m', 'new_v_attn_pre_norm': 'new_v', 'new_v_attn_post_norm': 'new_v', 'new_v_w_in': 'new_v', 'new_v_b_gate': 'new_v', 'new_v_sinks': 'new_v', 'new_v_q_a_norm': 'new_v', 'new_v_w_uq': 'new_v', 'new_v_kv_a_norm': 'new_v', 'new_v_w_ukv': 'new_v', 'new_v_w_branch_a': 'new_v', 'new_v_w_branch_b': 'new_v', 'new_v_w_out': 'new_v', 'new_v_mlp_pre_norm': 'new_v', 'new_v_mlp_post_norm': 'new_v', 'new_v_w_up': 'new_v', 'new_v_conv_w': 'new_v', 'new_v_conv_b': 'new_v', 'new_v_w_down': 'new_v', 'new_v_ple_norm': 'new_v', 'new_v_w_ple_gate': 'new_v', 'new_v_w_ple': 'new_v'}


def _forward(args):
    return _fwd_reference(*[args[k] for k in FWD_PARAMS])


def _output_shape():
    out = _jax.eval_shape(lambda: _forward(_fwd_setup_inputs(0)))
    return out.shape, out.dtype

N_MICROBATCH = 1
ADAM_LR = 0.001
ADAM_B1 = 0.9
ADAM_B2 = 0.999
ADAM_EPS = 1e-08
ADAM_WD = 0.01
ADAM_STEP = 10
PER_EXAMPLE_BATCH_AXIS = {'x': 0, 'p': 1, 'positions': 0, 'loss_target': 0}
SHARED_INPUTS = []
_WEIGHT_DTYPES = {'attn_pre_norm': _jnp.float32, 'attn_post_norm': _jnp.float32, 'w_in': _jnp.float32, 'b_gate': _jnp.float32, 'sinks': _jnp.float32, 'q_a_norm': _jnp.float32, 'w_uq': _jnp.float32, 'kv_a_norm': _jnp.float32, 'w_ukv': _jnp.float32, 'w_branch_a': _jnp.float32, 'w_branch_b': _jnp.float32, 'w_out': _jnp.float32, 'mlp_pre_norm': _jnp.float32, 'mlp_post_norm': _jnp.float32, 'w_up': _jnp.float32, 'conv_w': _jnp.float32, 'conv_b': _jnp.float32, 'w_down': _jnp.float32, 'ple_norm': _jnp.float32, 'w_ple_gate': _jnp.float32, 'w_ple': _jnp.float32}
MOMENT_SCALE = {'attn_pre_norm': 9.284142e-01, 'attn_post_norm': 3.154242e+01, 'w_in': 4.846365e-01, 'b_gate': 1.909364e-01, 'sinks': 1.575357e-01, 'q_a_norm': 6.294179e-01, 'w_uq': 3.676909e-01, 'kv_a_norm': 1.583362e+00, 'w_ukv': 4.884750e-01, 'w_branch_a': 5.563652e-01, 'w_branch_b': 4.128864e-01, 'w_out': 6.734481e-01, 'mlp_pre_norm': 5.882919e-01, 'mlp_post_norm': 3.206139e+01, 'w_up': 2.550109e-01, 'conv_w': 2.750466e-01, 'conv_b': 7.100417e-01, 'w_down': 4.901963e-01, 'ple_norm': 9.458268e-01, 'w_ple_gate': 1.777517e-01, 'w_ple': 4.195233e-01}


def _to_microbatches(a, axis):
    t = _jnp.moveaxis(a, axis, 0)
    t = t.reshape((N_MICROBATCH, t.shape[0] // N_MICROBATCH) + t.shape[1:])
    return _jnp.moveaxis(t, 1, axis + 1)


def setup_inputs(seed: int = 0) -> dict:
    inp = _fwd_setup_inputs(seed)
    key = _jax.random.fold_in(_jax.random.key(seed), 7919)
    shape, _ = _output_shape()
    out = dict(inp)
    out["loss_target"] = _jax.random.normal(_jax.random.fold_in(key, 0), shape, _jnp.float32)
    for i, name in enumerate(TWIN_WEIGHTS):
        w = inp[name].astype(_jnp.float32)
        if MOMENT_SCALE is None:
            s = _jnp.sqrt(_jnp.mean(_jnp.square(w)) + 1e-30)
        else:
            s = MOMENT_SCALE[name]
        km, kv = _jax.random.split(_jax.random.fold_in(key, i + 1))
        out[name] = w
        out["m_" + name] = s * _jax.random.normal(km, w.shape, _jnp.float32)
        out["v_" + name] = (s * s) * _jax.random.uniform(kv, w.shape, _jnp.float32, 0.5, 1.5)
    if N_MICROBATCH > 1:
        for name, axis in PER_EXAMPLE_BATCH_AXIS.items():
            out[name] = _to_microbatches(out[name], axis)
    return {'x': out['x'], 'p': out['p'], 'positions': out['positions'], 'attn_pre_norm': out['attn_pre_norm'], 'attn_post_norm': out['attn_post_norm'], 'w_in': out['w_in'], 'b_gate': out['b_gate'], 'sinks': out['sinks'], 'q_a_norm': out['q_a_norm'], 'w_uq': out['w_uq'], 'kv_a_norm': out['kv_a_norm'], 'w_ukv': out['w_ukv'], 'w_branch_a': out['w_branch_a'], 'w_branch_b': out['w_branch_b'], 'w_out': out['w_out'], 'mlp_pre_norm': out['mlp_pre_norm'], 'mlp_post_norm': out['mlp_post_norm'], 'w_up': out['w_up'], 'conv_w': out['conv_w'], 'conv_b': out['conv_b'], 'w_down': out['w_down'], 'ple_norm': out['ple_norm'], 'w_ple_gate': out['w_ple_gate'], 'w_ple': out['w_ple'], 'loss_target': out['loss_target'], 'm_attn_pre_norm': out['m_attn_pre_norm'], 'm_attn_post_norm': out['m_attn_post_norm'], 'm_w_in': out['m_w_in'], 'm_b_gate': out['m_b_gate'], 'm_sinks': out['m_sinks'], 'm_q_a_norm': out['m_q_a_norm'], 'm_w_uq': out['m_w_uq'], 'm_kv_a_norm': out['m_kv_a_norm'], 'm_w_ukv': out['m_w_ukv'], 'm_w_branch_a': out['m_w_branch_a'], 'm_w_branch_b': out['m_w_branch_b'], 'm_w_out': out['m_w_out'], 'm_mlp_pre_norm': out['m_mlp_pre_norm'], 'm_mlp_post_norm': out['m_mlp_post_norm'], 'm_w_up': out['m_w_up'], 'm_conv_w': out['m_conv_w'], 'm_conv_b': out['m_conv_b'], 'm_w_down': out['m_w_down'], 'm_ple_norm': out['m_ple_norm'], 'm_w_ple_gate': out['m_w_ple_gate'], 'm_w_ple': out['m_w_ple'], 'v_attn_pre_norm': out['v_attn_pre_norm'], 'v_attn_post_norm': out['v_attn_post_norm'], 'v_w_in': out['v_w_in'], 'v_b_gate': out['v_b_gate'], 'v_sinks': out['v_sinks'], 'v_q_a_norm': out['v_q_a_norm'], 'v_w_uq': out['v_w_uq'], 'v_kv_a_norm': out['v_kv_a_norm'], 'v_w_ukv': out['v_w_ukv'], 'v_w_branch_a': out['v_w_branch_a'], 'v_w_branch_b': out['v_w_branch_b'], 'v_w_out': out['v_w_out'], 'v_mlp_pre_norm': out['v_mlp_pre_norm'], 'v_mlp_post_norm': out['v_mlp_post_norm'], 'v_w_up': out['v_w_up'], 'v_conv_w': out['v_conv_w'], 'v_conv_b': out['v_conv_b'], 'v_w_down': out['v_w_down'], 'v_ple_norm': out['v_ple_norm'], 'v_w_ple_gate': out['v_w_ple_gate'], 'v_w_ple': out['v_w_ple']}


def _loss(weights, diff, rest, loss_target):
    with _jax.named_scope("forward"):
        args = {**rest, TWIN_DIFF_INPUT: diff, **{k: w.astype(_WEIGHT_DTYPES[k]) for k, w in weights.items()}}
        y = _forward(args)
    with _jax.named_scope("loss_head"):
        err = _jnp.square(y.astype(_jnp.float32) - loss_target)
        return 0.5 * _jnp.sum(_jnp.mean(err, axis=-1)) if err.ndim else 0.5 * err


def _adamw(w, g, m, v):
    m = ADAM_B1 * m + (1.0 - ADAM_B1) * g
    v = ADAM_B2 * v + (1.0 - ADAM_B2) * _jnp.square(g)
    m_hat = m / (1.0 - ADAM_B1 ** ADAM_STEP)
    v_hat = v / (1.0 - ADAM_B2 ** ADAM_STEP)
    delta = -ADAM_LR * (m_hat / (_jnp.sqrt(v_hat) + ADAM_EPS) + ADAM_WD * w)
    return delta, m, v


def reference(x, p, positions, attn_pre_norm, attn_post_norm, w_in, b_gate, sinks, q_a_norm, w_uq, kv_a_norm, w_ukv, w_branch_a, w_branch_b, w_out, mlp_pre_norm, mlp_post_norm, w_up, conv_w, conv_b, w_down, ple_norm, w_ple_gate, w_ple, loss_target, m_attn_pre_norm, m_attn_post_norm, m_w_in, m_b_gate, m_sinks, m_q_a_norm, m_w_uq, m_kv_a_norm, m_w_ukv, m_w_branch_a, m_w_branch_b, m_w_out, m_mlp_pre_norm, m_mlp_post_norm, m_w_up, m_conv_w, m_conv_b, m_w_down, m_ple_norm, m_w_ple_gate, m_w_ple, v_attn_pre_norm, v_attn_post_norm, v_w_in, v_b_gate, v_sinks, v_q_a_norm, v_w_uq, v_kv_a_norm, v_w_ukv, v_w_branch_a, v_w_branch_b, v_w_out, v_mlp_pre_norm, v_mlp_post_norm, v_w_up, v_conv_w, v_conv_b, v_w_down, v_ple_norm, v_w_ple_gate, v_w_ple):
    given = dict(x=x, p=p, positions=positions, attn_pre_norm=attn_pre_norm, attn_post_norm=attn_post_norm, w_in=w_in, b_gate=b_gate, sinks=sinks, q_a_norm=q_a_norm, w_uq=w_uq, kv_a_norm=kv_a_norm, w_ukv=w_ukv, w_branch_a=w_branch_a, w_branch_b=w_branch_b, w_out=w_out, mlp_pre_norm=mlp_pre_norm, mlp_post_norm=mlp_post_norm, w_up=w_up, conv_w=conv_w, conv_b=conv_b, w_down=w_down, ple_norm=ple_norm, w_ple_gate=w_ple_gate, w_ple=w_ple, loss_target=loss_target, m_attn_pre_norm=m_attn_pre_norm, m_attn_post_norm=m_attn_post_norm, m_w_in=m_w_in, m_b_gate=m_b_gate, m_sinks=m_sinks, m_q_a_norm=m_q_a_norm, m_w_uq=m_w_uq, m_kv_a_norm=m_kv_a_norm, m_w_ukv=m_w_ukv, m_w_branch_a=m_w_branch_a, m_w_branch_b=m_w_branch_b, m_w_out=m_w_out, m_mlp_pre_norm=m_mlp_pre_norm, m_mlp_post_norm=m_mlp_post_norm, m_w_up=m_w_up, m_conv_w=m_conv_w, m_conv_b=m_conv_b, m_w_down=m_w_down, m_ple_norm=m_ple_norm, m_w_ple_gate=m_w_ple_gate, m_w_ple=m_w_ple, v_attn_pre_norm=v_attn_pre_norm, v_attn_post_norm=v_attn_post_norm, v_w_in=v_w_in, v_b_gate=v_b_gate, v_sinks=v_sinks, v_q_a_norm=v_q_a_norm, v_w_uq=v_w_uq, v_kv_a_norm=v_kv_a_norm, v_w_ukv=v_w_ukv, v_w_branch_a=v_w_branch_a, v_w_branch_b=v_w_branch_b, v_w_out=v_w_out, v_mlp_pre_norm=v_mlp_pre_norm, v_mlp_post_norm=v_mlp_post_norm, v_w_up=v_w_up, v_conv_w=v_conv_w, v_conv_b=v_conv_b, v_w_down=v_w_down, v_ple_norm=v_ple_norm, v_w_ple_gate=v_w_ple_gate, v_w_ple=v_w_ple)
    weights = {n: given[n] for n in TWIN_WEIGHTS}
    shared = {n: given[n] for n in SHARED_INPUTS}
    per_example = {n: given[n] for n in ['x', 'p', 'positions']}
    grad_fn = _jax.value_and_grad(_loss, argnums=(0, 1))

    def one_microbatch(ex, loss_target):
        ex = dict(ex)
        diff = ex.pop(TWIN_DIFF_INPUT)
        return grad_fn(weights, diff, {**shared, **ex}, loss_target)

    if N_MICROBATCH == 1:
        loss, (grad_w, grad_x) = one_microbatch(per_example, given["loss_target"])
    else:
        def body(carry, xs):
            loss_sum, grad_sum = carry
            l_k, (gw_k, gx_k) = one_microbatch(xs[0], xs[1])
            with _jax.named_scope("update"):
                return (loss_sum + l_k, _jax.tree.map(_jnp.add, grad_sum, gw_k)), gx_k

        init = (_jnp.zeros((), _jnp.float32), _jax.tree.map(_jnp.zeros_like, weights))
        (loss, grad_w), grad_x = _jax.lax.scan(body, init, (per_example, given["loss_target"]))
    with _jax.named_scope("update"):
        delta_w, new_m, new_v = {}, {}, {}
        for n in TWIN_WEIGHTS:
            delta_w[n], new_m[n], new_v[n] = _adamw(weights[n], grad_w[n], given["m_" + n], given["v_" + n])
    return (loss, grad_x, *[grad_w[n] for n in TWIN_WEIGHTS], *[delta_w[n] for n in TWIN_WEIGHTS],
            *[new_m[n] for n in TWIN_WEIGHTS], *[new_v[n] for n in TWIN_WEIGHTS])
```

```python
import functools

import numpy as np
import jax
import jax.numpy as jnp
from jax import lax
from jax.experimental import pallas as pl
from jax.experimental.pallas import tpu as pltpu

F32 = jnp.float32
BF16 = jnp.bfloat16
MESH_ID = pl.DeviceIdType.MESH
AXES = ("x", "y", "c")
N_DEV = 8

D_MODEL = 1024
RMS_EPS = 1e-6
ROPE_THETA = 10000.0
SWA_BLOCK = 128
A_HEADS, A_KV_HEADS, A_HEAD_DIM = 8, 2, 64
A_GROUP = A_HEADS // A_KV_HEADS
B_HEADS, Q_LORA, KV_LORA, NOPE_DIM, ROPE_DIM, V_DIM = 8, 256, 128, 64, 32, 64
D_FF = 2816
CONV_W = 3
HEAD_PAD = 128

ADAM_LR, ADAM_B1, ADAM_B2, ADAM_EPS, ADAM_WD, ADAM_STEP = 0.001, 0.9, 0.999, 1e-08, 0.01, 10

VMEM_LIMIT = 48 * 1024 * 1024
MM_TM, MM_TN, MM_TK = 512, 1024, 512
FLASH_T = 512
CONV_TS, CONV_TC = 512, 256
PACK_COLS = 1024


def _params(*sem):
    return pltpu.CompilerParams(dimension_semantics=sem, vmem_limit_bytes=VMEM_LIMIT)


def _pick(dim, cap, mult):
    best = None
    for t in range(mult, min(dim, cap) + 1, mult):
        if dim % t == 0:
            best = t
    return dim if best is None else best


def _matmul(a, b, *, ta=False, tb=False, out_dtype=F32, name):
    (kdim, m) = a.shape if ta else a.shape[::-1]
    n = b.shape[0] if tb else b.shape[1]
    tm = _pick(m, MM_TM, 128)
    tn = _pick(n, MM_TN, 128)
    tk = _pick(kdim, MM_TK, 128)
    nk = kdim // tk
    a_spec = (pl.BlockSpec((tk, tm), lambda i, j, k: (k, i)) if ta
              else pl.BlockSpec((tm, tk), lambda i, j, k: (i, k)))
    b_spec = (pl.BlockSpec((tn, tk), lambda i, j, k: (j, k)) if tb
              else pl.BlockSpec((tk, tn), lambda i, j, k: (k, j)))
    dims = (((0 if ta else 1,), (1 if tb else 0,)), ((), ()))

    def body(a_ref, b_ref, o_ref, acc_ref):
        k = pl.program_id(2)

        @pl.when(k == 0)
        def _():
            acc_ref[...] = jnp.zeros_like(acc_ref)

        acc_ref[...] += lax.dot_general(a_ref[...].astype(BF16), b_ref[...].astype(BF16), dims,
                                        preferred_element_type=F32)

        @pl.when(k == nk - 1)
        def _():
            o_ref[...] = acc_ref[...].astype(o_ref.dtype)

    return pl.pallas_call(
        body, name=name, grid=(m // tm, n // tn, nk),
        in_specs=[a_spec, b_spec],
        out_specs=pl.BlockSpec((tm, tn), lambda i, j, k: (i, j)),
        out_shape=jax.ShapeDtypeStruct((m, n), out_dtype),
        scratch_shapes=[pltpu.VMEM((tm, tn), F32)],
        compiler_params=_params("parallel", "parallel", "arbitrary"),
    )(a, b)


@functools.partial(jax.custom_vjp, nondiff_argnums=(2, 3))
def mm(a, w, name, need_da):
    return _matmul(a, w, name=name + "_fwd")


def _mm_fwd(a, w, name, need_da):
    return _matmul(a, w, name=name + "_fwd"), (a, w)


def _mm_bwd(name, need_da, res, ct):
    a, w = res
    da = _matmul(ct, w, tb=True, out_dtype=a.dtype, name=name + "_da") if need_da else jnp.zeros_like(a)
    dw = _matmul(a, ct, ta=True, out_dtype=w.dtype, name=name + "_dw")
    return da, dw


mm.defvjp(_mm_fwd, _mm_bwd)


def _pairs(bounds):
    return list(zip(bounds[:-1], bounds[1:]))


def _split(v, bounds):
    return [v[:, a:b] for a, b in _pairs(bounds)]


def stage(name, f, tiled, params=(), consts=(), splits=None, ts=256):
    tiled, params, consts = tuple(tiled), tuple(params), tuple(consts)
    n_t, n_p, n_c = len(tiled), len(params), len(consts)
    s = tiled[0].shape[0]
    ts = min(ts, s)
    grid = (s // ts,)
    if splits is None:
        splits = [None] * n_t
    in_bounds = [(0, t.shape[1]) if b is None else tuple(b) for t, b in zip(tiled, splits)]

    def tile_aval(arr):
        return jax.ShapeDtypeStruct((ts, arr.shape[1]), arr.dtype)

    slab_avals = [[jax.ShapeDtypeStruct((ts, e - a), t.dtype) for a, e in _pairs(b)]
                  for t, b in zip(tiled, in_bounds)]
    out_avals = jax.eval_shape(f, slab_avals, list(params), [tile_aval(c) for c in consts])
    out_bounds = [tuple(np.cumsum([0] + [o.shape[1] for o in slabs]).tolist()) for slabs in out_avals]
    out_shapes = [jax.ShapeDtypeStruct((s, b[-1]), F32) for b in out_bounds]

    def row_spec(width):
        return pl.BlockSpec((ts, width), lambda i: (i, 0))

    def par_spec(arr):
        return pl.BlockSpec(arr.shape, lambda i: (0, 0))

    in_specs = ([row_spec(t.shape[1]) for t in tiled] + [par_spec(p) for p in params]
                + [row_spec(c.shape[1]) for c in consts])

    def load(refs):
        t = [_split(r[...], b) for r, b in zip(refs[:n_t], in_bounds)]
        p = [r[...] for r in refs[n_t:n_t + n_p]]
        c = [r[...] for r in refs[n_t + n_p:n_t + n_p + n_c]]
        return t, p, c

    def store(refs, values, bounds):
        for ref, slabs, b in zip(refs, values, bounds):
            for v, (a, e) in zip(slabs, _pairs(b)):
                ref[:, a:e] = v.astype(ref.dtype)

    def run_fwd(tiled, params, consts):
        def body(*refs):
            t, p, c = load(refs)
            store(refs[n_t + n_p + n_c:], f(t, p, c), out_bounds)

        return pl.pallas_call(
            body, name=name + "_fwd", grid=grid, in_specs=in_specs,
            out_specs=[row_spec(b[-1]) for b in out_bounds], out_shape=out_shapes,
            compiler_params=_params("parallel"),
        )(*tiled, *params, *consts)

    def run_bwd(tiled, params, consts, cts):
        n_in = n_t + n_p + n_c
        n_o = len(out_bounds)

        def body(*refs):
            t, p, c = load(refs)
            g = [_split(r[...], b) for r, b in zip(refs[n_in:n_in + n_o], out_bounds)]
            _, pull = jax.vjp(lambda t_, p_: f(t_, p_, c), t, p)
            dt, dp = pull(g)
            store(refs[n_in + n_o:n_in + n_o + n_t], dt, in_bounds)
            first = pl.program_id(0) == 0
            for ref, d in zip(refs[n_in + n_o + n_t:], dp):
                @pl.when(first)
                def _(ref=ref):
                    ref[...] = jnp.zeros_like(ref)

                ref[...] += d

        res = pl.pallas_call(
            body, name=name + "_bwd", grid=grid,
            in_specs=in_specs + [row_spec(b[-1]) for b in out_bounds],
            out_specs=[row_spec(t.shape[1]) for t in tiled] + [par_spec(p) for p in params],
            out_shape=[jax.ShapeDtypeStruct(t.shape, F32) for t in tiled]
                      + [jax.ShapeDtypeStruct(p.shape, F32) for p in params],
            compiler_params=_params("arbitrary"),
        )(*tiled, *params, *consts, *cts)
        return tuple(res[:n_t]), tuple(res[n_t:])

    @jax.custom_vjp
    def op(tiled, params, consts):
        return tuple(run_fwd(tiled, params, consts))

    def op_fwd(tiled, params, consts):
        return op(tiled, params, consts), (tiled, params, consts)

    def op_bwd(res, cts):
        tiled, params, consts = res
        dt, dp = run_bwd(tiled, params, consts, cts)
        return dt, dp, tuple(jnp.zeros_like(c) for c in consts)

    op.defvjp(op_fwd, op_bwd)
    return op(tiled, params, consts)


def _rms(t, g):
    return t * lax.rsqrt(jnp.mean(t * t, axis=-1, keepdims=True) + RMS_EPS) * g


def _rope(t1, t2, c, s):
    return t1 * c - t2 * s, t2 * c + t1 * s


def _f_prenorm(t, p, c):
    return [[_rms(t[0][0], p[0])]]


def _f_prep(t, p, c):
    qa1, qa2, ka1, ka2, va, cq, ckv, kr1, kr2 = t[0]
    ca, sa, cb, sb = c
    ca2, sa2 = jnp.concatenate([ca, ca], axis=1), jnp.concatenate([sa, sa], axis=1)
    return [list(_rope(qa1, qa2, ca2, sa2)), list(_rope(ka1, ka2, ca, sa)), [va],
            [_rms(cq, p[0])], [_rms(ckv, p[1])], list(_rope(kr1, kr2, cb, sb))]


def _f_qrope(t, p, c):
    nope, pe1, pe2 = t[0]
    return [[nope] + list(_rope(pe1, pe2, c[0], c[1]))]


def _f_gate(t, p, c):
    (ga, gb), (pa,), (pb,) = t
    ba, bb = p
    return [[jax.nn.sigmoid(ga + ba) * pa + jax.nn.sigmoid(gb + bb) * pb]]


def _f_post(t, p, c):
    x1 = t[0][0] + _rms(t[1][0], p[0])
    return [[x1], [_rms(x1, p[1])]]


def _f_glu(t, p, c):
    return [[jax.nn.gelu(t[0][0], approximate=True) * t[1][0]]]


def _f_out(t, p, c):
    y = t[0][0] + jax.nn.sigmoid(t[1][0]) * t[2][0]
    err = y - c[0]
    return [[0.5 * jnp.mean(err * err, axis=-1, keepdims=True)]]


def _shift_down(cur, prev, has_prev):
    rows = cur.shape[0]
    row = lax.broadcasted_iota(jnp.int32, cur.shape, 0)
    m1 = prev[7:8, :] * has_prev
    m2 = prev[6:7, :] * has_prev
    u1 = jnp.where(row >= 1, pltpu.roll(cur, 1, 0), m1)
    u2 = jnp.where(row >= 2, pltpu.roll(cur, 2, 0), jnp.where(row == 1, m1, m2))
    return u1, u2


def _shift_up(cur, nxt, has_next):
    rows = cur.shape[0]
    row = lax.broadcasted_iota(jnp.int32, cur.shape, 0)
    n0 = nxt[0:1, :] * has_next
    n1 = nxt[1:2, :] * has_next
    d1 = jnp.where(row < rows - 1, pltpu.roll(cur, rows - 1, 0), n0)
    d2 = jnp.where(row < rows - 2, pltpu.roll(cur, rows - 2, 0), jnp.where(row == rows - 2, n0, n1))
    return d1, d2


def _conv_tiles(s, ch):
    ts = min(CONV_TS, s)
    tc = _pick(ch, CONV_TC, 128)
    return ts, tc, s // ts, ch // tc


def _conv_fwd_call(up, w, b, name):
    s, ch = up.shape
    ts, tc, nt, nc = _conv_tiles(s, ch)
    hb = ts // 8

    def body(cur_ref, prev_ref, w_ref, b_ref, o_ref):
        cur = cur_ref[...]
        u1, u2 = _shift_down(cur, prev_ref[...], (pl.program_id(1) > 0).astype(F32))
        o_ref[...] = w_ref[2:3, :] * cur + w_ref[1:2, :] * u1 + w_ref[0:1, :] * u2 + b_ref[...]

    return pl.pallas_call(
        body, name=name + "_fwd", grid=(nc, nt),
        in_specs=[pl.BlockSpec((ts, tc), lambda c, i: (i, c)),
                  pl.BlockSpec((8, tc), lambda c, i: (jnp.maximum(i * hb - 1, 0), c)),
                  pl.BlockSpec((CONV_W, tc), lambda c, i: (0, c)),
                  pl.BlockSpec((1, tc), lambda c, i: (0, c))],
        out_specs=pl.BlockSpec((ts, tc), lambda c, i: (i, c)),
        out_shape=jax.ShapeDtypeStruct((s, ch), F32),
        compiler_params=_params("parallel", "parallel"),
    )(up, up, w, b)


def _conv_bwd_call(up, w, du, name):
    s, ch = up.shape
    ts, tc, nt, nc = _conv_tiles(s, ch)
    hb = ts // 8

    def body(cur_ref, prev_ref, w_ref, du_ref, nxt_ref, dup_ref, dw_ref, db_ref):
        i = pl.program_id(1)
        cur, du = cur_ref[...], du_ref[...]
        u1, u2 = _shift_down(cur, prev_ref[...], (i > 0).astype(F32))
        d1, d2 = _shift_up(du, nxt_ref[...], (i < nt - 1).astype(F32))
        dup_ref[...] = w_ref[2:3, :] * du + w_ref[1:2, :] * d1 + w_ref[0:1, :] * d2

        @pl.when(i == 0)
        def _():
            dw_ref[...] = jnp.zeros_like(dw_ref)
            db_ref[...] = jnp.zeros_like(db_ref)

        dw_ref[0:1, :] += jnp.sum(du * u2, axis=0, keepdims=True)
        dw_ref[1:2, :] += jnp.sum(du * u1, axis=0, keepdims=True)
        dw_ref[2:3, :] += jnp.sum(du * cur, axis=0, keepdims=True)
        db_ref[...] += jnp.sum(du, axis=0, keepdims=True)

    return pl.pallas_call(
        body, name=name + "_bwd", grid=(nc, nt),
        in_specs=[pl.BlockSpec((ts, tc), lambda c, i: (i, c)),
                  pl.BlockSpec((8, tc), lambda c, i: (jnp.maximum(i * hb - 1, 0), c)),
                  pl.BlockSpec((CONV_W, tc), lambda c, i: (0, c)),
                  pl.BlockSpec((ts, tc), lambda c, i: (i, c)),
                  pl.BlockSpec((8, tc), lambda c, i: (jnp.minimum((i + 1) * hb, s // 8 - 1), c))],
        out_specs=[pl.BlockSpec((ts, tc), lambda c, i: (i, c)),
                   pl.BlockSpec((CONV_W, tc), lambda c, i: (0, c)),
                   pl.BlockSpec((1, tc), lambda c, i: (0, c))],
        out_shape=[jax.ShapeDtypeStruct((s, ch), F32), jax.ShapeDtypeStruct((CONV_W, ch), F32),
                   jax.ShapeDtypeStruct((1, ch), F32)],
        compiler_params=_params("parallel", "arbitrary"),
    )(up, up, w, du, du)


@functools.partial(jax.custom_vjp, nondiff_argnums=(3,))
def dwconv(up, w, b, name):
    return _conv_fwd_call(up, w, b, name)


def _dwconv_fwd(up, w, b, name):
    return _conv_fwd_call(up, w, b, name), (up, w)


def _dwconv_bwd(name, res, ct):
    up, w = res
    return tuple(_conv_bwd_call(up, w, ct, name))


dwconv.defvjp(_dwconv_fwd, _dwconv_bwd)


def _swa_probs(q, kp, kc, sink, prev_off):
    scale = A_HEAD_DIM ** -0.5
    nt = (((1,), (1,)), ((), ()))
    sp = lax.dot_general(q, kp, nt, preferred_element_type=F32) * scale
    sc = lax.dot_general(q, kc, nt, preferred_element_type=F32) * scale
    qi = lax.broadcasted_iota(jnp.int32, sp.shape, 0)
    kj = lax.broadcasted_iota(jnp.int32, sp.shape, 1)
    sp = jnp.where(kj > qi + prev_off, sp, -jnp.inf)
    sc = jnp.where(kj <= qi, sc, -jnp.inf)
    m = jnp.maximum(jnp.maximum(jnp.max(sp, axis=-1, keepdims=True), jnp.max(sc, axis=-1, keepdims=True)), sink)
    ep, ec, es = jnp.exp(sp - m), jnp.exp(sc - m), jnp.exp(sink - m)
    den = jnp.sum(ep, axis=-1, keepdims=True) + jnp.sum(ec, axis=-1, keepdims=True) + es
    return ep / den, ec / den, es / den


def _swa_specs(s):
    blk = SWA_BLOCK
    q_spec = pl.BlockSpec((None, A_GROUP, blk, A_HEAD_DIM), lambda g, n: (g, 0, n, 0))
    prev_spec = pl.BlockSpec((None, blk, A_HEAD_DIM), lambda g, n: (g, jnp.maximum(n - 1, 0), 0))
    cur_spec = pl.BlockSpec((None, blk, A_HEAD_DIM), lambda g, n: (g, n, 0))
    sink_spec = pl.BlockSpec(memory_space=pltpu.SMEM)
    return q_spec, prev_spec, cur_spec, sink_spec


def _swa_fwd_call(q, k, v, sinks):
    s = q.shape[2]
    q_spec, prev_spec, cur_spec, sink_spec = _swa_specs(s)

    def body(q_ref, kp_ref, kc_ref, vp_ref, vc_ref, sink_ref, o_ref):
        g, n = pl.program_id(0), pl.program_id(1)
        prev_off = jnp.where(n > 0, 0, SWA_BLOCK)
        kp, kc = kp_ref[...].astype(BF16), kc_ref[...].astype(BF16)
        vp, vc = vp_ref[...].astype(BF16), vc_ref[...].astype(BF16)
        for h in range(A_GROUP):
            pp, pc, _ = _swa_probs(q_ref[h].astype(BF16), kp, kc, sink_ref[g * A_GROUP + h], prev_off)
            o_ref[h] = (jnp.dot(pp.astype(BF16), vp, preferred_element_type=F32)
                        + jnp.dot(pc.astype(BF16), vc, preferred_element_type=F32))

    return pl.pallas_call(
        body, name="swa_fwd", grid=(A_KV_HEADS, s // SWA_BLOCK),
        in_specs=[q_spec, prev_spec, cur_spec, prev_spec, cur_spec, sink_spec],
        out_specs=q_spec, out_shape=jax.ShapeDtypeStruct(q.shape, F32),
        compiler_params=_params("parallel", "parallel"),
    )(q, k, k, v, v, sinks)


def _swa_bwd_call(q, k, v, sinks, do):
    s = q.shape[2]
    q_spec, prev_spec, cur_spec, sink_spec = _swa_specs(s)
    scale = A_HEAD_DIM ** -0.5
    tn = (((0,), (0,)), ((), ()))
    nt = (((1,), (1,)), ((), ()))
    dsink_spec = pl.BlockSpec((None, A_GROUP, SWA_BLOCK, 1), lambda g, n: (g, 0, 0, 0))

    def body(q_ref, kp_ref, kc_ref, vp_ref, vc_ref, sink_ref, do_ref,
             dq_ref, dkp_ref, dkc_ref, dvp_ref, dvc_ref, dsink_ref):
        g, n = pl.program_id(0), pl.program_id(1)
        prev_off = jnp.where(n > 0, 0, SWA_BLOCK)
        kp, kc = kp_ref[...].astype(BF16), kc_ref[...].astype(BF16)
        vp, vc = vp_ref[...].astype(BF16), vc_ref[...].astype(BF16)

        @pl.when(n == 0)
        def _():
            dsink_ref[...] = jnp.zeros_like(dsink_ref)

        dkp = jnp.zeros(kp.shape, F32)
        dkc = jnp.zeros(kp.shape, F32)
        dvp = jnp.zeros(kp.shape, F32)
        dvc = jnp.zeros(kp.shape, F32)
        for h in range(A_GROUP):
            qh = q_ref[h].astype(BF16)
            pp, pc, ps = _swa_probs(qh, kp, kc, sink_ref[g * A_GROUP + h], prev_off)
            ppb, pcb = pp.astype(BF16), pc.astype(BF16)
            out = jnp.dot(ppb, vp, preferred_element_type=F32) + jnp.dot(pcb, vc, preferred_element_type=F32)
            doh = do_ref[h]
            dob = doh.astype(BF16)
            delta = jnp.sum(doh * out, axis=-1, keepdims=True)
            dsp = (pp * (lax.dot_general(dob, vp, nt, preferred_element_type=F32) - delta)).astype(BF16)
            dsc = (pc * (lax.dot_general(dob, vc, nt, preferred_element_type=F32) - delta)).astype(BF16)
            dsink_ref[h] += -ps * delta
            dq_ref[h] = (jnp.dot(dsp, kp, preferred_element_type=F32)
                         + jnp.dot(dsc, kc, preferred_element_type=F32)) * scale
            dkp += lax.dot_general(dsp, qh, tn, preferred_element_type=F32)
            dkc += lax.dot_general(dsc, qh, tn, preferred_element_type=F32)
            dvp += lax.dot_general(ppb, dob, tn, preferred_element_type=F32)
            dvc += lax.dot_general(pcb, dob, tn, preferred_element_type=F32)
        dkp_ref[...] = dkp * scale
        dkc_ref[...] = dkc * scale
        dvp_ref[...] = dvp
        dvc_ref[...] = dvc

    kv_shape = jax.ShapeDtypeStruct(k.shape, F32)
    return pl.pallas_call(
        body, name="swa_bwd", grid=(A_KV_HEADS, s // SWA_BLOCK),
        in_specs=[q_spec, prev_spec, cur_spec, prev_spec, cur_spec, sink_spec, q_spec],
        out_specs=[q_spec, cur_spec, cur_spec, cur_spec, cur_spec, dsink_spec],
        out_shape=[jax.ShapeDtypeStruct(q.shape, F32), kv_shape, kv_shape, kv_shape, kv_shape,
                   jax.ShapeDtypeStruct((A_KV_HEADS, A_GROUP, SWA_BLOCK, 1), F32)],
        compiler_params=_params("parallel", "arbitrary"),
    )(q, k, k, v, v, sinks, do)


@jax.custom_vjp
def swa(q, k, v, sinks):
    return _swa_fwd_call(q, k, v, sinks)


def _swa_fwd(q, k, v, sinks):
    return _swa_fwd_call(q, k, v, sinks), (q, k, v, sinks)


def _swa_bwd(res, do):
    q, k, v, sinks = res
    dq, dkp, dkc, dvp, dvc, dsink = _swa_bwd_call(q, k, v, sinks, do)

    def fold(prev_part, cur_part):
        shifted = jnp.concatenate([prev_part[:, SWA_BLOCK:], jnp.zeros_like(prev_part[:, :SWA_BLOCK])], axis=1)
        return cur_part + shifted

    return dq, fold(dkp, dkc), fold(dvp, dvc), jnp.sum(dsink, axis=(2, 3)).reshape(-1)


swa.defvjp(_swa_fwd, _swa_bwd)


MLA_SCALE = (NOPE_DIM + ROPE_DIM) ** -0.5


def _causal_mask(i, j, t):
    row = lax.broadcasted_iota(jnp.int32, (t, t), 0) + i * t
    col = lax.broadcasted_iota(jnp.int32, (t, t), 1) + j * t
    return row >= col


def _flash_fwd_call(q, k, v):
    h, s, d = q.shape
    t = min(FLASH_T, s)
    nb = s // t
    nt = (((1,), (1,)), ((), ()))

    def body(q_ref, k_ref, v_ref, o_ref, lse_ref, m_ref, l_ref, acc_ref):
        i, j = pl.program_id(1), pl.program_id(2)

        @pl.when(j == 0)
        def _():
            m_ref[...] = jnp.full_like(m_ref, -jnp.inf)
            l_ref[...] = jnp.zeros_like(l_ref)
            acc_ref[...] = jnp.zeros_like(acc_ref)

        @pl.when(j <= i)
        def _():
            sc = lax.dot_general(q_ref[...].astype(BF16), k_ref[...].astype(BF16), nt,
                                 preferred_element_type=F32) * MLA_SCALE
            sc = jnp.where(_causal_mask(i, j, t), sc, -jnp.inf)
            m_new = jnp.maximum(m_ref[...], jnp.max(sc, axis=-1, keepdims=True))
            alpha = jnp.exp(m_ref[...] - m_new)
            p = jnp.exp(sc - m_new)
            l_ref[...] = alpha * l_ref[...] + jnp.sum(p, axis=-1, keepdims=True)
            acc_ref[...] = alpha * acc_ref[...] + jnp.dot(p.astype(BF16), v_ref[...].astype(BF16),
                                                          preferred_element_type=F32)
            m_ref[...] = m_new

        @pl.when(j == nb - 1)
        def _():
            o_ref[...] = acc_ref[...] / l_ref[...]
            lse_ref[...] = m_ref[...] + jnp.log(l_ref[...])

    q_spec = pl.BlockSpec((None, t, d), lambda hh, i, j: (hh, i, 0))
    kv_spec = pl.BlockSpec((None, t, d), lambda hh, i, j: (hh, jnp.minimum(j, i), 0))
    return pl.pallas_call(
        body, name="mla_fwd", grid=(h, nb, nb),
        in_specs=[q_spec, kv_spec, kv_spec],
        out_specs=[q_spec, pl.BlockSpec((None, t, 1), lambda hh, i, j: (hh, i, 0))],
        out_shape=[jax.ShapeDtypeStruct((h, s, d), F32), jax.ShapeDtypeStruct((h, s, 1), F32)],
        scratch_shapes=[pltpu.VMEM((t, 1), F32), pltpu.VMEM((t, 1), F32), pltpu.VMEM((t, d), F32)],
        compiler_params=_params("parallel", "parallel", "arbitrary"),
    )(q, k, v)


def _flash_bwd_call(q, k, v, o, lse, do):
    h, s, d = q.shape
    t = min(FLASH_T, s)
    nb = s // t
    nt = (((1,), (1,)), ((), ()))
    tn = (((0,), (0,)), ((), ()))

    def body(q_ref, k_ref, v_ref, o_ref, lse_ref, do_ref, dq_ref, dk_ref, dv_ref, dk_acc, dv_acc):
        j, i = pl.program_id(1), pl.program_id(2)

        @pl.when((j == 0) & (i == 0))
        def _():
            dq_ref[...] = jnp.zeros_like(dq_ref)

        @pl.when(i == j)
        def _():
            dk_acc[...] = jnp.zeros_like(dk_acc)
            dv_acc[...] = jnp.zeros_like(dv_acc)

        @pl.when(i >= j)
        def _():
            qb, kb, vb = q_ref[...].astype(BF16), k_ref[...].astype(BF16), v_ref[...].astype(BF16)
            dof = do_ref[...]
            dob = dof.astype(BF16)
            sc = lax.dot_general(qb, kb, nt, preferred_element_type=F32) * MLA_SCALE
            p = jnp.where(_causal_mask(i, j, t), jnp.exp(sc - lse_ref[...]), 0.0)
            delta = jnp.sum(dof * o_ref[...], axis=-1, keepdims=True)
            ds = (p * (lax.dot_general(dob, vb, nt, preferred_element_type=F32) - delta)).astype(BF16)
            dv_acc[...] += lax.dot_general(p.astype(BF16), dob, tn, preferred_element_type=F32)
            dk_acc[...] += lax.dot_general(ds, qb, tn, preferred_element_type=F32)
            rows = pl.ds(pl.multiple_of(i * t, t), t)
            dq_ref[rows, :] += jnp.dot(ds, kb, preferred_element_type=F32) * MLA_SCALE

        @pl.when(i == nb - 1)
        def _():
            dk_ref[...] = dk_acc[...] * MLA_SCALE
            dv_ref[...] = dv_acc[...]

    q_spec = pl.BlockSpec((None, t, d), lambda hh, j, i: (hh, jnp.maximum(i, j), 0))
    kv_spec = pl.BlockSpec((None, t, d), lambda hh, j, i: (hh, j, 0))
    lse_spec = pl.BlockSpec((None, t, 1), lambda hh, j, i: (hh, jnp.maximum(i, j), 0))
    return pl.pallas_call(
        body, name="mla_bwd", grid=(h, nb, nb),
        in_specs=[q_spec, kv_spec, kv_spec, q_spec, lse_spec, q_spec],
        out_specs=[pl.BlockSpec((None, s, d), lambda hh, j, i: (hh, 0, 0)), kv_spec, kv_spec],
        out_shape=[jax.ShapeDtypeStruct((h, s, d), F32)] * 3,
        scratch_shapes=[pltpu.VMEM((t, d), F32), pltpu.VMEM((t, d), F32)],
        compiler_params=_params("parallel", "arbitrary", "arbitrary"),
    )(q, k, v, o, lse, do)


@jax.custom_vjp
def flash(q, k, v):
    return _flash_fwd_call(q, k, v)[0]


def _flash_fwd(q, k, v):
    o, lse = _flash_fwd_call(q, k, v)
    return o, (q, k, v, o, lse)


def _flash_bwd(res, do):
    return tuple(_flash_bwd_call(*res, do))


flash.defvjp(_flash_fwd, _flash_bwd)


HBM_SPEC = pl.BlockSpec(memory_space=pltpu.HBM)


def _allgather(shard, name):
    def body(x_ref, out_ref, send_sems, recv_sems, local_sem):
        x, y, c = lax.axis_index("x"), lax.axis_index("y"), lax.axis_index("c")
        me, sibling = (x, y, c), (x, y, 1 - c)
        chips = [(1 - x, y), (x, 1 - y), (1 - x, 1 - y)]

        def rows(px, py, pc):
            return out_ref.at[4 * px + 2 * py + pc]

        def copy(k, block, to, src=None):
            return pltpu.make_async_remote_copy(
                src_ref=rows(*block) if src is None else src, dst_ref=rows(*block),
                send_sem=send_sems.at[k], recv_sem=recv_sems.at[k], device_id=to, device_id_type=MESH_ID)

        mine = pltpu.make_async_copy(x_ref, rows(*me), local_sem)
        mine.start()
        first = [copy(0, me, sibling, src=x_ref)]
        first += [copy(1 + j, me, (*chip, c), src=x_ref) for j, chip in enumerate(chips)]
        for cp in first:
            cp.start()
        passed = [copy(4 + j, (*chip, c), sibling) for j, chip in enumerate(chips)]
        for j, chip in enumerate(chips):
            copy(1 + j, (*chip, c), me).wait_recv()
            passed[j].start()
        copy(0, sibling, me).wait_recv()
        for j, chip in enumerate(chips):
            copy(4 + j, (*chip, 1 - c), me).wait_recv()
        for cp in first + passed:
            cp.wait_send()
        mine.wait()

    return pl.pallas_call(
        body, name=name, out_shape=jax.ShapeDtypeStruct((N_DEV,) + shard.shape, shard.dtype),
        in_specs=[HBM_SPEC], out_specs=HBM_SPEC,
        scratch_shapes=[pltpu.SemaphoreType.DMA((7,)), pltpu.SemaphoreType.DMA((7,)), pltpu.SemaphoreType.DMA],
    )(shard)


def _exchange(parts, name):
    def body(in_ref, out_ref, send_sems, recv_sems, local_sem):
        x, y, c = lax.axis_index("x"), lax.axis_index("y"), lax.axis_index("c")
        me = 4 * x + 2 * y + c
        mine = pltpu.make_async_copy(in_ref.at[me], out_ref.at[me], local_sem)
        mine.start()
        copies = []
        for k in range(1, N_DEV):
            px = 1 - x if k & 4 else x
            py = 1 - y if k & 2 else y
            pc = 1 - c if k & 1 else c
            cp = pltpu.make_async_remote_copy(
                src_ref=in_ref.at[4 * px + 2 * py + pc], dst_ref=out_ref.at[me],
                send_sem=send_sems.at[k - 1], recv_sem=recv_sems.at[k - 1],
                device_id=(px, py, pc), device_id_type=MESH_ID)
            cp.start()
            copies.append(cp)
        for cp in copies:
            cp.wait()
        mine.wait()

    return pl.pallas_call(
        body, name=name, out_shape=jax.ShapeDtypeStruct(parts.shape, parts.dtype),
        in_specs=[HBM_SPEC], out_specs=HBM_SPEC,
        scratch_shapes=[pltpu.SemaphoreType.DMA((7,)), pltpu.SemaphoreType.DMA((7,)), pltpu.SemaphoreType.DMA],
    )(parts)


def _sum8(parts, name):
    _, r, ccols = parts.shape
    tr = _pick(r, 256, 8)

    def body(p_ref, o_ref):
        acc = p_ref[0].astype(F32)
        for i in range(1, N_DEV):
            acc = acc + p_ref[i].astype(F32)
        o_ref[...] = acc

    return pl.pallas_call(
        body, name=name, grid=(r // tr,),
        in_specs=[pl.BlockSpec((N_DEV, tr, ccols), lambda i: (0, i, 0))],
        out_specs=pl.BlockSpec((tr, ccols), lambda i: (i, 0)),
        out_shape=jax.ShapeDtypeStruct((r, ccols), F32),
        compiler_params=_params("parallel"),
    )(parts)


@functools.partial(jax.custom_vjp, nondiff_argnums=(1, 2))
def fsdp_gather(shard, wire_dtype, name):
    return _allgather(shard.astype(wire_dtype), name + "_allgather")


def _fsdp_gather_fwd(shard, wire_dtype, name):
    return _allgather(shard.astype(wire_dtype), name + "_allgather"), None


def _fsdp_gather_bwd(wire_dtype, name, _, ct):
    return (_sum8(_exchange(ct, name + "_exchange"), name + "_sum"),)


fsdp_gather.defvjp(_fsdp_gather_fwd, _fsdp_gather_bwd)


@jax.custom_vjp
def replicated(vec):
    return vec


def _replicated_fwd(vec):
    return vec, None


def _replicated_bwd(_, ct):
    return (_sum8(_allgather(ct, "small_grad_allgather"), "small_grad_sum"),)


replicated.defvjp(_replicated_fwd, _replicated_bwd)


def _adamw(w, g, m, v, name):
    rows, cols = w.shape
    tr = _pick(rows, 256, 8) if rows % 8 == 0 else rows

    def body(w_ref, g_ref, m_ref, v_ref, d_ref, nm_ref, nv_ref):
        g_ = g_ref[...]
        m_ = ADAM_B1 * m_ref[...] + (1.0 - ADAM_B1) * g_
        v_ = ADAM_B2 * v_ref[...] + (1.0 - ADAM_B2) * jnp.square(g_)
        m_hat = m_ / (1.0 - ADAM_B1 ** ADAM_STEP)
        v_hat = v_ / (1.0 - ADAM_B2 ** ADAM_STEP)
        d_ref[...] = -ADAM_LR * (m_hat / (jnp.sqrt(v_hat) + ADAM_EPS) + ADAM_WD * w_ref[...])
        nm_ref[...] = m_
        nv_ref[...] = v_

    spec = pl.BlockSpec((tr, cols), lambda i: (i, 0))
    return pl.pallas_call(
        body, name=name, grid=(rows // tr,), in_specs=[spec] * 4, out_specs=[spec] * 3,
        out_shape=[jax.ShapeDtypeStruct(w.shape, F32)] * 3, compiler_params=_params("parallel"),
    )(w, g, m, v)


COL_SHARDED = ("w_in", "w_uq", "w_ukv", "w_branch_a", "w_branch_b", "w_up", "w_ple")
ROW_SHARDED = ("w_out", "w_down", "w_ple_gate")
BIG = ("w_in", "w_uq", "w_ukv", "w_branch_a", "w_branch_b", "w_out", "w_up", "w_down", "w_ple_gate", "w_ple")
SMALL = ("attn_pre_norm", "attn_post_norm", "b_gate", "q_a_norm", "kv_a_norm", "mlp_pre_norm", "mlp_post_norm",
         "conv_b", "ple_norm", "sinks")
SMALL_COLS = 128


def _pack_rows(arrays, cols, row_mult):
    flat = jnp.concatenate([a.reshape(-1) for a in arrays])
    pad = (-flat.shape[0]) % (cols * row_mult)
    return jnp.pad(flat, (0, pad)).reshape(-1, cols)


def _unpack_big(gathered, shards):
    flat = gathered.reshape(N_DEV, -1)
    out, off = {}, 0
    for name in BIG:
        k, n = shards[name].shape
        piece = flat[:, off:off + k * n].reshape(N_DEV, k, n)
        off += k * n
        if name in COL_SHARDED:
            out[name] = piece.transpose(1, 0, 2).reshape(k, N_DEV * n)
        else:
            out[name] = piece.reshape(N_DEV * k, n)
    return out


def _unpack_small(vec, shapes):
    flat = vec.reshape(-1)
    out, off = {}, 0
    for name in SMALL:
        n = shapes[name]
        out[name] = flat[off:off + n].reshape(1, n)
        off += n + (-n) % SMALL_COLS
    return out


def _pad_lanes(t, width):
    return jnp.pad(t, [(0, 0)] * (t.ndim - 1) + [(0, width - t.shape[-1])])


def _arrange_w_in(w_in):
    k = w_in.shape[0]
    qa, ka, va, cq, ckv, kr, gates = jnp.split(w_in, np.cumsum([512, 128, 128, 256, 128, 32]).tolist(), axis=1)
    qa = qa.reshape(k, A_HEADS, 2, 32).transpose(0, 2, 1, 3).reshape(k, 512)
    ka = _pad_lanes(ka.reshape(k, A_KV_HEADS, 2, 32).transpose(0, 2, 1, 3).reshape(k, 2, 64), 128).reshape(k, 256)
    kr = _pad_lanes(kr.reshape(k, 2, 16), 128).reshape(k, 256)
    return jnp.concatenate([qa, ka, va, cq, ckv, kr], axis=1), gates


FRONT_BOUNDS = (0, 256, 512, 640, 768, 896, 1152, 1280, 1408, 1536)


def _arrange_w_uq(w_uq):
    k = w_uq.shape[0]
    w = w_uq.reshape(k, B_HEADS, NOPE_DIM + ROPE_DIM)
    return jnp.concatenate([w[:, :, :64].reshape(k, 512), w[:, :, 64:80].reshape(k, 128),
                            w[:, :, 80:96].reshape(k, 128)], axis=1)


def _rope_tables(positions, s):
    pos = positions.reshape(s, 1).astype(F32)

    def table(dim, reps):
        inv = ROPE_THETA ** (-(jnp.arange(0, dim, 2, dtype=F32) / dim))
        ang = pos * inv
        return jnp.tile(jnp.cos(ang), (1, reps)), jnp.tile(jnp.sin(ang), (1, reps))

    ca, sa = table(A_HEAD_DIM, 4)
    cb, sb = table(ROPE_DIM, 8)
    return ca, sa, cb, sb


def _local_loss(wts, x, p, tables, target):
    s = x.shape[0]
    small_shapes = {n: wts[n].shape[-1] for n in SMALL}
    small_vec = _pack_rows([_pad_lanes(wts[n].reshape(1, -1), small_shapes[n] + (-small_shapes[n]) % SMALL_COLS)
                            for n in SMALL], SMALL_COLS, 8)
    sm = _unpack_small(replicated(small_vec), small_shapes)
    big = _unpack_big(fsdp_gather(_pack_rows([wts[n] for n in BIG], PACK_COLS, 16), BF16, "weights"),
                      {n: wts[n] for n in BIG})
    conv_w = fsdp_gather(_pack_rows([wts["conv_w"]], SMALL_COLS, 8), F32, "conv_w")
    ch = wts["conv_w"].shape[1]
    conv_w = conv_w.reshape(N_DEV, -1)[:, :CONV_W * ch].reshape(N_DEV, CONV_W, ch)
    conv_w = conv_w.transpose(1, 0, 2).reshape(CONV_W, N_DEV * ch)

    w_front, w_gates = _arrange_w_in(big["w_in"])
    ca, sa, cb, sb = tables

    (h1,) = stage("prenorm", _f_prenorm, [x], [sm["attn_pre_norm"]])
    zf = mm(h1, w_front, "w_front", True)
    gates = mm(h1, w_gates, "w_gates", True)
    qar, kar, va, cqn, ckvn, kpe = stage("prep", _f_prep, [zf], [sm["q_a_norm"], sm["kv_a_norm"]],
                                         [ca, sa, cb, sb], splits=[FRONT_BOUNDS])

    q_a = jnp.concatenate([qar[:, :256].reshape(s, A_HEADS, 32), qar[:, 256:].reshape(s, A_HEADS, 32)], axis=-1)
    q_a = q_a.reshape(s, A_KV_HEADS, A_GROUP, A_HEAD_DIM).transpose(1, 2, 0, 3)
    k_a = jnp.concatenate([kar[:, 0:64].reshape(s, A_KV_HEADS, 32), kar[:, 128:192].reshape(s, A_KV_HEADS, 32)],
                          axis=-1).transpose(1, 0, 2)
    v_a = va.reshape(s, A_KV_HEADS, A_HEAD_DIM).transpose(1, 0, 2)
    ya = swa(q_a, k_a, v_a, sm["sinks"].reshape(-1)).transpose(2, 0, 1, 3).reshape(s, A_HEADS * A_HEAD_DIM)

    qb = mm(cqn, _arrange_w_uq(big["w_uq"]), "w_uq", True)
    kvb = mm(ckvn, big["w_ukv"], "w_ukv", True)
    (qbr,) = stage("qrope", _f_qrope, [qb], [], [cb, sb], splits=[(0, 512, 640, 768)])
    zeros32 = jnp.zeros((s, B_HEADS, HEAD_PAD - NOPE_DIM - ROPE_DIM), F32)
    q_b = jnp.concatenate([qbr[:, :512].reshape(s, B_HEADS, 64), qbr[:, 512:640].reshape(s, B_HEADS, 16),
                           qbr[:, 640:].reshape(s, B_HEADS, 16), zeros32], axis=-1).transpose(1, 0, 2)
    kv = kvb.reshape(s, B_HEADS, NOPE_DIM + V_DIM)
    k_b = jnp.concatenate([kv[:, :, :NOPE_DIM],
                           jnp.broadcast_to(kpe[:, None, 0:16], (s, B_HEADS, 16)),
                           jnp.broadcast_to(kpe[:, None, 128:144], (s, B_HEADS, 16)), zeros32],
                          axis=-1).transpose(1, 0, 2)
    v_b = _pad_lanes(kv[:, :, NOPE_DIM:], HEAD_PAD).transpose(1, 0, 2)
    yb = flash(q_b, k_b, v_b)[:, :, :V_DIM].transpose(1, 0, 2).reshape(s, B_HEADS * V_DIM)

    pa = mm(ya, big["w_branch_a"], "w_branch_a", True)
    pb = mm(yb, big["w_branch_b"], "w_branch_b", True)
    (mixed,) = stage("gate", _f_gate, [gates, pa, pb], [sm["b_gate"][:, :D_MODEL], sm["b_gate"][:, D_MODEL:]], splits=[(0, D_MODEL, 2 * D_MODEL), None, None])
    o = mm(mixed, big["w_out"], "w_out", True)
    x1, h2 = stage("post_attn", _f_post, [x, o], [sm["attn_post_norm"], sm["mlp_pre_norm"]])

    up_g = mm(h2, big["w_up"][:, :D_FF], "w_up_gate", True)
    up_v = mm(h2, big["w_up"][:, D_FF:], "w_up_val", True)
    u_g = dwconv(up_g, conv_w[:, :D_FF], sm["conv_b"][:, :D_FF], "conv_gate")
    u_v = dwconv(up_v, conv_w[:, D_FF:], sm["conv_b"][:, D_FF:], "conv_val")
    (act,) = stage("glu", _f_glu, [u_g, u_v], ts=128)
    ff = mm(act, big["w_down"], "w_down", True)
    x2, h3 = stage("post_mlp", _f_post, [x1, ff], [sm["mlp_post_norm"], sm["ple_norm"]])

    t = mm(h3, big["w_ple_gate"], "w_ple_gate", True)
    e = mm(p, big["w_ple"], "w_ple", False)
    (rowloss,) = stage("loss", _f_out, [x2, t, e], [], [target])
    return jnp.sum(rowloss)


WEIGHTS = ["attn_pre_norm", "attn_post_norm", "w_in", "b_gate", "sinks", "q_a_norm", "w_uq", "kv_a_norm", "w_ukv",
           "w_branch_a", "w_branch_b", "w_out", "mlp_pre_norm", "mlp_post_norm", "w_up", "conv_w", "conv_b",
           "w_down", "ple_norm", "w_ple_gate", "w_ple"]


def kernel(x, p, positions, attn_pre_norm, attn_post_norm, w_in, b_gate, sinks, q_a_norm, w_uq, kv_a_norm, w_ukv, w_branch_a, w_branch_b, w_out, mlp_pre_norm, mlp_post_norm, w_up, conv_w, conv_b, w_down, ple_norm, w_ple_gate, w_ple, loss_target, m_attn_pre_norm, m_attn_post_norm, m_w_in, m_b_gate, m_sinks, m_q_a_norm, m_w_uq, m_kv_a_norm, m_w_ukv, m_w_branch_a, m_w_branch_b, m_w_out, m_mlp_pre_norm, m_mlp_post_norm, m_w_up, m_conv_w, m_conv_b, m_w_down, m_ple_norm, m_w_ple_gate, m_w_ple, v_attn_pre_norm, v_attn_post_norm, v_w_in, v_b_gate, v_sinks, v_q_a_norm, v_w_uq, v_kv_a_norm, v_w_ukv, v_w_branch_a, v_w_branch_b, v_w_out, v_mlp_pre_norm, v_mlp_post_norm, v_w_up, v_conv_w, v_conv_b, v_w_down, v_ple_norm, v_w_ple_gate, v_w_ple):
    given = dict(locals())
    s = x.shape[1]
    wts = {n: given[n][0] if given[n].ndim == 3 else given[n] for n in WEIGHTS}
    tables = _rope_tables(positions, s)
    local_loss, (grads, grad_x) = jax.value_and_grad(_local_loss, argnums=(0, 1))(
        wts, x[0], p[0, 0], tables, loss_target[0])
    loss = lax.psum(local_loss, AXES)

    outs = {"grad": [], "delta": [], "m": [], "v": []}
    for n in WEIGHTS:
        shape = given[n].shape
        w2 = wts[n].reshape(-1, shape[-1])
        g2 = grads[n].reshape(w2.shape)
        delta, new_m, new_v = _adamw(w2, g2, given["m_" + n].reshape(w2.shape), given["v_" + n].reshape(w2.shape),
                                     "adamw_" + n)
        outs["grad"].append(g2.reshape(shape))
        outs["delta"].append(delta.reshape(shape))
        outs["m"].append(new_m.reshape(shape))
        outs["v"].append(new_v.reshape(shape))
    return (loss, grad_x[None], *outs["grad"], *outs["delta"], *outs["m"], *outs["v"])
```

```python
import functools

import numpy as np
import jax
import jax.numpy as jnp
from jax import lax
from jax.experimental import pallas as pl
from jax.experimental.pallas import tpu as pltpu

F32 = jnp.float32
BF16 = jnp.bfloat16
MESH_ID = pl.DeviceIdType.MESH
AXES = ("x", "y", "c")
N_DEV = 8

D_MODEL = 1024
RMS_EPS = 1e-6
ROPE_THETA = 10000.0
SWA_BLOCK = 128
A_HEADS, A_KV_HEADS, A_HEAD_DIM = 8, 2, 64
A_GROUP = A_HEADS // A_KV_HEADS
B_HEADS, Q_LORA, KV_LORA, NOPE_DIM, ROPE_DIM, V_DIM = 8, 256, 128, 64, 32, 64
D_FF = 2816
CONV_W = 3
HEAD_PAD = 128

ADAM_LR, ADAM_B1, ADAM_B2, ADAM_EPS, ADAM_WD, ADAM_STEP = 0.001, 0.9, 0.999, 1e-08, 0.01, 10

VMEM_LIMIT = 48 * 1024 * 1024
MM_TM, MM_TN, MM_TK_TOKENS = 512, 1408, 1024
MM_VMEM_BUDGET = 36 * 1024 * 1024
FLASH_T = 512
CONV_TS, CONV_TC = 512, 256


def _params(*sem):
    return pltpu.CompilerParams(dimension_semantics=sem, vmem_limit_bytes=VMEM_LIMIT)


def _pick(dim, cap, mult):
    best = None
    for t in range(mult, min(dim, cap) + 1, mult):
        if dim % t == 0:
            best = t
    return dim if best is None else best


def _divisors(dim, mult):
    return [t for t in range(mult, dim + 1, mult) if dim % t == 0] or [dim]


def _matmul_tiles(m, n, kdim, form, sizes):
    sa, sb, so = sizes
    tk = _pick(kdim, MM_TK_TOKENS, 128) if form == "tn" else kdim
    cap_m = MM_TN if form == "tn" else MM_TM
    best = None
    for tm in _divisors(m, 128):
        for tn in _divisors(n, 128):
            need = 2 * (tm * tk * sa + tk * tn * sb + tm * tn * so) + (tm * tn * 4 if tk != kdim else 0)
            if tm > cap_m or tn > MM_TN or need > MM_VMEM_BUDGET:
                continue
            if best is None or (tm * tn, tm) > (best[0] * best[1], best[0]):
                best = (tm, tn)
    return best[0], best[1], tk


def _matmul(a, b, form, *, out_dtype=F32, name):
    if form == "tn":
        (kdim, m), n = a.shape, b.shape[1]
    else:
        (m, kdim), n = a.shape, (b.shape[1] if form == "nn" else b.shape[0])
    sizes = (a.dtype.itemsize, b.dtype.itemsize, jnp.dtype(out_dtype).itemsize)
    tm, tn, tk = _matmul_tiles(m, n, kdim, form, sizes)
    nk = kdim // tk
    a_spec = (pl.BlockSpec((tk, tm), lambda i, j, k: (k, i)) if form == "tn"
              else pl.BlockSpec((tm, tk), lambda i, j, k: (i, k)))
    b_spec = (pl.BlockSpec((tn, tk), lambda i, j, k: (j, k)) if form == "nt"
              else pl.BlockSpec((tk, tn), lambda i, j, k: (k, j)))
    dims = (((0 if form == "tn" else 1,), (1 if form == "nt" else 0,)), ((), ()))

    def product(a_ref, b_ref):
        return lax.dot_general(a_ref[...].astype(BF16), b_ref[...].astype(BF16), dims, preferred_element_type=F32)

    if nk == 1:
        def body(a_ref, b_ref, o_ref):
            o_ref[...] = product(a_ref, b_ref).astype(o_ref.dtype)

        scratch = []
    else:
        def body(a_ref, b_ref, o_ref, acc_ref):
            k = pl.program_id(2)

            @pl.when(k == 0)
            def _():
                acc_ref[...] = jnp.zeros_like(acc_ref)

            acc_ref[...] += product(a_ref, b_ref)

            @pl.when(k == nk - 1)
            def _():
                o_ref[...] = acc_ref[...].astype(o_ref.dtype)

        scratch = [pltpu.VMEM((tm, tn), F32)]

    return pl.pallas_call(
        body, name=name, grid=(m // tm, n // tn, nk),
        in_specs=[a_spec, b_spec],
        out_specs=pl.BlockSpec((tm, tn), lambda i, j, k: (i, j)),
        out_shape=jax.ShapeDtypeStruct((m, n), out_dtype),
        scratch_shapes=scratch,
        compiler_params=_params("parallel", "parallel", "arbitrary"),
    )(a, b)


@functools.partial(jax.custom_vjp, nondiff_argnums=(2, 3, 4, 5))
def mm(a, w, form, name, need_da, out_dtype):
    return _matmul(a, w, form, out_dtype=out_dtype, name=name + "_fwd")


def _mm_fwd(a, w, form, name, need_da, out_dtype):
    return _matmul(a, w, form, out_dtype=out_dtype, name=name + "_fwd"), (a, w)


def _mm_bwd(form, name, need_da, out_dtype, res, ct):
    a, w = res
    if form == "nn":
        da = _matmul(ct, w, "nt", out_dtype=a.dtype, name=name + "_da") if need_da else jnp.zeros_like(a)
        dw = _matmul(a, ct, "tn", out_dtype=w.dtype, name=name + "_dw")
    else:
        da = _matmul(ct, w, "nn", out_dtype=a.dtype, name=name + "_da") if need_da else jnp.zeros_like(a)
        dw = _matmul(ct, a, "tn", out_dtype=w.dtype, name=name + "_dw")
    return da, dw


mm.defvjp(_mm_fwd, _mm_bwd)


def _pairs(bounds):
    return list(zip(bounds[:-1], bounds[1:]))


def _split(v, bounds):
    return [v[:, a:b] for a, b in _pairs(bounds)]


def stage(name, f, tiled, params=(), consts=(), splits=None, ts=256, out_dtypes=None):
    tiled, params, consts = tuple(tiled), tuple(params), tuple(consts)
    n_t, n_p, n_c = len(tiled), len(params), len(consts)
    s = tiled[0].shape[0]
    ts = min(ts, s)
    grid = (s // ts,)
    if splits is None:
        splits = [None] * n_t
    in_bounds = [(0, t.shape[1]) if b is None else tuple(b) for t, b in zip(tiled, splits)]

    def tile_aval(arr):
        return jax.ShapeDtypeStruct((ts, arr.shape[1]), arr.dtype)

    slab_avals = [[jax.ShapeDtypeStruct((ts, e - a), t.dtype) for a, e in _pairs(b)]
                  for t, b in zip(tiled, in_bounds)]
    out_avals = jax.eval_shape(f, slab_avals, list(params), [tile_aval(c) for c in consts])
    out_bounds = [tuple(np.cumsum([0] + [o.shape[1] for o in slabs]).tolist()) for slabs in out_avals]
    out_dtypes = [F32] * len(out_bounds) if out_dtypes is None else out_dtypes
    out_shapes = [jax.ShapeDtypeStruct((s, b[-1]), d) for b, d in zip(out_bounds, out_dtypes)]

    def row_spec(width):
        return pl.BlockSpec((ts, width), lambda i: (i, 0))

    def par_spec(arr):
        return pl.BlockSpec(arr.shape, lambda i: (0, 0))

    in_specs = ([row_spec(t.shape[1]) for t in tiled] + [par_spec(p) for p in params]
                + [row_spec(c.shape[1]) for c in consts])

    def load(refs):
        t = [_split(r[...], b) for r, b in zip(refs[:n_t], in_bounds)]
        p = [r[...] for r in refs[n_t:n_t + n_p]]
        c = [r[...] for r in refs[n_t + n_p:n_t + n_p + n_c]]
        return t, p, c

    def store(refs, values, bounds):
        for ref, slabs, b in zip(refs, values, bounds):
            for v, (a, e) in zip(slabs, _pairs(b)):
                ref[:, a:e] = v.astype(ref.dtype)

    def run_fwd(tiled, params, consts):
        def body(*refs):
            t, p, c = load(refs)
            store(refs[n_t + n_p + n_c:], f(t, p, c), out_bounds)

        return pl.pallas_call(
            body, name=name + "_fwd", grid=grid, in_specs=in_specs,
            out_specs=[row_spec(b[-1]) for b in out_bounds], out_shape=out_shapes,
            compiler_params=_params("parallel"),
        )(*tiled, *params, *consts)

    def run_bwd(tiled, params, consts, cts):
        n_in = n_t + n_p + n_c
        n_o = len(out_bounds)

        def body(*refs):
            t, p, c = load(refs)
            g = [_split(r[...].astype(F32), b) for r, b in zip(refs[n_in:n_in + n_o], out_bounds)]
            _, pull = jax.vjp(lambda t_, p_: f(t_, p_, c), t, p)
            dt, dp = pull(g)
            store(refs[n_in + n_o:n_in + n_o + n_t], dt, in_bounds)
            first = pl.program_id(0) == 0
            for ref, d in zip(refs[n_in + n_o + n_t:], dp):
                @pl.when(first)
                def _(ref=ref):
                    ref[...] = jnp.zeros_like(ref)

                ref[...] += d

        res = pl.pallas_call(
            body, name=name + "_bwd", grid=grid,
            in_specs=in_specs + [row_spec(b[-1]) for b in out_bounds],
            out_specs=[row_spec(t.shape[1]) for t in tiled] + [par_spec(p) for p in params],
            out_shape=[jax.ShapeDtypeStruct(t.shape, t.dtype) for t in tiled]
                      + [jax.ShapeDtypeStruct(p.shape, F32) for p in params],
            compiler_params=_params("arbitrary"),
        )(*tiled, *params, *consts, *cts)
        return tuple(res[:n_t]), tuple(res[n_t:])

    @jax.custom_vjp
    def op(tiled, params, consts):
        return tuple(run_fwd(tiled, params, consts))

    def op_fwd(tiled, params, consts):
        return op(tiled, params, consts), (tiled, params, consts)

    def op_bwd(res, cts):
        tiled, params, consts = res
        dt, dp = run_bwd(tiled, params, consts, cts)
        return dt, dp, tuple(jnp.zeros_like(c) for c in consts)

    op.defvjp(op_fwd, op_bwd)
    return op(tiled, params, consts)


def _rms(t, g):
    return t * lax.rsqrt(jnp.mean(t * t, axis=-1, keepdims=True) + RMS_EPS) * g


def _rope(t1, t2, c, s):
    return t1 * c - t2 * s, t2 * c + t1 * s


def _f_prenorm(t, p, c):
    return [[_rms(t[0][0], p[0])]]


def _f_prep(t, p, c):
    qa1, qa2, ka1, ka2, va, cq, ckv, kr1, kr2 = t[0]
    ca, sa, cb, sb = c
    ca2, sa2 = jnp.concatenate([ca, ca], axis=1), jnp.concatenate([sa, sa], axis=1)
    return [list(_rope(qa1, qa2, ca2, sa2)), list(_rope(ka1, ka2, ca, sa)), [va],
            [_rms(cq, p[0])], [_rms(ckv, p[1])], list(_rope(kr1, kr2, cb, sb))]


def _f_qrope(t, p, c):
    nope, pe1, pe2 = t[0]
    return [[nope] + list(_rope(pe1, pe2, c[0], c[1]))]


def _f_gate(t, p, c):
    (ga, gb), (pa,), (pb,) = t
    ba, bb = p
    return [[jax.nn.sigmoid(ga + ba) * pa + jax.nn.sigmoid(gb + bb) * pb]]


def _f_post(t, p, c):
    x1 = t[0][0] + _rms(t[1][0], p[0])
    return [[x1], [_rms(x1, p[1])]]


def _f_glu(t, p, c):
    return [[jax.nn.gelu(t[0][0], approximate=True) * t[1][0]]]


def _f_out(t, p, c):
    y = t[0][0] + jax.nn.sigmoid(t[1][0]) * t[2][0]
    err = y - c[0]
    return [[0.5 * jnp.mean(err * err, axis=-1, keepdims=True)]]


def _shift_down(cur, prev, has_prev):
    rows = cur.shape[0]
    row = lax.broadcasted_iota(jnp.int32, cur.shape, 0)
    m1 = prev[7:8, :] * has_prev
    m2 = prev[6:7, :] * has_prev
    u1 = jnp.where(row >= 1, pltpu.roll(cur, 1, 0), m1)
    u2 = jnp.where(row >= 2, pltpu.roll(cur, 2, 0), jnp.where(row == 1, m1, m2))
    return u1, u2


def _shift_up(cur, nxt, has_next):
    rows = cur.shape[0]
    row = lax.broadcasted_iota(jnp.int32, cur.shape, 0)
    n0 = nxt[0:1, :] * has_next
    n1 = nxt[1:2, :] * has_next
    d1 = jnp.where(row < rows - 1, pltpu.roll(cur, rows - 1, 0), n0)
    d2 = jnp.where(row < rows - 2, pltpu.roll(cur, rows - 2, 0), jnp.where(row == rows - 2, n0, n1))
    return d1, d2


def _conv_tiles(s, ch):
    ts = min(CONV_TS, s)
    tc = _pick(ch, CONV_TC, 128)
    return ts, tc, s // ts, ch // tc


def _conv_fwd_call(up, w, b, name):
    s, ch = up.shape
    ts, tc, nt, nc = _conv_tiles(s, ch)
    hb = ts // 8

    def body(cur_ref, prev_ref, w_ref, b_ref, o_ref):
        cur = cur_ref[...]
        u1, u2 = _shift_down(cur, prev_ref[...], (pl.program_id(1) > 0).astype(F32))
        o_ref[...] = w_ref[2:3, :] * cur + w_ref[1:2, :] * u1 + w_ref[0:1, :] * u2 + b_ref[...]

    return pl.pallas_call(
        body, name=name + "_fwd", grid=(nc, nt),
        in_specs=[pl.BlockSpec((ts, tc), lambda c, i: (i, c)),
                  pl.BlockSpec((8, tc), lambda c, i: (jnp.maximum(i * hb - 1, 0), c)),
                  pl.BlockSpec((CONV_W, tc), lambda c, i: (0, c)),
                  pl.BlockSpec((1, tc), lambda c, i: (0, c))],
        out_specs=pl.BlockSpec((ts, tc), lambda c, i: (i, c)),
        out_shape=jax.ShapeDtypeStruct((s, ch), F32),
        compiler_params=_params("parallel", "parallel"),
    )(up, up, w, b)


def _conv_bwd_call(up, w, du, name):
    s, ch = up.shape
    ts, tc, nt, nc = _conv_tiles(s, ch)
    hb = ts // 8

    def body(cur_ref, prev_ref, w_ref, du_ref, nxt_ref, dup_ref, dw_ref, db_ref):
        i = pl.program_id(1)
        cur, du = cur_ref[...], du_ref[...]
        u1, u2 = _shift_down(cur, prev_ref[...], (i > 0).astype(F32))
        d1, d2 = _shift_up(du, nxt_ref[...], (i < nt - 1).astype(F32))
        dup_ref[...] = w_ref[2:3, :] * du + w_ref[1:2, :] * d1 + w_ref[0:1, :] * d2

        @pl.when(i == 0)
        def _():
            dw_ref[...] = jnp.zeros_like(dw_ref)
            db_ref[...] = jnp.zeros_like(db_ref)

        dw_ref[0:1, :] += jnp.sum(du * u2, axis=0, keepdims=True)
        dw_ref[1:2, :] += jnp.sum(du * u1, axis=0, keepdims=True)
        dw_ref[2:3, :] += jnp.sum(du * cur, axis=0, keepdims=True)
        db_ref[...] += jnp.sum(du, axis=0, keepdims=True)

    return pl.pallas_call(
        body, name=name + "_bwd", grid=(nc, nt),
        in_specs=[pl.BlockSpec((ts, tc), lambda c, i: (i, c)),
                  pl.BlockSpec((8, tc), lambda c, i: (jnp.maximum(i * hb - 1, 0), c)),
                  pl.BlockSpec((CONV_W, tc), lambda c, i: (0, c)),
                  pl.BlockSpec((ts, tc), lambda c, i: (i, c)),
                  pl.BlockSpec((8, tc), lambda c, i: (jnp.minimum((i + 1) * hb, s // 8 - 1), c))],
        out_specs=[pl.BlockSpec((ts, tc), lambda c, i: (i, c)),
                   pl.BlockSpec((CONV_W, tc), lambda c, i: (0, c)),
                   pl.BlockSpec((1, tc), lambda c, i: (0, c))],
        out_shape=[jax.ShapeDtypeStruct((s, ch), F32), jax.ShapeDtypeStruct((CONV_W, ch), F32),
                   jax.ShapeDtypeStruct((1, ch), F32)],
        compiler_params=_params("parallel", "arbitrary"),
    )(up, up, w, du, du)


@functools.partial(jax.custom_vjp, nondiff_argnums=(3,))
def dwconv(up, w, b, name):
    return _conv_fwd_call(up, w, b, name)


def _dwconv_fwd(up, w, b, name):
    return _conv_fwd_call(up, w, b, name), (up, w)


def _dwconv_bwd(name, res, ct):
    up, w = res
    return tuple(_conv_bwd_call(up, w, ct, name))


dwconv.defvjp(_dwconv_fwd, _dwconv_bwd)


def _swa_probs(q, kp, kc, sink, prev_off):
    scale = A_HEAD_DIM ** -0.5
    nt = (((1,), (1,)), ((), ()))
    sp = lax.dot_general(q, kp, nt, preferred_element_type=F32) * scale
    sc = lax.dot_general(q, kc, nt, preferred_element_type=F32) * scale
    qi = lax.broadcasted_iota(jnp.int32, sp.shape, 0)
    kj = lax.broadcasted_iota(jnp.int32, sp.shape, 1)
    sp = jnp.where(kj > qi + prev_off, sp, -jnp.inf)
    sc = jnp.where(kj <= qi, sc, -jnp.inf)
    m = jnp.maximum(jnp.maximum(jnp.max(sp, axis=-1, keepdims=True), jnp.max(sc, axis=-1, keepdims=True)), sink)
    ep, ec, es = jnp.exp(sp - m), jnp.exp(sc - m), jnp.exp(sink - m)
    den = jnp.sum(ep, axis=-1, keepdims=True) + jnp.sum(ec, axis=-1, keepdims=True) + es
    return ep / den, ec / den, es / den


def _swa_specs(s):
    blk = SWA_BLOCK
    q_spec = pl.BlockSpec((None, A_GROUP, blk, A_HEAD_DIM), lambda g, n: (g, 0, n, 0))
    prev_spec = pl.BlockSpec((None, blk, A_HEAD_DIM), lambda g, n: (g, jnp.maximum(n - 1, 0), 0))
    cur_spec = pl.BlockSpec((None, blk, A_HEAD_DIM), lambda g, n: (g, n, 0))
    sink_spec = pl.BlockSpec(memory_space=pltpu.SMEM)
    return q_spec, prev_spec, cur_spec, sink_spec


def _swa_fwd_call(q, k, v, sinks):
    s = q.shape[2]
    q_spec, prev_spec, cur_spec, sink_spec = _swa_specs(s)

    def body(q_ref, kp_ref, kc_ref, vp_ref, vc_ref, sink_ref, o_ref):
        g, n = pl.program_id(0), pl.program_id(1)
        prev_off = jnp.where(n > 0, 0, SWA_BLOCK)
        kp, kc = kp_ref[...].astype(BF16), kc_ref[...].astype(BF16)
        vp, vc = vp_ref[...].astype(BF16), vc_ref[...].astype(BF16)
        for h in range(A_GROUP):
            pp, pc, _ = _swa_probs(q_ref[h].astype(BF16), kp, kc, sink_ref[g * A_GROUP + h], prev_off)
            o_ref[h] = (jnp.dot(pp.astype(BF16), vp, preferred_element_type=F32)
                        + jnp.dot(pc.astype(BF16), vc, preferred_element_type=F32))

    return pl.pallas_call(
        body, name="swa_fwd", grid=(A_KV_HEADS, s // SWA_BLOCK),
        in_specs=[q_spec, prev_spec, cur_spec, prev_spec, cur_spec, sink_spec],
        out_specs=q_spec, out_shape=jax.ShapeDtypeStruct(q.shape, F32),
        compiler_params=_params("parallel", "parallel"),
    )(q, k, k, v, v, sinks)


def _swa_bwd_call(q, k, v, sinks, do):
    s = q.shape[2]
    q_spec, prev_spec, cur_spec, sink_spec = _swa_specs(s)
    scale = A_HEAD_DIM ** -0.5
    tn = (((0,), (0,)), ((), ()))
    nt = (((1,), (1,)), ((), ()))
    dsink_spec = pl.BlockSpec((None, A_GROUP, SWA_BLOCK, 1), lambda g, n: (g, 0, 0, 0))

    def body(q_ref, kp_ref, kc_ref, vp_ref, vc_ref, sink_ref, do_ref,
             dq_ref, dkp_ref, dkc_ref, dvp_ref, dvc_ref, dsink_ref):
        g, n = pl.program_id(0), pl.program_id(1)
        prev_off = jnp.where(n > 0, 0, SWA_BLOCK)
        kp, kc = kp_ref[...].astype(BF16), kc_ref[...].astype(BF16)
        vp, vc = vp_ref[...].astype(BF16), vc_ref[...].astype(BF16)

        @pl.when(n == 0)
        def _():
            dsink_ref[...] = jnp.zeros_like(dsink_ref)

        dkp = jnp.zeros(kp.shape, F32)
        dkc = jnp.zeros(kp.shape, F32)
        dvp = jnp.zeros(kp.shape, F32)
        dvc = jnp.zeros(kp.shape, F32)
        for h in range(A_GROUP):
            qh = q_ref[h].astype(BF16)
            pp, pc, ps = _swa_probs(qh, kp, kc, sink_ref[g * A_GROUP + h], prev_off)
            ppb, pcb = pp.astype(BF16), pc.astype(BF16)
            out = jnp.dot(ppb, vp, preferred_element_type=F32) + jnp.dot(pcb, vc, preferred_element_type=F32)
            doh = do_ref[h]
            dob = doh.astype(BF16)
            delta = jnp.sum(doh * out, axis=-1, keepdims=True)
            dsp = (pp * (lax.dot_general(dob, vp, nt, preferred_element_type=F32) - delta)).astype(BF16)
            dsc = (pc * (lax.dot_general(dob, vc, nt, preferred_element_type=F32) - delta)).astype(BF16)
            dsink_ref[h] += -ps * delta
            dq_ref[h] = (jnp.dot(dsp, kp, preferred_element_type=F32)
                         + jnp.dot(dsc, kc, preferred_element_type=F32)) * scale
            dkp += lax.dot_general(dsp, qh, tn, preferred_element_type=F32)
            dkc += lax.dot_general(dsc, qh, tn, preferred_element_type=F32)
            dvp += lax.dot_general(ppb, dob, tn, preferred_element_type=F32)
            dvc += lax.dot_general(pcb, dob, tn, preferred_element_type=F32)
        dkp_ref[...] = dkp * scale
        dkc_ref[...] = dkc * scale
        dvp_ref[...] = dvp
        dvc_ref[...] = dvc

    kv_shape = jax.ShapeDtypeStruct(k.shape, F32)
    return pl.pallas_call(
        body, name="swa_bwd", grid=(A_KV_HEADS, s // SWA_BLOCK),
        in_specs=[q_spec, prev_spec, cur_spec, prev_spec, cur_spec, sink_spec, q_spec],
        out_specs=[q_spec, cur_spec, cur_spec, cur_spec, cur_spec, dsink_spec],
        out_shape=[jax.ShapeDtypeStruct(q.shape, F32), kv_shape, kv_shape, kv_shape, kv_shape,
                   jax.ShapeDtypeStruct((A_KV_HEADS, A_GROUP, SWA_BLOCK, 1), F32)],
        compiler_params=_params("parallel", "arbitrary"),
    )(q, k, k, v, v, sinks, do)


@jax.custom_vjp
def swa(q, k, v, sinks):
    return _swa_fwd_call(q, k, v, sinks)


def _swa_fwd(q, k, v, sinks):
    return _swa_fwd_call(q, k, v, sinks), (q, k, v, sinks)


def _swa_bwd(res, do):
    q, k, v, sinks = res
    dq, dkp, dkc, dvp, dvc, dsink = _swa_bwd_call(q, k, v, sinks, do)

    def fold(prev_part, cur_part):
        shifted = jnp.concatenate([prev_part[:, SWA_BLOCK:], jnp.zeros_like(prev_part[:, :SWA_BLOCK])], axis=1)
        return cur_part + shifted

    return dq, fold(dkp, dkc), fold(dvp, dvc), jnp.sum(dsink, axis=(2, 3)).reshape(-1)


swa.defvjp(_swa_fwd, _swa_bwd)


MLA_SCALE = (NOPE_DIM + ROPE_DIM) ** -0.5


def _causal_mask(i, j, t):
    row = lax.broadcasted_iota(jnp.int32, (t, t), 0) + i * t
    col = lax.broadcasted_iota(jnp.int32, (t, t), 1) + j * t
    return row >= col


def _flash_fwd_call(q, k, v):
    h, s, d = q.shape
    t = min(FLASH_T, s)
    nb = s // t
    nt = (((1,), (1,)), ((), ()))

    def body(q_ref, k_ref, v_ref, o_ref, lse_ref, m_ref, l_ref, acc_ref):
        i, j = pl.program_id(1), pl.program_id(2)

        @pl.when(j == 0)
        def _():
            m_ref[...] = jnp.full_like(m_ref, -jnp.inf)
            l_ref[...] = jnp.zeros_like(l_ref)
            acc_ref[...] = jnp.zeros_like(acc_ref)

        @pl.when(j <= i)
        def _():
            sc = lax.dot_general(q_ref[...].astype(BF16), k_ref[...].astype(BF16), nt,
                                 preferred_element_type=F32) * MLA_SCALE
            sc = jnp.where(_causal_mask(i, j, t), sc, -jnp.inf)
            m_new = jnp.maximum(m_ref[...], jnp.max(sc, axis=-1, keepdims=True))
            alpha = jnp.exp(m_ref[...] - m_new)
            p = jnp.exp(sc - m_new)
            l_ref[...] = alpha * l_ref[...] + jnp.sum(p, axis=-1, keepdims=True)
            acc_ref[...] = alpha * acc_ref[...] + jnp.dot(p.astype(BF16), v_ref[...].astype(BF16),
                                                          preferred_element_type=F32)
            m_ref[...] = m_new

        @pl.when(j == nb - 1)
        def _():
            o_ref[...] = acc_ref[...] / l_ref[...]
            lse_ref[...] = m_ref[...] + jnp.log(l_ref[...])

    q_spec = pl.BlockSpec((None, t, d), lambda hh, i, j: (hh, i, 0))
    kv_spec = pl.BlockSpec((None, t, d), lambda hh, i, j: (hh, jnp.minimum(j, i), 0))
    return pl.pallas_call(
        body, name="mla_fwd", grid=(h, nb, nb),
        in_specs=[q_spec, kv_spec, kv_spec],
        out_specs=[q_spec, pl.BlockSpec((None, t, 1), lambda hh, i, j: (hh, i, 0))],
        out_shape=[jax.ShapeDtypeStruct((h, s, d), F32), jax.ShapeDtypeStruct((h, s, 1), F32)],
        scratch_shapes=[pltpu.VMEM((t, 1), F32), pltpu.VMEM((t, 1), F32), pltpu.VMEM((t, d), F32)],
        compiler_params=_params("parallel", "parallel", "arbitrary"),
    )(q, k, v)


def _flash_bwd_call(q, k, v, o, lse, do):
    h, s, d = q.shape
    t = min(FLASH_T, s)
    nb = s // t
    nt = (((1,), (1,)), ((), ()))
    tn = (((0,), (0,)), ((), ()))

    def body(q_ref, k_ref, v_ref, o_ref, lse_ref, do_ref, dq_ref, dk_ref, dv_ref, dk_acc, dv_acc):
        j, i = pl.program_id(1), pl.program_id(2)

        @pl.when((j == 0) & (i == 0))
        def _():
            dq_ref[...] = jnp.zeros_like(dq_ref)

        @pl.when(i == j)
        def _():
            dk_acc[...] = jnp.zeros_like(dk_acc)
            dv_acc[...] = jnp.zeros_like(dv_acc)

        @pl.when(i >= j)
        def _():
            qb, kb, vb = q_ref[...].astype(BF16), k_ref[...].astype(BF16), v_ref[...].astype(BF16)
            dof = do_ref[...]
            dob = dof.astype(BF16)
            sc = lax.dot_general(qb, kb, nt, preferred_element_type=F32) * MLA_SCALE
            p = jnp.where(_causal_mask(i, j, t), jnp.exp(sc - lse_ref[...]), 0.0)
            delta = jnp.sum(dof * o_ref[...], axis=-1, keepdims=True)
            ds = (p * (lax.dot_general(dob, vb, nt, preferred_element_type=F32) - delta)).astype(BF16)
            dv_acc[...] += lax.dot_general(p.astype(BF16), dob, tn, preferred_element_type=F32)
            dk_acc[...] += lax.dot_general(ds, qb, tn, preferred_element_type=F32)
            rows = pl.ds(pl.multiple_of(i * t, t), t)
            dq_ref[rows, :] += jnp.dot(ds, kb, preferred_element_type=F32) * MLA_SCALE

        @pl.when(i == nb - 1)
        def _():
            dk_ref[...] = dk_acc[...] * MLA_SCALE
            dv_ref[...] = dv_acc[...]

    q_spec = pl.BlockSpec((None, t, d), lambda hh, j, i: (hh, jnp.maximum(i, j), 0))
    kv_spec = pl.BlockSpec((None, t, d), lambda hh, j, i: (hh, j, 0))
    lse_spec = pl.BlockSpec((None, t, 1), lambda hh, j, i: (hh, jnp.maximum(i, j), 0))
    return pl.pallas_call(
        body, name="mla_bwd", grid=(h, nb, nb),
        in_specs=[q_spec, kv_spec, kv_spec, q_spec, lse_spec, q_spec],
        out_specs=[pl.BlockSpec((None, s, d), lambda hh, j, i: (hh, 0, 0)), kv_spec, kv_spec],
        out_shape=[jax.ShapeDtypeStruct((h, s, d), F32)] * 3,
        scratch_shapes=[pltpu.VMEM((t, d), F32), pltpu.VMEM((t, d), F32)],
        compiler_params=_params("parallel", "arbitrary", "arbitrary"),
    )(q, k, v, o, lse, do)


@jax.custom_vjp
def flash(q, k, v):
    return _flash_fwd_call(q, k, v)[0]


def _flash_fwd(q, k, v):
    o, lse = _flash_fwd_call(q, k, v)
    return o, (q, k, v, o, lse)


def _flash_bwd(res, do):
    return tuple(_flash_bwd_call(*res, do))


flash.defvjp(_flash_fwd, _flash_bwd)


HBM_SPEC = pl.BlockSpec(memory_space=pltpu.HBM)


def _allgather(shards, name):
    n_arr = len(shards)

    def body(*refs):
        x_refs, out_refs = refs[:n_arr], refs[n_arr:2 * n_arr]
        send_sems, recv_sems, local_sems = refs[2 * n_arr:]
        x, y, c = lax.axis_index("x"), lax.axis_index("y"), lax.axis_index("c")
        me, sibling = (x, y, c), (x, y, 1 - c)
        chips = [(1 - x, y), (x, 1 - y), (1 - x, 1 - y)]
        arrays = range(n_arr)

        def rows(a, px, py, pc):
            return out_refs[a].at[4 * px + 2 * py + pc]

        def copy(a, k, block, to, src=None):
            return pltpu.make_async_remote_copy(
                src_ref=rows(a, *block) if src is None else src, dst_ref=rows(a, *block),
                send_sem=send_sems.at[k, a], recv_sem=recv_sems.at[k, a], device_id=to, device_id_type=MESH_ID)

        mine = [pltpu.make_async_copy(x_refs[a], rows(a, *me), local_sems.at[a]) for a in arrays]
        for cp in mine:
            cp.start()
        first = []
        for a in arrays:
            first.append(copy(a, 0, me, sibling, src=x_refs[a]))
            first += [copy(a, 1 + j, me, (*chip, c), src=x_refs[a]) for j, chip in enumerate(chips)]
        for cp in first:
            cp.start()
        passed = []
        for j, chip in enumerate(chips):
            for a in arrays:
                copy(a, 1 + j, (*chip, c), me).wait_recv()
                passed.append(copy(a, 4 + j, (*chip, c), sibling))
                passed[-1].start()
        for a in arrays:
            copy(a, 0, sibling, me).wait_recv()
        for j, chip in enumerate(chips):
            for a in arrays:
                copy(a, 4 + j, (*chip, 1 - c), me).wait_recv()
        for cp in first + passed:
            cp.wait_send()
        for cp in mine:
            cp.wait()

    return pl.pallas_call(
        body, name=name, out_shape=[jax.ShapeDtypeStruct((N_DEV,) + s.shape, s.dtype) for s in shards],
        in_specs=[HBM_SPEC] * n_arr, out_specs=[HBM_SPEC] * n_arr,
        scratch_shapes=[pltpu.SemaphoreType.DMA((7, n_arr)), pltpu.SemaphoreType.DMA((7, n_arr)),
                        pltpu.SemaphoreType.DMA((n_arr,))],
    )(*shards)


def _exchange(parts, name):
    n_arr = len(parts)

    def body(*refs):
        in_refs, out_refs = refs[:n_arr], refs[n_arr:2 * n_arr]
        send_sems, recv_sems, local_sems = refs[2 * n_arr:]
        x, y, c = lax.axis_index("x"), lax.axis_index("y"), lax.axis_index("c")
        me = 4 * x + 2 * y + c
        copies = [pltpu.make_async_copy(in_refs[a].at[me], out_refs[a].at[me], local_sems.at[a]) for a in range(n_arr)]
        for k in range(1, N_DEV):
            px = 1 - x if k & 4 else x
            py = 1 - y if k & 2 else y
            pc = 1 - c if k & 1 else c
            for a in range(n_arr):
                copies.append(pltpu.make_async_remote_copy(
                    src_ref=in_refs[a].at[4 * px + 2 * py + pc], dst_ref=out_refs[a].at[me],
                    send_sem=send_sems.at[k - 1, a], recv_sem=recv_sems.at[k - 1, a],
                    device_id=(px, py, pc), device_id_type=MESH_ID))
        for cp in copies:
            cp.start()
        for cp in copies:
            cp.wait()

    return pl.pallas_call(
        body, name=name, out_shape=[jax.ShapeDtypeStruct(p.shape, p.dtype) for p in parts],
        in_specs=[HBM_SPEC] * n_arr, out_specs=[HBM_SPEC] * n_arr,
        scratch_shapes=[pltpu.SemaphoreType.DMA((7, n_arr)), pltpu.SemaphoreType.DMA((7, n_arr)),
                        pltpu.SemaphoreType.DMA((n_arr,))],
    )(*parts)


def _sum8(parts, name):
    _, r, ccols = parts.shape
    tr = _pick(r, 256, 8)

    def body(p_ref, o_ref):
        acc = p_ref[0].astype(F32)
        for i in range(1, N_DEV):
            acc = acc + p_ref[i].astype(F32)
        o_ref[...] = acc

    return pl.pallas_call(
        body, name=name, grid=(r // tr,),
        in_specs=[pl.BlockSpec((N_DEV, tr, ccols), lambda i: (0, i, 0))],
        out_specs=pl.BlockSpec((tr, ccols), lambda i: (i, 0)),
        out_shape=jax.ShapeDtypeStruct((r, ccols), F32),
        compiler_params=_params("parallel"),
    )(parts)


def _gather_wire(shards, wire_dtypes):
    return tuple(_allgather([s.astype(d) for s, d in zip(shards, wire_dtypes)], "weights_allgather"))


@functools.partial(jax.custom_vjp, nondiff_argnums=(1,))
def fsdp_gather(shards, wire_dtypes):
    return _gather_wire(shards, wire_dtypes)


def _fsdp_gather_fwd(shards, wire_dtypes):
    return _gather_wire(shards, wire_dtypes), None


def _fsdp_gather_bwd(wire_dtypes, _, cts):
    received = _exchange(list(cts), "grads_exchange")
    return (tuple(_sum8(r, "grad_sum_%d" % i) for i, r in enumerate(received)),)


fsdp_gather.defvjp(_fsdp_gather_fwd, _fsdp_gather_bwd)


@jax.custom_vjp
def replicated(vec):
    return vec


def _replicated_fwd(vec):
    return vec, None


def _replicated_bwd(_, ct):
    return (_sum8(_allgather([ct], "small_grad_allgather")[0], "small_grad_sum"),)


replicated.defvjp(_replicated_fwd, _replicated_bwd)


def _adamw(w, g, m, v, name):
    rows, cols = w.shape
    tr = _pick(rows, 256, 8) if rows % 8 == 0 else rows

    def body(w_ref, g_ref, m_ref, v_ref, d_ref, nm_ref, nv_ref):
        g_ = g_ref[...]
        m_ = ADAM_B1 * m_ref[...] + (1.0 - ADAM_B1) * g_
        v_ = ADAM_B2 * v_ref[...] + (1.0 - ADAM_B2) * jnp.square(g_)
        m_hat = m_ / (1.0 - ADAM_B1 ** ADAM_STEP)
        v_hat = v_ / (1.0 - ADAM_B2 ** ADAM_STEP)
        d_ref[...] = -ADAM_LR * (m_hat / (jnp.sqrt(v_hat) + ADAM_EPS) + ADAM_WD * w_ref[...])
        nm_ref[...] = m_
        nv_ref[...] = v_

    spec = pl.BlockSpec((tr, cols), lambda i: (i, 0))
    return pl.pallas_call(
        body, name=name, grid=(rows // tr,), in_specs=[spec] * 4, out_specs=[spec] * 3,
        out_shape=[jax.ShapeDtypeStruct(w.shape, F32)] * 3, compiler_params=_params("parallel"),
    )(w, g, m, v)


COL_SHARDED = ("w_in", "w_uq", "w_ukv", "w_branch_a", "w_branch_b", "w_up", "w_ple")
ROW_SHARDED = ("w_out", "w_down", "w_ple_gate")
BIG = ("w_in", "w_uq", "w_ukv", "w_branch_a", "w_branch_b", "w_out", "w_up", "w_down", "w_ple_gate", "w_ple")
SMALL = ("attn_pre_norm", "attn_post_norm", "b_gate", "q_a_norm", "kv_a_norm", "mlp_pre_norm", "mlp_post_norm",
         "conv_b", "ple_norm", "sinks")
SMALL_COLS = 128


def _pack_rows(arrays, cols, row_mult):
    flat = jnp.concatenate([a.reshape(-1) for a in arrays])
    pad = (-flat.shape[0]) % (cols * row_mult)
    return jnp.pad(flat, (0, pad)).reshape(-1, cols)


def _unpack_small(vec, shapes):
    flat = vec.reshape(-1)
    out, off = {}, 0
    for name in SMALL:
        n = shapes[name]
        out[name] = flat[off:off + n].reshape(1, n)
        off += n + (-n) % SMALL_COLS
    return out


def _pad_lanes(t, width):
    return jnp.pad(t, [(0, 0)] * (t.ndim - 1) + [(0, width - t.shape[-1])])


def _pad_rows(t, rows):
    return jnp.pad(t, [(0, 0)] * (t.ndim - 2) + [(0, rows - t.shape[-2]), (0, 0)])


def _arrange_w_in_t(wt):
    k = wt.shape[1]
    qa, ka, va, cq, ckv, kr, gates = jnp.split(wt, np.cumsum([512, 128, 128, 256, 128, 32]).tolist(), axis=0)
    qa = qa.reshape(A_HEADS, 2, 32, k).transpose(1, 0, 2, 3).reshape(512, k)
    ka = _pad_rows(ka.reshape(A_KV_HEADS, 2, 32, k).transpose(1, 0, 2, 3).reshape(2, 64, k), 128).reshape(256, k)
    kr = _pad_rows(kr.reshape(2, 16, k), 128).reshape(256, k)
    return jnp.concatenate([qa, ka, va, cq, ckv, kr], axis=0), gates


FRONT_BOUNDS = (0, 256, 512, 640, 768, 896, 1152, 1280, 1408, 1536)


def _arrange_w_uq_t(wt):
    k = wt.shape[1]
    w = wt.reshape(B_HEADS, NOPE_DIM + ROPE_DIM, k)
    return jnp.concatenate([w[:, :64].reshape(512, k), w[:, 64:80].reshape(128, k), w[:, 80:96].reshape(128, k)],
                           axis=0)


def _rope_tables(positions, s):
    pos = positions.reshape(s, 1).astype(F32)

    def table(dim, reps):
        inv = ROPE_THETA ** (-(jnp.arange(0, dim, 2, dtype=F32) / dim))
        ang = pos * inv
        return jnp.tile(jnp.cos(ang), (1, reps)), jnp.tile(jnp.sin(ang), (1, reps))

    ca, sa = table(A_HEAD_DIM, 4)
    cb, sb = table(ROPE_DIM, 8)
    return ca, sa, cb, sb


def _local_loss(wts, x, p, tables, target):
    s = x.shape[0]
    small_shapes = {n: wts[n].shape[-1] for n in SMALL}
    small_vec = _pack_rows([_pad_lanes(wts[n].reshape(1, -1), small_shapes[n] + (-small_shapes[n]) % SMALL_COLS)
                            for n in SMALL], SMALL_COLS, 8)
    sm = _unpack_small(replicated(small_vec), small_shapes)
    shards = [wts[n].T if n in COL_SHARDED else wts[n] for n in BIG] + [_pack_rows([wts["conv_w"]], SMALL_COLS, 8)]
    gathered = fsdp_gather(tuple(shards), (BF16,) * len(BIG) + (F32,))
    big = {n: g.reshape(-1, g.shape[2]) for n, g in zip(BIG, gathered)}
    ch = wts["conv_w"].shape[1]
    conv_w = gathered[-1].reshape(N_DEV, -1)[:, :CONV_W * ch].reshape(N_DEV, CONV_W, ch)
    conv_w = conv_w.transpose(1, 0, 2).reshape(CONV_W, N_DEV * ch)

    w_front_t, w_gates_t = _arrange_w_in_t(big["w_in"])
    ca, sa, cb, sb = tables

    (h1,) = stage("prenorm", _f_prenorm, [x], [sm["attn_pre_norm"]], out_dtypes=[BF16])
    zf = mm(h1, w_front_t, "nt", "w_front", True, F32)
    gates = mm(h1, w_gates_t, "nt", "w_gates", True, F32)
    qar, kar, va, cqn, ckvn, kpe = stage("prep", _f_prep, [zf], [sm["q_a_norm"], sm["kv_a_norm"]],
                                         [ca, sa, cb, sb], splits=[FRONT_BOUNDS],
                                         out_dtypes=[F32, F32, F32, BF16, BF16, F32])

    q_a = jnp.concatenate([qar[:, :256].reshape(s, A_HEADS, 32), qar[:, 256:].reshape(s, A_HEADS, 32)], axis=-1)
    q_a = q_a.reshape(s, A_KV_HEADS, A_GROUP, A_HEAD_DIM).transpose(1, 2, 0, 3)
    k_a = jnp.concatenate([kar[:, 0:64].reshape(s, A_KV_HEADS, 32), kar[:, 128:192].reshape(s, A_KV_HEADS, 32)],
                          axis=-1).transpose(1, 0, 2)
    v_a = va.reshape(s, A_KV_HEADS, A_HEAD_DIM).transpose(1, 0, 2)
    ya = swa(q_a, k_a, v_a, sm["sinks"].reshape(-1)).transpose(2, 0, 1, 3).reshape(s, A_HEADS * A_HEAD_DIM)
    ya = ya.astype(BF16)

    qb = mm(cqn, _arrange_w_uq_t(big["w_uq"]), "nt", "w_uq", True, F32)
    kvb = mm(ckvn, big["w_ukv"], "nt", "w_ukv", True, F32)
    (qbr,) = stage("qrope", _f_qrope, [qb], [], [cb, sb], splits=[(0, 512, 640, 768)])
    zeros32 = jnp.zeros((s, B_HEADS, HEAD_PAD - NOPE_DIM - ROPE_DIM), F32)
    q_b = jnp.concatenate([qbr[:, :512].reshape(s, B_HEADS, 64), qbr[:, 512:640].reshape(s, B_HEADS, 16),
                           qbr[:, 640:].reshape(s, B_HEADS, 16), zeros32], axis=-1).transpose(1, 0, 2)
    kv = kvb.reshape(s, B_HEADS, NOPE_DIM + V_DIM)
    k_b = jnp.concatenate([kv[:, :, :NOPE_DIM],
                           jnp.broadcast_to(kpe[:, None, 0:16], (s, B_HEADS, 16)),
                           jnp.broadcast_to(kpe[:, None, 128:144], (s, B_HEADS, 16)), zeros32],
                          axis=-1).transpose(1, 0, 2)
    v_b = _pad_lanes(kv[:, :, NOPE_DIM:], HEAD_PAD).transpose(1, 0, 2)
    yb = flash(q_b, k_b, v_b)[:, :, :V_DIM].transpose(1, 0, 2).reshape(s, B_HEADS * V_DIM).astype(BF16)

    pa = mm(ya, big["w_branch_a"], "nt", "w_branch_a", True, F32)
    pb = mm(yb, big["w_branch_b"], "nt", "w_branch_b", True, F32)
    (mixed,) = stage("gate", _f_gate, [gates, pa, pb], [sm["b_gate"][:, :D_MODEL], sm["b_gate"][:, D_MODEL:]],
                     splits=[(0, D_MODEL, 2 * D_MODEL), None, None], out_dtypes=[BF16])
    o = mm(mixed, big["w_out"], "nn", "w_out", True, F32)
    x1, h2 = stage("post_attn", _f_post, [x, o], [sm["attn_post_norm"], sm["mlp_pre_norm"]], out_dtypes=[F32, BF16])

    up_g = mm(h2, big["w_up"][:D_FF], "nt", "w_up_gate", True, F32)
    up_v = mm(h2, big["w_up"][D_FF:], "nt", "w_up_val", True, F32)
    u_g = dwconv(up_g, conv_w[:, :D_FF], sm["conv_b"][:, :D_FF], "conv_gate")
    u_v = dwconv(up_v, conv_w[:, D_FF:], sm["conv_b"][:, D_FF:], "conv_val")
    (act,) = stage("glu", _f_glu, [u_g, u_v], ts=128, out_dtypes=[BF16])
    ff = mm(act, big["w_down"], "nn", "w_down", True, F32)
    x2, h3 = stage("post_mlp", _f_post, [x1, ff], [sm["mlp_post_norm"], sm["ple_norm"]], out_dtypes=[F32, BF16])

    t = mm(h3, big["w_ple_gate"], "nn", "w_ple_gate", True, F32)
    e = mm(p, big["w_ple"], "nt", "w_ple", False, F32)
    (rowloss,) = stage("loss", _f_out, [x2, t, e], [], [target])
    return jnp.sum(rowloss)


WEIGHTS = ["attn_pre_norm", "attn_post_norm", "w_in", "b_gate", "sinks", "q_a_norm", "w_uq", "kv_a_norm", "w_ukv",
           "w_branch_a", "w_branch_b", "w_out", "mlp_pre_norm", "mlp_post_norm", "w_up", "conv_w", "conv_b",
           "w_down", "ple_norm", "w_ple_gate", "w_ple"]


def kernel(x, p, positions, attn_pre_norm, attn_post_norm, w_in, b_gate, sinks, q_a_norm, w_uq, kv_a_norm, w_ukv, w_branch_a, w_branch_b, w_out, mlp_pre_norm, mlp_post_norm, w_up, conv_w, conv_b, w_down, ple_norm, w_ple_gate, w_ple, loss_target, m_attn_pre_norm, m_attn_post_norm, m_w_in, m_b_gate, m_sinks, m_q_a_norm, m_w_uq, m_kv_a_norm, m_w_ukv, m_w_branch_a, m_w_branch_b, m_w_out, m_mlp_pre_norm, m_mlp_post_norm, m_w_up, m_conv_w, m_conv_b, m_w_down, m_ple_norm, m_w_ple_gate, m_w_ple, v_attn_pre_norm, v_attn_post_norm, v_w_in, v_b_gate, v_sinks, v_q_a_norm, v_w_uq, v_kv_a_norm, v_w_ukv, v_w_branch_a, v_w_branch_b, v_w_out, v_mlp_pre_norm, v_mlp_post_norm, v_w_up, v_conv_w, v_conv_b, v_w_down, v_ple_norm, v_w_ple_gate, v_w_ple):
    given = dict(locals())
    s = x.shape[1]
    wts = {n: given[n][0] if given[n].ndim == 3 else given[n] for n in WEIGHTS}
    tables = _rope_tables(positions, s)
    local_loss, (grads, grad_x) = jax.value_and_grad(_local_loss, argnums=(0, 1))(
        wts, x[0], p[0, 0], tables, loss_target[0])
    loss = lax.psum(local_loss, AXES)

    outs = {"grad": [], "delta": [], "m": [], "v": []}
    for n in WEIGHTS:
        shape = given[n].shape
        w2 = wts[n].reshape(-1, shape[-1])
        g2 = grads[n].reshape(w2.shape)
        delta, new_m, new_v = _adamw(w2, g2, given["m_" + n].reshape(w2.shape), given["v_" + n].reshape(w2.shape),
                                     "adamw_" + n)
        outs["grad"].append(g2.reshape(shape))
        outs["delta"].append(delta.reshape(shape))
        outs["m"].append(new_m.reshape(shape))
        outs["v"].append(new_v.reshape(shape))
    return (loss, grad_x[None], *outs["grad"], *outs["delta"], *outs["m"], *outs["v"])
```

```python
import functools

import numpy as np
import jax
import jax.numpy as jnp
from jax import lax
from jax.experimental import pallas as pl
from jax.experimental.pallas import tpu as pltpu

F32 = jnp.float32
BF16 = jnp.bfloat16
MESH_ID = pl.DeviceIdType.MESH
AXES = ("x", "y", "c")
N_DEV = 8

D_MODEL = 1024
RMS_EPS = 1e-6
ROPE_THETA = 10000.0
SWA_BLOCK = 128
A_HEADS, A_KV_HEADS, A_HEAD_DIM = 8, 2, 64
A_GROUP = A_HEADS // A_KV_HEADS
B_HEADS, Q_LORA, KV_LORA, NOPE_DIM, ROPE_DIM, V_DIM = 8, 256, 128, 64, 32, 64
D_FF = 2816
CONV_W = 3
HEAD_PAD = 128

ADAM_LR, ADAM_B1, ADAM_B2, ADAM_EPS, ADAM_WD, ADAM_STEP = 0.001, 0.9, 0.999, 1e-08, 0.01, 10

VMEM_LIMIT = 48 * 1024 * 1024
MM_TM, MM_TN, MM_TK_TOKENS = 512, 1408, 1024
MM_VMEM_BUDGET = 36 * 1024 * 1024
FLASH_T = 512
CONV_TS, CONV_TC = 128, 2816


def _params(*sem):
    return pltpu.CompilerParams(dimension_semantics=sem, vmem_limit_bytes=VMEM_LIMIT)


def _pick(dim, cap, mult):
    best = None
    for t in range(mult, min(dim, cap) + 1, mult):
        if dim % t == 0:
            best = t
    return dim if best is None else best


def _divisors(dim, mult):
    return [t for t in range(mult, dim + 1, mult) if dim % t == 0] or [dim]


def _matmul_tiles(m, n, kdim, form, sizes):
    sa, sb, so = sizes
    tk = _pick(kdim, MM_TK_TOKENS, 128) if form == "tn" else kdim
    cap_m = MM_TN if form == "tn" else MM_TM
    best = None
    for tm in _divisors(m, 128):
        for tn in _divisors(n, 128):
            need = 2 * (tm * tk * sa + tk * tn * sb + tm * tn * so) + (tm * tn * 4 if tk != kdim else 0)
            if tm > cap_m or tn > MM_TN or need > MM_VMEM_BUDGET:
                continue
            if best is None or (tm * tn, tm) > (best[0] * best[1], best[0]):
                best = (tm, tn)
    return best[0], best[1], tk


def _matmul(a, b, form, *, out_dtype=F32, name):
    if form == "tn":
        (kdim, m), n = a.shape, b.shape[1]
    else:
        (m, kdim), n = a.shape, (b.shape[1] if form == "nn" else b.shape[0])
    sizes = (a.dtype.itemsize, b.dtype.itemsize, jnp.dtype(out_dtype).itemsize)
    tm, tn, tk = _matmul_tiles(m, n, kdim, form, sizes)
    nk = kdim // tk
    a_spec = (pl.BlockSpec((tk, tm), lambda i, j, k: (k, i)) if form == "tn"
              else pl.BlockSpec((tm, tk), lambda i, j, k: (i, k)))
    b_spec = (pl.BlockSpec((tn, tk), lambda i, j, k: (j, k)) if form == "nt"
              else pl.BlockSpec((tk, tn), lambda i, j, k: (k, j)))
    dims = (((0 if form == "tn" else 1,), (1 if form == "nt" else 0,)), ((), ()))

    def product(a_ref, b_ref):
        return lax.dot_general(a_ref[...].astype(BF16), b_ref[...].astype(BF16), dims, preferred_element_type=F32)

    if nk == 1:
        def body(a_ref, b_ref, o_ref):
            o_ref[...] = product(a_ref, b_ref).astype(o_ref.dtype)

        scratch = []
    else:
        def body(a_ref, b_ref, o_ref, acc_ref):
            k = pl.program_id(2)

            @pl.when(k == 0)
            def _():
                acc_ref[...] = jnp.zeros_like(acc_ref)

            acc_ref[...] += product(a_ref, b_ref)

            @pl.when(k == nk - 1)
            def _():
                o_ref[...] = acc_ref[...].astype(o_ref.dtype)

        scratch = [pltpu.VMEM((tm, tn), F32)]

    return pl.pallas_call(
        body, name=name, grid=(m // tm, n // tn, nk),
        in_specs=[a_spec, b_spec],
        out_specs=pl.BlockSpec((tm, tn), lambda i, j, k: (i, j)),
        out_shape=jax.ShapeDtypeStruct((m, n), out_dtype),
        scratch_shapes=scratch,
        compiler_params=_params("parallel", "parallel", "arbitrary"),
    )(a, b)


@functools.partial(jax.custom_vjp, nondiff_argnums=(2, 3, 4, 5))
def mm(a, w, form, name, need_da, out_dtype):
    return _matmul(a, w, form, out_dtype=out_dtype, name=name + "_fwd")


def _mm_fwd(a, w, form, name, need_da, out_dtype):
    return _matmul(a, w, form, out_dtype=out_dtype, name=name + "_fwd"), (a, w)


def _mm_bwd(form, name, need_da, out_dtype, res, ct):
    a, w = res
    if form == "nn":
        da = _matmul(ct, w, "nt", out_dtype=a.dtype, name=name + "_da") if need_da else jnp.zeros_like(a)
        dw = _matmul(a, ct, "tn", out_dtype=w.dtype, name=name + "_dw")
    else:
        da = _matmul(ct, w, "nn", out_dtype=a.dtype, name=name + "_da") if need_da else jnp.zeros_like(a)
        dw = _matmul(ct, a, "tn", out_dtype=w.dtype, name=name + "_dw")
    return da, dw


mm.defvjp(_mm_fwd, _mm_bwd)


def _pairs(bounds):
    return list(zip(bounds[:-1], bounds[1:]))


def _split(v, bounds):
    return [v[:, a:b] for a, b in _pairs(bounds)]


def stage(name, f, tiled, params=(), consts=(), splits=None, ts=256, out_dtypes=None):
    tiled, params, consts = tuple(tiled), tuple(params), tuple(consts)
    n_t, n_p, n_c = len(tiled), len(params), len(consts)
    s = tiled[0].shape[0]
    ts = min(ts, s)
    grid = (s // ts,)
    if splits is None:
        splits = [None] * n_t
    in_bounds = [(0, t.shape[1]) if b is None else tuple(b) for t, b in zip(tiled, splits)]

    def tile_aval(arr):
        return jax.ShapeDtypeStruct((ts, arr.shape[1]), arr.dtype)

    slab_avals = [[jax.ShapeDtypeStruct((ts, e - a), t.dtype) for a, e in _pairs(b)]
                  for t, b in zip(tiled, in_bounds)]
    out_avals = jax.eval_shape(f, slab_avals, list(params), [tile_aval(c) for c in consts])
    out_bounds = [tuple(np.cumsum([0] + [o.shape[1] for o in slabs]).tolist()) for slabs in out_avals]
    out_dtypes = [F32] * len(out_bounds) if out_dtypes is None else out_dtypes
    out_shapes = [jax.ShapeDtypeStruct((s, b[-1]), d) for b, d in zip(out_bounds, out_dtypes)]

    def row_spec(width):
        return pl.BlockSpec((ts, width), lambda i: (i, 0))

    def par_spec(arr):
        return pl.BlockSpec(arr.shape, lambda i: (0, 0))

    in_specs = ([row_spec(t.shape[1]) for t in tiled] + [par_spec(p) for p in params]
                + [row_spec(c.shape[1]) for c in consts])

    def load(refs):
        t = [_split(r[...], b) for r, b in zip(refs[:n_t], in_bounds)]
        p = [r[...] for r in refs[n_t:n_t + n_p]]
        c = [r[...] for r in refs[n_t + n_p:n_t + n_p + n_c]]
        return t, p, c

    def store(refs, values, bounds):
        for ref, slabs, b in zip(refs, values, bounds):
            for v, (a, e) in zip(slabs, _pairs(b)):
                ref[:, a:e] = v.astype(ref.dtype)

    def run_fwd(tiled, params, consts):
        def body(*refs):
            t, p, c = load(refs)
            store(refs[n_t + n_p + n_c:], f(t, p, c), out_bounds)

        return pl.pallas_call(
            body, name=name + "_fwd", grid=grid, in_specs=in_specs,
            out_specs=[row_spec(b[-1]) for b in out_bounds], out_shape=out_shapes,
            compiler_params=_params("parallel"),
        )(*tiled, *params, *consts)

    def run_bwd(tiled, params, consts, cts):
        n_in = n_t + n_p + n_c
        n_o = len(out_bounds)

        def body(*refs):
            t, p, c = load(refs)
            g = [_split(r[...].astype(F32), b) for r, b in zip(refs[n_in:n_in + n_o], out_bounds)]
            _, pull = jax.vjp(lambda t_, p_: f(t_, p_, c), t, p)
            dt, dp = pull(g)
            store(refs[n_in + n_o:n_in + n_o + n_t], dt, in_bounds)
            first = pl.program_id(0) == 0
            for ref, d in zip(refs[n_in + n_o + n_t:], dp):
                @pl.when(first)
                def _(ref=ref):
                    ref[...] = jnp.zeros_like(ref)

                ref[...] += d

        res = pl.pallas_call(
            body, name=name + "_bwd", grid=grid,
            in_specs=in_specs + [row_spec(b[-1]) for b in out_bounds],
            out_specs=[row_spec(t.shape[1]) for t in tiled] + [par_spec(p) for p in params],
            out_shape=[jax.ShapeDtypeStruct(t.shape, t.dtype) for t in tiled]
                      + [jax.ShapeDtypeStruct(p.shape, F32) for p in params],
            compiler_params=_params("arbitrary"),
        )(*tiled, *params, *consts, *cts)
        return tuple(res[:n_t]), tuple(res[n_t:])

    @jax.custom_vjp
    def op(tiled, params, consts):
        return tuple(run_fwd(tiled, params, consts))

    def op_fwd(tiled, params, consts):
        return op(tiled, params, consts), (tiled, params, consts)

    def op_bwd(res, cts):
        tiled, params, consts = res
        dt, dp = run_bwd(tiled, params, consts, cts)
        return dt, dp, tuple(jnp.zeros_like(c) for c in consts)

    op.defvjp(op_fwd, op_bwd)
    return op(tiled, params, consts)


def _rms(t, g):
    return t * lax.rsqrt(jnp.mean(t * t, axis=-1, keepdims=True) + RMS_EPS) * g


def _rope(t1, t2, c, s):
    return t1 * c - t2 * s, t2 * c + t1 * s


def _f_prenorm(t, p, c):
    return [[_rms(t[0][0], p[0])]]


def _f_prep(t, p, c):
    qa1, qa2, ka1, ka2, va, cq, ckv, kr1, kr2 = t[0]
    ca, sa, cb, sb = c
    ca2, sa2 = jnp.concatenate([ca, ca], axis=1), jnp.concatenate([sa, sa], axis=1)
    return [list(_rope(qa1, qa2, ca2, sa2)), list(_rope(ka1, ka2, ca, sa)), [va],
            [_rms(cq, p[0])], [_rms(ckv, p[1])], list(_rope(kr1, kr2, cb, sb))]


def _f_qrope(t, p, c):
    nope, pe1, pe2 = t[0]
    return [[nope] + list(_rope(pe1, pe2, c[0], c[1]))]


def _f_gate(t, p, c):
    (ga, gb), (pa,), (pb,) = t
    ba, bb = p
    return [[jax.nn.sigmoid(ga + ba) * pa + jax.nn.sigmoid(gb + bb) * pb]]


def _f_post(t, p, c):
    x1 = t[0][0] + _rms(t[1][0], p[0])
    return [[x1], [_rms(x1, p[1])]]


def _f_glu(t, p, c):
    return [[jax.nn.gelu(t[0][0], approximate=True) * t[1][0]]]


def _f_out(t, p, c):
    y = t[0][0] + jax.nn.sigmoid(t[1][0]) * t[2][0]
    err = y - c[0]
    return [[0.5 * jnp.mean(err * err, axis=-1, keepdims=True)]]


def _shift_down(cur, prev, has_prev):
    rows = cur.shape[0]
    row = lax.broadcasted_iota(jnp.int32, cur.shape, 0)
    m1 = prev[7:8, :] * has_prev
    m2 = prev[6:7, :] * has_prev
    u1 = jnp.where(row >= 1, pltpu.roll(cur, 1, 0), m1)
    u2 = jnp.where(row >= 2, pltpu.roll(cur, 2, 0), jnp.where(row == 1, m1, m2))
    return u1, u2


def _shift_up(cur, nxt, has_next):
    rows = cur.shape[0]
    row = lax.broadcasted_iota(jnp.int32, cur.shape, 0)
    n0 = nxt[0:1, :] * has_next
    n1 = nxt[1:2, :] * has_next
    d1 = jnp.where(row < rows - 1, pltpu.roll(cur, rows - 1, 0), n0)
    d2 = jnp.where(row < rows - 2, pltpu.roll(cur, rows - 2, 0), jnp.where(row == rows - 2, n0, n1))
    return d1, d2


def _conv_tiles(s, ch):
    ts = min(CONV_TS, s)
    tc = _pick(ch, CONV_TC, 128)
    return ts, tc, s // ts, ch // tc


def _conv_fwd_call(up, w, b, name):
    s, ch = up.shape
    ts, tc, nt, nc = _conv_tiles(s, ch)
    hb = ts // 8

    def body(cur_ref, prev_ref, w_ref, b_ref, o_ref):
        cur = cur_ref[...]
        u1, u2 = _shift_down(cur, prev_ref[...], (pl.program_id(1) > 0).astype(F32))
        o_ref[...] = w_ref[2:3, :] * cur + w_ref[1:2, :] * u1 + w_ref[0:1, :] * u2 + b_ref[...]

    return pl.pallas_call(
        body, name=name + "_fwd", grid=(nc, nt),
        in_specs=[pl.BlockSpec((ts, tc), lambda c, i: (i, c)),
                  pl.BlockSpec((8, tc), lambda c, i: (jnp.maximum(i * hb - 1, 0), c)),
                  pl.BlockSpec((CONV_W, tc), lambda c, i: (0, c)),
                  pl.BlockSpec((1, tc), lambda c, i: (0, c))],
        out_specs=pl.BlockSpec((ts, tc), lambda c, i: (i, c)),
        out_shape=jax.ShapeDtypeStruct((s, ch), F32),
        compiler_params=_params("parallel", "parallel"),
    )(up, up, w, b)


def _conv_bwd_call(up, w, du, name):
    s, ch = up.shape
    ts, tc, nt, nc = _conv_tiles(s, ch)
    hb = ts // 8

    def body(cur_ref, prev_ref, w_ref, du_ref, nxt_ref, dup_ref, dw_ref, db_ref):
        i = pl.program_id(1)
        cur, du = cur_ref[...], du_ref[...]
        u1, u2 = _shift_down(cur, prev_ref[...], (i > 0).astype(F32))
        d1, d2 = _shift_up(du, nxt_ref[...], (i < nt - 1).astype(F32))
        dup_ref[...] = w_ref[2:3, :] * du + w_ref[1:2, :] * d1 + w_ref[0:1, :] * d2

        @pl.when(i == 0)
        def _():
            dw_ref[...] = jnp.zeros_like(dw_ref)
            db_ref[...] = jnp.zeros_like(db_ref)

        dw_ref[0:1, :] += jnp.sum(du * u2, axis=0, keepdims=True)
        dw_ref[1:2, :] += jnp.sum(du * u1, axis=0, keepdims=True)
        dw_ref[2:3, :] += jnp.sum(du * cur, axis=0, keepdims=True)
        db_ref[...] += jnp.sum(du, axis=0, keepdims=True)

    return pl.pallas_call(
        body, name=name + "_bwd", grid=(nc, nt),
        in_specs=[pl.BlockSpec((ts, tc), lambda c, i: (i, c)),
                  pl.BlockSpec((8, tc), lambda c, i: (jnp.maximum(i * hb - 1, 0), c)),
                  pl.BlockSpec((CONV_W, tc), lambda c, i: (0, c)),
                  pl.BlockSpec((ts, tc), lambda c, i: (i, c)),
                  pl.BlockSpec((8, tc), lambda c, i: (jnp.minimum((i + 1) * hb, s // 8 - 1), c))],
        out_specs=[pl.BlockSpec((ts, tc), lambda c, i: (i, c)),
                   pl.BlockSpec((CONV_W, tc), lambda c, i: (0, c)),
                   pl.BlockSpec((1, tc), lambda c, i: (0, c))],
        out_shape=[jax.ShapeDtypeStruct((s, ch), F32), jax.ShapeDtypeStruct((CONV_W, ch), F32),
                   jax.ShapeDtypeStruct((1, ch), F32)],
        compiler_params=_params("parallel", "arbitrary"),
    )(up, up, w, du, du)


@functools.partial(jax.custom_vjp, nondiff_argnums=(3,))
def dwconv(up, w, b, name):
    return _conv_fwd_call(up, w, b, name)


def _dwconv_fwd(up, w, b, name):
    return _conv_fwd_call(up, w, b, name), (up, w)


def _dwconv_bwd(name, res, ct):
    up, w = res
    return tuple(_conv_bwd_call(up, w, ct, name))


dwconv.defvjp(_dwconv_fwd, _dwconv_bwd)


SWA_ROWS = A_GROUP * SWA_BLOCK


def _swa_stack(ref):
    return ref[...].reshape(SWA_ROWS, ref.shape[-1])


def _swa_sink_rows(sink_ref, g):
    return jnp.concatenate([jnp.full((SWA_BLOCK, 1), sink_ref[g * A_GROUP + h], F32) for h in range(A_GROUP)], axis=0)


def _swa_probs(q, kp, kc, sink, prev_off):
    scale = A_HEAD_DIM ** -0.5
    nt = (((1,), (1,)), ((), ()))
    sp = lax.dot_general(q, kp, nt, preferred_element_type=F32) * scale
    sc = lax.dot_general(q, kc, nt, preferred_element_type=F32) * scale
    qi = lax.broadcasted_iota(jnp.int32, sp.shape, 0) & (SWA_BLOCK - 1)
    kj = lax.broadcasted_iota(jnp.int32, sp.shape, 1)
    sp = jnp.where(kj > qi + prev_off, sp, -jnp.inf)
    sc = jnp.where(kj <= qi, sc, -jnp.inf)
    m = jnp.maximum(jnp.maximum(jnp.max(sp, axis=-1, keepdims=True), jnp.max(sc, axis=-1, keepdims=True)), sink)
    ep, ec, es = jnp.exp(sp - m), jnp.exp(sc - m), jnp.exp(sink - m)
    den = jnp.sum(ep, axis=-1, keepdims=True) + jnp.sum(ec, axis=-1, keepdims=True) + es
    return ep / den, ec / den, es / den


def _swa_specs(s):
    blk = SWA_BLOCK
    q_spec = pl.BlockSpec((None, A_GROUP, blk, A_HEAD_DIM), lambda g, n: (g, 0, n, 0))
    prev_spec = pl.BlockSpec((None, blk, A_HEAD_DIM), lambda g, n: (g, jnp.maximum(n - 1, 0), 0))
    cur_spec = pl.BlockSpec((None, blk, A_HEAD_DIM), lambda g, n: (g, n, 0))
    sink_spec = pl.BlockSpec(memory_space=pltpu.SMEM)
    return q_spec, prev_spec, cur_spec, sink_spec


def _swa_fwd_call(q, k, v, sinks):
    s = q.shape[2]
    q_spec, prev_spec, cur_spec, sink_spec = _swa_specs(s)

    def body(q_ref, kp_ref, kc_ref, vp_ref, vc_ref, sink_ref, o_ref):
        g, n = pl.program_id(0), pl.program_id(1)
        prev_off = jnp.where(n > 0, 0, SWA_BLOCK)
        kp, kc = kp_ref[...].astype(BF16), kc_ref[...].astype(BF16)
        vp, vc = vp_ref[...].astype(BF16), vc_ref[...].astype(BF16)
        pp, pc, _ = _swa_probs(_swa_stack(q_ref).astype(BF16), kp, kc, _swa_sink_rows(sink_ref, g), prev_off)
        out = (jnp.dot(pp.astype(BF16), vp, preferred_element_type=F32)
               + jnp.dot(pc.astype(BF16), vc, preferred_element_type=F32))
        o_ref[...] = out.reshape(o_ref.shape)

    return pl.pallas_call(
        body, name="swa_fwd", grid=(A_KV_HEADS, s // SWA_BLOCK),
        in_specs=[q_spec, prev_spec, cur_spec, prev_spec, cur_spec, sink_spec],
        out_specs=q_spec, out_shape=jax.ShapeDtypeStruct(q.shape, F32),
        compiler_params=_params("parallel", "parallel"),
    )(q, k, k, v, v, sinks)


def _swa_bwd_call(q, k, v, sinks, do):
    s = q.shape[2]
    q_spec, prev_spec, cur_spec, sink_spec = _swa_specs(s)
    scale = A_HEAD_DIM ** -0.5
    tn = (((0,), (0,)), ((), ()))
    nt = (((1,), (1,)), ((), ()))
    dsink_spec = pl.BlockSpec((None, A_GROUP, SWA_BLOCK, 1), lambda g, n: (g, 0, 0, 0))

    def body(q_ref, kp_ref, kc_ref, vp_ref, vc_ref, sink_ref, do_ref,
             dq_ref, dkp_ref, dkc_ref, dvp_ref, dvc_ref, dsink_ref):
        g, n = pl.program_id(0), pl.program_id(1)
        prev_off = jnp.where(n > 0, 0, SWA_BLOCK)
        kp, kc = kp_ref[...].astype(BF16), kc_ref[...].astype(BF16)
        vp, vc = vp_ref[...].astype(BF16), vc_ref[...].astype(BF16)

        @pl.when(n == 0)
        def _():
            dsink_ref[...] = jnp.zeros_like(dsink_ref)

        qb = _swa_stack(q_ref).astype(BF16)
        pp, pc, ps = _swa_probs(qb, kp, kc, _swa_sink_rows(sink_ref, g), prev_off)
        ppb, pcb = pp.astype(BF16), pc.astype(BF16)
        out = jnp.dot(ppb, vp, preferred_element_type=F32) + jnp.dot(pcb, vc, preferred_element_type=F32)
        dof = _swa_stack(do_ref)
        dob = dof.astype(BF16)
        delta = jnp.sum(dof * out, axis=-1, keepdims=True)
        dsp = (pp * (lax.dot_general(dob, vp, nt, preferred_element_type=F32) - delta)).astype(BF16)
        dsc = (pc * (lax.dot_general(dob, vc, nt, preferred_element_type=F32) - delta)).astype(BF16)
        dsink_ref[...] += (-ps * delta).reshape(dsink_ref.shape)
        dq = (jnp.dot(dsp, kp, preferred_element_type=F32) + jnp.dot(dsc, kc, preferred_element_type=F32)) * scale
        dq_ref[...] = dq.reshape(dq_ref.shape)
        dkp_ref[...] = lax.dot_general(dsp, qb, tn, preferred_element_type=F32) * scale
        dkc_ref[...] = lax.dot_general(dsc, qb, tn, preferred_element_type=F32) * scale
        dvp_ref[...] = lax.dot_general(ppb, dob, tn, preferred_element_type=F32)
        dvc_ref[...] = lax.dot_general(pcb, dob, tn, preferred_element_type=F32)

    kv_shape = jax.ShapeDtypeStruct(k.shape, F32)
    return pl.pallas_call(
        body, name="swa_bwd", grid=(A_KV_HEADS, s // SWA_BLOCK),
        in_specs=[q_spec, prev_spec, cur_spec, prev_spec, cur_spec, sink_spec, q_spec],
        out_specs=[q_spec, cur_spec, cur_spec, cur_spec, cur_spec, dsink_spec],
        out_shape=[jax.ShapeDtypeStruct(q.shape, F32), kv_shape, kv_shape, kv_shape, kv_shape,
                   jax.ShapeDtypeStruct((A_KV_HEADS, A_GROUP, SWA_BLOCK, 1), F32)],
        compiler_params=_params("parallel", "arbitrary"),
    )(q, k, k, v, v, sinks, do)


@jax.custom_vjp
def swa(q, k, v, sinks):
    return _swa_fwd_call(q, k, v, sinks)


def _swa_fwd(q, k, v, sinks):
    return _swa_fwd_call(q, k, v, sinks), (q, k, v, sinks)


def _swa_bwd(res, do):
    q, k, v, sinks = res
    dq, dkp, dkc, dvp, dvc, dsink = _swa_bwd_call(q, k, v, sinks, do)

    def fold(prev_part, cur_part):
        shifted = jnp.concatenate([prev_part[:, SWA_BLOCK:], jnp.zeros_like(prev_part[:, :SWA_BLOCK])], axis=1)
        return cur_part + shifted

    return dq, fold(dkp, dkc), fold(dvp, dvc), jnp.sum(dsink, axis=(2, 3)).reshape(-1)


swa.defvjp(_swa_fwd, _swa_bwd)


MLA_SCALE = (NOPE_DIM + ROPE_DIM) ** -0.5


EXP2_SCALE = MLA_SCALE * float(np.log2(np.e))
NT_DIMS = (((1,), (1,)), ((), ()))
TN_DIMS = (((0,), (0,)), ((), ()))


def _flash_fwd_call(q, k, v_t):
    h, s, d = q.shape
    t = min(FLASH_T, s)
    nb = s // t

    def body(q_ref, k_ref, vt_ref, ot_ref, lse_ref, m_ref, l_ref, acc_ref):
        i = pl.program_id(1)
        qb = q_ref[...]
        m_ref[...] = jnp.full_like(m_ref, -jnp.inf)
        l_ref[...] = jnp.zeros_like(l_ref)
        acc_ref[...] = jnp.zeros_like(acc_ref)

        def step(j, on_diagonal):
            keys = pl.ds(pl.multiple_of(j * t, t), t)
            sc_t = lax.dot_general(k_ref[keys, :], qb, NT_DIMS, preferred_element_type=F32)
            if on_diagonal:
                key = lax.broadcasted_iota(jnp.int32, (t, t), 0)
                qry = lax.broadcasted_iota(jnp.int32, (t, t), 1)
                sc_t = jnp.where(qry >= key, sc_t, -jnp.inf)
            m_old = m_ref[...]
            m_new = jnp.maximum(m_old, jnp.max(sc_t, axis=0, keepdims=True))
            alpha = jnp.exp2((m_old - m_new) * EXP2_SCALE)
            p_t = jnp.exp2((sc_t - m_new) * EXP2_SCALE)
            l_ref[...] = alpha * l_ref[...] + jnp.sum(p_t, axis=0, keepdims=True)
            acc_ref[...] = alpha * acc_ref[...] + jnp.dot(vt_ref[:, keys], p_t.astype(BF16),
                                                          preferred_element_type=F32)
            m_ref[...] = m_new

        def below(j, carry):
            step(j, False)
            return carry

        lax.fori_loop(0, i, below, 0)
        step(i, True)
        ot_ref[...] = acc_ref[...] / l_ref[...]
        lse_ref[...] = m_ref[...] * EXP2_SCALE + jnp.log2(l_ref[...])

    return pl.pallas_call(
        body, name="mla_fwd", grid=(h, nb),
        in_specs=[pl.BlockSpec((None, t, d), lambda hh, i: (hh, i, 0)),
                  pl.BlockSpec((None, s, d), lambda hh, i: (hh, 0, 0)),
                  pl.BlockSpec((None, d, s), lambda hh, i: (hh, 0, 0))],
        out_specs=[pl.BlockSpec((None, d, t), lambda hh, i: (hh, 0, i)),
                   pl.BlockSpec((None, 1, t), lambda hh, i: (hh, 0, i))],
        out_shape=[jax.ShapeDtypeStruct((h, d, s), F32), jax.ShapeDtypeStruct((h, 1, s), F32)],
        scratch_shapes=[pltpu.VMEM((1, t), F32), pltpu.VMEM((1, t), F32), pltpu.VMEM((d, t), F32)],
        compiler_params=_params("parallel", "arbitrary"),
    )(q, k, v_t)


def _flash_delta_call(o_t, do_t):
    h, d, s = o_t.shape
    t = min(FLASH_T, s)

    def body(o_ref, do_ref, out_ref, dob_ref):
        do = do_ref[...]
        out_ref[...] = jnp.sum(o_ref[...] * do, axis=0, keepdims=True)
        dob_ref[...] = do.astype(BF16)

    spec = pl.BlockSpec((None, d, t), lambda hh, i: (hh, 0, i))
    return pl.pallas_call(
        body, name="mla_delta", grid=(h, s // t), in_specs=[spec, spec],
        out_specs=[pl.BlockSpec((None, 1, t), lambda hh, i: (hh, 0, i)), spec],
        out_shape=[jax.ShapeDtypeStruct((h, 1, s), F32), jax.ShapeDtypeStruct((h, d, s), BF16)],
        compiler_params=_params("parallel", "parallel"),
    )(o_t, do_t)


def _flash_bwd_call(q, k, v, lse_row, delta_row, do_t):
    h, s, d = q.shape
    t = min(FLASH_T, s)
    nb = s // t

    def body(q_ref, k_ref, v_ref, lse_ref, delta_ref, dot_ref, dq_ref, dk_ref, dv_ref, dq_acc, dk_acc, dv_acc):
        j = pl.program_id(1)

        @pl.when(j == 0)
        def _():
            dq_acc[...] = jnp.zeros_like(dq_acc)

        kb, vb = k_ref[...], v_ref[...]
        dk_acc[...] = jnp.zeros_like(dk_acc)
        dv_acc[...] = jnp.zeros_like(dv_acc)

        def step(i, on_diagonal):
            rows = pl.ds(pl.multiple_of(i * t, t), t)
            qb, dob_t = q_ref[rows, :], dot_ref[:, rows]
            sc_t = lax.dot_general(kb, qb, NT_DIMS, preferred_element_type=F32)
            p_t = jnp.exp2(sc_t * EXP2_SCALE - lse_ref[:, rows])
            if on_diagonal:
                key = lax.broadcasted_iota(jnp.int32, (t, t), 0)
                qry = lax.broadcasted_iota(jnp.int32, (t, t), 1)
                p_t = jnp.where(qry >= key, p_t, 0.0)
            dp_t = jnp.dot(vb, dob_t, preferred_element_type=F32)
            ds_t = (p_t * (dp_t - delta_ref[:, rows])).astype(BF16)
            dv_acc[...] += lax.dot_general(p_t.astype(BF16), dob_t, NT_DIMS, preferred_element_type=F32)
            dk_acc[...] += jnp.dot(ds_t, qb, preferred_element_type=F32)
            dq_acc[rows, :] += lax.dot_general(ds_t, kb, TN_DIMS, preferred_element_type=F32)

        def above(i, carry):
            step(i, False)
            return carry

        step(j, True)
        lax.fori_loop(j + 1, nb, above, 0)
        dk_ref[...] = (dk_acc[...] * MLA_SCALE).astype(dk_ref.dtype)
        dv_ref[...] = dv_acc[...].astype(dv_ref.dtype)

        @pl.when(j == nb - 1)
        def _():
            dq_ref[...] = (dq_acc[...] * MLA_SCALE).astype(dq_ref.dtype)

    full_spec = pl.BlockSpec((None, s, d), lambda hh, j: (hh, 0, 0))
    tile_spec = pl.BlockSpec((None, t, d), lambda hh, j: (hh, j, 0))
    row_spec = pl.BlockSpec((None, 1, s), lambda hh, j: (hh, 0, 0))
    return pl.pallas_call(
        body, name="mla_bwd", grid=(h, nb),
        in_specs=[full_spec, tile_spec, tile_spec, row_spec, row_spec,
                  pl.BlockSpec((None, d, s), lambda hh, j: (hh, 0, 0))],
        out_specs=[full_spec, tile_spec, tile_spec],
        out_shape=[jax.ShapeDtypeStruct((h, s, d), q.dtype)] * 3,
        scratch_shapes=[pltpu.VMEM((s, d), F32), pltpu.VMEM((t, d), F32), pltpu.VMEM((t, d), F32)],
        compiler_params=_params("parallel", "arbitrary"),
    )(q, k, v, lse_row, delta_row, do_t)


@jax.custom_vjp
def flash(q, k, v):
    return _flash_fwd_call(q, k, v.transpose(0, 2, 1))[0]


def _flash_fwd(q, k, v):
    o_t, lse = _flash_fwd_call(q, k, v.transpose(0, 2, 1))
    return o_t, (q, k, v, o_t, lse)


def _flash_bwd(res, do_t):
    q, k, v, o_t, lse = res
    delta, dob_t = _flash_delta_call(o_t, do_t)
    return tuple(_flash_bwd_call(q, k, v, lse, delta, dob_t))


flash.defvjp(_flash_fwd, _flash_bwd)


HBM_SPEC = pl.BlockSpec(memory_space=pltpu.HBM)


def _allgather(shards, name):
    n_arr = len(shards)

    def body(*refs):
        x_refs, out_refs = refs[:n_arr], refs[n_arr:2 * n_arr]
        send_sems, recv_sems, local_sems = refs[2 * n_arr:]
        x, y, c = lax.axis_index("x"), lax.axis_index("y"), lax.axis_index("c")
        me, sibling = (x, y, c), (x, y, 1 - c)
        chips = [(1 - x, y), (x, 1 - y), (1 - x, 1 - y)]
        arrays = range(n_arr)

        def rows(a, px, py, pc):
            return out_refs[a].at[4 * px + 2 * py + pc]

        def copy(a, k, block, to, src=None):
            return pltpu.make_async_remote_copy(
                src_ref=rows(a, *block) if src is None else src, dst_ref=rows(a, *block),
                send_sem=send_sems.at[k, a], recv_sem=recv_sems.at[k, a], device_id=to, device_id_type=MESH_ID)

        mine = [pltpu.make_async_copy(x_refs[a], rows(a, *me), local_sems.at[a]) for a in arrays]
        for cp in mine:
            cp.start()
        first = []
        for a in arrays:
            first.append(copy(a, 0, me, sibling, src=x_refs[a]))
            first += [copy(a, 1 + j, me, (*chip, c), src=x_refs[a]) for j, chip in enumerate(chips)]
        for cp in first:
            cp.start()
        passed = []
        for j, chip in enumerate(chips):
            for a in arrays:
                copy(a, 1 + j, (*chip, c), me).wait_recv()
                passed.append(copy(a, 4 + j, (*chip, c), sibling))
                passed[-1].start()
        for a in arrays:
            copy(a, 0, sibling, me).wait_recv()
        for j, chip in enumerate(chips):
            for a in arrays:
                copy(a, 4 + j, (*chip, 1 - c), me).wait_recv()
        for cp in first + passed:
            cp.wait_send()
        for cp in mine:
            cp.wait()

    return pl.pallas_call(
        body, name=name, out_shape=[jax.ShapeDtypeStruct((N_DEV,) + s.shape, s.dtype) for s in shards],
        in_specs=[HBM_SPEC] * n_arr, out_specs=[HBM_SPEC] * n_arr,
        scratch_shapes=[pltpu.SemaphoreType.DMA((7, n_arr)), pltpu.SemaphoreType.DMA((7, n_arr)),
                        pltpu.SemaphoreType.DMA((n_arr,))],
    )(*shards)


N_CHIP = 4


def _exchange_sibling(parts, name):
    n_arr = len(parts)

    def body(*refs):
        in_refs, kept_refs, recv_refs = refs[:n_arr], refs[n_arr:2 * n_arr], refs[2 * n_arr:3 * n_arr]
        send_sems, recv_sems, local_sems = refs[3 * n_arr:]
        x, y, c = lax.axis_index("x"), lax.axis_index("y"), lax.axis_index("c")
        copies = []
        for a in range(n_arr):
            for q in range(N_CHIP):
                copies.append(pltpu.make_async_copy(in_refs[a].at[2 * q + c], kept_refs[a].at[q], local_sems.at[q, a]))
                copies.append(pltpu.make_async_remote_copy(
                    src_ref=in_refs[a].at[2 * q + 1 - c], dst_ref=recv_refs[a].at[q],
                    send_sem=send_sems.at[q, a], recv_sem=recv_sems.at[q, a],
                    device_id=(x, y, 1 - c), device_id_type=MESH_ID))
        for cp in copies:
            cp.start()
        for cp in copies:
            cp.wait()

    half = [jax.ShapeDtypeStruct((N_CHIP,) + p.shape[1:], p.dtype) for p in parts]
    out = pl.pallas_call(
        body, name=name, out_shape=half + half,
        in_specs=[HBM_SPEC] * n_arr, out_specs=[HBM_SPEC] * (2 * n_arr),
        scratch_shapes=[pltpu.SemaphoreType.DMA((N_CHIP, n_arr)), pltpu.SemaphoreType.DMA((N_CHIP, n_arr)),
                        pltpu.SemaphoreType.DMA((N_CHIP, n_arr))],
    )(*parts)
    return out[:n_arr], out[n_arr:]


def _exchange_chips(parts, name):
    n_arr = len(parts)

    def body(*refs):
        in_refs, out_refs = refs[:n_arr], refs[n_arr:2 * n_arr]
        send_sems, recv_sems, local_sems = refs[2 * n_arr:]
        x, y, c = lax.axis_index("x"), lax.axis_index("y"), lax.axis_index("c")
        me = 2 * x + y
        copies = [pltpu.make_async_copy(in_refs[a].at[me], out_refs[a].at[me], local_sems.at[a]) for a in range(n_arr)]
        for k in range(1, N_CHIP):
            px = 1 - x if k & 2 else x
            py = 1 - y if k & 1 else y
            for a in range(n_arr):
                copies.append(pltpu.make_async_remote_copy(
                    src_ref=in_refs[a].at[2 * px + py], dst_ref=out_refs[a].at[me],
                    send_sem=send_sems.at[k - 1, a], recv_sem=recv_sems.at[k - 1, a],
                    device_id=(px, py, c), device_id_type=MESH_ID))
        for cp in copies:
            cp.start()
        for cp in copies:
            cp.wait()

    return pl.pallas_call(
        body, name=name, out_shape=[jax.ShapeDtypeStruct(p.shape, p.dtype) for p in parts],
        in_specs=[HBM_SPEC] * n_arr, out_specs=[HBM_SPEC] * n_arr,
        scratch_shapes=[pltpu.SemaphoreType.DMA((N_CHIP - 1, n_arr)), pltpu.SemaphoreType.DMA((N_CHIP - 1, n_arr)),
                        pltpu.SemaphoreType.DMA((n_arr,))],
    )(*parts)


def _row_tile(r, ccols, blocks):
    cap = max(16, (2 * 1024 * 1024) // (4 * ccols * blocks))
    return _pick(r, cap, 16)


def _pair_add(a, b, name):
    nb, r, ccols = a.shape
    tr = _row_tile(r, ccols, 1)

    def body(a_ref, b_ref, o_ref):
        o_ref[...] = (a_ref[...].astype(F32) + b_ref[...].astype(F32)).astype(o_ref.dtype)

    spec = pl.BlockSpec((None, tr, ccols), lambda q, i: (q, i, 0))
    return pl.pallas_call(
        body, name=name, grid=(nb, r // tr), in_specs=[spec, spec], out_specs=spec,
        out_shape=jax.ShapeDtypeStruct(a.shape, a.dtype), compiler_params=_params("parallel", "parallel"),
    )(a, b)


def _sum_blocks(parts, name):
    nb, r, ccols = parts.shape
    tr = _row_tile(r, ccols, nb)

    def body(p_ref, o_ref):
        acc = p_ref[0].astype(F32)
        for i in range(1, nb):
            acc = acc + p_ref[i].astype(F32)
        o_ref[...] = acc

    return pl.pallas_call(
        body, name=name, grid=(r // tr,),
        in_specs=[pl.BlockSpec((nb, tr, ccols), lambda i: (0, i, 0))],
        out_specs=pl.BlockSpec((tr, ccols), lambda i: (i, 0)),
        out_shape=jax.ShapeDtypeStruct((r, ccols), F32),
        compiler_params=_params("parallel"),
    )(parts)


def _gather_wire(shards, wire_dtypes):
    return tuple(_allgather([s.astype(d) for s, d in zip(shards, wire_dtypes)], "weights_allgather"))


@functools.partial(jax.custom_vjp, nondiff_argnums=(1,))
def fsdp_gather(shards, wire_dtypes):
    return _gather_wire(shards, wire_dtypes)


def _fsdp_gather_fwd(shards, wire_dtypes):
    return _gather_wire(shards, wire_dtypes), None


def _fsdp_gather_bwd(wire_dtypes, _, cts):
    kept, received = _exchange_sibling(list(cts), "grads_exchange_sibling")
    pair_sums = [_pair_add(k, r, "grad_pair_add_%d" % i) for i, (k, r) in enumerate(zip(kept, received))]
    chip_parts = _exchange_chips(pair_sums, "grads_exchange_chips")
    return (tuple(_sum_blocks(r, "grad_sum_%d" % i) for i, r in enumerate(chip_parts)),)


fsdp_gather.defvjp(_fsdp_gather_fwd, _fsdp_gather_bwd)


@jax.custom_vjp
def replicated(vec):
    return vec


def _replicated_fwd(vec):
    return vec, None


def _replicated_bwd(_, ct):
    return (_sum_blocks(_allgather([ct], "small_grad_allgather")[0], "small_grad_sum"),)


replicated.defvjp(_replicated_fwd, _replicated_bwd)


def _adamw(w, g, m, v, name):
    rows, cols = w.shape
    tr = _pick(rows, 256, 8) if rows % 8 == 0 else rows

    def body(w_ref, g_ref, m_ref, v_ref, d_ref, nm_ref, nv_ref):
        g_ = g_ref[...]
        m_ = ADAM_B1 * m_ref[...] + (1.0 - ADAM_B1) * g_
        v_ = ADAM_B2 * v_ref[...] + (1.0 - ADAM_B2) * jnp.square(g_)
        m_hat = m_ / (1.0 - ADAM_B1 ** ADAM_STEP)
        v_hat = v_ / (1.0 - ADAM_B2 ** ADAM_STEP)
        d_ref[...] = -ADAM_LR * (m_hat / (jnp.sqrt(v_hat) + ADAM_EPS) + ADAM_WD * w_ref[...])
        nm_ref[...] = m_
        nv_ref[...] = v_

    spec = pl.BlockSpec((tr, cols), lambda i: (i, 0))
    return pl.pallas_call(
        body, name=name, grid=(rows // tr,), in_specs=[spec] * 4, out_specs=[spec] * 3,
        out_shape=[jax.ShapeDtypeStruct(w.shape, F32)] * 3, compiler_params=_params("parallel"),
    )(w, g, m, v)


COL_SHARDED = ("w_in", "w_uq", "w_ukv", "w_branch_a", "w_branch_b", "w_up", "w_ple")
ROW_SHARDED = ("w_out", "w_down", "w_ple_gate")
BIG = ("w_in", "w_uq", "w_ukv", "w_branch_a", "w_branch_b", "w_out", "w_up", "w_down", "w_ple_gate", "w_ple")
SMALL = ("attn_pre_norm", "attn_post_norm", "b_gate", "q_a_norm", "kv_a_norm", "mlp_pre_norm", "mlp_post_norm",
         "conv_b", "ple_norm", "sinks")
SMALL_COLS = 128


def _pack_rows(arrays, cols, row_mult):
    flat = jnp.concatenate([a.reshape(-1) for a in arrays])
    pad = (-flat.shape[0]) % (cols * row_mult)
    return jnp.pad(flat, (0, pad)).reshape(-1, cols)


def _unpack_small(vec, shapes):
    flat = vec.reshape(-1)
    out, off = {}, 0
    for name in SMALL:
        n = shapes[name]
        out[name] = flat[off:off + n].reshape(1, n)
        off += n + (-n) % SMALL_COLS
    return out


def _pad_lanes(t, width):
    return jnp.pad(t, [(0, 0)] * (t.ndim - 1) + [(0, width - t.shape[-1])])


def _pad_rows(t, rows):
    return jnp.pad(t, [(0, 0)] * (t.ndim - 2) + [(0, rows - t.shape[-2]), (0, 0)])


def _arrange_w_in_t(wt):
    k = wt.shape[1]
    qa, ka, va, cq, ckv, kr, gates = jnp.split(wt, np.cumsum([512, 128, 128, 256, 128, 32]).tolist(), axis=0)
    qa = qa.reshape(A_HEADS, 2, 32, k).transpose(1, 0, 2, 3).reshape(512, k)
    ka = _pad_rows(ka.reshape(A_KV_HEADS, 2, 32, k).transpose(1, 0, 2, 3).reshape(2, 64, k), 128).reshape(256, k)
    kr = _pad_rows(kr.reshape(2, 16, k), 128).reshape(256, k)
    return jnp.concatenate([qa, ka, va, cq, ckv, kr], axis=0), gates


FRONT_BOUNDS = (0, 256, 512, 640, 768, 896, 1152, 1280, 1408, 1536)


def _arrange_w_uq_t(wt):
    k = wt.shape[1]
    w = wt.reshape(B_HEADS, NOPE_DIM + ROPE_DIM, k)
    return jnp.concatenate([w[:, :64].reshape(512, k), w[:, 64:80].reshape(128, k), w[:, 80:96].reshape(128, k)],
                           axis=0)


def _rope_tables(positions, s):
    pos = positions.reshape(s, 1).astype(F32)

    def table(dim, reps):
        inv = ROPE_THETA ** (-(jnp.arange(0, dim, 2, dtype=F32) / dim))
        ang = pos * inv
        return jnp.tile(jnp.cos(ang), (1, reps)), jnp.tile(jnp.sin(ang), (1, reps))

    ca, sa = table(A_HEAD_DIM, 4)
    cb, sb = table(ROPE_DIM, 8)
    return ca, sa, cb, sb


def _local_loss(wts, x, p, tables, target):
    s = x.shape[0]
    small_shapes = {n: wts[n].shape[-1] for n in SMALL}
    small_vec = _pack_rows([_pad_lanes(wts[n].reshape(1, -1), small_shapes[n] + (-small_shapes[n]) % SMALL_COLS)
                            for n in SMALL], SMALL_COLS, 8)
    sm = _unpack_small(replicated(small_vec), small_shapes)
    shards = [wts[n].T if n in COL_SHARDED else wts[n] for n in BIG] + [_pack_rows([wts["conv_w"]], SMALL_COLS, 8)]
    gathered = fsdp_gather(tuple(shards), (BF16,) * len(BIG) + (F32,))
    big = {n: g.reshape(-1, g.shape[2]) for n, g in zip(BIG, gathered)}
    ch = wts["conv_w"].shape[1]
    conv_w = gathered[-1].reshape(N_DEV, -1)[:, :CONV_W * ch].reshape(N_DEV, CONV_W, ch)
    conv_w = conv_w.transpose(1, 0, 2).reshape(CONV_W, N_DEV * ch)

    w_front_t, w_gates_t = _arrange_w_in_t(big["w_in"])
    ca, sa, cb, sb = tables

    (h1,) = stage("prenorm", _f_prenorm, [x], [sm["attn_pre_norm"]], out_dtypes=[BF16])
    zf = mm(h1, w_front_t, "nt", "w_front", True, F32)
    gates = mm(h1, w_gates_t, "nt", "w_gates", True, F32)
    qar, kar, va, cqn, ckvn, kpe = stage("prep", _f_prep, [zf], [sm["q_a_norm"], sm["kv_a_norm"]],
                                         [ca, sa, cb, sb], splits=[FRONT_BOUNDS],
                                         out_dtypes=[F32, F32, F32, BF16, BF16, F32])

    q_a = jnp.concatenate([qar[:, :256].reshape(s, A_HEADS, 32), qar[:, 256:].reshape(s, A_HEADS, 32)], axis=-1)
    q_a = q_a.reshape(s, A_KV_HEADS, A_GROUP, A_HEAD_DIM).transpose(1, 2, 0, 3)
    k_a = jnp.concatenate([kar[:, 0:64].reshape(s, A_KV_HEADS, 32), kar[:, 128:192].reshape(s, A_KV_HEADS, 32)],
                          axis=-1).transpose(1, 0, 2)
    v_a = va.reshape(s, A_KV_HEADS, A_HEAD_DIM).transpose(1, 0, 2)
    ya = swa(q_a, k_a, v_a, sm["sinks"].reshape(-1)).transpose(2, 0, 1, 3).reshape(s, A_HEADS * A_HEAD_DIM)
    ya = ya.astype(BF16)

    qb = mm(cqn, _arrange_w_uq_t(big["w_uq"]), "nt", "w_uq", True, F32)
    kvb = mm(ckvn, big["w_ukv"], "nt", "w_ukv", True, F32)
    (qbr,) = stage("qrope", _f_qrope, [qb], [], [cb, sb], splits=[(0, 512, 640, 768)])
    zeros32 = jnp.zeros((s, B_HEADS, HEAD_PAD - NOPE_DIM - ROPE_DIM), F32)
    q_b = jnp.concatenate([qbr[:, :512].reshape(s, B_HEADS, 64), qbr[:, 512:640].reshape(s, B_HEADS, 16),
                           qbr[:, 640:].reshape(s, B_HEADS, 16), zeros32], axis=-1).transpose(1, 0, 2)
    kv = kvb.reshape(s, B_HEADS, NOPE_DIM + V_DIM)
    k_b = jnp.concatenate([kv[:, :, :NOPE_DIM],
                           jnp.broadcast_to(kpe[:, None, 0:16], (s, B_HEADS, 16)),
                           jnp.broadcast_to(kpe[:, None, 128:144], (s, B_HEADS, 16)), zeros32],
                          axis=-1).transpose(1, 0, 2)
    v_b = _pad_lanes(kv[:, :, NOPE_DIM:], HEAD_PAD).transpose(1, 0, 2)
    yb_t = flash(q_b.astype(BF16), k_b.astype(BF16), v_b.astype(BF16))[:, :V_DIM, :]
    yb = yb_t.reshape(B_HEADS * V_DIM, s).T.astype(BF16)

    pa = mm(ya, big["w_branch_a"], "nt", "w_branch_a", True, F32)
    pb = mm(yb, big["w_branch_b"], "nt", "w_branch_b", True, F32)
    (mixed,) = stage("gate", _f_gate, [gates, pa, pb], [sm["b_gate"][:, :D_MODEL], sm["b_gate"][:, D_MODEL:]],
                     splits=[(0, D_MODEL, 2 * D_MODEL), None, None], out_dtypes=[BF16])
    o = mm(mixed, big["w_out"], "nn", "w_out", True, F32)
    x1, h2 = stage("post_attn", _f_post, [x, o], [sm["attn_post_norm"], sm["mlp_pre_norm"]], out_dtypes=[F32, BF16])

    up_g = mm(h2, big["w_up"][:D_FF], "nt", "w_up_gate", True, F32)
    up_v = mm(h2, big["w_up"][D_FF:], "nt", "w_up_val", True, F32)
    u_g = dwconv(up_g, conv_w[:, :D_FF], sm["conv_b"][:, :D_FF], "conv_gate")
    u_v = dwconv(up_v, conv_w[:, D_FF:], sm["conv_b"][:, D_FF:], "conv_val")
    (act,) = stage("glu", _f_glu, [u_g, u_v], ts=128, out_dtypes=[BF16])
    ff = mm(act, big["w_down"], "nn", "w_down", True, F32)
    x2, h3 = stage("post_mlp", _f_post, [x1, ff], [sm["mlp_post_norm"], sm["ple_norm"]], out_dtypes=[F32, BF16])

    t = mm(h3, big["w_ple_gate"], "nn", "w_ple_gate", True, F32)
    e = mm(p, big["w_ple"], "nt", "w_ple", False, F32)
    (rowloss,) = stage("loss", _f_out, [x2, t, e], [], [target])
    return jnp.sum(rowloss)


WEIGHTS = ["attn_pre_norm", "attn_post_norm", "w_in", "b_gate", "sinks", "q_a_norm", "w_uq", "kv_a_norm", "w_ukv",
           "w_branch_a", "w_branch_b", "w_out", "mlp_pre_norm", "mlp_post_norm", "w_up", "conv_w", "conv_b",
           "w_down", "ple_norm", "w_ple_gate", "w_ple"]


def kernel(x, p, positions, attn_pre_norm, attn_post_norm, w_in, b_gate, sinks, q_a_norm, w_uq, kv_a_norm, w_ukv, w_branch_a, w_branch_b, w_out, mlp_pre_norm, mlp_post_norm, w_up, conv_w, conv_b, w_down, ple_norm, w_ple_gate, w_ple, loss_target, m_attn_pre_norm, m_attn_post_norm, m_w_in, m_b_gate, m_sinks, m_q_a_norm, m_w_uq, m_kv_a_norm, m_w_ukv, m_w_branch_a, m_w_branch_b, m_w_out, m_mlp_pre_norm, m_mlp_post_norm, m_w_up, m_conv_w, m_conv_b, m_w_down, m_ple_norm, m_w_ple_gate, m_w_ple, v_attn_pre_norm, v_attn_post_norm, v_w_in, v_b_gate, v_sinks, v_q_a_norm, v_w_uq, v_kv_a_norm, v_w_ukv, v_w_branch_a, v_w_branch_b, v_w_out, v_mlp_pre_norm, v_mlp_post_norm, v_w_up, v_conv_w, v_conv_b, v_w_down, v_ple_norm, v_w_ple_gate, v_w_ple):
    given = dict(locals())
    s = x.shape[1]
    wts = {n: given[n][0] if given[n].ndim == 3 else given[n] for n in WEIGHTS}
    tables = _rope_tables(positions, s)
    local_loss, (grads, grad_x) = jax.value_and_grad(_local_loss, argnums=(0, 1))(
        wts, x[0], p[0, 0], tables, loss_target[0])
    loss = lax.psum(local_loss, AXES)

    outs = {"grad": [], "delta": [], "m": [], "v": []}
    for n in WEIGHTS:
        shape = given[n].shape
        w2 = wts[n].reshape(-1, shape[-1])
        g2 = grads[n].reshape(w2.shape)
        delta, new_m, new_v = _adamw(w2, g2, given["m_" + n].reshape(w2.shape), given["v_" + n].reshape(w2.shape),
                                     "adamw_" + n)
        outs["grad"].append(g2.reshape(shape))
        outs["delta"].append(delta.reshape(shape))
        outs["m"].append(new_m.reshape(shape))
        outs["v"].append(new_v.reshape(shape))
    return (loss, grad_x[None], *outs["grad"], *outs["delta"], *outs["m"], *outs["v"])
```

```python
import functools

import numpy as np
import jax
import jax.numpy as jnp
from jax import lax
from jax.experimental import pallas as pl
from jax.experimental.pallas import tpu as pltpu

F32 = jnp.float32
BF16 = jnp.bfloat16
MESH_ID = pl.DeviceIdType.MESH
AXES = ("x", "y", "c")
N_DEV = 8

D_MODEL = 1024
RMS_EPS = 1e-6
ROPE_THETA = 10000.0
SWA_BLOCK = 128
A_HEADS, A_KV_HEADS, A_HEAD_DIM = 8, 2, 64
A_GROUP = A_HEADS // A_KV_HEADS
B_HEADS, Q_LORA, KV_LORA, NOPE_DIM, ROPE_DIM, V_DIM = 8, 256, 128, 64, 32, 64
D_FF = 2816
CONV_W = 3
HEAD_PAD = 128

ADAM_LR, ADAM_B1, ADAM_B2, ADAM_EPS, ADAM_WD, ADAM_STEP = 0.001, 0.9, 0.999, 1e-08, 0.01, 10

VMEM_LIMIT = 48 * 1024 * 1024
MM_TM, MM_TN, MM_TK_TOKENS = 512, 1408, 1024
MM_VMEM_BUDGET = 36 * 1024 * 1024
FLASH_T = 512
CONV_TS, CONV_TC = 128, 2816


def _params(*sem):
    return pltpu.CompilerParams(dimension_semantics=sem, vmem_limit_bytes=VMEM_LIMIT)


def _pick(dim, cap, mult):
    best = None
    for t in range(mult, min(dim, cap) + 1, mult):
        if dim % t == 0:
            best = t
    return dim if best is None else best


def _divisors(dim, mult):
    return [t for t in range(mult, dim + 1, mult) if dim % t == 0] or [dim]


def _matmul_tiles(m, n, kdim, form, sizes):
    sa, sb, so = sizes
    tk = _pick(kdim, MM_TK_TOKENS, 128) if form == "tn" else kdim
    cap_m = MM_TN if form == "tn" else MM_TM
    best = None
    for tm in _divisors(m, 128):
        for tn in _divisors(n, 128):
            need = 2 * (tm * tk * sa + tk * tn * sb + tm * tn * so) + (tm * tn * 4 if tk != kdim else 0)
            if tm > cap_m or tn > MM_TN or need > MM_VMEM_BUDGET:
                continue
            if best is None or (tm * tn, tm) > (best[0] * best[1], best[0]):
                best = (tm, tn)
    return best[0], best[1], tk


def _matmul(a, b, form, *, out_dtype=F32, name):
    if form == "tn":
        (kdim, m), n = a.shape, b.shape[1]
    else:
        (m, kdim), n = a.shape, (b.shape[1] if form == "nn" else b.shape[0])
    sizes = (a.dtype.itemsize, b.dtype.itemsize, jnp.dtype(out_dtype).itemsize)
    tm, tn, tk = _matmul_tiles(m, n, kdim, form, sizes)
    nk = kdim // tk
    a_spec = (pl.BlockSpec((tk, tm), lambda i, j, k: (k, i)) if form == "tn"
              else pl.BlockSpec((tm, tk), lambda i, j, k: (i, k)))
    b_spec = (pl.BlockSpec((tn, tk), lambda i, j, k: (j, k)) if form == "nt"
              else pl.BlockSpec((tk, tn), lambda i, j, k: (k, j)))
    dims = (((0 if form == "tn" else 1,), (1 if form == "nt" else 0,)), ((), ()))

    def product(a_ref, b_ref):
        return lax.dot_general(a_ref[...].astype(BF16), b_ref[...].astype(BF16), dims, preferred_element_type=F32)

    if nk == 1:
        def body(a_ref, b_ref, o_ref):
            o_ref[...] = product(a_ref, b_ref).astype(o_ref.dtype)

        scratch = []
    else:
        def body(a_ref, b_ref, o_ref, acc_ref):
            k = pl.program_id(2)

            @pl.when(k == 0)
            def _():
                acc_ref[...] = jnp.zeros_like(acc_ref)

            acc_ref[...] += product(a_ref, b_ref)

            @pl.when(k == nk - 1)
            def _():
                o_ref[...] = acc_ref[...].astype(o_ref.dtype)

        scratch = [pltpu.VMEM((tm, tn), F32)]

    return pl.pallas_call(
        body, name=name, grid=(m // tm, n // tn, nk),
        in_specs=[a_spec, b_spec],
        out_specs=pl.BlockSpec((tm, tn), lambda i, j, k: (i, j)),
        out_shape=jax.ShapeDtypeStruct((m, n), out_dtype),
        scratch_shapes=scratch,
        compiler_params=_params("parallel", "parallel", "arbitrary"),
    )(a, b)


@functools.partial(jax.custom_vjp, nondiff_argnums=(2, 3, 4, 5))
def mm(a, w, form, name, need_da, out_dtype):
    return _matmul(a, w, form, out_dtype=out_dtype, name=name + "_fwd")


def _mm_fwd(a, w, form, name, need_da, out_dtype):
    return _matmul(a, w, form, out_dtype=out_dtype, name=name + "_fwd"), (a, w)


def _mm_bwd(form, name, need_da, out_dtype, res, ct):
    a, w = res
    if form == "nn":
        da = _matmul(ct, w, "nt", out_dtype=a.dtype, name=name + "_da") if need_da else jnp.zeros_like(a)
        dw = _matmul(a, ct, "tn", out_dtype=w.dtype, name=name + "_dw")
    else:
        da = _matmul(ct, w, "nn", out_dtype=a.dtype, name=name + "_da") if need_da else jnp.zeros_like(a)
        dw = _matmul(ct, a, "tn", out_dtype=w.dtype, name=name + "_dw")
    return da, dw


mm.defvjp(_mm_fwd, _mm_bwd)


def _pairs(bounds):
    return list(zip(bounds[:-1], bounds[1:]))


def _split(v, bounds):
    return [v[:, a:b] for a, b in _pairs(bounds)]


def stage(name, f, tiled, params=(), consts=(), splits=None, ts=256, out_dtypes=None):
    tiled, params, consts = tuple(tiled), tuple(params), tuple(consts)
    n_t, n_p, n_c = len(tiled), len(params), len(consts)
    s = tiled[0].shape[0]
    ts = min(ts, s)
    grid = (s // ts,)
    if splits is None:
        splits = [None] * n_t
    in_bounds = [(0, t.shape[1]) if b is None else tuple(b) for t, b in zip(tiled, splits)]

    def tile_aval(arr):
        return jax.ShapeDtypeStruct((ts, arr.shape[1]), arr.dtype)

    slab_avals = [[jax.ShapeDtypeStruct((ts, e - a), t.dtype) for a, e in _pairs(b)]
                  for t, b in zip(tiled, in_bounds)]
    out_avals = jax.eval_shape(f, slab_avals, list(params), [tile_aval(c) for c in consts])
    out_bounds = [tuple(np.cumsum([0] + [o.shape[1] for o in slabs]).tolist()) for slabs in out_avals]
    out_dtypes = [F32] * len(out_bounds) if out_dtypes is None else out_dtypes
    out_shapes = [jax.ShapeDtypeStruct((s, b[-1]), d) for b, d in zip(out_bounds, out_dtypes)]

    def row_spec(width):
        return pl.BlockSpec((ts, width), lambda i: (i, 0))

    def par_spec(arr):
        return pl.BlockSpec(arr.shape, lambda i: (0, 0))

    in_specs = ([row_spec(t.shape[1]) for t in tiled] + [par_spec(p) for p in params]
                + [row_spec(c.shape[1]) for c in consts])

    def load(refs):
        t = [_split(r[...], b) for r, b in zip(refs[:n_t], in_bounds)]
        p = [r[...] for r in refs[n_t:n_t + n_p]]
        c = [r[...] for r in refs[n_t + n_p:n_t + n_p + n_c]]
        return t, p, c

    def store(refs, values, bounds):
        for ref, slabs, b in zip(refs, values, bounds):
            for v, (a, e) in zip(slabs, _pairs(b)):
                ref[:, a:e] = v.astype(ref.dtype)

    def run_fwd(tiled, params, consts):
        def body(*refs):
            t, p, c = load(refs)
            store(refs[n_t + n_p + n_c:], f(t, p, c), out_bounds)

        return pl.pallas_call(
            body, name=name + "_fwd", grid=grid, in_specs=in_specs,
            out_specs=[row_spec(b[-1]) for b in out_bounds], out_shape=out_shapes,
            compiler_params=_params("parallel"),
        )(*tiled, *params, *consts)

    def run_bwd(tiled, params, consts, cts):
        n_in = n_t + n_p + n_c
        n_o = len(out_bounds)

        def body(*refs):
            t, p, c = load(refs)
            g = [_split(r[...].astype(F32), b) for r, b in zip(refs[n_in:n_in + n_o], out_bounds)]
            _, pull = jax.vjp(lambda t_, p_: f(t_, p_, c), t, p)
            dt, dp = pull(g)
            store(refs[n_in + n_o:n_in + n_o + n_t], dt, in_bounds)
            first = pl.program_id(0) == 0
            for ref, d in zip(refs[n_in + n_o + n_t:], dp):
                @pl.when(first)
                def _(ref=ref):
                    ref[...] = jnp.zeros_like(ref)

                ref[...] += d

        res = pl.pallas_call(
            body, name=name + "_bwd", grid=grid,
            in_specs=in_specs + [row_spec(b[-1]) for b in out_bounds],
            out_specs=[row_spec(t.shape[1]) for t in tiled] + [par_spec(p) for p in params],
            out_shape=[jax.ShapeDtypeStruct(t.shape, t.dtype) for t in tiled]
                      + [jax.ShapeDtypeStruct(p.shape, F32) for p in params],
            compiler_params=_params("arbitrary"),
        )(*tiled, *params, *consts, *cts)
        return tuple(res[:n_t]), tuple(res[n_t:])

    @jax.custom_vjp
    def op(tiled, params, consts):
        return tuple(run_fwd(tiled, params, consts))

    def op_fwd(tiled, params, consts):
        return op(tiled, params, consts), (tiled, params, consts)

    def op_bwd(res, cts):
        tiled, params, consts = res
        dt, dp = run_bwd(tiled, params, consts, cts)
        return dt, dp, tuple(jnp.zeros_like(c) for c in consts)

    op.defvjp(op_fwd, op_bwd)
    return op(tiled, params, consts)


def _rms(t, g):
    return t * lax.rsqrt(jnp.mean(t * t, axis=-1, keepdims=True) + RMS_EPS) * g


def _rope(t1, t2, c, s):
    return t1 * c - t2 * s, t2 * c + t1 * s


def _f_prenorm(t, p, c):
    return [[_rms(t[0][0], p[0])]]


def _f_prep(t, p, c):
    qa1, qa2, ka1, ka2, va, cq, ckv, kr1, kr2 = t[0]
    ca, sa, cb, sb = c
    ca2, sa2 = jnp.concatenate([ca, ca], axis=1), jnp.concatenate([sa, sa], axis=1)
    return [list(_rope(qa1, qa2, ca2, sa2)), list(_rope(ka1, ka2, ca, sa)), [va],
            [_rms(cq, p[0])], [_rms(ckv, p[1])], list(_rope(kr1, kr2, cb, sb))]


def _f_qrope(t, p, c):
    nope, pe1, pe2 = t[0]
    return [[nope] + list(_rope(pe1, pe2, c[0], c[1]))]


def _f_gate(t, p, c):
    (ga, gb), (pa,), (pb,) = t
    ba, bb = p
    return [[jax.nn.sigmoid(ga + ba) * pa + jax.nn.sigmoid(gb + bb) * pb]]


def _f_post(t, p, c):
    x1 = t[0][0] + _rms(t[1][0], p[0])
    return [[x1], [_rms(x1, p[1])]]


def _f_glu(t, p, c):
    return [[jax.nn.gelu(t[0][0], approximate=True) * t[1][0]]]


def _f_out(t, p, c):
    y = t[0][0] + jax.nn.sigmoid(t[1][0]) * t[2][0]
    err = y - c[0]
    return [[0.5 * jnp.mean(err * err, axis=-1, keepdims=True)]]


def _shift_down(cur, prev, has_prev):
    rows = cur.shape[0]
    row = lax.broadcasted_iota(jnp.int32, cur.shape, 0)
    m1 = prev[7:8, :] * has_prev
    m2 = prev[6:7, :] * has_prev
    u1 = jnp.where(row >= 1, pltpu.roll(cur, 1, 0), m1)
    u2 = jnp.where(row >= 2, pltpu.roll(cur, 2, 0), jnp.where(row == 1, m1, m2))
    return u1, u2


def _shift_up(cur, nxt, has_next):
    rows = cur.shape[0]
    row = lax.broadcasted_iota(jnp.int32, cur.shape, 0)
    n0 = nxt[0:1, :] * has_next
    n1 = nxt[1:2, :] * has_next
    d1 = jnp.where(row < rows - 1, pltpu.roll(cur, rows - 1, 0), n0)
    d2 = jnp.where(row < rows - 2, pltpu.roll(cur, rows - 2, 0), jnp.where(row == rows - 2, n0, n1))
    return d1, d2


def _conv_tiles(s, ch):
    ts = min(CONV_TS, s)
    tc = _pick(ch, CONV_TC, 128)
    return ts, tc, s // ts, ch // tc


def _conv_fwd_call(up, w, b, name):
    s, ch = up.shape
    ts, tc, nt, nc = _conv_tiles(s, ch)
    hb = ts // 8

    def body(cur_ref, prev_ref, w_ref, b_ref, o_ref):
        cur = cur_ref[...]
        u1, u2 = _shift_down(cur, prev_ref[...], (pl.program_id(1) > 0).astype(F32))
        o_ref[...] = w_ref[2:3, :] * cur + w_ref[1:2, :] * u1 + w_ref[0:1, :] * u2 + b_ref[...]

    return pl.pallas_call(
        body, name=name + "_fwd", grid=(nc, nt),
        in_specs=[pl.BlockSpec((ts, tc), lambda c, i: (i, c)),
                  pl.BlockSpec((8, tc), lambda c, i: (jnp.maximum(i * hb - 1, 0), c)),
                  pl.BlockSpec((CONV_W, tc), lambda c, i: (0, c)),
                  pl.BlockSpec((1, tc), lambda c, i: (0, c))],
        out_specs=pl.BlockSpec((ts, tc), lambda c, i: (i, c)),
        out_shape=jax.ShapeDtypeStruct((s, ch), F32),
        compiler_params=_params("parallel", "parallel"),
    )(up, up, w, b)


def _conv_bwd_call(up, w, du, name):
    s, ch = up.shape
    ts, tc, nt, nc = _conv_tiles(s, ch)
    hb = ts // 8

    def body(cur_ref, prev_ref, w_ref, du_ref, nxt_ref, dup_ref, dw_ref, db_ref):
        i = pl.program_id(1)
        cur, du = cur_ref[...], du_ref[...]
        u1, u2 = _shift_down(cur, prev_ref[...], (i > 0).astype(F32))
        d1, d2 = _shift_up(du, nxt_ref[...], (i < nt - 1).astype(F32))
        dup_ref[...] = w_ref[2:3, :] * du + w_ref[1:2, :] * d1 + w_ref[0:1, :] * d2

        @pl.when(i == 0)
        def _():
            dw_ref[...] = jnp.zeros_like(dw_ref)
            db_ref[...] = jnp.zeros_like(db_ref)

        dw_ref[0:1, :] += jnp.sum(du * u2, axis=0, keepdims=True)
        dw_ref[1:2, :] += jnp.sum(du * u1, axis=0, keepdims=True)
        dw_ref[2:3, :] += jnp.sum(du * cur, axis=0, keepdims=True)
        db_ref[...] += jnp.sum(du, axis=0, keepdims=True)

    return pl.pallas_call(
        body, name=name + "_bwd", grid=(nc, nt),
        in_specs=[pl.BlockSpec((ts, tc), lambda c, i: (i, c)),
                  pl.BlockSpec((8, tc), lambda c, i: (jnp.maximum(i * hb - 1, 0), c)),
                  pl.BlockSpec((CONV_W, tc), lambda c, i: (0, c)),
                  pl.BlockSpec((ts, tc), lambda c, i: (i, c)),
                  pl.BlockSpec((8, tc), lambda c, i: (jnp.minimum((i + 1) * hb, s // 8 - 1), c))],
        out_specs=[pl.BlockSpec((ts, tc), lambda c, i: (i, c)),
                   pl.BlockSpec((CONV_W, tc), lambda c, i: (0, c)),
                   pl.BlockSpec((1, tc), lambda c, i: (0, c))],
        out_shape=[jax.ShapeDtypeStruct((s, ch), F32), jax.ShapeDtypeStruct((CONV_W, ch), F32),
                   jax.ShapeDtypeStruct((1, ch), F32)],
        compiler_params=_params("parallel", "arbitrary"),
    )(up, up, w, du, du)


@functools.partial(jax.custom_vjp, nondiff_argnums=(3,))
def dwconv(up, w, b, name):
    return _conv_fwd_call(up, w, b, name)


def _dwconv_fwd(up, w, b, name):
    return _conv_fwd_call(up, w, b, name), (up, w)


def _dwconv_bwd(name, res, ct):
    up, w = res
    return tuple(_conv_bwd_call(up, w, ct, name))


dwconv.defvjp(_dwconv_fwd, _dwconv_bwd)


SWA_ROWS = A_GROUP * SWA_BLOCK


def _swa_stack(ref):
    return ref[...].reshape(SWA_ROWS, ref.shape[-1])


def _swa_sink_rows(sink_ref, g):
    return jnp.concatenate([jnp.full((SWA_BLOCK, 1), sink_ref[g * A_GROUP + h], F32) for h in range(A_GROUP)], axis=0)


def _swa_probs(q, kp, kc, sink, prev_off):
    scale = A_HEAD_DIM ** -0.5
    nt = (((1,), (1,)), ((), ()))
    sp = lax.dot_general(q, kp, nt, preferred_element_type=F32) * scale
    sc = lax.dot_general(q, kc, nt, preferred_element_type=F32) * scale
    qi = lax.broadcasted_iota(jnp.int32, sp.shape, 0) & (SWA_BLOCK - 1)
    kj = lax.broadcasted_iota(jnp.int32, sp.shape, 1)
    sp = jnp.where(kj > qi + prev_off, sp, -jnp.inf)
    sc = jnp.where(kj <= qi, sc, -jnp.inf)
    m = jnp.maximum(jnp.maximum(jnp.max(sp, axis=-1, keepdims=True), jnp.max(sc, axis=-1, keepdims=True)), sink)
    ep, ec, es = jnp.exp(sp - m), jnp.exp(sc - m), jnp.exp(sink - m)
    den = jnp.sum(ep, axis=-1, keepdims=True) + jnp.sum(ec, axis=-1, keepdims=True) + es
    return ep / den, ec / den, es / den


def _swa_specs(s):
    blk = SWA_BLOCK
    q_spec = pl.BlockSpec((None, A_GROUP, blk, A_HEAD_DIM), lambda g, n: (g, 0, n, 0))
    prev_spec = pl.BlockSpec((None, blk, A_HEAD_DIM), lambda g, n: (g, jnp.maximum(n - 1, 0), 0))
    cur_spec = pl.BlockSpec((None, blk, A_HEAD_DIM), lambda g, n: (g, n, 0))
    sink_spec = pl.BlockSpec(memory_space=pltpu.SMEM)
    return q_spec, prev_spec, cur_spec, sink_spec


def _swa_fwd_call(q, k, v, sinks):
    s = q.shape[2]
    q_spec, prev_spec, cur_spec, sink_spec = _swa_specs(s)

    def body(q_ref, kp_ref, kc_ref, vp_ref, vc_ref, sink_ref, o_ref):
        g, n = pl.program_id(0), pl.program_id(1)
        prev_off = jnp.where(n > 0, 0, SWA_BLOCK)
        kp, kc = kp_ref[...].astype(BF16), kc_ref[...].astype(BF16)
        vp, vc = vp_ref[...].astype(BF16), vc_ref[...].astype(BF16)
        pp, pc, _ = _swa_probs(_swa_stack(q_ref).astype(BF16), kp, kc, _swa_sink_rows(sink_ref, g), prev_off)
        out = (jnp.dot(pp.astype(BF16), vp, preferred_element_type=F32)
               + jnp.dot(pc.astype(BF16), vc, preferred_element_type=F32))
        o_ref[...] = out.reshape(o_ref.shape)

    return pl.pallas_call(
        body, name="swa_fwd", grid=(A_KV_HEADS, s // SWA_BLOCK),
        in_specs=[q_spec, prev_spec, cur_spec, prev_spec, cur_spec, sink_spec],
        out_specs=q_spec, out_shape=jax.ShapeDtypeStruct(q.shape, F32),
        compiler_params=_params("parallel", "parallel"),
    )(q, k, k, v, v, sinks)


def _swa_bwd_call(q, k, v, sinks, do):
    s = q.shape[2]
    q_spec, prev_spec, cur_spec, sink_spec = _swa_specs(s)
    scale = A_HEAD_DIM ** -0.5
    tn = (((0,), (0,)), ((), ()))
    nt = (((1,), (1,)), ((), ()))
    dsink_spec = pl.BlockSpec((None, A_GROUP, SWA_BLOCK, 1), lambda g, n: (g, 0, 0, 0))

    def body(q_ref, kp_ref, kc_ref, vp_ref, vc_ref, sink_ref, do_ref,
             dq_ref, dkp_ref, dkc_ref, dvp_ref, dvc_ref, dsink_ref):
        g, n = pl.program_id(0), pl.program_id(1)
        prev_off = jnp.where(n > 0, 0, SWA_BLOCK)
        kp, kc = kp_ref[...].astype(BF16), kc_ref[...].astype(BF16)
        vp, vc = vp_ref[...].astype(BF16), vc_ref[...].astype(BF16)

        @pl.when(n == 0)
        def _():
            dsink_ref[...] = jnp.zeros_like(dsink_ref)

        qb = _swa_stack(q_ref).astype(BF16)
        pp, pc, ps = _swa_probs(qb, kp, kc, _swa_sink_rows(sink_ref, g), prev_off)
        ppb, pcb = pp.astype(BF16), pc.astype(BF16)
        out = jnp.dot(ppb, vp, preferred_element_type=F32) + jnp.dot(pcb, vc, preferred_element_type=F32)
        dof = _swa_stack(do_ref)
        dob = dof.astype(BF16)
        delta = jnp.sum(dof * out, axis=-1, keepdims=True)
        dsp = (pp * (lax.dot_general(dob, vp, nt, preferred_element_type=F32) - delta)).astype(BF16)
        dsc = (pc * (lax.dot_general(dob, vc, nt, preferred_element_type=F32) - delta)).astype(BF16)
        dsink_ref[...] += (-ps * delta).reshape(dsink_ref.shape)
        dq = (jnp.dot(dsp, kp, preferred_element_type=F32) + jnp.dot(dsc, kc, preferred_element_type=F32)) * scale
        dq_ref[...] = dq.reshape(dq_ref.shape)
        dkp_ref[...] = lax.dot_general(dsp, qb, tn, preferred_element_type=F32) * scale
        dkc_ref[...] = lax.dot_general(dsc, qb, tn, preferred_element_type=F32) * scale
        dvp_ref[...] = lax.dot_general(ppb, dob, tn, preferred_element_type=F32)
        dvc_ref[...] = lax.dot_general(pcb, dob, tn, preferred_element_type=F32)

    kv_shape = jax.ShapeDtypeStruct(k.shape, F32)
    return pl.pallas_call(
        body, name="swa_bwd", grid=(A_KV_HEADS, s // SWA_BLOCK),
        in_specs=[q_spec, prev_spec, cur_spec, prev_spec, cur_spec, sink_spec, q_spec],
        out_specs=[q_spec, cur_spec, cur_spec, cur_spec, cur_spec, dsink_spec],
        out_shape=[jax.ShapeDtypeStruct(q.shape, F32), kv_shape, kv_shape, kv_shape, kv_shape,
                   jax.ShapeDtypeStruct((A_KV_HEADS, A_GROUP, SWA_BLOCK, 1), F32)],
        compiler_params=_params("parallel", "arbitrary"),
    )(q, k, k, v, v, sinks, do)


@jax.custom_vjp
def swa(q, k, v, sinks):
    return _swa_fwd_call(q, k, v, sinks)


def _swa_fwd(q, k, v, sinks):
    return _swa_fwd_call(q, k, v, sinks), (q, k, v, sinks)


def _swa_bwd(res, do):
    q, k, v, sinks = res
    dq, dkp, dkc, dvp, dvc, dsink = _swa_bwd_call(q, k, v, sinks, do)

    def fold(prev_part, cur_part):
        shifted = jnp.concatenate([prev_part[:, SWA_BLOCK:], jnp.zeros_like(prev_part[:, :SWA_BLOCK])], axis=1)
        return cur_part + shifted

    return dq, fold(dkp, dkc), fold(dvp, dvc), jnp.sum(dsink, axis=(2, 3)).reshape(-1)


swa.defvjp(_swa_fwd, _swa_bwd)


MLA_SCALE = (NOPE_DIM + ROPE_DIM) ** -0.5


EXP2_SCALE = MLA_SCALE * float(np.log2(np.e))
NT_DIMS = (((1,), (1,)), ((), ()))
TN_DIMS = (((0,), (0,)), ((), ()))


def _flash_fwd_call(q, k, v_t):
    h, s, d = q.shape
    t = min(FLASH_T, s)
    nb = s // t

    def body(q_ref, k_ref, vt_ref, ot_ref, lse_ref, m_ref, l_ref, acc_ref):
        i = pl.program_id(1)
        qb = q_ref[...]
        m_ref[...] = jnp.full_like(m_ref, -jnp.inf)
        l_ref[...] = jnp.zeros_like(l_ref)
        acc_ref[...] = jnp.zeros_like(acc_ref)

        def step(j, on_diagonal):
            keys = pl.ds(pl.multiple_of(j * t, t), t)
            sc_t = lax.dot_general(k_ref[keys, :], qb, NT_DIMS, preferred_element_type=F32)
            if on_diagonal:
                key = lax.broadcasted_iota(jnp.int32, (t, t), 0)
                qry = lax.broadcasted_iota(jnp.int32, (t, t), 1)
                sc_t = jnp.where(qry >= key, sc_t, -jnp.inf)
            m_old = m_ref[...]
            m_new = jnp.maximum(m_old, jnp.max(sc_t, axis=0, keepdims=True))
            alpha = jnp.exp2((m_old - m_new) * EXP2_SCALE)
            p_t = jnp.exp2((sc_t - m_new) * EXP2_SCALE)
            l_ref[...] = alpha * l_ref[...] + jnp.sum(p_t, axis=0, keepdims=True)
            acc_ref[...] = alpha * acc_ref[...] + jnp.dot(vt_ref[:, keys], p_t.astype(BF16),
                                                          preferred_element_type=F32)
            m_ref[...] = m_new

        def below(j, carry):
            step(j, False)
            return carry

        lax.fori_loop(0, i, below, 0)
        step(i, True)
        ot_ref[...] = acc_ref[...] / l_ref[...]
        lse_ref[...] = m_ref[...] * EXP2_SCALE + jnp.log2(l_ref[...])

    return pl.pallas_call(
        body, name="mla_fwd", grid=(h, nb),
        in_specs=[pl.BlockSpec((None, t, d), lambda hh, i: (hh, i, 0)),
                  pl.BlockSpec((None, s, d), lambda hh, i: (hh, 0, 0)),
                  pl.BlockSpec((None, d, s), lambda hh, i: (hh, 0, 0))],
        out_specs=[pl.BlockSpec((None, d, t), lambda hh, i: (hh, 0, i)),
                   pl.BlockSpec((None, 1, t), lambda hh, i: (hh, 0, i))],
        out_shape=[jax.ShapeDtypeStruct((h, d, s), F32), jax.ShapeDtypeStruct((h, 1, s), F32)],
        scratch_shapes=[pltpu.VMEM((1, t), F32), pltpu.VMEM((1, t), F32), pltpu.VMEM((d, t), F32)],
        compiler_params=_params("parallel", "arbitrary"),
    )(q, k, v_t)


def _flash_delta_call(o_t, do_t):
    h, d, s = o_t.shape
    t = min(FLASH_T, s)

    def body(o_ref, do_ref, out_ref, dob_ref):
        do = do_ref[...]
        out_ref[...] = jnp.sum(o_ref[...] * do, axis=0, keepdims=True)
        dob_ref[...] = do.astype(BF16)

    spec = pl.BlockSpec((None, d, t), lambda hh, i: (hh, 0, i))
    return pl.pallas_call(
        body, name="mla_delta", grid=(h, s // t), in_specs=[spec, spec],
        out_specs=[pl.BlockSpec((None, 1, t), lambda hh, i: (hh, 0, i)), spec],
        out_shape=[jax.ShapeDtypeStruct((h, 1, s), F32), jax.ShapeDtypeStruct((h, d, s), BF16)],
        compiler_params=_params("parallel", "parallel"),
    )(o_t, do_t)


def _flash_bwd_call(q, k, v, lse_row, delta_row, do_t):
    h, s, d = q.shape
    t = min(FLASH_T, s)
    nb = s // t

    def body(q_ref, k_ref, v_ref, lse_ref, delta_ref, dot_ref, dq_ref, dk_ref, dv_ref, dq_acc, dk_acc, dv_acc):
        j = pl.program_id(1)

        @pl.when(j == 0)
        def _():
            dq_acc[...] = jnp.zeros_like(dq_acc)

        kb, vb = k_ref[...], v_ref[...]
        dk_acc[...] = jnp.zeros_like(dk_acc)
        dv_acc[...] = jnp.zeros_like(dv_acc)

        def step(i, on_diagonal):
            rows = pl.ds(pl.multiple_of(i * t, t), t)
            qb, dob_t = q_ref[rows, :], dot_ref[:, rows]
            sc_t = lax.dot_general(kb, qb, NT_DIMS, preferred_element_type=F32)
            p_t = jnp.exp2(sc_t * EXP2_SCALE - lse_ref[:, rows])
            if on_diagonal:
                key = lax.broadcasted_iota(jnp.int32, (t, t), 0)
                qry = lax.broadcasted_iota(jnp.int32, (t, t), 1)
                p_t = jnp.where(qry >= key, p_t, 0.0)
            dp_t = jnp.dot(vb, dob_t, preferred_element_type=F32)
            ds_t = (p_t * (dp_t - delta_ref[:, rows])).astype(BF16)
            dv_acc[...] += lax.dot_general(p_t.astype(BF16), dob_t, NT_DIMS, preferred_element_type=F32)
            dk_acc[...] += jnp.dot(ds_t, qb, preferred_element_type=F32)
            dq_acc[rows, :] += lax.dot_general(ds_t, kb, TN_DIMS, preferred_element_type=F32)

        def above(i, carry):
            step(i, False)
            return carry

        step(j, True)
        lax.fori_loop(j + 1, nb, above, 0)
        dk_ref[...] = (dk_acc[...] * MLA_SCALE).astype(dk_ref.dtype)
        dv_ref[...] = dv_acc[...].astype(dv_ref.dtype)

        @pl.when(j == nb - 1)
        def _():
            dq_ref[...] = (dq_acc[...] * MLA_SCALE).astype(dq_ref.dtype)

    full_spec = pl.BlockSpec((None, s, d), lambda hh, j: (hh, 0, 0))
    tile_spec = pl.BlockSpec((None, t, d), lambda hh, j: (hh, j, 0))
    row_spec = pl.BlockSpec((None, 1, s), lambda hh, j: (hh, 0, 0))
    return pl.pallas_call(
        body, name="mla_bwd", grid=(h, nb),
        in_specs=[full_spec, tile_spec, tile_spec, row_spec, row_spec,
                  pl.BlockSpec((None, d, s), lambda hh, j: (hh, 0, 0))],
        out_specs=[full_spec, tile_spec, tile_spec],
        out_shape=[jax.ShapeDtypeStruct((h, s, d), q.dtype)] * 3,
        scratch_shapes=[pltpu.VMEM((s, d), F32), pltpu.VMEM((t, d), F32), pltpu.VMEM((t, d), F32)],
        compiler_params=_params("parallel", "arbitrary"),
    )(q, k, v, lse_row, delta_row, do_t)


@jax.custom_vjp
def flash(q, k, v):
    return _flash_fwd_call(q, k, v.transpose(0, 2, 1))[0]


def _flash_fwd(q, k, v):
    o_t, lse = _flash_fwd_call(q, k, v.transpose(0, 2, 1))
    return o_t, (q, k, v, o_t, lse)


def _flash_bwd(res, do_t):
    q, k, v, o_t, lse = res
    delta, dob_t = _flash_delta_call(o_t, do_t)
    return tuple(_flash_bwd_call(q, k, v, lse, delta, dob_t))


flash.defvjp(_flash_fwd, _flash_bwd)


HBM_SPEC = pl.BlockSpec(memory_space=pltpu.HBM)


def _allgather(shards, name):
    n_arr = len(shards)

    def body(*refs):
        x_refs, out_refs = refs[:n_arr], refs[n_arr:2 * n_arr]
        send_sems, recv_sems, local_sems = refs[2 * n_arr:]
        x, y, c = lax.axis_index("x"), lax.axis_index("y"), lax.axis_index("c")
        me, sibling = (x, y, c), (x, y, 1 - c)
        chips = [(1 - x, y), (x, 1 - y), (1 - x, 1 - y)]
        arrays = range(n_arr)

        def rows(a, px, py, pc):
            return out_refs[a].at[4 * px + 2 * py + pc]

        def copy(a, k, block, to, src=None):
            return pltpu.make_async_remote_copy(
                src_ref=rows(a, *block) if src is None else src, dst_ref=rows(a, *block),
                send_sem=send_sems.at[k, a], recv_sem=recv_sems.at[k, a], device_id=to, device_id_type=MESH_ID)

        mine = [pltpu.make_async_copy(x_refs[a], rows(a, *me), local_sems.at[a]) for a in arrays]
        for cp in mine:
            cp.start()
        first = []
        for a in arrays:
            first.append(copy(a, 0, me, sibling, src=x_refs[a]))
            first += [copy(a, 1 + j, me, (*chip, c), src=x_refs[a]) for j, chip in enumerate(chips)]
        for cp in first:
            cp.start()
        passed = []
        for j, chip in enumerate(chips):
            for a in arrays:
                copy(a, 1 + j, (*chip, c), me).wait_recv()
                passed.append(copy(a, 4 + j, (*chip, c), sibling))
                passed[-1].start()
        for a in arrays:
            copy(a, 0, sibling, me).wait_recv()
        for j, chip in enumerate(chips):
            for a in arrays:
                copy(a, 4 + j, (*chip, 1 - c), me).wait_recv()
        for cp in first + passed:
            cp.wait_send()
        for cp in mine:
            cp.wait()

    return pl.pallas_call(
        body, name=name, out_shape=[jax.ShapeDtypeStruct((N_DEV,) + s.shape, s.dtype) for s in shards],
        in_specs=[HBM_SPEC] * n_arr, out_specs=[HBM_SPEC] * n_arr,
        scratch_shapes=[pltpu.SemaphoreType.DMA((7, n_arr)), pltpu.SemaphoreType.DMA((7, n_arr)),
                        pltpu.SemaphoreType.DMA((n_arr,))],
    )(*shards)


N_CHIP = 4


def _exchange_sibling(parts, name):
    n_arr = len(parts)

    def body(*refs):
        in_refs, recv_refs = refs[:n_arr], refs[n_arr:2 * n_arr]
        send_sems, recv_sems = refs[2 * n_arr:]
        x, y, c = lax.axis_index("x"), lax.axis_index("y"), lax.axis_index("c")
        copies = []
        for a in range(n_arr):
            for q in range(N_CHIP):
                copies.append(pltpu.make_async_remote_copy(
                    src_ref=in_refs[a].at[2 * q + 1 - c], dst_ref=recv_refs[a].at[q],
                    send_sem=send_sems.at[q, a], recv_sem=recv_sems.at[q, a],
                    device_id=(x, y, 1 - c), device_id_type=MESH_ID))
        for cp in copies:
            cp.start()
        for cp in copies:
            cp.wait()

    return pl.pallas_call(
        body, name=name, out_shape=[jax.ShapeDtypeStruct((N_CHIP,) + p.shape[1:], p.dtype) for p in parts],
        in_specs=[HBM_SPEC] * n_arr, out_specs=[HBM_SPEC] * n_arr,
        scratch_shapes=[pltpu.SemaphoreType.DMA((N_CHIP, n_arr)), pltpu.SemaphoreType.DMA((N_CHIP, n_arr))],
    )(*parts)


def _exchange_chips(parts, name):
    n_arr = len(parts)

    def body(*refs):
        in_refs, out_refs = refs[:n_arr], refs[n_arr:2 * n_arr]
        send_sems, recv_sems, local_sems = refs[2 * n_arr:]
        x, y, c = lax.axis_index("x"), lax.axis_index("y"), lax.axis_index("c")
        me = 2 * x + y
        copies = [pltpu.make_async_copy(in_refs[a].at[me], out_refs[a].at[me], local_sems.at[a]) for a in range(n_arr)]
        for k in range(1, N_CHIP):
            px = 1 - x if k & 2 else x
            py = 1 - y if k & 1 else y
            for a in range(n_arr):
                copies.append(pltpu.make_async_remote_copy(
                    src_ref=in_refs[a].at[2 * px + py], dst_ref=out_refs[a].at[me],
                    send_sem=send_sems.at[k - 1, a], recv_sem=recv_sems.at[k - 1, a],
                    device_id=(px, py, c), device_id_type=MESH_ID))
        for cp in copies:
            cp.start()
        for cp in copies:
            cp.wait()

    return pl.pallas_call(
        body, name=name, out_shape=[jax.ShapeDtypeStruct(p.shape, p.dtype) for p in parts],
        in_specs=[HBM_SPEC] * n_arr, out_specs=[HBM_SPEC] * n_arr,
        scratch_shapes=[pltpu.SemaphoreType.DMA((N_CHIP - 1, n_arr)), pltpu.SemaphoreType.DMA((N_CHIP - 1, n_arr)),
                        pltpu.SemaphoreType.DMA((n_arr,))],
    )(*parts)


def _row_tile(r, ccols, blocks):
    cap = max(16, (2 * 1024 * 1024) // (4 * ccols * blocks))
    return _pick(r, cap, 16)


def _pair_add(mine, theirs, my_c, name):
    _, r, ccols = mine.shape
    tr = _row_tile(r, ccols, 1)

    def body(c_ref, a_ref, b_ref, o_ref):
        o_ref[...] = (a_ref[...].astype(F32) + b_ref[...].astype(F32)).astype(o_ref.dtype)

    spec = pl.BlockSpec((None, tr, ccols), lambda q, i, c_ref: (q, i, 0))
    return pl.pallas_call(
        body, name=name,
        grid_spec=pltpu.PrefetchScalarGridSpec(
            num_scalar_prefetch=1, grid=(N_CHIP, r // tr),
            in_specs=[pl.BlockSpec((None, tr, ccols), lambda q, i, c_ref: (2 * q + c_ref[0], i, 0)), spec],
            out_specs=spec),
        out_shape=jax.ShapeDtypeStruct(theirs.shape, theirs.dtype),
        compiler_params=_params("parallel", "parallel"),
    )(my_c, mine, theirs)


def _sum_blocks(parts, name):
    nb, r, ccols = parts.shape
    tr = _row_tile(r, ccols, nb)

    def body(p_ref, o_ref):
        acc = p_ref[0].astype(F32)
        for i in range(1, nb):
            acc = acc + p_ref[i].astype(F32)
        o_ref[...] = acc

    return pl.pallas_call(
        body, name=name, grid=(r // tr,),
        in_specs=[pl.BlockSpec((nb, tr, ccols), lambda i: (0, i, 0))],
        out_specs=pl.BlockSpec((tr, ccols), lambda i: (i, 0)),
        out_shape=jax.ShapeDtypeStruct((r, ccols), F32),
        compiler_params=_params("parallel"),
    )(parts)


def _gather_wire(shards, wire_dtypes):
    return tuple(_allgather([s.astype(d) for s, d in zip(shards, wire_dtypes)], "weights_allgather"))


@functools.partial(jax.custom_vjp, nondiff_argnums=(1,))
def fsdp_gather(shards, wire_dtypes):
    return _gather_wire(shards, wire_dtypes)


def _fsdp_gather_fwd(shards, wire_dtypes):
    return _gather_wire(shards, wire_dtypes), None


def _fsdp_gather_bwd(wire_dtypes, _, cts):
    received = _exchange_sibling(list(cts), "grads_exchange_sibling")
    my_c = lax.axis_index("c").astype(jnp.int32).reshape(1)
    pair_sums = [_pair_add(m, r, my_c, "grad_pair_add_%d" % i) for i, (m, r) in enumerate(zip(cts, received))]
    chip_parts = _exchange_chips(pair_sums, "grads_exchange_chips")
    return (tuple(_sum_blocks(r, "grad_sum_%d" % i) for i, r in enumerate(chip_parts)),)


fsdp_gather.defvjp(_fsdp_gather_fwd, _fsdp_gather_bwd)


@jax.custom_vjp
def replicated(vec):
    return vec


def _replicated_fwd(vec):
    return vec, None


def _replicated_bwd(_, ct):
    return (_sum_blocks(_allgather([ct], "small_grad_allgather")[0], "small_grad_sum"),)


replicated.defvjp(_replicated_fwd, _replicated_bwd)


def _adamw(w, g, m, v, name):
    rows, cols = w.shape
    tr = _pick(rows, 256, 8) if rows % 8 == 0 else rows

    def body(w_ref, g_ref, m_ref, v_ref, d_ref, nm_ref, nv_ref):
        g_ = g_ref[...]
        m_ = ADAM_B1 * m_ref[...] + (1.0 - ADAM_B1) * g_
        v_ = ADAM_B2 * v_ref[...] + (1.0 - ADAM_B2) * jnp.square(g_)
        m_hat = m_ / (1.0 - ADAM_B1 ** ADAM_STEP)
        v_hat = v_ / (1.0 - ADAM_B2 ** ADAM_STEP)
        d_ref[...] = -ADAM_LR * (m_hat / (jnp.sqrt(v_hat) + ADAM_EPS) + ADAM_WD * w_ref[...])
        nm_ref[...] = m_
        nv_ref[...] = v_

    spec = pl.BlockSpec((tr, cols), lambda i: (i, 0))
    return pl.pallas_call(
        body, name=name, grid=(rows // tr,), in_specs=[spec] * 4, out_specs=[spec] * 3,
        out_shape=[jax.ShapeDtypeStruct(w.shape, F32)] * 3, compiler_params=_params("parallel"),
    )(w, g, m, v)


COL_SHARDED = ("w_in", "w_uq", "w_ukv", "w_branch_a", "w_branch_b", "w_up", "w_ple")
ROW_SHARDED = ("w_out", "w_down", "w_ple_gate")
BIG = ("w_in", "w_uq", "w_ukv", "w_branch_a", "w_branch_b", "w_out", "w_up", "w_down", "w_ple_gate", "w_ple")
SMALL = ("attn_pre_norm", "attn_post_norm", "b_gate", "q_a_norm", "kv_a_norm", "mlp_pre_norm", "mlp_post_norm",
         "conv_b", "ple_norm", "sinks")
SMALL_COLS = 128


def _pack_rows(arrays, cols, row_mult):
    flat = jnp.concatenate([a.reshape(-1) for a in arrays])
    pad = (-flat.shape[0]) % (cols * row_mult)
    return jnp.pad(flat, (0, pad)).reshape(-1, cols)


def _unpack_small(vec, shapes):
    flat = vec.reshape(-1)
    out, off = {}, 0
    for name in SMALL:
        n = shapes[name]
        out[name] = flat[off:off + n].reshape(1, n)
        off += n + (-n) % SMALL_COLS
    return out


def _pad_lanes(t, width):
    return jnp.pad(t, [(0, 0)] * (t.ndim - 1) + [(0, width - t.shape[-1])])


def _pad_rows(t, rows):
    return jnp.pad(t, [(0, 0)] * (t.ndim - 2) + [(0, rows - t.shape[-2]), (0, 0)])


def _arrange_w_in_t(wt):
    k = wt.shape[1]
    qa, ka, va, cq, ckv, kr, gates = jnp.split(wt, np.cumsum([512, 128, 128, 256, 128, 32]).tolist(), axis=0)
    qa = qa.reshape(A_HEADS, 2, 32, k).transpose(1, 0, 2, 3).reshape(512, k)
    ka = _pad_rows(ka.reshape(A_KV_HEADS, 2, 32, k).transpose(1, 0, 2, 3).reshape(2, 64, k), 128).reshape(256, k)
    kr = _pad_rows(kr.reshape(2, 16, k), 128).reshape(256, k)
    return jnp.concatenate([qa, ka, va, cq, ckv, kr], axis=0), gates


FRONT_BOUNDS = (0, 256, 512, 640, 768, 896, 1152, 1280, 1408, 1536)


def _arrange_w_uq_t(wt):
    k = wt.shape[1]
    w = wt.reshape(B_HEADS, NOPE_DIM + ROPE_DIM, k)
    return jnp.concatenate([w[:, :64].reshape(512, k), w[:, 64:80].reshape(128, k), w[:, 80:96].reshape(128, k)],
                           axis=0)


def _rope_tables(positions, s):
    pos = positions.reshape(s, 1).astype(F32)

    def table(dim, reps):
        inv = ROPE_THETA ** (-(jnp.arange(0, dim, 2, dtype=F32) / dim))
        ang = pos * inv
        return jnp.tile(jnp.cos(ang), (1, reps)), jnp.tile(jnp.sin(ang), (1, reps))

    ca, sa = table(A_HEAD_DIM, 4)
    cb, sb = table(ROPE_DIM, 8)
    return ca, sa, cb, sb


def _local_loss(wts, x, p, tables, target):
    s = x.shape[0]
    small_shapes = {n: wts[n].shape[-1] for n in SMALL}
    small_vec = _pack_rows([_pad_lanes(wts[n].reshape(1, -1), small_shapes[n] + (-small_shapes[n]) % SMALL_COLS)
                            for n in SMALL], SMALL_COLS, 8)
    sm = _unpack_small(replicated(small_vec), small_shapes)
    shards = [wts[n].T if n in COL_SHARDED else wts[n] for n in BIG] + [_pack_rows([wts["conv_w"]], SMALL_COLS, 8)]
    gathered = fsdp_gather(tuple(shards), (BF16,) * len(BIG) + (F32,))
    big = {n: g.reshape(-1, g.shape[2]) for n, g in zip(BIG, gathered)}
    ch = wts["conv_w"].shape[1]
    conv_w = gathered[-1].reshape(N_DEV, -1)[:, :CONV_W * ch].reshape(N_DEV, CONV_W, ch)
    conv_w = conv_w.transpose(1, 0, 2).reshape(CONV_W, N_DEV * ch)

    w_front_t, w_gates_t = _arrange_w_in_t(big["w_in"])
    ca, sa, cb, sb = tables

    (h1,) = stage("prenorm", _f_prenorm, [x], [sm["attn_pre_norm"]], out_dtypes=[BF16])
    zf = mm(h1, w_front_t, "nt", "w_front", True, F32)
    gates = mm(h1, w_gates_t, "nt", "w_gates", True, F32)
    qar, kar, va, cqn, ckvn, kpe = stage("prep", _f_prep, [zf], [sm["q_a_norm"], sm["kv_a_norm"]],
                                         [ca, sa, cb, sb], splits=[FRONT_BOUNDS],
                                         out_dtypes=[F32, F32, F32, BF16, BF16, F32])

    q_a = jnp.concatenate([qar[:, :256].reshape(s, A_HEADS, 32), qar[:, 256:].reshape(s, A_HEADS, 32)], axis=-1)
    q_a = q_a.reshape(s, A_KV_HEADS, A_GROUP, A_HEAD_DIM).transpose(1, 2, 0, 3)
    k_a = jnp.concatenate([kar[:, 0:64].reshape(s, A_KV_HEADS, 32), kar[:, 128:192].reshape(s, A_KV_HEADS, 32)],
                          axis=-1).transpose(1, 0, 2)
    v_a = va.reshape(s, A_KV_HEADS, A_HEAD_DIM).transpose(1, 0, 2)
    ya = swa(q_a, k_a, v_a, sm["sinks"].reshape(-1)).transpose(2, 0, 1, 3).reshape(s, A_HEADS * A_HEAD_DIM)
    ya = ya.astype(BF16)

    qb = mm(cqn, _arrange_w_uq_t(big["w_uq"]), "nt", "w_uq", True, F32)
    kvb = mm(ckvn, big["w_ukv"], "nt", "w_ukv", True, F32)
    (qbr,) = stage("qrope", _f_qrope, [qb], [], [cb, sb], splits=[(0, 512, 640, 768)])
    zeros32 = jnp.zeros((s, B_HEADS, HEAD_PAD - NOPE_DIM - ROPE_DIM), F32)
    q_b = jnp.concatenate([qbr[:, :512].reshape(s, B_HEADS, 64), qbr[:, 512:640].reshape(s, B_HEADS, 16),
                           qbr[:, 640:].reshape(s, B_HEADS, 16), zeros32], axis=-1).transpose(1, 0, 2)
    kv = kvb.reshape(s, B_HEADS, NOPE_DIM + V_DIM)
    k_b = jnp.concatenate([kv[:, :, :NOPE_DIM],
                           jnp.broadcast_to(kpe[:, None, 0:16], (s, B_HEADS, 16)),
                           jnp.broadcast_to(kpe[:, None, 128:144], (s, B_HEADS, 16)), zeros32],
                          axis=-1).transpose(1, 0, 2)
    v_b = _pad_lanes(kv[:, :, NOPE_DIM:], HEAD_PAD).transpose(1, 0, 2)
    yb_t = flash(q_b.astype(BF16), k_b.astype(BF16), v_b.astype(BF16))[:, :V_DIM, :]
    yb = yb_t.reshape(B_HEADS * V_DIM, s).T.astype(BF16)

    pa = mm(ya, big["w_branch_a"], "nt", "w_branch_a", True, F32)
    pb = mm(yb, big["w_branch_b"], "nt", "w_branch_b", True, F32)
    (mixed,) = stage("gate", _f_gate, [gates, pa, pb], [sm["b_gate"][:, :D_MODEL], sm["b_gate"][:, D_MODEL:]],
                     splits=[(0, D_MODEL, 2 * D_MODEL), None, None], out_dtypes=[BF16])
    o = mm(mixed, big["w_out"], "nn", "w_out", True, F32)
    x1, h2 = stage("post_attn", _f_post, [x, o], [sm["attn_post_norm"], sm["mlp_pre_norm"]], out_dtypes=[F32, BF16])

    up_g = mm(h2, big["w_up"][:D_FF], "nt", "w_up_gate", True, F32)
    up_v = mm(h2, big["w_up"][D_FF:], "nt", "w_up_val", True, F32)
    u_g = dwconv(up_g, conv_w[:, :D_FF], sm["conv_b"][:, :D_FF], "conv_gate")
    u_v = dwconv(up_v, conv_w[:, D_FF:], sm["conv_b"][:, D_FF:], "conv_val")
    (act,) = stage("glu", _f_glu, [u_g, u_v], ts=128, out_dtypes=[BF16])
    ff = mm(act, big["w_down"], "nn", "w_down", True, F32)
    x2, h3 = stage("post_mlp", _f_post, [x1, ff], [sm["mlp_post_norm"], sm["ple_norm"]], out_dtypes=[F32, BF16])

    t = mm(h3, big["w_ple_gate"], "nn", "w_ple_gate", True, F32)
    e = mm(p, big["w_ple"], "nt", "w_ple", False, F32)
    (rowloss,) = stage("loss", _f_out, [x2, t, e], [], [target])
    return jnp.sum(rowloss)


WEIGHTS = ["attn_pre_norm", "attn_post_norm", "w_in", "b_gate", "sinks", "q_a_norm", "w_uq", "kv_a_norm", "w_ukv",
           "w_branch_a", "w_branch_b", "w_out", "mlp_pre_norm", "mlp_post_norm", "w_up", "conv_w", "conv_b",
           "w_down", "ple_norm", "w_ple_gate", "w_ple"]


def kernel(x, p, positions, attn_pre_norm, attn_post_norm, w_in, b_gate, sinks, q_a_norm, w_uq, kv_a_norm, w_ukv, w_branch_a, w_branch_b, w_out, mlp_pre_norm, mlp_post_norm, w_up, conv_w, conv_b, w_down, ple_norm, w_ple_gate, w_ple, loss_target, m_attn_pre_norm, m_attn_post_norm, m_w_in, m_b_gate, m_sinks, m_q_a_norm, m_w_uq, m_kv_a_norm, m_w_ukv, m_w_branch_a, m_w_branch_b, m_w_out, m_mlp_pre_norm, m_mlp_post_norm, m_w_up, m_conv_w, m_conv_b, m_w_down, m_ple_norm, m_w_ple_gate, m_w_ple, v_attn_pre_norm, v_attn_post_norm, v_w_in, v_b_gate, v_sinks, v_q_a_norm, v_w_uq, v_kv_a_norm, v_w_ukv, v_w_branch_a, v_w_branch_b, v_w_out, v_mlp_pre_norm, v_mlp_post_norm, v_w_up, v_conv_w, v_conv_b, v_w_down, v_ple_norm, v_w_ple_gate, v_w_ple):
    given = dict(locals())
    s = x.shape[1]
    wts = {n: given[n][0] if given[n].ndim == 3 else given[n] for n in WEIGHTS}
    tables = _rope_tables(positions, s)
    local_loss, (grads, grad_x) = jax.value_and_grad(_local_loss, argnums=(0, 1))(
        wts, x[0], p[0, 0], tables, loss_target[0])
    loss = lax.psum(local_loss, AXES)

    outs = {"grad": [], "delta": [], "m": [], "v": []}
    for n in WEIGHTS:
        shape = given[n].shape
        w2 = wts[n].reshape(-1, shape[-1])
        g2 = grads[n].reshape(w2.shape)
        delta, new_m, new_v = _adamw(w2, g2, given["m_" + n].reshape(w2.shape), given["v_" + n].reshape(w2.shape),
                                     "adamw_" + n)
        outs["grad"].append(g2.reshape(shape))
        outs["delta"].append(delta.reshape(shape))
        outs["m"].append(new_m.reshape(shape))
        outs["v"].append(new_v.reshape(shape))
    return (loss, grad_x[None], *outs["grad"], *outs["delta"], *outs["m"], *outs["v"])
```

```python
import functools

import numpy as np
import jax
import jax.numpy as jnp
from jax import lax
from jax.experimental import pallas as pl
from jax.experimental.pallas import tpu as pltpu

F32 = jnp.float32
BF16 = jnp.bfloat16
MESH_ID = pl.DeviceIdType.MESH
AXES = ("x", "y", "c")
N_DEV = 8

D_MODEL = 1024
RMS_EPS = 1e-6
ROPE_THETA = 10000.0
SWA_BLOCK = 128
A_HEADS, A_KV_HEADS, A_HEAD_DIM = 8, 2, 64
A_GROUP = A_HEADS // A_KV_HEADS
B_HEADS, Q_LORA, KV_LORA, NOPE_DIM, ROPE_DIM, V_DIM = 8, 256, 128, 64, 32, 64
D_FF = 2816
CONV_W = 3
HEAD_PAD = 128

ADAM_LR, ADAM_B1, ADAM_B2, ADAM_EPS, ADAM_WD, ADAM_STEP = 0.001, 0.9, 0.999, 1e-08, 0.01, 10

VMEM_LIMIT = 48 * 1024 * 1024
MM_TM, MM_TN, MM_TK_TOKENS = 512, 1408, 1024
MM_VMEM_BUDGET = 36 * 1024 * 1024
FLASH_T = 512
CONV_TS, CONV_TC = 128, 2816


def _params(*sem):
    return pltpu.CompilerParams(dimension_semantics=sem, vmem_limit_bytes=VMEM_LIMIT)


def _pick(dim, cap, mult):
    best = None
    for t in range(mult, min(dim, cap) + 1, mult):
        if dim % t == 0:
            best = t
    return dim if best is None else best


def _divisors(dim, mult):
    return [t for t in range(mult, dim + 1, mult) if dim % t == 0] or [dim]


def _matmul_tiles(m, n, kdim, form, sizes):
    sa, sb, so = sizes
    tk = _pick(kdim, MM_TK_TOKENS, 128) if form == "tn" else kdim
    cap_m = MM_TN if form == "tn" else MM_TM
    best = None
    for tm in _divisors(m, 128):
        for tn in _divisors(n, 128):
            need = 2 * (tm * tk * sa + tk * tn * sb + tm * tn * so) + (tm * tn * 4 if tk != kdim else 0)
            if tm > cap_m or tn > MM_TN or need > MM_VMEM_BUDGET:
                continue
            if best is None or (tm * tn, tm) > (best[0] * best[1], best[0]):
                best = (tm, tn)
    return best[0], best[1], tk


def _matmul(a, b, form, *, out_dtype=F32, name):
    if form == "tn":
        (kdim, m), n = a.shape, b.shape[1]
    else:
        (m, kdim), n = a.shape, (b.shape[1] if form == "nn" else b.shape[0])
    sizes = (a.dtype.itemsize, b.dtype.itemsize, jnp.dtype(out_dtype).itemsize)
    tm, tn, tk = _matmul_tiles(m, n, kdim, form, sizes)
    nk = kdim // tk
    a_spec = (pl.BlockSpec((tk, tm), lambda i, j, k: (k, i)) if form == "tn"
              else pl.BlockSpec((tm, tk), lambda i, j, k: (i, k)))
    b_spec = (pl.BlockSpec((tn, tk), lambda i, j, k: (j, k)) if form == "nt"
              else pl.BlockSpec((tk, tn), lambda i, j, k: (k, j)))
    dims = (((0 if form == "tn" else 1,), (1 if form == "nt" else 0,)), ((), ()))

    def product(a_ref, b_ref):
        return lax.dot_general(a_ref[...].astype(BF16), b_ref[...].astype(BF16), dims, preferred_element_type=F32)

    if nk == 1:
        def body(a_ref, b_ref, o_ref):
            o_ref[...] = product(a_ref, b_ref).astype(o_ref.dtype)

        scratch = []
    else:
        def body(a_ref, b_ref, o_ref, acc_ref):
            k = pl.program_id(2)

            @pl.when(k == 0)
            def _():
                acc_ref[...] = jnp.zeros_like(acc_ref)

            acc_ref[...] += product(a_ref, b_ref)

            @pl.when(k == nk - 1)
            def _():
                o_ref[...] = acc_ref[...].astype(o_ref.dtype)

        scratch = [pltpu.VMEM((tm, tn), F32)]

    return pl.pallas_call(
        body, name=name, grid=(m // tm, n // tn, nk),
        in_specs=[a_spec, b_spec],
        out_specs=pl.BlockSpec((tm, tn), lambda i, j, k: (i, j)),
        out_shape=jax.ShapeDtypeStruct((m, n), out_dtype),
        scratch_shapes=scratch,
        compiler_params=_params("parallel", "parallel", "arbitrary"),
    )(a, b)


@functools.partial(jax.custom_vjp, nondiff_argnums=(2, 3, 4, 5))
def mm(a, w, form, name, need_da, out_dtype):
    return _matmul(a, w, form, out_dtype=out_dtype, name=name + "_fwd")


def _mm_fwd(a, w, form, name, need_da, out_dtype):
    return _matmul(a, w, form, out_dtype=out_dtype, name=name + "_fwd"), (a, w)


def _mm_bwd(form, name, need_da, out_dtype, res, ct):
    a, w = res
    if form == "nn":
        da = _matmul(ct, w, "nt", out_dtype=a.dtype, name=name + "_da") if need_da else jnp.zeros_like(a)
        dw = _matmul(a, ct, "tn", out_dtype=w.dtype, name=name + "_dw")
    else:
        da = _matmul(ct, w, "nn", out_dtype=a.dtype, name=name + "_da") if need_da else jnp.zeros_like(a)
        dw = _matmul(ct, a, "tn", out_dtype=w.dtype, name=name + "_dw")
    return da, dw


mm.defvjp(_mm_fwd, _mm_bwd)


def _pairs(bounds):
    return list(zip(bounds[:-1], bounds[1:]))


def _split(v, bounds):
    return [v[:, a:b] for a, b in _pairs(bounds)]


def stage(name, f, tiled, params=(), consts=(), splits=None, ts=256, out_dtypes=None):
    tiled, params, consts = tuple(tiled), tuple(params), tuple(consts)
    n_t, n_p, n_c = len(tiled), len(params), len(consts)
    s = tiled[0].shape[0]
    ts = min(ts, s)
    grid = (s // ts,)
    if splits is None:
        splits = [None] * n_t
    in_bounds = [(0, t.shape[1]) if b is None else tuple(b) for t, b in zip(tiled, splits)]

    def tile_aval(arr):
        return jax.ShapeDtypeStruct((ts, arr.shape[1]), arr.dtype)

    slab_avals = [[jax.ShapeDtypeStruct((ts, e - a), t.dtype) for a, e in _pairs(b)]
                  for t, b in zip(tiled, in_bounds)]
    out_avals = jax.eval_shape(f, slab_avals, list(params), [tile_aval(c) for c in consts])
    out_bounds = [tuple(np.cumsum([0] + [o.shape[1] for o in slabs]).tolist()) for slabs in out_avals]
    out_dtypes = [F32] * len(out_bounds) if out_dtypes is None else out_dtypes
    out_shapes = [jax.ShapeDtypeStruct((s, b[-1]), d) for b, d in zip(out_bounds, out_dtypes)]

    def row_spec(width):
        return pl.BlockSpec((ts, width), lambda i: (i, 0))

    def par_spec(arr):
        return pl.BlockSpec(arr.shape, lambda i: (0, 0))

    in_specs = ([row_spec(t.shape[1]) for t in tiled] + [par_spec(p) for p in params]
                + [row_spec(c.shape[1]) for c in consts])

    def load(refs):
        t = [_split(r[...], b) for r, b in zip(refs[:n_t], in_bounds)]
        p = [r[...] for r in refs[n_t:n_t + n_p]]
        c = [r[...] for r in refs[n_t + n_p:n_t + n_p + n_c]]
        return t, p, c

    def store(refs, values, bounds):
        for ref, slabs, b in zip(refs, values, bounds):
            for v, (a, e) in zip(slabs, _pairs(b)):
                ref[:, a:e] = v.astype(ref.dtype)

    def run_fwd(tiled, params, consts):
        def body(*refs):
            t, p, c = load(refs)
            store(refs[n_t + n_p + n_c:], f(t, p, c), out_bounds)

        return pl.pallas_call(
            body, name=name + "_fwd", grid=grid, in_specs=in_specs,
            out_specs=[row_spec(b[-1]) for b in out_bounds], out_shape=out_shapes,
            compiler_params=_params("parallel"),
        )(*tiled, *params, *consts)

    def run_bwd(tiled, params, consts, cts):
        n_in = n_t + n_p + n_c
        n_o = len(out_bounds)

        def body(*refs):
            t, p, c = load(refs)
            g = [_split(r[...].astype(F32), b) for r, b in zip(refs[n_in:n_in + n_o], out_bounds)]
            _, pull = jax.vjp(lambda t_, p_: f(t_, p_, c), t, p)
            dt, dp = pull(g)
            store(refs[n_in + n_o:n_in + n_o + n_t], dt, in_bounds)
            first = pl.program_id(0) == 0
            for ref, d in zip(refs[n_in + n_o + n_t:], dp):
                @pl.when(first)
                def _(ref=ref):
                    ref[...] = jnp.zeros_like(ref)

                ref[...] += d

        res = pl.pallas_call(
            body, name=name + "_bwd", grid=grid,
            in_specs=in_specs + [row_spec(b[-1]) for b in out_bounds],
            out_specs=[row_spec(t.shape[1]) for t in tiled] + [par_spec(p) for p in params],
            out_shape=[jax.ShapeDtypeStruct(t.shape, t.dtype) for t in tiled]
                      + [jax.ShapeDtypeStruct(p.shape, F32) for p in params],
            compiler_params=_params("arbitrary"),
        )(*tiled, *params, *consts, *cts)
        return tuple(res[:n_t]), tuple(res[n_t:])

    @jax.custom_vjp
    def op(tiled, params, consts):
        return tuple(run_fwd(tiled, params, consts))

    def op_fwd(tiled, params, consts):
        return op(tiled, params, consts), (tiled, params, consts)

    def op_bwd(res, cts):
        tiled, params, consts = res
        dt, dp = run_bwd(tiled, params, consts, cts)
        return dt, dp, tuple(jnp.zeros_like(c) for c in consts)

    op.defvjp(op_fwd, op_bwd)
    return op(tiled, params, consts)


def _rms(t, g):
    return t * lax.rsqrt(jnp.mean(t * t, axis=-1, keepdims=True) + RMS_EPS) * g


@functools.partial(jax.custom_vjp, nondiff_argnums=(1,))
def _lane_roll(t, shift):
    return pltpu.roll(t, shift % t.shape[-1], t.ndim - 1)


def _lane_roll_fwd(t, shift):
    return _lane_roll(t, shift), None


def _lane_roll_bwd(shift, _, ct):
    return (pltpu.roll(ct, (-shift) % ct.shape[-1], ct.ndim - 1),)


_lane_roll.defvjp(_lane_roll_fwd, _lane_roll_bwd)


def _rope_lanes(t, tables, half):
    reps = t.shape[1] // tables[0].shape[1]
    c, s_lo, s_hi = [jnp.concatenate([tb] * reps, axis=1) if reps > 1 else tb for tb in tables]
    return t * c + _lane_roll(t, -half) * s_lo + _lane_roll(t, half) * s_hi


def _f_prenorm(t, p, c):
    return [[_rms(t[0][0], p[0])]]


def _f_prep(t, p, c):
    qa, ka, va, cq, ckv, kr = t[0]
    return [[_rope_lanes(qa, c[0:3], A_HEAD_DIM // 2)], [_rope_lanes(ka, c[0:3], A_HEAD_DIM // 2)], [va],
            [_rms(cq, p[0])], [_rms(ckv, p[1])], [_rope_lanes(kr, c[3:6], ROPE_DIM // 2)]]


def _f_qrope(t, p, c):
    return [[_rope_lanes(t[0][0], c, ROPE_DIM // 2)]]


def _f_kv(t, p, c):
    (k_nope, v), (k_pe,) = t
    return [[k_nope + jnp.concatenate([k_pe] * B_HEADS, axis=1)], [v]]


def _f_gate(t, p, c):
    (ga, gb), (pa,), (pb,) = t
    ba, bb = p
    return [[jax.nn.sigmoid(ga + ba) * pa + jax.nn.sigmoid(gb + bb) * pb]]


def _f_post(t, p, c):
    x1 = t[0][0] + _rms(t[1][0], p[0])
    return [[x1], [_rms(x1, p[1])]]


def _f_glu(t, p, c):
    return [[jax.nn.gelu(t[0][0], approximate=True) * t[1][0]]]


def _f_out(t, p, c):
    y = t[0][0] + jax.nn.sigmoid(t[1][0]) * t[2][0]
    err = y - c[0]
    return [[0.5 * jnp.mean(err * err, axis=-1, keepdims=True)]]


def _shift_down(cur, prev, has_prev):
    rows = cur.shape[0]
    row = lax.broadcasted_iota(jnp.int32, cur.shape, 0)
    m1 = prev[7:8, :] * has_prev
    m2 = prev[6:7, :] * has_prev
    u1 = jnp.where(row >= 1, pltpu.roll(cur, 1, 0), m1)
    u2 = jnp.where(row >= 2, pltpu.roll(cur, 2, 0), jnp.where(row == 1, m1, m2))
    return u1, u2


def _shift_up(cur, nxt, has_next):
    rows = cur.shape[0]
    row = lax.broadcasted_iota(jnp.int32, cur.shape, 0)
    n0 = nxt[0:1, :] * has_next
    n1 = nxt[1:2, :] * has_next
    d1 = jnp.where(row < rows - 1, pltpu.roll(cur, rows - 1, 0), n0)
    d2 = jnp.where(row < rows - 2, pltpu.roll(cur, rows - 2, 0), jnp.where(row == rows - 2, n0, n1))
    return d1, d2


def _conv_tiles(s, ch):
    ts = min(CONV_TS, s)
    tc = _pick(ch, CONV_TC, 128)
    return ts, tc, s // ts, ch // tc


def _conv_fwd_call(up, w, b, name):
    s, ch = up.shape
    ts, tc, nt, nc = _conv_tiles(s, ch)
    hb = ts // 8

    def body(cur_ref, prev_ref, w_ref, b_ref, o_ref):
        cur = cur_ref[...]
        u1, u2 = _shift_down(cur, prev_ref[...], (pl.program_id(1) > 0).astype(F32))
        o_ref[...] = w_ref[2:3, :] * cur + w_ref[1:2, :] * u1 + w_ref[0:1, :] * u2 + b_ref[...]

    return pl.pallas_call(
        body, name=name + "_fwd", grid=(nc, nt),
        in_specs=[pl.BlockSpec((ts, tc), lambda c, i: (i, c)),
                  pl.BlockSpec((8, tc), lambda c, i: (jnp.maximum(i * hb - 1, 0), c)),
                  pl.BlockSpec((CONV_W, tc), lambda c, i: (0, c)),
                  pl.BlockSpec((1, tc), lambda c, i: (0, c))],
        out_specs=pl.BlockSpec((ts, tc), lambda c, i: (i, c)),
        out_shape=jax.ShapeDtypeStruct((s, ch), F32),
        compiler_params=_params("parallel", "parallel"),
    )(up, up, w, b)


def _conv_bwd_call(up, w, du, name):
    s, ch = up.shape
    ts, tc, nt, nc = _conv_tiles(s, ch)
    hb = ts // 8

    def body(cur_ref, prev_ref, w_ref, du_ref, nxt_ref, dup_ref, dw_ref, db_ref):
        i = pl.program_id(1)
        cur, du = cur_ref[...], du_ref[...]
        u1, u2 = _shift_down(cur, prev_ref[...], (i > 0).astype(F32))
        d1, d2 = _shift_up(du, nxt_ref[...], (i < nt - 1).astype(F32))
        dup_ref[...] = w_ref[2:3, :] * du + w_ref[1:2, :] * d1 + w_ref[0:1, :] * d2

        @pl.when(i == 0)
        def _():
            dw_ref[...] = jnp.zeros_like(dw_ref)
            db_ref[...] = jnp.zeros_like(db_ref)

        dw_ref[0:1, :] += jnp.sum(du * u2, axis=0, keepdims=True)
        dw_ref[1:2, :] += jnp.sum(du * u1, axis=0, keepdims=True)
        dw_ref[2:3, :] += jnp.sum(du * cur, axis=0, keepdims=True)
        db_ref[...] += jnp.sum(du, axis=0, keepdims=True)

    return pl.pallas_call(
        body, name=name + "_bwd", grid=(nc, nt),
        in_specs=[pl.BlockSpec((ts, tc), lambda c, i: (i, c)),
                  pl.BlockSpec((8, tc), lambda c, i: (jnp.maximum(i * hb - 1, 0), c)),
                  pl.BlockSpec((CONV_W, tc), lambda c, i: (0, c)),
                  pl.BlockSpec((ts, tc), lambda c, i: (i, c)),
                  pl.BlockSpec((8, tc), lambda c, i: (jnp.minimum((i + 1) * hb, s // 8 - 1), c))],
        out_specs=[pl.BlockSpec((ts, tc), lambda c, i: (i, c)),
                   pl.BlockSpec((CONV_W, tc), lambda c, i: (0, c)),
                   pl.BlockSpec((1, tc), lambda c, i: (0, c))],
        out_shape=[jax.ShapeDtypeStruct((s, ch), F32), jax.ShapeDtypeStruct((CONV_W, ch), F32),
                   jax.ShapeDtypeStruct((1, ch), F32)],
        compiler_params=_params("parallel", "arbitrary"),
    )(up, up, w, du, du)


@functools.partial(jax.custom_vjp, nondiff_argnums=(3,))
def dwconv(up, w, b, name):
    return _conv_fwd_call(up, w, b, name)


def _dwconv_fwd(up, w, b, name):
    return _conv_fwd_call(up, w, b, name), (up, w)


def _dwconv_bwd(name, res, ct):
    up, w = res
    return tuple(_conv_bwd_call(up, w, ct, name))


dwconv.defvjp(_dwconv_fwd, _dwconv_bwd)


SWA_ROWS = A_GROUP * SWA_BLOCK


def _swa_sink_rows(sink_ref, g):
    return jnp.concatenate([jnp.full((SWA_BLOCK, 1), sink_ref[g * A_GROUP + h], F32) for h in range(A_GROUP)], axis=0)


def _swa_probs(q, kp, kc, sink, prev_off):
    scale = A_HEAD_DIM ** -0.5
    nt = (((1,), (1,)), ((), ()))
    sp = lax.dot_general(q, kp, nt, preferred_element_type=F32) * scale
    sc = lax.dot_general(q, kc, nt, preferred_element_type=F32) * scale
    qi = lax.broadcasted_iota(jnp.int32, sp.shape, 0) & (SWA_BLOCK - 1)
    kj = lax.broadcasted_iota(jnp.int32, sp.shape, 1)
    sp = jnp.where(kj > qi + prev_off, sp, -jnp.inf)
    sc = jnp.where(kj <= qi, sc, -jnp.inf)
    m = jnp.maximum(jnp.maximum(jnp.max(sp, axis=-1, keepdims=True), jnp.max(sc, axis=-1, keepdims=True)), sink)
    ep, ec, es = jnp.exp(sp - m), jnp.exp(sc - m), jnp.exp(sink - m)
    den = jnp.sum(ep, axis=-1, keepdims=True) + jnp.sum(ec, axis=-1, keepdims=True) + es
    return ep / den, ec / den, es / den


MLA_SCALE = (NOPE_DIM + ROPE_DIM) ** -0.5
EXP2_SCALE = MLA_SCALE * float(np.log2(np.e))
NT_DIMS = (((1,), (1,)), ((), ()))
TN_DIMS = (((0,), (0,)), ((), ()))


LANES = 128
HALF = LANES // 2


def _low_half(shape):
    return lax.broadcasted_iota(jnp.int32, shape, len(shape) - 1) < HALF


def _dup_half(x, g):
    xf = x.astype(F32)
    keep = _low_half(xf.shape) if g == 0 else jnp.logical_not(_low_half(xf.shape))
    xm = jnp.where(keep, xf, 0.0)
    return (xm + pltpu.roll(xm, HALF, 1)).astype(x.dtype)


def _fold_half(r, g):
    total = r + pltpu.roll(r, HALF, 1)
    keep = _low_half(r.shape) if g == 0 else jnp.logical_not(_low_half(r.shape))
    return jnp.where(keep, total, 0.0)


def _swa_stack_heads(ref, g):
    parts = []
    for tile in range(2):
        slab = ref[:, (2 * g + tile) * LANES:(2 * g + tile + 1) * LANES]
        low = _low_half(slab.shape)
        parts += [jnp.where(low, slab, jnp.zeros_like(slab)), jnp.where(low, jnp.zeros_like(slab), slab)]
    return jnp.concatenate(parts, axis=0)


def _swa_unstack_heads(ref, g, rows):
    for tile in range(2):
        a = rows[(2 * tile) * SWA_BLOCK:(2 * tile + 1) * SWA_BLOCK]
        b = rows[(2 * tile + 1) * SWA_BLOCK:(2 * tile + 2) * SWA_BLOCK]
        ref[:, (2 * g + tile) * LANES:(2 * g + tile + 1) * LANES] = jnp.where(_low_half(a.shape), a, b).astype(ref.dtype)


def _swa_nat_specs():
    blk = SWA_BLOCK
    q_spec = pl.BlockSpec((blk, A_HEADS * A_HEAD_DIM), lambda n: (n, 0))
    prev_spec = pl.BlockSpec((blk, LANES), lambda n: (jnp.maximum(n - 1, 0), 0))
    cur_spec = pl.BlockSpec((blk, LANES), lambda n: (n, 0))
    return q_spec, prev_spec, cur_spec, pl.BlockSpec(memory_space=pltpu.SMEM)


def _swa_nat_fwd_call(q, k, v, sinks):
    s = q.shape[0]
    q_spec, prev_spec, cur_spec, sink_spec = _swa_nat_specs()

    def body(q_ref, kp_ref, kc_ref, vp_ref, vc_ref, sink_ref, o_ref):
        prev_off = jnp.where(pl.program_id(0) > 0, 0, SWA_BLOCK)
        for g in range(A_KV_HEADS):
            kp, kc = _dup_half(kp_ref[...], g), _dup_half(kc_ref[...], g)
            vp, vc = _dup_half(vp_ref[...], g), _dup_half(vc_ref[...], g)
            pp, pc, _ = _swa_probs(_swa_stack_heads(q_ref, g), kp, kc, _swa_sink_rows(sink_ref, g), prev_off)
            out = (jnp.dot(pp.astype(BF16), vp, preferred_element_type=F32)
                   + jnp.dot(pc.astype(BF16), vc, preferred_element_type=F32))
            _swa_unstack_heads(o_ref, g, out)

    return pl.pallas_call(
        body, name="swa_fwd", grid=(s // SWA_BLOCK,),
        in_specs=[q_spec, prev_spec, cur_spec, prev_spec, cur_spec, sink_spec],
        out_specs=q_spec, out_shape=jax.ShapeDtypeStruct(q.shape, BF16),
        compiler_params=_params("parallel"),
    )(q, k, k, v, v, sinks)


def _swa_nat_bwd_call(q, k, v, sinks, do):
    s = q.shape[0]
    q_spec, prev_spec, cur_spec, sink_spec = _swa_nat_specs()
    scale = A_HEAD_DIM ** -0.5
    dsink_spec = pl.BlockSpec((A_KV_HEADS, SWA_ROWS, 1), lambda n: (0, 0, 0))

    def body(q_ref, kp_ref, kc_ref, vp_ref, vc_ref, sink_ref, do_ref,
             dq_ref, dkp_ref, dkc_ref, dvp_ref, dvc_ref, dsink_ref):
        n = pl.program_id(0)
        prev_off = jnp.where(n > 0, 0, SWA_BLOCK)

        @pl.when(n == 0)
        def _():
            dsink_ref[...] = jnp.zeros_like(dsink_ref)

        totals = [jnp.zeros((SWA_BLOCK, LANES), F32) for _ in range(4)]
        for g in range(A_KV_HEADS):
            kp, kc = _dup_half(kp_ref[...], g), _dup_half(kc_ref[...], g)
            vp, vc = _dup_half(vp_ref[...], g), _dup_half(vc_ref[...], g)
            qb = _swa_stack_heads(q_ref, g)
            dob = _swa_stack_heads(do_ref, g)
            pp, pc, ps = _swa_probs(qb, kp, kc, _swa_sink_rows(sink_ref, g), prev_off)
            ppb, pcb = pp.astype(BF16), pc.astype(BF16)
            out = jnp.dot(ppb, vp, preferred_element_type=F32) + jnp.dot(pcb, vc, preferred_element_type=F32)
            delta = jnp.sum(dob.astype(F32) * out, axis=-1, keepdims=True)
            dsp = (pp * (lax.dot_general(dob, vp, NT_DIMS, preferred_element_type=F32) - delta)).astype(BF16)
            dsc = (pc * (lax.dot_general(dob, vc, NT_DIMS, preferred_element_type=F32) - delta)).astype(BF16)
            dsink_ref[g] += -ps * delta
            dq = (jnp.dot(dsp, kp, preferred_element_type=F32) + jnp.dot(dsc, kc, preferred_element_type=F32)) * scale
            _swa_unstack_heads(dq_ref, g, dq)
            parts = [lax.dot_general(dsp, qb, TN_DIMS, preferred_element_type=F32) * scale,
                     lax.dot_general(dsc, qb, TN_DIMS, preferred_element_type=F32) * scale,
                     lax.dot_general(ppb, dob, TN_DIMS, preferred_element_type=F32),
                     lax.dot_general(pcb, dob, TN_DIMS, preferred_element_type=F32)]
            totals = [tot + _fold_half(r, g) for tot, r in zip(totals, parts)]
        dkp_ref[...], dkc_ref[...], dvp_ref[...], dvc_ref[...] = totals

    kv_shape = jax.ShapeDtypeStruct(k.shape, F32)
    return pl.pallas_call(
        body, name="swa_bwd", grid=(s // SWA_BLOCK,),
        in_specs=[q_spec, prev_spec, cur_spec, prev_spec, cur_spec, sink_spec, q_spec],
        out_specs=[q_spec, cur_spec, cur_spec, cur_spec, cur_spec, dsink_spec],
        out_shape=[jax.ShapeDtypeStruct(q.shape, q.dtype), kv_shape, kv_shape, kv_shape, kv_shape,
                   jax.ShapeDtypeStruct((A_KV_HEADS, SWA_ROWS, 1), F32)],
        compiler_params=_params("arbitrary"),
    )(q, k, k, v, v, sinks, do)


@jax.custom_vjp
def swa_nat(q, k, v, sinks):
    return _swa_nat_fwd_call(q, k, v, sinks)


def _swa_nat_fwd(q, k, v, sinks):
    return _swa_nat_fwd_call(q, k, v, sinks), (q, k, v, sinks)


def _swa_nat_bwd(res, do):
    q, k, v, sinks = res
    dq, dkp, dkc, dvp, dvc, dsink = _swa_nat_bwd_call(q, k, v, sinks, do)

    def fold(prev_part, cur_part):
        shifted = jnp.concatenate([prev_part[SWA_BLOCK:], jnp.zeros_like(prev_part[:SWA_BLOCK])], axis=0)
        return (cur_part + shifted).astype(k.dtype)

    dsinks = jnp.sum(dsink.reshape(A_HEADS, SWA_BLOCK), axis=1)
    return dq, fold(dkp, dkc), fold(dvp, dvc), dsinks


swa_nat.defvjp(_swa_nat_fwd, _swa_nat_bwd)

N_PAIR = B_HEADS // 2


def _flash_nat_fwd_call(q, k, v):
    s = q.shape[0]
    t = min(FLASH_T, s)
    nb = s // t
    d = LANES

    def body(q_ref, k_ref, v_ref, o_ref, lse_ref, vt_ref, m_ref, l_ref, acc_ref):
        i = pl.program_id(1)

        @pl.when(i == 0)
        def _():
            for hh in range(2):
                for chunk in range(nb):
                    rows = slice(chunk * t, (chunk + 1) * t)
                    vt_ref[hh, :, rows] = v_ref[rows, hh * d:(hh + 1) * d].T

        outs = []
        for hh in range(2):
            qb = q_ref[:, hh * d:(hh + 1) * d]
            m_ref[...] = jnp.full_like(m_ref, -jnp.inf)
            l_ref[...] = jnp.zeros_like(l_ref)
            acc_ref[...] = jnp.zeros_like(acc_ref)

            def step(j, on_diagonal, hh=hh, qb=qb):
                keys = pl.ds(pl.multiple_of(j * t, t), t)
                sc_t = lax.dot_general(k_ref[keys, hh * d:(hh + 1) * d], qb, NT_DIMS, preferred_element_type=F32)
                if on_diagonal:
                    key = lax.broadcasted_iota(jnp.int32, (t, t), 0)
                    qry = lax.broadcasted_iota(jnp.int32, (t, t), 1)
                    sc_t = jnp.where(qry >= key, sc_t, -jnp.inf)
                m_old = m_ref[...]
                m_new = jnp.maximum(m_old, jnp.max(sc_t, axis=0, keepdims=True))
                alpha = jnp.exp2((m_old - m_new) * EXP2_SCALE)
                p_t = jnp.exp2((sc_t - m_new) * EXP2_SCALE)
                l_ref[...] = alpha * l_ref[...] + jnp.sum(p_t, axis=0, keepdims=True)
                acc_ref[...] = alpha * acc_ref[...] + jnp.dot(vt_ref[hh, :, keys], p_t.astype(BF16),
                                                              preferred_element_type=F32)
                m_ref[...] = m_new

            def below(j, carry, step=step):
                step(j, False)
                return carry

            lax.fori_loop(0, i, below, 0)
            step(i, True)
            outs.append((acc_ref[...] / l_ref[...]).T)
            lse_ref[hh] = m_ref[...] * EXP2_SCALE + jnp.log2(l_ref[...])
        o_ref[...] = (outs[0] + pltpu.roll(outs[1], HALF, 1)).astype(o_ref.dtype)

    return pl.pallas_call(
        body, name="mla_fwd", grid=(N_PAIR, nb),
        in_specs=[pl.BlockSpec((t, 2 * d), lambda p, i: (i, p)),
                  pl.BlockSpec((s, 2 * d), lambda p, i: (0, p)),
                  pl.BlockSpec((s, 2 * d), lambda p, i: (0, p))],
        out_specs=[pl.BlockSpec((t, d), lambda p, i: (i, p)),
                   pl.BlockSpec((2, 1, t), lambda p, i: (p, 0, i))],
        out_shape=[jax.ShapeDtypeStruct((s, N_PAIR * d), BF16), jax.ShapeDtypeStruct((B_HEADS, 1, s), F32)],
        scratch_shapes=[pltpu.VMEM((2, d, s), BF16), pltpu.VMEM((1, t), F32), pltpu.VMEM((1, t), F32),
                        pltpu.VMEM((d, t), F32)],
        compiler_params=_params("parallel", "arbitrary"),
    )(q, k, v)


def _flash_nat_delta_call(o, do):
    s, w = o.shape
    t = min(FLASH_T, s)

    def body(o_ref, do_ref, out_ref):
        prod = o_ref[...].astype(F32) * do_ref[...].astype(F32)
        lane = lax.broadcasted_iota(jnp.int32, (w, LANES), 0) // V_DIM
        head = lax.broadcasted_iota(jnp.int32, (w, LANES), 1)
        out_ref[...] = jnp.dot(prod, (lane == head).astype(F32), precision=lax.Precision.HIGHEST,
                               preferred_element_type=F32)

    spec = pl.BlockSpec((t, w), lambda i: (i, 0))
    return pl.pallas_call(
        body, name="mla_delta", grid=(s // t,), in_specs=[spec, spec],
        out_specs=pl.BlockSpec((t, LANES), lambda i: (i, 0)),
        out_shape=jax.ShapeDtypeStruct((s, LANES), F32), compiler_params=_params("parallel"),
    )(o, do)


def _flash_nat_bwd_call(q, k, v, lse_row, delta_row, do):
    s = q.shape[0]
    t = min(FLASH_T, s)
    nb = s // t
    d = LANES

    def body(q_ref, k_ref, v_ref, lse_ref, delta_ref, do_ref, dq_ref, dk_ref, dv_ref, dq_acc, dk_acc, dv_acc):
        j = pl.program_id(1)

        @pl.when(j == 0)
        def _():
            dq_acc[...] = jnp.zeros_like(dq_acc)

        for hh in range(2):
            kb, vb = k_ref[:, hh * d:(hh + 1) * d], v_ref[:, hh * d:(hh + 1) * d]
            dk_acc[...] = jnp.zeros_like(dk_acc)
            dv_acc[...] = jnp.zeros_like(dv_acc)

            def step(i, on_diagonal, hh=hh, kb=kb, vb=vb):
                rows = pl.ds(pl.multiple_of(i * t, t), t)
                qb = q_ref[rows, hh * d:(hh + 1) * d]
                do_pair = do_ref[rows, :].astype(F32)
                do_h = do_pair if hh == 0 else pltpu.roll(do_pair, HALF, 1)
                dob = jnp.where(_low_half(do_h.shape), do_h, 0.0).astype(BF16)
                sc_t = lax.dot_general(kb, qb, NT_DIMS, preferred_element_type=F32)
                p_t = jnp.exp2(sc_t * EXP2_SCALE - lse_ref[hh, :, rows])
                if on_diagonal:
                    key = lax.broadcasted_iota(jnp.int32, (t, t), 0)
                    qry = lax.broadcasted_iota(jnp.int32, (t, t), 1)
                    p_t = jnp.where(qry >= key, p_t, 0.0)
                dp_t = lax.dot_general(vb, dob, NT_DIMS, preferred_element_type=F32)
                ds_t = (p_t * (dp_t - delta_ref[hh, :, rows])).astype(BF16)
                dv_acc[...] += jnp.dot(p_t.astype(BF16), dob, preferred_element_type=F32)
                dk_acc[...] += jnp.dot(ds_t, qb, preferred_element_type=F32)
                dq_acc[hh, rows, :] += lax.dot_general(ds_t, kb, TN_DIMS, preferred_element_type=F32)

            def above(i, carry, step=step):
                step(i, False)
                return carry

            step(j, True)
            lax.fori_loop(j + 1, nb, above, 0)
            dk_ref[:, hh * d:(hh + 1) * d] = (dk_acc[...] * MLA_SCALE).astype(dk_ref.dtype)
            dv_ref[:, hh * d:(hh + 1) * d] = dv_acc[...].astype(dv_ref.dtype)

        @pl.when(j == nb - 1)
        def _():
            for hh in range(2):
                dq_ref[:, hh * d:(hh + 1) * d] = (dq_acc[hh] * MLA_SCALE).astype(dq_ref.dtype)

    full_spec = pl.BlockSpec((s, 2 * d), lambda p, j: (0, p))
    tile_spec = pl.BlockSpec((t, 2 * d), lambda p, j: (j, p))
    row_spec = pl.BlockSpec((2, 1, s), lambda p, j: (p, 0, 0))
    return pl.pallas_call(
        body, name="mla_bwd", grid=(N_PAIR, nb),
        in_specs=[full_spec, tile_spec, tile_spec, row_spec, row_spec, pl.BlockSpec((s, d), lambda p, j: (0, p))],
        out_specs=[full_spec, tile_spec, tile_spec],
        out_shape=[jax.ShapeDtypeStruct(q.shape, q.dtype)] * 3,
        scratch_shapes=[pltpu.VMEM((2, s, d), F32), pltpu.VMEM((t, d), F32), pltpu.VMEM((t, d), F32)],
        compiler_params=_params("parallel", "arbitrary"),
    )(q, k, v, lse_row, delta_row, do)


@jax.custom_vjp
def flash_nat(q, k, v):
    return _flash_nat_fwd_call(q, k, v)[0]


def _flash_nat_fwd(q, k, v):
    o, lse = _flash_nat_fwd_call(q, k, v)
    return o, (q, k, v, o, lse)


def _flash_nat_bwd(res, do):
    q, k, v, o, lse = res
    delta = _flash_nat_delta_call(o, do)[:, :B_HEADS].T.reshape(B_HEADS, 1, q.shape[0])
    return tuple(_flash_nat_bwd_call(q, k, v, lse, delta, do))


flash_nat.defvjp(_flash_nat_fwd, _flash_nat_bwd)


HBM_SPEC = pl.BlockSpec(memory_space=pltpu.HBM)


def _allgather(shards, name):
    n_arr = len(shards)

    def body(*refs):
        x_refs, out_refs = refs[:n_arr], refs[n_arr:2 * n_arr]
        send_sems, recv_sems, local_sems = refs[2 * n_arr:]
        x, y, c = lax.axis_index("x"), lax.axis_index("y"), lax.axis_index("c")
        me, sibling = (x, y, c), (x, y, 1 - c)
        chips = [(1 - x, y), (x, 1 - y), (1 - x, 1 - y)]
        arrays = range(n_arr)

        def rows(a, px, py, pc):
            return out_refs[a].at[4 * px + 2 * py + pc]

        def copy(a, k, block, to, src=None):
            return pltpu.make_async_remote_copy(
                src_ref=rows(a, *block) if src is None else src, dst_ref=rows(a, *block),
                send_sem=send_sems.at[k, a], recv_sem=recv_sems.at[k, a], device_id=to, device_id_type=MESH_ID)

        mine = [pltpu.make_async_copy(x_refs[a], rows(a, *me), local_sems.at[a]) for a in arrays]
        for cp in mine:
            cp.start()
        first = []
        for a in arrays:
            first.append(copy(a, 0, me, sibling, src=x_refs[a]))
            first += [copy(a, 1 + j, me, (*chip, c), src=x_refs[a]) for j, chip in enumerate(chips)]
        for cp in first:
            cp.start()
        passed = []
        for j, chip in enumerate(chips):
            for a in arrays:
                copy(a, 1 + j, (*chip, c), me).wait_recv()
                passed.append(copy(a, 4 + j, (*chip, c), sibling))
                passed[-1].start()
        for a in arrays:
            copy(a, 0, sibling, me).wait_recv()
        for j, chip in enumerate(chips):
            for a in arrays:
                copy(a, 4 + j, (*chip, 1 - c), me).wait_recv()
        for cp in first + passed:
            cp.wait_send()
        for cp in mine:
            cp.wait()

    return pl.pallas_call(
        body, name=name, out_shape=[jax.ShapeDtypeStruct((N_DEV,) + s.shape, s.dtype) for s in shards],
        in_specs=[HBM_SPEC] * n_arr, out_specs=[HBM_SPEC] * n_arr,
        scratch_shapes=[pltpu.SemaphoreType.DMA((7, n_arr)), pltpu.SemaphoreType.DMA((7, n_arr)),
                        pltpu.SemaphoreType.DMA((n_arr,))],
    )(*shards)


N_CHIP = 4


def _exchange_sibling(parts, name):
    n_arr = len(parts)

    def body(*refs):
        in_refs, recv_refs = refs[:n_arr], refs[n_arr:2 * n_arr]
        send_sems, recv_sems = refs[2 * n_arr:]
        x, y, c = lax.axis_index("x"), lax.axis_index("y"), lax.axis_index("c")
        copies = []
        for a in range(n_arr):
            for q in range(N_CHIP):
                copies.append(pltpu.make_async_remote_copy(
                    src_ref=in_refs[a].at[2 * q + 1 - c], dst_ref=recv_refs[a].at[q],
                    send_sem=send_sems.at[q, a], recv_sem=recv_sems.at[q, a],
                    device_id=(x, y, 1 - c), device_id_type=MESH_ID))
        for cp in copies:
            cp.start()
        for cp in copies:
            cp.wait()

    return pl.pallas_call(
        body, name=name, out_shape=[jax.ShapeDtypeStruct((N_CHIP,) + p.shape[1:], p.dtype) for p in parts],
        in_specs=[HBM_SPEC] * n_arr, out_specs=[HBM_SPEC] * n_arr,
        scratch_shapes=[pltpu.SemaphoreType.DMA((N_CHIP, n_arr)), pltpu.SemaphoreType.DMA((N_CHIP, n_arr))],
    )(*parts)


def _exchange_chips(parts, name):
    n_arr = len(parts)

    def body(*refs):
        in_refs, out_refs = refs[:n_arr], refs[n_arr:2 * n_arr]
        send_sems, recv_sems, local_sems = refs[2 * n_arr:]
        x, y, c = lax.axis_index("x"), lax.axis_index("y"), lax.axis_index("c")
        me = 2 * x + y
        copies = [pltpu.make_async_copy(in_refs[a].at[me], out_refs[a].at[me], local_sems.at[a]) for a in range(n_arr)]
        for k in range(1, N_CHIP):
            px = 1 - x if k & 2 else x
            py = 1 - y if k & 1 else y
            for a in range(n_arr):
                copies.append(pltpu.make_async_remote_copy(
                    src_ref=in_refs[a].at[2 * px + py], dst_ref=out_refs[a].at[me],
                    send_sem=send_sems.at[k - 1, a], recv_sem=recv_sems.at[k - 1, a],
                    device_id=(px, py, c), device_id_type=MESH_ID))
        for cp in copies:
            cp.start()
        for cp in copies:
            cp.wait()

    return pl.pallas_call(
        body, name=name, out_shape=[jax.ShapeDtypeStruct(p.shape, p.dtype) for p in parts],
        in_specs=[HBM_SPEC] * n_arr, out_specs=[HBM_SPEC] * n_arr,
        scratch_shapes=[pltpu.SemaphoreType.DMA((N_CHIP - 1, n_arr)), pltpu.SemaphoreType.DMA((N_CHIP - 1, n_arr)),
                        pltpu.SemaphoreType.DMA((n_arr,))],
    )(*parts)


def _row_tile(r, ccols, blocks):
    cap = max(16, (2 * 1024 * 1024) // (4 * ccols * blocks))
    return _pick(r, cap, 16)


def _pair_add(mine, theirs, my_c, name):
    _, r, ccols = mine.shape
    tr = _row_tile(r, ccols, 1)

    def body(c_ref, a_ref, b_ref, o_ref):
        o_ref[...] = (a_ref[...].astype(F32) + b_ref[...].astype(F32)).astype(o_ref.dtype)

    spec = pl.BlockSpec((None, tr, ccols), lambda q, i, c_ref: (q, i, 0))
    return pl.pallas_call(
        body, name=name,
        grid_spec=pltpu.PrefetchScalarGridSpec(
            num_scalar_prefetch=1, grid=(N_CHIP, r // tr),
            in_specs=[pl.BlockSpec((None, tr, ccols), lambda q, i, c_ref: (2 * q + c_ref[0], i, 0)), spec],
            out_specs=spec),
        out_shape=jax.ShapeDtypeStruct(theirs.shape, theirs.dtype),
        compiler_params=_params("parallel", "parallel"),
    )(my_c, mine, theirs)


def _sum_blocks(parts, name):
    nb, r, ccols = parts.shape
    tr = _row_tile(r, ccols, nb)

    def body(p_ref, o_ref):
        acc = p_ref[0].astype(F32)
        for i in range(1, nb):
            acc = acc + p_ref[i].astype(F32)
        o_ref[...] = acc

    return pl.pallas_call(
        body, name=name, grid=(r // tr,),
        in_specs=[pl.BlockSpec((nb, tr, ccols), lambda i: (0, i, 0))],
        out_specs=pl.BlockSpec((tr, ccols), lambda i: (i, 0)),
        out_shape=jax.ShapeDtypeStruct((r, ccols), F32),
        compiler_params=_params("parallel"),
    )(parts)


def _gather_wire(shards, wire_dtypes):
    return tuple(_allgather([s.astype(d) for s, d in zip(shards, wire_dtypes)], "weights_allgather"))


@functools.partial(jax.custom_vjp, nondiff_argnums=(1,))
def fsdp_gather(shards, wire_dtypes):
    return _gather_wire(shards, wire_dtypes)


def _fsdp_gather_fwd(shards, wire_dtypes):
    return _gather_wire(shards, wire_dtypes), None


def _fsdp_gather_bwd(wire_dtypes, _, cts):
    received = _exchange_sibling(list(cts), "grads_exchange_sibling")
    my_c = lax.axis_index("c").astype(jnp.int32).reshape(1)
    pair_sums = [_pair_add(m, r, my_c, "grad_pair_add_%d" % i) for i, (m, r) in enumerate(zip(cts, received))]
    chip_parts = _exchange_chips(pair_sums, "grads_exchange_chips")
    return (tuple(_sum_blocks(r, "grad_sum_%d" % i) for i, r in enumerate(chip_parts)),)


fsdp_gather.defvjp(_fsdp_gather_fwd, _fsdp_gather_bwd)


@jax.custom_vjp
def replicated(vec):
    return vec


def _replicated_fwd(vec):
    return vec, None


def _replicated_bwd(_, ct):
    return (_sum_blocks(_allgather([ct], "small_grad_allgather")[0], "small_grad_sum"),)


replicated.defvjp(_replicated_fwd, _replicated_bwd)


def _adamw(w, g, m, v, name):
    rows, cols = w.shape
    tr = _pick(rows, 256, 8) if rows % 8 == 0 else rows

    def body(w_ref, g_ref, m_ref, v_ref, d_ref, nm_ref, nv_ref):
        g_ = g_ref[...]
        m_ = ADAM_B1 * m_ref[...] + (1.0 - ADAM_B1) * g_
        v_ = ADAM_B2 * v_ref[...] + (1.0 - ADAM_B2) * jnp.square(g_)
        m_hat = m_ / (1.0 - ADAM_B1 ** ADAM_STEP)
        v_hat = v_ / (1.0 - ADAM_B2 ** ADAM_STEP)
        d_ref[...] = -ADAM_LR * (m_hat / (jnp.sqrt(v_hat) + ADAM_EPS) + ADAM_WD * w_ref[...])
        nm_ref[...] = m_
        nv_ref[...] = v_

    spec = pl.BlockSpec((tr, cols), lambda i: (i, 0))
    return pl.pallas_call(
        body, name=name, grid=(rows // tr,), in_specs=[spec] * 4, out_specs=[spec] * 3,
        out_shape=[jax.ShapeDtypeStruct(w.shape, F32)] * 3, compiler_params=_params("parallel"),
    )(w, g, m, v)


COL_SHARDED = ("w_in", "w_uq", "w_ukv", "w_branch_a", "w_branch_b", "w_up", "w_ple")
ROW_SHARDED = ("w_out", "w_down", "w_ple_gate")
BIG = ("w_in", "w_uq", "w_ukv", "w_branch_a", "w_branch_b", "w_out", "w_up", "w_down", "w_ple_gate", "w_ple")
SMALL = ("attn_pre_norm", "attn_post_norm", "b_gate", "q_a_norm", "kv_a_norm", "mlp_pre_norm", "mlp_post_norm",
         "conv_b", "ple_norm", "sinks")
SMALL_COLS = 128


def _pack_rows(arrays, cols, row_mult):
    flat = jnp.concatenate([a.reshape(-1) for a in arrays])
    pad = (-flat.shape[0]) % (cols * row_mult)
    return jnp.pad(flat, (0, pad)).reshape(-1, cols)


def _unpack_small(vec, shapes):
    flat = vec.reshape(-1)
    out, off = {}, 0
    for name in SMALL:
        n = shapes[name]
        out[name] = flat[off:off + n].reshape(1, n)
        off += n + (-n) % SMALL_COLS
    return out


def _pad_lanes(t, width):
    return jnp.pad(t, [(0, 0)] * (t.ndim - 1) + [(0, width - t.shape[-1])])


def _pad_rows(t, rows):
    return jnp.pad(t, [(0, 0)] * (t.ndim - 2) + [(0, rows - t.shape[-2]), (0, 0)])


FRONT_SIZES = (512, 128, 128, 256, 128)
FRONT_BOUNDS = (0, 512, 640, 768, 1024, 1152, 1280)
PE_LANE = NOPE_DIM


def _arrange_w_in_t(wt):
    k = wt.shape[1]
    n_front = sum(FRONT_SIZES)
    front, kr, gates = wt[:n_front], wt[n_front:n_front + ROPE_DIM], wt[n_front + ROPE_DIM:]
    kr_slab = jnp.concatenate([jnp.zeros((PE_LANE, k), wt.dtype), kr,
                               jnp.zeros((HEAD_PAD - PE_LANE - ROPE_DIM, k), wt.dtype)], axis=0)
    return jnp.concatenate([front, kr_slab], axis=0), gates


def _arrange_w_uq_t(wt):
    k = wt.shape[1]
    return _pad_rows(wt.reshape(B_HEADS, NOPE_DIM + ROPE_DIM, k), HEAD_PAD).reshape(B_HEADS * HEAD_PAD, k)


def _arrange_w_ukv_t(wt):
    k = wt.shape[1]
    w = wt.reshape(B_HEADS, 2, NOPE_DIM, k)
    slabs = [_pad_rows(w[:, part], HEAD_PAD).reshape(B_HEADS * HEAD_PAD, k) for part in range(2)]
    return jnp.concatenate(slabs, axis=0)


def _rope_tables(positions, s):
    pos = positions.reshape(s, 1).astype(F32)

    def angles(dim):
        return pos * ROPE_THETA ** (-(jnp.arange(0, dim, 2, dtype=F32) / dim))

    cos_a, sin_a = jnp.cos(angles(A_HEAD_DIM)), jnp.sin(angles(A_HEAD_DIM))
    zero_a = jnp.zeros_like(sin_a)
    tables_a = [jnp.tile(jnp.concatenate(pair, axis=1), (1, LANES // A_HEAD_DIM))
                for pair in ((cos_a, cos_a), (-sin_a, zero_a), (zero_a, sin_a))]
    cos_b, sin_b = jnp.cos(angles(ROPE_DIM)), jnp.sin(angles(ROPE_DIM))
    zero_b = jnp.zeros_like(sin_b)

    def slab(first, second, fill):
        return jnp.concatenate([jnp.full((s, PE_LANE), fill, F32), first, second,
                                jnp.full((s, HEAD_PAD - PE_LANE - ROPE_DIM), fill, F32)], axis=1)

    tables_b = [slab(cos_b, cos_b, 1.0), slab(-sin_b, zero_b, 0.0), slab(zero_b, sin_b, 0.0)]
    return tables_a + tables_b


def _local_loss(wts, x, p, tables, target):
    s = x.shape[0]
    small_shapes = {n: wts[n].shape[-1] for n in SMALL}
    small_vec = _pack_rows([_pad_lanes(wts[n].reshape(1, -1), small_shapes[n] + (-small_shapes[n]) % SMALL_COLS)
                            for n in SMALL], SMALL_COLS, 8)
    sm = _unpack_small(replicated(small_vec), small_shapes)
    shards = [wts[n].T if n in COL_SHARDED else wts[n] for n in BIG] + [_pack_rows([wts["conv_w"]], SMALL_COLS, 8)]
    gathered = fsdp_gather(tuple(shards), (BF16,) * len(BIG) + (F32,))
    big = {n: g.reshape(-1, g.shape[2]) for n, g in zip(BIG, gathered)}
    ch = wts["conv_w"].shape[1]
    conv_w = gathered[-1].reshape(N_DEV, -1)[:, :CONV_W * ch].reshape(N_DEV, CONV_W, ch)
    conv_w = conv_w.transpose(1, 0, 2).reshape(CONV_W, N_DEV * ch)

    w_front_t, w_gates_t = _arrange_w_in_t(big["w_in"])
    tables_a, tables_b = tables[:3], tables[3:]

    (h1,) = stage("prenorm", _f_prenorm, [x], [sm["attn_pre_norm"]], out_dtypes=[BF16])
    zf = mm(h1, w_front_t, "nt", "w_front", True, F32)
    gates = mm(h1, w_gates_t, "nt", "w_gates", True, F32)
    qa, ka, va, cqn, ckvn, kpe = stage("prep", _f_prep, [zf], [sm["q_a_norm"], sm["kv_a_norm"]], tables,
                                       splits=[FRONT_BOUNDS], out_dtypes=[BF16, BF16, BF16, BF16, BF16, F32])
    ya = swa_nat(qa, ka, va, sm["sinks"].reshape(-1))

    qb = mm(cqn, _arrange_w_uq_t(big["w_uq"]), "nt", "w_uq", True, F32)
    kvb = mm(ckvn, _arrange_w_ukv_t(big["w_ukv"]), "nt", "w_ukv", True, F32)
    (q2,) = stage("qrope", _f_qrope, [qb], [], tables_b, out_dtypes=[BF16])
    k2, v2 = stage("kv", _f_kv, [kvb, kpe], splits=[(0, B_HEADS * HEAD_PAD, 2 * B_HEADS * HEAD_PAD), None],
                   out_dtypes=[BF16, BF16])
    yb = flash_nat(q2, k2, v2)

    pa = mm(ya, big["w_branch_a"], "nt", "w_branch_a", True, F32)
    pb = mm(yb, big["w_branch_b"], "nt", "w_branch_b", True, F32)
    (mixed,) = stage("gate", _f_gate, [gates, pa, pb], [sm["b_gate"][:, :D_MODEL], sm["b_gate"][:, D_MODEL:]],
                     splits=[(0, D_MODEL, 2 * D_MODEL), None, None], out_dtypes=[BF16])
    o = mm(mixed, big["w_out"], "nn", "w_out", True, F32)
    x1, h2 = stage("post_attn", _f_post, [x, o], [sm["attn_post_norm"], sm["mlp_pre_norm"]], out_dtypes=[F32, BF16])

    up_g = mm(h2, big["w_up"][:D_FF], "nt", "w_up_gate", True, F32)
    up_v = mm(h2, big["w_up"][D_FF:], "nt", "w_up_val", True, F32)
    u_g = dwconv(up_g, conv_w[:, :D_FF], sm["conv_b"][:, :D_FF], "conv_gate")
    u_v = dwconv(up_v, conv_w[:, D_FF:], sm["conv_b"][:, D_FF:], "conv_val")
    (act,) = stage("glu", _f_glu, [u_g, u_v], ts=128, out_dtypes=[BF16])
    ff = mm(act, big["w_down"], "nn", "w_down", True, F32)
    x2, h3 = stage("post_mlp", _f_post, [x1, ff], [sm["mlp_post_norm"], sm["ple_norm"]], out_dtypes=[F32, BF16])

    t = mm(h3, big["w_ple_gate"], "nn", "w_ple_gate", True, F32)
    e = mm(p, big["w_ple"], "nt", "w_ple", False, F32)
    (rowloss,) = stage("loss", _f_out, [x2, t, e], [], [target])
    return jnp.sum(rowloss)


WEIGHTS = ["attn_pre_norm", "attn_post_norm", "w_in", "b_gate", "sinks", "q_a_norm", "w_uq", "kv_a_norm", "w_ukv",
           "w_branch_a", "w_branch_b", "w_out", "mlp_pre_norm", "mlp_post_norm", "w_up", "conv_w", "conv_b",
           "w_down", "ple_norm", "w_ple_gate", "w_ple"]


def kernel(x, p, positions, attn_pre_norm, attn_post_norm, w_in, b_gate, sinks, q_a_norm, w_uq, kv_a_norm, w_ukv, w_branch_a, w_branch_b, w_out, mlp_pre_norm, mlp_post_norm, w_up, conv_w, conv_b, w_down, ple_norm, w_ple_gate, w_ple, loss_target, m_attn_pre_norm, m_attn_post_norm, m_w_in, m_b_gate, m_sinks, m_q_a_norm, m_w_uq, m_kv_a_norm, m_w_ukv, m_w_branch_a, m_w_branch_b, m_w_out, m_mlp_pre_norm, m_mlp_post_norm, m_w_up, m_conv_w, m_conv_b, m_w_down, m_ple_norm, m_w_ple_gate, m_w_ple, v_attn_pre_norm, v_attn_post_norm, v_w_in, v_b_gate, v_sinks, v_q_a_norm, v_w_uq, v_kv_a_norm, v_w_ukv, v_w_branch_a, v_w_branch_b, v_w_out, v_mlp_pre_norm, v_mlp_post_norm, v_w_up, v_conv_w, v_conv_b, v_w_down, v_ple_norm, v_w_ple_gate, v_w_ple):
    given = dict(locals())
    s = x.shape[1]
    wts = {n: given[n][0] if given[n].ndim == 3 else given[n] for n in WEIGHTS}
    tables = _rope_tables(positions, s)
    local_loss, (grads, grad_x) = jax.value_and_grad(_local_loss, argnums=(0, 1))(
        wts, x[0], p[0, 0], tables, loss_target[0])
    loss = lax.psum(local_loss, AXES)

    outs = {"grad": [], "delta": [], "m": [], "v": []}
    for n in WEIGHTS:
        shape = given[n].shape
        w2 = wts[n].reshape(-1, shape[-1])
        g2 = grads[n].reshape(w2.shape)
        delta, new_m, new_v = _adamw(w2, g2, given["m_" + n].reshape(w2.shape), given["v_" + n].reshape(w2.shape),
                                     "adamw_" + n)
        outs["grad"].append(g2.reshape(shape))
        outs["delta"].append(delta.reshape(shape))
        outs["m"].append(new_m.reshape(shape))
        outs["v"].append(new_v.reshape(shape))
    return (loss, grad_x[None], *outs["grad"], *outs["delta"], *outs["m"], *outs["v"])
```

```python
import functools

import numpy as np
import jax
import jax.numpy as jnp
from jax import lax
from jax.experimental import pallas as pl
from jax.experimental.pallas import tpu as pltpu

F32 = jnp.float32
BF16 = jnp.bfloat16
MESH_ID = pl.DeviceIdType.MESH
AXES = ("x", "y", "c")
N_DEV = 8

D_MODEL = 1024
RMS_EPS = 1e-6
ROPE_THETA = 10000.0
SWA_BLOCK = 128
A_HEADS, A_KV_HEADS, A_HEAD_DIM = 8, 2, 64
A_GROUP = A_HEADS // A_KV_HEADS
B_HEADS, Q_LORA, KV_LORA, NOPE_DIM, ROPE_DIM, V_DIM = 8, 256, 128, 64, 32, 64
D_FF = 2816
CONV_W = 3
HEAD_PAD = 128

ADAM_LR, ADAM_B1, ADAM_B2, ADAM_EPS, ADAM_WD, ADAM_STEP = 0.001, 0.9, 0.999, 1e-08, 0.01, 10

VMEM_LIMIT = 48 * 1024 * 1024
MM_TM, MM_TN, MM_TK_TOKENS = 512, 1408, 1024
MM_VMEM_BUDGET = 36 * 1024 * 1024
FLASH_T = 512
CONV_TS, CONV_TC = 128, 2816


def _params(*sem):
    return pltpu.CompilerParams(dimension_semantics=sem, vmem_limit_bytes=VMEM_LIMIT)


def _pick(dim, cap, mult):
    best = None
    for t in range(mult, min(dim, cap) + 1, mult):
        if dim % t == 0:
            best = t
    return dim if best is None else best


def _divisors(dim, mult):
    return [t for t in range(mult, dim + 1, mult) if dim % t == 0] or [dim]


def _matmul_tiles(m, n, kdim, form, sizes):
    sa, sb, so = sizes
    tk = _pick(kdim, MM_TK_TOKENS, 128) if form == "tn" else kdim
    cap_m = MM_TN if form == "tn" else MM_TM
    best = None
    for tm in _divisors(m, 128):
        for tn in _divisors(n, 128):
            need = 2 * (tm * tk * sa + tk * tn * sb + tm * tn * so) + (tm * tn * 4 if tk != kdim else 0)
            if tm > cap_m or tn > MM_TN or need > MM_VMEM_BUDGET:
                continue
            if best is None or (tm * tn, tm) > (best[0] * best[1], best[0]):
                best = (tm, tn)
    return best[0], best[1], tk


def _matmul(a, b, form, *, out_dtype=F32, name):
    if form == "tn":
        (kdim, m), n = a.shape, b.shape[1]
    else:
        (m, kdim), n = a.shape, (b.shape[1] if form == "nn" else b.shape[0])
    sizes = (a.dtype.itemsize, b.dtype.itemsize, jnp.dtype(out_dtype).itemsize)
    tm, tn, tk = _matmul_tiles(m, n, kdim, form, sizes)
    nk = kdim // tk
    a_spec = (pl.BlockSpec((tk, tm), lambda i, j, k: (k, i)) if form == "tn"
              else pl.BlockSpec((tm, tk), lambda i, j, k: (i, k)))
    b_spec = (pl.BlockSpec((tn, tk), lambda i, j, k: (j, k)) if form == "nt"
              else pl.BlockSpec((tk, tn), lambda i, j, k: (k, j)))
    dims = (((0 if form == "tn" else 1,), (1 if form == "nt" else 0,)), ((), ()))

    def product(a_ref, b_ref):
        return lax.dot_general(a_ref[...].astype(BF16), b_ref[...].astype(BF16), dims, preferred_element_type=F32)

    if nk == 1:
        def body(a_ref, b_ref, o_ref):
            o_ref[...] = product(a_ref, b_ref).astype(o_ref.dtype)

        scratch = []
    else:
        def body(a_ref, b_ref, o_ref, acc_ref):
            k = pl.program_id(2)

            @pl.when(k == 0)
            def _():
                acc_ref[...] = jnp.zeros_like(acc_ref)

            acc_ref[...] += product(a_ref, b_ref)

            @pl.when(k == nk - 1)
            def _():
                o_ref[...] = acc_ref[...].astype(o_ref.dtype)

        scratch = [pltpu.VMEM((tm, tn), F32)]

    return pl.pallas_call(
        body, name=name, grid=(m // tm, n // tn, nk),
        in_specs=[a_spec, b_spec],
        out_specs=pl.BlockSpec((tm, tn), lambda i, j, k: (i, j)),
        out_shape=jax.ShapeDtypeStruct((m, n), out_dtype),
        scratch_shapes=scratch,
        compiler_params=_params("parallel", "parallel", "arbitrary"),
    )(a, b)


@functools.partial(jax.custom_vjp, nondiff_argnums=(2, 3, 4, 5))
def mm(a, w, form, name, need_da, out_dtype):
    return _matmul(a, w, form, out_dtype=out_dtype, name=name + "_fwd")


def _mm_fwd(a, w, form, name, need_da, out_dtype):
    return _matmul(a, w, form, out_dtype=out_dtype, name=name + "_fwd"), (a, w)


def _mm_bwd(form, name, need_da, out_dtype, res, ct):
    a, w = res
    if form == "nn":
        da = _matmul(ct, w, "nt", out_dtype=a.dtype, name=name + "_da") if need_da else jnp.zeros_like(a)
        dw = _matmul(a, ct, "tn", out_dtype=w.dtype, name=name + "_dw")
    else:
        da = _matmul(ct, w, "nn", out_dtype=a.dtype, name=name + "_da") if need_da else jnp.zeros_like(a)
        dw = _matmul(ct, a, "tn", out_dtype=w.dtype, name=name + "_dw")
    return da, dw


mm.defvjp(_mm_fwd, _mm_bwd)


def _pairs(bounds):
    return list(zip(bounds[:-1], bounds[1:]))


def _split(v, bounds):
    return [v[:, a:b] for a, b in _pairs(bounds)]


def stage(name, f, tiled, params=(), consts=(), splits=None, ts=256, out_dtypes=None):
    tiled, params, consts = tuple(tiled), tuple(params), tuple(consts)
    n_t, n_p, n_c = len(tiled), len(params), len(consts)
    s = tiled[0].shape[0]
    ts = min(ts, s)
    grid = (s // ts,)
    if splits is None:
        splits = [None] * n_t
    in_bounds = [(0, t.shape[1]) if b is None else tuple(b) for t, b in zip(tiled, splits)]

    def tile_aval(arr):
        return jax.ShapeDtypeStruct((ts, arr.shape[1]), arr.dtype)

    slab_avals = [[jax.ShapeDtypeStruct((ts, e - a), t.dtype) for a, e in _pairs(b)]
                  for t, b in zip(tiled, in_bounds)]
    out_avals = jax.eval_shape(f, slab_avals, list(params), [tile_aval(c) for c in consts])
    out_bounds = [tuple(np.cumsum([0] + [o.shape[1] for o in slabs]).tolist()) for slabs in out_avals]
    out_dtypes = [F32] * len(out_bounds) if out_dtypes is None else out_dtypes
    out_shapes = [jax.ShapeDtypeStruct((s, b[-1]), d) for b, d in zip(out_bounds, out_dtypes)]

    def row_spec(width):
        return pl.BlockSpec((ts, width), lambda i: (i, 0))

    def par_spec(arr):
        return pl.BlockSpec(arr.shape, lambda i: (0, 0))

    in_specs = ([row_spec(t.shape[1]) for t in tiled] + [par_spec(p) for p in params]
                + [row_spec(c.shape[1]) for c in consts])

    def load(refs):
        t = [_split(r[...], b) for r, b in zip(refs[:n_t], in_bounds)]
        p = [r[...] for r in refs[n_t:n_t + n_p]]
        c = [r[...] for r in refs[n_t + n_p:n_t + n_p + n_c]]
        return t, p, c

    def store(refs, values, bounds):
        for ref, slabs, b in zip(refs, values, bounds):
            for v, (a, e) in zip(slabs, _pairs(b)):
                ref[:, a:e] = v.astype(ref.dtype)

    def run_fwd(tiled, params, consts):
        def body(*refs):
            t, p, c = load(refs)
            store(refs[n_t + n_p + n_c:], f(t, p, c), out_bounds)

        return pl.pallas_call(
            body, name=name + "_fwd", grid=grid, in_specs=in_specs,
            out_specs=[row_spec(b[-1]) for b in out_bounds], out_shape=out_shapes,
            compiler_params=_params("parallel"),
        )(*tiled, *params, *consts)

    def run_bwd(tiled, params, consts, cts):
        n_in = n_t + n_p + n_c
        n_o = len(out_bounds)

        def body(*refs):
            t, p, c = load(refs)
            g = [_split(r[...].astype(F32), b) for r, b in zip(refs[n_in:n_in + n_o], out_bounds)]
            _, pull = jax.vjp(lambda t_, p_: f(t_, p_, c), t, p)
            dt, dp = pull(g)
            store(refs[n_in + n_o:n_in + n_o + n_t], dt, in_bounds)
            first = pl.program_id(0) == 0
            for ref, d in zip(refs[n_in + n_o + n_t:], dp):
                @pl.when(first)
                def _(ref=ref):
                    ref[...] = jnp.zeros_like(ref)

                ref[...] += d

        res = pl.pallas_call(
            body, name=name + "_bwd", grid=grid,
            in_specs=in_specs + [row_spec(b[-1]) for b in out_bounds],
            out_specs=[row_spec(t.shape[1]) for t in tiled] + [par_spec(p) for p in params],
            out_shape=[jax.ShapeDtypeStruct(t.shape, t.dtype) for t in tiled]
                      + [jax.ShapeDtypeStruct(p.shape, F32) for p in params],
            compiler_params=_params("arbitrary"),
        )(*tiled, *params, *consts, *cts)
        return tuple(res[:n_t]), tuple(res[n_t:])

    @jax.custom_vjp
    def op(tiled, params, consts):
        return tuple(run_fwd(tiled, params, consts))

    def op_fwd(tiled, params, consts):
        return op(tiled, params, consts), (tiled, params, consts)

    def op_bwd(res, cts):
        tiled, params, consts = res
        dt, dp = run_bwd(tiled, params, consts, cts)
        return dt, dp, tuple(jnp.zeros_like(c) for c in consts)

    op.defvjp(op_fwd, op_bwd)
    return op(tiled, params, consts)


def _rms(t, g):
    return t * lax.rsqrt(jnp.mean(t * t, axis=-1, keepdims=True) + RMS_EPS) * g


@functools.partial(jax.custom_vjp, nondiff_argnums=(1,))
def _lane_roll(t, shift):
    return pltpu.roll(t, shift % t.shape[-1], t.ndim - 1)


def _lane_roll_fwd(t, shift):
    return _lane_roll(t, shift), None


def _lane_roll_bwd(shift, _, ct):
    return (pltpu.roll(ct, (-shift) % ct.shape[-1], ct.ndim - 1),)


_lane_roll.defvjp(_lane_roll_fwd, _lane_roll_bwd)


def _rope_lanes(t, tables, half):
    reps = t.shape[1] // tables[0].shape[1]
    c, s_lo, s_hi = [jnp.concatenate([tb] * reps, axis=1) if reps > 1 else tb for tb in tables]
    return t * c + _lane_roll(t, -half) * s_lo + _lane_roll(t, half) * s_hi


def _f_prenorm(t, p, c):
    return [[_rms(t[0][0], p[0])]]


def _f_prep(t, p, c):
    qa, ka, va, cq, ckv, kr = t[0]
    return [[_rope_lanes(qa, c[0:3], A_HEAD_DIM // 2)], [_rope_lanes(ka, c[0:3], A_HEAD_DIM // 2)], [va],
            [_rms(cq, p[0])], [_rms(ckv, p[1])], [_rope_lanes(kr, c[3:6], ROPE_DIM // 2)]]


def _f_qrope(t, p, c):
    return [[_rope_lanes(t[0][0], c, ROPE_DIM // 2)]]


def _f_kv(t, p, c):
    (k_nope, v), (k_pe,) = t
    return [[k_nope + jnp.concatenate([k_pe] * B_HEADS, axis=1)], [v]]


def _f_gate(t, p, c):
    (ga, gb), (pa,), (pb,) = t
    ba, bb = p
    return [[jax.nn.sigmoid(ga + ba) * pa + jax.nn.sigmoid(gb + bb) * pb]]


def _f_post(t, p, c):
    x1 = t[0][0] + _rms(t[1][0], p[0])
    return [[x1], [_rms(x1, p[1])]]


def _f_glu(t, p, c):
    return [[jax.nn.gelu(t[0][0], approximate=True) * t[1][0]]]


def _f_out(t, p, c):
    y = t[0][0] + jax.nn.sigmoid(t[1][0]) * t[2][0]
    err = y - c[0]
    return [[0.5 * jnp.mean(err * err, axis=-1, keepdims=True)]]


def _shift_down(cur, prev, has_prev):
    rows = cur.shape[0]
    row = lax.broadcasted_iota(jnp.int32, cur.shape, 0)
    m1 = prev[7:8, :] * has_prev
    m2 = prev[6:7, :] * has_prev
    u1 = jnp.where(row >= 1, pltpu.roll(cur, 1, 0), m1)
    u2 = jnp.where(row >= 2, pltpu.roll(cur, 2, 0), jnp.where(row == 1, m1, m2))
    return u1, u2


def _shift_up(cur, nxt, has_next):
    rows = cur.shape[0]
    row = lax.broadcasted_iota(jnp.int32, cur.shape, 0)
    n0 = nxt[0:1, :] * has_next
    n1 = nxt[1:2, :] * has_next
    d1 = jnp.where(row < rows - 1, pltpu.roll(cur, rows - 1, 0), n0)
    d2 = jnp.where(row < rows - 2, pltpu.roll(cur, rows - 2, 0), jnp.where(row == rows - 2, n0, n1))
    return d1, d2


def _conv_tiles(s, ch):
    ts = min(CONV_TS, s)
    tc = _pick(ch, CONV_TC, 128)
    return ts, tc, s // ts, ch // tc


def _conv_fwd_call(up, w, b, name):
    s, ch = up.shape
    ts, tc, nt, nc = _conv_tiles(s, ch)
    hb = ts // 8

    def body(cur_ref, prev_ref, w_ref, b_ref, o_ref):
        cur = cur_ref[...]
        u1, u2 = _shift_down(cur, prev_ref[...], (pl.program_id(1) > 0).astype(F32))
        o_ref[...] = w_ref[2:3, :] * cur + w_ref[1:2, :] * u1 + w_ref[0:1, :] * u2 + b_ref[...]

    return pl.pallas_call(
        body, name=name + "_fwd", grid=(nc, nt),
        in_specs=[pl.BlockSpec((ts, tc), lambda c, i: (i, c)),
                  pl.BlockSpec((8, tc), lambda c, i: (jnp.maximum(i * hb - 1, 0), c)),
                  pl.BlockSpec((CONV_W, tc), lambda c, i: (0, c)),
                  pl.BlockSpec((1, tc), lambda c, i: (0, c))],
        out_specs=pl.BlockSpec((ts, tc), lambda c, i: (i, c)),
        out_shape=jax.ShapeDtypeStruct((s, ch), F32),
        compiler_params=_params("parallel", "parallel"),
    )(up, up, w, b)


def _conv_bwd_call(up, w, du, name):
    s, ch = up.shape
    ts, tc, nt, nc = _conv_tiles(s, ch)
    hb = ts // 8

    def body(cur_ref, prev_ref, w_ref, du_ref, nxt_ref, dup_ref, dw_ref, db_ref):
        i = pl.program_id(1)
        cur, du = cur_ref[...], du_ref[...]
        u1, u2 = _shift_down(cur, prev_ref[...], (i > 0).astype(F32))
        d1, d2 = _shift_up(du, nxt_ref[...], (i < nt - 1).astype(F32))
        dup_ref[...] = w_ref[2:3, :] * du + w_ref[1:2, :] * d1 + w_ref[0:1, :] * d2

        @pl.when(i == 0)
        def _():
            dw_ref[...] = jnp.zeros_like(dw_ref)
            db_ref[...] = jnp.zeros_like(db_ref)

        dw_ref[0:1, :] += jnp.sum(du * u2, axis=0, keepdims=True)
        dw_ref[1:2, :] += jnp.sum(du * u1, axis=0, keepdims=True)
        dw_ref[2:3, :] += jnp.sum(du * cur, axis=0, keepdims=True)
        db_ref[...] += jnp.sum(du, axis=0, keepdims=True)

    return pl.pallas_call(
        body, name=name + "_bwd", grid=(nc, nt),
        in_specs=[pl.BlockSpec((ts, tc), lambda c, i: (i, c)),
                  pl.BlockSpec((8, tc), lambda c, i: (jnp.maximum(i * hb - 1, 0), c)),
                  pl.BlockSpec((CONV_W, tc), lambda c, i: (0, c)),
                  pl.BlockSpec((ts, tc), lambda c, i: (i, c)),
                  pl.BlockSpec((8, tc), lambda c, i: (jnp.minimum((i + 1) * hb, s // 8 - 1), c))],
        out_specs=[pl.BlockSpec((ts, tc), lambda c, i: (i, c)),
                   pl.BlockSpec((CONV_W, tc), lambda c, i: (0, c)),
                   pl.BlockSpec((1, tc), lambda c, i: (0, c))],
        out_shape=[jax.ShapeDtypeStruct((s, ch), F32), jax.ShapeDtypeStruct((CONV_W, ch), F32),
                   jax.ShapeDtypeStruct((1, ch), F32)],
        compiler_params=_params("parallel", "arbitrary"),
    )(up, up, w, du, du)


@functools.partial(jax.custom_vjp, nondiff_argnums=(3,))
def dwconv(up, w, b, name):
    return _conv_fwd_call(up, w, b, name)


def _dwconv_fwd(up, w, b, name):
    return _conv_fwd_call(up, w, b, name), (up, w)


def _dwconv_bwd(name, res, ct):
    up, w = res
    return tuple(_conv_bwd_call(up, w, ct, name))


dwconv.defvjp(_dwconv_fwd, _dwconv_bwd)


SWA_ROWS = A_GROUP * SWA_BLOCK


def _swa_sink_rows(sink_ref, g):
    return jnp.concatenate([jnp.full((SWA_BLOCK, 1), sink_ref[g * A_GROUP + h], F32) for h in range(A_GROUP)], axis=0)


def _swa_probs(q, kp, kc, sink, prev_off):
    scale = A_HEAD_DIM ** -0.5
    nt = (((1,), (1,)), ((), ()))
    sp = lax.dot_general(q, kp, nt, preferred_element_type=F32) * scale
    sc = lax.dot_general(q, kc, nt, preferred_element_type=F32) * scale
    qi = lax.broadcasted_iota(jnp.int32, sp.shape, 0) & (SWA_BLOCK - 1)
    kj = lax.broadcasted_iota(jnp.int32, sp.shape, 1)
    sp = jnp.where(kj > qi + prev_off, sp, -jnp.inf)
    sc = jnp.where(kj <= qi, sc, -jnp.inf)
    m = jnp.maximum(jnp.maximum(jnp.max(sp, axis=-1, keepdims=True), jnp.max(sc, axis=-1, keepdims=True)), sink)
    ep, ec, es = jnp.exp(sp - m), jnp.exp(sc - m), jnp.exp(sink - m)
    den = jnp.sum(ep, axis=-1, keepdims=True) + jnp.sum(ec, axis=-1, keepdims=True) + es
    return ep / den, ec / den, es / den


MLA_SCALE = (NOPE_DIM + ROPE_DIM) ** -0.5
EXP2_SCALE = MLA_SCALE * float(np.log2(np.e))
NT_DIMS = (((1,), (1,)), ((), ()))
TN_DIMS = (((0,), (0,)), ((), ()))


LANES = 128
HALF = LANES // 2


def _low_half(shape):
    return lax.broadcasted_iota(jnp.int32, shape, len(shape) - 1) < HALF


def _dup_half(x, g):
    xf = x.astype(F32)
    keep = _low_half(xf.shape) if g == 0 else jnp.logical_not(_low_half(xf.shape))
    xm = jnp.where(keep, xf, 0.0)
    return (xm + pltpu.roll(xm, HALF, 1)).astype(x.dtype)


def _fold_half(r, g):
    total = r + pltpu.roll(r, HALF, 1)
    keep = _low_half(r.shape) if g == 0 else jnp.logical_not(_low_half(r.shape))
    return jnp.where(keep, total, 0.0)


def _swa_stack_heads(ref, g):
    parts = []
    for tile in range(2):
        slab = ref[:, (2 * g + tile) * LANES:(2 * g + tile + 1) * LANES]
        low = _low_half(slab.shape)
        parts += [jnp.where(low, slab, jnp.zeros_like(slab)), jnp.where(low, jnp.zeros_like(slab), slab)]
    return jnp.concatenate(parts, axis=0)


def _swa_unstack_heads(ref, g, rows):
    for tile in range(2):
        a = rows[(2 * tile) * SWA_BLOCK:(2 * tile + 1) * SWA_BLOCK]
        b = rows[(2 * tile + 1) * SWA_BLOCK:(2 * tile + 2) * SWA_BLOCK]
        ref[:, (2 * g + tile) * LANES:(2 * g + tile + 1) * LANES] = jnp.where(_low_half(a.shape), a, b).astype(ref.dtype)


def _swa_nat_specs():
    blk = SWA_BLOCK
    q_spec = pl.BlockSpec((blk, A_HEADS * A_HEAD_DIM), lambda n: (n, 0))
    prev_spec = pl.BlockSpec((blk, LANES), lambda n: (jnp.maximum(n - 1, 0), 0))
    cur_spec = pl.BlockSpec((blk, LANES), lambda n: (n, 0))
    return q_spec, prev_spec, cur_spec, pl.BlockSpec(memory_space=pltpu.SMEM)


def _swa_nat_fwd_call(q, k, v, sinks):
    s = q.shape[0]
    q_spec, prev_spec, cur_spec, sink_spec = _swa_nat_specs()

    def body(q_ref, kp_ref, kc_ref, vp_ref, vc_ref, sink_ref, o_ref):
        prev_off = jnp.where(pl.program_id(0) > 0, 0, SWA_BLOCK)
        for g in range(A_KV_HEADS):
            kp, kc = _dup_half(kp_ref[...], g), _dup_half(kc_ref[...], g)
            vp, vc = _dup_half(vp_ref[...], g), _dup_half(vc_ref[...], g)
            pp, pc, _ = _swa_probs(_swa_stack_heads(q_ref, g), kp, kc, _swa_sink_rows(sink_ref, g), prev_off)
            out = (jnp.dot(pp.astype(BF16), vp, preferred_element_type=F32)
                   + jnp.dot(pc.astype(BF16), vc, preferred_element_type=F32))
            _swa_unstack_heads(o_ref, g, out)

    return pl.pallas_call(
        body, name="swa_fwd", grid=(s // SWA_BLOCK,),
        in_specs=[q_spec, prev_spec, cur_spec, prev_spec, cur_spec, sink_spec],
        out_specs=q_spec, out_shape=jax.ShapeDtypeStruct(q.shape, BF16),
        compiler_params=_params("parallel"),
    )(q, k, k, v, v, sinks)


def _swa_nat_bwd_call(q, k, v, sinks, do):
    s = q.shape[0]
    q_spec, prev_spec, cur_spec, sink_spec = _swa_nat_specs()
    scale = A_HEAD_DIM ** -0.5
    dsink_spec = pl.BlockSpec((A_KV_HEADS, SWA_ROWS, 1), lambda n: (0, 0, 0))

    def body(q_ref, kp_ref, kc_ref, vp_ref, vc_ref, sink_ref, do_ref,
             dq_ref, dkp_ref, dkc_ref, dvp_ref, dvc_ref, dsink_ref):
        n = pl.program_id(0)
        prev_off = jnp.where(n > 0, 0, SWA_BLOCK)

        @pl.when(n == 0)
        def _():
            dsink_ref[...] = jnp.zeros_like(dsink_ref)

        totals = [jnp.zeros((SWA_BLOCK, LANES), F32) for _ in range(4)]
        for g in range(A_KV_HEADS):
            kp, kc = _dup_half(kp_ref[...], g), _dup_half(kc_ref[...], g)
            vp, vc = _dup_half(vp_ref[...], g), _dup_half(vc_ref[...], g)
            qb = _swa_stack_heads(q_ref, g)
            dob = _swa_stack_heads(do_ref, g)
            pp, pc, ps = _swa_probs(qb, kp, kc, _swa_sink_rows(sink_ref, g), prev_off)
            ppb, pcb = pp.astype(BF16), pc.astype(BF16)
            out = jnp.dot(ppb, vp, preferred_element_type=F32) + jnp.dot(pcb, vc, preferred_element_type=F32)
            delta = jnp.sum(dob.astype(F32) * out, axis=-1, keepdims=True)
            dsp = (pp * (lax.dot_general(dob, vp, NT_DIMS, preferred_element_type=F32) - delta)).astype(BF16)
            dsc = (pc * (lax.dot_general(dob, vc, NT_DIMS, preferred_element_type=F32) - delta)).astype(BF16)
            dsink_ref[g] += -ps * delta
            dq = (jnp.dot(dsp, kp, preferred_element_type=F32) + jnp.dot(dsc, kc, preferred_element_type=F32)) * scale
            _swa_unstack_heads(dq_ref, g, dq)
            parts = [lax.dot_general(dsp, qb, TN_DIMS, preferred_element_type=F32) * scale,
                     lax.dot_general(dsc, qb, TN_DIMS, preferred_element_type=F32) * scale,
                     lax.dot_general(ppb, dob, TN_DIMS, preferred_element_type=F32),
                     lax.dot_general(pcb, dob, TN_DIMS, preferred_element_type=F32)]
            totals = [tot + _fold_half(r, g) for tot, r in zip(totals, parts)]
        dkp_ref[...], dkc_ref[...], dvp_ref[...], dvc_ref[...] = totals

    kv_shape = jax.ShapeDtypeStruct(k.shape, F32)
    return pl.pallas_call(
        body, name="swa_bwd", grid=(s // SWA_BLOCK,),
        in_specs=[q_spec, prev_spec, cur_spec, prev_spec, cur_spec, sink_spec, q_spec],
        out_specs=[q_spec, cur_spec, cur_spec, cur_spec, cur_spec, dsink_spec],
        out_shape=[jax.ShapeDtypeStruct(q.shape, q.dtype), kv_shape, kv_shape, kv_shape, kv_shape,
                   jax.ShapeDtypeStruct((A_KV_HEADS, SWA_ROWS, 1), F32)],
        compiler_params=_params("arbitrary"),
    )(q, k, k, v, v, sinks, do)


@jax.custom_vjp
def swa_nat(q, k, v, sinks):
    return _swa_nat_fwd_call(q, k, v, sinks)


def _swa_nat_fwd(q, k, v, sinks):
    return _swa_nat_fwd_call(q, k, v, sinks), (q, k, v, sinks)


def _swa_nat_bwd(res, do):
    q, k, v, sinks = res
    dq, dkp, dkc, dvp, dvc, dsink = _swa_nat_bwd_call(q, k, v, sinks, do)

    def fold(prev_part, cur_part):
        shifted = jnp.concatenate([prev_part[SWA_BLOCK:], jnp.zeros_like(prev_part[:SWA_BLOCK])], axis=0)
        return (cur_part + shifted).astype(k.dtype)

    dsinks = jnp.sum(dsink.reshape(A_HEADS, SWA_BLOCK), axis=1)
    return dq, fold(dkp, dkc), fold(dvp, dvc), dsinks


swa_nat.defvjp(_swa_nat_fwd, _swa_nat_bwd)

N_PAIR = B_HEADS // 2


def _flash_nat_fwd_call(q, k, v, shards):
    s = q.shape[0]
    t = min(FLASH_T, s)
    nb = s // t
    d = LANES
    n_arr = len(shards)

    def body(*refs):
        q_ref, k_ref, v_ref = refs[:3]
        shard_refs = refs[3:3 + n_arr]
        o_ref, lse_ref = refs[3 + n_arr:5 + n_arr]
        gathered_refs = refs[5 + n_arr:5 + 2 * n_arr]
        vt_ref, m_ref, l_ref, acc_ref = refs[5 + 2 * n_arr:9 + 2 * n_arr]
        pair, i = pl.program_id(0), pl.program_id(1)
        ag_start, ag_forward, ag_finish = _allgather_phases(shard_refs, gathered_refs, *refs[9 + 2 * n_arr:])

        @pl.when((pair == 0) & (i == 0))
        def _():
            ag_start()

        @pl.when((pair == N_PAIR // 2) & (i == 0))
        def _():
            ag_forward()

        @pl.when(i == 0)
        def _():
            for hh in range(2):
                for chunk in range(nb):
                    rows = slice(chunk * t, (chunk + 1) * t)
                    vt_ref[hh, :, rows] = v_ref[rows, hh * d:(hh + 1) * d].T

        outs = []
        for hh in range(2):
            qb = q_ref[:, hh * d:(hh + 1) * d]
            m_ref[...] = jnp.full_like(m_ref, -jnp.inf)
            l_ref[...] = jnp.zeros_like(l_ref)
            acc_ref[...] = jnp.zeros_like(acc_ref)

            def step(j, on_diagonal, hh=hh, qb=qb):
                keys = pl.ds(pl.multiple_of(j * t, t), t)
                sc_t = lax.dot_general(k_ref[keys, hh * d:(hh + 1) * d], qb, NT_DIMS, preferred_element_type=F32)
                if on_diagonal:
                    key = lax.broadcasted_iota(jnp.int32, (t, t), 0)
                    qry = lax.broadcasted_iota(jnp.int32, (t, t), 1)
                    sc_t = jnp.where(qry >= key, sc_t, -jnp.inf)
                m_old = m_ref[...]
                m_new = jnp.maximum(m_old, jnp.max(sc_t, axis=0, keepdims=True))
                alpha = jnp.exp2((m_old - m_new) * EXP2_SCALE)
                p_t = jnp.exp2((sc_t - m_new) * EXP2_SCALE)
                l_ref[...] = alpha * l_ref[...] + jnp.sum(p_t, axis=0, keepdims=True)
                acc_ref[...] = alpha * acc_ref[...] + jnp.dot(vt_ref[hh, :, keys], p_t.astype(BF16),
                                                              preferred_element_type=F32)
                m_ref[...] = m_new

            def below(j, carry, step=step):
                step(j, False)
                return carry

            lax.fori_loop(0, i, below, 0)
            step(i, True)
            outs.append((acc_ref[...] / l_ref[...]).T)
            lse_ref[hh] = m_ref[...] * EXP2_SCALE + jnp.log2(l_ref[...])
        o_ref[...] = (outs[0] + pltpu.roll(outs[1], HALF, 1)).astype(o_ref.dtype)

        @pl.when((pair == N_PAIR - 1) & (i == nb - 1))
        def _():
            ag_finish()

    return pl.pallas_call(
        body, name="mla_fwd", grid=(N_PAIR, nb),
        in_specs=[pl.BlockSpec((t, 2 * d), lambda p, i: (i, p)),
                  pl.BlockSpec((s, 2 * d), lambda p, i: (0, p)),
                  pl.BlockSpec((s, 2 * d), lambda p, i: (0, p))] + [HBM_SPEC] * n_arr,
        out_specs=[pl.BlockSpec((t, d), lambda p, i: (i, p)),
                   pl.BlockSpec((2, 1, t), lambda p, i: (p, 0, i))] + [HBM_SPEC] * n_arr,
        out_shape=[jax.ShapeDtypeStruct((s, N_PAIR * d), BF16), jax.ShapeDtypeStruct((B_HEADS, 1, s), F32)]
                  + _allgather_out_shapes(shards),
        scratch_shapes=[pltpu.VMEM((2, d, s), BF16), pltpu.VMEM((1, t), F32), pltpu.VMEM((1, t), F32),
                        pltpu.VMEM((d, t), F32)] + _allgather_sems(n_arr),
        compiler_params=_params("arbitrary", "arbitrary"),
    )(q, k, v, *shards)


def _flash_nat_delta_call(o, do):
    s, w = o.shape
    t = min(FLASH_T, s)

    def body(o_ref, do_ref, out_ref):
        prod = o_ref[...].astype(F32) * do_ref[...].astype(F32)
        lane = lax.broadcasted_iota(jnp.int32, (w, LANES), 0) // V_DIM
        head = lax.broadcasted_iota(jnp.int32, (w, LANES), 1)
        out_ref[...] = jnp.dot(prod, (lane == head).astype(F32), precision=lax.Precision.HIGHEST,
                               preferred_element_type=F32)

    spec = pl.BlockSpec((t, w), lambda i: (i, 0))
    return pl.pallas_call(
        body, name="mla_delta", grid=(s // t,), in_specs=[spec, spec],
        out_specs=pl.BlockSpec((t, LANES), lambda i: (i, 0)),
        out_shape=jax.ShapeDtypeStruct((s, LANES), F32), compiler_params=_params("parallel"),
    )(o, do)


def _flash_nat_bwd_call(q, k, v, lse_row, delta_row, do, parts):
    s = q.shape[0]
    t = min(FLASH_T, s)
    nb = s // t
    d = LANES
    n_arr = len(parts)

    def body(*refs):
        q_ref, k_ref, v_ref, lse_ref, delta_ref, do_ref = refs[:6]
        part_refs = refs[6:6 + n_arr]
        dq_ref, dk_ref, dv_ref = refs[6 + n_arr:9 + n_arr]
        received_refs = refs[9 + n_arr:9 + 2 * n_arr]
        dq_acc, dk_acc, dv_acc = refs[9 + 2 * n_arr:12 + 2 * n_arr]
        pair, j = pl.program_id(0), pl.program_id(1)
        exchange_start, exchange_finish = _exchange_chips_phases(part_refs, received_refs, *refs[12 + 2 * n_arr:])

        @pl.when((pair == 0) & (j == 0))
        def _():
            exchange_start()

        @pl.when(j == 0)
        def _():
            dq_acc[...] = jnp.zeros_like(dq_acc)

        for hh in range(2):
            kb, vb = k_ref[:, hh * d:(hh + 1) * d], v_ref[:, hh * d:(hh + 1) * d]
            dk_acc[...] = jnp.zeros_like(dk_acc)
            dv_acc[...] = jnp.zeros_like(dv_acc)

            def step(i, on_diagonal, hh=hh, kb=kb, vb=vb):
                rows = pl.ds(pl.multiple_of(i * t, t), t)
                qb = q_ref[rows, hh * d:(hh + 1) * d]
                do_pair = do_ref[rows, :].astype(F32)
                do_h = do_pair if hh == 0 else pltpu.roll(do_pair, HALF, 1)
                dob = jnp.where(_low_half(do_h.shape), do_h, 0.0).astype(BF16)
                sc_t = lax.dot_general(kb, qb, NT_DIMS, preferred_element_type=F32)
                p_t = jnp.exp2(sc_t * EXP2_SCALE - lse_ref[hh, :, rows])
                if on_diagonal:
                    key = lax.broadcasted_iota(jnp.int32, (t, t), 0)
                    qry = lax.broadcasted_iota(jnp.int32, (t, t), 1)
                    p_t = jnp.where(qry >= key, p_t, 0.0)
                dp_t = lax.dot_general(vb, dob, NT_DIMS, preferred_element_type=F32)
                ds_t = (p_t * (dp_t - delta_ref[hh, :, rows])).astype(BF16)
                dv_acc[...] += jnp.dot(p_t.astype(BF16), dob, preferred_element_type=F32)
                dk_acc[...] += jnp.dot(ds_t, qb, preferred_element_type=F32)
                dq_acc[hh, rows, :] += lax.dot_general(ds_t, kb, TN_DIMS, preferred_element_type=F32)

            def above(i, carry, step=step):
                step(i, False)
                return carry

            step(j, True)
            lax.fori_loop(j + 1, nb, above, 0)
            dk_ref[:, hh * d:(hh + 1) * d] = (dk_acc[...] * MLA_SCALE).astype(dk_ref.dtype)
            dv_ref[:, hh * d:(hh + 1) * d] = dv_acc[...].astype(dv_ref.dtype)

        @pl.when(j == nb - 1)
        def _():
            for hh in range(2):
                dq_ref[:, hh * d:(hh + 1) * d] = (dq_acc[hh] * MLA_SCALE).astype(dq_ref.dtype)

        @pl.when((pair == N_PAIR - 1) & (j == nb - 1))
        def _():
            exchange_finish()

    full_spec = pl.BlockSpec((s, 2 * d), lambda p, j: (0, p))
    tile_spec = pl.BlockSpec((t, 2 * d), lambda p, j: (j, p))
    row_spec = pl.BlockSpec((2, 1, s), lambda p, j: (p, 0, 0))
    return pl.pallas_call(
        body, name="mla_bwd", grid=(N_PAIR, nb),
        in_specs=[full_spec, tile_spec, tile_spec, row_spec, row_spec, pl.BlockSpec((s, d), lambda p, j: (0, p))]
                 + [HBM_SPEC] * n_arr,
        out_specs=[full_spec, tile_spec, tile_spec] + [HBM_SPEC] * n_arr,
        out_shape=[jax.ShapeDtypeStruct(q.shape, q.dtype)] * 3 + [jax.ShapeDtypeStruct(p.shape, p.dtype) for p in parts],
        scratch_shapes=[pltpu.VMEM((2, s, d), F32), pltpu.VMEM((t, d), F32), pltpu.VMEM((t, d), F32)]
                       + _exchange_chips_sems(n_arr),
        compiler_params=_params("arbitrary", "arbitrary"),
    )(q, k, v, lse_row, delta_row, do, *parts)


def _reduce_scatter_head(cts, tag):
    received = _exchange_sibling(list(cts), tag + "_exchange_sibling")
    my_c = lax.axis_index("c").astype(jnp.int32).reshape(1)
    return [_pair_add(m, r, my_c, "%s_pair_add_%d" % (tag, i)) for i, (m, r) in enumerate(zip(cts, received))]


def _reduce_scatter_tail(chip_parts, tag):
    return tuple(_sum_blocks(r, "%s_sum_%d" % (tag, i)) for i, r in enumerate(chip_parts))


@jax.custom_vjp
def flash_nat(q, k, v, shards):
    out = _flash_nat_fwd_call(q, k, v, [s.astype(BF16) for s in shards])
    return out[0], tuple(out[2:])


def _flash_nat_fwd(q, k, v, shards):
    out = _flash_nat_fwd_call(q, k, v, [s.astype(BF16) for s in shards])
    return (out[0], tuple(out[2:])), (q, k, v, out[0], out[1])


def _flash_nat_bwd(res, cts):
    q, k, v, o, lse = res
    do, d_gathered = cts
    delta = _flash_nat_delta_call(o, do)[:, :B_HEADS].T.reshape(B_HEADS, 1, q.shape[0])
    out = _flash_nat_bwd_call(q, k, v, lse, delta, do, _reduce_scatter_head(d_gathered, "mlp_grads"))
    return out[0], out[1], out[2], _reduce_scatter_tail(out[3:], "mlp_grads")


flash_nat.defvjp(_flash_nat_fwd, _flash_nat_bwd)


HBM_SPEC = pl.BlockSpec(memory_space=pltpu.HBM)


def _allgather(shards, name):
    n_arr = len(shards)

    def body(*refs):
        start, forward, finish = _allgather_phases(refs[:n_arr], refs[n_arr:2 * n_arr], *refs[2 * n_arr:])
        start()
        forward()
        finish()

    return pl.pallas_call(
        body, name=name, out_shape=_allgather_out_shapes(shards),
        in_specs=[HBM_SPEC] * n_arr, out_specs=[HBM_SPEC] * n_arr,
        scratch_shapes=_allgather_sems(n_arr),
    )(*shards)


def _allgather_out_shapes(shards):
    return [jax.ShapeDtypeStruct((N_DEV,) + s.shape, s.dtype) for s in shards]


def _allgather_sems(n_arr):
    return [pltpu.SemaphoreType.DMA((7, n_arr)), pltpu.SemaphoreType.DMA((7, n_arr)), pltpu.SemaphoreType.DMA((n_arr,))]


def _allgather_phases(x_refs, out_refs, send_sems, recv_sems, local_sems):
    arrays = range(len(x_refs))
    x, y, c = lax.axis_index("x"), lax.axis_index("y"), lax.axis_index("c")
    me, sibling = (x, y, c), (x, y, 1 - c)
    chips = [(1 - x, y), (x, 1 - y), (1 - x, 1 - y)]

    def rows(a, px, py, pc):
        return out_refs[a].at[4 * px + 2 * py + pc]

    def copy(a, k, block, to, src=None):
        return pltpu.make_async_remote_copy(
            src_ref=rows(a, *block) if src is None else src, dst_ref=rows(a, *block),
            send_sem=send_sems.at[k, a], recv_sem=recv_sems.at[k, a], device_id=to, device_id_type=MESH_ID)

    def mine():
        return [pltpu.make_async_copy(x_refs[a], rows(a, *me), local_sems.at[a]) for a in arrays]

    def first():
        return [cp for a in arrays for cp in
                [copy(a, 0, me, sibling, src=x_refs[a])]
                + [copy(a, 1 + j, me, (*chip, c), src=x_refs[a]) for j, chip in enumerate(chips)]]

    def passed():
        return [copy(a, 4 + j, (*chip, c), sibling) for j, chip in enumerate(chips) for a in arrays]

    def start():
        for cp in mine() + first():
            cp.start()

    def forward():
        for j, chip in enumerate(chips):
            for a in arrays:
                copy(a, 1 + j, (*chip, c), me).wait_recv()
                copy(a, 4 + j, (*chip, c), sibling).start()

    def finish():
        for a in arrays:
            copy(a, 0, sibling, me).wait_recv()
        for j, chip in enumerate(chips):
            for a in arrays:
                copy(a, 4 + j, (*chip, 1 - c), me).wait_recv()
        for cp in first() + passed():
            cp.wait_send()
        for cp in mine():
            cp.wait()

    return start, forward, finish


N_CHIP = 4


def _exchange_sibling(parts, name):
    n_arr = len(parts)

    def body(*refs):
        in_refs, recv_refs = refs[:n_arr], refs[n_arr:2 * n_arr]
        send_sems, recv_sems = refs[2 * n_arr:]
        x, y, c = lax.axis_index("x"), lax.axis_index("y"), lax.axis_index("c")
        copies = []
        for a in range(n_arr):
            for q in range(N_CHIP):
                copies.append(pltpu.make_async_remote_copy(
                    src_ref=in_refs[a].at[2 * q + 1 - c], dst_ref=recv_refs[a].at[q],
                    send_sem=send_sems.at[q, a], recv_sem=recv_sems.at[q, a],
                    device_id=(x, y, 1 - c), device_id_type=MESH_ID))
        for cp in copies:
            cp.start()
        for cp in copies:
            cp.wait()

    return pl.pallas_call(
        body, name=name, out_shape=[jax.ShapeDtypeStruct((N_CHIP,) + p.shape[1:], p.dtype) for p in parts],
        in_specs=[HBM_SPEC] * n_arr, out_specs=[HBM_SPEC] * n_arr,
        scratch_shapes=[pltpu.SemaphoreType.DMA((N_CHIP, n_arr)), pltpu.SemaphoreType.DMA((N_CHIP, n_arr))],
    )(*parts)


def _exchange_chips(parts, name):
    n_arr = len(parts)

    def body(*refs):
        start, finish = _exchange_chips_phases(refs[:n_arr], refs[n_arr:2 * n_arr], *refs[2 * n_arr:])
        start()
        finish()

    return pl.pallas_call(
        body, name=name, out_shape=[jax.ShapeDtypeStruct(p.shape, p.dtype) for p in parts],
        in_specs=[HBM_SPEC] * n_arr, out_specs=[HBM_SPEC] * n_arr,
        scratch_shapes=_exchange_chips_sems(n_arr),
    )(*parts)


def _exchange_chips_sems(n_arr):
    return [pltpu.SemaphoreType.DMA((N_CHIP - 1, n_arr)), pltpu.SemaphoreType.DMA((N_CHIP - 1, n_arr)),
            pltpu.SemaphoreType.DMA((n_arr,))]


def _exchange_chips_phases(in_refs, out_refs, send_sems, recv_sems, local_sems):
    n_arr = len(in_refs)
    x, y, c = lax.axis_index("x"), lax.axis_index("y"), lax.axis_index("c")
    me = 2 * x + y

    def copies():
        out = [pltpu.make_async_copy(in_refs[a].at[me], out_refs[a].at[me], local_sems.at[a]) for a in range(n_arr)]
        for k in range(1, N_CHIP):
            px = 1 - x if k & 2 else x
            py = 1 - y if k & 1 else y
            for a in range(n_arr):
                out.append(pltpu.make_async_remote_copy(
                    src_ref=in_refs[a].at[2 * px + py], dst_ref=out_refs[a].at[me],
                    send_sem=send_sems.at[k - 1, a], recv_sem=recv_sems.at[k - 1, a],
                    device_id=(px, py, c), device_id_type=MESH_ID))
        return out

    def start():
        for cp in copies():
            cp.start()

    def finish():
        for cp in copies():
            cp.wait()

    return start, finish


def _row_tile(r, ccols, blocks):
    cap = max(16, (2 * 1024 * 1024) // (4 * ccols * blocks))
    return _pick(r, cap, 16)


def _pair_add(mine, theirs, my_c, name):
    _, r, ccols = mine.shape
    tr = _row_tile(r, ccols, 1)

    def body(c_ref, a_ref, b_ref, o_ref):
        o_ref[...] = (a_ref[...].astype(F32) + b_ref[...].astype(F32)).astype(o_ref.dtype)

    spec = pl.BlockSpec((None, tr, ccols), lambda q, i, c_ref: (q, i, 0))
    return pl.pallas_call(
        body, name=name,
        grid_spec=pltpu.PrefetchScalarGridSpec(
            num_scalar_prefetch=1, grid=(N_CHIP, r // tr),
            in_specs=[pl.BlockSpec((None, tr, ccols), lambda q, i, c_ref: (2 * q + c_ref[0], i, 0)), spec],
            out_specs=spec),
        out_shape=jax.ShapeDtypeStruct(theirs.shape, theirs.dtype),
        compiler_params=_params("parallel", "parallel"),
    )(my_c, mine, theirs)


def _sum_blocks(parts, name):
    nb, r, ccols = parts.shape
    tr = _row_tile(r, ccols, nb)

    def body(p_ref, o_ref):
        acc = p_ref[0].astype(F32)
        for i in range(1, nb):
            acc = acc + p_ref[i].astype(F32)
        o_ref[...] = acc

    return pl.pallas_call(
        body, name=name, grid=(r // tr,),
        in_specs=[pl.BlockSpec((nb, tr, ccols), lambda i: (0, i, 0))],
        out_specs=pl.BlockSpec((tr, ccols), lambda i: (i, 0)),
        out_shape=jax.ShapeDtypeStruct((r, ccols), F32),
        compiler_params=_params("parallel"),
    )(parts)


def _gather_wire(shards, wire_dtypes):
    return tuple(_allgather([s.astype(d) for s, d in zip(shards, wire_dtypes)], "weights_allgather"))


@functools.partial(jax.custom_vjp, nondiff_argnums=(1,))
def fsdp_gather(shards, wire_dtypes):
    return _gather_wire(shards, wire_dtypes)


def _fsdp_gather_fwd(shards, wire_dtypes):
    return _gather_wire(shards, wire_dtypes), None


def _fsdp_gather_bwd(wire_dtypes, _, cts):
    chip_parts = _exchange_chips(_reduce_scatter_head(cts, "grads"), "grads_exchange_chips")
    return (_reduce_scatter_tail(chip_parts, "grads"),)


fsdp_gather.defvjp(_fsdp_gather_fwd, _fsdp_gather_bwd)


@jax.custom_vjp
def replicated(vec):
    return vec


def _replicated_fwd(vec):
    return vec, None


def _replicated_bwd(_, ct):
    return (_sum_blocks(_allgather([ct], "small_grad_allgather")[0], "small_grad_sum"),)


replicated.defvjp(_replicated_fwd, _replicated_bwd)


def _adamw(w, g, m, v, name):
    rows, cols = w.shape
    tr = _pick(rows, 256, 8) if rows % 8 == 0 else rows

    def body(w_ref, g_ref, m_ref, v_ref, d_ref, nm_ref, nv_ref):
        g_ = g_ref[...]
        m_ = ADAM_B1 * m_ref[...] + (1.0 - ADAM_B1) * g_
        v_ = ADAM_B2 * v_ref[...] + (1.0 - ADAM_B2) * jnp.square(g_)
        m_hat = m_ / (1.0 - ADAM_B1 ** ADAM_STEP)
        v_hat = v_ / (1.0 - ADAM_B2 ** ADAM_STEP)
        d_ref[...] = -ADAM_LR * (m_hat / (jnp.sqrt(v_hat) + ADAM_EPS) + ADAM_WD * w_ref[...])
        nm_ref[...] = m_
        nv_ref[...] = v_

    spec = pl.BlockSpec((tr, cols), lambda i: (i, 0))
    return pl.pallas_call(
        body, name=name, grid=(rows // tr,), in_specs=[spec] * 4, out_specs=[spec] * 3,
        out_shape=[jax.ShapeDtypeStruct(w.shape, F32)] * 3, compiler_params=_params("parallel"),
    )(w, g, m, v)


COL_SHARDED = ("w_in", "w_uq", "w_ukv", "w_branch_a", "w_branch_b", "w_up", "w_ple")
EARLY = ("w_in", "w_uq", "w_ukv", "w_branch_a", "w_branch_b", "w_out")
LATE = ("w_up", "w_down", "w_ple_gate", "w_ple")
SMALL = ("attn_pre_norm", "attn_post_norm", "b_gate", "q_a_norm", "kv_a_norm", "mlp_pre_norm", "mlp_post_norm",
         "conv_b", "ple_norm", "sinks")
SMALL_COLS = 128


def _pack_rows(arrays, cols, row_mult):
    flat = jnp.concatenate([a.reshape(-1) for a in arrays])
    pad = (-flat.shape[0]) % (cols * row_mult)
    return jnp.pad(flat, (0, pad)).reshape(-1, cols)


def _unpack_small(vec, shapes):
    flat = vec.reshape(-1)
    out, off = {}, 0
    for name in SMALL:
        n = shapes[name]
        out[name] = flat[off:off + n].reshape(1, n)
        off += n + (-n) % SMALL_COLS
    return out


def _pad_lanes(t, width):
    return jnp.pad(t, [(0, 0)] * (t.ndim - 1) + [(0, width - t.shape[-1])])


def _pad_rows(t, rows):
    return jnp.pad(t, [(0, 0)] * (t.ndim - 2) + [(0, rows - t.shape[-2]), (0, 0)])


FRONT_SIZES = (512, 128, 128, 256, 128)
FRONT_BOUNDS = (0, 512, 640, 768, 1024, 1152, 1280)
PE_LANE = NOPE_DIM


def _arrange_w_in_t(wt):
    k = wt.shape[1]
    n_front = sum(FRONT_SIZES)
    front, kr, gates = wt[:n_front], wt[n_front:n_front + ROPE_DIM], wt[n_front + ROPE_DIM:]
    kr_slab = jnp.concatenate([jnp.zeros((PE_LANE, k), wt.dtype), kr,
                               jnp.zeros((HEAD_PAD - PE_LANE - ROPE_DIM, k), wt.dtype)], axis=0)
    return jnp.concatenate([front, kr_slab], axis=0), gates


def _arrange_w_uq_t(wt):
    k = wt.shape[1]
    return _pad_rows(wt.reshape(B_HEADS, NOPE_DIM + ROPE_DIM, k), HEAD_PAD).reshape(B_HEADS * HEAD_PAD, k)


def _arrange_w_ukv_t(wt):
    k = wt.shape[1]
    w = wt.reshape(B_HEADS, 2, NOPE_DIM, k)
    slabs = [_pad_rows(w[:, part], HEAD_PAD).reshape(B_HEADS * HEAD_PAD, k) for part in range(2)]
    return jnp.concatenate(slabs, axis=0)


def _rope_tables(positions, s):
    pos = positions.reshape(s, 1).astype(F32)

    def angles(dim):
        return pos * ROPE_THETA ** (-(jnp.arange(0, dim, 2, dtype=F32) / dim))

    cos_a, sin_a = jnp.cos(angles(A_HEAD_DIM)), jnp.sin(angles(A_HEAD_DIM))
    zero_a = jnp.zeros_like(sin_a)
    tables_a = [jnp.tile(jnp.concatenate(pair, axis=1), (1, LANES // A_HEAD_DIM))
                for pair in ((cos_a, cos_a), (-sin_a, zero_a), (zero_a, sin_a))]
    cos_b, sin_b = jnp.cos(angles(ROPE_DIM)), jnp.sin(angles(ROPE_DIM))
    zero_b = jnp.zeros_like(sin_b)

    def slab(first, second, fill):
        return jnp.concatenate([jnp.full((s, PE_LANE), fill, F32), first, second,
                                jnp.full((s, HEAD_PAD - PE_LANE - ROPE_DIM), fill, F32)], axis=1)

    tables_b = [slab(cos_b, cos_b, 1.0), slab(-sin_b, zero_b, 0.0), slab(zero_b, sin_b, 0.0)]
    return tables_a + tables_b


def _local_loss(wts, x, p, tables, target):
    s = x.shape[0]
    small_shapes = {n: wts[n].shape[-1] for n in SMALL}
    small_vec = _pack_rows([_pad_lanes(wts[n].reshape(1, -1), small_shapes[n] + (-small_shapes[n]) % SMALL_COLS)
                            for n in SMALL], SMALL_COLS, 8)
    sm = _unpack_small(replicated(small_vec), small_shapes)
    def shard(n):
        return wts[n].T if n in COL_SHARDED else wts[n]

    gathered = fsdp_gather(tuple([shard(n) for n in EARLY] + [_pack_rows([wts["conv_w"]], SMALL_COLS, 8)]),
                           (BF16,) * len(EARLY) + (F32,))
    big = {n: g.reshape(-1, g.shape[2]) for n, g in zip(EARLY, gathered)}
    ch = wts["conv_w"].shape[1]
    conv_w = gathered[-1].reshape(N_DEV, -1)[:, :CONV_W * ch].reshape(N_DEV, CONV_W, ch)
    conv_w = conv_w.transpose(1, 0, 2).reshape(CONV_W, N_DEV * ch)

    w_front_t, w_gates_t = _arrange_w_in_t(big["w_in"])
    tables_a, tables_b = tables[:3], tables[3:]

    (h1,) = stage("prenorm", _f_prenorm, [x], [sm["attn_pre_norm"]], out_dtypes=[BF16])
    zf = mm(h1, w_front_t, "nt", "w_front", True, F32)
    gates = mm(h1, w_gates_t, "nt", "w_gates", True, F32)
    qa, ka, va, cqn, ckvn, kpe = stage("prep", _f_prep, [zf], [sm["q_a_norm"], sm["kv_a_norm"]], tables,
                                       splits=[FRONT_BOUNDS], out_dtypes=[BF16, BF16, BF16, BF16, BF16, F32])
    ya = swa_nat(qa, ka, va, sm["sinks"].reshape(-1))

    qb = mm(cqn, _arrange_w_uq_t(big["w_uq"]), "nt", "w_uq", True, F32)
    kvb = mm(ckvn, _arrange_w_ukv_t(big["w_ukv"]), "nt", "w_ukv", True, F32)
    (q2,) = stage("qrope", _f_qrope, [qb], [], tables_b, out_dtypes=[BF16])
    k2, v2 = stage("kv", _f_kv, [kvb, kpe], splits=[(0, B_HEADS * HEAD_PAD, 2 * B_HEADS * HEAD_PAD), None],
                   out_dtypes=[BF16, BF16])
    yb, late = flash_nat(q2, k2, v2, tuple(shard(n) for n in LATE))
    big.update({n: g.reshape(-1, g.shape[2]) for n, g in zip(LATE, late)})

    pa = mm(ya, big["w_branch_a"], "nt", "w_branch_a", True, F32)
    pb = mm(yb, big["w_branch_b"], "nt", "w_branch_b", True, F32)
    (mixed,) = stage("gate", _f_gate, [gates, pa, pb], [sm["b_gate"][:, :D_MODEL], sm["b_gate"][:, D_MODEL:]],
                     splits=[(0, D_MODEL, 2 * D_MODEL), None, None], out_dtypes=[BF16])
    o = mm(mixed, big["w_out"], "nn", "w_out", True, F32)
    x1, h2 = stage("post_attn", _f_post, [x, o], [sm["attn_post_norm"], sm["mlp_pre_norm"]], out_dtypes=[F32, BF16])

    up_g = mm(h2, big["w_up"][:D_FF], "nt", "w_up_gate", True, F32)
    up_v = mm(h2, big["w_up"][D_FF:], "nt", "w_up_val", True, F32)
    u_g = dwconv(up_g, conv_w[:, :D_FF], sm["conv_b"][:, :D_FF], "conv_gate")
    u_v = dwconv(up_v, conv_w[:, D_FF:], sm["conv_b"][:, D_FF:], "conv_val")
    (act,) = stage("glu", _f_glu, [u_g, u_v], ts=128, out_dtypes=[BF16])
    ff = mm(act, big["w_down"], "nn", "w_down", True, F32)
    x2, h3 = stage("post_mlp", _f_post, [x1, ff], [sm["mlp_post_norm"], sm["ple_norm"]], out_dtypes=[F32, BF16])

    t = mm(h3, big["w_ple_gate"], "nn", "w_ple_gate", True, F32)
    e = mm(p, big["w_ple"], "nt", "w_ple", False, F32)
    (rowloss,) = stage("loss", _f_out, [x2, t, e], [], [target])
    return jnp.sum(rowloss)


WEIGHTS = ["attn_pre_norm", "attn_post_norm", "w_in", "b_gate", "sinks", "q_a_norm", "w_uq", "kv_a_norm", "w_ukv",
           "w_branch_a", "w_branch_b", "w_out", "mlp_pre_norm", "mlp_post_norm", "w_up", "conv_w", "conv_b",
           "w_down", "ple_norm", "w_ple_gate", "w_ple"]


def kernel(x, p, positions, attn_pre_norm, attn_post_norm, w_in, b_gate, sinks, q_a_norm, w_uq, kv_a_norm, w_ukv, w_branch_a, w_branch_b, w_out, mlp_pre_norm, mlp_post_norm, w_up, conv_w, conv_b, w_down, ple_norm, w_ple_gate, w_ple, loss_target, m_attn_pre_norm, m_attn_post_norm, m_w_in, m_b_gate, m_sinks, m_q_a_norm, m_w_uq, m_kv_a_norm, m_w_ukv, m_w_branch_a, m_w_branch_b, m_w_out, m_mlp_pre_norm, m_mlp_post_norm, m_w_up, m_conv_w, m_conv_b, m_w_down, m_ple_norm, m_w_ple_gate, m_w_ple, v_attn_pre_norm, v_attn_post_norm, v_w_in, v_b_gate, v_sinks, v_q_a_norm, v_w_uq, v_kv_a_norm, v_w_ukv, v_w_branch_a, v_w_branch_b, v_w_out, v_mlp_pre_norm, v_mlp_post_norm, v_w_up, v_conv_w, v_conv_b, v_w_down, v_ple_norm, v_w_ple_gate, v_w_ple):
    given = dict(locals())
    s = x.shape[1]
    wts = {n: given[n][0] if given[n].ndim == 3 else given[n] for n in WEIGHTS}
    tables = _rope_tables(positions, s)
    local_loss, (grads, grad_x) = jax.value_and_grad(_local_loss, argnums=(0, 1))(
        wts, x[0], p[0, 0], tables, loss_target[0])
    loss = lax.psum(local_loss, AXES)

    outs = {"grad": [], "delta": [], "m": [], "v": []}
    for n in WEIGHTS:
        shape = given[n].shape
        w2 = wts[n].reshape(-1, shape[-1])
        g2 = grads[n].reshape(w2.shape)
        delta, new_m, new_v = _adamw(w2, g2, given["m_" + n].reshape(w2.shape), given["v_" + n].reshape(w2.shape),
                                     "adamw_" + n)
        outs["grad"].append(g2.reshape(shape))
        outs["delta"].append(delta.reshape(shape))
        outs["m"].append(new_m.reshape(shape))
        outs["v"].append(new_v.reshape(shape))
    return (loss, grad_x[None], *outs["grad"], *outs["delta"], *outs["m"], *outs["v"])
```

```python
import functools

import numpy as np
import jax
import jax.numpy as jnp
from jax import lax
from jax.experimental import pallas as pl
from jax.experimental.pallas import tpu as pltpu

F32 = jnp.float32
BF16 = jnp.bfloat16
MESH_ID = pl.DeviceIdType.MESH
AXES = ("x", "y", "c")
N_DEV = 8

D_MODEL = 1024
RMS_EPS = 1e-6
ROPE_THETA = 10000.0
SWA_BLOCK = 128
A_HEADS, A_KV_HEADS, A_HEAD_DIM = 8, 2, 64
A_GROUP = A_HEADS // A_KV_HEADS
B_HEADS, Q_LORA, KV_LORA, NOPE_DIM, ROPE_DIM, V_DIM = 8, 256, 128, 64, 32, 64
D_FF = 2816
CONV_W = 3
HEAD_PAD = 128

ADAM_LR, ADAM_B1, ADAM_B2, ADAM_EPS, ADAM_WD, ADAM_STEP = 0.001, 0.9, 0.999, 1e-08, 0.01, 10

VMEM_LIMIT = 48 * 1024 * 1024
MM_TM, MM_TN, MM_TK_TOKENS = 512, 1408, 1024
MM_VMEM_BUDGET = 36 * 1024 * 1024
FLASH_T = 512
CONV_TS = 128


def _params(*sem):
    return pltpu.CompilerParams(dimension_semantics=sem, vmem_limit_bytes=VMEM_LIMIT)


def _pick(dim, cap, mult):
    best = None
    for t in range(mult, min(dim, cap) + 1, mult):
        if dim % t == 0:
            best = t
    return dim if best is None else best


def _divisors(dim, mult):
    return [t for t in range(mult, dim + 1, mult) if dim % t == 0] or [dim]


def _matmul_tiles(m, n, kdim, form, sizes):
    sa, sb, so = sizes
    tk = _pick(kdim, MM_TK_TOKENS, 128) if form == "tn" else kdim
    cap_m = MM_TN if form == "tn" else MM_TM
    best = None
    for tm in _divisors(m, 128):
        for tn in _divisors(n, 128):
            need = 2 * (tm * tk * sa + tk * tn * sb + tm * tn * so) + (tm * tn * 4 if tk != kdim else 0)
            if tm > cap_m or tn > MM_TN or need > MM_VMEM_BUDGET:
                continue
            if best is None or (tm * tn, tm) > (best[0] * best[1], best[0]):
                best = (tm, tn)
    return best[0], best[1], tk


def _matmul(a, b, form, *, out_dtype=F32, name):
    if form == "tn":
        (kdim, m), n = a.shape, b.shape[1]
    else:
        (m, kdim), n = a.shape, (b.shape[1] if form == "nn" else b.shape[0])
    sizes = (a.dtype.itemsize, b.dtype.itemsize, jnp.dtype(out_dtype).itemsize)
    tm, tn, tk = _matmul_tiles(m, n, kdim, form, sizes)
    nk = kdim // tk
    a_spec = (pl.BlockSpec((tk, tm), lambda i, j, k: (k, i)) if form == "tn"
              else pl.BlockSpec((tm, tk), lambda i, j, k: (i, k)))
    b_spec = (pl.BlockSpec((tn, tk), lambda i, j, k: (j, k)) if form == "nt"
              else pl.BlockSpec((tk, tn), lambda i, j, k: (k, j)))
    dims = (((0 if form == "tn" else 1,), (1 if form == "nt" else 0,)), ((), ()))

    def product(a_ref, b_ref):
        return lax.dot_general(a_ref[...].astype(BF16), b_ref[...].astype(BF16), dims, preferred_element_type=F32)

    if nk == 1:
        def body(a_ref, b_ref, o_ref):
            o_ref[...] = product(a_ref, b_ref).astype(o_ref.dtype)

        scratch = []
    else:
        def body(a_ref, b_ref, o_ref, acc_ref):
            k = pl.program_id(2)

            @pl.when(k == 0)
            def _():
                acc_ref[...] = jnp.zeros_like(acc_ref)

            acc_ref[...] += product(a_ref, b_ref)

            @pl.when(k == nk - 1)
            def _():
                o_ref[...] = acc_ref[...].astype(o_ref.dtype)

        scratch = [pltpu.VMEM((tm, tn), F32)]

    return pl.pallas_call(
        body, name=name, grid=(m // tm, n // tn, nk),
        in_specs=[a_spec, b_spec],
        out_specs=pl.BlockSpec((tm, tn), lambda i, j, k: (i, j)),
        out_shape=jax.ShapeDtypeStruct((m, n), out_dtype),
        scratch_shapes=scratch,
        compiler_params=_params("parallel", "parallel", "arbitrary"),
    )(a, b)


def _pairs(bounds):
    return list(zip(bounds[:-1], bounds[1:]))


def _split(v, bounds):
    return [v[:, a:b] for a, b in _pairs(bounds)]


def _stage_build(name, f, tiled, params, consts, splits, ts, out_dtypes, ct_dtypes=None):
    n_t, n_p, n_c = len(tiled), len(params), len(consts)
    ct_dtypes = [t.dtype for t in tiled] if ct_dtypes is None else ct_dtypes
    s = tiled[0].shape[0]
    ts = min(ts, s)
    grid = (s // ts,)
    if splits is None:
        splits = [None] * n_t
    in_bounds = [(0, t.shape[1]) if b is None else tuple(b) for t, b in zip(tiled, splits)]

    def tile_aval(arr):
        return jax.ShapeDtypeStruct((ts, arr.shape[1]), arr.dtype)

    slab_avals = [[jax.ShapeDtypeStruct((ts, e - a), t.dtype) for a, e in _pairs(b)]
                  for t, b in zip(tiled, in_bounds)]
    out_avals = jax.eval_shape(f, slab_avals, list(params), [tile_aval(c) for c in consts])
    out_bounds = [tuple(np.cumsum([0] + [o.shape[1] for o in slabs]).tolist()) for slabs in out_avals]
    out_dtypes = [F32] * len(out_bounds) if out_dtypes is None else out_dtypes
    out_shapes = [jax.ShapeDtypeStruct((s, b[-1]), d) for b, d in zip(out_bounds, out_dtypes)]

    def row_spec(width):
        return pl.BlockSpec((ts, width), lambda i: (i, 0))

    def par_spec(arr):
        return pl.BlockSpec(arr.shape, lambda i: (0, 0))

    in_specs = ([row_spec(t.shape[1]) for t in tiled] + [par_spec(p) for p in params]
                + [row_spec(c.shape[1]) for c in consts])

    def load(refs):
        t = [_split(r[...], b) for r, b in zip(refs[:n_t], in_bounds)]
        p = [r[...] for r in refs[n_t:n_t + n_p]]
        c = [r[...] for r in refs[n_t + n_p:n_t + n_p + n_c]]
        return t, p, c

    def store(refs, values, bounds):
        for ref, slabs, b in zip(refs, values, bounds):
            for v, (a, e) in zip(slabs, _pairs(b)):
                ref[:, a:e] = v.astype(ref.dtype)

    def run_fwd(tiled, params, consts):
        def body(*refs):
            t, p, c = load(refs)
            store(refs[n_t + n_p + n_c:], f(t, p, c), out_bounds)

        return pl.pallas_call(
            body, name=name + "_fwd", grid=grid, in_specs=in_specs,
            out_specs=[row_spec(b[-1]) for b in out_bounds], out_shape=out_shapes,
            compiler_params=_params("parallel"),
        )(*tiled, *params, *consts)

    def run_bwd(tiled, params, consts, cts):
        n_in = n_t + n_p + n_c
        n_o = len(out_bounds)

        def body(*refs):
            t, p, c = load(refs)
            g = [_split(r[...].astype(F32), b) for r, b in zip(refs[n_in:n_in + n_o], out_bounds)]
            _, pull = jax.vjp(lambda t_, p_: f(t_, p_, c), t, p)
            dt, dp = pull(g)
            store(refs[n_in + n_o:n_in + n_o + n_t], dt, in_bounds)
            first = pl.program_id(0) == 0
            for ref, d in zip(refs[n_in + n_o + n_t:], dp):
                @pl.when(first)
                def _(ref=ref):
                    ref[...] = jnp.zeros_like(ref)

                ref[...] += d

        res = pl.pallas_call(
            body, name=name + "_bwd", grid=grid,
            in_specs=in_specs + [row_spec(b[-1]) for b in out_bounds],
            out_specs=[row_spec(t.shape[1]) for t in tiled] + [par_spec(p) for p in params],
            out_shape=[jax.ShapeDtypeStruct(t.shape, d) for t, d in zip(tiled, ct_dtypes)]
                      + [jax.ShapeDtypeStruct(p.shape, F32) for p in params],
            compiler_params=_params("arbitrary"),
        )(*tiled, *params, *consts, *cts)
        return tuple(res[:n_t]), tuple(res[n_t:])

    return run_fwd, run_bwd


def stage(name, f, tiled, params=(), consts=(), splits=None, ts=256, out_dtypes=None):
    tiled, params, consts = tuple(tiled), tuple(params), tuple(consts)
    run_fwd, run_bwd = _stage_build(name, f, tiled, params, consts, splits, ts, out_dtypes)

    @jax.custom_vjp
    def op(tiled, params, consts):
        return tuple(run_fwd(tiled, params, consts))

    def op_fwd(tiled, params, consts):
        return op(tiled, params, consts), (tiled, params, consts)

    def op_bwd(res, cts):
        tiled, params, consts = res
        dt, dp = run_bwd(tiled, params, consts, cts)
        return dt, dp, tuple(jnp.zeros_like(c) for c in consts)

    op.defvjp(op_fwd, op_bwd)
    return op(tiled, params, consts)


def proj_stage(name, f, projections, extra=(), params=(), consts=(), splits=None, ts=256, out_dtypes=None):
    n_z = len(projections)
    forms = [pr[2] for pr in projections]
    names = [pr[3] for pr in projections]
    need_da = [pr[4] for pr in projections]
    extra, params, consts = tuple(extra), tuple(params), tuple(consts)

    def matmuls(a_list, w_list):
        return tuple(_matmul(a, w, form, out_dtype=F32, name=n + "_fwd")
                     for a, w, form, n in zip(a_list, w_list, forms, names))

    def build(zs, ct=False):
        ct_dtypes = [BF16] * n_z + [e.dtype for e in extra] if ct else None
        return _stage_build(name, f, tuple(zs) + extra, params, consts, splits, ts, out_dtypes, ct_dtypes)

    @jax.custom_vjp
    def op(a_list, w_list, extra, params, consts):
        zs = matmuls(a_list, w_list)
        return tuple(build(zs)[0](zs + extra, params, consts))

    def op_fwd(a_list, w_list, extra, params, consts):
        zs = matmuls(a_list, w_list)
        return tuple(build(zs)[0](zs + extra, params, consts)), (a_list, w_list, zs, extra, params, consts)

    def op_bwd(res, cts):
        a_list, w_list, zs, extra, params, consts = res
        dt, dp = build(zs, ct=True)[1](zs + extra, params, consts, cts)
        da_list, dw_list = [], []
        for a, w, dz, form, n, want in zip(a_list, w_list, dt[:n_z], forms, names, need_da):
            if form == "nn":
                da = _matmul(dz, w, "nt", out_dtype=a.dtype, name=n + "_da") if want else jnp.zeros_like(a)
                dw = _matmul(a, dz, "tn", out_dtype=w.dtype, name=n + "_dw")
            else:
                da = _matmul(dz, w, "nn", out_dtype=a.dtype, name=n + "_da") if want else jnp.zeros_like(a)
                dw = _matmul(dz, a, "tn", out_dtype=w.dtype, name=n + "_dw")
            da_list.append(da)
            dw_list.append(dw)
        return tuple(da_list), tuple(dw_list), tuple(dt[n_z:]), dp, tuple(jnp.zeros_like(c) for c in consts)

    op.defvjp(op_fwd, op_bwd)
    return op(tuple(pr[0] for pr in projections), tuple(pr[1] for pr in projections), extra, params, consts)


def _rms(t, g):
    return t * lax.rsqrt(jnp.mean(t * t, axis=-1, keepdims=True) + RMS_EPS) * g


@functools.partial(jax.custom_vjp, nondiff_argnums=(1,))
def _lane_roll(t, shift):
    return pltpu.roll(t, shift % t.shape[-1], t.ndim - 1)


def _lane_roll_fwd(t, shift):
    return _lane_roll(t, shift), None


def _lane_roll_bwd(shift, _, ct):
    return (pltpu.roll(ct, (-shift) % ct.shape[-1], ct.ndim - 1),)


_lane_roll.defvjp(_lane_roll_fwd, _lane_roll_bwd)


def _rope_lanes(t, tables, half):
    reps = t.shape[1] // tables[0].shape[1]
    c, s_lo, s_hi = [jnp.concatenate([tb] * reps, axis=1) if reps > 1 else tb for tb in tables]
    return t * c + _lane_roll(t, -half) * s_lo + _lane_roll(t, half) * s_hi


def _f_prenorm(t, p, c):
    return [[_rms(t[0][0], p[0])]]


def _f_prep(t, p, c):
    qa, ka, va, cq, ckv, kr = t[0]
    return [[_rope_lanes(qa, c[0:3], A_HEAD_DIM // 2)], [_rope_lanes(ka, c[0:3], A_HEAD_DIM // 2)], [va],
            [_rms(cq, p[0])], [_rms(ckv, p[1])], [_rope_lanes(kr, c[3:6], ROPE_DIM // 2)]]


def _f_qrope(t, p, c):
    return [[_rope_lanes(t[0][0], c, ROPE_DIM // 2)]]


def _f_kv(t, p, c):
    (k_nope, v), (k_pe,) = t
    return [[k_nope + jnp.concatenate([k_pe] * B_HEADS, axis=1)], [v]]


def _f_gate(t, p, c):
    (ga, gb), (pa,), (pb,) = t
    ba, bb = p
    return [[jax.nn.sigmoid(ga + ba) * pa + jax.nn.sigmoid(gb + bb) * pb]]


def _f_post(t, p, c):
    (branch,), (residual,) = t
    x1 = residual + _rms(branch, p[0])
    return [[x1], [_rms(x1, p[1])]]


def _f_out(t, p, c):
    (gate,), (emb,), (x2,) = t
    y = x2 + jax.nn.sigmoid(gate) * emb
    err = y - c[0]
    return [[0.5 * jnp.mean(err * err, axis=-1, keepdims=True)]]


def _shift_down(cur, prev, has_prev):
    rows = cur.shape[0]
    row = lax.broadcasted_iota(jnp.int32, cur.shape, 0)
    m1 = prev[7:8, :] * has_prev
    m2 = prev[6:7, :] * has_prev
    u1 = jnp.where(row >= 1, pltpu.roll(cur, 1, 0), m1)
    u2 = jnp.where(row >= 2, pltpu.roll(cur, 2, 0), jnp.where(row == 1, m1, m2))
    return u1, u2


GELU_C = float(np.sqrt(2.0 / np.pi))
GELU_A = 0.044715
HALO = 8


def _gelu_tanh(x):
    x2 = x * x
    th = jnp.tanh(GELU_C * (x + GELU_A * x * x2))
    half = 0.5 * (1.0 + th)
    return x * half, half + 0.5 * x * (1.0 - th * th) * GELU_C * (1.0 + 3.0 * GELU_A * x2)


def _conv3(cur, prev, w_ref, b_ref, has_prev):
    u1, u2 = _shift_down(cur, prev, has_prev)
    return w_ref[2:3, :] * cur + w_ref[1:2, :] * u1 + w_ref[0:1, :] * u2 + b_ref[...], u1, u2


def _mlp_act_specs(s):
    ts = min(CONV_TS, s)
    hb = ts // HALO

    def half_specs(h):
        return [pl.BlockSpec((ts, D_FF), lambda i: (i, h)),
                pl.BlockSpec((HALO, D_FF), lambda i: (jnp.maximum(i * hb - 1, 0), h))]

    def par_specs(h):
        return [pl.BlockSpec((CONV_W, D_FF), lambda i: (0, h)), pl.BlockSpec((1, D_FF), lambda i: (0, h))]

    return ts, hb, half_specs, par_specs


def _mlp_act_fwd_call(up, conv_w, conv_b):
    s = up.shape[0]
    ts, hb, half_specs, par_specs = _mlp_act_specs(s)

    def body(g_ref, gp_ref, v_ref, vp_ref, wg_ref, bg_ref, wv_ref, bv_ref, o_ref):
        has_prev = (pl.program_id(0) > 0).astype(F32)
        u_g, _, _ = _conv3(g_ref[...], gp_ref[...], wg_ref, bg_ref, has_prev)
        u_v, _, _ = _conv3(v_ref[...], vp_ref[...], wv_ref, bv_ref, has_prev)
        o_ref[...] = (_gelu_tanh(u_g)[0] * u_v).astype(o_ref.dtype)

    return pl.pallas_call(
        body, name="mlp_act_fwd", grid=(s // ts,),
        in_specs=half_specs(0) + half_specs(1) + par_specs(0) + par_specs(1),
        out_specs=pl.BlockSpec((ts, D_FF), lambda i: (i, 0)),
        out_shape=jax.ShapeDtypeStruct((s, D_FF), BF16),
        compiler_params=_params("parallel"),
    )(up, up, up, up, conv_w, conv_b, conv_w, conv_b)


def _mlp_act_bwd_call(up, conv_w, conv_b, dact):
    s = up.shape[0]
    ts, hb, half_specs, par_specs = _mlp_act_specs(s)
    nt = s // ts
    ext = ts + HALO
    bf16_rows = 2 * HALO

    def next_spec(rows, h):
        return pl.BlockSpec((rows, D_FF), lambda i: (jnp.minimum((i + 1) * (ts // rows), s // rows - 1), h))

    def body(g_ref, gp_ref, gn_ref, v_ref, vp_ref, vn_ref, wg_ref, bg_ref, wv_ref, bv_ref, da_ref, dan_ref,
             dup_ref, dwg_ref, dbg_ref, dwv_ref, dbv_ref):
        i = pl.program_id(0)
        has_prev, has_next = (i > 0).astype(F32), (i < nt - 1).astype(F32)
        g_ext = jnp.concatenate([g_ref[...], gn_ref[...]], axis=0)
        v_ext = jnp.concatenate([v_ref[...], vn_ref[...]], axis=0)
        u_g, g1, g2 = _conv3(g_ext, gp_ref[...], wg_ref, bg_ref, has_prev)
        u_v, v1, v2 = _conv3(v_ext, vp_ref[...], wv_ref, bv_ref, has_prev)
        da_ext = jnp.concatenate([da_ref[...].astype(F32), dan_ref[...].astype(F32)[0:HALO] * has_next], axis=0)
        act_g, dact_g = _gelu_tanh(u_g)
        du_g = da_ext * u_v * dact_g
        du_v = da_ext * act_g

        @pl.when(i == 0)
        def _():
            for ref in (dwg_ref, dbg_ref, dwv_ref, dbv_ref):
                ref[...] = jnp.zeros_like(ref)

        for du, w_ref, x0, x1, x2, dw_ref, db_ref, lo in ((du_g, wg_ref, g_ext, g1, g2, dwg_ref, dbg_ref, 0),
                                                      (du_v, wv_ref, v_ext, v1, v2, dwv_ref, dbv_ref, D_FF)):
            d1 = pltpu.roll(du, ext - 1, 0)
            d2 = pltpu.roll(du, ext - 2, 0)
            dup = w_ref[2:3, :] * du + w_ref[1:2, :] * d1 + w_ref[0:1, :] * d2
            dup_ref[:, lo:lo + D_FF] = dup[0:ts].astype(dup_ref.dtype)
            own = du[0:ts]
            dw_ref[0:1, :] += jnp.sum(own * x2[0:ts], axis=0, keepdims=True)
            dw_ref[1:2, :] += jnp.sum(own * x1[0:ts], axis=0, keepdims=True)
            dw_ref[2:3, :] += jnp.sum(own * x0[0:ts], axis=0, keepdims=True)
            db_ref[...] += jnp.sum(own, axis=0, keepdims=True)

    par_out = [pl.BlockSpec((CONV_W, D_FF), lambda i: (0, 0)), pl.BlockSpec((1, D_FF), lambda i: (0, 0))]
    par_shapes = [jax.ShapeDtypeStruct((CONV_W, D_FF), F32), jax.ShapeDtypeStruct((1, D_FF), F32)]
    return pl.pallas_call(
        body, name="mlp_act_bwd", grid=(nt,),
        in_specs=(half_specs(0) + [next_spec(HALO, 0)] + half_specs(1) + [next_spec(HALO, 1)]
                  + par_specs(0) + par_specs(1)
                  + [pl.BlockSpec((ts, D_FF), lambda i: (i, 0)), next_spec(bf16_rows, 0)]),
        out_specs=[pl.BlockSpec((ts, 2 * D_FF), lambda i: (i, 0))] + par_out + par_out,
        out_shape=[jax.ShapeDtypeStruct((s, 2 * D_FF), BF16)] + par_shapes + par_shapes,
        compiler_params=_params("arbitrary"),
    )(up, up, up, up, up, up, conv_w, conv_b, conv_w, conv_b, dact, dact)


@jax.custom_vjp
def mlp_up(h2, w_up_t, conv_w, conv_b):
    return _mlp_act_fwd_call(_matmul(h2, w_up_t, "nt", out_dtype=F32, name="w_up_fwd"), conv_w, conv_b)


def _mlp_up_fwd(h2, w_up_t, conv_w, conv_b):
    up = _matmul(h2, w_up_t, "nt", out_dtype=F32, name="w_up_fwd")
    return _mlp_act_fwd_call(up, conv_w, conv_b), (h2, w_up_t, up, conv_w, conv_b)


def _mlp_up_bwd(res, dact):
    h2, w_up_t, up, conv_w, conv_b = res
    dup, dwg, dbg, dwv, dbv = _mlp_act_bwd_call(up, conv_w, conv_b, dact)
    dh2 = _matmul(dup, w_up_t, "nn", out_dtype=h2.dtype, name="w_up_da")
    dw = _matmul(dup, h2, "tn", out_dtype=w_up_t.dtype, name="w_up_dw")
    return dh2, dw, jnp.concatenate([dwg, dwv], axis=1), jnp.concatenate([dbg, dbv], axis=1)


mlp_up.defvjp(_mlp_up_fwd, _mlp_up_bwd)


SWA_ROWS = A_GROUP * SWA_BLOCK


def _swa_sink_rows(sink_ref, g):
    return jnp.concatenate([jnp.full((SWA_BLOCK, 1), sink_ref[g * A_GROUP + h], F32) for h in range(A_GROUP)], axis=0)


def _swa_probs(q, kp, kc, sink, prev_off):
    scale = A_HEAD_DIM ** -0.5
    nt = (((1,), (1,)), ((), ()))
    sp = lax.dot_general(q, kp, nt, preferred_element_type=F32) * scale
    sc = lax.dot_general(q, kc, nt, preferred_element_type=F32) * scale
    qi = lax.broadcasted_iota(jnp.int32, sp.shape, 0) & (SWA_BLOCK - 1)
    kj = lax.broadcasted_iota(jnp.int32, sp.shape, 1)
    sp = jnp.where(kj > qi + prev_off, sp, -jnp.inf)
    sc = jnp.where(kj <= qi, sc, -jnp.inf)
    m = jnp.maximum(jnp.maximum(jnp.max(sp, axis=-1, keepdims=True), jnp.max(sc, axis=-1, keepdims=True)), sink)
    ep, ec, es = jnp.exp(sp - m), jnp.exp(sc - m), jnp.exp(sink - m)
    den = jnp.sum(ep, axis=-1, keepdims=True) + jnp.sum(ec, axis=-1, keepdims=True) + es
    return ep / den, ec / den, es / den


MLA_SCALE = (NOPE_DIM + ROPE_DIM) ** -0.5
EXP2_SCALE = MLA_SCALE * float(np.log2(np.e))
NT_DIMS = (((1,), (1,)), ((), ()))
TN_DIMS = (((0,), (0,)), ((), ()))


LANES = 128
HALF = LANES // 2


def _low_half(shape):
    return lax.broadcasted_iota(jnp.int32, shape, len(shape) - 1) < HALF


def _dup_half(x, g):
    xf = x.astype(F32)
    keep = _low_half(xf.shape) if g == 0 else jnp.logical_not(_low_half(xf.shape))
    xm = jnp.where(keep, xf, 0.0)
    return (xm + pltpu.roll(xm, HALF, 1)).astype(x.dtype)


def _fold_half(r, g):
    total = r + pltpu.roll(r, HALF, 1)
    keep = _low_half(r.shape) if g == 0 else jnp.logical_not(_low_half(r.shape))
    return jnp.where(keep, total, 0.0)


def _swa_stack_heads(ref, g):
    parts = []
    for tile in range(2):
        slab = ref[:, (2 * g + tile) * LANES:(2 * g + tile + 1) * LANES]
        low = _low_half(slab.shape)
        parts += [jnp.where(low, slab, jnp.zeros_like(slab)), jnp.where(low, jnp.zeros_like(slab), slab)]
    return jnp.concatenate(parts, axis=0)


def _swa_unstack_heads(ref, g, rows):
    for tile in range(2):
        a = rows[(2 * tile) * SWA_BLOCK:(2 * tile + 1) * SWA_BLOCK]
        b = rows[(2 * tile + 1) * SWA_BLOCK:(2 * tile + 2) * SWA_BLOCK]
        ref[:, (2 * g + tile) * LANES:(2 * g + tile + 1) * LANES] = jnp.where(_low_half(a.shape), a, b).astype(ref.dtype)


def _swa_nat_specs():
    blk = SWA_BLOCK
    q_spec = pl.BlockSpec((blk, A_HEADS * A_HEAD_DIM), lambda n: (n, 0))
    prev_spec = pl.BlockSpec((blk, LANES), lambda n: (jnp.maximum(n - 1, 0), 0))
    cur_spec = pl.BlockSpec((blk, LANES), lambda n: (n, 0))
    return q_spec, prev_spec, cur_spec, pl.BlockSpec(memory_space=pltpu.SMEM)


def _swa_nat_fwd_call(q, k, v, sinks):
    s = q.shape[0]
    q_spec, prev_spec, cur_spec, sink_spec = _swa_nat_specs()

    def body(q_ref, kp_ref, kc_ref, vp_ref, vc_ref, sink_ref, o_ref):
        prev_off = jnp.where(pl.program_id(0) > 0, 0, SWA_BLOCK)
        for g in range(A_KV_HEADS):
            kp, kc = _dup_half(kp_ref[...], g), _dup_half(kc_ref[...], g)
            vp, vc = _dup_half(vp_ref[...], g), _dup_half(vc_ref[...], g)
            pp, pc, _ = _swa_probs(_swa_stack_heads(q_ref, g), kp, kc, _swa_sink_rows(sink_ref, g), prev_off)
            out = (jnp.dot(pp.astype(BF16), vp, preferred_element_type=F32)
                   + jnp.dot(pc.astype(BF16), vc, preferred_element_type=F32))
            _swa_unstack_heads(o_ref, g, out)

    return pl.pallas_call(
        body, name="swa_fwd", grid=(s // SWA_BLOCK,),
        in_specs=[q_spec, prev_spec, cur_spec, prev_spec, cur_spec, sink_spec],
        out_specs=q_spec, out_shape=jax.ShapeDtypeStruct(q.shape, BF16),
        compiler_params=_params("parallel"),
    )(q, k, k, v, v, sinks)


def _swa_nat_bwd_call(q, k, v, sinks, do):
    s = q.shape[0]
    q_spec, prev_spec, cur_spec, sink_spec = _swa_nat_specs()
    scale = A_HEAD_DIM ** -0.5
    dsink_spec = pl.BlockSpec((A_KV_HEADS, SWA_ROWS, 1), lambda n: (0, 0, 0))

    def body(q_ref, kp_ref, kc_ref, vp_ref, vc_ref, sink_ref, do_ref,
             dq_ref, dkp_ref, dkc_ref, dvp_ref, dvc_ref, dsink_ref):
        n = pl.program_id(0)
        prev_off = jnp.where(n > 0, 0, SWA_BLOCK)

        @pl.when(n == 0)
        def _():
            dsink_ref[...] = jnp.zeros_like(dsink_ref)

        totals = [jnp.zeros((SWA_BLOCK, LANES), F32) for _ in range(4)]
        for g in range(A_KV_HEADS):
            kp, kc = _dup_half(kp_ref[...], g), _dup_half(kc_ref[...], g)
            vp, vc = _dup_half(vp_ref[...], g), _dup_half(vc_ref[...], g)
            qb = _swa_stack_heads(q_ref, g)
            dob = _swa_stack_heads(do_ref, g)
            pp, pc, ps = _swa_probs(qb, kp, kc, _swa_sink_rows(sink_ref, g), prev_off)
            ppb, pcb = pp.astype(BF16), pc.astype(BF16)
            out = jnp.dot(ppb, vp, preferred_element_type=F32) + jnp.dot(pcb, vc, preferred_element_type=F32)
            delta = jnp.sum(dob.astype(F32) * out, axis=-1, keepdims=True)
            dsp = (pp * (lax.dot_general(dob, vp, NT_DIMS, preferred_element_type=F32) - delta)).astype(BF16)
            dsc = (pc * (lax.dot_general(dob, vc, NT_DIMS, preferred_element_type=F32) - delta)).astype(BF16)
            dsink_ref[g] += -ps * delta
            dq = (jnp.dot(dsp, kp, preferred_element_type=F32) + jnp.dot(dsc, kc, preferred_element_type=F32)) * scale
            _swa_unstack_heads(dq_ref, g, dq)
            parts = [lax.dot_general(dsp, qb, TN_DIMS, preferred_element_type=F32) * scale,
                     lax.dot_general(dsc, qb, TN_DIMS, preferred_element_type=F32) * scale,
                     lax.dot_general(ppb, dob, TN_DIMS, preferred_element_type=F32),
                     lax.dot_general(pcb, dob, TN_DIMS, preferred_element_type=F32)]
            totals = [tot + _fold_half(r, g) for tot, r in zip(totals, parts)]
        dkp_ref[...], dkc_ref[...], dvp_ref[...], dvc_ref[...] = totals

    kv_shape = jax.ShapeDtypeStruct(k.shape, F32)
    return pl.pallas_call(
        body, name="swa_bwd", grid=(s // SWA_BLOCK,),
        in_specs=[q_spec, prev_spec, cur_spec, prev_spec, cur_spec, sink_spec, q_spec],
        out_specs=[q_spec, cur_spec, cur_spec, cur_spec, cur_spec, dsink_spec],
        out_shape=[jax.ShapeDtypeStruct(q.shape, q.dtype), kv_shape, kv_shape, kv_shape, kv_shape,
                   jax.ShapeDtypeStruct((A_KV_HEADS, SWA_ROWS, 1), F32)],
        compiler_params=_params("arbitrary"),
    )(q, k, k, v, v, sinks, do)


@jax.custom_vjp
def swa_nat(q, k, v, sinks):
    return _swa_nat_fwd_call(q, k, v, sinks)


def _swa_nat_fwd(q, k, v, sinks):
    return _swa_nat_fwd_call(q, k, v, sinks), (q, k, v, sinks)


def _swa_nat_bwd(res, do):
    q, k, v, sinks = res
    dq, dkp, dkc, dvp, dvc, dsink = _swa_nat_bwd_call(q, k, v, sinks, do)

    def fold(prev_part, cur_part):
        shifted = jnp.concatenate([prev_part[SWA_BLOCK:], jnp.zeros_like(prev_part[:SWA_BLOCK])], axis=0)
        return (cur_part + shifted).astype(k.dtype)

    dsinks = jnp.sum(dsink.reshape(A_HEADS, SWA_BLOCK), axis=1)
    return dq, fold(dkp, dkc), fold(dvp, dvc), dsinks


swa_nat.defvjp(_swa_nat_fwd, _swa_nat_bwd)

N_PAIR = B_HEADS // 2


def _flash_nat_fwd_call(q, k, v, shards):
    s = q.shape[0]
    t = min(FLASH_T, s)
    nb = s // t
    d = LANES
    n_arr = len(shards)

    def body(*refs):
        q_ref, k_ref, v_ref = refs[:3]
        shard_refs = refs[3:3 + n_arr]
        o_ref, lse_ref = refs[3 + n_arr:5 + n_arr]
        gathered_refs = refs[5 + n_arr:5 + 2 * n_arr]
        vt_ref, m_ref, l_ref, acc_ref = refs[5 + 2 * n_arr:9 + 2 * n_arr]
        pair, i = pl.program_id(0), pl.program_id(1)
        ag_start, ag_forward, ag_finish = _allgather_phases(shard_refs, gathered_refs, *refs[9 + 2 * n_arr:])

        @pl.when((pair == 0) & (i == 0))
        def _():
            ag_start()

        @pl.when((pair == N_PAIR // 2) & (i == 0))
        def _():
            ag_forward()

        @pl.when(i == 0)
        def _():
            for hh in range(2):
                for chunk in range(nb):
                    rows = slice(chunk * t, (chunk + 1) * t)
                    vt_ref[hh, :, rows] = v_ref[rows, hh * d:(hh + 1) * d].T

        outs = []
        for hh in range(2):
            qb = q_ref[:, hh * d:(hh + 1) * d]
            m_ref[...] = jnp.full_like(m_ref, -jnp.inf)
            l_ref[...] = jnp.zeros_like(l_ref)
            acc_ref[...] = jnp.zeros_like(acc_ref)

            def step(j, on_diagonal, hh=hh, qb=qb):
                keys = pl.ds(pl.multiple_of(j * t, t), t)
                sc_t = lax.dot_general(k_ref[keys, hh * d:(hh + 1) * d], qb, NT_DIMS, preferred_element_type=F32)
                if on_diagonal:
                    key = lax.broadcasted_iota(jnp.int32, (t, t), 0)
                    qry = lax.broadcasted_iota(jnp.int32, (t, t), 1)
                    sc_t = jnp.where(qry >= key, sc_t, -jnp.inf)
                m_old = m_ref[...]
                m_new = jnp.maximum(m_old, jnp.max(sc_t, axis=0, keepdims=True))
                alpha = jnp.exp2((m_old - m_new) * EXP2_SCALE)
                p_t = jnp.exp2((sc_t - m_new) * EXP2_SCALE)
                l_ref[...] = alpha * l_ref[...] + jnp.sum(p_t, axis=0, keepdims=True)
                acc_ref[...] = alpha * acc_ref[...] + jnp.dot(vt_ref[hh, :, keys], p_t.astype(BF16),
                                                              preferred_element_type=F32)
                m_ref[...] = m_new

            def below(j, carry, step=step):
                step(j, False)
                return carry

            lax.fori_loop(0, i, below, 0)
            step(i, True)
            outs.append((acc_ref[...] / l_ref[...]).T)
            lse_ref[hh] = m_ref[...] * EXP2_SCALE + jnp.log2(l_ref[...])
        o_ref[...] = (outs[0] + pltpu.roll(outs[1], HALF, 1)).astype(o_ref.dtype)

        @pl.when((pair == N_PAIR - 1) & (i == nb - 1))
        def _():
            ag_finish()

    return pl.pallas_call(
        body, name="mla_fwd", grid=(N_PAIR, nb),
        in_specs=[pl.BlockSpec((t, 2 * d), lambda p, i: (i, p)),
                  pl.BlockSpec((s, 2 * d), lambda p, i: (0, p)),
                  pl.BlockSpec((s, 2 * d), lambda p, i: (0, p))] + [HBM_SPEC] * n_arr,
        out_specs=[pl.BlockSpec((t, d), lambda p, i: (i, p)),
                   pl.BlockSpec((2, 1, t), lambda p, i: (p, 0, i))] + [HBM_SPEC] * n_arr,
        out_shape=[jax.ShapeDtypeStruct((s, N_PAIR * d), BF16), jax.ShapeDtypeStruct((B_HEADS, 1, s), F32)]
                  + _allgather_out_shapes(shards),
        scratch_shapes=[pltpu.VMEM((2, d, s), BF16), pltpu.VMEM((1, t), F32), pltpu.VMEM((1, t), F32),
                        pltpu.VMEM((d, t), F32)] + _allgather_sems(n_arr),
        compiler_params=_params("arbitrary", "arbitrary"),
    )(q, k, v, *shards)


def _flash_nat_delta_call(o, do):
    s, w = o.shape
    t = min(FLASH_T, s)

    def body(o_ref, do_ref, out_ref):
        prod = o_ref[...].astype(F32) * do_ref[...].astype(F32)
        lane = lax.broadcasted_iota(jnp.int32, (w, LANES), 0) // V_DIM
        head = lax.broadcasted_iota(jnp.int32, (w, LANES), 1)
        out_ref[...] = jnp.dot(prod, (lane == head).astype(F32), precision=lax.Precision.HIGHEST,
                               preferred_element_type=F32)

    spec = pl.BlockSpec((t, w), lambda i: (i, 0))
    return pl.pallas_call(
        body, name="mla_delta", grid=(s // t,), in_specs=[spec, spec],
        out_specs=pl.BlockSpec((t, LANES), lambda i: (i, 0)),
        out_shape=jax.ShapeDtypeStruct((s, LANES), F32), compiler_params=_params("parallel"),
    )(o, do)


def _flash_nat_bwd_call(q, k, v, lse_row, delta_row, do, parts):
    s = q.shape[0]
    t = min(FLASH_T, s)
    nb = s // t
    d = LANES
    n_arr = len(parts)

    def body(*refs):
        q_ref, k_ref, v_ref, lse_ref, delta_ref, do_ref = refs[:6]
        part_refs = refs[6:6 + n_arr]
        dq_ref, dk_ref, dv_ref = refs[6 + n_arr:9 + n_arr]
        received_refs = refs[9 + n_arr:9 + 2 * n_arr]
        dq_acc, dk_acc, dv_acc = refs[9 + 2 * n_arr:12 + 2 * n_arr]
        pair, j = pl.program_id(0), pl.program_id(1)
        exchange_start, exchange_finish = _exchange_chips_phases(part_refs, received_refs, *refs[12 + 2 * n_arr:])

        @pl.when((pair == 0) & (j == 0))
        def _():
            exchange_start()

        @pl.when(j == 0)
        def _():
            dq_acc[...] = jnp.zeros_like(dq_acc)

        for hh in range(2):
            kb, vb = k_ref[:, hh * d:(hh + 1) * d], v_ref[:, hh * d:(hh + 1) * d]
            dk_acc[...] = jnp.zeros_like(dk_acc)
            dv_acc[...] = jnp.zeros_like(dv_acc)

            def step(i, on_diagonal, hh=hh, kb=kb, vb=vb):
                rows = pl.ds(pl.multiple_of(i * t, t), t)
                qb = q_ref[rows, hh * d:(hh + 1) * d]
                do_pair = do_ref[rows, :].astype(F32)
                do_h = do_pair if hh == 0 else pltpu.roll(do_pair, HALF, 1)
                dob = jnp.where(_low_half(do_h.shape), do_h, 0.0).astype(BF16)
                sc_t = lax.dot_general(kb, qb, NT_DIMS, preferred_element_type=F32)
                p_t = jnp.exp2(sc_t * EXP2_SCALE - lse_ref[hh, :, rows])
                if on_diagonal:
                    key = lax.broadcasted_iota(jnp.int32, (t, t), 0)
                    qry = lax.broadcasted_iota(jnp.int32, (t, t), 1)
                    p_t = jnp.where(qry >= key, p_t, 0.0)
                dp_t = lax.dot_general(vb, dob, NT_DIMS, preferred_element_type=F32)
                ds_t = (p_t * (dp_t - delta_ref[hh, :, rows])).astype(BF16)
                dv_acc[...] += jnp.dot(p_t.astype(BF16), dob, preferred_element_type=F32)
                dk_acc[...] += jnp.dot(ds_t, qb, preferred_element_type=F32)
                dq_acc[hh, rows, :] += lax.dot_general(ds_t, kb, TN_DIMS, preferred_element_type=F32)

            def above(i, carry, step=step):
                step(i, False)
                return carry

            step(j, True)
            lax.fori_loop(j + 1, nb, above, 0)
            dk_ref[:, hh * d:(hh + 1) * d] = (dk_acc[...] * MLA_SCALE).astype(dk_ref.dtype)
            dv_ref[:, hh * d:(hh + 1) * d] = dv_acc[...].astype(dv_ref.dtype)

        @pl.when(j == nb - 1)
        def _():
            for hh in range(2):
                dq_ref[:, hh * d:(hh + 1) * d] = (dq_acc[hh] * MLA_SCALE).astype(dq_ref.dtype)

        @pl.when((pair == N_PAIR - 1) & (j == nb - 1))
        def _():
            exchange_finish()

    full_spec = pl.BlockSpec((s, 2 * d), lambda p, j: (0, p))
    tile_spec = pl.BlockSpec((t, 2 * d), lambda p, j: (j, p))
    row_spec = pl.BlockSpec((2, 1, s), lambda p, j: (p, 0, 0))
    return pl.pallas_call(
        body, name="mla_bwd", grid=(N_PAIR, nb),
        in_specs=[full_spec, tile_spec, tile_spec, row_spec, row_spec, pl.BlockSpec((s, d), lambda p, j: (0, p))]
                 + [HBM_SPEC] * n_arr,
        out_specs=[full_spec, tile_spec, tile_spec] + [HBM_SPEC] * n_arr,
        out_shape=[jax.ShapeDtypeStruct(q.shape, q.dtype)] * 3 + [jax.ShapeDtypeStruct(p.shape, p.dtype) for p in parts],
        scratch_shapes=[pltpu.VMEM((2, s, d), F32), pltpu.VMEM((t, d), F32), pltpu.VMEM((t, d), F32)]
                       + _exchange_chips_sems(n_arr),
        compiler_params=_params("arbitrary", "arbitrary"),
    )(q, k, v, lse_row, delta_row, do, *parts)


def _reduce_scatter_head(cts, tag):
    received = _exchange_sibling(list(cts), tag + "_exchange_sibling")
    my_c = lax.axis_index("c").astype(jnp.int32).reshape(1)
    return [_pair_add(m, r, my_c, "%s_pair_add_%d" % (tag, i)) for i, (m, r) in enumerate(zip(cts, received))]


def _reduce_scatter_tail(chip_parts, tag):
    return tuple(_sum_blocks(r, "%s_sum_%d" % (tag, i)) for i, r in enumerate(chip_parts))


@jax.custom_vjp
def flash_nat(q, k, v, shards):
    out = _flash_nat_fwd_call(q, k, v, [s.astype(BF16) for s in shards])
    return out[0], tuple(out[2:])


def _flash_nat_fwd(q, k, v, shards):
    out = _flash_nat_fwd_call(q, k, v, [s.astype(BF16) for s in shards])
    return (out[0], tuple(out[2:])), (q, k, v, out[0], out[1])


def _flash_nat_bwd(res, cts):
    q, k, v, o, lse = res
    do, d_gathered = cts
    delta = _flash_nat_delta_call(o, do)[:, :B_HEADS].T.reshape(B_HEADS, 1, q.shape[0])
    out = _flash_nat_bwd_call(q, k, v, lse, delta, do, _reduce_scatter_head(d_gathered, "mlp_grads"))
    return out[0], out[1], out[2], _reduce_scatter_tail(out[3:], "mlp_grads")


flash_nat.defvjp(_flash_nat_fwd, _flash_nat_bwd)


HBM_SPEC = pl.BlockSpec(memory_space=pltpu.HBM)


def _allgather(shards, name):
    n_arr = len(shards)

    def body(*refs):
        start, forward, finish = _allgather_phases(refs[:n_arr], refs[n_arr:2 * n_arr], *refs[2 * n_arr:])
        start()
        forward()
        finish()

    return pl.pallas_call(
        body, name=name, out_shape=_allgather_out_shapes(shards),
        in_specs=[HBM_SPEC] * n_arr, out_specs=[HBM_SPEC] * n_arr,
        scratch_shapes=_allgather_sems(n_arr),
    )(*shards)


def _allgather_out_shapes(shards):
    return [jax.ShapeDtypeStruct((N_DEV,) + s.shape, s.dtype) for s in shards]


def _allgather_sems(n_arr):
    return [pltpu.SemaphoreType.DMA((7, n_arr)), pltpu.SemaphoreType.DMA((7, n_arr)), pltpu.SemaphoreType.DMA((n_arr,))]


def _allgather_phases(x_refs, out_refs, send_sems, recv_sems, local_sems):
    arrays = range(len(x_refs))
    x, y, c = lax.axis_index("x"), lax.axis_index("y"), lax.axis_index("c")
    me, sibling = (x, y, c), (x, y, 1 - c)
    chips = [(1 - x, y), (x, 1 - y), (1 - x, 1 - y)]

    def rows(a, px, py, pc):
        return out_refs[a].at[4 * px + 2 * py + pc]

    def copy(a, k, block, to, src=None):
        return pltpu.make_async_remote_copy(
            src_ref=rows(a, *block) if src is None else src, dst_ref=rows(a, *block),
            send_sem=send_sems.at[k, a], recv_sem=recv_sems.at[k, a], device_id=to, device_id_type=MESH_ID)

    def mine():
        return [pltpu.make_async_copy(x_refs[a], rows(a, *me), local_sems.at[a]) for a in arrays]

    def first():
        return [cp for a in arrays for cp in
                [copy(a, 0, me, sibling, src=x_refs[a])]
                + [copy(a, 1 + j, me, (*chip, c), src=x_refs[a]) for j, chip in enumerate(chips)]]

    def passed():
        return [copy(a, 4 + j, (*chip, c), sibling) for j, chip in enumerate(chips) for a in arrays]

    def start():
        for cp in mine() + first():
            cp.start()

    def forward():
        for j, chip in enumerate(chips):
            for a in arrays:
                copy(a, 1 + j, (*chip, c), me).wait_recv()
                copy(a, 4 + j, (*chip, c), sibling).start()

    def finish():
        for a in arrays:
            copy(a, 0, sibling, me).wait_recv()
        for j, chip in enumerate(chips):
            for a in arrays:
                copy(a, 4 + j, (*chip, 1 - c), me).wait_recv()
        for cp in first() + passed():
            cp.wait_send()
        for cp in mine():
            cp.wait()

    return start, forward, finish


N_CHIP = 4


def _exchange_sibling(parts, name):
    n_arr = len(parts)

    def body(*refs):
        in_refs, recv_refs = refs[:n_arr], refs[n_arr:2 * n_arr]
        send_sems, recv_sems = refs[2 * n_arr:]
        x, y, c = lax.axis_index("x"), lax.axis_index("y"), lax.axis_index("c")
        copies = []
        for a in range(n_arr):
            for q in range(N_CHIP):
                copies.append(pltpu.make_async_remote_copy(
                    src_ref=in_refs[a].at[2 * q + 1 - c], dst_ref=recv_refs[a].at[q],
                    send_sem=send_sems.at[q, a], recv_sem=recv_sems.at[q, a],
                    device_id=(x, y, 1 - c), device_id_type=MESH_ID))
        for cp in copies:
            cp.start()
        for cp in copies:
            cp.wait()

    return pl.pallas_call(
        body, name=name, out_shape=[jax.ShapeDtypeStruct((N_CHIP,) + p.shape[1:], p.dtype) for p in parts],
        in_specs=[HBM_SPEC] * n_arr, out_specs=[HBM_SPEC] * n_arr,
        scratch_shapes=[pltpu.SemaphoreType.DMA((N_CHIP, n_arr)), pltpu.SemaphoreType.DMA((N_CHIP, n_arr))],
    )(*parts)


def _exchange_chips(parts, name):
    n_arr = len(parts)

    def body(*refs):
        start, finish = _exchange_chips_phases(refs[:n_arr], refs[n_arr:2 * n_arr], *refs[2 * n_arr:])
        start()
        finish()

    return pl.pallas_call(
        body, name=name, out_shape=[jax.ShapeDtypeStruct(p.shape, p.dtype) for p in parts],
        in_specs=[HBM_SPEC] * n_arr, out_specs=[HBM_SPEC] * n_arr,
        scratch_shapes=_exchange_chips_sems(n_arr),
    )(*parts)


def _exchange_chips_sems(n_arr):
    return [pltpu.SemaphoreType.DMA((N_CHIP - 1, n_arr)), pltpu.SemaphoreType.DMA((N_CHIP - 1, n_arr)),
            pltpu.SemaphoreType.DMA((n_arr,))]


def _exchange_chips_phases(in_refs, out_refs, send_sems, recv_sems, local_sems):
    n_arr = len(in_refs)
    x, y, c = lax.axis_index("x"), lax.axis_index("y"), lax.axis_index("c")
    me = 2 * x + y

    def copies():
        out = [pltpu.make_async_copy(in_refs[a].at[me], out_refs[a].at[me], local_sems.at[a]) for a in range(n_arr)]
        for k in range(1, N_CHIP):
            px = 1 - x if k & 2 else x
            py = 1 - y if k & 1 else y
            for a in range(n_arr):
                out.append(pltpu.make_async_remote_copy(
                    src_ref=in_refs[a].at[2 * px + py], dst_ref=out_refs[a].at[me],
                    send_sem=send_sems.at[k - 1, a], recv_sem=recv_sems.at[k - 1, a],
                    device_id=(px, py, c), device_id_type=MESH_ID))
        return out

    def start():
        for cp in copies():
            cp.start()

    def finish():
        for cp in copies():
            cp.wait()

    return start, finish


def _row_tile(r, ccols, blocks):
    cap = max(16, (2 * 1024 * 1024) // (4 * ccols * blocks))
    return _pick(r, cap, 16)


def _pair_add(mine, theirs, my_c, name):
    _, r, ccols = mine.shape
    tr = _row_tile(r, ccols, 1)

    def body(c_ref, a_ref, b_ref, o_ref):
        o_ref[...] = (a_ref[...].astype(F32) + b_ref[...].astype(F32)).astype(o_ref.dtype)

    spec = pl.BlockSpec((None, tr, ccols), lambda q, i, c_ref: (q, i, 0))
    return pl.pallas_call(
        body, name=name,
        grid_spec=pltpu.PrefetchScalarGridSpec(
            num_scalar_prefetch=1, grid=(N_CHIP, r // tr),
            in_specs=[pl.BlockSpec((None, tr, ccols), lambda q, i, c_ref: (2 * q + c_ref[0], i, 0)), spec],
            out_specs=spec),
        out_shape=jax.ShapeDtypeStruct(theirs.shape, theirs.dtype),
        compiler_params=_params("parallel", "parallel"),
    )(my_c, mine, theirs)


def _sum_blocks(parts, name):
    nb, r, ccols = parts.shape
    tr = _row_tile(r, ccols, nb)

    def body(p_ref, o_ref):
        acc = p_ref[0].astype(F32)
        for i in range(1, nb):
            acc = acc + p_ref[i].astype(F32)
        o_ref[...] = acc

    return pl.pallas_call(
        body, name=name, grid=(r // tr,),
        in_specs=[pl.BlockSpec((nb, tr, ccols), lambda i: (0, i, 0))],
        out_specs=pl.BlockSpec((tr, ccols), lambda i: (i, 0)),
        out_shape=jax.ShapeDtypeStruct((r, ccols), F32),
        compiler_params=_params("parallel"),
    )(parts)


def _gather_wire(shards, wire_dtypes):
    return tuple(_allgather([s.astype(d) for s, d in zip(shards, wire_dtypes)], "weights_allgather"))


@functools.partial(jax.custom_vjp, nondiff_argnums=(1,))
def fsdp_gather(shards, wire_dtypes):
    return _gather_wire(shards, wire_dtypes)


def _fsdp_gather_fwd(shards, wire_dtypes):
    return _gather_wire(shards, wire_dtypes), None


def _fsdp_gather_bwd(wire_dtypes, _, cts):
    chip_parts = _exchange_chips(_reduce_scatter_head(cts, "grads"), "grads_exchange_chips")
    return (_reduce_scatter_tail(chip_parts, "grads"),)


fsdp_gather.defvjp(_fsdp_gather_fwd, _fsdp_gather_bwd)


@jax.custom_vjp
def replicated(vec):
    return vec


def _replicated_fwd(vec):
    return vec, None


def _replicated_bwd(_, ct):
    return (_sum_blocks(_allgather([ct], "small_grad_allgather")[0], "small_grad_sum"),)


replicated.defvjp(_replicated_fwd, _replicated_bwd)


def _adamw(w, g, m, v, name):
    rows, cols = w.shape
    tr = _pick(rows, 256, 8) if rows % 8 == 0 else rows

    def body(w_ref, g_ref, m_ref, v_ref, d_ref, nm_ref, nv_ref):
        g_ = g_ref[...]
        m_ = ADAM_B1 * m_ref[...] + (1.0 - ADAM_B1) * g_
        v_ = ADAM_B2 * v_ref[...] + (1.0 - ADAM_B2) * jnp.square(g_)
        m_hat = m_ / (1.0 - ADAM_B1 ** ADAM_STEP)
        v_hat = v_ / (1.0 - ADAM_B2 ** ADAM_STEP)
        d_ref[...] = -ADAM_LR * (m_hat / (jnp.sqrt(v_hat) + ADAM_EPS) + ADAM_WD * w_ref[...])
        nm_ref[...] = m_
        nv_ref[...] = v_

    spec = pl.BlockSpec((tr, cols), lambda i: (i, 0))
    return pl.pallas_call(
        body, name=name, grid=(rows // tr,), in_specs=[spec] * 4, out_specs=[spec] * 3,
        out_shape=[jax.ShapeDtypeStruct(w.shape, F32)] * 3, compiler_params=_params("parallel"),
    )(w, g, m, v)


COL_SHARDED = ("w_in", "w_uq", "w_ukv", "w_branch_a", "w_branch_b", "w_up", "w_ple")
EARLY = ("w_in", "w_uq", "w_ukv", "w_branch_a", "w_branch_b", "w_out")
LATE = ("w_up", "w_down", "w_ple_gate", "w_ple")
SMALL = ("attn_pre_norm", "attn_post_norm", "b_gate", "q_a_norm", "kv_a_norm", "mlp_pre_norm", "mlp_post_norm",
         "conv_b", "ple_norm", "sinks")
SMALL_COLS = 128


def _pack_rows(arrays, cols, row_mult):
    flat = jnp.concatenate([a.reshape(-1) for a in arrays])
    pad = (-flat.shape[0]) % (cols * row_mult)
    return jnp.pad(flat, (0, pad)).reshape(-1, cols)


def _unpack_small(vec, shapes):
    flat = vec.reshape(-1)
    out, off = {}, 0
    for name in SMALL:
        n = shapes[name]
        out[name] = flat[off:off + n].reshape(1, n)
        off += n + (-n) % SMALL_COLS
    return out


def _pad_lanes(t, width):
    return jnp.pad(t, [(0, 0)] * (t.ndim - 1) + [(0, width - t.shape[-1])])


def _pad_rows(t, rows):
    return jnp.pad(t, [(0, 0)] * (t.ndim - 2) + [(0, rows - t.shape[-2]), (0, 0)])


FRONT_SIZES = (512, 128, 128, 256, 128)
FRONT_BOUNDS = (0, 512, 640, 768, 1024, 1152, 1280)
PE_LANE = NOPE_DIM


def _arrange_w_in_t(wt):
    k = wt.shape[1]
    n_front = sum(FRONT_SIZES)
    front, kr, gates = wt[:n_front], wt[n_front:n_front + ROPE_DIM], wt[n_front + ROPE_DIM:]
    kr_slab = jnp.concatenate([jnp.zeros((PE_LANE, k), wt.dtype), kr,
                               jnp.zeros((HEAD_PAD - PE_LANE - ROPE_DIM, k), wt.dtype)], axis=0)
    return jnp.concatenate([front, kr_slab], axis=0), gates


def _arrange_w_uq_t(wt):
    k = wt.shape[1]
    return _pad_rows(wt.reshape(B_HEADS, NOPE_DIM + ROPE_DIM, k), HEAD_PAD).reshape(B_HEADS * HEAD_PAD, k)


def _arrange_w_ukv_t(wt):
    k = wt.shape[1]
    w = wt.reshape(B_HEADS, 2, NOPE_DIM, k)
    slabs = [_pad_rows(w[:, part], HEAD_PAD).reshape(B_HEADS * HEAD_PAD, k) for part in range(2)]
    return jnp.concatenate(slabs, axis=0)


def _rope_tables(positions, s):
    pos = positions.reshape(s, 1).astype(F32)

    def angles(dim):
        return pos * ROPE_THETA ** (-(jnp.arange(0, dim, 2, dtype=F32) / dim))

    cos_a, sin_a = jnp.cos(angles(A_HEAD_DIM)), jnp.sin(angles(A_HEAD_DIM))
    zero_a = jnp.zeros_like(sin_a)
    tables_a = [jnp.tile(jnp.concatenate(pair, axis=1), (1, LANES // A_HEAD_DIM))
                for pair in ((cos_a, cos_a), (-sin_a, zero_a), (zero_a, sin_a))]
    cos_b, sin_b = jnp.cos(angles(ROPE_DIM)), jnp.sin(angles(ROPE_DIM))
    zero_b = jnp.zeros_like(sin_b)

    def slab(first, second, fill):
        return jnp.concatenate([jnp.full((s, PE_LANE), fill, F32), first, second,
                                jnp.full((s, HEAD_PAD - PE_LANE - ROPE_DIM), fill, F32)], axis=1)

    tables_b = [slab(cos_b, cos_b, 1.0), slab(-sin_b, zero_b, 0.0), slab(zero_b, sin_b, 0.0)]
    return tables_a + tables_b


def _local_loss(wts, x, p, tables, target):
    s = x.shape[0]
    small_shapes = {n: wts[n].shape[-1] for n in SMALL}
    small_vec = _pack_rows([_pad_lanes(wts[n].reshape(1, -1), small_shapes[n] + (-small_shapes[n]) % SMALL_COLS)
                            for n in SMALL], SMALL_COLS, 8)
    sm = _unpack_small(replicated(small_vec), small_shapes)
    def shard(n):
        return wts[n].T if n in COL_SHARDED else wts[n]

    gathered = fsdp_gather(tuple([shard(n) for n in EARLY] + [_pack_rows([wts["conv_w"]], SMALL_COLS, 8)]),
                           (BF16,) * len(EARLY) + (F32,))
    big = {n: g.reshape(-1, g.shape[2]) for n, g in zip(EARLY, gathered)}
    ch = wts["conv_w"].shape[1]
    conv_w = gathered[-1].reshape(N_DEV, -1)[:, :CONV_W * ch].reshape(N_DEV, CONV_W, ch)
    conv_w = conv_w.transpose(1, 0, 2).reshape(CONV_W, N_DEV * ch)

    w_front_t, w_gates_t = _arrange_w_in_t(big["w_in"])
    tables_a, tables_b = tables[:3], tables[3:]

    (h1,) = stage("prenorm", _f_prenorm, [x], [sm["attn_pre_norm"]], out_dtypes=[BF16])
    qa, ka, va, cqn, ckvn, kpe = proj_stage(
        "prep", _f_prep, [(h1, w_front_t, "nt", "w_front", True)], params=[sm["q_a_norm"], sm["kv_a_norm"]],
        consts=tables, splits=[FRONT_BOUNDS], out_dtypes=[BF16, BF16, BF16, BF16, BF16, F32])
    ya = swa_nat(qa, ka, va, sm["sinks"].reshape(-1))

    (q2,) = proj_stage("qrope", _f_qrope, [(cqn, _arrange_w_uq_t(big["w_uq"]), "nt", "w_uq", True)],
                       consts=tables_b, out_dtypes=[BF16])
    k2, v2 = proj_stage("kv", _f_kv, [(ckvn, _arrange_w_ukv_t(big["w_ukv"]), "nt", "w_ukv", True)], extra=[kpe],
                        splits=[(0, B_HEADS * HEAD_PAD, 2 * B_HEADS * HEAD_PAD), None], out_dtypes=[BF16, BF16])
    yb, late = flash_nat(q2, k2, v2, tuple(shard(n) for n in LATE))
    big.update({n: g.reshape(-1, g.shape[2]) for n, g in zip(LATE, late)})

    (mixed,) = proj_stage(
        "gate", _f_gate, [(h1, w_gates_t, "nt", "w_gates", True), (ya, big["w_branch_a"], "nt", "w_branch_a", True),
                          (yb, big["w_branch_b"], "nt", "w_branch_b", True)],
        params=[sm["b_gate"][:, :D_MODEL], sm["b_gate"][:, D_MODEL:]],
        splits=[(0, D_MODEL, 2 * D_MODEL), None, None], out_dtypes=[BF16])
    x1, h2 = proj_stage("post_attn", _f_post, [(mixed, big["w_out"], "nn", "w_out", True)], extra=[x],
                        params=[sm["attn_post_norm"], sm["mlp_pre_norm"]], out_dtypes=[F32, BF16])

    act = mlp_up(h2, big["w_up"], conv_w, sm["conv_b"])
    x2, h3 = proj_stage("post_mlp", _f_post, [(act, big["w_down"], "nn", "w_down", True)], extra=[x1],
                        params=[sm["mlp_post_norm"], sm["ple_norm"]], out_dtypes=[F32, BF16])

    (rowloss,) = proj_stage("loss", _f_out, [(h3, big["w_ple_gate"], "nn", "w_ple_gate", True),
                                             (p, big["w_ple"], "nt", "w_ple", False)], extra=[x2], consts=[target])
    return jnp.sum(rowloss)


WEIGHTS = ["attn_pre_norm", "attn_post_norm", "w_in", "b_gate", "sinks", "q_a_norm", "w_uq", "kv_a_norm", "w_ukv",
           "w_branch_a", "w_branch_b", "w_out", "mlp_pre_norm", "mlp_post_norm", "w_up", "conv_w", "conv_b",
           "w_down", "ple_norm", "w_ple_gate", "w_ple"]


def kernel(x, p, positions, attn_pre_norm, attn_post_norm, w_in, b_gate, sinks, q_a_norm, w_uq, kv_a_norm, w_ukv, w_branch_a, w_branch_b, w_out, mlp_pre_norm, mlp_post_norm, w_up, conv_w, conv_b, w_down, ple_norm, w_ple_gate, w_ple, loss_target, m_attn_pre_norm, m_attn_post_norm, m_w_in, m_b_gate, m_sinks, m_q_a_norm, m_w_uq, m_kv_a_norm, m_w_ukv, m_w_branch_a, m_w_branch_b, m_w_out, m_mlp_pre_norm, m_mlp_post_norm, m_w_up, m_conv_w, m_conv_b, m_w_down, m_ple_norm, m_w_ple_gate, m_w_ple, v_attn_pre_norm, v_attn_post_norm, v_w_in, v_b_gate, v_sinks, v_q_a_norm, v_w_uq, v_kv_a_norm, v_w_ukv, v_w_branch_a, v_w_branch_b, v_w_out, v_mlp_pre_norm, v_mlp_post_norm, v_w_up, v_conv_w, v_conv_b, v_w_down, v_ple_norm, v_w_ple_gate, v_w_ple):
    given = dict(locals())
    s = x.shape[1]
    wts = {n: given[n][0] if given[n].ndim == 3 else given[n] for n in WEIGHTS}
    tables = _rope_tables(positions, s)
    local_loss, (grads, grad_x) = jax.value_and_grad(_local_loss, argnums=(0, 1))(
        wts, x[0], p[0, 0], tables, loss_target[0])
    loss = lax.psum(local_loss, AXES)

    outs = {"grad": [], "delta": [], "m": [], "v": []}
    for n in WEIGHTS:
        shape = given[n].shape
        w2 = wts[n].reshape(-1, shape[-1])
        g2 = grads[n].reshape(w2.shape)
        delta, new_m, new_v = _adamw(w2, g2, given["m_" + n].reshape(w2.shape), given["v_" + n].reshape(w2.shape),
                                     "adamw_" + n)
        outs["grad"].append(g2.reshape(shape))
        outs["delta"].append(delta.reshape(shape))
        outs["m"].append(new_m.reshape(shape))
        outs["v"].append(new_v.reshape(shape))
    return (loss, grad_x[None], *outs["grad"], *outs["delta"], *outs["m"], *outs["v"])
```

```python
import functools

import numpy as np
import jax
import jax.numpy as jnp
from jax import lax
from jax.experimental import pallas as pl
from jax.experimental.pallas import tpu as pltpu

F32 = jnp.float32
BF16 = jnp.bfloat16
MESH_ID = pl.DeviceIdType.MESH
AXES = ("x", "y", "c")
N_DEV = 8

D_MODEL = 1024
RMS_EPS = 1e-6
ROPE_THETA = 10000.0
SWA_BLOCK = 128
A_HEADS, A_KV_HEADS, A_HEAD_DIM = 8, 2, 64
A_GROUP = A_HEADS // A_KV_HEADS
B_HEADS, Q_LORA, KV_LORA, NOPE_DIM, ROPE_DIM, V_DIM = 8, 256, 128, 64, 32, 64
D_FF = 2816
CONV_W = 3
HEAD_PAD = 128

ADAM_LR, ADAM_B1, ADAM_B2, ADAM_EPS, ADAM_WD, ADAM_STEP = 0.001, 0.9, 0.999, 1e-08, 0.01, 10

VMEM_LIMIT = 48 * 1024 * 1024
MM_TM, MM_TN, MM_TK_TOKENS = 512, 1408, 1024
MM_VMEM_BUDGET = 36 * 1024 * 1024
FLASH_T = 1024
CONV_TS = 128
CONV_CHUNK = 256


def _params(*sem):
    return pltpu.CompilerParams(dimension_semantics=sem, vmem_limit_bytes=VMEM_LIMIT)


def _pick(dim, cap, mult):
    best = None
    for t in range(mult, min(dim, cap) + 1, mult):
        if dim % t == 0:
            best = t
    return dim if best is None else best


def _divisors(dim, mult):
    return [t for t in range(mult, dim + 1, mult) if dim % t == 0] or [dim]


def _matmul_tiles(m, n, kdim, form, sizes):
    sa, sb, so = sizes
    tk = _pick(kdim, MM_TK_TOKENS, 128) if form == "tn" else kdim
    cap_m = MM_TN if form == "tn" else MM_TM
    best = None
    for tm in _divisors(m, 128):
        for tn in _divisors(n, 128):
            need = 2 * (tm * tk * sa + tk * tn * sb + tm * tn * so) + (tm * tn * 4 if tk != kdim else 0)
            if tm > cap_m or tn > MM_TN or need > MM_VMEM_BUDGET:
                continue
            if best is None or (tm * tn, tm) > (best[0] * best[1], best[0]):
                best = (tm, tn)
    return best[0], best[1], tk


def _matmul(a, b, form, *, out_dtype=F32, name):
    if form == "tn":
        (kdim, m), n = a.shape, b.shape[1]
    else:
        (m, kdim), n = a.shape, (b.shape[1] if form == "nn" else b.shape[0])
    sizes = (a.dtype.itemsize, b.dtype.itemsize, jnp.dtype(out_dtype).itemsize)
    tm, tn, tk = _matmul_tiles(m, n, kdim, form, sizes)
    nk = kdim // tk
    a_spec = (pl.BlockSpec((tk, tm), lambda i, j, k: (k, i)) if form == "tn"
              else pl.BlockSpec((tm, tk), lambda i, j, k: (i, k)))
    b_spec = (pl.BlockSpec((tn, tk), lambda i, j, k: (j, k)) if form == "nt"
              else pl.BlockSpec((tk, tn), lambda i, j, k: (k, j)))
    dims = (((0 if form == "tn" else 1,), (1 if form == "nt" else 0,)), ((), ()))

    def product(a_ref, b_ref):
        return lax.dot_general(a_ref[...].astype(BF16), b_ref[...].astype(BF16), dims, preferred_element_type=F32)

    if nk == 1:
        def body(a_ref, b_ref, o_ref):
            o_ref[...] = product(a_ref, b_ref).astype(o_ref.dtype)

        scratch = []
    else:
        def body(a_ref, b_ref, o_ref, acc_ref):
            k = pl.program_id(2)

            @pl.when(k == 0)
            def _():
                acc_ref[...] = jnp.zeros_like(acc_ref)

            acc_ref[...] += product(a_ref, b_ref)

            @pl.when(k == nk - 1)
            def _():
                o_ref[...] = acc_ref[...].astype(o_ref.dtype)

        scratch = [pltpu.VMEM((tm, tn), F32)]

    return pl.pallas_call(
        body, name=name, grid=(m // tm, n // tn, nk),
        in_specs=[a_spec, b_spec],
        out_specs=pl.BlockSpec((tm, tn), lambda i, j, k: (i, j)),
        out_shape=jax.ShapeDtypeStruct((m, n), out_dtype),
        scratch_shapes=scratch,
        compiler_params=_params("parallel", "parallel", "arbitrary"),
    )(a, b)


def _pairs(bounds):
    return list(zip(bounds[:-1], bounds[1:]))


def _split(v, bounds):
    return [v[:, a:b] for a, b in _pairs(bounds)]


def _stage_build(name, f, tiled, params, consts, splits, ts, out_dtypes, ct_dtypes=None):
    n_t, n_p, n_c = len(tiled), len(params), len(consts)
    ct_dtypes = [t.dtype for t in tiled] if ct_dtypes is None else ct_dtypes
    s = tiled[0].shape[0]
    ts = min(ts, s)
    grid = (s // ts,)
    if splits is None:
        splits = [None] * n_t
    in_bounds = [(0, t.shape[1]) if b is None else tuple(b) for t, b in zip(tiled, splits)]

    def tile_aval(arr):
        return jax.ShapeDtypeStruct((ts, arr.shape[1]), arr.dtype)

    slab_avals = [[jax.ShapeDtypeStruct((ts, e - a), t.dtype) for a, e in _pairs(b)]
                  for t, b in zip(tiled, in_bounds)]
    out_avals = jax.eval_shape(f, slab_avals, list(params), [tile_aval(c) for c in consts])
    out_bounds = [tuple(np.cumsum([0] + [o.shape[1] for o in slabs]).tolist()) for slabs in out_avals]
    out_dtypes = [F32] * len(out_bounds) if out_dtypes is None else out_dtypes
    out_shapes = [jax.ShapeDtypeStruct((s, b[-1]), d) for b, d in zip(out_bounds, out_dtypes)]

    def row_spec(width):
        return pl.BlockSpec((ts, width), lambda i: (i, 0))

    def par_spec(arr):
        return pl.BlockSpec(arr.shape, lambda i: (0, 0))

    in_specs = ([row_spec(t.shape[1]) for t in tiled] + [par_spec(p) for p in params]
                + [row_spec(c.shape[1]) for c in consts])

    def load(refs):
        t = [_split(r[...], b) for r, b in zip(refs[:n_t], in_bounds)]
        p = [r[...] for r in refs[n_t:n_t + n_p]]
        c = [r[...] for r in refs[n_t + n_p:n_t + n_p + n_c]]
        return t, p, c

    def store(refs, values, bounds):
        for ref, slabs, b in zip(refs, values, bounds):
            for v, (a, e) in zip(slabs, _pairs(b)):
                ref[:, a:e] = v.astype(ref.dtype)

    def run_fwd(tiled, params, consts):
        def body(*refs):
            t, p, c = load(refs)
            store(refs[n_t + n_p + n_c:], f(t, p, c), out_bounds)

        return pl.pallas_call(
            body, name=name + "_fwd", grid=grid, in_specs=in_specs,
            out_specs=[row_spec(b[-1]) for b in out_bounds], out_shape=out_shapes,
            compiler_params=_params("parallel"),
        )(*tiled, *params, *consts)

    def run_bwd(tiled, params, consts, cts):
        n_in = n_t + n_p + n_c
        n_o = len(out_bounds)

        def body(*refs):
            t, p, c = load(refs)
            g = [_split(r[...].astype(F32), b) for r, b in zip(refs[n_in:n_in + n_o], out_bounds)]
            _, pull = jax.vjp(lambda t_, p_: f(t_, p_, c), t, p)
            dt, dp = pull(g)
            store(refs[n_in + n_o:n_in + n_o + n_t], dt, in_bounds)
            first = pl.program_id(0) == 0
            for ref, d in zip(refs[n_in + n_o + n_t:], dp):
                @pl.when(first)
                def _(ref=ref):
                    ref[...] = jnp.zeros_like(ref)

                ref[...] += d

        res = pl.pallas_call(
            body, name=name + "_bwd", grid=grid,
            in_specs=in_specs + [row_spec(b[-1]) for b in out_bounds],
            out_specs=[row_spec(t.shape[1]) for t in tiled] + [par_spec(p) for p in params],
            out_shape=[jax.ShapeDtypeStruct(t.shape, d) for t, d in zip(tiled, ct_dtypes)]
                      + [jax.ShapeDtypeStruct(p.shape, F32) for p in params],
            compiler_params=_params("arbitrary"),
        )(*tiled, *params, *consts, *cts)
        return tuple(res[:n_t]), tuple(res[n_t:])

    return run_fwd, run_bwd


def stage(name, f, tiled, params=(), consts=(), splits=None, ts=256, out_dtypes=None):
    tiled, params, consts = tuple(tiled), tuple(params), tuple(consts)
    run_fwd, run_bwd = _stage_build(name, f, tiled, params, consts, splits, ts, out_dtypes)

    @jax.custom_vjp
    def op(tiled, params, consts):
        return tuple(run_fwd(tiled, params, consts))

    def op_fwd(tiled, params, consts):
        return op(tiled, params, consts), (tiled, params, consts)

    def op_bwd(res, cts):
        tiled, params, consts = res
        dt, dp = run_bwd(tiled, params, consts, cts)
        return dt, dp, tuple(jnp.zeros_like(c) for c in consts)

    op.defvjp(op_fwd, op_bwd)
    return op(tiled, params, consts)


def proj_stage(name, f, projections, extra=(), params=(), consts=(), splits=None, ts=256, out_dtypes=None):
    n_z = len(projections)
    forms = [pr[2] for pr in projections]
    names = [pr[3] for pr in projections]
    need_da = [pr[4] for pr in projections]
    extra, params, consts = tuple(extra), tuple(params), tuple(consts)

    def matmuls(a_list, w_list):
        return tuple(_matmul(a, w, form, out_dtype=F32, name=n + "_fwd")
                     for a, w, form, n in zip(a_list, w_list, forms, names))

    def build(zs, ct=False):
        ct_dtypes = [BF16] * n_z + [e.dtype for e in extra] if ct else None
        return _stage_build(name, f, tuple(zs) + extra, params, consts, splits, ts, out_dtypes, ct_dtypes)

    @jax.custom_vjp
    def op(a_list, w_list, extra, params, consts):
        zs = matmuls(a_list, w_list)
        return tuple(build(zs)[0](zs + extra, params, consts))

    def op_fwd(a_list, w_list, extra, params, consts):
        zs = matmuls(a_list, w_list)
        return tuple(build(zs)[0](zs + extra, params, consts)), (a_list, w_list, zs, extra, params, consts)

    def op_bwd(res, cts):
        a_list, w_list, zs, extra, params, consts = res
        dt, dp = build(zs, ct=True)[1](zs + extra, params, consts, cts)
        da_list, dw_list = [], []
        for a, w, dz, form, n, want in zip(a_list, w_list, dt[:n_z], forms, names, need_da):
            if form == "nn":
                da = _matmul(dz, w, "nt", out_dtype=a.dtype, name=n + "_da") if want else jnp.zeros_like(a)
                dw = _matmul(a, dz, "tn", out_dtype=w.dtype, name=n + "_dw")
            else:
                da = _matmul(dz, w, "nn", out_dtype=a.dtype, name=n + "_da") if want else jnp.zeros_like(a)
                dw = _matmul(dz, a, "tn", out_dtype=w.dtype, name=n + "_dw")
            da_list.append(da)
            dw_list.append(dw)
        return tuple(da_list), tuple(dw_list), tuple(dt[n_z:]), dp, tuple(jnp.zeros_like(c) for c in consts)

    op.defvjp(op_fwd, op_bwd)
    return op(tuple(pr[0] for pr in projections), tuple(pr[1] for pr in projections), extra, params, consts)


def _rms(t, g):
    return t * lax.rsqrt(jnp.mean(t * t, axis=-1, keepdims=True) + RMS_EPS) * g


@functools.partial(jax.custom_vjp, nondiff_argnums=(1,))
def _lane_roll(t, shift):
    return pltpu.roll(t, shift % t.shape[-1], t.ndim - 1)


def _lane_roll_fwd(t, shift):
    return _lane_roll(t, shift), None


def _lane_roll_bwd(shift, _, ct):
    return (pltpu.roll(ct, (-shift) % ct.shape[-1], ct.ndim - 1),)


_lane_roll.defvjp(_lane_roll_fwd, _lane_roll_bwd)


def _rope_lanes(t, tables, half):
    reps = t.shape[1] // tables[0].shape[1]
    c, s_lo, s_hi = [jnp.concatenate([tb] * reps, axis=1) if reps > 1 else tb for tb in tables]
    return t * c + _lane_roll(t, -half) * s_lo + _lane_roll(t, half) * s_hi


def _f_prenorm(t, p, c):
    return [[_rms(t[0][0], p[0])]]


def _f_prep(t, p, c):
    qa, ka, va, cq, ckv, kr = t[0]
    return [[_rope_lanes(qa, c[0:3], A_HEAD_DIM // 2)], [_rope_lanes(ka, c[0:3], A_HEAD_DIM // 2)], [va],
            [_rms(cq, p[0])], [_rms(ckv, p[1])], [_rope_lanes(kr, c[3:6], ROPE_DIM // 2)]]


def _f_qrope(t, p, c):
    return [[_rope_lanes(t[0][0], c, ROPE_DIM // 2)]]


def _f_kv(t, p, c):
    (k_nope, v), (k_pe,) = t
    return [[k_nope + jnp.concatenate([k_pe] * B_HEADS, axis=1)], [v]]


def _f_gate(t, p, c):
    (ga, gb), (pa,), (pb,) = t
    ba, bb = p
    return [[jax.nn.sigmoid(ga + ba) * pa + jax.nn.sigmoid(gb + bb) * pb]]


def _f_post(t, p, c):
    (branch,), (residual,) = t
    x1 = residual + _rms(branch, p[0])
    return [[x1], [_rms(x1, p[1])]]


def _f_out(t, p, c):
    (gate,), (emb,), (x2,) = t
    y = x2 + jax.nn.sigmoid(gate) * emb
    err = y - c[0]
    return [[0.5 * jnp.mean(err * err, axis=-1, keepdims=True)]]


def _shift_down(cur, prev, has_prev):
    full = jnp.concatenate([prev * has_prev, cur], axis=0)
    return pltpu.roll(full, 1, 0)[HALO:], pltpu.roll(full, 2, 0)[HALO:]


GELU_C = float(np.sqrt(2.0 / np.pi))
GELU_A = 0.044715
HALO = 8


def _gelu_tanh(x):
    x2 = x * x
    th = jnp.tanh(x * (GELU_C + (GELU_C * GELU_A) * x2))
    half = 0.5 + 0.5 * th
    return x * half, half + x * (0.5 - 0.5 * (th * th)) * (GELU_C + (3.0 * GELU_C * GELU_A) * x2)


def _row_sum(t):
    return jnp.sum(t, axis=0, keepdims=True)


def _conv3(cur, prev, w_ref, b_ref, has_prev):
    u1, u2 = _shift_down(cur, prev, has_prev)
    return w_ref[2:3, :] * cur + w_ref[1:2, :] * u1 + w_ref[0:1, :] * u2 + b_ref[...], u1, u2


def _mlp_act_specs(s):
    ts = min(CONV_TS, s)
    hb = ts // HALO

    def half_specs(h):
        return [pl.BlockSpec((ts, D_FF), lambda i: (i, h)),
                pl.BlockSpec((HALO, D_FF), lambda i: (jnp.maximum(i * hb - 1, 0), h))]

    def par_specs(h):
        return [pl.BlockSpec((CONV_W, D_FF), lambda i: (0, h)), pl.BlockSpec((1, D_FF), lambda i: (0, h))]

    return ts, hb, half_specs, par_specs


def _mlp_act_fwd_call(up, conv_w, conv_b):
    s = up.shape[0]
    ts, hb, half_specs, par_specs = _mlp_act_specs(s)

    def body(g_ref, gp_ref, v_ref, vp_ref, wg_ref, bg_ref, wv_ref, bv_ref, o_ref):
        has_prev = (pl.program_id(0) > 0).astype(F32)

        def chunk(cidx, carry):
            cols = pl.ds(pl.multiple_of(cidx * CONV_CHUNK, CONV_CHUNK), CONV_CHUNK)
            u_g, _, _ = _conv3(g_ref[:, cols], gp_ref[:, cols], wg_ref.at[:, cols], bg_ref.at[:, cols], has_prev)
            u_v, _, _ = _conv3(v_ref[:, cols], vp_ref[:, cols], wv_ref.at[:, cols], bv_ref.at[:, cols], has_prev)
            o_ref[:, cols] = (_gelu_tanh(u_g)[0] * u_v).astype(o_ref.dtype)
            return carry

        lax.fori_loop(0, D_FF // CONV_CHUNK, chunk, 0)

    return pl.pallas_call(
        body, name="mlp_act_fwd", grid=(s // ts,),
        in_specs=half_specs(0) + half_specs(1) + par_specs(0) + par_specs(1),
        out_specs=pl.BlockSpec((ts, D_FF), lambda i: (i, 0)),
        out_shape=jax.ShapeDtypeStruct((s, D_FF), BF16),
        compiler_params=_params("parallel"),
    )(up, up, up, up, conv_w, conv_b, conv_w, conv_b)


def _mlp_act_bwd_call(up, conv_w, conv_b, dact):
    s = up.shape[0]
    ts, hb, half_specs, par_specs = _mlp_act_specs(s)
    nt = s // ts
    ext = ts + HALO
    bf16_rows = 2 * HALO

    def next_spec(rows, h):
        return pl.BlockSpec((rows, D_FF), lambda i: (jnp.minimum((i + 1) * (ts // rows), s // rows - 1), h))

    def body(g_ref, gp_ref, gn_ref, v_ref, vp_ref, vn_ref, wg_ref, bg_ref, wv_ref, bv_ref, da_ref, dan_ref,
             dup_ref, dwg_ref, dbg_ref, dwv_ref, dbv_ref):
        i = pl.program_id(0)
        has_prev, has_next = (i > 0).astype(F32), (i < nt - 1).astype(F32)

        @pl.when(i == 0)
        def _():
            for ref in (dwg_ref, dbg_ref, dwv_ref, dbv_ref):
                ref[...] = jnp.zeros_like(ref)

        def chunk(cidx, carry):
            cols = pl.ds(pl.multiple_of(cidx * CONV_CHUNK, CONV_CHUNK), CONV_CHUNK)
            g_ext = jnp.concatenate([g_ref[:, cols], gn_ref[:, cols]], axis=0)
            v_ext = jnp.concatenate([v_ref[:, cols], vn_ref[:, cols]], axis=0)
            u_g, g1, g2 = _conv3(g_ext, gp_ref[:, cols], wg_ref.at[:, cols], bg_ref.at[:, cols], has_prev)
            u_v, v1, v2 = _conv3(v_ext, vp_ref[:, cols], wv_ref.at[:, cols], bv_ref.at[:, cols], has_prev)
            da_ext = jnp.concatenate([da_ref[:, cols].astype(F32),
                                      dan_ref[:, cols].astype(F32)[0:HALO] * has_next], axis=0)
            act_g, dact_g = _gelu_tanh(u_g)
            du_g = da_ext * u_v * dact_g
            du_v = da_ext * act_g
            for du, w_ref, x0, x1, x2, dw_ref, db_ref, lo in ((du_g, wg_ref, g_ext, g1, g2, dwg_ref, dbg_ref, 0),
                                                          (du_v, wv_ref, v_ext, v1, v2, dwv_ref, dbv_ref, D_FF)):
                d1 = pltpu.roll(du, ext - 1, 0)
                d2 = pltpu.roll(du, ext - 2, 0)
                dup = w_ref[2:3, cols] * du + w_ref[1:2, cols] * d1 + w_ref[0:1, cols] * d2
                out_cols = pl.ds(pl.multiple_of(lo + cidx * CONV_CHUNK, CONV_CHUNK), CONV_CHUNK)
                dup_ref[:, out_cols] = dup[0:ts].astype(dup_ref.dtype)
                own = du[0:ts]
                dw_ref[0:1, cols] += _row_sum(own * x2[0:ts])
                dw_ref[1:2, cols] += _row_sum(own * x1[0:ts])
                dw_ref[2:3, cols] += _row_sum(own * x0[0:ts])
                db_ref[:, cols] += _row_sum(own)
            return carry

        lax.fori_loop(0, D_FF // CONV_CHUNK, chunk, 0)

    par_out = [pl.BlockSpec((CONV_W, D_FF), lambda i: (0, 0)), pl.BlockSpec((1, D_FF), lambda i: (0, 0))]
    par_shapes = [jax.ShapeDtypeStruct((CONV_W, D_FF), F32), jax.ShapeDtypeStruct((1, D_FF), F32)]
    return pl.pallas_call(
        body, name="mlp_act_bwd", grid=(nt,),
        in_specs=(half_specs(0) + [next_spec(HALO, 0)] + half_specs(1) + [next_spec(HALO, 1)]
                  + par_specs(0) + par_specs(1)
                  + [pl.BlockSpec((ts, D_FF), lambda i: (i, 0)), next_spec(bf16_rows, 0)]),
        out_specs=[pl.BlockSpec((ts, 2 * D_FF), lambda i: (i, 0))] + par_out + par_out,
        out_shape=[jax.ShapeDtypeStruct((s, 2 * D_FF), BF16)] + par_shapes + par_shapes,
        compiler_params=_params("arbitrary"),
    )(up, up, up, up, up, up, conv_w, conv_b, conv_w, conv_b, dact, dact)


@jax.custom_vjp
def mlp_up(h2, w_up_t, conv_w, conv_b):
    return _mlp_act_fwd_call(_matmul(h2, w_up_t, "nt", out_dtype=F32, name="w_up_fwd"), conv_w, conv_b)


def _mlp_up_fwd(h2, w_up_t, conv_w, conv_b):
    up = _matmul(h2, w_up_t, "nt", out_dtype=F32, name="w_up_fwd")
    return _mlp_act_fwd_call(up, conv_w, conv_b), (h2, w_up_t, up, conv_w, conv_b)


def _mlp_up_bwd(res, dact):
    h2, w_up_t, up, conv_w, conv_b = res
    dup, dwg, dbg, dwv, dbv = _mlp_act_bwd_call(up, conv_w, conv_b, dact)
    dh2 = _matmul(dup, w_up_t, "nn", out_dtype=h2.dtype, name="w_up_da")
    dw = _matmul(dup, h2, "tn", out_dtype=w_up_t.dtype, name="w_up_dw")
    return dh2, dw, jnp.concatenate([dwg, dwv], axis=1), jnp.concatenate([dbg, dbv], axis=1)


mlp_up.defvjp(_mlp_up_fwd, _mlp_up_bwd)


SWA_ROWS = A_GROUP * SWA_BLOCK


def _swa_sink_rows(sink_ref, g):
    return jnp.concatenate([jnp.full((SWA_BLOCK, 1), sink_ref[g * A_GROUP + h], F32) for h in range(A_GROUP)], axis=0)


def _swa_probs(q, kp, kc, sink, prev_off):
    scale = A_HEAD_DIM ** -0.5
    nt = (((1,), (1,)), ((), ()))
    sp = lax.dot_general(q, kp, nt, preferred_element_type=F32) * scale
    sc = lax.dot_general(q, kc, nt, preferred_element_type=F32) * scale
    qi = lax.broadcasted_iota(jnp.int32, sp.shape, 0) & (SWA_BLOCK - 1)
    kj = lax.broadcasted_iota(jnp.int32, sp.shape, 1)
    sp = jnp.where(kj > qi + prev_off, sp, -jnp.inf)
    sc = jnp.where(kj <= qi, sc, -jnp.inf)
    m = jnp.maximum(jnp.maximum(jnp.max(sp, axis=-1, keepdims=True), jnp.max(sc, axis=-1, keepdims=True)), sink)
    ep, ec, es = jnp.exp(sp - m), jnp.exp(sc - m), jnp.exp(sink - m)
    den = jnp.sum(ep, axis=-1, keepdims=True) + jnp.sum(ec, axis=-1, keepdims=True) + es
    return ep / den, ec / den, es / den


MLA_SCALE = (NOPE_DIM + ROPE_DIM) ** -0.5
EXP2_SCALE = MLA_SCALE * float(np.log2(np.e))
NT_DIMS = (((1,), (1,)), ((), ()))
TN_DIMS = (((0,), (0,)), ((), ()))


LANES = 128
HALF = LANES // 2


def _low_half(shape):
    return lax.broadcasted_iota(jnp.int32, shape, len(shape) - 1) < HALF


def _dup_half(x, g):
    xf = x.astype(F32)
    keep = _low_half(xf.shape) if g == 0 else jnp.logical_not(_low_half(xf.shape))
    xm = jnp.where(keep, xf, 0.0)
    return (xm + pltpu.roll(xm, HALF, 1)).astype(x.dtype)


def _fold_half(r, g):
    total = r + pltpu.roll(r, HALF, 1)
    keep = _low_half(r.shape) if g == 0 else jnp.logical_not(_low_half(r.shape))
    return jnp.where(keep, total, 0.0)


def _swa_stack_heads(ref, g):
    parts = []
    for tile in range(2):
        slab = ref[:, (2 * g + tile) * LANES:(2 * g + tile + 1) * LANES]
        low = _low_half(slab.shape)
        parts += [jnp.where(low, slab, jnp.zeros_like(slab)), jnp.where(low, jnp.zeros_like(slab), slab)]
    return jnp.concatenate(parts, axis=0)


def _swa_unstack_heads(ref, g, rows):
    for tile in range(2):
        a = rows[(2 * tile) * SWA_BLOCK:(2 * tile + 1) * SWA_BLOCK]
        b = rows[(2 * tile + 1) * SWA_BLOCK:(2 * tile + 2) * SWA_BLOCK]
        ref[:, (2 * g + tile) * LANES:(2 * g + tile + 1) * LANES] = jnp.where(_low_half(a.shape), a, b).astype(ref.dtype)


def _swa_nat_specs():
    blk = SWA_BLOCK
    q_spec = pl.BlockSpec((blk, A_HEADS * A_HEAD_DIM), lambda n: (n, 0))
    prev_spec = pl.BlockSpec((blk, LANES), lambda n: (jnp.maximum(n - 1, 0), 0))
    cur_spec = pl.BlockSpec((blk, LANES), lambda n: (n, 0))
    return q_spec, prev_spec, cur_spec, pl.BlockSpec(memory_space=pltpu.SMEM)


def _swa_nat_fwd_call(q, k, v, sinks):
    s = q.shape[0]
    q_spec, prev_spec, cur_spec, sink_spec = _swa_nat_specs()

    def body(q_ref, kp_ref, kc_ref, vp_ref, vc_ref, sink_ref, o_ref):
        prev_off = jnp.where(pl.program_id(0) > 0, 0, SWA_BLOCK)
        for g in range(A_KV_HEADS):
            kp, kc = _dup_half(kp_ref[...], g), _dup_half(kc_ref[...], g)
            vp, vc = _dup_half(vp_ref[...], g), _dup_half(vc_ref[...], g)
            pp, pc, _ = _swa_probs(_swa_stack_heads(q_ref, g), kp, kc, _swa_sink_rows(sink_ref, g), prev_off)
            out = (jnp.dot(pp.astype(BF16), vp, preferred_element_type=F32)
                   + jnp.dot(pc.astype(BF16), vc, preferred_element_type=F32))
            _swa_unstack_heads(o_ref, g, out)

    return pl.pallas_call(
        body, name="swa_fwd", grid=(s // SWA_BLOCK,),
        in_specs=[q_spec, prev_spec, cur_spec, prev_spec, cur_spec, sink_spec],
        out_specs=q_spec, out_shape=jax.ShapeDtypeStruct(q.shape, BF16),
        compiler_params=_params("parallel"),
    )(q, k, k, v, v, sinks)


def _swa_nat_bwd_call(q, k, v, sinks, do):
    s = q.shape[0]
    q_spec, prev_spec, cur_spec, sink_spec = _swa_nat_specs()
    scale = A_HEAD_DIM ** -0.5
    dsink_spec = pl.BlockSpec((A_KV_HEADS, SWA_ROWS, 1), lambda n: (0, 0, 0))

    def body(q_ref, kp_ref, kc_ref, vp_ref, vc_ref, sink_ref, do_ref,
             dq_ref, dkp_ref, dkc_ref, dvp_ref, dvc_ref, dsink_ref):
        n = pl.program_id(0)
        prev_off = jnp.where(n > 0, 0, SWA_BLOCK)

        @pl.when(n == 0)
        def _():
            dsink_ref[...] = jnp.zeros_like(dsink_ref)

        totals = [jnp.zeros((SWA_BLOCK, LANES), F32) for _ in range(4)]
        for g in range(A_KV_HEADS):
            kp, kc = _dup_half(kp_ref[...], g), _dup_half(kc_ref[...], g)
            vp, vc = _dup_half(vp_ref[...], g), _dup_half(vc_ref[...], g)
            qb = _swa_stack_heads(q_ref, g)
            dob = _swa_stack_heads(do_ref, g)
            pp, pc, ps = _swa_probs(qb, kp, kc, _swa_sink_rows(sink_ref, g), prev_off)
            ppb, pcb = pp.astype(BF16), pc.astype(BF16)
            out = jnp.dot(ppb, vp, preferred_element_type=F32) + jnp.dot(pcb, vc, preferred_element_type=F32)
            delta = jnp.sum(dob.astype(F32) * out, axis=-1, keepdims=True)
            dsp = (pp * (lax.dot_general(dob, vp, NT_DIMS, preferred_element_type=F32) - delta)).astype(BF16)
            dsc = (pc * (lax.dot_general(dob, vc, NT_DIMS, preferred_element_type=F32) - delta)).astype(BF16)
            dsink_ref[g] += -ps * delta
            dq = (jnp.dot(dsp, kp, preferred_element_type=F32) + jnp.dot(dsc, kc, preferred_element_type=F32)) * scale
            _swa_unstack_heads(dq_ref, g, dq)
            parts = [lax.dot_general(dsp, qb, TN_DIMS, preferred_element_type=F32) * scale,
                     lax.dot_general(dsc, qb, TN_DIMS, preferred_element_type=F32) * scale,
                     lax.dot_general(ppb, dob, TN_DIMS, preferred_element_type=F32),
                     lax.dot_general(pcb, dob, TN_DIMS, preferred_element_type=F32)]
            totals = [tot + _fold_half(r, g) for tot, r in zip(totals, parts)]
        dkp_ref[...], dkc_ref[...], dvp_ref[...], dvc_ref[...] = totals

    kv_shape = jax.ShapeDtypeStruct(k.shape, F32)
    return pl.pallas_call(
        body, name="swa_bwd", grid=(s // SWA_BLOCK,),
        in_specs=[q_spec, prev_spec, cur_spec, prev_spec, cur_spec, sink_spec, q_spec],
        out_specs=[q_spec, cur_spec, cur_spec, cur_spec, cur_spec, dsink_spec],
        out_shape=[jax.ShapeDtypeStruct(q.shape, q.dtype), kv_shape, kv_shape, kv_shape, kv_shape,
                   jax.ShapeDtypeStruct((A_KV_HEADS, SWA_ROWS, 1), F32)],
        compiler_params=_params("arbitrary"),
    )(q, k, k, v, v, sinks, do)


@jax.custom_vjp
def swa_nat(q, k, v, sinks):
    return _swa_nat_fwd_call(q, k, v, sinks)


def _swa_nat_fwd(q, k, v, sinks):
    return _swa_nat_fwd_call(q, k, v, sinks), (q, k, v, sinks)


def _swa_nat_bwd(res, do):
    q, k, v, sinks = res
    dq, dkp, dkc, dvp, dvc, dsink = _swa_nat_bwd_call(q, k, v, sinks, do)

    def fold(prev_part, cur_part):
        shifted = jnp.concatenate([prev_part[SWA_BLOCK:], jnp.zeros_like(prev_part[:SWA_BLOCK])], axis=0)
        return (cur_part + shifted).astype(k.dtype)

    dsinks = jnp.sum(dsink.reshape(A_HEADS, SWA_BLOCK), axis=1)
    return dq, fold(dkp, dkc), fold(dvp, dvc), dsinks


swa_nat.defvjp(_swa_nat_fwd, _swa_nat_bwd)

N_PAIR = B_HEADS // 2


def _flash_nat_fwd_call(q, k, v, shards):
    s = q.shape[0]
    t = min(FLASH_T, s)
    nb = s // t
    d = LANES
    n_arr = len(shards)

    def body(*refs):
        q_ref, k_ref, v_ref = refs[:3]
        shard_refs = refs[3:3 + n_arr]
        o_ref, lse_ref = refs[3 + n_arr:5 + n_arr]
        gathered_refs = refs[5 + n_arr:5 + 2 * n_arr]
        vt_ref, m_ref, l_ref, acc_ref = refs[5 + 2 * n_arr:9 + 2 * n_arr]
        pair, i = pl.program_id(0), pl.program_id(1)
        ag_start, ag_forward, ag_finish = _allgather_phases(shard_refs, gathered_refs, *refs[9 + 2 * n_arr:])

        @pl.when((pair == 0) & (i == 0))
        def _():
            ag_start()

        @pl.when((pair == N_PAIR // 2) & (i == 0))
        def _():
            ag_forward()

        @pl.when(i == 0)
        def _():
            for hh in range(2):
                for chunk in range(nb):
                    rows = slice(chunk * t, (chunk + 1) * t)
                    vt_ref[hh, :, rows] = v_ref[rows, hh * d:(hh + 1) * d].T

        outs = []
        for hh in range(2):
            qb = q_ref[:, hh * d:(hh + 1) * d]
            m_ref[...] = jnp.full_like(m_ref, -jnp.inf)
            l_ref[...] = jnp.zeros_like(l_ref)
            acc_ref[...] = jnp.zeros_like(acc_ref)

            def step(j, on_diagonal, hh=hh, qb=qb):
                keys = pl.ds(pl.multiple_of(j * t, t), t)
                sc_t = lax.dot_general(k_ref[keys, hh * d:(hh + 1) * d], qb, NT_DIMS, preferred_element_type=F32)
                if on_diagonal:
                    key = lax.broadcasted_iota(jnp.int32, (t, t), 0)
                    qry = lax.broadcasted_iota(jnp.int32, (t, t), 1)
                    sc_t = jnp.where(qry >= key, sc_t, -jnp.inf)
                m_old = m_ref[...]
                m_new = jnp.maximum(m_old, jnp.max(sc_t, axis=0, keepdims=True))
                alpha = jnp.exp2((m_old - m_new) * EXP2_SCALE)
                p_t = jnp.exp2((sc_t - m_new) * EXP2_SCALE)
                l_ref[...] = alpha * l_ref[...] + jnp.sum(p_t, axis=0, keepdims=True)
                acc_ref[...] = alpha * acc_ref[...] + jnp.dot(vt_ref[hh, :, keys], p_t.astype(BF16),
                                                              preferred_element_type=F32)
                m_ref[...] = m_new

            def below(j, carry, step=step):
                step(j, False)
                return carry

            lax.fori_loop(0, i, below, 0)
            step(i, True)
            outs.append((acc_ref[...] / l_ref[...]).T)
            lse_ref[hh] = m_ref[...] * EXP2_SCALE + jnp.log2(l_ref[...])
        o_ref[...] = (outs[0] + pltpu.roll(outs[1], HALF, 1)).astype(o_ref.dtype)

        @pl.when((pair == N_PAIR - 1) & (i == nb - 1))
        def _():
            ag_finish()

    return pl.pallas_call(
        body, name="mla_fwd", grid=(N_PAIR, nb),
        in_specs=[pl.BlockSpec((t, 2 * d), lambda p, i: (i, p)),
                  pl.BlockSpec((s, 2 * d), lambda p, i: (0, p)),
                  pl.BlockSpec((s, 2 * d), lambda p, i: (0, p))] + [HBM_SPEC] * n_arr,
        out_specs=[pl.BlockSpec((t, d), lambda p, i: (i, p)),
                   pl.BlockSpec((2, 1, t), lambda p, i: (p, 0, i))] + [HBM_SPEC] * n_arr,
        out_shape=[jax.ShapeDtypeStruct((s, N_PAIR * d), BF16), jax.ShapeDtypeStruct((B_HEADS, 1, s), F32)]
                  + _allgather_out_shapes(shards),
        scratch_shapes=[pltpu.VMEM((2, d, s), BF16), pltpu.VMEM((1, t), F32), pltpu.VMEM((1, t), F32),
                        pltpu.VMEM((d, t), F32)] + _allgather_sems(n_arr),
        compiler_params=_params("arbitrary", "arbitrary"),
    )(q, k, v, *shards)


def _flash_nat_delta_call(o, do):
    s, w = o.shape
    t = min(FLASH_T, s)

    def body(o_ref, do_ref, out_ref):
        prod = o_ref[...].astype(F32) * do_ref[...].astype(F32)
        lane = lax.broadcasted_iota(jnp.int32, (w, LANES), 0) // V_DIM
        head = lax.broadcasted_iota(jnp.int32, (w, LANES), 1)
        out_ref[...] = jnp.dot(prod, (lane == head).astype(F32), precision=lax.Precision.HIGHEST,
                               preferred_element_type=F32)

    spec = pl.BlockSpec((t, w), lambda i: (i, 0))
    return pl.pallas_call(
        body, name="mla_delta", grid=(s // t,), in_specs=[spec, spec],
        out_specs=pl.BlockSpec((t, LANES), lambda i: (i, 0)),
        out_shape=jax.ShapeDtypeStruct((s, LANES), F32), compiler_params=_params("parallel"),
    )(o, do)


def _flash_nat_bwd_call(q, k, v, lse_row, delta_row, do, parts):
    s = q.shape[0]
    t = min(FLASH_T, s)
    nb = s // t
    d = LANES
    n_arr = len(parts)

    def body(*refs):
        q_ref, k_ref, v_ref, lse_ref, delta_ref, do_ref = refs[:6]
        part_refs = refs[6:6 + n_arr]
        dq_ref, dk_ref, dv_ref = refs[6 + n_arr:9 + n_arr]
        received_refs = refs[9 + n_arr:9 + 2 * n_arr]
        dq_acc, dk_acc, dv_acc = refs[9 + 2 * n_arr:12 + 2 * n_arr]
        pair, j = pl.program_id(0), pl.program_id(1)
        exchange_start, exchange_finish = _exchange_chips_phases(part_refs, received_refs, *refs[12 + 2 * n_arr:])

        @pl.when((pair == 0) & (j == 0))
        def _():
            exchange_start()

        @pl.when(j == 0)
        def _():
            dq_acc[...] = jnp.zeros_like(dq_acc)

        for hh in range(2):
            kb, vb = k_ref[:, hh * d:(hh + 1) * d], v_ref[:, hh * d:(hh + 1) * d]
            dk_acc[...] = jnp.zeros_like(dk_acc)
            dv_acc[...] = jnp.zeros_like(dv_acc)

            def step(i, on_diagonal, hh=hh, kb=kb, vb=vb):
                rows = pl.ds(pl.multiple_of(i * t, t), t)
                qb = q_ref[rows, hh * d:(hh + 1) * d]
                do_pair = do_ref[rows, :].astype(F32)
                do_h = do_pair if hh == 0 else pltpu.roll(do_pair, HALF, 1)
                dob = jnp.where(_low_half(do_h.shape), do_h, 0.0).astype(BF16)
                sc_t = lax.dot_general(kb, qb, NT_DIMS, preferred_element_type=F32)
                p_t = jnp.exp2(sc_t * EXP2_SCALE - lse_ref[hh, :, rows])
                if on_diagonal:
                    key = lax.broadcasted_iota(jnp.int32, (t, t), 0)
                    qry = lax.broadcasted_iota(jnp.int32, (t, t), 1)
                    p_t = jnp.where(qry >= key, p_t, 0.0)
                dp_t = lax.dot_general(vb, dob, NT_DIMS, preferred_element_type=F32)
                ds_t = (p_t * (dp_t - delta_ref[hh, :, rows])).astype(BF16)
                dv_acc[...] += jnp.dot(p_t.astype(BF16), dob, preferred_element_type=F32)
                dk_acc[...] += jnp.dot(ds_t, qb, preferred_element_type=F32)
                dq_acc[hh, rows, :] += lax.dot_general(ds_t, kb, TN_DIMS, preferred_element_type=F32)

            def above(i, carry, step=step):
                step(i, False)
                return carry

            step(j, True)
            lax.fori_loop(j + 1, nb, above, 0)
            dk_ref[:, hh * d:(hh + 1) * d] = (dk_acc[...] * MLA_SCALE).astype(dk_ref.dtype)
            dv_ref[:, hh * d:(hh + 1) * d] = dv_acc[...].astype(dv_ref.dtype)

        @pl.when(j == nb - 1)
        def _():
            for hh in range(2):
                dq_ref[:, hh * d:(hh + 1) * d] = (dq_acc[hh] * MLA_SCALE).astype(dq_ref.dtype)

        @pl.when((pair == N_PAIR - 1) & (j == nb - 1))
        def _():
            exchange_finish()

    full_spec = pl.BlockSpec((s, 2 * d), lambda p, j: (0, p))
    tile_spec = pl.BlockSpec((t, 2 * d), lambda p, j: (j, p))
    row_spec = pl.BlockSpec((2, 1, s), lambda p, j: (p, 0, 0))
    return pl.pallas_call(
        body, name="mla_bwd", grid=(N_PAIR, nb),
        in_specs=[full_spec, tile_spec, tile_spec, row_spec, row_spec, pl.BlockSpec((s, d), lambda p, j: (0, p))]
                 + [HBM_SPEC] * n_arr,
        out_specs=[full_spec, tile_spec, tile_spec] + [HBM_SPEC] * n_arr,
        out_shape=[jax.ShapeDtypeStruct(q.shape, q.dtype)] * 3 + [jax.ShapeDtypeStruct(p.shape, p.dtype) for p in parts],
        scratch_shapes=[pltpu.VMEM((2, s, d), F32), pltpu.VMEM((t, d), F32), pltpu.VMEM((t, d), F32)]
                       + _exchange_chips_sems(n_arr),
        compiler_params=_params("arbitrary", "arbitrary"),
    )(q, k, v, lse_row, delta_row, do, *parts)


def _reduce_scatter_head(cts, tag):
    received = _exchange_sibling(list(cts), tag + "_exchange_sibling")
    my_c = lax.axis_index("c").astype(jnp.int32).reshape(1)
    return [_pair_add(m, r, my_c, "%s_pair_add_%d" % (tag, i)) for i, (m, r) in enumerate(zip(cts, received))]


def _reduce_scatter_tail(chip_parts, tag):
    return tuple(_sum_blocks(r, "%s_sum_%d" % (tag, i)) for i, r in enumerate(chip_parts))


@jax.custom_vjp
def flash_nat(q, k, v, shards):
    out = _flash_nat_fwd_call(q, k, v, [s.astype(BF16) for s in shards])
    return out[0], tuple(out[2:])


def _flash_nat_fwd(q, k, v, shards):
    out = _flash_nat_fwd_call(q, k, v, [s.astype(BF16) for s in shards])
    return (out[0], tuple(out[2:])), (q, k, v, out[0], out[1])


def _flash_nat_bwd(res, cts):
    q, k, v, o, lse = res
    do, d_gathered = cts
    delta = _flash_nat_delta_call(o, do)[:, :B_HEADS].T.reshape(B_HEADS, 1, q.shape[0])
    out = _flash_nat_bwd_call(q, k, v, lse, delta, do, _reduce_scatter_head(d_gathered, "mlp_grads"))
    return out[0], out[1], out[2], _reduce_scatter_tail(out[3:], "mlp_grads")


flash_nat.defvjp(_flash_nat_fwd, _flash_nat_bwd)


HBM_SPEC = pl.BlockSpec(memory_space=pltpu.HBM)


def _allgather(shards, name):
    n_arr = len(shards)

    def body(*refs):
        start, forward, finish = _allgather_phases(refs[:n_arr], refs[n_arr:2 * n_arr], *refs[2 * n_arr:])
        start()
        forward()
        finish()

    return pl.pallas_call(
        body, name=name, out_shape=_allgather_out_shapes(shards),
        in_specs=[HBM_SPEC] * n_arr, out_specs=[HBM_SPEC] * n_arr,
        scratch_shapes=_allgather_sems(n_arr),
    )(*shards)


def _allgather_out_shapes(shards):
    return [jax.ShapeDtypeStruct((N_DEV,) + s.shape, s.dtype) for s in shards]


def _allgather_sems(n_arr):
    return [pltpu.SemaphoreType.DMA((7, n_arr)), pltpu.SemaphoreType.DMA((7, n_arr)), pltpu.SemaphoreType.DMA((n_arr,))]


def _allgather_phases(x_refs, out_refs, send_sems, recv_sems, local_sems):
    arrays = range(len(x_refs))
    x, y, c = lax.axis_index("x"), lax.axis_index("y"), lax.axis_index("c")
    me, sibling = (x, y, c), (x, y, 1 - c)
    chips = [(1 - x, y), (x, 1 - y), (1 - x, 1 - y)]

    def rows(a, px, py, pc):
        return out_refs[a].at[4 * px + 2 * py + pc]

    def copy(a, k, block, to, src=None):
        return pltpu.make_async_remote_copy(
            src_ref=rows(a, *block) if src is None else src, dst_ref=rows(a, *block),
            send_sem=send_sems.at[k, a], recv_sem=recv_sems.at[k, a], device_id=to, device_id_type=MESH_ID)

    def mine():
        return [pltpu.make_async_copy(x_refs[a], rows(a, *me), local_sems.at[a]) for a in arrays]

    def first():
        return [cp for a in arrays for cp in
                [copy(a, 0, me, sibling, src=x_refs[a])]
                + [copy(a, 1 + j, me, (*chip, c), src=x_refs[a]) for j, chip in enumerate(chips)]]

    def passed():
        return [copy(a, 4 + j, (*chip, c), sibling) for j, chip in enumerate(chips) for a in arrays]

    def start():
        for cp in mine() + first():
            cp.start()

    def forward():
        for j, chip in enumerate(chips):
            for a in arrays:
                copy(a, 1 + j, (*chip, c), me).wait_recv()
                copy(a, 4 + j, (*chip, c), sibling).start()

    def finish():
        for a in arrays:
            copy(a, 0, sibling, me).wait_recv()
        for j, chip in enumerate(chips):
            for a in arrays:
                copy(a, 4 + j, (*chip, 1 - c), me).wait_recv()
        for cp in first() + passed():
            cp.wait_send()
        for cp in mine():
            cp.wait()

    return start, forward, finish


N_CHIP = 4


def _exchange_sibling(parts, name):
    n_arr = len(parts)

    def body(*refs):
        in_refs, recv_refs = refs[:n_arr], refs[n_arr:2 * n_arr]
        send_sems, recv_sems = refs[2 * n_arr:]
        x, y, c = lax.axis_index("x"), lax.axis_index("y"), lax.axis_index("c")
        copies = []
        for a in range(n_arr):
            for q in range(N_CHIP):
                copies.append(pltpu.make_async_remote_copy(
                    src_ref=in_refs[a].at[2 * q + 1 - c], dst_ref=recv_refs[a].at[q],
                    send_sem=send_sems.at[q, a], recv_sem=recv_sems.at[q, a],
                    device_id=(x, y, 1 - c), device_id_type=MESH_ID))
        for cp in copies:
            cp.start()
        for cp in copies:
            cp.wait()

    return pl.pallas_call(
        body, name=name, out_shape=[jax.ShapeDtypeStruct((N_CHIP,) + p.shape[1:], p.dtype) for p in parts],
        in_specs=[HBM_SPEC] * n_arr, out_specs=[HBM_SPEC] * n_arr,
        scratch_shapes=[pltpu.SemaphoreType.DMA((N_CHIP, n_arr)), pltpu.SemaphoreType.DMA((N_CHIP, n_arr))],
    )(*parts)


def _exchange_chips(parts, name):
    n_arr = len(parts)

    def body(*refs):
        start, finish = _exchange_chips_phases(refs[:n_arr], refs[n_arr:2 * n_arr], *refs[2 * n_arr:])
        start()
        finish()

    return pl.pallas_call(
        body, name=name, out_shape=[jax.ShapeDtypeStruct(p.shape, p.dtype) for p in parts],
        in_specs=[HBM_SPEC] * n_arr, out_specs=[HBM_SPEC] * n_arr,
        scratch_shapes=_exchange_chips_sems(n_arr),
    )(*parts)


def _exchange_chips_sems(n_arr):
    return [pltpu.SemaphoreType.DMA((N_CHIP - 1, n_arr)), pltpu.SemaphoreType.DMA((N_CHIP - 1, n_arr)),
            pltpu.SemaphoreType.DMA((n_arr,))]


def _exchange_chips_phases(in_refs, out_refs, send_sems, recv_sems, local_sems):
    n_arr = len(in_refs)
    x, y, c = lax.axis_index("x"), lax.axis_index("y"), lax.axis_index("c")
    me = 2 * x + y

    def copies():
        out = [pltpu.make_async_copy(in_refs[a].at[me], out_refs[a].at[me], local_sems.at[a]) for a in range(n_arr)]
        for k in range(1, N_CHIP):
            px = 1 - x if k & 2 else x
            py = 1 - y if k & 1 else y
            for a in range(n_arr):
                out.append(pltpu.make_async_remote_copy(
                    src_ref=in_refs[a].at[2 * px + py], dst_ref=out_refs[a].at[me],
                    send_sem=send_sems.at[k - 1, a], recv_sem=recv_sems.at[k - 1, a],
                    device_id=(px, py, c), device_id_type=MESH_ID))
        return out

    def start():
        for cp in copies():
            cp.start()

    def finish():
        for cp in copies():
            cp.wait()

    return start, finish


def _row_tile(r, ccols, blocks):
    cap = max(16, (2 * 1024 * 1024) // (4 * ccols * blocks))
    return _pick(r, cap, 16)


def _pair_add(mine, theirs, my_c, name):
    _, r, ccols = mine.shape
    tr = _row_tile(r, ccols, 1)

    def body(c_ref, a_ref, b_ref, o_ref):
        o_ref[...] = (a_ref[...].astype(F32) + b_ref[...].astype(F32)).astype(o_ref.dtype)

    spec = pl.BlockSpec((None, tr, ccols), lambda q, i, c_ref: (q, i, 0))
    return pl.pallas_call(
        body, name=name,
        grid_spec=pltpu.PrefetchScalarGridSpec(
            num_scalar_prefetch=1, grid=(N_CHIP, r // tr),
            in_specs=[pl.BlockSpec((None, tr, ccols), lambda q, i, c_ref: (2 * q + c_ref[0], i, 0)), spec],
            out_specs=spec),
        out_shape=jax.ShapeDtypeStruct(theirs.shape, theirs.dtype),
        compiler_params=_params("parallel", "parallel"),
    )(my_c, mine, theirs)


def _sum_blocks(parts, name):
    nb, r, ccols = parts.shape
    tr = _row_tile(r, ccols, nb)

    def body(p_ref, o_ref):
        acc = p_ref[0].astype(F32)
        for i in range(1, nb):
            acc = acc + p_ref[i].astype(F32)
        o_ref[...] = acc

    return pl.pallas_call(
        body, name=name, grid=(r // tr,),
        in_specs=[pl.BlockSpec((nb, tr, ccols), lambda i: (0, i, 0))],
        out_specs=pl.BlockSpec((tr, ccols), lambda i: (i, 0)),
        out_shape=jax.ShapeDtypeStruct((r, ccols), F32),
        compiler_params=_params("parallel"),
    )(parts)


def _gather_wire(shards, wire_dtypes):
    return tuple(_allgather([s.astype(d) for s, d in zip(shards, wire_dtypes)], "weights_allgather"))


@functools.partial(jax.custom_vjp, nondiff_argnums=(1,))
def fsdp_gather(shards, wire_dtypes):
    return _gather_wire(shards, wire_dtypes)


def _fsdp_gather_fwd(shards, wire_dtypes):
    return _gather_wire(shards, wire_dtypes), None


def _fsdp_gather_bwd(wire_dtypes, _, cts):
    chip_parts = _exchange_chips(_reduce_scatter_head(cts, "grads"), "grads_exchange_chips")
    return (_reduce_scatter_tail(chip_parts, "grads"),)


fsdp_gather.defvjp(_fsdp_gather_fwd, _fsdp_gather_bwd)


@jax.custom_vjp
def replicated(vec):
    return vec


def _replicated_fwd(vec):
    return vec, None


def _replicated_bwd(_, ct):
    return (_sum_blocks(_allgather([ct], "small_grad_allgather")[0], "small_grad_sum"),)


replicated.defvjp(_replicated_fwd, _replicated_bwd)


def _adamw(w, g, m, v, name):
    rows, cols = w.shape
    tr = _pick(rows, 256, 8) if rows % 8 == 0 else rows

    def body(w_ref, g_ref, m_ref, v_ref, d_ref, nm_ref, nv_ref):
        g_ = g_ref[...]
        m_ = ADAM_B1 * m_ref[...] + (1.0 - ADAM_B1) * g_
        v_ = ADAM_B2 * v_ref[...] + (1.0 - ADAM_B2) * jnp.square(g_)
        m_hat = m_ / (1.0 - ADAM_B1 ** ADAM_STEP)
        v_hat = v_ / (1.0 - ADAM_B2 ** ADAM_STEP)
        d_ref[...] = -ADAM_LR * (m_hat / (jnp.sqrt(v_hat) + ADAM_EPS) + ADAM_WD * w_ref[...])
        nm_ref[...] = m_
        nv_ref[...] = v_

    spec = pl.BlockSpec((tr, cols), lambda i: (i, 0))
    return pl.pallas_call(
        body, name=name, grid=(rows // tr,), in_specs=[spec] * 4, out_specs=[spec] * 3,
        out_shape=[jax.ShapeDtypeStruct(w.shape, F32)] * 3, compiler_params=_params("parallel"),
    )(w, g, m, v)


COL_SHARDED = ("w_in", "w_uq", "w_ukv", "w_branch_a", "w_branch_b", "w_up", "w_ple")
EARLY = ("w_in", "w_uq", "w_ukv", "w_branch_a", "w_branch_b", "w_out")
LATE = ("w_up", "w_down", "w_ple_gate", "w_ple")
SMALL = ("attn_pre_norm", "attn_post_norm", "b_gate", "q_a_norm", "kv_a_norm", "mlp_pre_norm", "mlp_post_norm",
         "conv_b", "ple_norm", "sinks")
SMALL_COLS = 128


def _pack_rows(arrays, cols, row_mult):
    flat = jnp.concatenate([a.reshape(-1) for a in arrays])
    pad = (-flat.shape[0]) % (cols * row_mult)
    return jnp.pad(flat, (0, pad)).reshape(-1, cols)


def _unpack_small(vec, shapes):
    flat = vec.reshape(-1)
    out, off = {}, 0
    for name in SMALL:
        n = shapes[name]
        out[name] = flat[off:off + n].reshape(1, n)
        off += n + (-n) % SMALL_COLS
    return out


def _pad_lanes(t, width):
    return jnp.pad(t, [(0, 0)] * (t.ndim - 1) + [(0, width - t.shape[-1])])


def _pad_rows(t, rows):
    return jnp.pad(t, [(0, 0)] * (t.ndim - 2) + [(0, rows - t.shape[-2]), (0, 0)])


FRONT_SIZES = (512, 128, 128, 256, 128)
FRONT_BOUNDS = (0, 512, 640, 768, 1024, 1152, 1280)
PE_LANE = NOPE_DIM


def _arrange_w_in_t(wt):
    k = wt.shape[1]
    n_front = sum(FRONT_SIZES)
    front, kr, gates = wt[:n_front], wt[n_front:n_front + ROPE_DIM], wt[n_front + ROPE_DIM:]
    kr_slab = jnp.concatenate([jnp.zeros((PE_LANE, k), wt.dtype), kr,
                               jnp.zeros((HEAD_PAD - PE_LANE - ROPE_DIM, k), wt.dtype)], axis=0)
    return jnp.concatenate([front, kr_slab], axis=0), gates


def _arrange_w_uq_t(wt):
    k = wt.shape[1]
    return _pad_rows(wt.reshape(B_HEADS, NOPE_DIM + ROPE_DIM, k), HEAD_PAD).reshape(B_HEADS * HEAD_PAD, k)


def _arrange_w_ukv_t(wt):
    k = wt.shape[1]
    w = wt.reshape(B_HEADS, 2, NOPE_DIM, k)
    slabs = [_pad_rows(w[:, part], HEAD_PAD).reshape(B_HEADS * HEAD_PAD, k) for part in range(2)]
    return jnp.concatenate(slabs, axis=0)


def _rope_tables(positions, s):
    pos = positions.reshape(s, 1).astype(F32)

    def angles(dim):
        return pos * ROPE_THETA ** (-(jnp.arange(0, dim, 2, dtype=F32) / dim))

    cos_a, sin_a = jnp.cos(angles(A_HEAD_DIM)), jnp.sin(angles(A_HEAD_DIM))
    zero_a = jnp.zeros_like(sin_a)
    tables_a = [jnp.tile(jnp.concatenate(pair, axis=1), (1, LANES // A_HEAD_DIM))
                for pair in ((cos_a, cos_a), (-sin_a, zero_a), (zero_a, sin_a))]
    cos_b, sin_b = jnp.cos(angles(ROPE_DIM)), jnp.sin(angles(ROPE_DIM))
    zero_b = jnp.zeros_like(sin_b)

    def slab(first, second, fill):
        return jnp.concatenate([jnp.full((s, PE_LANE), fill, F32), first, second,
                                jnp.full((s, HEAD_PAD - PE_LANE - ROPE_DIM), fill, F32)], axis=1)

    tables_b = [slab(cos_b, cos_b, 1.0), slab(-sin_b, zero_b, 0.0), slab(zero_b, sin_b, 0.0)]
    return tables_a + tables_b


def _local_loss(wts, x, p, tables, target):
    s = x.shape[0]
    small_shapes = {n: wts[n].shape[-1] for n in SMALL}
    small_vec = _pack_rows([_pad_lanes(wts[n].reshape(1, -1), small_shapes[n] + (-small_shapes[n]) % SMALL_COLS)
                            for n in SMALL], SMALL_COLS, 8)
    sm = _unpack_small(replicated(small_vec), small_shapes)
    def shard(n):
        return wts[n].T if n in COL_SHARDED else wts[n]

    gathered = fsdp_gather(tuple([shard(n) for n in EARLY] + [_pack_rows([wts["conv_w"]], SMALL_COLS, 8)]),
                           (BF16,) * len(EARLY) + (F32,))
    big = {n: g.reshape(-1, g.shape[2]) for n, g in zip(EARLY, gathered)}
    ch = wts["conv_w"].shape[1]
    conv_w = gathered[-1].reshape(N_DEV, -1)[:, :CONV_W * ch].reshape(N_DEV, CONV_W, ch)
    conv_w = conv_w.transpose(1, 0, 2).reshape(CONV_W, N_DEV * ch)

    w_front_t, w_gates_t = _arrange_w_in_t(big["w_in"])
    tables_a, tables_b = tables[:3], tables[3:]

    (h1,) = stage("prenorm", _f_prenorm, [x], [sm["attn_pre_norm"]], out_dtypes=[BF16])
    qa, ka, va, cqn, ckvn, kpe = proj_stage(
        "prep", _f_prep, [(h1, w_front_t, "nt", "w_front", True)], params=[sm["q_a_norm"], sm["kv_a_norm"]],
        consts=tables, splits=[FRONT_BOUNDS], out_dtypes=[BF16, BF16, BF16, BF16, BF16, F32])
    ya = swa_nat(qa, ka, va, sm["sinks"].reshape(-1))

    (q2,) = proj_stage("qrope", _f_qrope, [(cqn, _arrange_w_uq_t(big["w_uq"]), "nt", "w_uq", True)],
                       consts=tables_b, out_dtypes=[BF16])
    k2, v2 = proj_stage("kv", _f_kv, [(ckvn, _arrange_w_ukv_t(big["w_ukv"]), "nt", "w_ukv", True)], extra=[kpe],
                        splits=[(0, B_HEADS * HEAD_PAD, 2 * B_HEADS * HEAD_PAD), None], out_dtypes=[BF16, BF16])
    yb, late = flash_nat(q2, k2, v2, tuple(shard(n) for n in LATE))
    big.update({n: g.reshape(-1, g.shape[2]) for n, g in zip(LATE, late)})

    (mixed,) = proj_stage(
        "gate", _f_gate, [(h1, w_gates_t, "nt", "w_gates", True), (ya, big["w_branch_a"], "nt", "w_branch_a", True),
                          (yb, big["w_branch_b"], "nt", "w_branch_b", True)],
        params=[sm["b_gate"][:, :D_MODEL], sm["b_gate"][:, D_MODEL:]],
        splits=[(0, D_MODEL, 2 * D_MODEL), None, None], out_dtypes=[BF16])
    x1, h2 = proj_stage("post_attn", _f_post, [(mixed, big["w_out"], "nn", "w_out", True)], extra=[x],
                        params=[sm["attn_post_norm"], sm["mlp_pre_norm"]], out_dtypes=[F32, BF16])

    act = mlp_up(h2, big["w_up"], conv_w, sm["conv_b"])
    x2, h3 = proj_stage("post_mlp", _f_post, [(act, big["w_down"], "nn", "w_down", True)], extra=[x1],
                        params=[sm["mlp_post_norm"], sm["ple_norm"]], out_dtypes=[F32, BF16])

    (rowloss,) = proj_stage("loss", _f_out, [(h3, big["w_ple_gate"], "nn", "w_ple_gate", True),
                                             (p, big["w_ple"], "nt", "w_ple", False)], extra=[x2], consts=[target])
    return jnp.sum(rowloss)


WEIGHTS = ["attn_pre_norm", "attn_post_norm", "w_in", "b_gate", "sinks", "q_a_norm", "w_uq", "kv_a_norm", "w_ukv",
           "w_branch_a", "w_branch_b", "w_out", "mlp_pre_norm", "mlp_post_norm", "w_up", "conv_w", "conv_b",
           "w_down", "ple_norm", "w_ple_gate", "w_ple"]


def kernel(x, p, positions, attn_pre_norm, attn_post_norm, w_in, b_gate, sinks, q_a_norm, w_uq, kv_a_norm, w_ukv, w_branch_a, w_branch_b, w_out, mlp_pre_norm, mlp_post_norm, w_up, conv_w, conv_b, w_down, ple_norm, w_ple_gate, w_ple, loss_target, m_attn_pre_norm, m_attn_post_norm, m_w_in, m_b_gate, m_sinks, m_q_a_norm, m_w_uq, m_kv_a_norm, m_w_ukv, m_w_branch_a, m_w_branch_b, m_w_out, m_mlp_pre_norm, m_mlp_post_norm, m_w_up, m_conv_w, m_conv_b, m_w_down, m_ple_norm, m_w_ple_gate, m_w_ple, v_attn_pre_norm, v_attn_post_norm, v_w_in, v_b_gate, v_sinks, v_q_a_norm, v_w_uq, v_kv_a_norm, v_w_ukv, v_w_branch_a, v_w_branch_b, v_w_out, v_mlp_pre_norm, v_mlp_post_norm, v_w_up, v_conv_w, v_conv_b, v_w_down, v_ple_norm, v_w_ple_gate, v_w_ple):
    given = dict(locals())
    s = x.shape[1]
    wts = {n: given[n][0] if given[n].ndim == 3 else given[n] for n in WEIGHTS}
    tables = _rope_tables(positions, s)
    local_loss, (grads, grad_x) = jax.value_and_grad(_local_loss, argnums=(0, 1))(
        wts, x[0], p[0, 0], tables, loss_target[0])
    loss = lax.psum(local_loss, AXES)

    outs = {"grad": [], "delta": [], "m": [], "v": []}
    for n in WEIGHTS:
        shape = given[n].shape
        w2 = wts[n].reshape(-1, shape[-1])
        g2 = grads[n].reshape(w2.shape)
        delta, new_m, new_v = _adamw(w2, g2, given["m_" + n].reshape(w2.shape), given["v_" + n].reshape(w2.shape),
                                     "adamw_" + n)
        outs["grad"].append(g2.reshape(shape))
        outs["delta"].append(delta.reshape(shape))
        outs["m"].append(new_m.reshape(shape))
        outs["v"].append(new_v.reshape(shape))
    return (loss, grad_x[None], *outs["grad"], *outs["delta"], *outs["m"], *outs["v"])
```

```python
import functools

import numpy as np
import jax
import jax.numpy as jnp
from jax import lax
from jax.experimental import pallas as pl
from jax.experimental.pallas import tpu as pltpu

F32 = jnp.float32
BF16 = jnp.bfloat16
MESH_ID = pl.DeviceIdType.MESH
AXES = ("x", "y", "c")
N_DEV = 8

D_MODEL = 1024
RMS_EPS = 1e-6
ROPE_THETA = 10000.0
SWA_BLOCK = 128
A_HEADS, A_KV_HEADS, A_HEAD_DIM = 8, 2, 64
A_GROUP = A_HEADS // A_KV_HEADS
B_HEADS, Q_LORA, KV_LORA, NOPE_DIM, ROPE_DIM, V_DIM = 8, 256, 128, 64, 32, 64
D_FF = 2816
CONV_W = 3
HEAD_PAD = 128

ADAM_LR, ADAM_B1, ADAM_B2, ADAM_EPS, ADAM_WD, ADAM_STEP = 0.001, 0.9, 0.999, 1e-08, 0.01, 10

VMEM_LIMIT = 48 * 1024 * 1024
MM_TM, MM_TN, MM_TK_TOKENS = 512, 1408, 1024
MM_VMEM_BUDGET = 36 * 1024 * 1024
FLASH_T = 1024
CONV_TS = 128
CONV_CHUNK = 256


def _params(*sem):
    return pltpu.CompilerParams(dimension_semantics=sem, vmem_limit_bytes=VMEM_LIMIT)


def _pick(dim, cap, mult):
    best = None
    for t in range(mult, min(dim, cap) + 1, mult):
        if dim % t == 0:
            best = t
    return dim if best is None else best


def _divisors(dim, mult):
    return [t for t in range(mult, dim + 1, mult) if dim % t == 0] or [dim]


def _matmul_tiles(m, n, kdim, form, sizes):
    sa, sb, so = sizes
    tk = _pick(kdim, MM_TK_TOKENS, 128) if form == "tn" else kdim
    cap_m = MM_TN if form == "tn" else MM_TM
    best = None
    for tm in _divisors(m, 128):
        for tn in _divisors(n, 128):
            need = 2 * (tm * tk * sa + tk * tn * sb + tm * tn * so) + (tm * tn * 4 if tk != kdim else 0)
            if tm > cap_m or tn > MM_TN or need > MM_VMEM_BUDGET:
                continue
            if best is None or (tm * tn, tm) > (best[0] * best[1], best[0]):
                best = (tm, tn)
    return best[0], best[1], tk


def _matmul(a, b, form, *, out_dtype=F32, name):
    if form == "tn":
        (kdim, m), n = a.shape, b.shape[1]
    else:
        (m, kdim), n = a.shape, (b.shape[1] if form == "nn" else b.shape[0])
    sizes = (a.dtype.itemsize, b.dtype.itemsize, jnp.dtype(out_dtype).itemsize)
    tm, tn, tk = _matmul_tiles(m, n, kdim, form, sizes)
    nk = kdim // tk
    rows_outer = nk > 1 or (m // tm) * b.size * sizes[1] <= (n // tn) * a.size * sizes[0]

    def ij(fn):
        return (lambda i, j, k: fn(i, j, k)) if rows_outer else (lambda j, i, k: fn(i, j, k))

    a_spec = (pl.BlockSpec((tk, tm), ij(lambda i, j, k: (k, i))) if form == "tn"
              else pl.BlockSpec((tm, tk), ij(lambda i, j, k: (i, k))))
    b_spec = (pl.BlockSpec((tn, tk), ij(lambda i, j, k: (j, k))) if form == "nt"
              else pl.BlockSpec((tk, tn), ij(lambda i, j, k: (k, j))))
    dims = (((0 if form == "tn" else 1,), (1 if form == "nt" else 0,)), ((), ()))

    def product(a_ref, b_ref):
        return lax.dot_general(a_ref[...].astype(BF16), b_ref[...].astype(BF16), dims, preferred_element_type=F32)

    if nk == 1:
        def body(a_ref, b_ref, o_ref):
            o_ref[...] = product(a_ref, b_ref).astype(o_ref.dtype)

        scratch = []
    else:
        def body(a_ref, b_ref, o_ref, acc_ref):
            k = pl.program_id(2)

            @pl.when(k == 0)
            def _():
                acc_ref[...] = jnp.zeros_like(acc_ref)

            acc_ref[...] += product(a_ref, b_ref)

            @pl.when(k == nk - 1)
            def _():
                o_ref[...] = acc_ref[...].astype(o_ref.dtype)

        scratch = [pltpu.VMEM((tm, tn), F32)]

    return pl.pallas_call(
        body, name=name, grid=(m // tm, n // tn, nk) if rows_outer else (n // tn, m // tm, nk),
        in_specs=[a_spec, b_spec],
        out_specs=pl.BlockSpec((tm, tn), ij(lambda i, j, k: (i, j))),
        out_shape=jax.ShapeDtypeStruct((m, n), out_dtype),
        scratch_shapes=scratch,
        compiler_params=_params("parallel", "parallel", "arbitrary"),
    )(a, b)


def _pairs(bounds):
    return list(zip(bounds[:-1], bounds[1:]))


def _split(v, bounds):
    return [v[:, a:b] for a, b in _pairs(bounds)]


def _stage_build(name, f, tiled, params, consts, splits, ts, out_dtypes, ct_dtypes=None):
    n_t, n_p, n_c = len(tiled), len(params), len(consts)
    ct_dtypes = [t.dtype for t in tiled] if ct_dtypes is None else ct_dtypes
    s = tiled[0].shape[0]
    ts = min(ts, s)
    grid = (s // ts,)
    if splits is None:
        splits = [None] * n_t
    in_bounds = [(0, t.shape[1]) if b is None else tuple(b) for t, b in zip(tiled, splits)]

    def tile_aval(arr):
        return jax.ShapeDtypeStruct((ts, arr.shape[1]), arr.dtype)

    slab_avals = [[jax.ShapeDtypeStruct((ts, e - a), t.dtype) for a, e in _pairs(b)]
                  for t, b in zip(tiled, in_bounds)]
    out_avals = jax.eval_shape(f, slab_avals, list(params), [tile_aval(c) for c in consts])
    out_bounds = [tuple(np.cumsum([0] + [o.shape[1] for o in slabs]).tolist()) for slabs in out_avals]
    out_dtypes = [F32] * len(out_bounds) if out_dtypes is None else out_dtypes
    out_shapes = [jax.ShapeDtypeStruct((s, b[-1]), d) for b, d in zip(out_bounds, out_dtypes)]

    def row_spec(width):
        return pl.BlockSpec((ts, width), lambda i: (i, 0))

    def par_spec(arr):
        return pl.BlockSpec(arr.shape, lambda i: (0, 0))

    in_specs = ([row_spec(t.shape[1]) for t in tiled] + [par_spec(p) for p in params]
                + [row_spec(c.shape[1]) for c in consts])

    def load(refs):
        t = [_split(r[...], b) for r, b in zip(refs[:n_t], in_bounds)]
        p = [r[...] for r in refs[n_t:n_t + n_p]]
        c = [r[...] for r in refs[n_t + n_p:n_t + n_p + n_c]]
        return t, p, c

    def store(refs, values, bounds):
        for ref, slabs, b in zip(refs, values, bounds):
            for v, (a, e) in zip(slabs, _pairs(b)):
                ref[:, a:e] = v.astype(ref.dtype)

    def run_fwd(tiled, params, consts):
        def body(*refs):
            t, p, c = load(refs)
            store(refs[n_t + n_p + n_c:], f(t, p, c), out_bounds)

        return pl.pallas_call(
            body, name=name + "_fwd", grid=grid, in_specs=in_specs,
            out_specs=[row_spec(b[-1]) for b in out_bounds], out_shape=out_shapes,
            compiler_params=_params("parallel"),
        )(*tiled, *params, *consts)

    def run_bwd(tiled, params, consts, cts):
        n_in = n_t + n_p + n_c
        n_o = len(out_bounds)

        def body(*refs):
            t, p, c = load(refs)
            g = [_split(r[...].astype(F32), b) for r, b in zip(refs[n_in:n_in + n_o], out_bounds)]
            _, pull = jax.vjp(lambda t_, p_: f(t_, p_, c), t, p)
            dt, dp = pull(g)
            store(refs[n_in + n_o:n_in + n_o + n_t], dt, in_bounds)
            first = pl.program_id(0) == 0
            for ref, d in zip(refs[n_in + n_o + n_t:], dp):
                @pl.when(first)
                def _(ref=ref):
                    ref[...] = jnp.zeros_like(ref)

                ref[...] += d

        res = pl.pallas_call(
            body, name=name + "_bwd", grid=grid,
            in_specs=in_specs + [row_spec(b[-1]) for b in out_bounds],
            out_specs=[row_spec(t.shape[1]) for t in tiled] + [par_spec(p) for p in params],
            out_shape=[jax.ShapeDtypeStruct(t.shape, d) for t, d in zip(tiled, ct_dtypes)]
                      + [jax.ShapeDtypeStruct(p.shape, F32) for p in params],
            compiler_params=_params("arbitrary"),
        )(*tiled, *params, *consts, *cts)
        return tuple(res[:n_t]), tuple(res[n_t:])

    return run_fwd, run_bwd


def stage(name, f, tiled, params=(), consts=(), splits=None, ts=256, out_dtypes=None):
    tiled, params, consts = tuple(tiled), tuple(params), tuple(consts)
    run_fwd, run_bwd = _stage_build(name, f, tiled, params, consts, splits, ts, out_dtypes)

    @jax.custom_vjp
    def op(tiled, params, consts):
        return tuple(run_fwd(tiled, params, consts))

    def op_fwd(tiled, params, consts):
        return op(tiled, params, consts), (tiled, params, consts)

    def op_bwd(res, cts):
        tiled, params, consts = res
        dt, dp = run_bwd(tiled, params, consts, cts)
        return dt, dp, tuple(jnp.zeros_like(c) for c in consts)

    op.defvjp(op_fwd, op_bwd)
    return op(tiled, params, consts)


def proj_stage(name, f, projections, extra=(), params=(), consts=(), splits=None, ts=256, out_dtypes=None):
    n_z = len(projections)
    forms = [pr[2] for pr in projections]
    names = [pr[3] for pr in projections]
    need_da = [pr[4] for pr in projections]
    extra, params, consts = tuple(extra), tuple(params), tuple(consts)

    def matmuls(a_list, w_list):
        return tuple(_matmul(a, w, form, out_dtype=F32, name=n + "_fwd")
                     for a, w, form, n in zip(a_list, w_list, forms, names))

    def build(zs, ct=False):
        ct_dtypes = [BF16] * n_z + [e.dtype for e in extra] if ct else None
        return _stage_build(name, f, tuple(zs) + extra, params, consts, splits, ts, out_dtypes, ct_dtypes)

    @jax.custom_vjp
    def op(a_list, w_list, extra, params, consts):
        zs = matmuls(a_list, w_list)
        return tuple(build(zs)[0](zs + extra, params, consts))

    def op_fwd(a_list, w_list, extra, params, consts):
        zs = matmuls(a_list, w_list)
        return tuple(build(zs)[0](zs + extra, params, consts)), (a_list, w_list, zs, extra, params, consts)

    def op_bwd(res, cts):
        a_list, w_list, zs, extra, params, consts = res
        dt, dp = build(zs, ct=True)[1](zs + extra, params, consts, cts)
        da_list, dw_list = [], []
        for a, w, dz, form, n, want in zip(a_list, w_list, dt[:n_z], forms, names, need_da):
            if form == "nn":
                da = _matmul(dz, w, "nt", out_dtype=a.dtype, name=n + "_da") if want else jnp.zeros_like(a)
                dw = _matmul(a, dz, "tn", out_dtype=w.dtype, name=n + "_dw")
            else:
                da = _matmul(dz, w, "nn", out_dtype=a.dtype, name=n + "_da") if want else jnp.zeros_like(a)
                dw = _matmul(dz, a, "tn", out_dtype=w.dtype, name=n + "_dw")
            da_list.append(da)
            dw_list.append(dw)
        return tuple(da_list), tuple(dw_list), tuple(dt[n_z:]), dp, tuple(jnp.zeros_like(c) for c in consts)

    op.defvjp(op_fwd, op_bwd)
    return op(tuple(pr[0] for pr in projections), tuple(pr[1] for pr in projections), extra, params, consts)


def _rms(t, g):
    return t * lax.rsqrt(jnp.mean(t * t, axis=-1, keepdims=True) + RMS_EPS) * g


@functools.partial(jax.custom_vjp, nondiff_argnums=(1,))
def _lane_roll(t, shift):
    return pltpu.roll(t, shift % t.shape[-1], t.ndim - 1)


def _lane_roll_fwd(t, shift):
    return _lane_roll(t, shift), None


def _lane_roll_bwd(shift, _, ct):
    return (pltpu.roll(ct, (-shift) % ct.shape[-1], ct.ndim - 1),)


_lane_roll.defvjp(_lane_roll_fwd, _lane_roll_bwd)


def _rope_lanes(t, tables, half):
    reps = t.shape[1] // tables[0].shape[1]
    c, s_lo, s_hi = [jnp.concatenate([tb] * reps, axis=1) if reps > 1 else tb for tb in tables]
    return t * c + _lane_roll(t, -half) * s_lo + _lane_roll(t, half) * s_hi


def _f_prenorm(t, p, c):
    return [[_rms(t[0][0], p[0])]]


def _f_prep(t, p, c):
    qa, ka, va, cq, ckv, kr = t[0]
    return [[_rope_lanes(qa, c[0:3], A_HEAD_DIM // 2)], [_rope_lanes(ka, c[0:3], A_HEAD_DIM // 2)], [va],
            [_rms(cq, p[0])], [_rms(ckv, p[1])], [_rope_lanes(kr, c[3:6], ROPE_DIM // 2)]]


def _f_qrope(t, p, c):
    return [[_rope_lanes(t[0][0], c, ROPE_DIM // 2)]]


def _f_kv(t, p, c):
    (k_nope, v), (k_pe,) = t
    return [[k_nope + jnp.concatenate([k_pe] * B_HEADS, axis=1)], [v]]


def _f_gate(t, p, c):
    (ga, gb), (pa,), (pb,) = t
    ba, bb = p
    return [[jax.nn.sigmoid(ga + ba) * pa + jax.nn.sigmoid(gb + bb) * pb]]


def _f_post(t, p, c):
    (branch,), (residual,) = t
    x1 = residual + _rms(branch, p[0])
    return [[x1], [_rms(x1, p[1])]]


def _f_out(t, p, c):
    (gate,), (emb,), (x2,) = t
    y = x2 + jax.nn.sigmoid(gate) * emb
    err = y - c[0]
    return [[0.5 * jnp.mean(err * err, axis=-1, keepdims=True)]]


def _shift_down(cur, prev, has_prev):
    full = jnp.concatenate([prev * has_prev, cur], axis=0)
    return pltpu.roll(full, 1, 0)[HALO:], pltpu.roll(full, 2, 0)[HALO:]


GELU_C = float(np.sqrt(2.0 / np.pi))
GELU_A = 0.044715
HALO = 8


def _gelu_tanh(x):
    x2 = x * x
    th = jnp.tanh(x * (GELU_C + (GELU_C * GELU_A) * x2))
    half = 0.5 + 0.5 * th
    return x * half, half + x * (0.5 - 0.5 * (th * th)) * (GELU_C + (3.0 * GELU_C * GELU_A) * x2)


def _row_sum(t):
    return jnp.sum(t, axis=0, keepdims=True)


def _conv3(cur, prev, w_ref, b_ref, has_prev):
    u1, u2 = _shift_down(cur, prev, has_prev)
    return w_ref[2:3, :] * cur + w_ref[1:2, :] * u1 + w_ref[0:1, :] * u2 + b_ref[...], u1, u2


def _mlp_act_specs(s):
    ts = min(CONV_TS, s)
    hb = ts // HALO

    def half_specs(h):
        return [pl.BlockSpec((ts, D_FF), lambda i: (i, h)),
                pl.BlockSpec((HALO, D_FF), lambda i: (jnp.maximum(i * hb - 1, 0), h))]

    def par_specs(h):
        return [pl.BlockSpec((CONV_W, D_FF), lambda i: (0, h)), pl.BlockSpec((1, D_FF), lambda i: (0, h))]

    return ts, hb, half_specs, par_specs


def _mlp_act_fwd_call(up, conv_w, conv_b):
    s = up.shape[0]
    ts, hb, half_specs, par_specs = _mlp_act_specs(s)

    def body(g_ref, gp_ref, v_ref, vp_ref, wg_ref, bg_ref, wv_ref, bv_ref, o_ref):
        has_prev = (pl.program_id(0) > 0).astype(F32)

        def chunk(cidx, carry):
            cols = pl.ds(pl.multiple_of(cidx * CONV_CHUNK, CONV_CHUNK), CONV_CHUNK)
            u_g, _, _ = _conv3(g_ref[:, cols], gp_ref[:, cols], wg_ref.at[:, cols], bg_ref.at[:, cols], has_prev)
            u_v, _, _ = _conv3(v_ref[:, cols], vp_ref[:, cols], wv_ref.at[:, cols], bv_ref.at[:, cols], has_prev)
            o_ref[:, cols] = (_gelu_tanh(u_g)[0] * u_v).astype(o_ref.dtype)
            return carry

        lax.fori_loop(0, D_FF // CONV_CHUNK, chunk, 0)

    return pl.pallas_call(
        body, name="mlp_act_fwd", grid=(s // ts,),
        in_specs=half_specs(0) + half_specs(1) + par_specs(0) + par_specs(1),
        out_specs=pl.BlockSpec((ts, D_FF), lambda i: (i, 0)),
        out_shape=jax.ShapeDtypeStruct((s, D_FF), BF16),
        compiler_params=_params("parallel"),
    )(up, up, up, up, conv_w, conv_b, conv_w, conv_b)


def _mlp_act_bwd_call(up, conv_w, conv_b, dact):
    s = up.shape[0]
    ts, hb, half_specs, par_specs = _mlp_act_specs(s)
    nt = s // ts
    ext = ts + HALO
    bf16_rows = 2 * HALO

    def next_spec(rows, h):
        return pl.BlockSpec((rows, D_FF), lambda i: (jnp.minimum((i + 1) * (ts // rows), s // rows - 1), h))

    def body(g_ref, gp_ref, gn_ref, v_ref, vp_ref, vn_ref, wg_ref, bg_ref, wv_ref, bv_ref, da_ref, dan_ref,
             dup_ref, dwg_ref, dbg_ref, dwv_ref, dbv_ref):
        i = pl.program_id(0)
        has_prev, has_next = (i > 0).astype(F32), (i < nt - 1).astype(F32)

        @pl.when(i == 0)
        def _():
            for ref in (dwg_ref, dbg_ref, dwv_ref, dbv_ref):
                ref[...] = jnp.zeros_like(ref)

        def chunk(cidx, carry):
            cols = pl.ds(pl.multiple_of(cidx * CONV_CHUNK, CONV_CHUNK), CONV_CHUNK)
            g_ext = jnp.concatenate([g_ref[:, cols], gn_ref[:, cols]], axis=0)
            v_ext = jnp.concatenate([v_ref[:, cols], vn_ref[:, cols]], axis=0)
            u_g, g1, g2 = _conv3(g_ext, gp_ref[:, cols], wg_ref.at[:, cols], bg_ref.at[:, cols], has_prev)
            u_v, v1, v2 = _conv3(v_ext, vp_ref[:, cols], wv_ref.at[:, cols], bv_ref.at[:, cols], has_prev)
            da_ext = jnp.concatenate([da_ref[:, cols].astype(F32),
                                      dan_ref[:, cols].astype(F32)[0:HALO] * has_next], axis=0)
            act_g, dact_g = _gelu_tanh(u_g)
            du_g = da_ext * u_v * dact_g
            du_v = da_ext * act_g
            for du, w_ref, x0, x1, x2, dw_ref, db_ref, lo in ((du_g, wg_ref, g_ext, g1, g2, dwg_ref, dbg_ref, 0),
                                                          (du_v, wv_ref, v_ext, v1, v2, dwv_ref, dbv_ref, D_FF)):
                d1 = pltpu.roll(du, ext - 1, 0)
                d2 = pltpu.roll(du, ext - 2, 0)
                dup = w_ref[2:3, cols] * du + w_ref[1:2, cols] * d1 + w_ref[0:1, cols] * d2
                out_cols = pl.ds(pl.multiple_of(lo + cidx * CONV_CHUNK, CONV_CHUNK), CONV_CHUNK)
                dup_ref[:, out_cols] = dup[0:ts].astype(dup_ref.dtype)
                own = du[0:ts]
                dw_ref[0:1, cols] += _row_sum(own * x2[0:ts])
                dw_ref[1:2, cols] += _row_sum(own * x1[0:ts])
                dw_ref[2:3, cols] += _row_sum(own * x0[0:ts])
                db_ref[:, cols] += _row_sum(own)
            return carry

        lax.fori_loop(0, D_FF // CONV_CHUNK, chunk, 0)

    par_out = [pl.BlockSpec((CONV_W, D_FF), lambda i: (0, 0)), pl.BlockSpec((1, D_FF), lambda i: (0, 0))]
    par_shapes = [jax.ShapeDtypeStruct((CONV_W, D_FF), F32), jax.ShapeDtypeStruct((1, D_FF), F32)]
    return pl.pallas_call(
        body, name="mlp_act_bwd", grid=(nt,),
        in_specs=(half_specs(0) + [next_spec(HALO, 0)] + half_specs(1) + [next_spec(HALO, 1)]
                  + par_specs(0) + par_specs(1)
                  + [pl.BlockSpec((ts, D_FF), lambda i: (i, 0)), next_spec(bf16_rows, 0)]),
        out_specs=[pl.BlockSpec((ts, 2 * D_FF), lambda i: (i, 0))] + par_out + par_out,
        out_shape=[jax.ShapeDtypeStruct((s, 2 * D_FF), BF16)] + par_shapes + par_shapes,
        compiler_params=_params("arbitrary"),
    )(up, up, up, up, up, up, conv_w, conv_b, conv_w, conv_b, dact, dact)


@jax.custom_vjp
def mlp_up(h2, w_up_t, conv_w, conv_b):
    return _mlp_act_fwd_call(_matmul(h2, w_up_t, "nt", out_dtype=F32, name="w_up_fwd"), conv_w, conv_b)


def _mlp_up_fwd(h2, w_up_t, conv_w, conv_b):
    up = _matmul(h2, w_up_t, "nt", out_dtype=F32, name="w_up_fwd")
    return _mlp_act_fwd_call(up, conv_w, conv_b), (h2, w_up_t, up, conv_w, conv_b)


def _mlp_up_bwd(res, dact):
    h2, w_up_t, up, conv_w, conv_b = res
    dup, dwg, dbg, dwv, dbv = _mlp_act_bwd_call(up, conv_w, conv_b, dact)
    dh2 = _matmul(dup, w_up_t, "nn", out_dtype=h2.dtype, name="w_up_da")
    dw = _matmul(dup, h2, "tn", out_dtype=w_up_t.dtype, name="w_up_dw")
    return dh2, dw, jnp.concatenate([dwg, dwv], axis=1), jnp.concatenate([dbg, dbv], axis=1)


mlp_up.defvjp(_mlp_up_fwd, _mlp_up_bwd)


SWA_ROWS = A_GROUP * SWA_BLOCK


def _swa_sink_rows(sink_ref, g):
    return jnp.concatenate([jnp.full((SWA_BLOCK, 1), sink_ref[g * A_GROUP + h], F32) for h in range(A_GROUP)], axis=0)


def _swa_probs(q, kp, kc, sink, prev_off):
    scale = A_HEAD_DIM ** -0.5
    nt = (((1,), (1,)), ((), ()))
    sp = lax.dot_general(q, kp, nt, preferred_element_type=F32) * scale
    sc = lax.dot_general(q, kc, nt, preferred_element_type=F32) * scale
    qi = lax.broadcasted_iota(jnp.int32, sp.shape, 0) & (SWA_BLOCK - 1)
    kj = lax.broadcasted_iota(jnp.int32, sp.shape, 1)
    sp = jnp.where(kj > qi + prev_off, sp, -jnp.inf)
    sc = jnp.where(kj <= qi, sc, -jnp.inf)
    m = jnp.maximum(jnp.maximum(jnp.max(sp, axis=-1, keepdims=True), jnp.max(sc, axis=-1, keepdims=True)), sink)
    ep, ec, es = jnp.exp(sp - m), jnp.exp(sc - m), jnp.exp(sink - m)
    den = jnp.sum(ep, axis=-1, keepdims=True) + jnp.sum(ec, axis=-1, keepdims=True) + es
    return ep / den, ec / den, es / den


MLA_SCALE = (NOPE_DIM + ROPE_DIM) ** -0.5
EXP2_SCALE = MLA_SCALE * float(np.log2(np.e))
NT_DIMS = (((1,), (1,)), ((), ()))
TN_DIMS = (((0,), (0,)), ((), ()))


LANES = 128
HALF = LANES // 2


def _low_half(shape):
    return lax.broadcasted_iota(jnp.int32, shape, len(shape) - 1) < HALF


def _dup_half(x, g):
    xf = x.astype(F32)
    keep = _low_half(xf.shape) if g == 0 else jnp.logical_not(_low_half(xf.shape))
    xm = jnp.where(keep, xf, 0.0)
    return (xm + pltpu.roll(xm, HALF, 1)).astype(x.dtype)


def _fold_half(r, g):
    total = r + pltpu.roll(r, HALF, 1)
    keep = _low_half(r.shape) if g == 0 else jnp.logical_not(_low_half(r.shape))
    return jnp.where(keep, total, 0.0)


def _swa_stack_heads(ref, g):
    parts = []
    for tile in range(2):
        slab = ref[:, (2 * g + tile) * LANES:(2 * g + tile + 1) * LANES]
        low = _low_half(slab.shape)
        parts += [jnp.where(low, slab, jnp.zeros_like(slab)), jnp.where(low, jnp.zeros_like(slab), slab)]
    return jnp.concatenate(parts, axis=0)


def _swa_unstack_heads(ref, g, rows):
    for tile in range(2):
        a = rows[(2 * tile) * SWA_BLOCK:(2 * tile + 1) * SWA_BLOCK]
        b = rows[(2 * tile + 1) * SWA_BLOCK:(2 * tile + 2) * SWA_BLOCK]
        ref[:, (2 * g + tile) * LANES:(2 * g + tile + 1) * LANES] = jnp.where(_low_half(a.shape), a, b).astype(ref.dtype)


def _swa_nat_specs():
    blk = SWA_BLOCK
    q_spec = pl.BlockSpec((blk, A_HEADS * A_HEAD_DIM), lambda n: (n, 0))
    prev_spec = pl.BlockSpec((blk, LANES), lambda n: (jnp.maximum(n - 1, 0), 0))
    cur_spec = pl.BlockSpec((blk, LANES), lambda n: (n, 0))
    return q_spec, prev_spec, cur_spec, pl.BlockSpec(memory_space=pltpu.SMEM)


def _swa_nat_fwd_call(q, k, v, sinks, shards):
    s = q.shape[0]
    nblk = s // SWA_BLOCK
    n_arr = len(shards)
    q_spec, prev_spec, cur_spec, sink_spec = _swa_nat_specs()

    def body(*refs):
        q_ref, kp_ref, kc_ref, vp_ref, vc_ref, sink_ref = refs[:6]
        o_ref = refs[6 + n_arr]
        n = pl.program_id(0)
        ag_start, ag_forward, ag_finish = _allgather_phases(refs[6:6 + n_arr], refs[7 + n_arr:7 + 2 * n_arr],
                                                            *refs[7 + 2 * n_arr:])

        @pl.when(n == 0)
        def _():
            ag_start()

        @pl.when(n == nblk // 2)
        def _():
            ag_forward()

        prev_off = jnp.where(n > 0, 0, SWA_BLOCK)
        for g in range(A_KV_HEADS):
            kp, kc = _dup_half(kp_ref[...], g), _dup_half(kc_ref[...], g)
            vp, vc = _dup_half(vp_ref[...], g), _dup_half(vc_ref[...], g)
            pp, pc, _ = _swa_probs(_swa_stack_heads(q_ref, g), kp, kc, _swa_sink_rows(sink_ref, g), prev_off)
            out = (jnp.dot(pp.astype(BF16), vp, preferred_element_type=F32)
                   + jnp.dot(pc.astype(BF16), vc, preferred_element_type=F32))
            _swa_unstack_heads(o_ref, g, out)

        @pl.when(n == nblk - 1)
        def _():
            ag_finish()

    return pl.pallas_call(
        body, name="swa_fwd", grid=(nblk,),
        in_specs=[q_spec, prev_spec, cur_spec, prev_spec, cur_spec, sink_spec] + [HBM_SPEC] * n_arr,
        out_specs=[q_spec] + [HBM_SPEC] * n_arr,
        out_shape=[jax.ShapeDtypeStruct(q.shape, BF16)] + _allgather_out_shapes(shards),
        scratch_shapes=_allgather_sems(n_arr),
        compiler_params=_params("arbitrary"),
    )(q, k, k, v, v, sinks, *shards)


def _swa_nat_bwd_call(q, k, v, sinks, do, parts):
    s = q.shape[0]
    nblk = s // SWA_BLOCK
    n_arr = len(parts)
    q_spec, prev_spec, cur_spec, sink_spec = _swa_nat_specs()
    scale = A_HEAD_DIM ** -0.5
    dsink_spec = pl.BlockSpec((A_KV_HEADS, SWA_ROWS, 1), lambda n: (0, 0, 0))

    def body(*refs):
        q_ref, kp_ref, kc_ref, vp_ref, vc_ref, sink_ref, do_ref = refs[:7]
        dq_ref, dkp_ref, dkc_ref, dvp_ref, dvc_ref, dsink_ref = refs[7 + n_arr:13 + n_arr]
        n = pl.program_id(0)
        exchange_start, exchange_finish = _exchange_chips_phases(
            refs[7:7 + n_arr], refs[13 + n_arr:13 + 2 * n_arr], *refs[13 + 2 * n_arr:])

        @pl.when(n == 0)
        def _():
            exchange_start()
        prev_off = jnp.where(n > 0, 0, SWA_BLOCK)

        @pl.when(n == 0)
        def _():
            dsink_ref[...] = jnp.zeros_like(dsink_ref)

        totals = [jnp.zeros((SWA_BLOCK, LANES), F32) for _ in range(4)]
        for g in range(A_KV_HEADS):
            kp, kc = _dup_half(kp_ref[...], g), _dup_half(kc_ref[...], g)
            vp, vc = _dup_half(vp_ref[...], g), _dup_half(vc_ref[...], g)
            qb = _swa_stack_heads(q_ref, g)
            dob = _swa_stack_heads(do_ref, g)
            pp, pc, ps = _swa_probs(qb, kp, kc, _swa_sink_rows(sink_ref, g), prev_off)
            ppb, pcb = pp.astype(BF16), pc.astype(BF16)
            out = jnp.dot(ppb, vp, preferred_element_type=F32) + jnp.dot(pcb, vc, preferred_element_type=F32)
            delta = jnp.sum(dob.astype(F32) * out, axis=-1, keepdims=True)
            dsp = (pp * (lax.dot_general(dob, vp, NT_DIMS, preferred_element_type=F32) - delta)).astype(BF16)
            dsc = (pc * (lax.dot_general(dob, vc, NT_DIMS, preferred_element_type=F32) - delta)).astype(BF16)
            dsink_ref[g] += -ps * delta
            dq = (jnp.dot(dsp, kp, preferred_element_type=F32) + jnp.dot(dsc, kc, preferred_element_type=F32)) * scale
            _swa_unstack_heads(dq_ref, g, dq)
            pieces = [lax.dot_general(dsp, qb, TN_DIMS, preferred_element_type=F32) * scale,
                      lax.dot_general(dsc, qb, TN_DIMS, preferred_element_type=F32) * scale,
                      lax.dot_general(ppb, dob, TN_DIMS, preferred_element_type=F32),
                      lax.dot_general(pcb, dob, TN_DIMS, preferred_element_type=F32)]
            totals = [tot + _fold_half(r, g) for tot, r in zip(totals, pieces)]
        dkp_ref[...], dkc_ref[...], dvp_ref[...], dvc_ref[...] = totals

        @pl.when(n == nblk - 1)
        def _():
            exchange_finish()

    kv_shape = jax.ShapeDtypeStruct(k.shape, F32)
    return pl.pallas_call(
        body, name="swa_bwd", grid=(nblk,),
        in_specs=[q_spec, prev_spec, cur_spec, prev_spec, cur_spec, sink_spec, q_spec] + [HBM_SPEC] * n_arr,
        out_specs=[q_spec, cur_spec, cur_spec, cur_spec, cur_spec, dsink_spec] + [HBM_SPEC] * n_arr,
        out_shape=[jax.ShapeDtypeStruct(q.shape, q.dtype), kv_shape, kv_shape, kv_shape, kv_shape,
                   jax.ShapeDtypeStruct((A_KV_HEADS, SWA_ROWS, 1), F32)]
                  + [jax.ShapeDtypeStruct(p.shape, p.dtype) for p in parts],
        scratch_shapes=_exchange_chips_sems(n_arr),
        compiler_params=_params("arbitrary"),
    )(q, k, k, v, v, sinks, do, *parts)


@jax.custom_vjp
def swa_nat(q, k, v, sinks, shards):
    out = _swa_nat_fwd_call(q, k, v, sinks, [s.astype(BF16) for s in shards])
    return out[0], tuple(out[1:])


def _swa_nat_fwd(q, k, v, sinks, shards):
    out = _swa_nat_fwd_call(q, k, v, sinks, [s.astype(BF16) for s in shards])
    return (out[0], tuple(out[1:])), (q, k, v, sinks)


def _swa_nat_bwd(res, cts):
    q, k, v, sinks = res
    do, d_gathered = cts
    out = _swa_nat_bwd_call(q, k, v, sinks, do, _reduce_scatter_head(d_gathered, "mid_grads"))
    dq, dkp, dkc, dvp, dvc, dsink = out[:6]

    def fold(prev_part, cur_part):
        shifted = jnp.concatenate([prev_part[SWA_BLOCK:], jnp.zeros_like(prev_part[:SWA_BLOCK])], axis=0)
        return (cur_part + shifted).astype(k.dtype)

    dsinks = jnp.sum(dsink.reshape(A_HEADS, SWA_BLOCK), axis=1)
    return dq, fold(dkp, dkc), fold(dvp, dvc), dsinks, _reduce_scatter_tail(out[6:], "mid_grads")


swa_nat.defvjp(_swa_nat_fwd, _swa_nat_bwd)

N_PAIR = B_HEADS // 2


def _flash_nat_fwd_call(q, k, v, shards):
    s = q.shape[0]
    t = min(FLASH_T, s)
    nb = s // t
    d = LANES
    n_arr = len(shards)

    def body(*refs):
        q_ref, k_ref, v_ref = refs[:3]
        shard_refs = refs[3:3 + n_arr]
        o_ref, lse_ref = refs[3 + n_arr:5 + n_arr]
        gathered_refs = refs[5 + n_arr:5 + 2 * n_arr]
        vt_ref, m_ref, l_ref, acc_ref = refs[5 + 2 * n_arr:9 + 2 * n_arr]
        pair, i = pl.program_id(0), pl.program_id(1)
        ag_start, ag_forward, ag_finish = _allgather_phases(shard_refs, gathered_refs, *refs[9 + 2 * n_arr:])

        @pl.when((pair == 0) & (i == 0))
        def _():
            ag_start()

        @pl.when((pair == N_PAIR // 2) & (i == 0))
        def _():
            ag_forward()

        @pl.when(i == 0)
        def _():
            for hh in range(2):
                for chunk in range(nb):
                    rows = slice(chunk * t, (chunk + 1) * t)
                    vt_ref[hh, :, rows] = v_ref[rows, hh * d:(hh + 1) * d].T

        outs = []
        for hh in range(2):
            qb = q_ref[:, hh * d:(hh + 1) * d]
            m_ref[...] = jnp.full_like(m_ref, -jnp.inf)
            l_ref[...] = jnp.zeros_like(l_ref)
            acc_ref[...] = jnp.zeros_like(acc_ref)

            def step(j, on_diagonal, hh=hh, qb=qb):
                keys = pl.ds(pl.multiple_of(j * t, t), t)
                sc_t = lax.dot_general(k_ref[keys, hh * d:(hh + 1) * d], qb, NT_DIMS, preferred_element_type=F32)
                if on_diagonal:
                    key = lax.broadcasted_iota(jnp.int32, (t, t), 0)
                    qry = lax.broadcasted_iota(jnp.int32, (t, t), 1)
                    sc_t = jnp.where(qry >= key, sc_t, -jnp.inf)
                m_old = m_ref[...]
                m_new = jnp.maximum(m_old, jnp.max(sc_t, axis=0, keepdims=True))
                alpha = jnp.exp2((m_old - m_new) * EXP2_SCALE)
                p_t = jnp.exp2((sc_t - m_new) * EXP2_SCALE)
                l_ref[...] = alpha * l_ref[...] + jnp.sum(p_t, axis=0, keepdims=True)
                acc_ref[...] = alpha * acc_ref[...] + jnp.dot(vt_ref[hh, :, keys], p_t.astype(BF16),
                                                              preferred_element_type=F32)
                m_ref[...] = m_new

            def below(j, carry, step=step):
                step(j, False)
                return carry

            lax.fori_loop(0, i, below, 0)
            step(i, True)
            outs.append((acc_ref[...] / l_ref[...]).T)
            lse_ref[hh] = m_ref[...] * EXP2_SCALE + jnp.log2(l_ref[...])
        o_ref[...] = (outs[0] + pltpu.roll(outs[1], HALF, 1)).astype(o_ref.dtype)

        @pl.when((pair == N_PAIR - 1) & (i == nb - 1))
        def _():
            ag_finish()

    return pl.pallas_call(
        body, name="mla_fwd", grid=(N_PAIR, nb),
        in_specs=[pl.BlockSpec((t, 2 * d), lambda p, i: (i, p)),
                  pl.BlockSpec((s, 2 * d), lambda p, i: (0, p)),
                  pl.BlockSpec((s, 2 * d), lambda p, i: (0, p))] + [HBM_SPEC] * n_arr,
        out_specs=[pl.BlockSpec((t, d), lambda p, i: (i, p)),
                   pl.BlockSpec((2, 1, t), lambda p, i: (p, 0, i))] + [HBM_SPEC] * n_arr,
        out_shape=[jax.ShapeDtypeStruct((s, N_PAIR * d), BF16), jax.ShapeDtypeStruct((B_HEADS, 1, s), F32)]
                  + _allgather_out_shapes(shards),
        scratch_shapes=[pltpu.VMEM((2, d, s), BF16), pltpu.VMEM((1, t), F32), pltpu.VMEM((1, t), F32),
                        pltpu.VMEM((d, t), F32)] + _allgather_sems(n_arr),
        compiler_params=_params("arbitrary", "arbitrary"),
    )(q, k, v, *shards)


def _flash_nat_delta_call(o, do):
    s, w = o.shape
    t = min(FLASH_T, s)

    def body(o_ref, do_ref, out_ref):
        prod = o_ref[...].astype(F32) * do_ref[...].astype(F32)
        lane = lax.broadcasted_iota(jnp.int32, (w, LANES), 0) // V_DIM
        head = lax.broadcasted_iota(jnp.int32, (w, LANES), 1)
        out_ref[...] = jnp.dot(prod, (lane == head).astype(F32), precision=lax.Precision.HIGHEST,
                               preferred_element_type=F32)

    spec = pl.BlockSpec((t, w), lambda i: (i, 0))
    return pl.pallas_call(
        body, name="mla_delta", grid=(s // t,), in_specs=[spec, spec],
        out_specs=pl.BlockSpec((t, LANES), lambda i: (i, 0)),
        out_shape=jax.ShapeDtypeStruct((s, LANES), F32), compiler_params=_params("parallel"),
    )(o, do)


def _flash_nat_bwd_call(q, k, v, lse_row, delta_row, do, parts):
    s = q.shape[0]
    t = min(FLASH_T, s)
    nb = s // t
    d = LANES
    n_arr = len(parts)

    def body(*refs):
        q_ref, k_ref, v_ref, lse_ref, delta_ref, do_ref = refs[:6]
        part_refs = refs[6:6 + n_arr]
        dq_ref, dk_ref, dv_ref = refs[6 + n_arr:9 + n_arr]
        received_refs = refs[9 + n_arr:9 + 2 * n_arr]
        dq_acc, dk_acc, dv_acc = refs[9 + 2 * n_arr:12 + 2 * n_arr]
        pair, j = pl.program_id(0), pl.program_id(1)
        exchange_start, exchange_finish = _exchange_chips_phases(part_refs, received_refs, *refs[12 + 2 * n_arr:])

        @pl.when((pair == 0) & (j == 0))
        def _():
            exchange_start()

        @pl.when(j == 0)
        def _():
            dq_acc[...] = jnp.zeros_like(dq_acc)

        for hh in range(2):
            kb, vb = k_ref[:, hh * d:(hh + 1) * d], v_ref[:, hh * d:(hh + 1) * d]
            dk_acc[...] = jnp.zeros_like(dk_acc)
            dv_acc[...] = jnp.zeros_like(dv_acc)

            def step(i, on_diagonal, hh=hh, kb=kb, vb=vb):
                rows = pl.ds(pl.multiple_of(i * t, t), t)
                qb = q_ref[rows, hh * d:(hh + 1) * d]
                do_pair = do_ref[rows, :].astype(F32)
                do_h = do_pair if hh == 0 else pltpu.roll(do_pair, HALF, 1)
                dob = jnp.where(_low_half(do_h.shape), do_h, 0.0).astype(BF16)
                sc_t = lax.dot_general(kb, qb, NT_DIMS, preferred_element_type=F32)
                p_t = jnp.exp2(sc_t * EXP2_SCALE - lse_ref[hh, :, rows])
                if on_diagonal:
                    key = lax.broadcasted_iota(jnp.int32, (t, t), 0)
                    qry = lax.broadcasted_iota(jnp.int32, (t, t), 1)
                    p_t = jnp.where(qry >= key, p_t, 0.0)
                dp_t = lax.dot_general(vb, dob, NT_DIMS, preferred_element_type=F32)
                ds_t = (p_t * (dp_t - delta_ref[hh, :, rows])).astype(BF16)
                dv_acc[...] += jnp.dot(p_t.astype(BF16), dob, preferred_element_type=F32)
                dk_acc[...] += jnp.dot(ds_t, qb, preferred_element_type=F32)
                dq_acc[hh, rows, :] += lax.dot_general(ds_t, kb, TN_DIMS, preferred_element_type=F32)

            def above(i, carry, step=step):
                step(i, False)
                return carry

            step(j, True)
            lax.fori_loop(j + 1, nb, above, 0)
            dk_ref[:, hh * d:(hh + 1) * d] = (dk_acc[...] * MLA_SCALE).astype(dk_ref.dtype)
            dv_ref[:, hh * d:(hh + 1) * d] = dv_acc[...].astype(dv_ref.dtype)

        @pl.when(j == nb - 1)
        def _():
            for hh in range(2):
                dq_ref[:, hh * d:(hh + 1) * d] = (dq_acc[hh] * MLA_SCALE).astype(dq_ref.dtype)

        @pl.when((pair == N_PAIR - 1) & (j == nb - 1))
        def _():
            exchange_finish()

    full_spec = pl.BlockSpec((s, 2 * d), lambda p, j: (0, p))
    tile_spec = pl.BlockSpec((t, 2 * d), lambda p, j: (j, p))
    row_spec = pl.BlockSpec((2, 1, s), lambda p, j: (p, 0, 0))
    return pl.pallas_call(
        body, name="mla_bwd", grid=(N_PAIR, nb),
        in_specs=[full_spec, tile_spec, tile_spec, row_spec, row_spec, pl.BlockSpec((s, d), lambda p, j: (0, p))]
                 + [HBM_SPEC] * n_arr,
        out_specs=[full_spec, tile_spec, tile_spec] + [HBM_SPEC] * n_arr,
        out_shape=[jax.ShapeDtypeStruct(q.shape, q.dtype)] * 3 + [jax.ShapeDtypeStruct(p.shape, p.dtype) for p in parts],
        scratch_shapes=[pltpu.VMEM((2, s, d), F32), pltpu.VMEM((t, d), F32), pltpu.VMEM((t, d), F32)]
                       + _exchange_chips_sems(n_arr),
        compiler_params=_params("arbitrary", "arbitrary"),
    )(q, k, v, lse_row, delta_row, do, *parts)


def _reduce_scatter_head(cts, tag):
    received = _exchange_sibling(list(cts), tag + "_exchange_sibling")
    my_c = lax.axis_index("c").astype(jnp.int32).reshape(1)
    return [_pair_add(m, r, my_c, "%s_pair_add_%d" % (tag, i)) for i, (m, r) in enumerate(zip(cts, received))]


def _reduce_scatter_tail(chip_parts, tag):
    return tuple(_sum_blocks(r, "%s_sum_%d" % (tag, i)) for i, r in enumerate(chip_parts))


@jax.custom_vjp
def flash_nat(q, k, v, shards):
    out = _flash_nat_fwd_call(q, k, v, [s.astype(BF16) for s in shards])
    return out[0], tuple(out[2:])


def _flash_nat_fwd(q, k, v, shards):
    out = _flash_nat_fwd_call(q, k, v, [s.astype(BF16) for s in shards])
    return (out[0], tuple(out[2:])), (q, k, v, out[0], out[1])


def _flash_nat_bwd(res, cts):
    q, k, v, o, lse = res
    do, d_gathered = cts
    delta = _flash_nat_delta_call(o, do)[:, :B_HEADS].T.reshape(B_HEADS, 1, q.shape[0])
    out = _flash_nat_bwd_call(q, k, v, lse, delta, do, _reduce_scatter_head(d_gathered, "mlp_grads"))
    return out[0], out[1], out[2], _reduce_scatter_tail(out[3:], "mlp_grads")


flash_nat.defvjp(_flash_nat_fwd, _flash_nat_bwd)


HBM_SPEC = pl.BlockSpec(memory_space=pltpu.HBM)


def _allgather(shards, name):
    n_arr = len(shards)

    def body(*refs):
        start, forward, finish = _allgather_phases(refs[:n_arr], refs[n_arr:2 * n_arr], *refs[2 * n_arr:])
        start()
        forward()
        finish()

    return pl.pallas_call(
        body, name=name, out_shape=_allgather_out_shapes(shards),
        in_specs=[HBM_SPEC] * n_arr, out_specs=[HBM_SPEC] * n_arr,
        scratch_shapes=_allgather_sems(n_arr),
    )(*shards)


def _allgather_out_shapes(shards):
    return [jax.ShapeDtypeStruct((N_DEV,) + s.shape, s.dtype) for s in shards]


def _allgather_sems(n_arr):
    return [pltpu.SemaphoreType.DMA((7, n_arr)), pltpu.SemaphoreType.DMA((7, n_arr)), pltpu.SemaphoreType.DMA((n_arr,))]


def _allgather_phases(x_refs, out_refs, send_sems, recv_sems, local_sems):
    arrays = range(len(x_refs))
    x, y, c = lax.axis_index("x"), lax.axis_index("y"), lax.axis_index("c")
    me, sibling = (x, y, c), (x, y, 1 - c)
    chips = [(1 - x, y), (x, 1 - y), (1 - x, 1 - y)]

    def rows(a, px, py, pc):
        return out_refs[a].at[4 * px + 2 * py + pc]

    def copy(a, k, block, to, src=None):
        return pltpu.make_async_remote_copy(
            src_ref=rows(a, *block) if src is None else src, dst_ref=rows(a, *block),
            send_sem=send_sems.at[k, a], recv_sem=recv_sems.at[k, a], device_id=to, device_id_type=MESH_ID)

    def mine():
        return [pltpu.make_async_copy(x_refs[a], rows(a, *me), local_sems.at[a]) for a in arrays]

    def first():
        return [cp for a in arrays for cp in
                [copy(a, 0, me, sibling, src=x_refs[a])]
                + [copy(a, 1 + j, me, (*chip, c), src=x_refs[a]) for j, chip in enumerate(chips)]]

    def passed():
        return [copy(a, 4 + j, (*chip, c), sibling) for j, chip in enumerate(chips) for a in arrays]

    def start():
        for cp in mine() + first():
            cp.start()

    def forward():
        for j, chip in enumerate(chips):
            for a in arrays:
                copy(a, 1 + j, (*chip, c), me).wait_recv()
                copy(a, 4 + j, (*chip, c), sibling).start()

    def finish():
        for a in arrays:
            copy(a, 0, sibling, me).wait_recv()
        for j, chip in enumerate(chips):
            for a in arrays:
                copy(a, 4 + j, (*chip, 1 - c), me).wait_recv()
        for cp in first() + passed():
            cp.wait_send()
        for cp in mine():
            cp.wait()

    return start, forward, finish


N_CHIP = 4


def _exchange_sibling(parts, name):
    n_arr = len(parts)

    def body(*refs):
        in_refs, recv_refs = refs[:n_arr], refs[n_arr:2 * n_arr]
        send_sems, recv_sems = refs[2 * n_arr:]
        x, y, c = lax.axis_index("x"), lax.axis_index("y"), lax.axis_index("c")
        copies = []
        for a in range(n_arr):
            for q in range(N_CHIP):
                copies.append(pltpu.make_async_remote_copy(
                    src_ref=in_refs[a].at[2 * q + 1 - c], dst_ref=recv_refs[a].at[q],
                    send_sem=send_sems.at[q, a], recv_sem=recv_sems.at[q, a],
                    device_id=(x, y, 1 - c), device_id_type=MESH_ID))
        for cp in copies:
            cp.start()
        for cp in copies:
            cp.wait()

    return pl.pallas_call(
        body, name=name, out_shape=[jax.ShapeDtypeStruct((N_CHIP,) + p.shape[1:], p.dtype) for p in parts],
        in_specs=[HBM_SPEC] * n_arr, out_specs=[HBM_SPEC] * n_arr,
        scratch_shapes=[pltpu.SemaphoreType.DMA((N_CHIP, n_arr)), pltpu.SemaphoreType.DMA((N_CHIP, n_arr))],
    )(*parts)


def _exchange_chips(parts, name):
    n_arr = len(parts)

    def body(*refs):
        start, finish = _exchange_chips_phases(refs[:n_arr], refs[n_arr:2 * n_arr], *refs[2 * n_arr:])
        start()
        finish()

    return pl.pallas_call(
        body, name=name, out_shape=[jax.ShapeDtypeStruct(p.shape, p.dtype) for p in parts],
        in_specs=[HBM_SPEC] * n_arr, out_specs=[HBM_SPEC] * n_arr,
        scratch_shapes=_exchange_chips_sems(n_arr),
    )(*parts)


def _exchange_chips_sems(n_arr):
    return [pltpu.SemaphoreType.DMA((N_CHIP - 1, n_arr)), pltpu.SemaphoreType.DMA((N_CHIP - 1, n_arr)),
            pltpu.SemaphoreType.DMA((n_arr,))]


def _exchange_chips_phases(in_refs, out_refs, send_sems, recv_sems, local_sems):
    n_arr = len(in_refs)
    x, y, c = lax.axis_index("x"), lax.axis_index("y"), lax.axis_index("c")
    me = 2 * x + y

    def copies():
        out = [pltpu.make_async_copy(in_refs[a].at[me], out_refs[a].at[me], local_sems.at[a]) for a in range(n_arr)]
        for k in range(1, N_CHIP):
            px = 1 - x if k & 2 else x
            py = 1 - y if k & 1 else y
            for a in range(n_arr):
                out.append(pltpu.make_async_remote_copy(
                    src_ref=in_refs[a].at[2 * px + py], dst_ref=out_refs[a].at[me],
                    send_sem=send_sems.at[k - 1, a], recv_sem=recv_sems.at[k - 1, a],
                    device_id=(px, py, c), device_id_type=MESH_ID))
        return out

    def start():
        for cp in copies():
            cp.start()

    def finish():
        for cp in copies():
            cp.wait()

    return start, finish


def _row_tile(r, ccols, blocks):
    cap = max(16, (2 * 1024 * 1024) // (4 * ccols * blocks))
    return _pick(r, cap, 16)


def _pair_add(mine, theirs, my_c, name):
    _, r, ccols = mine.shape
    tr = _row_tile(r, ccols, 1)

    def body(c_ref, a_ref, b_ref, o_ref):
        o_ref[...] = (a_ref[...].astype(F32) + b_ref[...].astype(F32)).astype(o_ref.dtype)

    spec = pl.BlockSpec((None, tr, ccols), lambda q, i, c_ref: (q, i, 0))
    return pl.pallas_call(
        body, name=name,
        grid_spec=pltpu.PrefetchScalarGridSpec(
            num_scalar_prefetch=1, grid=(N_CHIP, r // tr),
            in_specs=[pl.BlockSpec((None, tr, ccols), lambda q, i, c_ref: (2 * q + c_ref[0], i, 0)), spec],
            out_specs=spec),
        out_shape=jax.ShapeDtypeStruct(theirs.shape, theirs.dtype),
        compiler_params=_params("parallel", "parallel"),
    )(my_c, mine, theirs)


def _sum_blocks(parts, name):
    nb, r, ccols = parts.shape
    tr = _row_tile(r, ccols, nb)

    def body(p_ref, o_ref):
        acc = p_ref[0].astype(F32)
        for i in range(1, nb):
            acc = acc + p_ref[i].astype(F32)
        o_ref[...] = acc

    return pl.pallas_call(
        body, name=name, grid=(r // tr,),
        in_specs=[pl.BlockSpec((nb, tr, ccols), lambda i: (0, i, 0))],
        out_specs=pl.BlockSpec((tr, ccols), lambda i: (i, 0)),
        out_shape=jax.ShapeDtypeStruct((r, ccols), F32),
        compiler_params=_params("parallel"),
    )(parts)


def _gather_wire(shards, wire_dtypes):
    return tuple(_allgather([s.astype(d) for s, d in zip(shards, wire_dtypes)], "weights_allgather"))


@functools.partial(jax.custom_vjp, nondiff_argnums=(1,))
def fsdp_gather(shards, wire_dtypes):
    return _gather_wire(shards, wire_dtypes)


def _fsdp_gather_fwd(shards, wire_dtypes):
    return _gather_wire(shards, wire_dtypes), None


def _fsdp_gather_bwd(wire_dtypes, _, cts):
    chip_parts = _exchange_chips(_reduce_scatter_head(cts, "grads"), "grads_exchange_chips")
    return (_reduce_scatter_tail(chip_parts, "grads"),)


fsdp_gather.defvjp(_fsdp_gather_fwd, _fsdp_gather_bwd)


@jax.custom_vjp
def replicated(vec):
    return vec


def _replicated_fwd(vec):
    return vec, None


def _replicated_bwd(_, ct):
    return (_sum_blocks(_allgather([ct], "small_grad_allgather")[0], "small_grad_sum"),)


replicated.defvjp(_replicated_fwd, _replicated_bwd)


def _adamw(w, g, m, v, name):
    rows, cols = w.shape
    tr = _pick(rows, 256, 8) if rows % 8 == 0 else rows

    def body(w_ref, g_ref, m_ref, v_ref, d_ref, nm_ref, nv_ref):
        g_ = g_ref[...]
        m_ = ADAM_B1 * m_ref[...] + (1.0 - ADAM_B1) * g_
        v_ = ADAM_B2 * v_ref[...] + (1.0 - ADAM_B2) * jnp.square(g_)
        m_hat = m_ / (1.0 - ADAM_B1 ** ADAM_STEP)
        v_hat = v_ / (1.0 - ADAM_B2 ** ADAM_STEP)
        d_ref[...] = -ADAM_LR * (m_hat / (jnp.sqrt(v_hat) + ADAM_EPS) + ADAM_WD * w_ref[...])
        nm_ref[...] = m_
        nv_ref[...] = v_

    spec = pl.BlockSpec((tr, cols), lambda i: (i, 0))
    return pl.pallas_call(
        body, name=name, grid=(rows // tr,), in_specs=[spec] * 4, out_specs=[spec] * 3,
        out_shape=[jax.ShapeDtypeStruct(w.shape, F32)] * 3, compiler_params=_params("parallel"),
    )(w, g, m, v)


COL_SHARDED = ("w_in", "w_uq", "w_ukv", "w_branch_a", "w_branch_b", "w_up", "w_ple")
EARLY = ("w_in",)
MID = ("w_uq", "w_ukv", "w_branch_a", "w_branch_b", "w_out")
LATE = ("w_up", "w_down", "w_ple_gate", "w_ple")
SMALL = ("attn_pre_norm", "attn_post_norm", "b_gate", "q_a_norm", "kv_a_norm", "mlp_pre_norm", "mlp_post_norm",
         "conv_b", "ple_norm", "sinks")
SMALL_COLS = 128


def _pack_rows(arrays, cols, row_mult):
    flat = jnp.concatenate([a.reshape(-1) for a in arrays])
    pad = (-flat.shape[0]) % (cols * row_mult)
    return jnp.pad(flat, (0, pad)).reshape(-1, cols)


def _unpack_small(vec, shapes):
    flat = vec.reshape(-1)
    out, off = {}, 0
    for name in SMALL:
        n = shapes[name]
        out[name] = flat[off:off + n].reshape(1, n)
        off += n + (-n) % SMALL_COLS
    return out


def _pad_lanes(t, width):
    return jnp.pad(t, [(0, 0)] * (t.ndim - 1) + [(0, width - t.shape[-1])])


def _pad_rows(t, rows):
    return jnp.pad(t, [(0, 0)] * (t.ndim - 2) + [(0, rows - t.shape[-2]), (0, 0)])


FRONT_SIZES = (512, 128, 128, 256, 128)
FRONT_BOUNDS = (0, 512, 640, 768, 1024, 1152, 1280)
PE_LANE = NOPE_DIM


def _arrange_w_in_t(wt):
    k = wt.shape[1]
    n_front = sum(FRONT_SIZES)
    front, kr, gates = wt[:n_front], wt[n_front:n_front + ROPE_DIM], wt[n_front + ROPE_DIM:]
    kr_slab = jnp.concatenate([jnp.zeros((PE_LANE, k), wt.dtype), kr,
                               jnp.zeros((HEAD_PAD - PE_LANE - ROPE_DIM, k), wt.dtype)], axis=0)
    return jnp.concatenate([front, kr_slab], axis=0), gates


def _arrange_w_uq_t(wt):
    k = wt.shape[1]
    return _pad_rows(wt.reshape(B_HEADS, NOPE_DIM + ROPE_DIM, k), HEAD_PAD).reshape(B_HEADS * HEAD_PAD, k)


def _arrange_w_ukv_t(wt):
    k = wt.shape[1]
    w = wt.reshape(B_HEADS, 2, NOPE_DIM, k)
    slabs = [_pad_rows(w[:, part], HEAD_PAD).reshape(B_HEADS * HEAD_PAD, k) for part in range(2)]
    return jnp.concatenate(slabs, axis=0)


def _rope_tables(positions, s):
    pos = positions.reshape(s, 1).astype(F32)

    def angles(dim):
        return pos * ROPE_THETA ** (-(jnp.arange(0, dim, 2, dtype=F32) / dim))

    cos_a, sin_a = jnp.cos(angles(A_HEAD_DIM)), jnp.sin(angles(A_HEAD_DIM))
    zero_a = jnp.zeros_like(sin_a)
    tables_a = [jnp.tile(jnp.concatenate(pair, axis=1), (1, LANES // A_HEAD_DIM))
                for pair in ((cos_a, cos_a), (-sin_a, zero_a), (zero_a, sin_a))]
    cos_b, sin_b = jnp.cos(angles(ROPE_DIM)), jnp.sin(angles(ROPE_DIM))
    zero_b = jnp.zeros_like(sin_b)

    def slab(first, second, fill):
        return jnp.concatenate([jnp.full((s, PE_LANE), fill, F32), first, second,
                                jnp.full((s, HEAD_PAD - PE_LANE - ROPE_DIM), fill, F32)], axis=1)

    tables_b = [slab(cos_b, cos_b, 1.0), slab(-sin_b, zero_b, 0.0), slab(zero_b, sin_b, 0.0)]
    return tables_a + tables_b


def _local_loss(wts, x, p, tables, target):
    s = x.shape[0]
    small_shapes = {n: wts[n].shape[-1] for n in SMALL}
    small_vec = _pack_rows([_pad_lanes(wts[n].reshape(1, -1), small_shapes[n] + (-small_shapes[n]) % SMALL_COLS)
                            for n in SMALL], SMALL_COLS, 8)
    sm = _unpack_small(replicated(small_vec), small_shapes)
    def shard(n):
        return wts[n].T if n in COL_SHARDED else wts[n]

    gathered = fsdp_gather(tuple([shard(n) for n in EARLY] + [_pack_rows([wts["conv_w"]], SMALL_COLS, 8)]),
                           (BF16,) * len(EARLY) + (F32,))
    big = {n: g.reshape(-1, g.shape[2]) for n, g in zip(EARLY, gathered)}
    ch = wts["conv_w"].shape[1]
    conv_w = gathered[-1].reshape(N_DEV, -1)[:, :CONV_W * ch].reshape(N_DEV, CONV_W, ch)
    conv_w = conv_w.transpose(1, 0, 2).reshape(CONV_W, N_DEV * ch)

    w_front_t, w_gates_t = _arrange_w_in_t(big["w_in"])
    tables_a, tables_b = tables[:3], tables[3:]

    (h1,) = stage("prenorm", _f_prenorm, [x], [sm["attn_pre_norm"]], out_dtypes=[BF16])
    qa, ka, va, cqn, ckvn, kpe = proj_stage(
        "prep", _f_prep, [(h1, w_front_t, "nt", "w_front", True)], params=[sm["q_a_norm"], sm["kv_a_norm"]],
        consts=tables, splits=[FRONT_BOUNDS], out_dtypes=[BF16, BF16, BF16, BF16, BF16, F32])
    ya, mid = swa_nat(qa, ka, va, sm["sinks"].reshape(-1), tuple(shard(n) for n in MID))
    big.update({n: g.reshape(-1, g.shape[2]) for n, g in zip(MID, mid)})

    (q2,) = proj_stage("qrope", _f_qrope, [(cqn, _arrange_w_uq_t(big["w_uq"]), "nt", "w_uq", True)],
                       consts=tables_b, out_dtypes=[BF16])
    k2, v2 = proj_stage("kv", _f_kv, [(ckvn, _arrange_w_ukv_t(big["w_ukv"]), "nt", "w_ukv", True)], extra=[kpe],
                        splits=[(0, B_HEADS * HEAD_PAD, 2 * B_HEADS * HEAD_PAD), None], out_dtypes=[BF16, BF16])
    yb, late = flash_nat(q2, k2, v2, tuple(shard(n) for n in LATE))
    big.update({n: g.reshape(-1, g.shape[2]) for n, g in zip(LATE, late)})

    (mixed,) = proj_stage(
        "gate", _f_gate, [(h1, w_gates_t, "nt", "w_gates", True), (ya, big["w_branch_a"], "nt", "w_branch_a", True),
                          (yb, big["w_branch_b"], "nt", "w_branch_b", True)],
        params=[sm["b_gate"][:, :D_MODEL], sm["b_gate"][:, D_MODEL:]],
        splits=[(0, D_MODEL, 2 * D_MODEL), None, None], out_dtypes=[BF16])
    x1, h2 = proj_stage("post_attn", _f_post, [(mixed, big["w_out"], "nn", "w_out", True)], extra=[x],
                        params=[sm["attn_post_norm"], sm["mlp_pre_norm"]], out_dtypes=[F32, BF16])

    act = mlp_up(h2, big["w_up"], conv_w, sm["conv_b"])
    x2, h3 = proj_stage("post_mlp", _f_post, [(act, big["w_down"], "nn", "w_down", True)], extra=[x1],
                        params=[sm["mlp_post_norm"], sm["ple_norm"]], out_dtypes=[F32, BF16])

    (rowloss,) = proj_stage("loss", _f_out, [(h3, big["w_ple_gate"], "nn", "w_ple_gate", True),
                                             (p, big["w_ple"], "nt", "w_ple", False)], extra=[x2], consts=[target])
    return jnp.sum(rowloss)


WEIGHTS = ["attn_pre_norm", "attn_post_norm", "w_in", "b_gate", "sinks", "q_a_norm", "w_uq", "kv_a_norm", "w_ukv",
           "w_branch_a", "w_branch_b", "w_out", "mlp_pre_norm", "mlp_post_norm", "w_up", "conv_w", "conv_b",
           "w_down", "ple_norm", "w_ple_gate", "w_ple"]


def kernel(x, p, positions, attn_pre_norm, attn_post_norm, w_in, b_gate, sinks, q_a_norm, w_uq, kv_a_norm, w_ukv, w_branch_a, w_branch_b, w_out, mlp_pre_norm, mlp_post_norm, w_up, conv_w, conv_b, w_down, ple_norm, w_ple_gate, w_ple, loss_target, m_attn_pre_norm, m_attn_post_norm, m_w_in, m_b_gate, m_sinks, m_q_a_norm, m_w_uq, m_kv_a_norm, m_w_ukv, m_w_branch_a, m_w_branch_b, m_w_out, m_mlp_pre_norm, m_mlp_post_norm, m_w_up, m_conv_w, m_conv_b, m_w_down, m_ple_norm, m_w_ple_gate, m_w_ple, v_attn_pre_norm, v_attn_post_norm, v_w_in, v_b_gate, v_sinks, v_q_a_norm, v_w_uq, v_kv_a_norm, v_w_ukv, v_w_branch_a, v_w_branch_b, v_w_out, v_mlp_pre_norm, v_mlp_post_norm, v_w_up, v_conv_w, v_conv_b, v_w_down, v_ple_norm, v_w_ple_gate, v_w_ple):
    given = dict(locals())
    s = x.shape[1]
    wts = {n: given[n][0] if given[n].ndim == 3 else given[n] for n in WEIGHTS}
    tables = _rope_tables(positions, s)
    local_loss, (grads, grad_x) = jax.value_and_grad(_local_loss, argnums=(0, 1))(
        wts, x[0], p[0, 0], tables, loss_target[0])
    loss = lax.psum(local_loss, AXES)

    outs = {"grad": [], "delta": [], "m": [], "v": []}
    for n in WEIGHTS:
        shape = given[n].shape
        w2 = wts[n].reshape(-1, shape[-1])
        g2 = grads[n].reshape(w2.shape)
        delta, new_m, new_v = _adamw(w2, g2, given["m_" + n].reshape(w2.shape), given["v_" + n].reshape(w2.shape),
                                     "adamw_" + n)
        outs["grad"].append(g2.reshape(shape))
        outs["delta"].append(delta.reshape(shape))
        outs["m"].append(new_m.reshape(shape))
        outs["v"].append(new_v.reshape(shape))
    return (loss, grad_x[None], *outs["grad"], *outs["delta"], *outs["m"], *outs["v"])
```

```python
import functools

import numpy as np
import jax
import jax.numpy as jnp
from jax import lax
from jax.experimental import pallas as pl
from jax.experimental.pallas import tpu as pltpu

F32 = jnp.float32
BF16 = jnp.bfloat16
MESH_ID = pl.DeviceIdType.MESH
AXES = ("x", "y", "c")
N_DEV = 8

D_MODEL = 1024
RMS_EPS = 1e-6
ROPE_THETA = 10000.0
SWA_BLOCK = 128
A_HEADS, A_KV_HEADS, A_HEAD_DIM = 8, 2, 64
A_GROUP = A_HEADS // A_KV_HEADS
B_HEADS, Q_LORA, KV_LORA, NOPE_DIM, ROPE_DIM, V_DIM = 8, 256, 128, 64, 32, 64
D_FF = 2816
CONV_W = 3
HEAD_PAD = 128

ADAM_LR, ADAM_B1, ADAM_B2, ADAM_EPS, ADAM_WD, ADAM_STEP = 0.001, 0.9, 0.999, 1e-08, 0.01, 10

VMEM_LIMIT = 48 * 1024 * 1024
MM_TM, MM_TN, MM_TK_TOKENS = 512, 1408, 1024
MM_VMEM_BUDGET = 36 * 1024 * 1024
FLASH_T = 1024
CONV_TS = 128
CONV_CHUNK = 256


def _params(*sem):
    return pltpu.CompilerParams(dimension_semantics=sem, vmem_limit_bytes=VMEM_LIMIT)


def _pick(dim, cap, mult):
    best = None
    for t in range(mult, min(dim, cap) + 1, mult):
        if dim % t == 0:
            best = t
    return dim if best is None else best


def _divisors(dim, mult):
    return [t for t in range(mult, dim + 1, mult) if dim % t == 0] or [dim]


def _matmul_tiles(m, n, kdim, form, sizes):
    sa, sb, so = sizes
    tk = _pick(kdim, MM_TK_TOKENS, 128) if form == "tn" else kdim
    cap_m = MM_TN if form == "tn" else MM_TM
    best = None
    for tm in _divisors(m, 128):
        for tn in _divisors(n, 128):
            need = 2 * (tm * tk * sa + tk * tn * sb + tm * tn * so) + (tm * tn * 4 if tk != kdim else 0)
            if tm > cap_m or tn > MM_TN or need > MM_VMEM_BUDGET:
                continue
            if best is None or (tm * tn, tm) > (best[0] * best[1], best[0]):
                best = (tm, tn)
    return best[0], best[1], tk


def _matmul(a, b, form, *, out_dtype=F32, name):
    if form == "tn":
        (kdim, m), n = a.shape, b.shape[1]
    else:
        (m, kdim), n = a.shape, (b.shape[1] if form == "nn" else b.shape[0])
    sizes = (a.dtype.itemsize, b.dtype.itemsize, jnp.dtype(out_dtype).itemsize)
    tm, tn, tk = _matmul_tiles(m, n, kdim, form, sizes)
    nk = kdim // tk
    rows_outer = nk > 1 or (m // tm) * b.size * sizes[1] <= (n // tn) * a.size * sizes[0]

    def ij(fn):
        return (lambda i, j, k: fn(i, j, k)) if rows_outer else (lambda j, i, k: fn(i, j, k))

    a_spec = (pl.BlockSpec((tk, tm), ij(lambda i, j, k: (k, i))) if form == "tn"
              else pl.BlockSpec((tm, tk), ij(lambda i, j, k: (i, k))))
    b_spec = (pl.BlockSpec((tn, tk), ij(lambda i, j, k: (j, k))) if form == "nt"
              else pl.BlockSpec((tk, tn), ij(lambda i, j, k: (k, j))))
    dims = (((0 if form == "tn" else 1,), (1 if form == "nt" else 0,)), ((), ()))

    def product(a_ref, b_ref):
        return lax.dot_general(a_ref[...].astype(BF16), b_ref[...].astype(BF16), dims, preferred_element_type=F32)

    if nk == 1:
        def body(a_ref, b_ref, o_ref):
            o_ref[...] = product(a_ref, b_ref).astype(o_ref.dtype)

        scratch = []
    else:
        def body(a_ref, b_ref, o_ref, acc_ref):
            k = pl.program_id(2)

            @pl.when(k == 0)
            def _():
                acc_ref[...] = jnp.zeros_like(acc_ref)

            acc_ref[...] += product(a_ref, b_ref)

            @pl.when(k == nk - 1)
            def _():
                o_ref[...] = acc_ref[...].astype(o_ref.dtype)

        scratch = [pltpu.VMEM((tm, tn), F32)]

    return pl.pallas_call(
        body, name=name, grid=(m // tm, n // tn, nk) if rows_outer else (n // tn, m // tm, nk),
        in_specs=[a_spec, b_spec],
        out_specs=pl.BlockSpec((tm, tn), ij(lambda i, j, k: (i, j))),
        out_shape=jax.ShapeDtypeStruct((m, n), out_dtype),
        scratch_shapes=scratch,
        compiler_params=_params("parallel", "parallel", "arbitrary"),
    )(a, b)


def _pairs(bounds):
    return list(zip(bounds[:-1], bounds[1:]))


def _split(v, bounds):
    return [v[:, a:b] for a, b in _pairs(bounds)]


def _stage_build(name, f, tiled, params, consts, splits, ts, out_dtypes, ct_dtypes=None):
    n_t, n_p, n_c = len(tiled), len(params), len(consts)
    ct_dtypes = [t.dtype for t in tiled] if ct_dtypes is None else ct_dtypes
    s = tiled[0].shape[0]
    ts = min(ts, s)
    grid = (s // ts,)
    if splits is None:
        splits = [None] * n_t
    in_bounds = [(0, t.shape[1]) if b is None else tuple(b) for t, b in zip(tiled, splits)]

    def tile_aval(arr):
        return jax.ShapeDtypeStruct((ts, arr.shape[1]), arr.dtype)

    slab_avals = [[jax.ShapeDtypeStruct((ts, e - a), t.dtype) for a, e in _pairs(b)]
                  for t, b in zip(tiled, in_bounds)]
    out_avals = jax.eval_shape(f, slab_avals, list(params), [tile_aval(c) for c in consts])
    out_bounds = [tuple(np.cumsum([0] + [o.shape[1] for o in slabs]).tolist()) for slabs in out_avals]
    out_dtypes = [F32] * len(out_bounds) if out_dtypes is None else out_dtypes
    out_shapes = [jax.ShapeDtypeStruct((s, b[-1]), d) for b, d in zip(out_bounds, out_dtypes)]

    def row_spec(width):
        return pl.BlockSpec((ts, width), lambda i: (i, 0))

    def par_spec(arr):
        return pl.BlockSpec(arr.shape, lambda i: (0, 0))

    in_specs = ([row_spec(t.shape[1]) for t in tiled] + [par_spec(p) for p in params]
                + [row_spec(c.shape[1]) for c in consts])

    def load(refs):
        t = [_split(r[...], b) for r, b in zip(refs[:n_t], in_bounds)]
        p = [r[...] for r in refs[n_t:n_t + n_p]]
        c = [r[...] for r in refs[n_t + n_p:n_t + n_p + n_c]]
        return t, p, c

    def store(refs, values, bounds):
        for ref, slabs, b in zip(refs, values, bounds):
            for v, (a, e) in zip(slabs, _pairs(b)):
                ref[:, a:e] = v.astype(ref.dtype)

    def run_fwd(tiled, params, consts):
        def body(*refs):
            t, p, c = load(refs)
            store(refs[n_t + n_p + n_c:], f(t, p, c), out_bounds)

        return pl.pallas_call(
            body, name=name + "_fwd", grid=grid, in_specs=in_specs,
            out_specs=[row_spec(b[-1]) for b in out_bounds], out_shape=out_shapes,
            compiler_params=_params("parallel"),
        )(*tiled, *params, *consts)

    def run_bwd(tiled, params, consts, cts):
        n_in = n_t + n_p + n_c
        n_o = len(out_bounds)

        def body(*refs):
            t, p, c = load(refs)
            g = [_split(r[...].astype(F32), b) for r, b in zip(refs[n_in:n_in + n_o], out_bounds)]
            _, pull = jax.vjp(lambda t_, p_: f(t_, p_, c), t, p)
            dt, dp = pull(g)
            store(refs[n_in + n_o:n_in + n_o + n_t], dt, in_bounds)
            first = pl.program_id(0) == 0
            for ref, d in zip(refs[n_in + n_o + n_t:], dp):
                @pl.when(first)
                def _(ref=ref):
                    ref[...] = jnp.zeros_like(ref)

                ref[...] += d

        res = pl.pallas_call(
            body, name=name + "_bwd", grid=grid,
            in_specs=in_specs + [row_spec(b[-1]) for b in out_bounds],
            out_specs=[row_spec(t.shape[1]) for t in tiled] + [par_spec(p) for p in params],
            out_shape=[jax.ShapeDtypeStruct(t.shape, d) for t, d in zip(tiled, ct_dtypes)]
                      + [jax.ShapeDtypeStruct(p.shape, F32) for p in params],
            compiler_params=_params("arbitrary"),
        )(*tiled, *params, *consts, *cts)
        return tuple(res[:n_t]), tuple(res[n_t:])

    return run_fwd, run_bwd


def stage(name, f, tiled, params=(), consts=(), splits=None, ts=256, out_dtypes=None):
    tiled, params, consts = tuple(tiled), tuple(params), tuple(consts)
    run_fwd, run_bwd = _stage_build(name, f, tiled, params, consts, splits, ts, out_dtypes)

    @jax.custom_vjp
    def op(tiled, params, consts):
        return tuple(run_fwd(tiled, params, consts))

    def op_fwd(tiled, params, consts):
        return op(tiled, params, consts), (tiled, params, consts)

    def op_bwd(res, cts):
        tiled, params, consts = res
        dt, dp = run_bwd(tiled, params, consts, cts)
        return dt, dp, tuple(jnp.zeros_like(c) for c in consts)

    op.defvjp(op_fwd, op_bwd)
    return op(tiled, params, consts)


def proj_stage(name, f, projections, extra=(), params=(), consts=(), splits=None, ts=256, out_dtypes=None):
    n_z = len(projections)
    forms = [pr[2] for pr in projections]
    names = [pr[3] for pr in projections]
    need_da = [pr[4] for pr in projections]
    extra, params, consts = tuple(extra), tuple(params), tuple(consts)

    def matmuls(a_list, w_list):
        return tuple(_matmul(a, w, form, out_dtype=F32, name=n + "_fwd")
                     for a, w, form, n in zip(a_list, w_list, forms, names))

    def build(zs, ct=False):
        ct_dtypes = [BF16] * n_z + [e.dtype for e in extra] if ct else None
        return _stage_build(name, f, tuple(zs) + extra, params, consts, splits, ts, out_dtypes, ct_dtypes)

    @jax.custom_vjp
    def op(a_list, w_list, extra, params, consts):
        zs = matmuls(a_list, w_list)
        return tuple(build(zs)[0](zs + extra, params, consts))

    def op_fwd(a_list, w_list, extra, params, consts):
        zs = matmuls(a_list, w_list)
        return tuple(build(zs)[0](zs + extra, params, consts)), (a_list, w_list, zs, extra, params, consts)

    def op_bwd(res, cts):
        a_list, w_list, zs, extra, params, consts = res
        dt, dp = build(zs, ct=True)[1](zs + extra, params, consts, cts)
        da_list, dw_list = [], []
        for a, w, dz, form, n, want in zip(a_list, w_list, dt[:n_z], forms, names, need_da):
            if form == "nn":
                da = _matmul(dz, w, "nt", out_dtype=a.dtype, name=n + "_da") if want else jnp.zeros_like(a)
                dw = _matmul(a, dz, "tn", out_dtype=w.dtype, name=n + "_dw")
            else:
                da = _matmul(dz, w, "nn", out_dtype=a.dtype, name=n + "_da") if want else jnp.zeros_like(a)
                dw = _matmul(dz, a, "tn", out_dtype=w.dtype, name=n + "_dw")
            da_list.append(da)
            dw_list.append(dw)
        return tuple(da_list), tuple(dw_list), tuple(dt[n_z:]), dp, tuple(jnp.zeros_like(c) for c in consts)

    op.defvjp(op_fwd, op_bwd)
    return op(tuple(pr[0] for pr in projections), tuple(pr[1] for pr in projections), extra, params, consts)


def _rms(t, g):
    return t * lax.rsqrt(jnp.mean(t * t, axis=-1, keepdims=True) + RMS_EPS) * g


@functools.partial(jax.custom_vjp, nondiff_argnums=(1,))
def _lane_roll(t, shift):
    return pltpu.roll(t, shift % t.shape[-1], t.ndim - 1)


def _lane_roll_fwd(t, shift):
    return _lane_roll(t, shift), None


def _lane_roll_bwd(shift, _, ct):
    return (pltpu.roll(ct, (-shift) % ct.shape[-1], ct.ndim - 1),)


_lane_roll.defvjp(_lane_roll_fwd, _lane_roll_bwd)


def _rope_lanes(t, tables, half):
    reps = t.shape[1] // tables[0].shape[1]
    c, s_lo, s_hi = [jnp.concatenate([tb] * reps, axis=1) if reps > 1 else tb for tb in tables]
    return t * c + _lane_roll(t, -half) * s_lo + _lane_roll(t, half) * s_hi


def _f_prenorm(t, p, c):
    return [[_rms(t[0][0], p[0])]]


def _f_prep(t, p, c):
    qa, ka, va, cq, ckv, kr = t[0]
    return [[_rope_lanes(qa, c[0:3], A_HEAD_DIM // 2)], [_rope_lanes(ka, c[0:3], A_HEAD_DIM // 2)], [va],
            [_rms(cq, p[0])], [_rms(ckv, p[1])], [_rope_lanes(kr, c[3:6], ROPE_DIM // 2)]]


def _f_qrope(t, p, c):
    return [[_rope_lanes(t[0][0], c, ROPE_DIM // 2)]]


def _f_kv(t, p, c):
    (k_nope, v), (k_pe,) = t
    return [[k_nope + jnp.concatenate([k_pe] * B_HEADS, axis=1)], [v]]


def _f_gate(t, p, c):
    (ga, gb), (pa,), (pb,) = t
    ba, bb = p
    return [[jax.nn.sigmoid(ga + ba) * pa + jax.nn.sigmoid(gb + bb) * pb]]


def _f_post(t, p, c):
    (branch,), (residual,) = t
    x1 = residual + _rms(branch, p[0])
    return [[x1], [_rms(x1, p[1])]]


def _f_out(t, p, c):
    (gate,), (emb,), (x2,) = t
    y = x2 + jax.nn.sigmoid(gate) * emb
    err = y - c[0]
    return [[0.5 * jnp.mean(err * err, axis=-1, keepdims=True)]]


def _shift_down(cur, prev, has_prev):
    full = jnp.concatenate([prev * has_prev, cur], axis=0)
    return pltpu.roll(full, 1, 0)[HALO:], pltpu.roll(full, 2, 0)[HALO:]


GELU_C = float(np.sqrt(2.0 / np.pi))
GELU_A = 0.044715
HALO = 8


def _gelu_tanh(x):
    x2 = x * x
    th = jnp.tanh(x * (GELU_C + (GELU_C * GELU_A) * x2))
    half = 0.5 + 0.5 * th
    return x * half, half + x * (0.5 - 0.5 * (th * th)) * (GELU_C + (3.0 * GELU_C * GELU_A) * x2)


def _row_sum(t):
    return jnp.sum(t, axis=0, keepdims=True)


def _conv3(cur, prev, w_ref, b_ref, has_prev):
    u1, u2 = _shift_down(cur, prev, has_prev)
    return w_ref[2:3, :] * cur + w_ref[1:2, :] * u1 + w_ref[0:1, :] * u2 + b_ref[...], u1, u2


def _mlp_act_specs(s):
    ts = min(CONV_TS, s)
    hb = ts // HALO

    def half_specs(h):
        return [pl.BlockSpec((ts, D_FF), lambda i: (i, h)),
                pl.BlockSpec((HALO, D_FF), lambda i: (jnp.maximum(i * hb - 1, 0), h))]

    def par_specs(h):
        return [pl.BlockSpec((CONV_W, D_FF), lambda i: (0, h)), pl.BlockSpec((1, D_FF), lambda i: (0, h))]

    return ts, hb, half_specs, par_specs


def _mlp_act_fwd_call(up, conv_w, conv_b):
    s = up.shape[0]
    ts, hb, half_specs, par_specs = _mlp_act_specs(s)

    def body(g_ref, gp_ref, v_ref, vp_ref, wg_ref, bg_ref, wv_ref, bv_ref, o_ref):
        has_prev = (pl.program_id(0) > 0).astype(F32)

        def chunk(cidx, carry):
            cols = pl.ds(pl.multiple_of(cidx * CONV_CHUNK, CONV_CHUNK), CONV_CHUNK)
            u_g, _, _ = _conv3(g_ref[:, cols], gp_ref[:, cols], wg_ref.at[:, cols], bg_ref.at[:, cols], has_prev)
            u_v, _, _ = _conv3(v_ref[:, cols], vp_ref[:, cols], wv_ref.at[:, cols], bv_ref.at[:, cols], has_prev)
            o_ref[:, cols] = (_gelu_tanh(u_g)[0] * u_v).astype(o_ref.dtype)
            return carry

        lax.fori_loop(0, D_FF // CONV_CHUNK, chunk, 0)

    return pl.pallas_call(
        body, name="mlp_act_fwd", grid=(s // ts,),
        in_specs=half_specs(0) + half_specs(1) + par_specs(0) + par_specs(1),
        out_specs=pl.BlockSpec((ts, D_FF), lambda i: (i, 0)),
        out_shape=jax.ShapeDtypeStruct((s, D_FF), BF16),
        compiler_params=_params("parallel"),
    )(up, up, up, up, conv_w, conv_b, conv_w, conv_b)


def _mlp_act_bwd_call(up, conv_w, conv_b, dact):
    s = up.shape[0]
    ts, hb, half_specs, par_specs = _mlp_act_specs(s)
    nt = s // ts
    ext = ts + HALO
    bf16_rows = 2 * HALO

    def next_spec(rows, h):
        return pl.BlockSpec((rows, D_FF), lambda i: (jnp.minimum((i + 1) * (ts // rows), s // rows - 1), h))

    def body(g_ref, gp_ref, gn_ref, v_ref, vp_ref, vn_ref, wg_ref, bg_ref, wv_ref, bv_ref, da_ref, dan_ref,
             dup_ref, dwg_ref, dbg_ref, dwv_ref, dbv_ref):
        i = pl.program_id(0)
        has_prev, has_next = (i > 0).astype(F32), (i < nt - 1).astype(F32)

        @pl.when(i == 0)
        def _():
            for ref in (dwg_ref, dbg_ref, dwv_ref, dbv_ref):
                ref[...] = jnp.zeros_like(ref)

        def chunk(cidx, carry):
            cols = pl.ds(pl.multiple_of(cidx * CONV_CHUNK, CONV_CHUNK), CONV_CHUNK)
            g_ext = jnp.concatenate([g_ref[:, cols], gn_ref[:, cols]], axis=0)
            v_ext = jnp.concatenate([v_ref[:, cols], vn_ref[:, cols]], axis=0)
            u_g, g1, g2 = _conv3(g_ext, gp_ref[:, cols], wg_ref.at[:, cols], bg_ref.at[:, cols], has_prev)
            u_v, v1, v2 = _conv3(v_ext, vp_ref[:, cols], wv_ref.at[:, cols], bv_ref.at[:, cols], has_prev)
            da_ext = jnp.concatenate([da_ref[:, cols].astype(F32),
                                      dan_ref[:, cols].astype(F32)[0:HALO] * has_next], axis=0)
            act_g, dact_g = _gelu_tanh(u_g)
            du_g = da_ext * u_v * dact_g
            du_v = da_ext * act_g
            for du, w_ref, x0, x1, x2, dw_ref, db_ref, lo in ((du_g, wg_ref, g_ext, g1, g2, dwg_ref, dbg_ref, 0),
                                                          (du_v, wv_ref, v_ext, v1, v2, dwv_ref, dbv_ref, D_FF)):
                d1 = pltpu.roll(du, ext - 1, 0)
                d2 = pltpu.roll(du, ext - 2, 0)
                dup = w_ref[2:3, cols] * du + w_ref[1:2, cols] * d1 + w_ref[0:1, cols] * d2
                out_cols = pl.ds(pl.multiple_of(lo + cidx * CONV_CHUNK, CONV_CHUNK), CONV_CHUNK)
                dup_ref[:, out_cols] = dup[0:ts].astype(dup_ref.dtype)
                own = du[0:ts]
                dw_ref[0:1, cols] += _row_sum(own * x2[0:ts])
                dw_ref[1:2, cols] += _row_sum(own * x1[0:ts])
                dw_ref[2:3, cols] += _row_sum(own * x0[0:ts])
                db_ref[:, cols] += _row_sum(own)
            return carry

        lax.fori_loop(0, D_FF // CONV_CHUNK, chunk, 0)

    par_out = [pl.BlockSpec((CONV_W, D_FF), lambda i: (0, 0)), pl.BlockSpec((1, D_FF), lambda i: (0, 0))]
    par_shapes = [jax.ShapeDtypeStruct((CONV_W, D_FF), F32), jax.ShapeDtypeStruct((1, D_FF), F32)]
    return pl.pallas_call(
        body, name="mlp_act_bwd", grid=(nt,),
        in_specs=(half_specs(0) + [next_spec(HALO, 0)] + half_specs(1) + [next_spec(HALO, 1)]
                  + par_specs(0) + par_specs(1)
                  + [pl.BlockSpec((ts, D_FF), lambda i: (i, 0)), next_spec(bf16_rows, 0)]),
        out_specs=[pl.BlockSpec((ts, 2 * D_FF), lambda i: (i, 0))] + par_out + par_out,
        out_shape=[jax.ShapeDtypeStruct((s, 2 * D_FF), BF16)] + par_shapes + par_shapes,
        compiler_params=_params("arbitrary"),
    )(up, up, up, up, up, up, conv_w, conv_b, conv_w, conv_b, dact, dact)


@jax.custom_vjp
def mlp_up(h2, w_up_t, conv_w, conv_b):
    return _mlp_act_fwd_call(_matmul(h2, w_up_t, "nt", out_dtype=F32, name="w_up_fwd"), conv_w, conv_b)


def _mlp_up_fwd(h2, w_up_t, conv_w, conv_b):
    up = _matmul(h2, w_up_t, "nt", out_dtype=F32, name="w_up_fwd")
    return _mlp_act_fwd_call(up, conv_w, conv_b), (h2, w_up_t, up, conv_w, conv_b)


def _mlp_up_bwd(res, dact):
    h2, w_up_t, up, conv_w, conv_b = res
    dup, dwg, dbg, dwv, dbv = _mlp_act_bwd_call(up, conv_w, conv_b, dact)
    dh2 = _matmul(dup, w_up_t, "nn", out_dtype=h2.dtype, name="w_up_da")
    dw = _matmul(dup, h2, "tn", out_dtype=w_up_t.dtype, name="w_up_dw")
    return dh2, dw, jnp.concatenate([dwg, dwv], axis=1), jnp.concatenate([dbg, dbv], axis=1)


mlp_up.defvjp(_mlp_up_fwd, _mlp_up_bwd)


SWA_ROWS = A_GROUP * SWA_BLOCK


def _swa_sink_rows(sink_ref, g):
    return jnp.concatenate([jnp.full((SWA_BLOCK, 1), sink_ref[g * A_GROUP + h], F32) for h in range(A_GROUP)], axis=0)


def _swa_operands(q_ref, kp_ref, kc_ref, vp_ref, vc_ref, sink_ref):
    groups = []
    for g in range(A_KV_HEADS):
        groups.append((_swa_stack_heads(q_ref, g), _dup_half(kp_ref[...], g), _dup_half(kc_ref[...], g),
                       _dup_half(vp_ref[...], g), _dup_half(vc_ref[...], g)))
    return groups, jnp.concatenate([_swa_sink_rows(sink_ref, g) for g in range(A_KV_HEADS)], axis=0)


def _swa_probs(groups, sink, prev_off):
    scale = A_HEAD_DIM ** -0.5
    sp = jnp.concatenate([lax.dot_general(gr[0], gr[1], NT_DIMS, preferred_element_type=F32) for gr in groups], axis=0)
    sc = jnp.concatenate([lax.dot_general(gr[0], gr[2], NT_DIMS, preferred_element_type=F32) for gr in groups], axis=0)
    qi = lax.broadcasted_iota(jnp.int32, sp.shape, 0) & (SWA_BLOCK - 1)
    kj = lax.broadcasted_iota(jnp.int32, sp.shape, 1)
    in_cur = kj <= qi
    sw = jnp.where(in_cur, sc, jnp.where(kj > qi + prev_off, sp, -jnp.inf)) * scale
    m = jnp.maximum(jnp.max(sw, axis=-1, keepdims=True), sink)
    e, es = jnp.exp(sw - m), jnp.exp(sink - m)
    den = jnp.sum(e, axis=-1, keepdims=True) + es
    return e / den, in_cur, es / den


def _swa_split(t, in_cur):
    cur = jnp.where(in_cur, t, 0.0)
    return t - cur, cur


MLA_SCALE = (NOPE_DIM + ROPE_DIM) ** -0.5
EXP2_SCALE = MLA_SCALE * float(np.log2(np.e))
NT_DIMS = (((1,), (1,)), ((), ()))
TN_DIMS = (((0,), (0,)), ((), ()))


LANES = 128
HALF = LANES // 2


def _low_half(shape):
    return lax.broadcasted_iota(jnp.int32, shape, len(shape) - 1) < HALF


def _dup_half(x, g):
    xf = x.astype(F32)
    keep = _low_half(xf.shape) if g == 0 else jnp.logical_not(_low_half(xf.shape))
    xm = jnp.where(keep, xf, 0.0)
    return (xm + pltpu.roll(xm, HALF, 1)).astype(x.dtype)


def _fold_half(r, g):
    total = r + pltpu.roll(r, HALF, 1)
    keep = _low_half(r.shape) if g == 0 else jnp.logical_not(_low_half(r.shape))
    return jnp.where(keep, total, 0.0)


def _swa_stack_heads(ref, g):
    parts = []
    for tile in range(2):
        slab = ref[:, (2 * g + tile) * LANES:(2 * g + tile + 1) * LANES]
        low = _low_half(slab.shape)
        parts += [jnp.where(low, slab, jnp.zeros_like(slab)), jnp.where(low, jnp.zeros_like(slab), slab)]
    return jnp.concatenate(parts, axis=0)


def _swa_unstack_heads(ref, g, rows):
    for tile in range(2):
        a = rows[(2 * tile) * SWA_BLOCK:(2 * tile + 1) * SWA_BLOCK]
        b = rows[(2 * tile + 1) * SWA_BLOCK:(2 * tile + 2) * SWA_BLOCK]
        ref[:, (2 * g + tile) * LANES:(2 * g + tile + 1) * LANES] = jnp.where(_low_half(a.shape), a, b).astype(ref.dtype)


def _swa_nat_specs():
    blk = SWA_BLOCK
    q_spec = pl.BlockSpec((blk, A_HEADS * A_HEAD_DIM), lambda n: (n, 0))
    prev_spec = pl.BlockSpec((blk, LANES), lambda n: (jnp.maximum(n - 1, 0), 0))
    cur_spec = pl.BlockSpec((blk, LANES), lambda n: (n, 0))
    return q_spec, prev_spec, cur_spec, pl.BlockSpec(memory_space=pltpu.SMEM)


def _swa_nat_fwd_call(q, k, v, sinks, shards):
    s = q.shape[0]
    nblk = s // SWA_BLOCK
    n_arr = len(shards)
    q_spec, prev_spec, cur_spec, sink_spec = _swa_nat_specs()

    def body(*refs):
        q_ref, kp_ref, kc_ref, vp_ref, vc_ref, sink_ref = refs[:6]
        o_ref = refs[6 + n_arr]
        n = pl.program_id(0)
        ag_start, ag_forward, ag_finish = _allgather_phases(refs[6:6 + n_arr], refs[7 + n_arr:7 + 2 * n_arr],
                                                            *refs[7 + 2 * n_arr:])

        @pl.when(n == 0)
        def _():
            ag_start()

        @pl.when(n == nblk // 2)
        def _():
            ag_forward()

        prev_off = jnp.where(n > 0, 0, SWA_BLOCK)
        groups, sink = _swa_operands(q_ref, kp_ref, kc_ref, vp_ref, vc_ref, sink_ref)
        p, in_cur, _ = _swa_probs(groups, sink, prev_off)
        ppb, pcb = [t.astype(BF16) for t in _swa_split(p, in_cur)]
        for g, (_, _, _, vp, vc) in enumerate(groups):
            rows = slice(g * SWA_ROWS, (g + 1) * SWA_ROWS)
            out = (jnp.dot(ppb[rows], vp, preferred_element_type=F32)
                   + jnp.dot(pcb[rows], vc, preferred_element_type=F32))
            _swa_unstack_heads(o_ref, g, out)

        @pl.when(n == nblk - 1)
        def _():
            ag_finish()

    return pl.pallas_call(
        body, name="swa_fwd", grid=(nblk,),
        in_specs=[q_spec, prev_spec, cur_spec, prev_spec, cur_spec, sink_spec] + [HBM_SPEC] * n_arr,
        out_specs=[q_spec] + [HBM_SPEC] * n_arr,
        out_shape=[jax.ShapeDtypeStruct(q.shape, BF16)] + _allgather_out_shapes(shards),
        scratch_shapes=_allgather_sems(n_arr),
        compiler_params=_params("arbitrary"),
    )(q, k, k, v, v, sinks, *shards)


def _swa_nat_bwd_call(q, k, v, sinks, do, parts):
    s = q.shape[0]
    nblk = s // SWA_BLOCK
    n_arr = len(parts)
    q_spec, prev_spec, cur_spec, sink_spec = _swa_nat_specs()
    scale = A_HEAD_DIM ** -0.5
    dsink_spec = pl.BlockSpec((A_KV_HEADS, SWA_ROWS, 1), lambda n: (0, 0, 0))

    def body(*refs):
        q_ref, kp_ref, kc_ref, vp_ref, vc_ref, sink_ref, do_ref = refs[:7]
        dq_ref, dkp_ref, dkc_ref, dvp_ref, dvc_ref, dsink_ref = refs[7 + n_arr:13 + n_arr]
        n = pl.program_id(0)
        exchange_start, exchange_finish = _exchange_chips_phases(
            refs[7:7 + n_arr], refs[13 + n_arr:13 + 2 * n_arr], *refs[13 + 2 * n_arr:])

        @pl.when(n == 0)
        def _():
            exchange_start()
        prev_off = jnp.where(n > 0, 0, SWA_BLOCK)

        @pl.when(n == 0)
        def _():
            dsink_ref[...] = jnp.zeros_like(dsink_ref)

        groups, sink = _swa_operands(q_ref, kp_ref, kc_ref, vp_ref, vc_ref, sink_ref)
        dobs = [_swa_stack_heads(do_ref, g) for g in range(A_KV_HEADS)]
        p, in_cur, ps = _swa_probs(groups, sink, prev_off)
        ppb, pcb = [t.astype(BF16) for t in _swa_split(p, in_cur)]

        def per_group(fn):
            return jnp.concatenate([fn(g, slice(g * SWA_ROWS, (g + 1) * SWA_ROWS)) for g in range(A_KV_HEADS)], axis=0)

        out = per_group(lambda g, rows: jnp.dot(ppb[rows], groups[g][3], preferred_element_type=F32)
                        + jnp.dot(pcb[rows], groups[g][4], preferred_element_type=F32))
        delta = jnp.sum(jnp.concatenate(dobs, axis=0).astype(F32) * out, axis=-1, keepdims=True)
        dp = jnp.where(in_cur,
                       per_group(lambda g, rows: lax.dot_general(dobs[g], groups[g][4], NT_DIMS,
                                                                 preferred_element_type=F32)),
                       per_group(lambda g, rows: lax.dot_general(dobs[g], groups[g][3], NT_DIMS,
                                                                 preferred_element_type=F32)))
        dsp, dsc = [t.astype(BF16) for t in _swa_split(p * (dp - delta), in_cur)]
        dsink_ref[...] += (-ps * delta).reshape(dsink_ref.shape)
        totals = [jnp.zeros((SWA_BLOCK, LANES), F32) for _ in range(4)]
        for g, (qb, kp, kc, _, _) in enumerate(groups):
            rows = slice(g * SWA_ROWS, (g + 1) * SWA_ROWS)
            dq = (jnp.dot(dsp[rows], kp, preferred_element_type=F32)
                  + jnp.dot(dsc[rows], kc, preferred_element_type=F32)) * scale
            _swa_unstack_heads(dq_ref, g, dq)
            pieces = [lax.dot_general(dsp[rows], qb, TN_DIMS, preferred_element_type=F32) * scale,
                      lax.dot_general(dsc[rows], qb, TN_DIMS, preferred_element_type=F32) * scale,
                      lax.dot_general(ppb[rows], dobs[g], TN_DIMS, preferred_element_type=F32),
                      lax.dot_general(pcb[rows], dobs[g], TN_DIMS, preferred_element_type=F32)]
            totals = [tot + _fold_half(r, g) for tot, r in zip(totals, pieces)]
        dkp_ref[...], dkc_ref[...], dvp_ref[...], dvc_ref[...] = totals

        @pl.when(n == nblk - 1)
        def _():
            exchange_finish()

    kv_shape = jax.ShapeDtypeStruct(k.shape, F32)
    return pl.pallas_call(
        body, name="swa_bwd", grid=(nblk,),
        in_specs=[q_spec, prev_spec, cur_spec, prev_spec, cur_spec, sink_spec, q_spec] + [HBM_SPEC] * n_arr,
        out_specs=[q_spec, cur_spec, cur_spec, cur_spec, cur_spec, dsink_spec] + [HBM_SPEC] * n_arr,
        out_shape=[jax.ShapeDtypeStruct(q.shape, q.dtype), kv_shape, kv_shape, kv_shape, kv_shape,
                   jax.ShapeDtypeStruct((A_KV_HEADS, SWA_ROWS, 1), F32)]
                  + [jax.ShapeDtypeStruct(p.shape, p.dtype) for p in parts],
        scratch_shapes=_exchange_chips_sems(n_arr),
        compiler_params=_params("arbitrary"),
    )(q, k, k, v, v, sinks, do, *parts)


@jax.custom_vjp
def swa_nat(q, k, v, sinks, shards):
    out = _swa_nat_fwd_call(q, k, v, sinks, [s.astype(BF16) for s in shards])
    return out[0], tuple(out[1:])


def _swa_nat_fwd(q, k, v, sinks, shards):
    out = _swa_nat_fwd_call(q, k, v, sinks, [s.astype(BF16) for s in shards])
    return (out[0], tuple(out[1:])), (q, k, v, sinks)


def _swa_nat_bwd(res, cts):
    q, k, v, sinks = res
    do, d_gathered = cts
    out = _swa_nat_bwd_call(q, k, v, sinks, do, _reduce_scatter_head(d_gathered, "mid_grads"))
    dq, dkp, dkc, dvp, dvc, dsink = out[:6]

    def fold(prev_part, cur_part):
        shifted = jnp.concatenate([prev_part[SWA_BLOCK:], jnp.zeros_like(prev_part[:SWA_BLOCK])], axis=0)
        return (cur_part + shifted).astype(k.dtype)

    dsinks = jnp.sum(dsink.reshape(A_HEADS, SWA_BLOCK), axis=1)
    return dq, fold(dkp, dkc), fold(dvp, dvc), dsinks, _reduce_scatter_tail(out[6:], "mid_grads")


swa_nat.defvjp(_swa_nat_fwd, _swa_nat_bwd)

N_PAIR = B_HEADS // 2


def _flash_nat_fwd_call(q, k, v, shards):
    s = q.shape[0]
    t = min(FLASH_T, s)
    nb = s // t
    d = LANES
    n_arr = len(shards)

    def body(*refs):
        q_ref, k_ref, v_ref = refs[:3]
        shard_refs = refs[3:3 + n_arr]
        o_ref, lse_ref = refs[3 + n_arr:5 + n_arr]
        gathered_refs = refs[5 + n_arr:5 + 2 * n_arr]
        vt_ref, m_ref, l_ref, acc_ref = refs[5 + 2 * n_arr:9 + 2 * n_arr]
        pair, i = pl.program_id(0), pl.program_id(1)
        ag_start, ag_forward, ag_finish = _allgather_phases(shard_refs, gathered_refs, *refs[9 + 2 * n_arr:])

        @pl.when((pair == 0) & (i == 0))
        def _():
            ag_start()

        @pl.when((pair == N_PAIR // 2) & (i == 0))
        def _():
            ag_forward()

        @pl.when(i == 0)
        def _():
            for hh in range(2):
                for chunk in range(nb):
                    rows = slice(chunk * t, (chunk + 1) * t)
                    vt_ref[hh, :, rows] = v_ref[rows, hh * d:(hh + 1) * d].T

        outs = []
        for hh in range(2):
            qb = q_ref[:, hh * d:(hh + 1) * d]
            m_ref[...] = jnp.full_like(m_ref, -jnp.inf)
            l_ref[...] = jnp.zeros_like(l_ref)
            acc_ref[...] = jnp.zeros_like(acc_ref)

            def step(j, on_diagonal, hh=hh, qb=qb):
                keys = pl.ds(pl.multiple_of(j * t, t), t)
                sc_t = lax.dot_general(k_ref[keys, hh * d:(hh + 1) * d], qb, NT_DIMS, preferred_element_type=F32)
                if on_diagonal:
                    key = lax.broadcasted_iota(jnp.int32, (t, t), 0)
                    qry = lax.broadcasted_iota(jnp.int32, (t, t), 1)
                    sc_t = jnp.where(qry >= key, sc_t, -jnp.inf)
                m_old = m_ref[...]
                m_new = jnp.maximum(m_old, jnp.max(sc_t, axis=0, keepdims=True))
                alpha = jnp.exp2((m_old - m_new) * EXP2_SCALE)
                p_t = jnp.exp2((sc_t - m_new) * EXP2_SCALE)
                l_ref[...] = alpha * l_ref[...] + jnp.sum(p_t, axis=0, keepdims=True)
                acc_ref[...] = alpha * acc_ref[...] + jnp.dot(vt_ref[hh, :, keys], p_t.astype(BF16),
                                                              preferred_element_type=F32)
                m_ref[...] = m_new

            def below(j, carry, step=step):
                step(j, False)
                return carry

            lax.fori_loop(0, i, below, 0)
            step(i, True)
            outs.append((acc_ref[...] / l_ref[...]).T)
            lse_ref[hh] = m_ref[...] * EXP2_SCALE + jnp.log2(l_ref[...])
        o_ref[...] = (outs[0] + pltpu.roll(outs[1], HALF, 1)).astype(o_ref.dtype)

        @pl.when((pair == N_PAIR - 1) & (i == nb - 1))
        def _():
            ag_finish()

    return pl.pallas_call(
        body, name="mla_fwd", grid=(N_PAIR, nb),
        in_specs=[pl.BlockSpec((t, 2 * d), lambda p, i: (i, p)),
                  pl.BlockSpec((s, 2 * d), lambda p, i: (0, p)),
                  pl.BlockSpec((s, 2 * d), lambda p, i: (0, p))] + [HBM_SPEC] * n_arr,
        out_specs=[pl.BlockSpec((t, d), lambda p, i: (i, p)),
                   pl.BlockSpec((2, 1, t), lambda p, i: (p, 0, i))] + [HBM_SPEC] * n_arr,
        out_shape=[jax.ShapeDtypeStruct((s, N_PAIR * d), BF16), jax.ShapeDtypeStruct((B_HEADS, 1, s), F32)]
                  + _allgather_out_shapes(shards),
        scratch_shapes=[pltpu.VMEM((2, d, s), BF16), pltpu.VMEM((1, t), F32), pltpu.VMEM((1, t), F32),
                        pltpu.VMEM((d, t), F32)] + _allgather_sems(n_arr),
        compiler_params=_params("arbitrary", "arbitrary"),
    )(q, k, v, *shards)


def _flash_nat_delta_call(o, do):
    s, w = o.shape
    t = min(FLASH_T, s)

    def body(o_ref, do_ref, out_ref):
        prod = o_ref[...].astype(F32) * do_ref[...].astype(F32)
        lane = lax.broadcasted_iota(jnp.int32, (w, LANES), 0) // V_DIM
        head = lax.broadcasted_iota(jnp.int32, (w, LANES), 1)
        out_ref[...] = jnp.dot(prod, (lane == head).astype(F32), precision=lax.Precision.HIGHEST,
                               preferred_element_type=F32)

    spec = pl.BlockSpec((t, w), lambda i: (i, 0))
    return pl.pallas_call(
        body, name="mla_delta", grid=(s // t,), in_specs=[spec, spec],
        out_specs=pl.BlockSpec((t, LANES), lambda i: (i, 0)),
        out_shape=jax.ShapeDtypeStruct((s, LANES), F32), compiler_params=_params("parallel"),
    )(o, do)


def _flash_nat_bwd_call(q, k, v, lse_row, delta_row, do, parts):
    s = q.shape[0]
    t = min(FLASH_T, s)
    nb = s // t
    d = LANES
    n_arr = len(parts)

    def body(*refs):
        q_ref, k_ref, v_ref, lse_ref, delta_ref, do_ref = refs[:6]
        part_refs = refs[6:6 + n_arr]
        dq_ref, dk_ref, dv_ref = refs[6 + n_arr:9 + n_arr]
        received_refs = refs[9 + n_arr:9 + 2 * n_arr]
        dq_acc, dk_acc, dv_acc = refs[9 + 2 * n_arr:12 + 2 * n_arr]
        pair, j = pl.program_id(0), pl.program_id(1)
        exchange_start, exchange_finish = _exchange_chips_phases(part_refs, received_refs, *refs[12 + 2 * n_arr:])

        @pl.when((pair == 0) & (j == 0))
        def _():
            exchange_start()

        @pl.when(j == 0)
        def _():
            dq_acc[...] = jnp.zeros_like(dq_acc)

        for hh in range(2):
            kb, vb = k_ref[:, hh * d:(hh + 1) * d], v_ref[:, hh * d:(hh + 1) * d]
            dk_acc[...] = jnp.zeros_like(dk_acc)
            dv_acc[...] = jnp.zeros_like(dv_acc)

            def step(i, on_diagonal, hh=hh, kb=kb, vb=vb):
                rows = pl.ds(pl.multiple_of(i * t, t), t)
                qb = q_ref[rows, hh * d:(hh + 1) * d]
                do_pair = do_ref[rows, :].astype(F32)
                do_h = do_pair if hh == 0 else pltpu.roll(do_pair, HALF, 1)
                dob = jnp.where(_low_half(do_h.shape), do_h, 0.0).astype(BF16)
                sc_t = lax.dot_general(kb, qb, NT_DIMS, preferred_element_type=F32)
                p_t = jnp.exp2(sc_t * EXP2_SCALE - lse_ref[hh, :, rows])
                if on_diagonal:
                    key = lax.broadcasted_iota(jnp.int32, (t, t), 0)
                    qry = lax.broadcasted_iota(jnp.int32, (t, t), 1)
                    p_t = jnp.where(qry >= key, p_t, 0.0)
                dp_t = lax.dot_general(vb, dob, NT_DIMS, preferred_element_type=F32)
                ds_t = (p_t * (dp_t - delta_ref[hh, :, rows])).astype(BF16)
                dv_acc[...] += jnp.dot(p_t.astype(BF16), dob, preferred_element_type=F32)
                dk_acc[...] += jnp.dot(ds_t, qb, preferred_element_type=F32)
                dq_acc[hh, rows, :] += lax.dot_general(ds_t, kb, TN_DIMS, preferred_element_type=F32)

            def above(i, carry, step=step):
                step(i, False)
                return carry

            step(j, True)
            lax.fori_loop(j + 1, nb, above, 0)
            dk_ref[:, hh * d:(hh + 1) * d] = (dk_acc[...] * MLA_SCALE).astype(dk_ref.dtype)
            dv_ref[:, hh * d:(hh + 1) * d] = dv_acc[...].astype(dv_ref.dtype)

        @pl.when(j == nb - 1)
        def _():
            for hh in range(2):
                dq_ref[:, hh * d:(hh + 1) * d] = (dq_acc[hh] * MLA_SCALE).astype(dq_ref.dtype)

        @pl.when((pair == N_PAIR - 1) & (j == nb - 1))
        def _():
            exchange_finish()

    full_spec = pl.BlockSpec((s, 2 * d), lambda p, j: (0, p))
    tile_spec = pl.BlockSpec((t, 2 * d), lambda p, j: (j, p))
    row_spec = pl.BlockSpec((2, 1, s), lambda p, j: (p, 0, 0))
    return pl.pallas_call(
        body, name="mla_bwd", grid=(N_PAIR, nb),
        in_specs=[full_spec, tile_spec, tile_spec, row_spec, row_spec, pl.BlockSpec((s, d), lambda p, j: (0, p))]
                 + [HBM_SPEC] * n_arr,
        out_specs=[full_spec, tile_spec, tile_spec] + [HBM_SPEC] * n_arr,
        out_shape=[jax.ShapeDtypeStruct(q.shape, q.dtype)] * 3 + [jax.ShapeDtypeStruct(p.shape, p.dtype) for p in parts],
        scratch_shapes=[pltpu.VMEM((2, s, d), F32), pltpu.VMEM((t, d), F32), pltpu.VMEM((t, d), F32)]
                       + _exchange_chips_sems(n_arr),
        compiler_params=_params("arbitrary", "arbitrary"),
    )(q, k, v, lse_row, delta_row, do, *parts)


def _reduce_scatter_head(cts, tag):
    received = _exchange_sibling(list(cts), tag + "_exchange_sibling")
    my_c = lax.axis_index("c").astype(jnp.int32).reshape(1)
    return [_pair_add(m, r, my_c, "%s_pair_add_%d" % (tag, i)) for i, (m, r) in enumerate(zip(cts, received))]


def _reduce_scatter_tail(chip_parts, tag):
    return tuple(_sum_blocks(r, "%s_sum_%d" % (tag, i)) for i, r in enumerate(chip_parts))


@jax.custom_vjp
def flash_nat(q, k, v, shards):
    out = _flash_nat_fwd_call(q, k, v, [s.astype(BF16) for s in shards])
    return out[0], tuple(out[2:])


def _flash_nat_fwd(q, k, v, shards):
    out = _flash_nat_fwd_call(q, k, v, [s.astype(BF16) for s in shards])
    return (out[0], tuple(out[2:])), (q, k, v, out[0], out[1])


def _flash_nat_bwd(res, cts):
    q, k, v, o, lse = res
    do, d_gathered = cts
    delta = _flash_nat_delta_call(o, do)[:, :B_HEADS].T.reshape(B_HEADS, 1, q.shape[0])
    out = _flash_nat_bwd_call(q, k, v, lse, delta, do, _reduce_scatter_head(d_gathered, "mlp_grads"))
    return out[0], out[1], out[2], _reduce_scatter_tail(out[3:], "mlp_grads")


flash_nat.defvjp(_flash_nat_fwd, _flash_nat_bwd)


HBM_SPEC = pl.BlockSpec(memory_space=pltpu.HBM)


def _allgather(shards, name):
    n_arr = len(shards)

    def body(*refs):
        start, forward, finish = _allgather_phases(refs[:n_arr], refs[n_arr:2 * n_arr], *refs[2 * n_arr:])
        start()
        forward()
        finish()

    return pl.pallas_call(
        body, name=name, out_shape=_allgather_out_shapes(shards),
        in_specs=[HBM_SPEC] * n_arr, out_specs=[HBM_SPEC] * n_arr,
        scratch_shapes=_allgather_sems(n_arr),
    )(*shards)


def _allgather_out_shapes(shards):
    return [jax.ShapeDtypeStruct((N_DEV,) + s.shape, s.dtype) for s in shards]


def _allgather_sems(n_arr):
    return [pltpu.SemaphoreType.DMA((7, n_arr)), pltpu.SemaphoreType.DMA((7, n_arr)), pltpu.SemaphoreType.DMA((n_arr,))]


def _allgather_phases(x_refs, out_refs, send_sems, recv_sems, local_sems):
    arrays = range(len(x_refs))
    x, y, c = lax.axis_index("x"), lax.axis_index("y"), lax.axis_index("c")
    me, sibling = (x, y, c), (x, y, 1 - c)
    chips = [(1 - x, y), (x, 1 - y), (1 - x, 1 - y)]

    def rows(a, px, py, pc):
        return out_refs[a].at[4 * px + 2 * py + pc]

    def copy(a, k, block, to, src=None):
        return pltpu.make_async_remote_copy(
            src_ref=rows(a, *block) if src is None else src, dst_ref=rows(a, *block),
            send_sem=send_sems.at[k, a], recv_sem=recv_sems.at[k, a], device_id=to, device_id_type=MESH_ID)

    def mine():
        return [pltpu.make_async_copy(x_refs[a], rows(a, *me), local_sems.at[a]) for a in arrays]

    def first():
        return [cp for a in arrays for cp in
                [copy(a, 0, me, sibling, src=x_refs[a])]
                + [copy(a, 1 + j, me, (*chip, c), src=x_refs[a]) for j, chip in enumerate(chips)]]

    def passed():
        return [copy(a, 4 + j, (*chip, c), sibling) for j, chip in enumerate(chips) for a in arrays]

    def start():
        for cp in mine() + first():
            cp.start()

    def forward():
        for j, chip in enumerate(chips):
            for a in arrays:
                copy(a, 1 + j, (*chip, c), me).wait_recv()
                copy(a, 4 + j, (*chip, c), sibling).start()

    def finish():
        for a in arrays:
            copy(a, 0, sibling, me).wait_recv()
        for j, chip in enumerate(chips):
            for a in arrays:
                copy(a, 4 + j, (*chip, 1 - c), me).wait_recv()
        for cp in first() + passed():
            cp.wait_send()
        for cp in mine():
            cp.wait()

    return start, forward, finish


N_CHIP = 4


def _exchange_sibling(parts, name):
    n_arr = len(parts)

    def body(*refs):
        in_refs, recv_refs = refs[:n_arr], refs[n_arr:2 * n_arr]
        send_sems, recv_sems = refs[2 * n_arr:]
        x, y, c = lax.axis_index("x"), lax.axis_index("y"), lax.axis_index("c")
        copies = []
        for a in range(n_arr):
            for q in range(N_CHIP):
                copies.append(pltpu.make_async_remote_copy(
                    src_ref=in_refs[a].at[2 * q + 1 - c], dst_ref=recv_refs[a].at[q],
                    send_sem=send_sems.at[q, a], recv_sem=recv_sems.at[q, a],
                    device_id=(x, y, 1 - c), device_id_type=MESH_ID))
        for cp in copies:
            cp.start()
        for cp in copies:
            cp.wait()

    return pl.pallas_call(
        body, name=name, out_shape=[jax.ShapeDtypeStruct((N_CHIP,) + p.shape[1:], p.dtype) for p in parts],
        in_specs=[HBM_SPEC] * n_arr, out_specs=[HBM_SPEC] * n_arr,
        scratch_shapes=[pltpu.SemaphoreType.DMA((N_CHIP, n_arr)), pltpu.SemaphoreType.DMA((N_CHIP, n_arr))],
    )(*parts)


def _exchange_chips(parts, name):
    n_arr = len(parts)

    def body(*refs):
        start, finish = _exchange_chips_phases(refs[:n_arr], refs[n_arr:2 * n_arr], *refs[2 * n_arr:])
        start()
        finish()

    return pl.pallas_call(
        body, name=name, out_shape=[jax.ShapeDtypeStruct(p.shape, p.dtype) for p in parts],
        in_specs=[HBM_SPEC] * n_arr, out_specs=[HBM_SPEC] * n_arr,
        scratch_shapes=_exchange_chips_sems(n_arr),
    )(*parts)


def _exchange_chips_sems(n_arr):
    return [pltpu.SemaphoreType.DMA((N_CHIP - 1, n_arr)), pltpu.SemaphoreType.DMA((N_CHIP - 1, n_arr)),
            pltpu.SemaphoreType.DMA((n_arr,))]


def _exchange_chips_phases(in_refs, out_refs, send_sems, recv_sems, local_sems):
    n_arr = len(in_refs)
    x, y, c = lax.axis_index("x"), lax.axis_index("y"), lax.axis_index("c")
    me = 2 * x + y

    def copies():
        out = [pltpu.make_async_copy(in_refs[a].at[me], out_refs[a].at[me], local_sems.at[a]) for a in range(n_arr)]
        for k in range(1, N_CHIP):
            px = 1 - x if k & 2 else x
            py = 1 - y if k & 1 else y
            for a in range(n_arr):
                out.append(pltpu.make_async_remote_copy(
                    src_ref=in_refs[a].at[2 * px + py], dst_ref=out_refs[a].at[me],
                    send_sem=send_sems.at[k - 1, a], recv_sem=recv_sems.at[k - 1, a],
                    device_id=(px, py, c), device_id_type=MESH_ID))
        return out

    def start():
        for cp in copies():
            cp.start()

    def finish():
        for cp in copies():
            cp.wait()

    return start, finish


def _row_tile(r, ccols, blocks):
    cap = max(16, (2 * 1024 * 1024) // (4 * ccols * blocks))
    return _pick(r, cap, 16)


def _pair_add(mine, theirs, my_c, name):
    _, r, ccols = mine.shape
    tr = _row_tile(r, ccols, 1)

    def body(c_ref, a_ref, b_ref, o_ref):
        o_ref[...] = (a_ref[...].astype(F32) + b_ref[...].astype(F32)).astype(o_ref.dtype)

    spec = pl.BlockSpec((None, tr, ccols), lambda q, i, c_ref: (q, i, 0))
    return pl.pallas_call(
        body, name=name,
        grid_spec=pltpu.PrefetchScalarGridSpec(
            num_scalar_prefetch=1, grid=(N_CHIP, r // tr),
            in_specs=[pl.BlockSpec((None, tr, ccols), lambda q, i, c_ref: (2 * q + c_ref[0], i, 0)), spec],
            out_specs=spec),
        out_shape=jax.ShapeDtypeStruct(theirs.shape, theirs.dtype),
        compiler_params=_params("parallel", "parallel"),
    )(my_c, mine, theirs)


def _sum_blocks(parts, name):
    nb, r, ccols = parts.shape
    tr = _row_tile(r, ccols, nb)

    def body(p_ref, o_ref):
        acc = p_ref[0].astype(F32)
        for i in range(1, nb):
            acc = acc + p_ref[i].astype(F32)
        o_ref[...] = acc

    return pl.pallas_call(
        body, name=name, grid=(r // tr,),
        in_specs=[pl.BlockSpec((nb, tr, ccols), lambda i: (0, i, 0))],
        out_specs=pl.BlockSpec((tr, ccols), lambda i: (i, 0)),
        out_shape=jax.ShapeDtypeStruct((r, ccols), F32),
        compiler_params=_params("parallel"),
    )(parts)


def _gather_wire(shards, wire_dtypes):
    return tuple(_allgather([s.astype(d) for s, d in zip(shards, wire_dtypes)], "weights_allgather"))


@functools.partial(jax.custom_vjp, nondiff_argnums=(1,))
def fsdp_gather(shards, wire_dtypes):
    return _gather_wire(shards, wire_dtypes)


def _fsdp_gather_fwd(shards, wire_dtypes):
    return _gather_wire(shards, wire_dtypes), None


def _fsdp_gather_bwd(wire_dtypes, _, cts):
    chip_parts = _exchange_chips(_reduce_scatter_head(cts, "grads"), "grads_exchange_chips")
    return (_reduce_scatter_tail(chip_parts, "grads"),)


fsdp_gather.defvjp(_fsdp_gather_fwd, _fsdp_gather_bwd)


@jax.custom_vjp
def replicated(vec):
    return vec


def _replicated_fwd(vec):
    return vec, None


def _replicated_bwd(_, ct):
    return (_sum_blocks(_allgather([ct], "small_grad_allgather")[0], "small_grad_sum"),)


replicated.defvjp(_replicated_fwd, _replicated_bwd)


def _adamw(w, g, m, v, name):
    rows, cols = w.shape
    tr = _pick(rows, 256, 8) if rows % 8 == 0 else rows

    def body(w_ref, g_ref, m_ref, v_ref, d_ref, nm_ref, nv_ref):
        g_ = g_ref[...]
        m_ = ADAM_B1 * m_ref[...] + (1.0 - ADAM_B1) * g_
        v_ = ADAM_B2 * v_ref[...] + (1.0 - ADAM_B2) * jnp.square(g_)
        m_hat = m_ / (1.0 - ADAM_B1 ** ADAM_STEP)
        v_hat = v_ / (1.0 - ADAM_B2 ** ADAM_STEP)
        d_ref[...] = -ADAM_LR * (m_hat / (jnp.sqrt(v_hat) + ADAM_EPS) + ADAM_WD * w_ref[...])
        nm_ref[...] = m_
        nv_ref[...] = v_

    spec = pl.BlockSpec((tr, cols), lambda i: (i, 0))
    return pl.pallas_call(
        body, name=name, grid=(rows // tr,), in_specs=[spec] * 4, out_specs=[spec] * 3,
        out_shape=[jax.ShapeDtypeStruct(w.shape, F32)] * 3, compiler_params=_params("parallel"),
    )(w, g, m, v)


COL_SHARDED = ("w_in", "w_uq", "w_ukv", "w_branch_a", "w_branch_b", "w_up", "w_ple")
EARLY = ("w_in",)
MID = ("w_uq", "w_ukv", "w_branch_a", "w_branch_b", "w_out")
LATE = ("w_up", "w_down", "w_ple_gate", "w_ple")
SMALL = ("attn_pre_norm", "attn_post_norm", "b_gate", "q_a_norm", "kv_a_norm", "mlp_pre_norm", "mlp_post_norm",
         "conv_b", "ple_norm", "sinks")
SMALL_COLS = 128


def _pack_rows(arrays, cols, row_mult):
    flat = jnp.concatenate([a.reshape(-1) for a in arrays])
    pad = (-flat.shape[0]) % (cols * row_mult)
    return jnp.pad(flat, (0, pad)).reshape(-1, cols)


def _unpack_small(vec, shapes):
    flat = vec.reshape(-1)
    out, off = {}, 0
    for name in SMALL:
        n = shapes[name]
        out[name] = flat[off:off + n].reshape(1, n)
        off += n + (-n) % SMALL_COLS
    return out


def _pad_lanes(t, width):
    return jnp.pad(t, [(0, 0)] * (t.ndim - 1) + [(0, width - t.shape[-1])])


def _pad_rows(t, rows):
    return jnp.pad(t, [(0, 0)] * (t.ndim - 2) + [(0, rows - t.shape[-2]), (0, 0)])


FRONT_SIZES = (512, 128, 128, 256, 128)
FRONT_BOUNDS = (0, 512, 640, 768, 1024, 1152, 1280)
PE_LANE = NOPE_DIM


def _arrange_w_in_t(wt):
    k = wt.shape[1]
    n_front = sum(FRONT_SIZES)
    front, kr, gates = wt[:n_front], wt[n_front:n_front + ROPE_DIM], wt[n_front + ROPE_DIM:]
    kr_slab = jnp.concatenate([jnp.zeros((PE_LANE, k), wt.dtype), kr,
                               jnp.zeros((HEAD_PAD - PE_LANE - ROPE_DIM, k), wt.dtype)], axis=0)
    return jnp.concatenate([front, kr_slab], axis=0), gates


def _arrange_w_uq_t(wt):
    k = wt.shape[1]
    return _pad_rows(wt.reshape(B_HEADS, NOPE_DIM + ROPE_DIM, k), HEAD_PAD).reshape(B_HEADS * HEAD_PAD, k)


def _arrange_w_ukv_t(wt):
    k = wt.shape[1]
    w = wt.reshape(B_HEADS, 2, NOPE_DIM, k)
    slabs = [_pad_rows(w[:, part], HEAD_PAD).reshape(B_HEADS * HEAD_PAD, k) for part in range(2)]
    return jnp.concatenate(slabs, axis=0)


def _rope_tables(positions, s):
    pos = positions.reshape(s, 1).astype(F32)

    def angles(dim):
        return pos * ROPE_THETA ** (-(jnp.arange(0, dim, 2, dtype=F32) / dim))

    cos_a, sin_a = jnp.cos(angles(A_HEAD_DIM)), jnp.sin(angles(A_HEAD_DIM))
    zero_a = jnp.zeros_like(sin_a)
    tables_a = [jnp.tile(jnp.concatenate(pair, axis=1), (1, LANES // A_HEAD_DIM))
                for pair in ((cos_a, cos_a), (-sin_a, zero_a), (zero_a, sin_a))]
    cos_b, sin_b = jnp.cos(angles(ROPE_DIM)), jnp.sin(angles(ROPE_DIM))
    zero_b = jnp.zeros_like(sin_b)

    def slab(first, second, fill):
        return jnp.concatenate([jnp.full((s, PE_LANE), fill, F32), first, second,
                                jnp.full((s, HEAD_PAD - PE_LANE - ROPE_DIM), fill, F32)], axis=1)

    tables_b = [slab(cos_b, cos_b, 1.0), slab(-sin_b, zero_b, 0.0), slab(zero_b, sin_b, 0.0)]
    return tables_a + tables_b


def _local_loss(wts, x, p, tables, target):
    s = x.shape[0]
    small_shapes = {n: wts[n].shape[-1] for n in SMALL}
    small_vec = _pack_rows([_pad_lanes(wts[n].reshape(1, -1), small_shapes[n] + (-small_shapes[n]) % SMALL_COLS)
                            for n in SMALL], SMALL_COLS, 8)
    sm = _unpack_small(replicated(small_vec), small_shapes)
    def shard(n):
        return wts[n].T if n in COL_SHARDED else wts[n]

    gathered = fsdp_gather(tuple([shard(n) for n in EARLY] + [_pack_rows([wts["conv_w"]], SMALL_COLS, 8)]),
                           (BF16,) * len(EARLY) + (F32,))
    big = {n: g.reshape(-1, g.shape[2]) for n, g in zip(EARLY, gathered)}
    ch = wts["conv_w"].shape[1]
    conv_w = gathered[-1].reshape(N_DEV, -1)[:, :CONV_W * ch].reshape(N_DEV, CONV_W, ch)
    conv_w = conv_w.transpose(1, 0, 2).reshape(CONV_W, N_DEV * ch)

    w_front_t, w_gates_t = _arrange_w_in_t(big["w_in"])
    tables_a, tables_b = tables[:3], tables[3:]

    (h1,) = stage("prenorm", _f_prenorm, [x], [sm["attn_pre_norm"]], out_dtypes=[BF16])
    qa, ka, va, cqn, ckvn, kpe = proj_stage(
        "prep", _f_prep, [(h1, w_front_t, "nt", "w_front", True)], params=[sm["q_a_norm"], sm["kv_a_norm"]],
        consts=tables, splits=[FRONT_BOUNDS], out_dtypes=[BF16, BF16, BF16, BF16, BF16, F32])
    ya, mid = swa_nat(qa, ka, va, sm["sinks"].reshape(-1), tuple(shard(n) for n in MID))
    big.update({n: g.reshape(-1, g.shape[2]) for n, g in zip(MID, mid)})

    (q2,) = proj_stage("qrope", _f_qrope, [(cqn, _arrange_w_uq_t(big["w_uq"]), "nt", "w_uq", True)],
                       consts=tables_b, out_dtypes=[BF16])
    k2, v2 = proj_stage("kv", _f_kv, [(ckvn, _arrange_w_ukv_t(big["w_ukv"]), "nt", "w_ukv", True)], extra=[kpe],
                        splits=[(0, B_HEADS * HEAD_PAD, 2 * B_HEADS * HEAD_PAD), None], out_dtypes=[BF16, BF16])
    yb, late = flash_nat(q2, k2, v2, tuple(shard(n) for n in LATE))
    big.update({n: g.reshape(-1, g.shape[2]) for n, g in zip(LATE, late)})

    (mixed,) = proj_stage(
        "gate", _f_gate, [(h1, w_gates_t, "nt", "w_gates", True), (ya, big["w_branch_a"], "nt", "w_branch_a", True),
                          (yb, big["w_branch_b"], "nt", "w_branch_b", True)],
        params=[sm["b_gate"][:, :D_MODEL], sm["b_gate"][:, D_MODEL:]],
        splits=[(0, D_MODEL, 2 * D_MODEL), None, None], out_dtypes=[BF16])
    x1, h2 = proj_stage("post_attn", _f_post, [(mixed, big["w_out"], "nn", "w_out", True)], extra=[x],
                        params=[sm["attn_post_norm"], sm["mlp_pre_norm"]], out_dtypes=[F32, BF16])

    act = mlp_up(h2, big["w_up"], conv_w, sm["conv_b"])
    x2, h3 = proj_stage("post_mlp", _f_post, [(act, big["w_down"], "nn", "w_down", True)], extra=[x1],
                        params=[sm["mlp_post_norm"], sm["ple_norm"]], out_dtypes=[F32, BF16])

    (rowloss,) = proj_stage("loss", _f_out, [(h3, big["w_ple_gate"], "nn", "w_ple_gate", True),
                                             (p, big["w_ple"], "nt", "w_ple", False)], extra=[x2], consts=[target])
    return jnp.sum(rowloss)


WEIGHTS = ["attn_pre_norm", "attn_post_norm", "w_in", "b_gate", "sinks", "q_a_norm", "w_uq", "kv_a_norm", "w_ukv",
           "w_branch_a", "w_branch_b", "w_out", "mlp_pre_norm", "mlp_post_norm", "w_up", "conv_w", "conv_b",
           "w_down", "ple_norm", "w_ple_gate", "w_ple"]


def kernel(x, p, positions, attn_pre_norm, attn_post_norm, w_in, b_gate, sinks, q_a_norm, w_uq, kv_a_norm, w_ukv, w_branch_a, w_branch_b, w_out, mlp_pre_norm, mlp_post_norm, w_up, conv_w, conv_b, w_down, ple_norm, w_ple_gate, w_ple, loss_target, m_attn_pre_norm, m_attn_post_norm, m_w_in, m_b_gate, m_sinks, m_q_a_norm, m_w_uq, m_kv_a_norm, m_w_ukv, m_w_branch_a, m_w_branch_b, m_w_out, m_mlp_pre_norm, m_mlp_post_norm, m_w_up, m_conv_w, m_conv_b, m_w_down, m_ple_norm, m_w_ple_gate, m_w_ple, v_attn_pre_norm, v_attn_post_norm, v_w_in, v_b_gate, v_sinks, v_q_a_norm, v_w_uq, v_kv_a_norm, v_w_ukv, v_w_branch_a, v_w_branch_b, v_w_out, v_mlp_pre_norm, v_mlp_post_norm, v_w_up, v_conv_w, v_conv_b, v_w_down, v_ple_norm, v_w_ple_gate, v_w_ple):
    given = dict(locals())
    s = x.shape[1]
    wts = {n: given[n][0] if given[n].ndim == 3 else given[n] for n in WEIGHTS}
    tables = _rope_tables(positions, s)
    local_loss, (grads, grad_x) = jax.value_and_grad(_local_loss, argnums=(0, 1))(
        wts, x[0], p[0, 0], tables, loss_target[0])
    loss = lax.psum(local_loss, AXES)

    outs = {"grad": [], "delta": [], "m": [], "v": []}
    for n in WEIGHTS:
        shape = given[n].shape
        w2 = wts[n].reshape(-1, shape[-1])
        g2 = grads[n].reshape(w2.shape)
        delta, new_m, new_v = _adamw(w2, g2, given["m_" + n].reshape(w2.shape), given["v_" + n].reshape(w2.shape),
                                     "adamw_" + n)
        outs["grad"].append(g2.reshape(shape))
        outs["delta"].append(delta.reshape(shape))
        outs["m"].append(new_m.reshape(shape))
        outs["v"].append(new_v.reshape(shape))
    return (loss, grad_x[None], *outs["grad"], *outs["delta"], *outs["m"], *outs["v"])
```

```python
import functools

import numpy as np
import jax
import jax.numpy as jnp
from jax import lax
from jax.experimental import pallas as pl
from jax.experimental.pallas import tpu as pltpu

F32 = jnp.float32
BF16 = jnp.bfloat16
MESH_ID = pl.DeviceIdType.MESH
AXES = ("x", "y", "c")
N_DEV = 8

D_MODEL = 1024
RMS_EPS = 1e-6
ROPE_THETA = 10000.0
SWA_BLOCK = 128
A_HEADS, A_KV_HEADS, A_HEAD_DIM = 8, 2, 64
A_GROUP = A_HEADS // A_KV_HEADS
B_HEADS, Q_LORA, KV_LORA, NOPE_DIM, ROPE_DIM, V_DIM = 8, 256, 128, 64, 32, 64
D_FF = 2816
CONV_W = 3
HEAD_PAD = 128

ADAM_LR, ADAM_B1, ADAM_B2, ADAM_EPS, ADAM_WD, ADAM_STEP = 0.001, 0.9, 0.999, 1e-08, 0.01, 10

VMEM_LIMIT = 48 * 1024 * 1024
MM_TM, MM_TN, MM_TK_TOKENS = 512, 1408, 1024
MM_VMEM_BUDGET = 36 * 1024 * 1024
FLASH_T = 1024
CONV_TS = 128
CONV_CHUNK = 256


def _params(*sem):
    return pltpu.CompilerParams(dimension_semantics=sem, vmem_limit_bytes=VMEM_LIMIT)


def _pick(dim, cap, mult):
    best = None
    for t in range(mult, min(dim, cap) + 1, mult):
        if dim % t == 0:
            best = t
    return dim if best is None else best


def _divisors(dim, mult):
    return [t for t in range(mult, dim + 1, mult) if dim % t == 0] or [dim]


def _matmul_tiles(m, n, kdim, form, sizes):
    sa, sb, so = sizes
    tk = _pick(kdim, MM_TK_TOKENS, 128) if form == "tn" else kdim
    cap_m = MM_TN if form == "tn" else MM_TM
    best = None
    for tm in _divisors(m, 128):
        for tn in _divisors(n, 128):
            need = 2 * (tm * tk * sa + tk * tn * sb + tm * tn * so) + (tm * tn * 4 if tk != kdim else 0)
            if tm > cap_m or tn > MM_TN or need > MM_VMEM_BUDGET:
                continue
            if best is None or (tm * tn, tm) > (best[0] * best[1], best[0]):
                best = (tm, tn)
    return best[0], best[1], tk


def _matmul(a, b, form, *, out_dtype=F32, name):
    if form == "tn":
        (kdim, m), n = a.shape, b.shape[1]
    else:
        (m, kdim), n = a.shape, (b.shape[1] if form == "nn" else b.shape[0])
    sizes = (a.dtype.itemsize, b.dtype.itemsize, jnp.dtype(out_dtype).itemsize)
    tm, tn, tk = _matmul_tiles(m, n, kdim, form, sizes)
    nk = kdim // tk
    rows_outer = nk > 1 or (m // tm) * b.size * sizes[1] <= (n // tn) * a.size * sizes[0]

    def ij(fn):
        return (lambda i, j, k: fn(i, j, k)) if rows_outer else (lambda j, i, k: fn(i, j, k))

    a_spec = (pl.BlockSpec((tk, tm), ij(lambda i, j, k: (k, i))) if form == "tn"
              else pl.BlockSpec((tm, tk), ij(lambda i, j, k: (i, k))))
    b_spec = (pl.BlockSpec((tn, tk), ij(lambda i, j, k: (j, k))) if form == "nt"
              else pl.BlockSpec((tk, tn), ij(lambda i, j, k: (k, j))))
    dims = (((0 if form == "tn" else 1,), (1 if form == "nt" else 0,)), ((), ()))

    def product(a_ref, b_ref):
        return lax.dot_general(a_ref[...].astype(BF16), b_ref[...].astype(BF16), dims, preferred_element_type=F32)

    if nk == 1:
        def body(a_ref, b_ref, o_ref):
            o_ref[...] = product(a_ref, b_ref).astype(o_ref.dtype)

        scratch = []
    else:
        def body(a_ref, b_ref, o_ref, acc_ref):
            k = pl.program_id(2)

            @pl.when(k == 0)
            def _():
                acc_ref[...] = jnp.zeros_like(acc_ref)

            acc_ref[...] += product(a_ref, b_ref)

            @pl.when(k == nk - 1)
            def _():
                o_ref[...] = acc_ref[...].astype(o_ref.dtype)

        scratch = [pltpu.VMEM((tm, tn), F32)]

    return pl.pallas_call(
        body, name=name, grid=(m // tm, n // tn, nk) if rows_outer else (n // tn, m // tm, nk),
        in_specs=[a_spec, b_spec],
        out_specs=pl.BlockSpec((tm, tn), ij(lambda i, j, k: (i, j))),
        out_shape=jax.ShapeDtypeStruct((m, n), out_dtype),
        scratch_shapes=scratch,
        compiler_params=_params("parallel", "parallel", "arbitrary"),
    )(a, b)


def _pairs(bounds):
    return list(zip(bounds[:-1], bounds[1:]))


def _split(v, bounds):
    return [v[:, a:b] for a, b in _pairs(bounds)]


def _stage_build(name, f, tiled, params, consts, splits, ts, out_dtypes, ct_dtypes=None):
    n_t, n_p, n_c = len(tiled), len(params), len(consts)
    ct_dtypes = [t.dtype for t in tiled] if ct_dtypes is None else ct_dtypes
    s = tiled[0].shape[0]
    ts = min(ts, s)
    grid = (s // ts,)
    if splits is None:
        splits = [None] * n_t
    in_bounds = [(0, t.shape[1]) if b is None else tuple(b) for t, b in zip(tiled, splits)]

    def tile_aval(arr):
        return jax.ShapeDtypeStruct((ts, arr.shape[1]), arr.dtype)

    slab_avals = [[jax.ShapeDtypeStruct((ts, e - a), t.dtype) for a, e in _pairs(b)]
                  for t, b in zip(tiled, in_bounds)]
    out_avals = jax.eval_shape(f, slab_avals, list(params), [tile_aval(c) for c in consts])
    out_bounds = [tuple(np.cumsum([0] + [o.shape[1] for o in slabs]).tolist()) for slabs in out_avals]
    out_dtypes = [F32] * len(out_bounds) if out_dtypes is None else out_dtypes
    out_shapes = [jax.ShapeDtypeStruct((s, b[-1]), d) for b, d in zip(out_bounds, out_dtypes)]

    def row_spec(width):
        return pl.BlockSpec((ts, width), lambda i: (i, 0))

    def par_spec(arr):
        return pl.BlockSpec(arr.shape, lambda i: (0, 0))

    in_specs = ([row_spec(t.shape[1]) for t in tiled] + [par_spec(p) for p in params]
                + [row_spec(c.shape[1]) for c in consts])

    def load(refs):
        t = [_split(r[...], b) for r, b in zip(refs[:n_t], in_bounds)]
        p = [r[...] for r in refs[n_t:n_t + n_p]]
        c = [r[...] for r in refs[n_t + n_p:n_t + n_p + n_c]]
        return t, p, c

    def store(refs, values, bounds):
        for ref, slabs, b in zip(refs, values, bounds):
            for v, (a, e) in zip(slabs, _pairs(b)):
                ref[:, a:e] = v.astype(ref.dtype)

    def run_fwd(tiled, params, consts):
        def body(*refs):
            t, p, c = load(refs)
            store(refs[n_t + n_p + n_c:], f(t, p, c), out_bounds)

        return pl.pallas_call(
            body, name=name + "_fwd", grid=grid, in_specs=in_specs,
            out_specs=[row_spec(b[-1]) for b in out_bounds], out_shape=out_shapes,
            compiler_params=_params("parallel"),
        )(*tiled, *params, *consts)

    def run_bwd(tiled, params, consts, cts):
        n_in = n_t + n_p + n_c
        n_o = len(out_bounds)

        def body(*refs):
            t, p, c = load(refs)
            g = [_split(r[...].astype(F32), b) for r, b in zip(refs[n_in:n_in + n_o], out_bounds)]
            _, pull = jax.vjp(lambda t_, p_: f(t_, p_, c), t, p)
            dt, dp = pull(g)
            store(refs[n_in + n_o:n_in + n_o + n_t], dt, in_bounds)
            first = pl.program_id(0) == 0
            for ref, d in zip(refs[n_in + n_o + n_t:], dp):
                @pl.when(first)
                def _(ref=ref):
                    ref[...] = jnp.zeros_like(ref)

                ref[...] += d

        res = pl.pallas_call(
            body, name=name + "_bwd", grid=grid,
            in_specs=in_specs + [row_spec(b[-1]) for b in out_bounds],
            out_specs=[row_spec(t.shape[1]) for t in tiled] + [par_spec(p) for p in params],
            out_shape=[jax.ShapeDtypeStruct(t.shape, d) for t, d in zip(tiled, ct_dtypes)]
                      + [jax.ShapeDtypeStruct(p.shape, F32) for p in params],
            compiler_params=_params("arbitrary"),
        )(*tiled, *params, *consts, *cts)
        return tuple(res[:n_t]), tuple(res[n_t:])

    return run_fwd, run_bwd


def stage(name, f, tiled, params=(), consts=(), splits=None, ts=256, out_dtypes=None):
    tiled, params, consts = tuple(tiled), tuple(params), tuple(consts)
    run_fwd, run_bwd = _stage_build(name, f, tiled, params, consts, splits, ts, out_dtypes)

    @jax.custom_vjp
    def op(tiled, params, consts):
        return tuple(run_fwd(tiled, params, consts))

    def op_fwd(tiled, params, consts):
        return op(tiled, params, consts), (tiled, params, consts)

    def op_bwd(res, cts):
        tiled, params, consts = res
        dt, dp = run_bwd(tiled, params, consts, cts)
        return dt, dp, tuple(jnp.zeros_like(c) for c in consts)

    op.defvjp(op_fwd, op_bwd)
    return op(tiled, params, consts)


def proj_stage(name, f, projections, extra=(), params=(), consts=(), splits=None, ts=256, out_dtypes=None):
    n_z = len(projections)
    forms = [pr[2] for pr in projections]
    names = [pr[3] for pr in projections]
    need_da = [pr[4] for pr in projections]
    extra, params, consts = tuple(extra), tuple(params), tuple(consts)

    def matmuls(a_list, w_list):
        return tuple(_matmul(a, w, form, out_dtype=F32, name=n + "_fwd")
                     for a, w, form, n in zip(a_list, w_list, forms, names))

    def build(zs, ct=False):
        ct_dtypes = [BF16] * n_z + [e.dtype for e in extra] if ct else None
        return _stage_build(name, f, tuple(zs) + extra, params, consts, splits, ts, out_dtypes, ct_dtypes)

    @jax.custom_vjp
    def op(a_list, w_list, extra, params, consts):
        zs = matmuls(a_list, w_list)
        return tuple(build(zs)[0](zs + extra, params, consts))

    def op_fwd(a_list, w_list, extra, params, consts):
        zs = matmuls(a_list, w_list)
        return tuple(build(zs)[0](zs + extra, params, consts)), (a_list, w_list, zs, extra, params, consts)

    def op_bwd(res, cts):
        a_list, w_list, zs, extra, params, consts = res
        dt, dp = build(zs, ct=True)[1](zs + extra, params, consts, cts)
        da_list, dw_list = [], []
        for a, w, dz, form, n, want in zip(a_list, w_list, dt[:n_z], forms, names, need_da):
            if form == "nn":
                da = _matmul(dz, w, "nt", out_dtype=a.dtype, name=n + "_da") if want else jnp.zeros_like(a)
                dw = _matmul(a, dz, "tn", out_dtype=w.dtype, name=n + "_dw")
            else:
                da = _matmul(dz, w, "nn", out_dtype=a.dtype, name=n + "_da") if want else jnp.zeros_like(a)
                dw = _matmul(dz, a, "tn", out_dtype=w.dtype, name=n + "_dw")
            da_list.append(da)
            dw_list.append(dw)
        return tuple(da_list), tuple(dw_list), tuple(dt[n_z:]), dp, tuple(jnp.zeros_like(c) for c in consts)

    op.defvjp(op_fwd, op_bwd)
    return op(tuple(pr[0] for pr in projections), tuple(pr[1] for pr in projections), extra, params, consts)


def _rms(t, g):
    return t * lax.rsqrt(jnp.mean(t * t, axis=-1, keepdims=True) + RMS_EPS) * g


@functools.partial(jax.custom_vjp, nondiff_argnums=(1,))
def _lane_roll(t, shift):
    return pltpu.roll(t, shift % t.shape[-1], t.ndim - 1)


def _lane_roll_fwd(t, shift):
    return _lane_roll(t, shift), None


def _lane_roll_bwd(shift, _, ct):
    return (pltpu.roll(ct, (-shift) % ct.shape[-1], ct.ndim - 1),)


_lane_roll.defvjp(_lane_roll_fwd, _lane_roll_bwd)


def _rope_lanes(t, tables, half):
    reps = t.shape[1] // tables[0].shape[1]
    c, s_lo, s_hi = [jnp.concatenate([tb] * reps, axis=1) if reps > 1 else tb for tb in tables]
    return t * c + _lane_roll(t, -half) * s_lo + _lane_roll(t, half) * s_hi


def _f_prenorm(t, p, c):
    return [[_rms(t[0][0], p[0])]]


def _f_prep(t, p, c):
    qa, ka, va, cq, ckv, kr = t[0]
    return [[_rope_lanes(qa, c[0:3], A_HEAD_DIM // 2)], [_rope_lanes(ka, c[0:3], A_HEAD_DIM // 2)], [va],
            [_rms(cq, p[0])], [_rms(ckv, p[1])], [_rope_lanes(kr, c[3:6], ROPE_DIM // 2)]]


def _f_qrope(t, p, c):
    return [[_rope_lanes(t[0][0], c, ROPE_DIM // 2)]]


def _f_kv(t, p, c):
    (k_nope, v), (k_pe,) = t
    return [[k_nope + jnp.concatenate([k_pe] * B_HEADS, axis=1)], [v]]


def _f_gate(t, p, c):
    (ga, gb), (pa,), (pb,) = t
    ba, bb = p
    return [[jax.nn.sigmoid(ga + ba) * pa + jax.nn.sigmoid(gb + bb) * pb]]


def _f_post(t, p, c):
    (branch,), (residual,) = t
    x1 = residual + _rms(branch, p[0])
    return [[x1], [_rms(x1, p[1])]]


def _f_out(t, p, c):
    (gate,), (emb,), (x2,) = t
    y = x2 + jax.nn.sigmoid(gate) * emb
    err = y - c[0]
    return [[0.5 * jnp.mean(err * err, axis=-1, keepdims=True)]]


def _shift_down(cur, prev, has_prev):
    full = jnp.concatenate([prev * has_prev, cur], axis=0)
    return pltpu.roll(full, 1, 0)[HALO:], pltpu.roll(full, 2, 0)[HALO:]


GELU_C = float(np.sqrt(2.0 / np.pi))
GELU_A = 0.044715
HALO = 8


def _gelu_tanh(x):
    x2 = x * x
    th = jnp.tanh(x * (GELU_C + (GELU_C * GELU_A) * x2))
    half = 0.5 + 0.5 * th
    return x * half, half + x * (0.5 - 0.5 * (th * th)) * (GELU_C + (3.0 * GELU_C * GELU_A) * x2)


def _row_sum(t):
    return jnp.sum(t, axis=0, keepdims=True)


def _conv3(cur, prev, w_ref, b_ref, has_prev):
    u1, u2 = _shift_down(cur, prev, has_prev)
    return w_ref[2:3, :] * cur + w_ref[1:2, :] * u1 + w_ref[0:1, :] * u2 + b_ref[...], u1, u2


def _mlp_act_specs(s):
    ts = min(CONV_TS, s)
    hb = ts // HALO

    def half_specs(h):
        return [pl.BlockSpec((ts, D_FF), lambda i: (i, h)),
                pl.BlockSpec((HALO, D_FF), lambda i: (jnp.maximum(i * hb - 1, 0), h))]

    def par_specs(h):
        return [pl.BlockSpec((CONV_W, D_FF), lambda i: (0, h)), pl.BlockSpec((1, D_FF), lambda i: (0, h))]

    return ts, hb, half_specs, par_specs


def _mlp_act_fwd_call(up, conv_w, conv_b):
    s = up.shape[0]
    ts, hb, half_specs, par_specs = _mlp_act_specs(s)

    def body(g_ref, gp_ref, v_ref, vp_ref, wg_ref, bg_ref, wv_ref, bv_ref, o_ref):
        has_prev = (pl.program_id(0) > 0).astype(F32)

        def chunk(cidx, carry):
            cols = pl.ds(pl.multiple_of(cidx * CONV_CHUNK, CONV_CHUNK), CONV_CHUNK)
            u_g, _, _ = _conv3(g_ref[:, cols], gp_ref[:, cols], wg_ref.at[:, cols], bg_ref.at[:, cols], has_prev)
            u_v, _, _ = _conv3(v_ref[:, cols], vp_ref[:, cols], wv_ref.at[:, cols], bv_ref.at[:, cols], has_prev)
            o_ref[:, cols] = (_gelu_tanh(u_g)[0] * u_v).astype(o_ref.dtype)
            return carry

        lax.fori_loop(0, D_FF // CONV_CHUNK, chunk, 0)

    return pl.pallas_call(
        body, name="mlp_act_fwd", grid=(s // ts,),
        in_specs=half_specs(0) + half_specs(1) + par_specs(0) + par_specs(1),
        out_specs=pl.BlockSpec((ts, D_FF), lambda i: (i, 0)),
        out_shape=jax.ShapeDtypeStruct((s, D_FF), BF16),
        compiler_params=_params("parallel"),
    )(up, up, up, up, conv_w, conv_b, conv_w, conv_b)


def _mlp_act_bwd_call(up, conv_w, conv_b, dact):
    s = up.shape[0]
    ts, hb, half_specs, par_specs = _mlp_act_specs(s)
    nt = s // ts
    ext = ts + HALO
    bf16_rows = 2 * HALO

    def next_spec(rows, h):
        return pl.BlockSpec((rows, D_FF), lambda i: (jnp.minimum((i + 1) * (ts // rows), s // rows - 1), h))

    def body(g_ref, gp_ref, gn_ref, v_ref, vp_ref, vn_ref, wg_ref, bg_ref, wv_ref, bv_ref, da_ref, dan_ref,
             dup_ref, dwg_ref, dbg_ref, dwv_ref, dbv_ref):
        i = pl.program_id(0)
        has_prev, has_next = (i > 0).astype(F32), (i < nt - 1).astype(F32)

        @pl.when(i == 0)
        def _():
            for ref in (dwg_ref, dbg_ref, dwv_ref, dbv_ref):
                ref[...] = jnp.zeros_like(ref)

        def chunk(cidx, carry):
            cols = pl.ds(pl.multiple_of(cidx * CONV_CHUNK, CONV_CHUNK), CONV_CHUNK)
            g_ext = jnp.concatenate([g_ref[:, cols], gn_ref[:, cols]], axis=0)
            v_ext = jnp.concatenate([v_ref[:, cols], vn_ref[:, cols]], axis=0)
            u_g, g1, g2 = _conv3(g_ext, gp_ref[:, cols], wg_ref.at[:, cols], bg_ref.at[:, cols], has_prev)
            u_v, v1, v2 = _conv3(v_ext, vp_ref[:, cols], wv_ref.at[:, cols], bv_ref.at[:, cols], has_prev)
            da_ext = jnp.concatenate([da_ref[:, cols].astype(F32),
                                      dan_ref[:, cols].astype(F32)[0:HALO] * has_next], axis=0)
            act_g, dact_g = _gelu_tanh(u_g)
            du_g = da_ext * u_v * dact_g
            du_v = da_ext * act_g
            for du, w_ref, x0, x1, x2, dw_ref, db_ref, lo in ((du_g, wg_ref, g_ext, g1, g2, dwg_ref, dbg_ref, 0),
                                                          (du_v, wv_ref, v_ext, v1, v2, dwv_ref, dbv_ref, D_FF)):
                d1 = pltpu.roll(du, ext - 1, 0)
                d2 = pltpu.roll(du, ext - 2, 0)
                dup = w_ref[2:3, cols] * du + w_ref[1:2, cols] * d1 + w_ref[0:1, cols] * d2
                out_cols = pl.ds(pl.multiple_of(lo + cidx * CONV_CHUNK, CONV_CHUNK), CONV_CHUNK)
                dup_ref[:, out_cols] = dup[0:ts].astype(dup_ref.dtype)
                own = du[0:ts]
                dw_ref[0:1, cols] += _row_sum(own * x2[0:ts])
                dw_ref[1:2, cols] += _row_sum(own * x1[0:ts])
                dw_ref[2:3, cols] += _row_sum(own * x0[0:ts])
                db_ref[:, cols] += _row_sum(own)
            return carry

        lax.fori_loop(0, D_FF // CONV_CHUNK, chunk, 0)

    par_out = [pl.BlockSpec((CONV_W, D_FF), lambda i: (0, 0)), pl.BlockSpec((1, D_FF), lambda i: (0, 0))]
    par_shapes = [jax.ShapeDtypeStruct((CONV_W, D_FF), F32), jax.ShapeDtypeStruct((1, D_FF), F32)]
    return pl.pallas_call(
        body, name="mlp_act_bwd", grid=(nt,),
        in_specs=(half_specs(0) + [next_spec(HALO, 0)] + half_specs(1) + [next_spec(HALO, 1)]
                  + par_specs(0) + par_specs(1)
                  + [pl.BlockSpec((ts, D_FF), lambda i: (i, 0)), next_spec(bf16_rows, 0)]),
        out_specs=[pl.BlockSpec((ts, 2 * D_FF), lambda i: (i, 0))] + par_out + par_out,
        out_shape=[jax.ShapeDtypeStruct((s, 2 * D_FF), BF16)] + par_shapes + par_shapes,
        compiler_params=_params("arbitrary"),
    )(up, up, up, up, up, up, conv_w, conv_b, conv_w, conv_b, dact, dact)


@jax.custom_vjp
def mlp_up(h2, w_up_t, conv_w, conv_b):
    return _mlp_act_fwd_call(_matmul(h2, w_up_t, "nt", out_dtype=F32, name="w_up_fwd"), conv_w, conv_b)


def _mlp_up_fwd(h2, w_up_t, conv_w, conv_b):
    up = _matmul(h2, w_up_t, "nt", out_dtype=F32, name="w_up_fwd")
    return _mlp_act_fwd_call(up, conv_w, conv_b), (h2, w_up_t, up, conv_w, conv_b)


def _mlp_up_bwd(res, dact):
    h2, w_up_t, up, conv_w, conv_b = res
    dup, dwg, dbg, dwv, dbv = _mlp_act_bwd_call(up, conv_w, conv_b, dact)
    dh2 = _matmul(dup, w_up_t, "nn", out_dtype=h2.dtype, name="w_up_da")
    dw = _matmul(dup, h2, "tn", out_dtype=w_up_t.dtype, name="w_up_dw")
    return dh2, dw, jnp.concatenate([dwg, dwv], axis=1), jnp.concatenate([dbg, dbv], axis=1)


mlp_up.defvjp(_mlp_up_fwd, _mlp_up_bwd)


SWA_ROWS = A_GROUP * SWA_BLOCK


def _swa_sink_rows(sink_ref, g):
    return jnp.concatenate([jnp.full((SWA_BLOCK, 1), sink_ref[g * A_GROUP + h], F32) for h in range(A_GROUP)], axis=0)


def _swa_operands(q_ref, kp_ref, kc_ref, vp_ref, vc_ref, sink_ref):
    groups = []
    for g in range(A_KV_HEADS):
        groups.append((_swa_stack_heads(q_ref, g), _dup_half(kp_ref[...], g), _dup_half(kc_ref[...], g),
                       _dup_half(vp_ref[...], g), _dup_half(vc_ref[...], g)))
    return groups, jnp.concatenate([_swa_sink_rows(sink_ref, g) for g in range(A_KV_HEADS)], axis=0)


def _swa_probs(groups, sink, prev_off):
    scale = A_HEAD_DIM ** -0.5
    sp = jnp.concatenate([lax.dot_general(gr[0], gr[1], NT_DIMS, preferred_element_type=F32) for gr in groups], axis=0)
    sc = jnp.concatenate([lax.dot_general(gr[0], gr[2], NT_DIMS, preferred_element_type=F32) for gr in groups], axis=0)
    qi = lax.broadcasted_iota(jnp.int32, sp.shape, 0) & (SWA_BLOCK - 1)
    kj = lax.broadcasted_iota(jnp.int32, sp.shape, 1)
    in_cur = kj <= qi
    sw = jnp.where(in_cur, sc, jnp.where(kj > qi + prev_off, sp, -jnp.inf)) * scale
    m = jnp.maximum(jnp.max(sw, axis=-1, keepdims=True), sink)
    e, es = jnp.exp(sw - m), jnp.exp(sink - m)
    den = jnp.sum(e, axis=-1, keepdims=True) + es
    return e / den, in_cur, es / den


def _swa_split(t, in_cur):
    cur = jnp.where(in_cur, t, 0.0)
    return t - cur, cur


MLA_SCALE = (NOPE_DIM + ROPE_DIM) ** -0.5
EXP2_SCALE = MLA_SCALE * float(np.log2(np.e))
NT_DIMS = (((1,), (1,)), ((), ()))
TN_DIMS = (((0,), (0,)), ((), ()))


LANES = 128
HALF = LANES // 2


def _low_half(shape):
    return lax.broadcasted_iota(jnp.int32, shape, len(shape) - 1) < HALF


def _dup_half(x, g):
    xf = x.astype(F32)
    keep = _low_half(xf.shape) if g == 0 else jnp.logical_not(_low_half(xf.shape))
    xm = jnp.where(keep, xf, 0.0)
    return (xm + pltpu.roll(xm, HALF, 1)).astype(x.dtype)


def _fold_half(r, g):
    total = r + pltpu.roll(r, HALF, 1)
    keep = _low_half(r.shape) if g == 0 else jnp.logical_not(_low_half(r.shape))
    return jnp.where(keep, total, 0.0)


def _swa_stack_heads(ref, g):
    parts = []
    for tile in range(2):
        slab = ref[:, (2 * g + tile) * LANES:(2 * g + tile + 1) * LANES]
        low = _low_half(slab.shape)
        parts += [jnp.where(low, slab, jnp.zeros_like(slab)), jnp.where(low, jnp.zeros_like(slab), slab)]
    return jnp.concatenate(parts, axis=0)


def _swa_unstack_heads(ref, g, rows):
    for tile in range(2):
        a = rows[(2 * tile) * SWA_BLOCK:(2 * tile + 1) * SWA_BLOCK]
        b = rows[(2 * tile + 1) * SWA_BLOCK:(2 * tile + 2) * SWA_BLOCK]
        ref[:, (2 * g + tile) * LANES:(2 * g + tile + 1) * LANES] = jnp.where(_low_half(a.shape), a, b).astype(ref.dtype)


def _swa_nat_specs():
    blk = SWA_BLOCK
    q_spec = pl.BlockSpec((blk, A_HEADS * A_HEAD_DIM), lambda n: (n, 0))
    prev_spec = pl.BlockSpec((blk, LANES), lambda n: (jnp.maximum(n - 1, 0), 0))
    cur_spec = pl.BlockSpec((blk, LANES), lambda n: (n, 0))
    return q_spec, prev_spec, cur_spec, pl.BlockSpec(memory_space=pltpu.SMEM)


def _swa_nat_fwd_call(q, k, v, sinks, shards):
    s = q.shape[0]
    nblk = s // SWA_BLOCK
    n_arr = len(shards)
    q_spec, prev_spec, cur_spec, sink_spec = _swa_nat_specs()

    def body(*refs):
        q_ref, kp_ref, kc_ref, vp_ref, vc_ref, sink_ref = refs[:6]
        o_ref = refs[6 + n_arr]
        n = pl.program_id(0)
        ag_start, ag_forward, ag_finish = _allgather_phases(refs[6:6 + n_arr], refs[7 + n_arr:7 + 2 * n_arr],
                                                            *refs[7 + 2 * n_arr:])

        @pl.when(n == 0)
        def _():
            ag_start()

        @pl.when(n == nblk // 2)
        def _():
            ag_forward()

        prev_off = jnp.where(n > 0, 0, SWA_BLOCK)
        groups, sink = _swa_operands(q_ref, kp_ref, kc_ref, vp_ref, vc_ref, sink_ref)
        p, in_cur, _ = _swa_probs(groups, sink, prev_off)
        ppb, pcb = [t.astype(BF16) for t in _swa_split(p, in_cur)]
        for g, (_, _, _, vp, vc) in enumerate(groups):
            rows = slice(g * SWA_ROWS, (g + 1) * SWA_ROWS)
            out = (jnp.dot(ppb[rows], vp, preferred_element_type=F32)
                   + jnp.dot(pcb[rows], vc, preferred_element_type=F32))
            _swa_unstack_heads(o_ref, g, out)

        @pl.when(n == nblk - 1)
        def _():
            ag_finish()

    return pl.pallas_call(
        body, name="swa_fwd", grid=(nblk,),
        in_specs=[q_spec, prev_spec, cur_spec, prev_spec, cur_spec, sink_spec] + [HBM_SPEC] * n_arr,
        out_specs=[q_spec] + [HBM_SPEC] * n_arr,
        out_shape=[jax.ShapeDtypeStruct(q.shape, BF16)] + _allgather_out_shapes(shards),
        scratch_shapes=_allgather_sems(n_arr),
        compiler_params=_params("arbitrary"),
    )(q, k, k, v, v, sinks, *shards)


def _swa_nat_bwd_call(q, k, v, sinks, do, parts):
    s = q.shape[0]
    nblk = s // SWA_BLOCK
    n_arr = len(parts)
    q_spec, prev_spec, cur_spec, sink_spec = _swa_nat_specs()
    scale = A_HEAD_DIM ** -0.5
    dsink_spec = pl.BlockSpec((A_KV_HEADS, SWA_ROWS, 1), lambda n: (0, 0, 0))

    def body(*refs):
        q_ref, kp_ref, kc_ref, vp_ref, vc_ref, sink_ref, do_ref = refs[:7]
        dq_ref, dkp_ref, dkc_ref, dvp_ref, dvc_ref, dsink_ref = refs[7 + n_arr:13 + n_arr]
        n = pl.program_id(0)
        exchange_start, exchange_finish = _exchange_chips_phases(
            refs[7:7 + n_arr], refs[13 + n_arr:13 + 2 * n_arr], *refs[13 + 2 * n_arr:])

        @pl.when(n == 0)
        def _():
            exchange_start()
        prev_off = jnp.where(n > 0, 0, SWA_BLOCK)

        @pl.when(n == 0)
        def _():
            dsink_ref[...] = jnp.zeros_like(dsink_ref)

        groups, sink = _swa_operands(q_ref, kp_ref, kc_ref, vp_ref, vc_ref, sink_ref)
        dobs = [_swa_stack_heads(do_ref, g) for g in range(A_KV_HEADS)]
        p, in_cur, ps = _swa_probs(groups, sink, prev_off)
        ppb, pcb = [t.astype(BF16) for t in _swa_split(p, in_cur)]

        def per_group(fn):
            return jnp.concatenate([fn(g, slice(g * SWA_ROWS, (g + 1) * SWA_ROWS)) for g in range(A_KV_HEADS)], axis=0)

        out = per_group(lambda g, rows: jnp.dot(ppb[rows], groups[g][3], preferred_element_type=F32)
                        + jnp.dot(pcb[rows], groups[g][4], preferred_element_type=F32))
        delta = jnp.sum(jnp.concatenate(dobs, axis=0).astype(F32) * out, axis=-1, keepdims=True)
        dp = jnp.where(in_cur,
                       per_group(lambda g, rows: lax.dot_general(dobs[g], groups[g][4], NT_DIMS,
                                                                 preferred_element_type=F32)),
                       per_group(lambda g, rows: lax.dot_general(dobs[g], groups[g][3], NT_DIMS,
                                                                 preferred_element_type=F32)))
        dsp, dsc = [t.astype(BF16) for t in _swa_split(p * (dp - delta), in_cur)]
        dsink_ref[...] += (-ps * delta).reshape(dsink_ref.shape)
        totals = [jnp.zeros((SWA_BLOCK, LANES), F32) for _ in range(4)]
        for g, (qb, kp, kc, _, _) in enumerate(groups):
            rows = slice(g * SWA_ROWS, (g + 1) * SWA_ROWS)
            dq = (jnp.dot(dsp[rows], kp, preferred_element_type=F32)
                  + jnp.dot(dsc[rows], kc, preferred_element_type=F32)) * scale
            _swa_unstack_heads(dq_ref, g, dq)
            pieces = [lax.dot_general(dsp[rows], qb, TN_DIMS, preferred_element_type=F32) * scale,
                      lax.dot_general(dsc[rows], qb, TN_DIMS, preferred_element_type=F32) * scale,
                      lax.dot_general(ppb[rows], dobs[g], TN_DIMS, preferred_element_type=F32),
                      lax.dot_general(pcb[rows], dobs[g], TN_DIMS, preferred_element_type=F32)]
            totals = [tot + _fold_half(r, g) for tot, r in zip(totals, pieces)]
        dkp_ref[...], dkc_ref[...], dvp_ref[...], dvc_ref[...] = totals

        @pl.when(n == nblk - 1)
        def _():
            exchange_finish()

    kv_shape = jax.ShapeDtypeStruct(k.shape, F32)
    return pl.pallas_call(
        body, name="swa_bwd", grid=(nblk,),
        in_specs=[q_spec, prev_spec, cur_spec, prev_spec, cur_spec, sink_spec, q_spec] + [HBM_SPEC] * n_arr,
        out_specs=[q_spec, cur_spec, cur_spec, cur_spec, cur_spec, dsink_spec] + [HBM_SPEC] * n_arr,
        out_shape=[jax.ShapeDtypeStruct(q.shape, q.dtype), kv_shape, kv_shape, kv_shape, kv_shape,
                   jax.ShapeDtypeStruct((A_KV_HEADS, SWA_ROWS, 1), F32)]
                  + [jax.ShapeDtypeStruct(p.shape, p.dtype) for p in parts],
        scratch_shapes=_exchange_chips_sems(n_arr),
        compiler_params=_params("arbitrary"),
    )(q, k, k, v, v, sinks, do, *parts)


@jax.custom_vjp
def swa_nat(q, k, v, sinks, shards):
    out = _swa_nat_fwd_call(q, k, v, sinks, [s.astype(BF16) for s in shards])
    return out[0], tuple(out[1:])


def _swa_nat_fwd(q, k, v, sinks, shards):
    out = _swa_nat_fwd_call(q, k, v, sinks, [s.astype(BF16) for s in shards])
    return (out[0], tuple(out[1:])), (q, k, v, sinks)


def _swa_nat_bwd(res, cts):
    q, k, v, sinks = res
    do, d_gathered = cts
    out = _swa_nat_bwd_call(q, k, v, sinks, do, _reduce_scatter_head(d_gathered, "mid_grads"))
    dq, dkp, dkc, dvp, dvc, dsink = out[:6]

    def fold(prev_part, cur_part):
        shifted = jnp.concatenate([prev_part[SWA_BLOCK:], jnp.zeros_like(prev_part[:SWA_BLOCK])], axis=0)
        return (cur_part + shifted).astype(k.dtype)

    dsinks = jnp.sum(dsink.reshape(A_HEADS, SWA_BLOCK), axis=1)
    return dq, fold(dkp, dkc), fold(dvp, dvc), dsinks, _reduce_scatter_tail(out[6:], "mid_grads")


swa_nat.defvjp(_swa_nat_fwd, _swa_nat_bwd)

N_PAIR = B_HEADS // 2


def _flash_nat_fwd_call(q, k, v, shards):
    s = q.shape[0]
    t = min(FLASH_T, s)
    nb = s // t
    d = LANES
    n_arr = len(shards)

    def body(*refs):
        q_ref, k_ref, v_ref = refs[:3]
        shard_refs = refs[3:3 + n_arr]
        o_ref, lse_ref = refs[3 + n_arr:5 + n_arr]
        gathered_refs = refs[5 + n_arr:5 + 2 * n_arr]
        vt_ref, m_ref, l_ref, acc_ref = refs[5 + 2 * n_arr:9 + 2 * n_arr]
        pair, i = pl.program_id(0), pl.program_id(1)
        ag_start, ag_forward, ag_finish = _allgather_phases(shard_refs, gathered_refs, *refs[9 + 2 * n_arr:])

        @pl.when((pair == 0) & (i == 0))
        def _():
            ag_start()

        @pl.when((pair == N_PAIR // 2) & (i == 0))
        def _():
            ag_forward()

        @pl.when(i == 0)
        def _():
            for hh in range(2):
                for chunk in range(nb):
                    rows = slice(chunk * t, (chunk + 1) * t)
                    vt_ref[hh, :, rows] = v_ref[rows, hh * d:(hh + 1) * d].T

        m_ref[...] = jnp.full_like(m_ref, -jnp.inf)
        l_ref[...] = jnp.zeros_like(l_ref)
        acc_ref[...] = jnp.zeros_like(acc_ref)

        def step(j, on_diagonal):
            keys = pl.ds(pl.multiple_of(j * t, t), t)
            for hh in range(2):
                sc_t = lax.dot_general(k_ref[keys, hh * d:(hh + 1) * d], q_ref[:, hh * d:(hh + 1) * d], NT_DIMS,
                                       preferred_element_type=F32)
                if on_diagonal:
                    key = lax.broadcasted_iota(jnp.int32, (t, t), 0)
                    qry = lax.broadcasted_iota(jnp.int32, (t, t), 1)
                    sc_t = jnp.where(qry >= key, sc_t, -jnp.inf)
                m_old = m_ref[hh]
                m_new = jnp.maximum(m_old, jnp.max(sc_t, axis=0, keepdims=True))
                alpha = jnp.exp2((m_old - m_new) * EXP2_SCALE)
                p_t = jnp.exp2((sc_t - m_new) * EXP2_SCALE)
                l_ref[hh] = alpha * l_ref[hh] + jnp.sum(p_t, axis=0, keepdims=True)
                acc_ref[hh] = alpha * acc_ref[hh] + jnp.dot(vt_ref[hh, :, keys], p_t.astype(BF16),
                                                            preferred_element_type=F32)
                m_ref[hh] = m_new

        def below(j, carry):
            step(j, False)
            return carry

        lax.fori_loop(0, i, below, 0)
        step(i, True)
        outs = [(acc_ref[hh] / l_ref[hh]).T for hh in range(2)]
        for hh in range(2):
            lse_ref[hh] = m_ref[hh] * EXP2_SCALE + jnp.log2(l_ref[hh])
        o_ref[...] = (outs[0] + pltpu.roll(outs[1], HALF, 1)).astype(o_ref.dtype)

        @pl.when((pair == N_PAIR - 1) & (i == nb - 1))
        def _():
            ag_finish()

    return pl.pallas_call(
        body, name="mla_fwd", grid=(N_PAIR, nb),
        in_specs=[pl.BlockSpec((t, 2 * d), lambda p, i: (i, p)),
                  pl.BlockSpec((s, 2 * d), lambda p, i: (0, p)),
                  pl.BlockSpec((s, 2 * d), lambda p, i: (0, p))] + [HBM_SPEC] * n_arr,
        out_specs=[pl.BlockSpec((t, d), lambda p, i: (i, p)),
                   pl.BlockSpec((2, 1, t), lambda p, i: (p, 0, i))] + [HBM_SPEC] * n_arr,
        out_shape=[jax.ShapeDtypeStruct((s, N_PAIR * d), BF16), jax.ShapeDtypeStruct((B_HEADS, 1, s), F32)]
                  + _allgather_out_shapes(shards),
        scratch_shapes=[pltpu.VMEM((2, d, s), BF16), pltpu.VMEM((2, 1, t), F32), pltpu.VMEM((2, 1, t), F32),
                        pltpu.VMEM((2, d, t), F32)] + _allgather_sems(n_arr),
        compiler_params=_params("arbitrary", "arbitrary"),
    )(q, k, v, *shards)


def _flash_nat_delta_call(o, do):
    s, w = o.shape
    t = min(FLASH_T, s)

    def body(o_ref, do_ref, out_ref):
        prod = o_ref[...].astype(F32) * do_ref[...].astype(F32)
        lane = lax.broadcasted_iota(jnp.int32, (w, LANES), 0) // V_DIM
        head = lax.broadcasted_iota(jnp.int32, (w, LANES), 1)
        out_ref[...] = jnp.dot(prod, (lane == head).astype(F32), precision=lax.Precision.HIGHEST,
                               preferred_element_type=F32)

    spec = pl.BlockSpec((t, w), lambda i: (i, 0))
    return pl.pallas_call(
        body, name="mla_delta", grid=(s // t,), in_specs=[spec, spec],
        out_specs=pl.BlockSpec((t, LANES), lambda i: (i, 0)),
        out_shape=jax.ShapeDtypeStruct((s, LANES), F32), compiler_params=_params("parallel"),
    )(o, do)


def _flash_nat_bwd_call(q, k, v, lse_row, delta_row, do, parts):
    s = q.shape[0]
    t = min(FLASH_T, s)
    nb = s // t
    d = LANES
    n_arr = len(parts)

    def body(*refs):
        q_ref, k_ref, v_ref, lse_ref, delta_ref, do_ref = refs[:6]
        part_refs = refs[6:6 + n_arr]
        dq_ref, dk_ref, dv_ref = refs[6 + n_arr:9 + n_arr]
        received_refs = refs[9 + n_arr:9 + 2 * n_arr]
        dq_acc, dk_acc, dv_acc = refs[9 + 2 * n_arr:12 + 2 * n_arr]
        pair, j = pl.program_id(0), pl.program_id(1)
        exchange_start, exchange_finish = _exchange_chips_phases(part_refs, received_refs, *refs[12 + 2 * n_arr:])

        @pl.when((pair == 0) & (j == 0))
        def _():
            exchange_start()

        @pl.when(j == 0)
        def _():
            dq_acc[...] = jnp.zeros_like(dq_acc)

        for hh in range(2):
            kb, vb = k_ref[:, hh * d:(hh + 1) * d], v_ref[:, hh * d:(hh + 1) * d]
            dk_acc[...] = jnp.zeros_like(dk_acc)
            dv_acc[...] = jnp.zeros_like(dv_acc)

            def step(i, on_diagonal, hh=hh, kb=kb, vb=vb):
                rows = pl.ds(pl.multiple_of(i * t, t), t)
                qb = q_ref[rows, hh * d:(hh + 1) * d]
                do_pair = do_ref[rows, :].astype(F32)
                do_h = do_pair if hh == 0 else pltpu.roll(do_pair, HALF, 1)
                dob = jnp.where(_low_half(do_h.shape), do_h, 0.0).astype(BF16)
                sc_t = lax.dot_general(kb, qb, NT_DIMS, preferred_element_type=F32)
                p_t = jnp.exp2(sc_t * EXP2_SCALE - lse_ref[hh, :, rows])
                if on_diagonal:
                    key = lax.broadcasted_iota(jnp.int32, (t, t), 0)
                    qry = lax.broadcasted_iota(jnp.int32, (t, t), 1)
                    p_t = jnp.where(qry >= key, p_t, 0.0)
                dp_t = lax.dot_general(vb, dob, NT_DIMS, preferred_element_type=F32)
                ds_t = (p_t * (dp_t - delta_ref[hh, :, rows])).astype(BF16)
                dv_acc[...] += jnp.dot(p_t.astype(BF16), dob, preferred_element_type=F32)
                dk_acc[...] += jnp.dot(ds_t, qb, preferred_element_type=F32)
                dq_acc[hh, rows, :] += lax.dot_general(ds_t, kb, TN_DIMS, preferred_element_type=F32)

            def above(i, carry, step=step):
                step(i, False)
                return carry

            step(j, True)
            lax.fori_loop(j + 1, nb, above, 0)
            dk_ref[:, hh * d:(hh + 1) * d] = (dk_acc[...] * MLA_SCALE).astype(dk_ref.dtype)
            dv_ref[:, hh * d:(hh + 1) * d] = dv_acc[...].astype(dv_ref.dtype)

        @pl.when(j == nb - 1)
        def _():
            for hh in range(2):
                dq_ref[:, hh * d:(hh + 1) * d] = (dq_acc[hh] * MLA_SCALE).astype(dq_ref.dtype)

        @pl.when((pair == N_PAIR - 1) & (j == nb - 1))
        def _():
            exchange_finish()

    full_spec = pl.BlockSpec((s, 2 * d), lambda p, j: (0, p))
    tile_spec = pl.BlockSpec((t, 2 * d), lambda p, j: (j, p))
    row_spec = pl.BlockSpec((2, 1, s), lambda p, j: (p, 0, 0))
    return pl.pallas_call(
        body, name="mla_bwd", grid=(N_PAIR, nb),
        in_specs=[full_spec, tile_spec, tile_spec, row_spec, row_spec, pl.BlockSpec((s, d), lambda p, j: (0, p))]
                 + [HBM_SPEC] * n_arr,
        out_specs=[full_spec, tile_spec, tile_spec] + [HBM_SPEC] * n_arr,
        out_shape=[jax.ShapeDtypeStruct(q.shape, q.dtype)] * 3 + [jax.ShapeDtypeStruct(p.shape, p.dtype) for p in parts],
        scratch_shapes=[pltpu.VMEM((2, s, d), F32), pltpu.VMEM((t, d), F32), pltpu.VMEM((t, d), F32)]
                       + _exchange_chips_sems(n_arr),
        compiler_params=_params("arbitrary", "arbitrary"),
    )(q, k, v, lse_row, delta_row, do, *parts)


def _reduce_scatter_head(cts, tag):
    received = _exchange_sibling(list(cts), tag + "_exchange_sibling")
    my_c = lax.axis_index("c").astype(jnp.int32).reshape(1)
    return [_pair_add(m, r, my_c, "%s_pair_add_%d" % (tag, i)) for i, (m, r) in enumerate(zip(cts, received))]


def _reduce_scatter_tail(chip_parts, tag):
    return tuple(_sum_blocks(r, "%s_sum_%d" % (tag, i)) for i, r in enumerate(chip_parts))


@jax.custom_vjp
def flash_nat(q, k, v, shards):
    out = _flash_nat_fwd_call(q, k, v, [s.astype(BF16) for s in shards])
    return out[0], tuple(out[2:])


def _flash_nat_fwd(q, k, v, shards):
    out = _flash_nat_fwd_call(q, k, v, [s.astype(BF16) for s in shards])
    return (out[0], tuple(out[2:])), (q, k, v, out[0], out[1])


def _flash_nat_bwd(res, cts):
    q, k, v, o, lse = res
    do, d_gathered = cts
    delta = _flash_nat_delta_call(o, do)[:, :B_HEADS].T.reshape(B_HEADS, 1, q.shape[0])
    out = _flash_nat_bwd_call(q, k, v, lse, delta, do, _reduce_scatter_head(d_gathered, "mlp_grads"))
    return out[0], out[1], out[2], _reduce_scatter_tail(out[3:], "mlp_grads")


flash_nat.defvjp(_flash_nat_fwd, _flash_nat_bwd)


HBM_SPEC = pl.BlockSpec(memory_space=pltpu.HBM)


def _allgather(shards, name):
    n_arr = len(shards)

    def body(*refs):
        start, forward, finish = _allgather_phases(refs[:n_arr], refs[n_arr:2 * n_arr], *refs[2 * n_arr:])
        start()
        forward()
        finish()

    return pl.pallas_call(
        body, name=name, out_shape=_allgather_out_shapes(shards),
        in_specs=[HBM_SPEC] * n_arr, out_specs=[HBM_SPEC] * n_arr,
        scratch_shapes=_allgather_sems(n_arr),
    )(*shards)


def _allgather_out_shapes(shards):
    return [jax.ShapeDtypeStruct((N_DEV,) + s.shape, s.dtype) for s in shards]


def _allgather_sems(n_arr):
    return [pltpu.SemaphoreType.DMA((7, n_arr)), pltpu.SemaphoreType.DMA((7, n_arr)), pltpu.SemaphoreType.DMA((n_arr,))]


def _allgather_phases(x_refs, out_refs, send_sems, recv_sems, local_sems):
    arrays = range(len(x_refs))
    x, y, c = lax.axis_index("x"), lax.axis_index("y"), lax.axis_index("c")
    me, sibling = (x, y, c), (x, y, 1 - c)
    chips = [(1 - x, y), (x, 1 - y), (1 - x, 1 - y)]

    def rows(a, px, py, pc):
        return out_refs[a].at[4 * px + 2 * py + pc]

    def copy(a, k, block, to, src=None):
        return pltpu.make_async_remote_copy(
            src_ref=rows(a, *block) if src is None else src, dst_ref=rows(a, *block),
            send_sem=send_sems.at[k, a], recv_sem=recv_sems.at[k, a], device_id=to, device_id_type=MESH_ID)

    def mine():
        return [pltpu.make_async_copy(x_refs[a], rows(a, *me), local_sems.at[a]) for a in arrays]

    def first():
        return [cp for a in arrays for cp in
                [copy(a, 0, me, sibling, src=x_refs[a])]
                + [copy(a, 1 + j, me, (*chip, c), src=x_refs[a]) for j, chip in enumerate(chips)]]

    def passed():
        return [copy(a, 4 + j, (*chip, c), sibling) for j, chip in enumerate(chips) for a in arrays]

    def start():
        for cp in mine() + first():
            cp.start()

    def forward():
        for j, chip in enumerate(chips):
            for a in arrays:
                copy(a, 1 + j, (*chip, c), me).wait_recv()
                copy(a, 4 + j, (*chip, c), sibling).start()

    def finish():
        for a in arrays:
            copy(a, 0, sibling, me).wait_recv()
        for j, chip in enumerate(chips):
            for a in arrays:
                copy(a, 4 + j, (*chip, 1 - c), me).wait_recv()
        for cp in first() + passed():
            cp.wait_send()
        for cp in mine():
            cp.wait()

    return start, forward, finish


N_CHIP = 4


def _exchange_sibling(parts, name):
    n_arr = len(parts)

    def body(*refs):
        in_refs, recv_refs = refs[:n_arr], refs[n_arr:2 * n_arr]
        send_sems, recv_sems = refs[2 * n_arr:]
        x, y, c = lax.axis_index("x"), lax.axis_index("y"), lax.axis_index("c")
        copies = []
        for a in range(n_arr):
            for q in range(N_CHIP):
                copies.append(pltpu.make_async_remote_copy(
                    src_ref=in_refs[a].at[2 * q + 1 - c], dst_ref=recv_refs[a].at[q],
                    send_sem=send_sems.at[q, a], recv_sem=recv_sems.at[q, a],
                    device_id=(x, y, 1 - c), device_id_type=MESH_ID))
        for cp in copies:
            cp.start()
        for cp in copies:
            cp.wait()

    return pl.pallas_call(
        body, name=name, out_shape=[jax.ShapeDtypeStruct((N_CHIP,) + p.shape[1:], p.dtype) for p in parts],
        in_specs=[HBM_SPEC] * n_arr, out_specs=[HBM_SPEC] * n_arr,
        scratch_shapes=[pltpu.SemaphoreType.DMA((N_CHIP, n_arr)), pltpu.SemaphoreType.DMA((N_CHIP, n_arr))],
    )(*parts)


def _exchange_chips(parts, name):
    n_arr = len(parts)

    def body(*refs):
        start, finish = _exchange_chips_phases(refs[:n_arr], refs[n_arr:2 * n_arr], *refs[2 * n_arr:])
        start()
        finish()

    return pl.pallas_call(
        body, name=name, out_shape=[jax.ShapeDtypeStruct(p.shape, p.dtype) for p in parts],
        in_specs=[HBM_SPEC] * n_arr, out_specs=[HBM_SPEC] * n_arr,
        scratch_shapes=_exchange_chips_sems(n_arr),
    )(*parts)


def _exchange_chips_sems(n_arr):
    return [pltpu.SemaphoreType.DMA((N_CHIP - 1, n_arr)), pltpu.SemaphoreType.DMA((N_CHIP - 1, n_arr)),
            pltpu.SemaphoreType.DMA((n_arr,))]


def _exchange_chips_phases(in_refs, out_refs, send_sems, recv_sems, local_sems):
    n_arr = len(in_refs)
    x, y, c = lax.axis_index("x"), lax.axis_index("y"), lax.axis_index("c")
    me = 2 * x + y

    def copies():
        out = [pltpu.make_async_copy(in_refs[a].at[me], out_refs[a].at[me], local_sems.at[a]) for a in range(n_arr)]
        for k in range(1, N_CHIP):
            px = 1 - x if k & 2 else x
            py = 1 - y if k & 1 else y
            for a in range(n_arr):
                out.append(pltpu.make_async_remote_copy(
                    src_ref=in_refs[a].at[2 * px + py], dst_ref=out_refs[a].at[me],
                    send_sem=send_sems.at[k - 1, a], recv_sem=recv_sems.at[k - 1, a],
                    device_id=(px, py, c), device_id_type=MESH_ID))
        return out

    def start():
        for cp in copies():
            cp.start()

    def finish():
        for cp in copies():
            cp.wait()

    return start, finish


def _row_tile(r, ccols, blocks):
    cap = max(16, (2 * 1024 * 1024) // (4 * ccols * blocks))
    return _pick(r, cap, 16)


def _pair_add(mine, theirs, my_c, name):
    _, r, ccols = mine.shape
    tr = _row_tile(r, ccols, 1)

    def body(c_ref, a_ref, b_ref, o_ref):
        o_ref[...] = (a_ref[...].astype(F32) + b_ref[...].astype(F32)).astype(o_ref.dtype)

    spec = pl.BlockSpec((None, tr, ccols), lambda q, i, c_ref: (q, i, 0))
    return pl.pallas_call(
        body, name=name,
        grid_spec=pltpu.PrefetchScalarGridSpec(
            num_scalar_prefetch=1, grid=(N_CHIP, r // tr),
            in_specs=[pl.BlockSpec((None, tr, ccols), lambda q, i, c_ref: (2 * q + c_ref[0], i, 0)), spec],
            out_specs=spec),
        out_shape=jax.ShapeDtypeStruct(theirs.shape, theirs.dtype),
        compiler_params=_params("parallel", "parallel"),
    )(my_c, mine, theirs)


def _sum_blocks(parts, name):
    nb, r, ccols = parts.shape
    tr = _row_tile(r, ccols, nb)

    def body(p_ref, o_ref):
        acc = p_ref[0].astype(F32)
        for i in range(1, nb):
            acc = acc + p_ref[i].astype(F32)
        o_ref[...] = acc

    return pl.pallas_call(
        body, name=name, grid=(r // tr,),
        in_specs=[pl.BlockSpec((nb, tr, ccols), lambda i: (0, i, 0))],
        out_specs=pl.BlockSpec((tr, ccols), lambda i: (i, 0)),
        out_shape=jax.ShapeDtypeStruct((r, ccols), F32),
        compiler_params=_params("parallel"),
    )(parts)


def _gather_wire(shards, wire_dtypes):
    return tuple(_allgather([s.astype(d) for s, d in zip(shards, wire_dtypes)], "weights_allgather"))


@functools.partial(jax.custom_vjp, nondiff_argnums=(1,))
def fsdp_gather(shards, wire_dtypes):
    return _gather_wire(shards, wire_dtypes)


def _fsdp_gather_fwd(shards, wire_dtypes):
    return _gather_wire(shards, wire_dtypes), None


def _fsdp_gather_bwd(wire_dtypes, _, cts):
    chip_parts = _exchange_chips(_reduce_scatter_head(cts, "grads"), "grads_exchange_chips")
    return (_reduce_scatter_tail(chip_parts, "grads"),)


fsdp_gather.defvjp(_fsdp_gather_fwd, _fsdp_gather_bwd)


@jax.custom_vjp
def replicated(vec):
    return vec


def _replicated_fwd(vec):
    return vec, None


def _replicated_bwd(_, ct):
    return (_sum_blocks(_allgather([ct], "small_grad_allgather")[0], "small_grad_sum"),)


replicated.defvjp(_replicated_fwd, _replicated_bwd)


def _adamw(w, g, m, v, name):
    rows, cols = w.shape
    tr = _pick(rows, 256, 8) if rows % 8 == 0 else rows

    def body(w_ref, g_ref, m_ref, v_ref, d_ref, nm_ref, nv_ref):
        g_ = g_ref[...]
        m_ = ADAM_B1 * m_ref[...] + (1.0 - ADAM_B1) * g_
        v_ = ADAM_B2 * v_ref[...] + (1.0 - ADAM_B2) * jnp.square(g_)
        m_hat = m_ / (1.0 - ADAM_B1 ** ADAM_STEP)
        v_hat = v_ / (1.0 - ADAM_B2 ** ADAM_STEP)
        d_ref[...] = -ADAM_LR * (m_hat / (jnp.sqrt(v_hat) + ADAM_EPS) + ADAM_WD * w_ref[...])
        nm_ref[...] = m_
        nv_ref[...] = v_

    spec = pl.BlockSpec((tr, cols), lambda i: (i, 0))
    return pl.pallas_call(
        body, name=name, grid=(rows // tr,), in_specs=[spec] * 4, out_specs=[spec] * 3,
        out_shape=[jax.ShapeDtypeStruct(w.shape, F32)] * 3, compiler_params=_params("parallel"),
    )(w, g, m, v)


COL_SHARDED = ("w_in", "w_uq", "w_ukv", "w_branch_a", "w_branch_b", "w_up", "w_ple")
EARLY = ("w_in",)
MID = ("w_uq", "w_ukv", "w_branch_a", "w_branch_b", "w_out")
LATE = ("w_up", "w_down", "w_ple_gate", "w_ple")
SMALL = ("attn_pre_norm", "attn_post_norm", "b_gate", "q_a_norm", "kv_a_norm", "mlp_pre_norm", "mlp_post_norm",
         "conv_b", "ple_norm", "sinks")
SMALL_COLS = 128


def _pack_rows(arrays, cols, row_mult):
    flat = jnp.concatenate([a.reshape(-1) for a in arrays])
    pad = (-flat.shape[0]) % (cols * row_mult)
    return jnp.pad(flat, (0, pad)).reshape(-1, cols)


def _unpack_small(vec, shapes):
    flat = vec.reshape(-1)
    out, off = {}, 0
    for name in SMALL:
        n = shapes[name]
        out[name] = flat[off:off + n].reshape(1, n)
        off += n + (-n) % SMALL_COLS
    return out


def _pad_lanes(t, width):
    return jnp.pad(t, [(0, 0)] * (t.ndim - 1) + [(0, width - t.shape[-1])])


def _pad_rows(t, rows):
    return jnp.pad(t, [(0, 0)] * (t.ndim - 2) + [(0, rows - t.shape[-2]), (0, 0)])


FRONT_SIZES = (512, 128, 128, 256, 128)
FRONT_BOUNDS = (0, 512, 640, 768, 1024, 1152, 1280)
PE_LANE = NOPE_DIM


def _arrange_w_in_t(wt):
    k = wt.shape[1]
    n_front = sum(FRONT_SIZES)
    front, kr, gates = wt[:n_front], wt[n_front:n_front + ROPE_DIM], wt[n_front + ROPE_DIM:]
    kr_slab = jnp.concatenate([jnp.zeros((PE_LANE, k), wt.dtype), kr,
                               jnp.zeros((HEAD_PAD - PE_LANE - ROPE_DIM, k), wt.dtype)], axis=0)
    return jnp.concatenate([front, kr_slab], axis=0), gates


def _arrange_w_uq_t(wt):
    k = wt.shape[1]
    return _pad_rows(wt.reshape(B_HEADS, NOPE_DIM + ROPE_DIM, k), HEAD_PAD).reshape(B_HEADS * HEAD_PAD, k)


def _arrange_w_ukv_t(wt):
    k = wt.shape[1]
    w = wt.reshape(B_HEADS, 2, NOPE_DIM, k)
    slabs = [_pad_rows(w[:, part], HEAD_PAD).reshape(B_HEADS * HEAD_PAD, k) for part in range(2)]
    return jnp.concatenate(slabs, axis=0)


def _rope_tables(positions, s):
    pos = positions.reshape(s, 1).astype(F32)

    def angles(dim):
        return pos * ROPE_THETA ** (-(jnp.arange(0, dim, 2, dtype=F32) / dim))

    cos_a, sin_a = jnp.cos(angles(A_HEAD_DIM)), jnp.sin(angles(A_HEAD_DIM))
    zero_a = jnp.zeros_like(sin_a)
    tables_a = [jnp.tile(jnp.concatenate(pair, axis=1), (1, LANES // A_HEAD_DIM))
                for pair in ((cos_a, cos_a), (-sin_a, zero_a), (zero_a, sin_a))]
    cos_b, sin_b = jnp.cos(angles(ROPE_DIM)), jnp.sin(angles(ROPE_DIM))
    zero_b = jnp.zeros_like(sin_b)

    def slab(first, second, fill):
        return jnp.concatenate([jnp.full((s, PE_LANE), fill, F32), first, second,
                                jnp.full((s, HEAD_PAD - PE_LANE - ROPE_DIM), fill, F32)], axis=1)

    tables_b = [slab(cos_b, cos_b, 1.0), slab(-sin_b, zero_b, 0.0), slab(zero_b, sin_b, 0.0)]
    return tables_a + tables_b


def _local_loss(wts, x, p, tables, target):
    s = x.shape[0]
    small_shapes = {n: wts[n].shape[-1] for n in SMALL}
    small_vec = _pack_rows([_pad_lanes(wts[n].reshape(1, -1), small_shapes[n] + (-small_shapes[n]) % SMALL_COLS)
                            for n in SMALL], SMALL_COLS, 8)
    sm = _unpack_small(replicated(small_vec), small_shapes)
    def shard(n):
        return wts[n].T if n in COL_SHARDED else wts[n]

    gathered = fsdp_gather(tuple([shard(n) for n in EARLY] + [_pack_rows([wts["conv_w"]], SMALL_COLS, 8)]),
                           (BF16,) * len(EARLY) + (F32,))
    big = {n: g.reshape(-1, g.shape[2]) for n, g in zip(EARLY, gathered)}
    ch = wts["conv_w"].shape[1]
    conv_w = gathered[-1].reshape(N_DEV, -1)[:, :CONV_W * ch].reshape(N_DEV, CONV_W, ch)
    conv_w = conv_w.transpose(1, 0, 2).reshape(CONV_W, N_DEV * ch)

    w_front_t, w_gates_t = _arrange_w_in_t(big["w_in"])
    tables_a, tables_b = tables[:3], tables[3:]

    (h1,) = stage("prenorm", _f_prenorm, [x], [sm["attn_pre_norm"]], out_dtypes=[BF16])
    qa, ka, va, cqn, ckvn, kpe = proj_stage(
        "prep", _f_prep, [(h1, w_front_t, "nt", "w_front", True)], params=[sm["q_a_norm"], sm["kv_a_norm"]],
        consts=tables, splits=[FRONT_BOUNDS], out_dtypes=[BF16, BF16, BF16, BF16, BF16, F32])
    ya, mid = swa_nat(qa, ka, va, sm["sinks"].reshape(-1), tuple(shard(n) for n in MID))
    big.update({n: g.reshape(-1, g.shape[2]) for n, g in zip(MID, mid)})

    (q2,) = proj_stage("qrope", _f_qrope, [(cqn, _arrange_w_uq_t(big["w_uq"]), "nt", "w_uq", True)],
                       consts=tables_b, out_dtypes=[BF16])
    k2, v2 = proj_stage("kv", _f_kv, [(ckvn, _arrange_w_ukv_t(big["w_ukv"]), "nt", "w_ukv", True)], extra=[kpe],
                        splits=[(0, B_HEADS * HEAD_PAD, 2 * B_HEADS * HEAD_PAD), None], out_dtypes=[BF16, BF16])
    yb, late = flash_nat(q2, k2, v2, tuple(shard(n) for n in LATE))
    big.update({n: g.reshape(-1, g.shape[2]) for n, g in zip(LATE, late)})

    (mixed,) = proj_stage(
        "gate", _f_gate, [(h1, w_gates_t, "nt", "w_gates", True), (ya, big["w_branch_a"], "nt", "w_branch_a", True),
                          (yb, big["w_branch_b"], "nt", "w_branch_b", True)],
        params=[sm["b_gate"][:, :D_MODEL], sm["b_gate"][:, D_MODEL:]],
        splits=[(0, D_MODEL, 2 * D_MODEL), None, None], out_dtypes=[BF16])
    x1, h2 = proj_stage("post_attn", _f_post, [(mixed, big["w_out"], "nn", "w_out", True)], extra=[x],
                        params=[sm["attn_post_norm"], sm["mlp_pre_norm"]], out_dtypes=[F32, BF16])

    act = mlp_up(h2, big["w_up"], conv_w, sm["conv_b"])
    x2, h3 = proj_stage("post_mlp", _f_post, [(act, big["w_down"], "nn", "w_down", True)], extra=[x1],
                        params=[sm["mlp_post_norm"], sm["ple_norm"]], out_dtypes=[F32, BF16])

    (rowloss,) = proj_stage("loss", _f_out, [(h3, big["w_ple_gate"], "nn", "w_ple_gate", True),
                                             (p, big["w_ple"], "nt", "w_ple", False)], extra=[x2], consts=[target])
    return jnp.sum(rowloss)


WEIGHTS = ["attn_pre_norm", "attn_post_norm", "w_in", "b_gate", "sinks", "q_a_norm", "w_uq", "kv_a_norm", "w_ukv",
           "w_branch_a", "w_branch_b", "w_out", "mlp_pre_norm", "mlp_post_norm", "w_up", "conv_w", "conv_b",
           "w_down", "ple_norm", "w_ple_gate", "w_ple"]


def kernel(x, p, positions, attn_pre_norm, attn_post_norm, w_in, b_gate, sinks, q_a_norm, w_uq, kv_a_norm, w_ukv, w_branch_a, w_branch_b, w_out, mlp_pre_norm, mlp_post_norm, w_up, conv_w, conv_b, w_down, ple_norm, w_ple_gate, w_ple, loss_target, m_attn_pre_norm, m_attn_post_norm, m_w_in, m_b_gate, m_sinks, m_q_a_norm, m_w_uq, m_kv_a_norm, m_w_ukv, m_w_branch_a, m_w_branch_b, m_w_out, m_mlp_pre_norm, m_mlp_post_norm, m_w_up, m_conv_w, m_conv_b, m_w_down, m_ple_norm, m_w_ple_gate, m_w_ple, v_attn_pre_norm, v_attn_post_norm, v_w_in, v_b_gate, v_sinks, v_q_a_norm, v_w_uq, v_kv_a_norm, v_w_ukv, v_w_branch_a, v_w_branch_b, v_w_out, v_mlp_pre_norm, v_mlp_post_norm, v_w_up, v_conv_w, v_conv_b, v_w_down, v_ple_norm, v_w_ple_gate, v_w_ple):
    given = dict(locals())
    s = x.shape[1]
    wts = {n: given[n][0] if given[n].ndim == 3 else given[n] for n in WEIGHTS}
    tables = _rope_tables(positions, s)
    local_loss, (grads, grad_x) = jax.value_and_grad(_local_loss, argnums=(0, 1))(
        wts, x[0], p[0, 0], tables, loss_target[0])
    loss = lax.psum(local_loss, AXES)

    outs = {"grad": [], "delta": [], "m": [], "v": []}
    for n in WEIGHTS:
        shape = given[n].shape
        w2 = wts[n].reshape(-1, shape[-1])
        g2 = grads[n].reshape(w2.shape)
        delta, new_m, new_v = _adamw(w2, g2, given["m_" + n].reshape(w2.shape), given["v_" + n].reshape(w2.shape),
                                     "adamw_" + n)
        outs["grad"].append(g2.reshape(shape))
        outs["delta"].append(delta.reshape(shape))
        outs["m"].append(new_m.reshape(shape))
        outs["v"].append(new_v.reshape(shape))
    return (loss, grad_x[None], *outs["grad"], *outs["delta"], *outs["m"], *outs["v"])
```

```python
import functools

import numpy as np
import jax
import jax.numpy as jnp
from jax import lax
from jax.experimental import pallas as pl
from jax.experimental.pallas import tpu as pltpu

F32 = jnp.float32
BF16 = jnp.bfloat16
MESH_ID = pl.DeviceIdType.MESH
AXES = ("x", "y", "c")
N_DEV = 8

D_MODEL = 1024
RMS_EPS = 1e-6
ROPE_THETA = 10000.0
SWA_BLOCK = 128
A_HEADS, A_KV_HEADS, A_HEAD_DIM = 8, 2, 64
A_GROUP = A_HEADS // A_KV_HEADS
B_HEADS, Q_LORA, KV_LORA, NOPE_DIM, ROPE_DIM, V_DIM = 8, 256, 128, 64, 32, 64
D_FF = 2816
CONV_W = 3
HEAD_PAD = 128

ADAM_LR, ADAM_B1, ADAM_B2, ADAM_EPS, ADAM_WD, ADAM_STEP = 0.001, 0.9, 0.999, 1e-08, 0.01, 10

VMEM_LIMIT = 48 * 1024 * 1024
MM_TM, MM_TN, MM_TK_TOKENS = 512, 1408, 1024
MM_VMEM_BUDGET = 36 * 1024 * 1024
FLASH_T = 1024
CONV_TS = 128
CONV_CHUNK = 256


def _params(*sem):
    return pltpu.CompilerParams(dimension_semantics=sem, vmem_limit_bytes=VMEM_LIMIT)


def _pick(dim, cap, mult):
    best = None
    for t in range(mult, min(dim, cap) + 1, mult):
        if dim % t == 0:
            best = t
    return dim if best is None else best


def _divisors(dim, mult):
    return [t for t in range(mult, dim + 1, mult) if dim % t == 0] or [dim]


def _matmul_tiles(m, n, kdim, form, sizes):
    sa, sb, so = sizes
    tk = _pick(kdim, MM_TK_TOKENS, 128) if form == "tn" else kdim
    cap_m = MM_TN if form == "tn" else MM_TM
    best = None
    for tm in _divisors(m, 128):
        for tn in _divisors(n, 128):
            need = 2 * (tm * tk * sa + tk * tn * sb + tm * tn * so) + (tm * tn * 4 if tk != kdim else 0)
            if tm > cap_m or tn > MM_TN or need > MM_VMEM_BUDGET:
                continue
            if best is None or (tm * tn, tm) > (best[0] * best[1], best[0]):
                best = (tm, tn)
    return best[0], best[1], tk


def _matmul(a, b, form, *, out_dtype=F32, name):
    if form == "tn":
        (kdim, m), n = a.shape, b.shape[1]
    else:
        (m, kdim), n = a.shape, (b.shape[1] if form == "nn" else b.shape[0])
    sizes = (a.dtype.itemsize, b.dtype.itemsize, jnp.dtype(out_dtype).itemsize)
    tm, tn, tk = _matmul_tiles(m, n, kdim, form, sizes)
    nk = kdim // tk
    rows_outer = nk > 1 or (m // tm) * b.size * sizes[1] <= (n // tn) * a.size * sizes[0]

    def ij(fn):
        return (lambda i, j, k: fn(i, j, k)) if rows_outer else (lambda j, i, k: fn(i, j, k))

    a_spec = (pl.BlockSpec((tk, tm), ij(lambda i, j, k: (k, i))) if form == "tn"
              else pl.BlockSpec((tm, tk), ij(lambda i, j, k: (i, k))))
    b_spec = (pl.BlockSpec((tn, tk), ij(lambda i, j, k: (j, k))) if form == "nt"
              else pl.BlockSpec((tk, tn), ij(lambda i, j, k: (k, j))))
    dims = (((0 if form == "tn" else 1,), (1 if form == "nt" else 0,)), ((), ()))

    def product(a_ref, b_ref):
        return lax.dot_general(a_ref[...].astype(BF16), b_ref[...].astype(BF16), dims, preferred_element_type=F32)

    if nk == 1:
        def body(a_ref, b_ref, o_ref):
            o_ref[...] = product(a_ref, b_ref).astype(o_ref.dtype)

        scratch = []
    else:
        def body(a_ref, b_ref, o_ref, acc_ref):
            k = pl.program_id(2)

            @pl.when(k == 0)
            def _():
                acc_ref[...] = jnp.zeros_like(acc_ref)

            acc_ref[...] += product(a_ref, b_ref)

            @pl.when(k == nk - 1)
            def _():
                o_ref[...] = acc_ref[...].astype(o_ref.dtype)

        scratch = [pltpu.VMEM((tm, tn), F32)]

    return pl.pallas_call(
        body, name=name, grid=(m // tm, n // tn, nk) if rows_outer else (n // tn, m // tm, nk),
        in_specs=[a_spec, b_spec],
        out_specs=pl.BlockSpec((tm, tn), ij(lambda i, j, k: (i, j))),
        out_shape=jax.ShapeDtypeStruct((m, n), out_dtype),
        scratch_shapes=scratch,
        compiler_params=_params("parallel", "parallel", "arbitrary"),
    )(a, b)


def _pairs(bounds):
    return list(zip(bounds[:-1], bounds[1:]))


def _split(v, bounds):
    return [v[:, a:b] for a, b in _pairs(bounds)]


def _stage_build(name, f, tiled, params, consts, splits, ts, out_dtypes, ct_dtypes=None):
    n_t, n_p, n_c = len(tiled), len(params), len(consts)
    ct_dtypes = [t.dtype for t in tiled] if ct_dtypes is None else ct_dtypes
    s = tiled[0].shape[0]
    ts = min(ts, s)
    grid = (s // ts,)
    if splits is None:
        splits = [None] * n_t
    in_bounds = [(0, t.shape[1]) if b is None else tuple(b) for t, b in zip(tiled, splits)]

    def tile_aval(arr):
        return jax.ShapeDtypeStruct((ts, arr.shape[1]), arr.dtype)

    slab_avals = [[jax.ShapeDtypeStruct((ts, e - a), t.dtype) for a, e in _pairs(b)]
                  for t, b in zip(tiled, in_bounds)]
    out_avals = jax.eval_shape(f, slab_avals, list(params), [tile_aval(c) for c in consts])
    out_bounds = [tuple(np.cumsum([0] + [o.shape[1] for o in slabs]).tolist()) for slabs in out_avals]
    out_dtypes = [F32] * len(out_bounds) if out_dtypes is None else out_dtypes
    out_shapes = [jax.ShapeDtypeStruct((s, b[-1]), d) for b, d in zip(out_bounds, out_dtypes)]

    def row_spec(width):
        return pl.BlockSpec((ts, width), lambda i: (i, 0))

    def par_spec(arr):
        return pl.BlockSpec(arr.shape, lambda i: (0, 0))

    in_specs = ([row_spec(t.shape[1]) for t in tiled] + [par_spec(p) for p in params]
                + [row_spec(c.shape[1]) for c in consts])

    def load(refs):
        t = [_split(r[...], b) for r, b in zip(refs[:n_t], in_bounds)]
        p = [r[...] for r in refs[n_t:n_t + n_p]]
        c = [r[...] for r in refs[n_t + n_p:n_t + n_p + n_c]]
        return t, p, c

    def store(refs, values, bounds):
        for ref, slabs, b in zip(refs, values, bounds):
            for v, (a, e) in zip(slabs, _pairs(b)):
                ref[:, a:e] = v.astype(ref.dtype)

    def run_fwd(tiled, params, consts):
        def body(*refs):
            t, p, c = load(refs)
            store(refs[n_t + n_p + n_c:], f(t, p, c), out_bounds)

        return pl.pallas_call(
            body, name=name + "_fwd", grid=grid, in_specs=in_specs,
            out_specs=[row_spec(b[-1]) for b in out_bounds], out_shape=out_shapes,
            compiler_params=_params("parallel"),
        )(*tiled, *params, *consts)

    def run_bwd(tiled, params, consts, cts):
        n_in = n_t + n_p + n_c
        n_o = len(out_bounds)

        def body(*refs):
            t, p, c = load(refs)
            g = [_split(r[...].astype(F32), b) for r, b in zip(refs[n_in:n_in + n_o], out_bounds)]
            _, pull = jax.vjp(lambda t_, p_: f(t_, p_, c), t, p)
            dt, dp = pull(g)
            store(refs[n_in + n_o:n_in + n_o + n_t], dt, in_bounds)
            first = pl.program_id(0) == 0
            for ref, d in zip(refs[n_in + n_o + n_t:], dp):
                @pl.when(first)
                def _(ref=ref):
                    ref[...] = jnp.zeros_like(ref)

                ref[...] += d

        res = pl.pallas_call(
            body, name=name + "_bwd", grid=grid,
            in_specs=in_specs + [row_spec(b[-1]) for b in out_bounds],
            out_specs=[row_spec(t.shape[1]) for t in tiled] + [par_spec(p) for p in params],
            out_shape=[jax.ShapeDtypeStruct(t.shape, d) for t, d in zip(tiled, ct_dtypes)]
                      + [jax.ShapeDtypeStruct(p.shape, F32) for p in params],
            compiler_params=_params("arbitrary"),
        )(*tiled, *params, *consts, *cts)
        return tuple(res[:n_t]), tuple(res[n_t:])

    return run_fwd, run_bwd


def proj_stage(name, f, projections, extra=(), params=(), consts=(), splits=None, ts=256, out_dtypes=None):
    n_z = len(projections)
    forms = [pr[2] for pr in projections]
    names = [pr[3] for pr in projections]
    need_da = [pr[4] for pr in projections]
    extra, params, consts = tuple(extra), tuple(params), tuple(consts)

    def matmuls(a_list, w_list):
        return tuple(_matmul(a, w, form, out_dtype=F32, name=n + "_fwd")
                     for a, w, form, n in zip(a_list, w_list, forms, names))

    def build(zs, ct=False):
        ct_dtypes = [BF16] * n_z + [e.dtype for e in extra] if ct else None
        return _stage_build(name, f, tuple(zs) + extra, params, consts, splits, ts, out_dtypes, ct_dtypes)

    @jax.custom_vjp
    def op(a_list, w_list, extra, params, consts):
        zs = matmuls(a_list, w_list)
        return tuple(build(zs)[0](zs + extra, params, consts))

    def op_fwd(a_list, w_list, extra, params, consts):
        zs = matmuls(a_list, w_list)
        return tuple(build(zs)[0](zs + extra, params, consts)), (a_list, w_list, zs, extra, params, consts)

    def op_bwd(res, cts):
        a_list, w_list, zs, extra, params, consts = res
        dt, dp = build(zs, ct=True)[1](zs + extra, params, consts, cts)
        da_list, dw_list = [], []
        for a, w, dz, form, n, want in zip(a_list, w_list, dt[:n_z], forms, names, need_da):
            if form == "nn":
                da = _matmul(dz, w, "nt", out_dtype=a.dtype, name=n + "_da") if want else jnp.zeros_like(a)
                dw = _matmul(a, dz, "tn", out_dtype=w.dtype, name=n + "_dw")
            else:
                da = _matmul(dz, w, "nn", out_dtype=a.dtype, name=n + "_da") if want else jnp.zeros_like(a)
                dw = _matmul(dz, a, "tn", out_dtype=w.dtype, name=n + "_dw")
            da_list.append(da)
            dw_list.append(dw)
        return tuple(da_list), tuple(dw_list), tuple(dt[n_z:]), dp, tuple(jnp.zeros_like(c) for c in consts)

    op.defvjp(op_fwd, op_bwd)
    return op(tuple(pr[0] for pr in projections), tuple(pr[1] for pr in projections), extra, params, consts)


def _rms(t, g):
    return t * lax.rsqrt(jnp.mean(t * t, axis=-1, keepdims=True) + RMS_EPS) * g


@functools.partial(jax.custom_vjp, nondiff_argnums=(1,))
def _lane_roll(t, shift):
    return pltpu.roll(t, shift % t.shape[-1], t.ndim - 1)


def _lane_roll_fwd(t, shift):
    return _lane_roll(t, shift), None


def _lane_roll_bwd(shift, _, ct):
    return (pltpu.roll(ct, (-shift) % ct.shape[-1], ct.ndim - 1),)


_lane_roll.defvjp(_lane_roll_fwd, _lane_roll_bwd)


def _rope_lanes(t, tables, half):
    reps = t.shape[1] // tables[0].shape[1]
    c, s_lo, s_hi = [jnp.concatenate([tb] * reps, axis=1) if reps > 1 else tb for tb in tables]
    return t * c + _lane_roll(t, -half) * s_lo + _lane_roll(t, half) * s_hi


PRENORM_TS = 256


def _prenorm_fwd_call(x, g, shards):
    s, width = x.shape
    ts = min(PRENORM_TS, s)
    nt = s // ts
    n_arr = len(shards)

    def body(*refs):
        x_ref, g_ref = refs[:2]
        o_ref = refs[2 + n_arr]
        i = pl.program_id(0)
        ag_start, ag_forward, ag_finish = _allgather_phases(refs[2:2 + n_arr], refs[3 + n_arr:3 + 2 * n_arr],
                                                            *refs[3 + 2 * n_arr:])

        @pl.when(i == 0)
        def _():
            ag_start()

        @pl.when(i == nt // 2)
        def _():
            ag_forward()

        o_ref[...] = _rms(x_ref[...], g_ref[...]).astype(o_ref.dtype)

        @pl.when(i == nt - 1)
        def _():
            ag_finish()

    return pl.pallas_call(
        body, name="prenorm_fwd", grid=(nt,),
        in_specs=[pl.BlockSpec((ts, width), lambda i: (i, 0)), pl.BlockSpec(g.shape, lambda i: (0, 0))]
                 + [HBM_SPEC] * n_arr,
        out_specs=[pl.BlockSpec((ts, width), lambda i: (i, 0))] + [HBM_SPEC] * n_arr,
        out_shape=[jax.ShapeDtypeStruct(x.shape, BF16)] + _allgather_out_shapes(shards),
        scratch_shapes=_allgather_sems(n_arr),
        compiler_params=_params("arbitrary"),
    )(x, g, *shards)


def _prenorm_bwd_call(x, g, dh, parts):
    s, width = x.shape
    ts = min(PRENORM_TS, s)
    nt = s // ts
    n_arr = len(parts)

    def body(*refs):
        x_ref, g_ref, dh_ref = refs[:3]
        dx_ref, dg_ref = refs[3 + n_arr:5 + n_arr]
        i = pl.program_id(0)
        exchange_start, exchange_finish = _exchange_chips_phases(
            refs[3:3 + n_arr], refs[5 + n_arr:5 + 2 * n_arr], *refs[5 + 2 * n_arr:])

        @pl.when(i == 0)
        def _():
            exchange_start()
            dg_ref[...] = jnp.zeros_like(dg_ref)

        _, pull = jax.vjp(_rms, x_ref[...], g_ref[...])
        dx, dg = pull(dh_ref[...].astype(F32))
        dx_ref[...] = dx
        dg_ref[...] += dg

        @pl.when(i == nt - 1)
        def _():
            exchange_finish()

    row = pl.BlockSpec((ts, width), lambda i: (i, 0))
    par = pl.BlockSpec(g.shape, lambda i: (0, 0))
    return pl.pallas_call(
        body, name="prenorm_bwd", grid=(nt,),
        in_specs=[row, par, row] + [HBM_SPEC] * n_arr,
        out_specs=[row, par] + [HBM_SPEC] * n_arr,
        out_shape=[jax.ShapeDtypeStruct(x.shape, F32), jax.ShapeDtypeStruct(g.shape, F32)]
                  + [jax.ShapeDtypeStruct(p.shape, p.dtype) for p in parts],
        scratch_shapes=_exchange_chips_sems(n_arr),
        compiler_params=_params("arbitrary"),
    )(x, g, dh, *parts)


@functools.partial(jax.custom_vjp, nondiff_argnums=(3,))
def prenorm_gather(x, g, shards, wire_dtypes):
    out = _prenorm_fwd_call(x, g, [s.astype(d) for s, d in zip(shards, wire_dtypes)])
    return out[0], tuple(out[1:])


def _prenorm_gather_fwd(x, g, shards, wire_dtypes):
    return prenorm_gather(x, g, shards, wire_dtypes), (x, g)


def _prenorm_gather_bwd(wire_dtypes, res, cts):
    x, g = res
    dh, d_gathered = cts
    out = _prenorm_bwd_call(x, g, dh, _reduce_scatter_head(d_gathered, "grads"))
    return out[0], out[1], _reduce_scatter_tail(out[2:], "grads")


prenorm_gather.defvjp(_prenorm_gather_fwd, _prenorm_gather_bwd)


def _f_prep(t, p, c):
    qa, ka, va, cq, ckv, kr = t[0]
    return [[_rope_lanes(qa, c[0:3], A_HEAD_DIM // 2)], [_rope_lanes(ka, c[0:3], A_HEAD_DIM // 2)], [va],
            [_rms(cq, p[0])], [_rms(ckv, p[1])], [_rope_lanes(kr, c[3:6], ROPE_DIM // 2)]]


def _f_qrope(t, p, c):
    return [[_rope_lanes(t[0][0], c, ROPE_DIM // 2)]]


def _f_kv(t, p, c):
    (k_nope, v), (k_pe,) = t
    return [[k_nope + jnp.concatenate([k_pe] * B_HEADS, axis=1)], [v]]


def _f_gate(t, p, c):
    (ga, gb), (pa,), (pb,) = t
    ba, bb = p
    return [[jax.nn.sigmoid(ga + ba) * pa + jax.nn.sigmoid(gb + bb) * pb]]


def _f_post(t, p, c):
    (branch,), (residual,) = t
    x1 = residual + _rms(branch, p[0])
    return [[x1], [_rms(x1, p[1])]]


def _f_out(t, p, c):
    (gate,), (emb,), (x2,) = t
    y = x2 + jax.nn.sigmoid(gate) * emb
    err = y - c[0]
    return [[0.5 * jnp.mean(err * err, axis=-1, keepdims=True)]]


def _shift_down(cur, prev, has_prev):
    full = jnp.concatenate([prev * has_prev, cur], axis=0)
    return pltpu.roll(full, 1, 0)[HALO:], pltpu.roll(full, 2, 0)[HALO:]


GELU_C = float(np.sqrt(2.0 / np.pi))
GELU_A = 0.044715
HALO = 8


def _gelu_tanh(x):
    x2 = x * x
    th = jnp.tanh(x * (GELU_C + (GELU_C * GELU_A) * x2))
    half = 0.5 + 0.5 * th
    return x * half, half + x * (0.5 - 0.5 * (th * th)) * (GELU_C + (3.0 * GELU_C * GELU_A) * x2)


def _row_sum(t):
    return jnp.sum(t, axis=0, keepdims=True)


def _conv3(cur, prev, w_ref, b_ref, has_prev):
    u1, u2 = _shift_down(cur, prev, has_prev)
    return w_ref[2:3, :] * cur + w_ref[1:2, :] * u1 + w_ref[0:1, :] * u2 + b_ref[...], u1, u2


def _mlp_act_specs(s):
    ts = min(CONV_TS, s)
    hb = ts // HALO

    def half_specs(h):
        return [pl.BlockSpec((ts, D_FF), lambda i: (i, h)),
                pl.BlockSpec((HALO, D_FF), lambda i: (jnp.maximum(i * hb - 1, 0), h))]

    def par_specs(h):
        return [pl.BlockSpec((CONV_W, D_FF), lambda i: (0, h)), pl.BlockSpec((1, D_FF), lambda i: (0, h))]

    return ts, hb, half_specs, par_specs


def _mlp_act_fwd_call(up, conv_w, conv_b):
    s = up.shape[0]
    ts, hb, half_specs, par_specs = _mlp_act_specs(s)

    def body(g_ref, gp_ref, v_ref, vp_ref, wg_ref, bg_ref, wv_ref, bv_ref, o_ref):
        has_prev = (pl.program_id(0) > 0).astype(F32)

        def chunk(cidx, carry):
            cols = pl.ds(pl.multiple_of(cidx * CONV_CHUNK, CONV_CHUNK), CONV_CHUNK)
            u_g, _, _ = _conv3(g_ref[:, cols], gp_ref[:, cols], wg_ref.at[:, cols], bg_ref.at[:, cols], has_prev)
            u_v, _, _ = _conv3(v_ref[:, cols], vp_ref[:, cols], wv_ref.at[:, cols], bv_ref.at[:, cols], has_prev)
            o_ref[:, cols] = (_gelu_tanh(u_g)[0] * u_v).astype(o_ref.dtype)
            return carry

        lax.fori_loop(0, D_FF // CONV_CHUNK, chunk, 0)

    return pl.pallas_call(
        body, name="mlp_act_fwd", grid=(s // ts,),
        in_specs=half_specs(0) + half_specs(1) + par_specs(0) + par_specs(1),
        out_specs=pl.BlockSpec((ts, D_FF), lambda i: (i, 0)),
        out_shape=jax.ShapeDtypeStruct((s, D_FF), BF16),
        compiler_params=_params("parallel"),
    )(up, up, up, up, conv_w, conv_b, conv_w, conv_b)


def _mlp_act_bwd_call(up, conv_w, conv_b, dact):
    s = up.shape[0]
    ts, hb, half_specs, par_specs = _mlp_act_specs(s)
    nt = s // ts
    ext = ts + HALO
    bf16_rows = 2 * HALO

    def next_spec(rows, h):
        return pl.BlockSpec((rows, D_FF), lambda i: (jnp.minimum((i + 1) * (ts // rows), s // rows - 1), h))

    def body(g_ref, gp_ref, gn_ref, v_ref, vp_ref, vn_ref, wg_ref, bg_ref, wv_ref, bv_ref, da_ref, dan_ref,
             dup_ref, dwg_ref, dbg_ref, dwv_ref, dbv_ref):
        i = pl.program_id(0)
        has_prev, has_next = (i > 0).astype(F32), (i < nt - 1).astype(F32)

        @pl.when(i == 0)
        def _():
            for ref in (dwg_ref, dbg_ref, dwv_ref, dbv_ref):
                ref[...] = jnp.zeros_like(ref)

        def chunk(cidx, carry):
            cols = pl.ds(pl.multiple_of(cidx * CONV_CHUNK, CONV_CHUNK), CONV_CHUNK)
            g_ext = jnp.concatenate([g_ref[:, cols], gn_ref[:, cols]], axis=0)
            v_ext = jnp.concatenate([v_ref[:, cols], vn_ref[:, cols]], axis=0)
            u_g, g1, g2 = _conv3(g_ext, gp_ref[:, cols], wg_ref.at[:, cols], bg_ref.at[:, cols], has_prev)
            u_v, v1, v2 = _conv3(v_ext, vp_ref[:, cols], wv_ref.at[:, cols], bv_ref.at[:, cols], has_prev)
            da_ext = jnp.concatenate([da_ref[:, cols].astype(F32),
                                      dan_ref[:, cols].astype(F32)[0:HALO] * has_next], axis=0)
            act_g, dact_g = _gelu_tanh(u_g)
            du_g = da_ext * u_v * dact_g
            du_v = da_ext * act_g
            for du, w_ref, x0, x1, x2, dw_ref, db_ref, lo in ((du_g, wg_ref, g_ext, g1, g2, dwg_ref, dbg_ref, 0),
                                                          (du_v, wv_ref, v_ext, v1, v2, dwv_ref, dbv_ref, D_FF)):
                d1 = pltpu.roll(du, ext - 1, 0)
                d2 = pltpu.roll(du, ext - 2, 0)
                dup = w_ref[2:3, cols] * du + w_ref[1:2, cols] * d1 + w_ref[0:1, cols] * d2
                out_cols = pl.ds(pl.multiple_of(lo + cidx * CONV_CHUNK, CONV_CHUNK), CONV_CHUNK)
                dup_ref[:, out_cols] = dup[0:ts].astype(dup_ref.dtype)
                own = du[0:ts]
                dw_ref[0:1, cols] += _row_sum(own * x2[0:ts])
                dw_ref[1:2, cols] += _row_sum(own * x1[0:ts])
                dw_ref[2:3, cols] += _row_sum(own * x0[0:ts])
                db_ref[:, cols] += _row_sum(own)
            return carry

        lax.fori_loop(0, D_FF // CONV_CHUNK, chunk, 0)

    par_out = [pl.BlockSpec((CONV_W, D_FF), lambda i: (0, 0)), pl.BlockSpec((1, D_FF), lambda i: (0, 0))]
    par_shapes = [jax.ShapeDtypeStruct((CONV_W, D_FF), F32), jax.ShapeDtypeStruct((1, D_FF), F32)]
    return pl.pallas_call(
        body, name="mlp_act_bwd", grid=(nt,),
        in_specs=(half_specs(0) + [next_spec(HALO, 0)] + half_specs(1) + [next_spec(HALO, 1)]
                  + par_specs(0) + par_specs(1)
                  + [pl.BlockSpec((ts, D_FF), lambda i: (i, 0)), next_spec(bf16_rows, 0)]),
        out_specs=[pl.BlockSpec((ts, 2 * D_FF), lambda i: (i, 0))] + par_out + par_out,
        out_shape=[jax.ShapeDtypeStruct((s, 2 * D_FF), BF16)] + par_shapes + par_shapes,
        compiler_params=_params("arbitrary"),
    )(up, up, up, up, up, up, conv_w, conv_b, conv_w, conv_b, dact, dact)


@jax.custom_vjp
def mlp_up(h2, w_up_t, conv_w, conv_b):
    return _mlp_act_fwd_call(_matmul(h2, w_up_t, "nt", out_dtype=F32, name="w_up_fwd"), conv_w, conv_b)


def _mlp_up_fwd(h2, w_up_t, conv_w, conv_b):
    up = _matmul(h2, w_up_t, "nt", out_dtype=F32, name="w_up_fwd")
    return _mlp_act_fwd_call(up, conv_w, conv_b), (h2, w_up_t, up, conv_w, conv_b)


def _mlp_up_bwd(res, dact):
    h2, w_up_t, up, conv_w, conv_b = res
    dup, dwg, dbg, dwv, dbv = _mlp_act_bwd_call(up, conv_w, conv_b, dact)
    dh2 = _matmul(dup, w_up_t, "nn", out_dtype=h2.dtype, name="w_up_da")
    dw = _matmul(dup, h2, "tn", out_dtype=w_up_t.dtype, name="w_up_dw")
    return dh2, dw, jnp.concatenate([dwg, dwv], axis=1), jnp.concatenate([dbg, dbv], axis=1)


mlp_up.defvjp(_mlp_up_fwd, _mlp_up_bwd)


SWA_ROWS = A_GROUP * SWA_BLOCK


def _swa_sink_rows(sink_ref, g):
    return jnp.concatenate([jnp.full((SWA_BLOCK, 1), sink_ref[g * A_GROUP + h], F32) for h in range(A_GROUP)], axis=0)


def _swa_operands(q_ref, kp_ref, kc_ref, vp_ref, vc_ref, sink_ref):
    groups = []
    for g in range(A_KV_HEADS):
        groups.append((_swa_stack_heads(q_ref, g), _dup_half(kp_ref[...], g), _dup_half(kc_ref[...], g),
                       _dup_half(vp_ref[...], g), _dup_half(vc_ref[...], g)))
    return groups, jnp.concatenate([_swa_sink_rows(sink_ref, g) for g in range(A_KV_HEADS)], axis=0)


def _swa_probs(groups, sink, prev_off):
    scale = A_HEAD_DIM ** -0.5
    sp = jnp.concatenate([lax.dot_general(gr[0], gr[1], NT_DIMS, preferred_element_type=F32) for gr in groups], axis=0)
    sc = jnp.concatenate([lax.dot_general(gr[0], gr[2], NT_DIMS, preferred_element_type=F32) for gr in groups], axis=0)
    qi = lax.broadcasted_iota(jnp.int32, sp.shape, 0) & (SWA_BLOCK - 1)
    kj = lax.broadcasted_iota(jnp.int32, sp.shape, 1)
    in_cur = kj <= qi
    sw = jnp.where(in_cur, sc, jnp.where(kj > qi + prev_off, sp, -jnp.inf)) * scale
    m = jnp.maximum(jnp.max(sw, axis=-1, keepdims=True), sink)
    e, es = jnp.exp(sw - m), jnp.exp(sink - m)
    den = jnp.sum(e, axis=-1, keepdims=True) + es
    return e / den, in_cur, es / den


def _swa_split(t, in_cur):
    cur = jnp.where(in_cur, t, 0.0)
    return t - cur, cur


MLA_SCALE = (NOPE_DIM + ROPE_DIM) ** -0.5
EXP2_SCALE = MLA_SCALE * float(np.log2(np.e))
NT_DIMS = (((1,), (1,)), ((), ()))
TN_DIMS = (((0,), (0,)), ((), ()))


LANES = 128
HALF = LANES // 2


def _low_half(shape):
    return lax.broadcasted_iota(jnp.int32, shape, len(shape) - 1) < HALF


def _dup_half(x, g):
    xf = x.astype(F32)
    keep = _low_half(xf.shape) if g == 0 else jnp.logical_not(_low_half(xf.shape))
    xm = jnp.where(keep, xf, 0.0)
    return (xm + pltpu.roll(xm, HALF, 1)).astype(x.dtype)


def _fold_half(r, g):
    total = r + pltpu.roll(r, HALF, 1)
    keep = _low_half(r.shape) if g == 0 else jnp.logical_not(_low_half(r.shape))
    return jnp.where(keep, total, 0.0)


def _swa_stack_heads(ref, g):
    parts = []
    for tile in range(2):
        slab = ref[:, (2 * g + tile) * LANES:(2 * g + tile + 1) * LANES]
        low = _low_half(slab.shape)
        parts += [jnp.where(low, slab, jnp.zeros_like(slab)), jnp.where(low, jnp.zeros_like(slab), slab)]
    return jnp.concatenate(parts, axis=0)


def _swa_unstack_heads(ref, g, rows):
    for tile in range(2):
        a = rows[(2 * tile) * SWA_BLOCK:(2 * tile + 1) * SWA_BLOCK]
        b = rows[(2 * tile + 1) * SWA_BLOCK:(2 * tile + 2) * SWA_BLOCK]
        ref[:, (2 * g + tile) * LANES:(2 * g + tile + 1) * LANES] = jnp.where(_low_half(a.shape), a, b).astype(ref.dtype)


def _swa_nat_specs():
    blk = SWA_BLOCK
    q_spec = pl.BlockSpec((blk, A_HEADS * A_HEAD_DIM), lambda n: (n, 0))
    prev_spec = pl.BlockSpec((blk, LANES), lambda n: (jnp.maximum(n - 1, 0), 0))
    cur_spec = pl.BlockSpec((blk, LANES), lambda n: (n, 0))
    return q_spec, prev_spec, cur_spec, pl.BlockSpec(memory_space=pltpu.SMEM)


def _swa_nat_fwd_call(q, k, v, sinks, shards):
    s = q.shape[0]
    nblk = s // SWA_BLOCK
    n_arr = len(shards)
    q_spec, prev_spec, cur_spec, sink_spec = _swa_nat_specs()

    def body(*refs):
        q_ref, kp_ref, kc_ref, vp_ref, vc_ref, sink_ref = refs[:6]
        o_ref = refs[6 + n_arr]
        n = pl.program_id(0)
        ag_start, ag_forward, ag_finish = _allgather_phases(refs[6:6 + n_arr], refs[7 + n_arr:7 + 2 * n_arr],
                                                            *refs[7 + 2 * n_arr:])

        @pl.when(n == 0)
        def _():
            ag_start()

        @pl.when(n == nblk // 2)
        def _():
            ag_forward()

        prev_off = jnp.where(n > 0, 0, SWA_BLOCK)
        groups, sink = _swa_operands(q_ref, kp_ref, kc_ref, vp_ref, vc_ref, sink_ref)
        p, in_cur, _ = _swa_probs(groups, sink, prev_off)
        ppb, pcb = [t.astype(BF16) for t in _swa_split(p, in_cur)]
        for g, (_, _, _, vp, vc) in enumerate(groups):
            rows = slice(g * SWA_ROWS, (g + 1) * SWA_ROWS)
            out = (jnp.dot(ppb[rows], vp, preferred_element_type=F32)
                   + jnp.dot(pcb[rows], vc, preferred_element_type=F32))
            _swa_unstack_heads(o_ref, g, out)

        @pl.when(n == nblk - 1)
        def _():
            ag_finish()

    return pl.pallas_call(
        body, name="swa_fwd", grid=(nblk,),
        in_specs=[q_spec, prev_spec, cur_spec, prev_spec, cur_spec, sink_spec] + [HBM_SPEC] * n_arr,
        out_specs=[q_spec] + [HBM_SPEC] * n_arr,
        out_shape=[jax.ShapeDtypeStruct(q.shape, BF16)] + _allgather_out_shapes(shards),
        scratch_shapes=_allgather_sems(n_arr),
        compiler_params=_params("arbitrary"),
    )(q, k, k, v, v, sinks, *shards)


def _swa_nat_bwd_call(q, k, v, sinks, do, parts):
    s = q.shape[0]
    nblk = s // SWA_BLOCK
    n_arr = len(parts)
    q_spec, prev_spec, cur_spec, sink_spec = _swa_nat_specs()
    scale = A_HEAD_DIM ** -0.5
    dsink_spec = pl.BlockSpec((A_KV_HEADS, SWA_ROWS, 1), lambda n: (0, 0, 0))

    def body(*refs):
        q_ref, kp_ref, kc_ref, vp_ref, vc_ref, sink_ref, do_ref = refs[:7]
        dq_ref, dkp_ref, dkc_ref, dvp_ref, dvc_ref, dsink_ref = refs[7 + n_arr:13 + n_arr]
        n = pl.program_id(0)
        exchange_start, exchange_finish = _exchange_chips_phases(
            refs[7:7 + n_arr], refs[13 + n_arr:13 + 2 * n_arr], *refs[13 + 2 * n_arr:])

        @pl.when(n == 0)
        def _():
            exchange_start()
        prev_off = jnp.where(n > 0, 0, SWA_BLOCK)

        @pl.when(n == 0)
        def _():
            dsink_ref[...] = jnp.zeros_like(dsink_ref)

        groups, sink = _swa_operands(q_ref, kp_ref, kc_ref, vp_ref, vc_ref, sink_ref)
        dobs = [_swa_stack_heads(do_ref, g) for g in range(A_KV_HEADS)]
        p, in_cur, ps = _swa_probs(groups, sink, prev_off)
        ppb, pcb = [t.astype(BF16) for t in _swa_split(p, in_cur)]

        def per_group(fn):
            return jnp.concatenate([fn(g, slice(g * SWA_ROWS, (g + 1) * SWA_ROWS)) for g in range(A_KV_HEADS)], axis=0)

        out = per_group(lambda g, rows: jnp.dot(ppb[rows], groups[g][3], preferred_element_type=F32)
                        + jnp.dot(pcb[rows], groups[g][4], preferred_element_type=F32))
        delta = jnp.sum(jnp.concatenate(dobs, axis=0).astype(F32) * out, axis=-1, keepdims=True)
        dp = jnp.where(in_cur,
                       per_group(lambda g, rows: lax.dot_general(dobs[g], groups[g][4], NT_DIMS,
                                                                 preferred_element_type=F32)),
                       per_group(lambda g, rows: lax.dot_general(dobs[g], groups[g][3], NT_DIMS,
                                                                 preferred_element_type=F32)))
        dsp, dsc = [t.astype(BF16) for t in _swa_split(p * (dp - delta), in_cur)]
        dsink_ref[...] += (-ps * delta).reshape(dsink_ref.shape)
        totals = [jnp.zeros((SWA_BLOCK, LANES), F32) for _ in range(4)]
        for g, (qb, kp, kc, _, _) in enumerate(groups):
            rows = slice(g * SWA_ROWS, (g + 1) * SWA_ROWS)
            dq = (jnp.dot(dsp[rows], kp, preferred_element_type=F32)
                  + jnp.dot(dsc[rows], kc, preferred_element_type=F32)) * scale
            _swa_unstack_heads(dq_ref, g, dq)
            pieces = [lax.dot_general(dsp[rows], qb, TN_DIMS, preferred_element_type=F32) * scale,
                      lax.dot_general(dsc[rows], qb, TN_DIMS, preferred_element_type=F32) * scale,
                      lax.dot_general(ppb[rows], dobs[g], TN_DIMS, preferred_element_type=F32),
                      lax.dot_general(pcb[rows], dobs[g], TN_DIMS, preferred_element_type=F32)]
            totals = [tot + _fold_half(r, g) for tot, r in zip(totals, pieces)]
        dkp_ref[...], dkc_ref[...], dvp_ref[...], dvc_ref[...] = totals

        @pl.when(n == nblk - 1)
        def _():
            exchange_finish()

    kv_shape = jax.ShapeDtypeStruct(k.shape, F32)
    return pl.pallas_call(
        body, name="swa_bwd", grid=(nblk,),
        in_specs=[q_spec, prev_spec, cur_spec, prev_spec, cur_spec, sink_spec, q_spec] + [HBM_SPEC] * n_arr,
        out_specs=[q_spec, cur_spec, cur_spec, cur_spec, cur_spec, dsink_spec] + [HBM_SPEC] * n_arr,
        out_shape=[jax.ShapeDtypeStruct(q.shape, q.dtype), kv_shape, kv_shape, kv_shape, kv_shape,
                   jax.ShapeDtypeStruct((A_KV_HEADS, SWA_ROWS, 1), F32)]
                  + [jax.ShapeDtypeStruct(p.shape, p.dtype) for p in parts],
        scratch_shapes=_exchange_chips_sems(n_arr),
        compiler_params=_params("arbitrary"),
    )(q, k, k, v, v, sinks, do, *parts)


@jax.custom_vjp
def swa_nat(q, k, v, sinks, shards):
    out = _swa_nat_fwd_call(q, k, v, sinks, [s.astype(BF16) for s in shards])
    return out[0], tuple(out[1:])


def _swa_nat_fwd(q, k, v, sinks, shards):
    out = _swa_nat_fwd_call(q, k, v, sinks, [s.astype(BF16) for s in shards])
    return (out[0], tuple(out[1:])), (q, k, v, sinks)


def _swa_nat_bwd(res, cts):
    q, k, v, sinks = res
    do, d_gathered = cts
    out = _swa_nat_bwd_call(q, k, v, sinks, do, _reduce_scatter_head(d_gathered, "mid_grads"))
    dq, dkp, dkc, dvp, dvc, dsink = out[:6]

    def fold(prev_part, cur_part):
        shifted = jnp.concatenate([prev_part[SWA_BLOCK:], jnp.zeros_like(prev_part[:SWA_BLOCK])], axis=0)
        return (cur_part + shifted).astype(k.dtype)

    dsinks = jnp.sum(dsink.reshape(A_HEADS, SWA_BLOCK), axis=1)
    return dq, fold(dkp, dkc), fold(dvp, dvc), dsinks, _reduce_scatter_tail(out[6:], "mid_grads")


swa_nat.defvjp(_swa_nat_fwd, _swa_nat_bwd)

N_PAIR = B_HEADS // 2


def _flash_nat_fwd_call(q, k, v, shards):
    s = q.shape[0]
    t = min(FLASH_T, s)
    nb = s // t
    d = LANES
    n_arr = len(shards)

    def body(*refs):
        q_ref, k_ref, v_ref = refs[:3]
        shard_refs = refs[3:3 + n_arr]
        o_ref, lse_ref = refs[3 + n_arr:5 + n_arr]
        gathered_refs = refs[5 + n_arr:5 + 2 * n_arr]
        vt_ref, m_ref, l_ref, acc_ref = refs[5 + 2 * n_arr:9 + 2 * n_arr]
        pair, i = pl.program_id(0), pl.program_id(1)
        ag_start, ag_forward, ag_finish = _allgather_phases(shard_refs, gathered_refs, *refs[9 + 2 * n_arr:])

        @pl.when((pair == 0) & (i == 0))
        def _():
            ag_start()

        @pl.when((pair == N_PAIR // 2) & (i == 0))
        def _():
            ag_forward()

        @pl.when(i == 0)
        def _():
            for hh in range(2):
                for chunk in range(nb):
                    rows = slice(chunk * t, (chunk + 1) * t)
                    vt_ref[hh, :, rows] = v_ref[rows, hh * d:(hh + 1) * d].T

        m_ref[...] = jnp.full_like(m_ref, -jnp.inf)
        l_ref[...] = jnp.zeros_like(l_ref)
        acc_ref[...] = jnp.zeros_like(acc_ref)

        def step(j, on_diagonal):
            keys = pl.ds(pl.multiple_of(j * t, t), t)
            scores = [lax.dot_general(k_ref[keys, hh * d:(hh + 1) * d], q_ref[:, hh * d:(hh + 1) * d], NT_DIMS,
                                      preferred_element_type=F32) for hh in range(2)]
            for hh in range(2):
                sc_t = scores[hh]
                if on_diagonal:
                    key = lax.broadcasted_iota(jnp.int32, (t, t), 0)
                    qry = lax.broadcasted_iota(jnp.int32, (t, t), 1)
                    sc_t = jnp.where(qry >= key, sc_t, -jnp.inf)
                m_old = m_ref[hh]
                m_new = jnp.maximum(m_old, jnp.max(sc_t, axis=0, keepdims=True))
                alpha = jnp.exp2((m_old - m_new) * EXP2_SCALE)
                p_t = jnp.exp2((sc_t - m_new) * EXP2_SCALE)
                l_ref[hh] = alpha * l_ref[hh] + jnp.sum(p_t, axis=0, keepdims=True)
                acc_ref[hh] = alpha * acc_ref[hh] + jnp.dot(vt_ref[hh, :, keys], p_t.astype(BF16),
                                                            preferred_element_type=F32)
                m_ref[hh] = m_new

        def below(j, carry):
            step(j, False)
            return carry

        lax.fori_loop(0, i, below, 0)
        step(i, True)
        outs =[(acc_ref[hh] / l_ref[hh]).T for hh in range(2)]
        for hh in range(2):
            lse_ref[hh] = m_ref[hh] * EXP2_SCALE + jnp.log2(l_ref[hh])
        o_ref[...] = (outs[0] + pltpu.roll(outs[1], HALF, 1)).astype(o_ref.dtype)

        @pl.when((pair == N_PAIR - 1) & (i == nb - 1))
        def _():
            ag_finish()

    return pl.pallas_call(
        body, name="mla_fwd", grid=(N_PAIR, nb),
        in_specs=[pl.BlockSpec((t, 2 * d), lambda p, i: (i, p)),
                  pl.BlockSpec((s, 2 * d), lambda p, i: (0, p)),
                  pl.BlockSpec((s, 2 * d), lambda p, i: (0, p))] + [HBM_SPEC] * n_arr,
        out_specs=[pl.BlockSpec((t, d), lambda p, i: (i, p)),
                   pl.BlockSpec((2, 1, t), lambda p, i: (p, 0, i))] + [HBM_SPEC] * n_arr,
        out_shape=[jax.ShapeDtypeStruct((s, N_PAIR * d), BF16), jax.ShapeDtypeStruct((B_HEADS, 1, s), F32)]
                  + _allgather_out_shapes(shards),
        scratch_shapes=[pltpu.VMEM((2, d, s), BF16), pltpu.VMEM((2, 1, t), F32), pltpu.VMEM((2, 1, t), F32),
                        pltpu.VMEM((2, d, t), F32)] + _allgather_sems(n_arr),
        compiler_params=_params("arbitrary", "arbitrary"),
    )(q, k, v, *shards)


def _flash_nat_delta_call(o, do):
    s, w = o.shape
    t = min(FLASH_T, s)

    def body(o_ref, do_ref, out_ref):
        prod = o_ref[...].astype(F32) * do_ref[...].astype(F32)
        lane = lax.broadcasted_iota(jnp.int32, (w, LANES), 0) // V_DIM
        head = lax.broadcasted_iota(jnp.int32, (w, LANES), 1)
        out_ref[...] = jnp.dot(prod, (lane == head).astype(F32), precision=lax.Precision.HIGHEST,
                               preferred_element_type=F32)

    spec = pl.BlockSpec((t, w), lambda i: (i, 0))
    return pl.pallas_call(
        body, name="mla_delta", grid=(s // t,), in_specs=[spec, spec],
        out_specs=pl.BlockSpec((t, LANES), lambda i: (i, 0)),
        out_shape=jax.ShapeDtypeStruct((s, LANES), F32), compiler_params=_params("parallel"),
    )(o, do)


def _flash_nat_bwd_call(q, k, v, lse_row, delta_row, do, parts):
    s = q.shape[0]
    t = min(FLASH_T, s)
    nb = s // t
    d = LANES
    n_arr = len(parts)

    def body(*refs):
        q_ref, k_ref, v_ref, lse_ref, delta_ref, do_ref = refs[:6]
        part_refs = refs[6:6 + n_arr]
        dq_ref, dk_ref, dv_ref = refs[6 + n_arr:9 + n_arr]
        received_refs = refs[9 + n_arr:9 + 2 * n_arr]
        dq_acc, dk_acc, dv_acc = refs[9 + 2 * n_arr:12 + 2 * n_arr]
        pair, j = pl.program_id(0), pl.program_id(1)
        exchange_start, exchange_finish = _exchange_chips_phases(part_refs, received_refs, *refs[12 + 2 * n_arr:])

        @pl.when((pair == 0) & (j == 0))
        def _():
            exchange_start()

        @pl.when(j == 0)
        def _():
            dq_acc[...] = jnp.zeros_like(dq_acc)

        for hh in range(2):
            kb, vb = k_ref[:, hh * d:(hh + 1) * d], v_ref[:, hh * d:(hh + 1) * d]
            dk_acc[...] = jnp.zeros_like(dk_acc)
            dv_acc[...] = jnp.zeros_like(dv_acc)

            def step(i, on_diagonal, hh=hh, kb=kb, vb=vb):
                rows = pl.ds(pl.multiple_of(i * t, t), t)
                qb = q_ref[rows, hh * d:(hh + 1) * d]
                do_pair = do_ref[rows, :].astype(F32)
                do_h = do_pair if hh == 0 else pltpu.roll(do_pair, HALF, 1)
                dob = jnp.where(_low_half(do_h.shape), do_h, 0.0).astype(BF16)
                sc_t = lax.dot_general(kb, qb, NT_DIMS, preferred_element_type=F32)
                p_t = jnp.exp2(sc_t * EXP2_SCALE - lse_ref[hh, :, rows])
                if on_diagonal:
                    key = lax.broadcasted_iota(jnp.int32, (t, t), 0)
                    qry = lax.broadcasted_iota(jnp.int32, (t, t), 1)
                    p_t = jnp.where(qry >= key, p_t, 0.0)
                dp_t = lax.dot_general(vb, dob, NT_DIMS, preferred_element_type=F32)
                ds_t = (p_t * (dp_t - delta_ref[hh, :, rows])).astype(BF16)
                dv_acc[...] += jnp.dot(p_t.astype(BF16), dob, preferred_element_type=F32)
                dk_acc[...] += jnp.dot(ds_t, qb, preferred_element_type=F32)
                dq_acc[hh, rows, :] += lax.dot_general(ds_t, kb, TN_DIMS, preferred_element_type=F32)

            def above(i, carry, step=step):
                step(i, False)
                return carry

            step(j, True)
            lax.fori_loop(j + 1, nb, above, 0)
            dk_ref[:, hh * d:(hh + 1) * d] = (dk_acc[...] * MLA_SCALE).astype(dk_ref.dtype)
            dv_ref[:, hh * d:(hh + 1) * d] = dv_acc[...].astype(dv_ref.dtype)

        @pl.when(j == nb - 1)
        def _():
            for hh in range(2):
                dq_ref[:, hh * d:(hh + 1) * d] = (dq_acc[hh] * MLA_SCALE).astype(dq_ref.dtype)

        @pl.when((pair == N_PAIR - 1) & (j == nb - 1))
        def _():
            exchange_finish()

    full_spec = pl.BlockSpec((s, 2 * d), lambda p, j: (0, p))
    tile_spec = pl.BlockSpec((t, 2 * d), lambda p, j: (j, p))
    row_spec = pl.BlockSpec((2, 1, s), lambda p, j: (p, 0, 0))
    return pl.pallas_call(
        body, name="mla_bwd", grid=(N_PAIR, nb),
        in_specs=[full_spec, tile_spec, tile_spec, row_spec, row_spec, pl.BlockSpec((s, d), lambda p, j: (0, p))]
                 + [HBM_SPEC] * n_arr,
        out_specs=[full_spec, tile_spec, tile_spec] + [HBM_SPEC] * n_arr,
        out_shape=[jax.ShapeDtypeStruct(q.shape, q.dtype)] * 3 + [jax.ShapeDtypeStruct(p.shape, p.dtype) for p in parts],
        scratch_shapes=[pltpu.VMEM((2, s, d), F32), pltpu.VMEM((t, d), F32), pltpu.VMEM((t, d), F32)]
                       + _exchange_chips_sems(n_arr),
        compiler_params=_params("arbitrary", "arbitrary"),
    )(q, k, v, lse_row, delta_row, do, *parts)


def _reduce_scatter_head(cts, tag):
    received = _exchange_sibling(list(cts), tag + "_exchange_sibling")
    my_c = lax.axis_index("c").astype(jnp.int32).reshape(1)
    return [_pair_add(m, r, my_c, "%s_pair_add_%d" % (tag, i)) for i, (m, r) in enumerate(zip(cts, received))]


def _reduce_scatter_tail(chip_parts, tag):
    return tuple(_sum_blocks(r, "%s_sum_%d" % (tag, i)) for i, r in enumerate(chip_parts))


@jax.custom_vjp
def flash_nat(q, k, v, shards):
    out = _flash_nat_fwd_call(q, k, v, [s.astype(BF16) for s in shards])
    return out[0], tuple(out[2:])


def _flash_nat_fwd(q, k, v, shards):
    out = _flash_nat_fwd_call(q, k, v, [s.astype(BF16) for s in shards])
    return (out[0], tuple(out[2:])), (q, k, v, out[0], out[1])


def _flash_nat_bwd(res, cts):
    q, k, v, o, lse = res
    do, d_gathered = cts
    delta = _flash_nat_delta_call(o, do)[:, :B_HEADS].T.reshape(B_HEADS, 1, q.shape[0])
    out = _flash_nat_bwd_call(q, k, v, lse, delta, do, _reduce_scatter_head(d_gathered, "mlp_grads"))
    return out[0], out[1], out[2], _reduce_scatter_tail(out[3:], "mlp_grads")


flash_nat.defvjp(_flash_nat_fwd, _flash_nat_bwd)


HBM_SPEC = pl.BlockSpec(memory_space=pltpu.HBM)


def _allgather(shards, name):
    n_arr = len(shards)

    def body(*refs):
        start, forward, finish = _allgather_phases(refs[:n_arr], refs[n_arr:2 * n_arr], *refs[2 * n_arr:])
        start()
        forward()
        finish()

    return pl.pallas_call(
        body, name=name, out_shape=_allgather_out_shapes(shards),
        in_specs=[HBM_SPEC] * n_arr, out_specs=[HBM_SPEC] * n_arr,
        scratch_shapes=_allgather_sems(n_arr),
    )(*shards)


def _allgather_out_shapes(shards):
    return [jax.ShapeDtypeStruct((N_DEV,) + s.shape, s.dtype) for s in shards]


def _allgather_sems(n_arr):
    return [pltpu.SemaphoreType.DMA((7, n_arr)), pltpu.SemaphoreType.DMA((7, n_arr)), pltpu.SemaphoreType.DMA((n_arr,))]


def _allgather_phases(x_refs, out_refs, send_sems, recv_sems, local_sems):
    arrays = range(len(x_refs))
    x, y, c = lax.axis_index("x"), lax.axis_index("y"), lax.axis_index("c")
    me, sibling = (x, y, c), (x, y, 1 - c)
    chips = [(1 - x, y), (x, 1 - y), (1 - x, 1 - y)]

    def rows(a, px, py, pc):
        return out_refs[a].at[4 * px + 2 * py + pc]

    def copy(a, k, block, to, src=None):
        return pltpu.make_async_remote_copy(
            src_ref=rows(a, *block) if src is None else src, dst_ref=rows(a, *block),
            send_sem=send_sems.at[k, a], recv_sem=recv_sems.at[k, a], device_id=to, device_id_type=MESH_ID)

    def mine():
        return [pltpu.make_async_copy(x_refs[a], rows(a, *me), local_sems.at[a]) for a in arrays]

    def first():
        return [cp for a in arrays for cp in
                [copy(a, 0, me, sibling, src=x_refs[a])]
                + [copy(a, 1 + j, me, (*chip, c), src=x_refs[a]) for j, chip in enumerate(chips)]]

    def passed():
        return [copy(a, 4 + j, (*chip, c), sibling) for j, chip in enumerate(chips) for a in arrays]

    def start():
        for cp in mine() + first():
            cp.start()

    def forward():
        for j, chip in enumerate(chips):
            for a in arrays:
                copy(a, 1 + j, (*chip, c), me).wait_recv()
                copy(a, 4 + j, (*chip, c), sibling).start()

    def finish():
        for a in arrays:
            copy(a, 0, sibling, me).wait_recv()
        for j, chip in enumerate(chips):
            for a in arrays:
                copy(a, 4 + j, (*chip, 1 - c), me).wait_recv()
        for cp in first() + passed():
            cp.wait_send()
        for cp in mine():
            cp.wait()

    return start, forward, finish


N_CHIP = 4


def _exchange_sibling(parts, name):
    n_arr = len(parts)

    def body(*refs):
        in_refs, recv_refs = refs[:n_arr], refs[n_arr:2 * n_arr]
        send_sems, recv_sems = refs[2 * n_arr:]
        x, y, c = lax.axis_index("x"), lax.axis_index("y"), lax.axis_index("c")
        copies = []
        for a in range(n_arr):
            for q in range(N_CHIP):
                copies.append(pltpu.make_async_remote_copy(
                    src_ref=in_refs[a].at[2 * q + 1 - c], dst_ref=recv_refs[a].at[q],
                    send_sem=send_sems.at[q, a], recv_sem=recv_sems.at[q, a],
                    device_id=(x, y, 1 - c), device_id_type=MESH_ID))
        for cp in copies:
            cp.start()
        for cp in copies:
            cp.wait()

    return pl.pallas_call(
        body, name=name, out_shape=[jax.ShapeDtypeStruct((N_CHIP,) + p.shape[1:], p.dtype) for p in parts],
        in_specs=[HBM_SPEC] * n_arr, out_specs=[HBM_SPEC] * n_arr,
        scratch_shapes=[pltpu.SemaphoreType.DMA((N_CHIP, n_arr)), pltpu.SemaphoreType.DMA((N_CHIP, n_arr))],
    )(*parts)


def _exchange_chips_sems(n_arr):
    return [pltpu.SemaphoreType.DMA((N_CHIP - 1, n_arr)), pltpu.SemaphoreType.DMA((N_CHIP - 1, n_arr)),
            pltpu.SemaphoreType.DMA((n_arr,))]


def _exchange_chips_phases(in_refs, out_refs, send_sems, recv_sems, local_sems):
    n_arr = len(in_refs)
    x, y, c = lax.axis_index("x"), lax.axis_index("y"), lax.axis_index("c")
    me = 2 * x + y

    def copies():
        out = [pltpu.make_async_copy(in_refs[a].at[me], out_refs[a].at[me], local_sems.at[a]) for a in range(n_arr)]
        for k in range(1, N_CHIP):
            px = 1 - x if k & 2 else x
            py = 1 - y if k & 1 else y
            for a in range(n_arr):
                out.append(pltpu.make_async_remote_copy(
                    src_ref=in_refs[a].at[2 * px + py], dst_ref=out_refs[a].at[me],
                    send_sem=send_sems.at[k - 1, a], recv_sem=recv_sems.at[k - 1, a],
                    device_id=(px, py, c), device_id_type=MESH_ID))
        return out

    def start():
        for cp in copies():
            cp.start()

    def finish():
        for cp in copies():
            cp.wait()

    return start, finish


def _row_tile(r, ccols, blocks):
    cap = max(16, (2 * 1024 * 1024) // (4 * ccols * blocks))
    return _pick(r, cap, 16)


def _pair_add(mine, theirs, my_c, name):
    _, r, ccols = mine.shape
    tr = _row_tile(r, ccols, 1)

    def body(c_ref, a_ref, b_ref, o_ref):
        o_ref[...] = (a_ref[...].astype(F32) + b_ref[...].astype(F32)).astype(o_ref.dtype)

    spec = pl.BlockSpec((None, tr, ccols), lambda q, i, c_ref: (q, i, 0))
    return pl.pallas_call(
        body, name=name,
        grid_spec=pltpu.PrefetchScalarGridSpec(
            num_scalar_prefetch=1, grid=(N_CHIP, r // tr),
            in_specs=[pl.BlockSpec((None, tr, ccols), lambda q, i, c_ref: (2 * q + c_ref[0], i, 0)), spec],
            out_specs=spec),
        out_shape=jax.ShapeDtypeStruct(theirs.shape, theirs.dtype),
        compiler_params=_params("parallel", "parallel"),
    )(my_c, mine, theirs)


def _sum_blocks(parts, name):
    nb, r, ccols = parts.shape
    tr = _row_tile(r, ccols, nb)

    def body(p_ref, o_ref):
        acc = p_ref[0].astype(F32)
        for i in range(1, nb):
            acc = acc + p_ref[i].astype(F32)
        o_ref[...] = acc

    return pl.pallas_call(
        body, name=name, grid=(r // tr,),
        in_specs=[pl.BlockSpec((nb, tr, ccols), lambda i: (0, i, 0))],
        out_specs=pl.BlockSpec((tr, ccols), lambda i: (i, 0)),
        out_shape=jax.ShapeDtypeStruct((r, ccols), F32),
        compiler_params=_params("parallel"),
    )(parts)


@jax.custom_vjp
def replicated(vec):
    return vec


def _replicated_fwd(vec):
    return vec, None


def _replicated_bwd(_, ct):
    return (_sum_blocks(_allgather([ct], "small_grad_allgather")[0], "small_grad_sum"),)


replicated.defvjp(_replicated_fwd, _replicated_bwd)


def _adamw(w, g, m, v, name):
    rows, cols = w.shape
    tr = _pick(rows, 256, 8) if rows % 8 == 0 else rows

    def body(w_ref, g_ref, m_ref, v_ref, d_ref, nm_ref, nv_ref):
        g_ = g_ref[...]
        m_ = ADAM_B1 * m_ref[...] + (1.0 - ADAM_B1) * g_
        v_ = ADAM_B2 * v_ref[...] + (1.0 - ADAM_B2) * jnp.square(g_)
        m_hat = m_ / (1.0 - ADAM_B1 ** ADAM_STEP)
        v_hat = v_ / (1.0 - ADAM_B2 ** ADAM_STEP)
        d_ref[...] = -ADAM_LR * (m_hat / (jnp.sqrt(v_hat) + ADAM_EPS) + ADAM_WD * w_ref[...])
        nm_ref[...] = m_
        nv_ref[...] = v_

    spec = pl.BlockSpec((tr, cols), lambda i: (i, 0))
    return pl.pallas_call(
        body, name=name, grid=(rows // tr,), in_specs=[spec] * 4, out_specs=[spec] * 3,
        out_shape=[jax.ShapeDtypeStruct(w.shape, F32)] * 3, compiler_params=_params("parallel"),
    )(w, g, m, v)


COL_SHARDED = ("w_in", "w_uq", "w_ukv", "w_branch_a", "w_branch_b", "w_up", "w_ple")
EARLY = ("w_in",)
MID = ("w_uq", "w_ukv", "w_branch_a", "w_branch_b", "w_out")
LATE = ("w_up", "w_down", "w_ple_gate", "w_ple")
SMALL = ("attn_pre_norm", "attn_post_norm", "b_gate", "q_a_norm", "kv_a_norm", "mlp_pre_norm", "mlp_post_norm",
         "conv_b", "ple_norm", "sinks")
SMALL_COLS = 128


def _pack_rows(arrays, cols, row_mult):
    flat = jnp.concatenate([a.reshape(-1) for a in arrays])
    pad = (-flat.shape[0]) % (cols * row_mult)
    return jnp.pad(flat, (0, pad)).reshape(-1, cols)


def _unpack_small(vec, shapes):
    flat = vec.reshape(-1)
    out, off = {}, 0
    for name in SMALL:
        n = shapes[name]
        out[name] = flat[off:off + n].reshape(1, n)
        off += n + (-n) % SMALL_COLS
    return out


def _pad_lanes(t, width):
    return jnp.pad(t, [(0, 0)] * (t.ndim - 1) + [(0, width - t.shape[-1])])


def _pad_rows(t, rows):
    return jnp.pad(t, [(0, 0)] * (t.ndim - 2) + [(0, rows - t.shape[-2]), (0, 0)])


FRONT_SIZES = (512, 128, 128, 256, 128)
FRONT_BOUNDS = (0, 512, 640, 768, 1024, 1152, 1280)
PE_LANE = NOPE_DIM


def _arrange_w_in_t(wt):
    k = wt.shape[1]
    n_front = sum(FRONT_SIZES)
    front, kr, gates = wt[:n_front], wt[n_front:n_front + ROPE_DIM], wt[n_front + ROPE_DIM:]
    kr_slab = jnp.concatenate([jnp.zeros((PE_LANE, k), wt.dtype), kr,
                               jnp.zeros((HEAD_PAD - PE_LANE - ROPE_DIM, k), wt.dtype)], axis=0)
    return jnp.concatenate([front, kr_slab], axis=0), gates


def _arrange_w_uq_t(wt):
    k = wt.shape[1]
    return _pad_rows(wt.reshape(B_HEADS, NOPE_DIM + ROPE_DIM, k), HEAD_PAD).reshape(B_HEADS * HEAD_PAD, k)


def _arrange_w_ukv_t(wt):
    k = wt.shape[1]
    w = wt.reshape(B_HEADS, 2, NOPE_DIM, k)
    slabs = [_pad_rows(w[:, part], HEAD_PAD).reshape(B_HEADS * HEAD_PAD, k) for part in range(2)]
    return jnp.concatenate(slabs, axis=0)


def _rope_tables(positions, s):
    pos = positions.reshape(s, 1).astype(F32)

    def angles(dim):
        return pos * ROPE_THETA ** (-(jnp.arange(0, dim, 2, dtype=F32) / dim))

    cos_a, sin_a = jnp.cos(angles(A_HEAD_DIM)), jnp.sin(angles(A_HEAD_DIM))
    zero_a = jnp.zeros_like(sin_a)
    tables_a = [jnp.tile(jnp.concatenate(pair, axis=1), (1, LANES // A_HEAD_DIM))
                for pair in ((cos_a, cos_a), (-sin_a, zero_a), (zero_a, sin_a))]
    cos_b, sin_b = jnp.cos(angles(ROPE_DIM)), jnp.sin(angles(ROPE_DIM))
    zero_b = jnp.zeros_like(sin_b)

    def slab(first, second, fill):
        return jnp.concatenate([jnp.full((s, PE_LANE), fill, F32), first, second,
                                jnp.full((s, HEAD_PAD - PE_LANE - ROPE_DIM), fill, F32)], axis=1)

    tables_b = [slab(cos_b, cos_b, 1.0), slab(-sin_b, zero_b, 0.0), slab(zero_b, sin_b, 0.0)]
    return tables_a + tables_b


def _local_loss(wts, x, p, tables, target):
    s = x.shape[0]
    small_shapes = {n: wts[n].shape[-1] for n in SMALL}
    small_vec = _pack_rows([_pad_lanes(wts[n].reshape(1, -1), small_shapes[n] + (-small_shapes[n]) % SMALL_COLS)
                            for n in SMALL], SMALL_COLS, 8)
    sm = _unpack_small(replicated(small_vec), small_shapes)
    def shard(n):
        return wts[n].T if n in COL_SHARDED else wts[n]

    h1, gathered = prenorm_gather(
        x, sm["attn_pre_norm"], tuple([shard(n) for n in EARLY] + [_pack_rows([wts["conv_w"]], SMALL_COLS, 8)]),
        (BF16,) * len(EARLY) + (F32,))
    big = {n: g.reshape(-1, g.shape[2]) for n, g in zip(EARLY, gathered)}
    ch = wts["conv_w"].shape[1]
    conv_w = gathered[-1].reshape(N_DEV, -1)[:, :CONV_W * ch].reshape(N_DEV, CONV_W, ch)
    conv_w = conv_w.transpose(1, 0, 2).reshape(CONV_W, N_DEV * ch)

    w_front_t, w_gates_t = _arrange_w_in_t(big["w_in"])
    tables_a, tables_b = tables[:3], tables[3:]

    qa, ka, va, cqn, ckvn, kpe = proj_stage(
        "prep", _f_prep, [(h1, w_front_t, "nt", "w_front", True)], params=[sm["q_a_norm"], sm["kv_a_norm"]],
        consts=tables, splits=[FRONT_BOUNDS], out_dtypes=[BF16, BF16, BF16, BF16, BF16, F32])
    ya, mid = swa_nat(qa, ka, va, sm["sinks"].reshape(-1), tuple(shard(n) for n in MID))
    big.update({n: g.reshape(-1, g.shape[2]) for n, g in zip(MID, mid)})

    (q2,) = proj_stage("qrope", _f_qrope, [(cqn, _arrange_w_uq_t(big["w_uq"]), "nt", "w_uq", True)],
                       consts=tables_b, out_dtypes=[BF16])
    k2, v2 = proj_stage("kv", _f_kv, [(ckvn, _arrange_w_ukv_t(big["w_ukv"]), "nt", "w_ukv", True)], extra=[kpe],
                        splits=[(0, B_HEADS * HEAD_PAD, 2 * B_HEADS * HEAD_PAD), None], out_dtypes=[BF16, BF16])
    yb, late = flash_nat(q2, k2, v2, tuple(shard(n) for n in LATE))
    big.update({n: g.reshape(-1, g.shape[2]) for n, g in zip(LATE, late)})

    (mixed,) = proj_stage(
        "gate", _f_gate, [(h1, w_gates_t, "nt", "w_gates", True), (ya, big["w_branch_a"], "nt", "w_branch_a", True),
                          (yb, big["w_branch_b"], "nt", "w_branch_b", True)],
        params=[sm["b_gate"][:, :D_MODEL], sm["b_gate"][:, D_MODEL:]],
        splits=[(0, D_MODEL, 2 * D_MODEL), None, None], out_dtypes=[BF16])
    x1, h2 = proj_stage("post_attn", _f_post, [(mixed, big["w_out"], "nn", "w_out", True)], extra=[x],
                        params=[sm["attn_post_norm"], sm["mlp_pre_norm"]], out_dtypes=[F32, BF16])

    act = mlp_up(h2, big["w_up"], conv_w, sm["conv_b"])
    x2, h3 = proj_stage("post_mlp", _f_post, [(act, big["w_down"], "nn", "w_down", True)], extra=[x1],
                        params=[sm["mlp_post_norm"], sm["ple_norm"]], out_dtypes=[F32, BF16])

    (rowloss,) = proj_stage("loss", _f_out, [(h3, big["w_ple_gate"], "nn", "w_ple_gate", True),
                                             (p, big["w_ple"], "nt", "w_ple", False)], extra=[x2], consts=[target])
    return jnp.sum(rowloss)


WEIGHTS = ["attn_pre_norm", "attn_post_norm", "w_in", "b_gate", "sinks", "q_a_norm", "w_uq", "kv_a_norm", "w_ukv",
           "w_branch_a", "w_branch_b", "w_out", "mlp_pre_norm", "mlp_post_norm", "w_up", "conv_w", "conv_b",
           "w_down", "ple_norm", "w_ple_gate", "w_ple"]


def kernel(x, p, positions, attn_pre_norm, attn_post_norm, w_in, b_gate, sinks, q_a_norm, w_uq, kv_a_norm, w_ukv, w_branch_a, w_branch_b, w_out, mlp_pre_norm, mlp_post_norm, w_up, conv_w, conv_b, w_down, ple_norm, w_ple_gate, w_ple, loss_target, m_attn_pre_norm, m_attn_post_norm, m_w_in, m_b_gate, m_sinks, m_q_a_norm, m_w_uq, m_kv_a_norm, m_w_ukv, m_w_branch_a, m_w_branch_b, m_w_out, m_mlp_pre_norm, m_mlp_post_norm, m_w_up, m_conv_w, m_conv_b, m_w_down, m_ple_norm, m_w_ple_gate, m_w_ple, v_attn_pre_norm, v_attn_post_norm, v_w_in, v_b_gate, v_sinks, v_q_a_norm, v_w_uq, v_kv_a_norm, v_w_ukv, v_w_branch_a, v_w_branch_b, v_w_out, v_mlp_pre_norm, v_mlp_post_norm, v_w_up, v_conv_w, v_conv_b, v_w_down, v_ple_norm, v_w_ple_gate, v_w_ple):
    given = dict(locals())
    s = x.shape[1]
    wts = {n: given[n][0] if given[n].ndim == 3 else given[n] for n in WEIGHTS}
    tables = _rope_tables(positions, s)
    local_loss, (grads, grad_x) = jax.value_and_grad(_local_loss, argnums=(0, 1))(
        wts, x[0], p[0, 0], tables, loss_target[0])
    loss = lax.psum(local_loss, AXES)

    outs = {"grad": [], "delta": [], "m": [], "v": []}
    for n in WEIGHTS:
        shape = given[n].shape
        w2 = wts[n].reshape(-1, shape[-1])
        g2 = grads[n].reshape(w2.shape)
        delta, new_m, new_v = _adamw(w2, g2, given["m_" + n].reshape(w2.shape), given["v_" + n].reshape(w2.shape),
                                     "adamw_" + n)
        outs["grad"].append(g2.reshape(shape))
        outs["delta"].append(delta.reshape(shape))
        outs["m"].append(new_m.reshape(shape))
        outs["v"].append(new_v.reshape(shape))
    return (loss, grad_x[None], *outs["grad"], *outs["delta"], *outs["m"], *outs["v"])
```

```python
import functools

import numpy as np
import jax
import jax.numpy as jnp
from jax import lax
from jax.experimental import pallas as pl
from jax.experimental.pallas import tpu as pltpu

F32 = jnp.float32
BF16 = jnp.bfloat16
MESH_ID = pl.DeviceIdType.MESH
AXES = ("x", "y", "c")
N_DEV = 8

D_MODEL = 1024
RMS_EPS = 1e-6
ROPE_THETA = 10000.0
SWA_BLOCK = 128
A_HEADS, A_KV_HEADS, A_HEAD_DIM = 8, 2, 64
A_GROUP = A_HEADS // A_KV_HEADS
B_HEADS, Q_LORA, KV_LORA, NOPE_DIM, ROPE_DIM, V_DIM = 8, 256, 128, 64, 32, 64
D_FF = 2816
CONV_W = 3
HEAD_PAD = 128

ADAM_LR, ADAM_B1, ADAM_B2, ADAM_EPS, ADAM_WD, ADAM_STEP = 0.001, 0.9, 0.999, 1e-08, 0.01, 10

VMEM_LIMIT = 48 * 1024 * 1024
MM_TM, MM_TN, MM_TK_TOKENS = 512, 1408, 1024
MM_VMEM_BUDGET = 36 * 1024 * 1024
FLASH_T = 1024
CONV_TS = 128
CONV_CHUNK = 256


def _params(*sem):
    return pltpu.CompilerParams(dimension_semantics=sem, vmem_limit_bytes=VMEM_LIMIT)


def _pick(dim, cap, mult):
    best = None
    for t in range(mult, min(dim, cap) + 1, mult):
        if dim % t == 0:
            best = t
    return dim if best is None else best


def _divisors(dim, mult):
    return [t for t in range(mult, dim + 1, mult) if dim % t == 0] or [dim]


def _matmul_tiles(m, n, kdim, form, sizes):
    sa, sb, so = sizes
    tk = _pick(kdim, MM_TK_TOKENS, 128) if form == "tn" else kdim
    cap_m = MM_TN if form == "tn" else MM_TM
    best = None
    for tm in _divisors(m, 128):
        for tn in _divisors(n, 128):
            need = 2 * (tm * tk * sa + tk * tn * sb + tm * tn * so) + (tm * tn * 4 if tk != kdim else 0)
            if tm > cap_m or tn > MM_TN or need > MM_VMEM_BUDGET:
                continue
            if best is None or (tm * tn, tm) > (best[0] * best[1], best[0]):
                best = (tm, tn)
    return best[0], best[1], tk


def _matmul(a, b, form, *, out_dtype=F32, name):
    if form == "tn":
        (kdim, m), n = a.shape, b.shape[1]
    else:
        (m, kdim), n = a.shape, (b.shape[1] if form == "nn" else b.shape[0])
    sizes = (a.dtype.itemsize, b.dtype.itemsize, jnp.dtype(out_dtype).itemsize)
    tm, tn, tk = _matmul_tiles(m, n, kdim, form, sizes)
    nk = kdim // tk
    rows_outer = nk > 1 or (m // tm) * b.size * sizes[1] <= (n // tn) * a.size * sizes[0]

    def ij(fn):
        return (lambda i, j, k: fn(i, j, k)) if rows_outer else (lambda j, i, k: fn(i, j, k))

    a_spec = (pl.BlockSpec((tk, tm), ij(lambda i, j, k: (k, i))) if form == "tn"
              else pl.BlockSpec((tm, tk), ij(lambda i, j, k: (i, k))))
    b_spec = (pl.BlockSpec((tn, tk), ij(lambda i, j, k: (j, k))) if form == "nt"
              else pl.BlockSpec((tk, tn), ij(lambda i, j, k: (k, j))))
    dims = (((0 if form == "tn" else 1,), (1 if form == "nt" else 0,)), ((), ()))

    def product(a_ref, b_ref):
        return lax.dot_general(a_ref[...].astype(BF16), b_ref[...].astype(BF16), dims, preferred_element_type=F32)

    if nk == 1:
        def body(a_ref, b_ref, o_ref):
            o_ref[...] = product(a_ref, b_ref).astype(o_ref.dtype)

        scratch = []
    else:
        def body(a_ref, b_ref, o_ref, acc_ref):
            k = pl.program_id(2)

            @pl.when(k == 0)
            def _():
                acc_ref[...] = jnp.zeros_like(acc_ref)

            acc_ref[...] += product(a_ref, b_ref)

            @pl.when(k == nk - 1)
            def _():
                o_ref[...] = acc_ref[...].astype(o_ref.dtype)

        scratch = [pltpu.VMEM((tm, tn), F32)]

    return pl.pallas_call(
        body, name=name, grid=(m // tm, n // tn, nk) if rows_outer else (n // tn, m // tm, nk),
        in_specs=[a_spec, b_spec],
        out_specs=pl.BlockSpec((tm, tn), ij(lambda i, j, k: (i, j))),
        out_shape=jax.ShapeDtypeStruct((m, n), out_dtype),
        scratch_shapes=scratch,
        compiler_params=_params("parallel", "parallel", "arbitrary"),
    )(a, b)


def _pairs(bounds):
    return list(zip(bounds[:-1], bounds[1:]))


def _split(v, bounds):
    return [v[:, a:b] for a, b in _pairs(bounds)]


def _stage_build(name, f, tiled, params, consts, splits, ts, out_dtypes, ct_dtypes=None):
    n_t, n_p, n_c = len(tiled), len(params), len(consts)
    ct_dtypes = [t.dtype for t in tiled] if ct_dtypes is None else ct_dtypes
    s = tiled[0].shape[0]
    ts = min(ts, s)
    grid = (s // ts,)
    if splits is None:
        splits = [None] * n_t
    in_bounds = [(0, t.shape[1]) if b is None else tuple(b) for t, b in zip(tiled, splits)]

    def tile_aval(arr):
        return jax.ShapeDtypeStruct((ts, arr.shape[1]), arr.dtype)

    slab_avals = [[jax.ShapeDtypeStruct((ts, e - a), F32) for a, e in _pairs(b)]
                  for t, b in zip(tiled, in_bounds)]
    out_avals = jax.eval_shape(f, slab_avals, list(params), [tile_aval(c) for c in consts])
    out_bounds = [tuple(np.cumsum([0] + [o.shape[1] for o in slabs]).tolist()) for slabs in out_avals]
    out_dtypes = [F32] * len(out_bounds) if out_dtypes is None else out_dtypes
    out_shapes = [jax.ShapeDtypeStruct((s, b[-1]), d) for b, d in zip(out_bounds, out_dtypes)]

    def row_spec(width):
        return pl.BlockSpec((ts, width), lambda i: (i, 0))

    def par_spec(arr):
        return pl.BlockSpec(arr.shape, lambda i: (0, 0))

    in_specs = ([row_spec(t.shape[1]) for t in tiled] + [par_spec(p) for p in params]
                + [row_spec(c.shape[1]) for c in consts])

    def load(refs):
        t = [_split(r[...].astype(F32), b) for r, b in zip(refs[:n_t], in_bounds)]
        p = [r[...] for r in refs[n_t:n_t + n_p]]
        c = [r[...] for r in refs[n_t + n_p:n_t + n_p + n_c]]
        return t, p, c

    def store(refs, values, bounds):
        for ref, slabs, b in zip(refs, values, bounds):
            for v, (a, e) in zip(slabs, _pairs(b)):
                ref[:, a:e] = v.astype(ref.dtype)

    def run_fwd(tiled, params, consts):
        def body(*refs):
            t, p, c = load(refs)
            store(refs[n_t + n_p + n_c:], f(t, p, c), out_bounds)

        return pl.pallas_call(
            body, name=name + "_fwd", grid=grid, in_specs=in_specs,
            out_specs=[row_spec(b[-1]) for b in out_bounds], out_shape=out_shapes,
            compiler_params=_params("parallel"),
        )(*tiled, *params, *consts)

    def run_bwd(tiled, params, consts, cts):
        n_in = n_t + n_p + n_c
        n_o = len(out_bounds)

        def body(*refs):
            t, p, c = load(refs)
            g = [_split(r[...].astype(F32), b) for r, b in zip(refs[n_in:n_in + n_o], out_bounds)]
            _, pull = jax.vjp(lambda t_, p_: f(t_, p_, c), t, p)
            dt, dp = pull(g)
            store(refs[n_in + n_o:n_in + n_o + n_t], dt, in_bounds)
            first = pl.program_id(0) == 0
            for ref, d in zip(refs[n_in + n_o + n_t:], dp):
                @pl.when(first)
                def _(ref=ref):
                    ref[...] = jnp.zeros_like(ref)

                ref[...] += d

        res = pl.pallas_call(
            body, name=name + "_bwd", grid=grid,
            in_specs=in_specs + [row_spec(b[-1]) for b in out_bounds],
            out_specs=[row_spec(t.shape[1]) for t in tiled] + [par_spec(p) for p in params],
            out_shape=[jax.ShapeDtypeStruct(t.shape, d) for t, d in zip(tiled, ct_dtypes)]
                      + [jax.ShapeDtypeStruct(p.shape, F32) for p in params],
            compiler_params=_params("arbitrary"),
        )(*tiled, *params, *consts, *cts)
        return tuple(res[:n_t]), tuple(res[n_t:])

    return run_fwd, run_bwd


def proj_stage(name, f, projections, extra=(), params=(), consts=(), splits=None, ts=256, out_dtypes=None):
    n_z = len(projections)
    forms = [pr[2] for pr in projections]
    names = [pr[3] for pr in projections]
    need_da = [pr[4] for pr in projections]
    store = [pr[5] for pr in projections]
    extra, params, consts = tuple(extra), tuple(params), tuple(consts)

    def matmuls(a_list, w_list):
        return tuple(_matmul(a, w, form, out_dtype=dt, name=n + "_fwd")
                     for a, w, form, n, dt in zip(a_list, w_list, forms, names, store))

    def build(zs, ct=False):
        ct_dtypes = [BF16] * n_z + [e.dtype for e in extra] if ct else None
        return _stage_build(name, f, tuple(zs) + extra, params, consts, splits, ts, out_dtypes, ct_dtypes)

    @jax.custom_vjp
    def op(a_list, w_list, extra, params, consts):
        zs = matmuls(a_list, w_list)
        return tuple(build(zs)[0](zs + extra, params, consts))

    def op_fwd(a_list, w_list, extra, params, consts):
        zs = matmuls(a_list, w_list)
        return tuple(build(zs)[0](zs + extra, params, consts)), (a_list, w_list, zs, extra, params, consts)

    def op_bwd(res, cts):
        a_list, w_list, zs, extra, params, consts = res
        dt, dp = build(zs, ct=True)[1](zs + extra, params, consts, cts)
        da_list, dw_list = [], []
        for a, w, dz, form, n, want in zip(a_list, w_list, dt[:n_z], forms, names, need_da):
            if form == "nn":
                da = _matmul(dz, w, "nt", out_dtype=a.dtype, name=n + "_da") if want else jnp.zeros_like(a)
                dw = _matmul(a, dz, "tn", out_dtype=w.dtype, name=n + "_dw")
            else:
                da = _matmul(dz, w, "nn", out_dtype=a.dtype, name=n + "_da") if want else jnp.zeros_like(a)
                dw = _matmul(dz, a, "tn", out_dtype=w.dtype, name=n + "_dw")
            da_list.append(da)
            dw_list.append(dw)
        return tuple(da_list), tuple(dw_list), tuple(dt[n_z:]), dp, tuple(jnp.zeros_like(c) for c in consts)

    op.defvjp(op_fwd, op_bwd)
    return op(tuple(pr[0] for pr in projections), tuple(pr[1] for pr in projections), extra, params, consts)


def _rms(t, g):
    return t * lax.rsqrt(jnp.mean(t * t, axis=-1, keepdims=True) + RMS_EPS) * g


@functools.partial(jax.custom_vjp, nondiff_argnums=(1,))
def _lane_roll(t, shift):
    return pltpu.roll(t, shift % t.shape[-1], t.ndim - 1)


def _lane_roll_fwd(t, shift):
    return _lane_roll(t, shift), None


def _lane_roll_bwd(shift, _, ct):
    return (pltpu.roll(ct, (-shift) % ct.shape[-1], ct.ndim - 1),)


_lane_roll.defvjp(_lane_roll_fwd, _lane_roll_bwd)


def _rope_lanes(t, tables, half):
    reps = t.shape[1] // tables[0].shape[1]
    c, s_lo, s_hi = [jnp.concatenate([tb] * reps, axis=1) if reps > 1 else tb for tb in tables]
    return t * c + _lane_roll(t, -half) * s_lo + _lane_roll(t, half) * s_hi


PRENORM_TS = 256


def _prenorm_fwd_call(x, g, shards):
    s, width = x.shape
    ts = min(PRENORM_TS, s)
    nt = s // ts
    n_arr = len(shards)

    def body(*refs):
        x_ref, g_ref = refs[:2]
        o_ref = refs[2 + n_arr]
        i = pl.program_id(0)
        ag_start, ag_forward, ag_finish = _allgather_phases(refs[2:2 + n_arr], refs[3 + n_arr:3 + 2 * n_arr],
                                                            *refs[3 + 2 * n_arr:])

        @pl.when(i == 0)
        def _():
            ag_start()

        @pl.when(i == nt // 2)
        def _():
            ag_forward()

        o_ref[...] = _rms(x_ref[...], g_ref[...]).astype(o_ref.dtype)

        @pl.when(i == nt - 1)
        def _():
            ag_finish()

    return pl.pallas_call(
        body, name="prenorm_fwd", grid=(nt,),
        in_specs=[pl.BlockSpec((ts, width), lambda i: (i, 0)), pl.BlockSpec(g.shape, lambda i: (0, 0))]
                 + [HBM_SPEC] * n_arr,
        out_specs=[pl.BlockSpec((ts, width), lambda i: (i, 0))] + [HBM_SPEC] * n_arr,
        out_shape=[jax.ShapeDtypeStruct(x.shape, BF16)] + _allgather_out_shapes(shards),
        scratch_shapes=_allgather_sems(n_arr),
        compiler_params=_params("arbitrary"),
    )(x, g, *shards)


def _prenorm_bwd_call(x, g, dh, parts):
    s, width = x.shape
    ts = min(PRENORM_TS, s)
    nt = s // ts
    n_arr = len(parts)

    def body(*refs):
        x_ref, g_ref, dh_ref = refs[:3]
        dx_ref, dg_ref = refs[3 + n_arr:5 + n_arr]
        i = pl.program_id(0)
        exchange_start, exchange_finish = _exchange_chips_phases(
            refs[3:3 + n_arr], refs[5 + n_arr:5 + 2 * n_arr], *refs[5 + 2 * n_arr:])

        @pl.when(i == 0)
        def _():
            exchange_start()
            dg_ref[...] = jnp.zeros_like(dg_ref)

        _, pull = jax.vjp(_rms, x_ref[...], g_ref[...])
        dx, dg = pull(dh_ref[...].astype(F32))
        dx_ref[...] = dx
        dg_ref[...] += dg

        @pl.when(i == nt - 1)
        def _():
            exchange_finish()

    row = pl.BlockSpec((ts, width), lambda i: (i, 0))
    par = pl.BlockSpec(g.shape, lambda i: (0, 0))
    return pl.pallas_call(
        body, name="prenorm_bwd", grid=(nt,),
        in_specs=[row, par, row] + [HBM_SPEC] * n_arr,
        out_specs=[row, par] + [HBM_SPEC] * n_arr,
        out_shape=[jax.ShapeDtypeStruct(x.shape, F32), jax.ShapeDtypeStruct(g.shape, F32)]
                  + [jax.ShapeDtypeStruct(p.shape, p.dtype) for p in parts],
        scratch_shapes=_exchange_chips_sems(n_arr),
        compiler_params=_params("arbitrary"),
    )(x, g, dh, *parts)


@functools.partial(jax.custom_vjp, nondiff_argnums=(3,))
def prenorm_gather(x, g, shards, wire_dtypes):
    out = _prenorm_fwd_call(x, g, [s.astype(d) for s, d in zip(shards, wire_dtypes)])
    return out[0], tuple(out[1:])


def _prenorm_gather_fwd(x, g, shards, wire_dtypes):
    return prenorm_gather(x, g, shards, wire_dtypes), (x, g)


def _prenorm_gather_bwd(wire_dtypes, res, cts):
    x, g = res
    dh, d_gathered = cts
    out = _prenorm_bwd_call(x, g, dh, _reduce_scatter_head(d_gathered, "grads"))
    return out[0], out[1], _reduce_scatter_tail(out[2:], "grads")


prenorm_gather.defvjp(_prenorm_gather_fwd, _prenorm_gather_bwd)


def _f_prep(t, p, c):
    qa, ka, va, cq, ckv, kr = t[0]
    return [[_rope_lanes(qa, c[0:3], A_HEAD_DIM // 2)], [_rope_lanes(ka, c[0:3], A_HEAD_DIM // 2)], [va],
            [_rms(cq, p[0])], [_rms(ckv, p[1])], [_rope_lanes(kr, c[3:6], ROPE_DIM // 2)]]


def _f_qrope(t, p, c):
    return [[_rope_lanes(t[0][0], c, ROPE_DIM // 2)]]


def _f_kv(t, p, c):
    (k_nope, v), (k_pe,) = t
    return [[k_nope + jnp.concatenate([k_pe] * B_HEADS, axis=1)], [v]]


def _f_gate(t, p, c):
    (ga, gb), (pa,), (pb,) = t
    ba, bb = p
    return [[jax.nn.sigmoid(ga + ba) * pa + jax.nn.sigmoid(gb + bb) * pb]]


def _f_post(t, p, c):
    (branch,), (residual,) = t
    x1 = residual + _rms(branch, p[0])
    return [[x1], [_rms(x1, p[1])]]


def _f_out(t, p, c):
    (gate,), (emb,), (x2,) = t
    y = x2 + jax.nn.sigmoid(gate) * emb
    err = y - c[0]
    return [[0.5 * jnp.mean(err * err, axis=-1, keepdims=True)]]


def _shift_down(cur, prev, has_prev):
    full = jnp.concatenate([prev * has_prev, cur], axis=0)
    return pltpu.roll(full, 1, 0)[HALO:], pltpu.roll(full, 2, 0)[HALO:]


GELU_C = float(np.sqrt(2.0 / np.pi))
GELU_A = 0.044715
HALO = 8


def _gelu_tanh(x):
    x2 = x * x
    th = jnp.tanh(x * (GELU_C + (GELU_C * GELU_A) * x2))
    half = 0.5 + 0.5 * th
    return x * half, half + x * (0.5 - 0.5 * (th * th)) * (GELU_C + (3.0 * GELU_C * GELU_A) * x2)


def _row_sum(t):
    return jnp.sum(t, axis=0, keepdims=True)


def _conv3(cur, prev, w_ref, b_ref, has_prev):
    u1, u2 = _shift_down(cur, prev, has_prev)
    return w_ref[2:3, :] * cur + w_ref[1:2, :] * u1 + w_ref[0:1, :] * u2 + b_ref[...], u1, u2


def _mlp_act_specs(s):
    ts = min(CONV_TS, s)
    hb = ts // HALO

    def half_specs(h):
        return [pl.BlockSpec((ts, D_FF), lambda i: (i, h)),
                pl.BlockSpec((HALO, D_FF), lambda i: (jnp.maximum(i * hb - 1, 0), h))]

    def par_specs(h):
        return [pl.BlockSpec((CONV_W, D_FF), lambda i: (0, h)), pl.BlockSpec((1, D_FF), lambda i: (0, h))]

    return ts, hb, half_specs, par_specs


def _mlp_act_fwd_call(up, conv_w, conv_b):
    s = up.shape[0]
    ts, hb, half_specs, par_specs = _mlp_act_specs(s)

    def body(g_ref, gp_ref, v_ref, vp_ref, wg_ref, bg_ref, wv_ref, bv_ref, o_ref):
        has_prev = (pl.program_id(0) > 0).astype(F32)

        def chunk(cidx, carry):
            cols = pl.ds(pl.multiple_of(cidx * CONV_CHUNK, CONV_CHUNK), CONV_CHUNK)
            u_g, _, _ = _conv3(g_ref[:, cols], gp_ref[:, cols], wg_ref.at[:, cols], bg_ref.at[:, cols], has_prev)
            u_v, _, _ = _conv3(v_ref[:, cols], vp_ref[:, cols], wv_ref.at[:, cols], bv_ref.at[:, cols], has_prev)
            o_ref[:, cols] = (_gelu_tanh(u_g)[0] * u_v).astype(o_ref.dtype)
            return carry

        lax.fori_loop(0, D_FF // CONV_CHUNK, chunk, 0)

    return pl.pallas_call(
        body, name="mlp_act_fwd", grid=(s // ts,),
        in_specs=half_specs(0) + half_specs(1) + par_specs(0) + par_specs(1),
        out_specs=pl.BlockSpec((ts, D_FF), lambda i: (i, 0)),
        out_shape=jax.ShapeDtypeStruct((s, D_FF), BF16),
        compiler_params=_params("parallel"),
    )(up, up, up, up, conv_w, conv_b, conv_w, conv_b)


def _mlp_act_bwd_call(up, conv_w, conv_b, dact):
    s = up.shape[0]
    ts, hb, half_specs, par_specs = _mlp_act_specs(s)
    nt = s // ts
    ext = ts + HALO
    bf16_rows = 2 * HALO

    def next_spec(rows, h):
        return pl.BlockSpec((rows, D_FF), lambda i: (jnp.minimum((i + 1) * (ts // rows), s // rows - 1), h))

    def body(g_ref, gp_ref, gn_ref, v_ref, vp_ref, vn_ref, wg_ref, bg_ref, wv_ref, bv_ref, da_ref, dan_ref,
             dup_ref, dwg_ref, dbg_ref, dwv_ref, dbv_ref):
        i = pl.program_id(0)
        has_prev, has_next = (i > 0).astype(F32), (i < nt - 1).astype(F32)

        @pl.when(i == 0)
        def _():
            for ref in (dwg_ref, dbg_ref, dwv_ref, dbv_ref):
                ref[...] = jnp.zeros_like(ref)

        def chunk(cidx, carry):
            cols = pl.ds(pl.multiple_of(cidx * CONV_CHUNK, CONV_CHUNK), CONV_CHUNK)
            g_ext = jnp.concatenate([g_ref[:, cols], gn_ref[:, cols]], axis=0)
            v_ext = jnp.concatenate([v_ref[:, cols], vn_ref[:, cols]], axis=0)
            u_g, g1, g2 = _conv3(g_ext, gp_ref[:, cols], wg_ref.at[:, cols], bg_ref.at[:, cols], has_prev)
            u_v, v1, v2 = _conv3(v_ext, vp_ref[:, cols], wv_ref.at[:, cols], bv_ref.at[:, cols], has_prev)
            da_ext = jnp.concatenate([da_ref[:, cols].astype(F32),
                                      dan_ref[:, cols].astype(F32)[0:HALO] * has_next], axis=0)
            act_g, dact_g = _gelu_tanh(u_g)
            du_g = da_ext * u_v * dact_g
            du_v = da_ext * act_g
            for du, w_ref, x0, x1, x2, dw_ref, db_ref, lo in ((du_g, wg_ref, g_ext, g1, g2, dwg_ref, dbg_ref, 0),
                                                          (du_v, wv_ref, v_ext, v1, v2, dwv_ref, dbv_ref, D_FF)):
                d1 = pltpu.roll(du, ext - 1, 0)
                d2 = pltpu.roll(du, ext - 2, 0)
                dup = w_ref[2:3, cols] * du + w_ref[1:2, cols] * d1 + w_ref[0:1, cols] * d2
                out_cols = pl.ds(pl.multiple_of(lo + cidx * CONV_CHUNK, CONV_CHUNK), CONV_CHUNK)
                dup_ref[:, out_cols] = dup[0:ts].astype(dup_ref.dtype)
                own = du[0:ts]
                dw_ref[0:1, cols] += _row_sum(own * x2[0:ts])
                dw_ref[1:2, cols] += _row_sum(own * x1[0:ts])
                dw_ref[2:3, cols] += _row_sum(own * x0[0:ts])
                db_ref[:, cols] += _row_sum(own)
            return carry

        lax.fori_loop(0, D_FF // CONV_CHUNK, chunk, 0)

    par_out = [pl.BlockSpec((CONV_W, D_FF), lambda i: (0, 0)), pl.BlockSpec((1, D_FF), lambda i: (0, 0))]
    par_shapes = [jax.ShapeDtypeStruct((CONV_W, D_FF), F32), jax.ShapeDtypeStruct((1, D_FF), F32)]
    return pl.pallas_call(
        body, name="mlp_act_bwd", grid=(nt,),
        in_specs=(half_specs(0) + [next_spec(HALO, 0)] + half_specs(1) + [next_spec(HALO, 1)]
                  + par_specs(0) + par_specs(1)
                  + [pl.BlockSpec((ts, D_FF), lambda i: (i, 0)), next_spec(bf16_rows, 0)]),
        out_specs=[pl.BlockSpec((ts, 2 * D_FF), lambda i: (i, 0))] + par_out + par_out,
        out_shape=[jax.ShapeDtypeStruct((s, 2 * D_FF), BF16)] + par_shapes + par_shapes,
        compiler_params=_params("arbitrary"),
    )(up, up, up, up, up, up, conv_w, conv_b, conv_w, conv_b, dact, dact)


@jax.custom_vjp
def mlp_up(h2, w_up_t, conv_w, conv_b):
    return _mlp_act_fwd_call(_matmul(h2, w_up_t, "nt", out_dtype=F32, name="w_up_fwd"), conv_w, conv_b)


def _mlp_up_fwd(h2, w_up_t, conv_w, conv_b):
    up = _matmul(h2, w_up_t, "nt", out_dtype=F32, name="w_up_fwd")
    return _mlp_act_fwd_call(up, conv_w, conv_b), (h2, w_up_t, up, conv_w, conv_b)


def _mlp_up_bwd(res, dact):
    h2, w_up_t, up, conv_w, conv_b = res
    dup, dwg, dbg, dwv, dbv = _mlp_act_bwd_call(up, conv_w, conv_b, dact)
    dh2 = _matmul(dup, w_up_t, "nn", out_dtype=h2.dtype, name="w_up_da")
    dw = _matmul(dup, h2, "tn", out_dtype=w_up_t.dtype, name="w_up_dw")
    return dh2, dw, jnp.concatenate([dwg, dwv], axis=1), jnp.concatenate([dbg, dbv], axis=1)


mlp_up.defvjp(_mlp_up_fwd, _mlp_up_bwd)


SWA_ROWS = A_GROUP * SWA_BLOCK


def _swa_sink_rows(sink_ref, g):
    return jnp.concatenate([jnp.full((SWA_BLOCK, 1), sink_ref[g * A_GROUP + h], F32) for h in range(A_GROUP)], axis=0)


def _swa_operands(q_ref, kp_ref, kc_ref, vp_ref, vc_ref, sink_ref):
    groups = []
    for g in range(A_KV_HEADS):
        groups.append((_swa_stack_heads(q_ref, g), _dup_half(kp_ref[...], g), _dup_half(kc_ref[...], g),
                       _dup_half(vp_ref[...], g), _dup_half(vc_ref[...], g)))
    return groups, jnp.concatenate([_swa_sink_rows(sink_ref, g) for g in range(A_KV_HEADS)], axis=0)


def _swa_probs(groups, sink, prev_off):
    scale = A_HEAD_DIM ** -0.5
    sp = jnp.concatenate([lax.dot_general(gr[0], gr[1], NT_DIMS, preferred_element_type=F32) for gr in groups], axis=0)
    sc = jnp.concatenate([lax.dot_general(gr[0], gr[2], NT_DIMS, preferred_element_type=F32) for gr in groups], axis=0)
    qi = lax.broadcasted_iota(jnp.int32, sp.shape, 0) & (SWA_BLOCK - 1)
    kj = lax.broadcasted_iota(jnp.int32, sp.shape, 1)
    in_cur = kj <= qi
    sw = jnp.where(in_cur, sc, jnp.where(kj > qi + prev_off, sp, -jnp.inf)) * scale
    m = jnp.maximum(jnp.max(sw, axis=-1, keepdims=True), sink)
    e, es = jnp.exp(sw - m), jnp.exp(sink - m)
    den = jnp.sum(e, axis=-1, keepdims=True) + es
    return e / den, in_cur, es / den


def _swa_split(t, in_cur):
    cur = jnp.where(in_cur, t, 0.0)
    return t - cur, cur


MLA_SCALE = (NOPE_DIM + ROPE_DIM) ** -0.5
EXP2_SCALE = MLA_SCALE * float(np.log2(np.e))
NT_DIMS = (((1,), (1,)), ((), ()))
TN_DIMS = (((0,), (0,)), ((), ()))


LANES = 128
HALF = LANES // 2


def _low_half(shape):
    return lax.broadcasted_iota(jnp.int32, shape, len(shape) - 1) < HALF


def _dup_half(x, g):
    xf = x.astype(F32)
    keep = _low_half(xf.shape) if g == 0 else jnp.logical_not(_low_half(xf.shape))
    xm = jnp.where(keep, xf, 0.0)
    return (xm + pltpu.roll(xm, HALF, 1)).astype(x.dtype)


def _fold_half(r, g):
    total = r + pltpu.roll(r, HALF, 1)
    keep = _low_half(r.shape) if g == 0 else jnp.logical_not(_low_half(r.shape))
    return jnp.where(keep, total, 0.0)


def _swa_stack_heads(ref, g):
    parts = []
    for tile in range(2):
        slab = ref[:, (2 * g + tile) * LANES:(2 * g + tile + 1) * LANES]
        low = _low_half(slab.shape)
        parts += [jnp.where(low, slab, jnp.zeros_like(slab)), jnp.where(low, jnp.zeros_like(slab), slab)]
    return jnp.concatenate(parts, axis=0)


def _swa_unstack_heads(ref, g, rows):
    for tile in range(2):
        a = rows[(2 * tile) * SWA_BLOCK:(2 * tile + 1) * SWA_BLOCK]
        b = rows[(2 * tile + 1) * SWA_BLOCK:(2 * tile + 2) * SWA_BLOCK]
        ref[:, (2 * g + tile) * LANES:(2 * g + tile + 1) * LANES] = jnp.where(_low_half(a.shape), a, b).astype(ref.dtype)


def _swa_nat_specs():
    blk = SWA_BLOCK
    q_spec = pl.BlockSpec((blk, A_HEADS * A_HEAD_DIM), lambda n: (n, 0))
    prev_spec = pl.BlockSpec((blk, LANES), lambda n: (jnp.maximum(n - 1, 0), 0))
    cur_spec = pl.BlockSpec((blk, LANES), lambda n: (n, 0))
    return q_spec, prev_spec, cur_spec, pl.BlockSpec(memory_space=pltpu.SMEM)


def _swa_nat_fwd_call(q, k, v, sinks, shards):
    s = q.shape[0]
    nblk = s // SWA_BLOCK
    n_arr = len(shards)
    q_spec, prev_spec, cur_spec, sink_spec = _swa_nat_specs()

    def body(*refs):
        q_ref, kp_ref, kc_ref, vp_ref, vc_ref, sink_ref = refs[:6]
        o_ref = refs[6 + n_arr]
        n = pl.program_id(0)
        ag_start, ag_forward, ag_finish = _allgather_phases(refs[6:6 + n_arr], refs[7 + n_arr:7 + 2 * n_arr],
                                                            *refs[7 + 2 * n_arr:])

        @pl.when(n == 0)
        def _():
            ag_start()

        @pl.when(n == nblk // 2)
        def _():
            ag_forward()

        prev_off = jnp.where(n > 0, 0, SWA_BLOCK)
        groups, sink = _swa_operands(q_ref, kp_ref, kc_ref, vp_ref, vc_ref, sink_ref)
        p, in_cur, _ = _swa_probs(groups, sink, prev_off)
        ppb, pcb = [t.astype(BF16) for t in _swa_split(p, in_cur)]
        for g, (_, _, _, vp, vc) in enumerate(groups):
            rows = slice(g * SWA_ROWS, (g + 1) * SWA_ROWS)
            out = (jnp.dot(ppb[rows], vp, preferred_element_type=F32)
                   + jnp.dot(pcb[rows], vc, preferred_element_type=F32))
            _swa_unstack_heads(o_ref, g, out)

        @pl.when(n == nblk - 1)
        def _():
            ag_finish()

    return pl.pallas_call(
        body, name="swa_fwd", grid=(nblk,),
        in_specs=[q_spec, prev_spec, cur_spec, prev_spec, cur_spec, sink_spec] + [HBM_SPEC] * n_arr,
        out_specs=[q_spec] + [HBM_SPEC] * n_arr,
        out_shape=[jax.ShapeDtypeStruct(q.shape, BF16)] + _allgather_out_shapes(shards),
        scratch_shapes=_allgather_sems(n_arr),
        compiler_params=_params("arbitrary"),
    )(q, k, k, v, v, sinks, *shards)


def _swa_nat_bwd_call(q, k, v, sinks, do, parts):
    s = q.shape[0]
    nblk = s // SWA_BLOCK
    n_arr = len(parts)
    q_spec, prev_spec, cur_spec, sink_spec = _swa_nat_specs()
    scale = A_HEAD_DIM ** -0.5
    dsink_spec = pl.BlockSpec((A_KV_HEADS, SWA_ROWS, 1), lambda n: (0, 0, 0))

    def body(*refs):
        q_ref, kp_ref, kc_ref, vp_ref, vc_ref, sink_ref, do_ref = refs[:7]
        dq_ref, dkp_ref, dkc_ref, dvp_ref, dvc_ref, dsink_ref = refs[7 + n_arr:13 + n_arr]
        n = pl.program_id(0)
        exchange_start, exchange_finish = _exchange_chips_phases(
            refs[7:7 + n_arr], refs[13 + n_arr:13 + 2 * n_arr], *refs[13 + 2 * n_arr:])

        @pl.when(n == 0)
        def _():
            exchange_start()
        prev_off = jnp.where(n > 0, 0, SWA_BLOCK)

        @pl.when(n == 0)
        def _():
            dsink_ref[...] = jnp.zeros_like(dsink_ref)

        groups, sink = _swa_operands(q_ref, kp_ref, kc_ref, vp_ref, vc_ref, sink_ref)
        dobs = [_swa_stack_heads(do_ref, g) for g in range(A_KV_HEADS)]
        p, in_cur, ps = _swa_probs(groups, sink, prev_off)
        ppb, pcb = [t.astype(BF16) for t in _swa_split(p, in_cur)]

        def per_group(fn):
            return jnp.concatenate([fn(g, slice(g * SWA_ROWS, (g + 1) * SWA_ROWS)) for g in range(A_KV_HEADS)], axis=0)

        out = per_group(lambda g, rows: jnp.dot(ppb[rows], groups[g][3], preferred_element_type=F32)
                        + jnp.dot(pcb[rows], groups[g][4], preferred_element_type=F32))
        delta = jnp.sum(jnp.concatenate(dobs, axis=0).astype(F32) * out, axis=-1, keepdims=True)
        dp = jnp.where(in_cur,
                       per_group(lambda g, rows: lax.dot_general(dobs[g], groups[g][4], NT_DIMS,
                                                                 preferred_element_type=F32)),
                       per_group(lambda g, rows: lax.dot_general(dobs[g], groups[g][3], NT_DIMS,
                                                                 preferred_element_type=F32)))
        dsp, dsc = [t.astype(BF16) for t in _swa_split(p * (dp - delta), in_cur)]
        dsink_ref[...] += (-ps * delta).reshape(dsink_ref.shape)
        totals = [jnp.zeros((SWA_BLOCK, LANES), F32) for _ in range(4)]
        for g, (qb, kp, kc, _, _) in enumerate(groups):
            rows = slice(g * SWA_ROWS, (g + 1) * SWA_ROWS)
            dq = (jnp.dot(dsp[rows], kp, preferred_element_type=F32)
                  + jnp.dot(dsc[rows], kc, preferred_element_type=F32)) * scale
            _swa_unstack_heads(dq_ref, g, dq)
            pieces = [lax.dot_general(dsp[rows], qb, TN_DIMS, preferred_element_type=F32) * scale,
                      lax.dot_general(dsc[rows], qb, TN_DIMS, preferred_element_type=F32) * scale,
                      lax.dot_general(ppb[rows], dobs[g], TN_DIMS, preferred_element_type=F32),
                      lax.dot_general(pcb[rows], dobs[g], TN_DIMS, preferred_element_type=F32)]
            totals = [tot + _fold_half(r, g) for tot, r in zip(totals, pieces)]
        dkp_ref[...], dkc_ref[...], dvp_ref[...], dvc_ref[...] = totals

        @pl.when(n == nblk - 1)
        def _():
            exchange_finish()

    kv_shape = jax.ShapeDtypeStruct(k.shape, F32)
    return pl.pallas_call(
        body, name="swa_bwd", grid=(nblk,),
        in_specs=[q_spec, prev_spec, cur_spec, prev_spec, cur_spec, sink_spec, q_spec] + [HBM_SPEC] * n_arr,
        out_specs=[q_spec, cur_spec, cur_spec, cur_spec, cur_spec, dsink_spec] + [HBM_SPEC] * n_arr,
        out_shape=[jax.ShapeDtypeStruct(q.shape, q.dtype), kv_shape, kv_shape, kv_shape, kv_shape,
                   jax.ShapeDtypeStruct((A_KV_HEADS, SWA_ROWS, 1), F32)]
                  + [jax.ShapeDtypeStruct(p.shape, p.dtype) for p in parts],
        scratch_shapes=_exchange_chips_sems(n_arr),
        compiler_params=_params("arbitrary"),
    )(q, k, k, v, v, sinks, do, *parts)


@jax.custom_vjp
def swa_nat(q, k, v, sinks, shards):
    out = _swa_nat_fwd_call(q, k, v, sinks, [s.astype(BF16) for s in shards])
    return out[0], tuple(out[1:])


def _swa_nat_fwd(q, k, v, sinks, shards):
    out = _swa_nat_fwd_call(q, k, v, sinks, [s.astype(BF16) for s in shards])
    return (out[0], tuple(out[1:])), (q, k, v, sinks)


def _swa_nat_bwd(res, cts):
    q, k, v, sinks = res
    do, d_gathered = cts
    out = _swa_nat_bwd_call(q, k, v, sinks, do, _reduce_scatter_head(d_gathered, "mid_grads"))
    dq, dkp, dkc, dvp, dvc, dsink = out[:6]

    def fold(prev_part, cur_part):
        shifted = jnp.concatenate([prev_part[SWA_BLOCK:], jnp.zeros_like(prev_part[:SWA_BLOCK])], axis=0)
        return (cur_part + shifted).astype(k.dtype)

    dsinks = jnp.sum(dsink.reshape(A_HEADS, SWA_BLOCK), axis=1)
    return dq, fold(dkp, dkc), fold(dvp, dvc), dsinks, _reduce_scatter_tail(out[6:], "mid_grads")


swa_nat.defvjp(_swa_nat_fwd, _swa_nat_bwd)

N_PAIR = B_HEADS // 2


def _flash_nat_fwd_call(q, k, v, shards):
    s = q.shape[0]
    t = min(FLASH_T, s)
    nb = s // t
    d = LANES
    n_arr = len(shards)

    def body(*refs):
        q_ref, k_ref, v_ref = refs[:3]
        shard_refs = refs[3:3 + n_arr]
        o_ref, lse_ref = refs[3 + n_arr:5 + n_arr]
        gathered_refs = refs[5 + n_arr:5 + 2 * n_arr]
        vt_ref, m_ref, l_ref, acc_ref = refs[5 + 2 * n_arr:9 + 2 * n_arr]
        pair, i = pl.program_id(0), pl.program_id(1)
        ag_start, ag_forward, ag_finish = _allgather_phases(shard_refs, gathered_refs, *refs[9 + 2 * n_arr:])

        @pl.when((pair == 0) & (i == 0))
        def _():
            ag_start()

        @pl.when((pair == N_PAIR // 2) & (i == 0))
        def _():
            ag_forward()

        @pl.when(i == 0)
        def _():
            for hh in range(2):
                for chunk in range(nb):
                    rows = slice(chunk * t, (chunk + 1) * t)
                    vt_ref[hh, :, rows] = v_ref[rows, hh * d:(hh + 1) * d].T

        m_ref[...] = jnp.full_like(m_ref, -jnp.inf)
        l_ref[...] = jnp.zeros_like(l_ref)
        acc_ref[...] = jnp.zeros_like(acc_ref)

        def step(j, on_diagonal):
            keys = pl.ds(pl.multiple_of(j * t, t), t)
            scores = [lax.dot_general(k_ref[keys, hh * d:(hh + 1) * d], q_ref[:, hh * d:(hh + 1) * d], NT_DIMS,
                                      preferred_element_type=F32) for hh in range(2)]
            for hh in range(2):
                sc_t = scores[hh]
                if on_diagonal:
                    key = lax.broadcasted_iota(jnp.int32, (t, t), 0)
                    qry = lax.broadcasted_iota(jnp.int32, (t, t), 1)
                    sc_t = jnp.where(qry >= key, sc_t, -jnp.inf)
                m_old = m_ref[hh]
                m_new = jnp.maximum(m_old, jnp.max(sc_t, axis=0, keepdims=True))
                alpha = jnp.exp2((m_old - m_new) * EXP2_SCALE)
                p_t = jnp.exp2((sc_t - m_new) * EXP2_SCALE)
                l_ref[hh] = alpha * l_ref[hh] + jnp.sum(p_t, axis=0, keepdims=True)
                acc_ref[hh] = alpha * acc_ref[hh] + jnp.dot(vt_ref[hh, :, keys], p_t.astype(BF16),
                                                            preferred_element_type=F32)
                m_ref[hh] = m_new

        def below(j, carry):
            step(j, False)
            return carry

        lax.fori_loop(0, i, below, 0)
        step(i, True)
        outs =[(acc_ref[hh] / l_ref[hh]).T for hh in range(2)]
        for hh in range(2):
            lse_ref[hh] = m_ref[hh] * EXP2_SCALE + jnp.log2(l_ref[hh])
        o_ref[...] = (outs[0] + pltpu.roll(outs[1], HALF, 1)).astype(o_ref.dtype)

        @pl.when((pair == N_PAIR - 1) & (i == nb - 1))
        def _():
            ag_finish()

    return pl.pallas_call(
        body, name="mla_fwd", grid=(N_PAIR, nb),
        in_specs=[pl.BlockSpec((t, 2 * d), lambda p, i: (i, p)),
                  pl.BlockSpec((s, 2 * d), lambda p, i: (0, p)),
                  pl.BlockSpec((s, 2 * d), lambda p, i: (0, p))] + [HBM_SPEC] * n_arr,
        out_specs=[pl.BlockSpec((t, d), lambda p, i: (i, p)),
                   pl.BlockSpec((2, 1, t), lambda p, i: (p, 0, i))] + [HBM_SPEC] * n_arr,
        out_shape=[jax.ShapeDtypeStruct((s, N_PAIR * d), BF16), jax.ShapeDtypeStruct((B_HEADS, 1, s), F32)]
                  + _allgather_out_shapes(shards),
        scratch_shapes=[pltpu.VMEM((2, d, s), BF16), pltpu.VMEM((2, 1, t), F32), pltpu.VMEM((2, 1, t), F32),
                        pltpu.VMEM((2, d, t), F32)] + _allgather_sems(n_arr),
        compiler_params=_params("arbitrary", "arbitrary"),
    )(q, k, v, *shards)


def _flash_nat_delta_call(o, do):
    s, w = o.shape
    t = min(FLASH_T, s)

    def body(o_ref, do_ref, out_ref):
        prod = o_ref[...].astype(F32) * do_ref[...].astype(F32)
        lane = lax.broadcasted_iota(jnp.int32, (w, LANES), 0) // V_DIM
        head = lax.broadcasted_iota(jnp.int32, (w, LANES), 1)
        out_ref[...] = jnp.dot(prod, (lane == head).astype(F32), precision=lax.Precision.HIGHEST,
                               preferred_element_type=F32)

    spec = pl.BlockSpec((t, w), lambda i: (i, 0))
    return pl.pallas_call(
        body, name="mla_delta", grid=(s // t,), in_specs=[spec, spec],
        out_specs=pl.BlockSpec((t, LANES), lambda i: (i, 0)),
        out_shape=jax.ShapeDtypeStruct((s, LANES), F32), compiler_params=_params("parallel"),
    )(o, do)


def _flash_nat_bwd_call(q, k, v, lse_row, delta_row, do, parts):
    s = q.shape[0]
    t = min(FLASH_T, s)
    nb = s // t
    d = LANES
    n_arr = len(parts)

    def body(*refs):
        q_ref, k_ref, v_ref, lse_ref, delta_ref, do_ref = refs[:6]
        part_refs = refs[6:6 + n_arr]
        dq_ref, dk_ref, dv_ref = refs[6 + n_arr:9 + n_arr]
        received_refs = refs[9 + n_arr:9 + 2 * n_arr]
        dq_acc, dk_acc, dv_acc = refs[9 + 2 * n_arr:12 + 2 * n_arr]
        pair, j = pl.program_id(0), pl.program_id(1)
        exchange_start, exchange_finish = _exchange_chips_phases(part_refs, received_refs, *refs[12 + 2 * n_arr:])

        @pl.when((pair == 0) & (j == 0))
        def _():
            exchange_start()

        @pl.when(j == 0)
        def _():
            dq_acc[...] = jnp.zeros_like(dq_acc)

        for hh in range(2):
            kb, vb = k_ref[:, hh * d:(hh + 1) * d], v_ref[:, hh * d:(hh + 1) * d]
            dk_acc[...] = jnp.zeros_like(dk_acc)
            dv_acc[...] = jnp.zeros_like(dv_acc)

            def step(i, on_diagonal, hh=hh, kb=kb, vb=vb):
                rows = pl.ds(pl.multiple_of(i * t, t), t)
                qb = q_ref[rows, hh * d:(hh + 1) * d]
                do_pair = do_ref[rows, :].astype(F32)
                do_h = do_pair if hh == 0 else pltpu.roll(do_pair, HALF, 1)
                dob = jnp.where(_low_half(do_h.shape), do_h, 0.0).astype(BF16)
                sc_t = lax.dot_general(kb, qb, NT_DIMS, preferred_element_type=F32)
                p_t = jnp.exp2(sc_t * EXP2_SCALE - lse_ref[hh, :, rows])
                if on_diagonal:
                    key = lax.broadcasted_iota(jnp.int32, (t, t), 0)
                    qry = lax.broadcasted_iota(jnp.int32, (t, t), 1)
                    p_t = jnp.where(qry >= key, p_t, 0.0)
                dp_t = lax.dot_general(vb, dob, NT_DIMS, preferred_element_type=F32)
                ds_t = (p_t * (dp_t - delta_ref[hh, :, rows])).astype(BF16)
                dv_acc[...] += jnp.dot(p_t.astype(BF16), dob, preferred_element_type=F32)
                dk_acc[...] += jnp.dot(ds_t, qb, preferred_element_type=F32)
                dq_acc[hh, rows, :] += lax.dot_general(ds_t, kb, TN_DIMS, preferred_element_type=F32)

            def above(i, carry, step=step):
                step(i, False)
                return carry

            step(j, True)
            lax.fori_loop(j + 1, nb, above, 0)
            dk_ref[:, hh * d:(hh + 1) * d] = (dk_acc[...] * MLA_SCALE).astype(dk_ref.dtype)
            dv_ref[:, hh * d:(hh + 1) * d] = dv_acc[...].astype(dv_ref.dtype)

        @pl.when(j == nb - 1)
        def _():
            for hh in range(2):
                dq_ref[:, hh * d:(hh + 1) * d] = (dq_acc[hh] * MLA_SCALE).astype(dq_ref.dtype)

        @pl.when((pair == N_PAIR - 1) & (j == nb - 1))
        def _():
            exchange_finish()

    full_spec = pl.BlockSpec((s, 2 * d), lambda p, j: (0, p))
    tile_spec = pl.BlockSpec((t, 2 * d), lambda p, j: (j, p))
    row_spec = pl.BlockSpec((2, 1, s), lambda p, j: (p, 0, 0))
    return pl.pallas_call(
        body, name="mla_bwd", grid=(N_PAIR, nb),
        in_specs=[full_spec, tile_spec, tile_spec, row_spec, row_spec, pl.BlockSpec((s, d), lambda p, j: (0, p))]
                 + [HBM_SPEC] * n_arr,
        out_specs=[full_spec, tile_spec, tile_spec] + [HBM_SPEC] * n_arr,
        out_shape=[jax.ShapeDtypeStruct(q.shape, q.dtype)] * 3 + [jax.ShapeDtypeStruct(p.shape, p.dtype) for p in parts],
        scratch_shapes=[pltpu.VMEM((2, s, d), F32), pltpu.VMEM((t, d), F32), pltpu.VMEM((t, d), F32)]
                       + _exchange_chips_sems(n_arr),
        compiler_params=_params("arbitrary", "arbitrary"),
    )(q, k, v, lse_row, delta_row, do, *parts)


def _reduce_scatter_head(cts, tag):
    received = _exchange_sibling(list(cts), tag + "_exchange_sibling")
    my_c = lax.axis_index("c").astype(jnp.int32).reshape(1)
    return [_pair_add(m, r, my_c, "%s_pair_add_%d" % (tag, i)) for i, (m, r) in enumerate(zip(cts, received))]


def _reduce_scatter_tail(chip_parts, tag):
    return tuple(_sum_blocks(r, "%s_sum_%d" % (tag, i)) for i, r in enumerate(chip_parts))


@jax.custom_vjp
def flash_nat(q, k, v, shards):
    out = _flash_nat_fwd_call(q, k, v, [s.astype(BF16) for s in shards])
    return out[0], tuple(out[2:])


def _flash_nat_fwd(q, k, v, shards):
    out = _flash_nat_fwd_call(q, k, v, [s.astype(BF16) for s in shards])
    return (out[0], tuple(out[2:])), (q, k, v, out[0], out[1])


def _flash_nat_bwd(res, cts):
    q, k, v, o, lse = res
    do, d_gathered = cts
    delta = _flash_nat_delta_call(o, do)[:, :B_HEADS].T.reshape(B_HEADS, 1, q.shape[0])
    out = _flash_nat_bwd_call(q, k, v, lse, delta, do, _reduce_scatter_head(d_gathered, "mlp_grads"))
    return out[0], out[1], out[2], _reduce_scatter_tail(out[3:], "mlp_grads")


flash_nat.defvjp(_flash_nat_fwd, _flash_nat_bwd)


HBM_SPEC = pl.BlockSpec(memory_space=pltpu.HBM)


def _allgather(shards, name):
    n_arr = len(shards)

    def body(*refs):
        start, forward, finish = _allgather_phases(refs[:n_arr], refs[n_arr:2 * n_arr], *refs[2 * n_arr:])
        start()
        forward()
        finish()

    return pl.pallas_call(
        body, name=name, out_shape=_allgather_out_shapes(shards),
        in_specs=[HBM_SPEC] * n_arr, out_specs=[HBM_SPEC] * n_arr,
        scratch_shapes=_allgather_sems(n_arr),
    )(*shards)


def _allgather_out_shapes(shards):
    return [jax.ShapeDtypeStruct((N_DEV,) + s.shape, s.dtype) for s in shards]


def _allgather_sems(n_arr):
    return [pltpu.SemaphoreType.DMA((7, n_arr)), pltpu.SemaphoreType.DMA((7, n_arr)), pltpu.SemaphoreType.DMA((n_arr,))]


def _allgather_phases(x_refs, out_refs, send_sems, recv_sems, local_sems):
    arrays = range(len(x_refs))
    x, y, c = lax.axis_index("x"), lax.axis_index("y"), lax.axis_index("c")
    me, sibling = (x, y, c), (x, y, 1 - c)
    chips = [(1 - x, y), (x, 1 - y), (1 - x, 1 - y)]

    def rows(a, px, py, pc):
        return out_refs[a].at[4 * px + 2 * py + pc]

    def copy(a, k, block, to, src=None):
        return pltpu.make_async_remote_copy(
            src_ref=rows(a, *block) if src is None else src, dst_ref=rows(a, *block),
            send_sem=send_sems.at[k, a], recv_sem=recv_sems.at[k, a], device_id=to, device_id_type=MESH_ID)

    def mine():
        return [pltpu.make_async_copy(x_refs[a], rows(a, *me), local_sems.at[a]) for a in arrays]

    def first():
        return [cp for a in arrays for cp in
                [copy(a, 0, me, sibling, src=x_refs[a])]
                + [copy(a, 1 + j, me, (*chip, c), src=x_refs[a]) for j, chip in enumerate(chips)]]

    def passed():
        return [copy(a, 4 + j, (*chip, c), sibling) for j, chip in enumerate(chips) for a in arrays]

    def start():
        for cp in mine() + first():
            cp.start()

    def forward():
        for j, chip in enumerate(chips):
            for a in arrays:
                copy(a, 1 + j, (*chip, c), me).wait_recv()
                copy(a, 4 + j, (*chip, c), sibling).start()

    def finish():
        for a in arrays:
            copy(a, 0, sibling, me).wait_recv()
        for j, chip in enumerate(chips):
            for a in arrays:
                copy(a, 4 + j, (*chip, 1 - c), me).wait_recv()
        for cp in first() + passed():
            cp.wait_send()
        for cp in mine():
            cp.wait()

    return start, forward, finish


N_CHIP = 4


def _exchange_sibling(parts, name):
    n_arr = len(parts)

    def body(*refs):
        in_refs, recv_refs = refs[:n_arr], refs[n_arr:2 * n_arr]
        send_sems, recv_sems = refs[2 * n_arr:]
        x, y, c = lax.axis_index("x"), lax.axis_index("y"), lax.axis_index("c")
        copies = []
        for a in range(n_arr):
            for q in range(N_CHIP):
                copies.append(pltpu.make_async_remote_copy(
                    src_ref=in_refs[a].at[2 * q + 1 - c], dst_ref=recv_refs[a].at[q],
                    send_sem=send_sems.at[q, a], recv_sem=recv_sems.at[q, a],
                    device_id=(x, y, 1 - c), device_id_type=MESH_ID))
        for cp in copies:
            cp.start()
        for cp in copies:
            cp.wait()

    return pl.pallas_call(
        body, name=name, out_shape=[jax.ShapeDtypeStruct((N_CHIP,) + p.shape[1:], p.dtype) for p in parts],
        in_specs=[HBM_SPEC] * n_arr, out_specs=[HBM_SPEC] * n_arr,
        scratch_shapes=[pltpu.SemaphoreType.DMA((N_CHIP, n_arr)), pltpu.SemaphoreType.DMA((N_CHIP, n_arr))],
    )(*parts)


def _exchange_chips_sems(n_arr):
    return [pltpu.SemaphoreType.DMA((N_CHIP - 1, n_arr)), pltpu.SemaphoreType.DMA((N_CHIP - 1, n_arr)),
            pltpu.SemaphoreType.DMA((n_arr,))]


def _exchange_chips_phases(in_refs, out_refs, send_sems, recv_sems, local_sems):
    n_arr = len(in_refs)
    x, y, c = lax.axis_index("x"), lax.axis_index("y"), lax.axis_index("c")
    me = 2 * x + y

    def copies():
        out = [pltpu.make_async_copy(in_refs[a].at[me], out_refs[a].at[me], local_sems.at[a]) for a in range(n_arr)]
        for k in range(1, N_CHIP):
            px = 1 - x if k & 2 else x
            py = 1 - y if k & 1 else y
            for a in range(n_arr):
                out.append(pltpu.make_async_remote_copy(
                    src_ref=in_refs[a].at[2 * px + py], dst_ref=out_refs[a].at[me],
                    send_sem=send_sems.at[k - 1, a], recv_sem=recv_sems.at[k - 1, a],
                    device_id=(px, py, c), device_id_type=MESH_ID))
        return out

    def start():
        for cp in copies():
            cp.start()

    def finish():
        for cp in copies():
            cp.wait()

    return start, finish


def _row_tile(r, ccols, blocks):
    cap = max(16, (2 * 1024 * 1024) // (4 * ccols * blocks))
    return _pick(r, cap, 16)


def _pair_add(mine, theirs, my_c, name):
    _, r, ccols = mine.shape
    tr = _row_tile(r, ccols, 1)

    def body(c_ref, a_ref, b_ref, o_ref):
        o_ref[...] = (a_ref[...].astype(F32) + b_ref[...].astype(F32)).astype(o_ref.dtype)

    spec = pl.BlockSpec((None, tr, ccols), lambda q, i, c_ref: (q, i, 0))
    return pl.pallas_call(
        body, name=name,
        grid_spec=pltpu.PrefetchScalarGridSpec(
            num_scalar_prefetch=1, grid=(N_CHIP, r // tr),
            in_specs=[pl.BlockSpec((None, tr, ccols), lambda q, i, c_ref: (2 * q + c_ref[0], i, 0)), spec],
            out_specs=spec),
        out_shape=jax.ShapeDtypeStruct(theirs.shape, theirs.dtype),
        compiler_params=_params("parallel", "parallel"),
    )(my_c, mine, theirs)


def _sum_blocks(parts, name):
    nb, r, ccols = parts.shape
    tr = _row_tile(r, ccols, nb)

    def body(p_ref, o_ref):
        acc = p_ref[0].astype(F32)
        for i in range(1, nb):
            acc = acc + p_ref[i].astype(F32)
        o_ref[...] = acc

    return pl.pallas_call(
        body, name=name, grid=(r // tr,),
        in_specs=[pl.BlockSpec((nb, tr, ccols), lambda i: (0, i, 0))],
        out_specs=pl.BlockSpec((tr, ccols), lambda i: (i, 0)),
        out_shape=jax.ShapeDtypeStruct((r, ccols), F32),
        compiler_params=_params("parallel"),
    )(parts)


@jax.custom_vjp
def replicated(vec):
    return vec


def _replicated_fwd(vec):
    return vec, None


def _replicated_bwd(_, ct):
    return (_sum_blocks(_allgather([ct], "small_grad_allgather")[0], "small_grad_sum"),)


replicated.defvjp(_replicated_fwd, _replicated_bwd)


def _adamw(w, g, m, v, name):
    rows, cols = w.shape
    tr = _pick(rows, 256, 8) if rows % 8 == 0 else rows

    def body(w_ref, g_ref, m_ref, v_ref, d_ref, nm_ref, nv_ref):
        g_ = g_ref[...]
        m_ = ADAM_B1 * m_ref[...] + (1.0 - ADAM_B1) * g_
        v_ = ADAM_B2 * v_ref[...] + (1.0 - ADAM_B2) * jnp.square(g_)
        m_hat = m_ / (1.0 - ADAM_B1 ** ADAM_STEP)
        v_hat = v_ / (1.0 - ADAM_B2 ** ADAM_STEP)
        d_ref[...] = -ADAM_LR * (m_hat / (jnp.sqrt(v_hat) + ADAM_EPS) + ADAM_WD * w_ref[...])
        nm_ref[...] = m_
        nv_ref[...] = v_

    spec = pl.BlockSpec((tr, cols), lambda i: (i, 0))
    return pl.pallas_call(
        body, name=name, grid=(rows // tr,), in_specs=[spec] * 4, out_specs=[spec] * 3,
        out_shape=[jax.ShapeDtypeStruct(w.shape, F32)] * 3, compiler_params=_params("parallel"),
    )(w, g, m, v)


COL_SHARDED = ("w_in", "w_uq", "w_ukv", "w_branch_a", "w_branch_b", "w_up", "w_ple")
EARLY = ("w_in",)
MID = ("w_uq", "w_ukv", "w_branch_a", "w_branch_b", "w_out")
LATE = ("w_up", "w_down", "w_ple_gate", "w_ple")
SMALL = ("attn_pre_norm", "attn_post_norm", "b_gate", "q_a_norm", "kv_a_norm", "mlp_pre_norm", "mlp_post_norm",
         "conv_b", "ple_norm", "sinks")
SMALL_COLS = 128


def _pack_rows(arrays, cols, row_mult):
    flat = jnp.concatenate([a.reshape(-1) for a in arrays])
    pad = (-flat.shape[0]) % (cols * row_mult)
    return jnp.pad(flat, (0, pad)).reshape(-1, cols)


def _unpack_small(vec, shapes):
    flat = vec.reshape(-1)
    out, off = {}, 0
    for name in SMALL:
        n = shapes[name]
        out[name] = flat[off:off + n].reshape(1, n)
        off += n + (-n) % SMALL_COLS
    return out


def _pad_lanes(t, width):
    return jnp.pad(t, [(0, 0)] * (t.ndim - 1) + [(0, width - t.shape[-1])])


def _pad_rows(t, rows):
    return jnp.pad(t, [(0, 0)] * (t.ndim - 2) + [(0, rows - t.shape[-2]), (0, 0)])


FRONT_SIZES = (512, 128, 128, 256, 128)
FRONT_BOUNDS = (0, 512, 640, 768, 1024, 1152, 1280)
PE_LANE = NOPE_DIM


def _arrange_w_in_t(wt):
    k = wt.shape[1]
    n_front = sum(FRONT_SIZES)
    front, kr, gates = wt[:n_front], wt[n_front:n_front + ROPE_DIM], wt[n_front + ROPE_DIM:]
    kr_slab = jnp.concatenate([jnp.zeros((PE_LANE, k), wt.dtype), kr,
                               jnp.zeros((HEAD_PAD - PE_LANE - ROPE_DIM, k), wt.dtype)], axis=0)
    return jnp.concatenate([front, kr_slab], axis=0), gates


def _arrange_w_uq_t(wt):
    k = wt.shape[1]
    return _pad_rows(wt.reshape(B_HEADS, NOPE_DIM + ROPE_DIM, k), HEAD_PAD).reshape(B_HEADS * HEAD_PAD, k)


def _arrange_w_ukv_t(wt):
    k = wt.shape[1]
    w = wt.reshape(B_HEADS, 2, NOPE_DIM, k)
    slabs = [_pad_rows(w[:, part], HEAD_PAD).reshape(B_HEADS * HEAD_PAD, k) for part in range(2)]
    return jnp.concatenate(slabs, axis=0)


def _rope_tables(positions, s):
    pos = positions.reshape(s, 1).astype(F32)

    def angles(dim):
        return pos * ROPE_THETA ** (-(jnp.arange(0, dim, 2, dtype=F32) / dim))

    cos_a, sin_a = jnp.cos(angles(A_HEAD_DIM)), jnp.sin(angles(A_HEAD_DIM))
    zero_a = jnp.zeros_like(sin_a)
    tables_a = [jnp.tile(jnp.concatenate(pair, axis=1), (1, LANES // A_HEAD_DIM))
                for pair in ((cos_a, cos_a), (-sin_a, zero_a), (zero_a, sin_a))]
    cos_b, sin_b = jnp.cos(angles(ROPE_DIM)), jnp.sin(angles(ROPE_DIM))
    zero_b = jnp.zeros_like(sin_b)

    def slab(first, second, fill):
        return jnp.concatenate([jnp.full((s, PE_LANE), fill, F32), first, second,
                                jnp.full((s, HEAD_PAD - PE_LANE - ROPE_DIM), fill, F32)], axis=1)

    tables_b = [slab(cos_b, cos_b, 1.0), slab(-sin_b, zero_b, 0.0), slab(zero_b, sin_b, 0.0)]
    return tables_a + tables_b


def _local_loss(wts, x, p, tables, target):
    s = x.shape[0]
    small_shapes = {n: wts[n].shape[-1] for n in SMALL}
    small_vec = _pack_rows([_pad_lanes(wts[n].reshape(1, -1), small_shapes[n] + (-small_shapes[n]) % SMALL_COLS)
                            for n in SMALL], SMALL_COLS, 8)
    sm = _unpack_small(replicated(small_vec), small_shapes)
    def shard(n):
        return wts[n].T if n in COL_SHARDED else wts[n]

    h1, gathered = prenorm_gather(
        x, sm["attn_pre_norm"], tuple([shard(n) for n in EARLY] + [_pack_rows([wts["conv_w"]], SMALL_COLS, 8)]),
        (BF16,) * len(EARLY) + (F32,))
    big = {n: g.reshape(-1, g.shape[2]) for n, g in zip(EARLY, gathered)}
    ch = wts["conv_w"].shape[1]
    conv_w = gathered[-1].reshape(N_DEV, -1)[:, :CONV_W * ch].reshape(N_DEV, CONV_W, ch)
    conv_w = conv_w.transpose(1, 0, 2).reshape(CONV_W, N_DEV * ch)

    w_front_t, w_gates_t = _arrange_w_in_t(big["w_in"])
    tables_a, tables_b = tables[:3], tables[3:]

    qa, ka, va, cqn, ckvn, kpe = proj_stage(
        "prep", _f_prep, [(h1, w_front_t, "nt", "w_front", True, F32)], params=[sm["q_a_norm"], sm["kv_a_norm"]],
        consts=tables, splits=[FRONT_BOUNDS], ts=512, out_dtypes=[BF16, BF16, BF16, BF16, BF16, F32])
    ya, mid = swa_nat(qa, ka, va, sm["sinks"].reshape(-1), tuple(shard(n) for n in MID))
    big.update({n: g.reshape(-1, g.shape[2]) for n, g in zip(MID, mid)})

    (q2,) = proj_stage("qrope", _f_qrope, [(cqn, _arrange_w_uq_t(big["w_uq"]), "nt", "w_uq", True, BF16)],
                       consts=tables_b, ts=512, out_dtypes=[BF16])
    k2, v2 = proj_stage("kv", _f_kv, [(ckvn, _arrange_w_ukv_t(big["w_ukv"]), "nt", "w_ukv", True, BF16)],
                        extra=[kpe], splits=[(0, B_HEADS * HEAD_PAD, 2 * B_HEADS * HEAD_PAD), None], ts=512,
                        out_dtypes=[BF16, BF16])
    yb, late = flash_nat(q2, k2, v2, tuple(shard(n) for n in LATE))
    big.update({n: g.reshape(-1, g.shape[2]) for n, g in zip(LATE, late)})

    (mixed,) = proj_stage(
        "gate", _f_gate, [(h1, w_gates_t, "nt", "w_gates", True, F32),
                          (ya, big["w_branch_a"], "nt", "w_branch_a", True, BF16),
                          (yb, big["w_branch_b"], "nt", "w_branch_b", True, BF16)],
        params=[sm["b_gate"][:, :D_MODEL], sm["b_gate"][:, D_MODEL:]],
        splits=[(0, D_MODEL, 2 * D_MODEL), None, None], out_dtypes=[BF16])
    x1, h2 = proj_stage("post_attn", _f_post, [(mixed, big["w_out"], "nn", "w_out", True, F32)], extra=[x],
                        params=[sm["attn_post_norm"], sm["mlp_pre_norm"]], ts=512, out_dtypes=[F32, BF16])

    act = mlp_up(h2, big["w_up"], conv_w, sm["conv_b"])
    x2, h3 = proj_stage("post_mlp", _f_post, [(act, big["w_down"], "nn", "w_down", True, F32)], extra=[x1],
                        params=[sm["mlp_post_norm"], sm["ple_norm"]], ts=512, out_dtypes=[F32, BF16])

    (rowloss,) = proj_stage("loss", _f_out, [(h3, big["w_ple_gate"], "nn", "w_ple_gate", True, F32),
                                             (p, big["w_ple"], "nt", "w_ple", False, BF16)], extra=[x2],
                            consts=[target], ts=512)
    return jnp.sum(rowloss)


WEIGHTS = ["attn_pre_norm", "attn_post_norm", "w_in", "b_gate", "sinks", "q_a_norm", "w_uq", "kv_a_norm", "w_ukv",
           "w_branch_a", "w_branch_b", "w_out", "mlp_pre_norm", "mlp_post_norm", "w_up", "conv_w", "conv_b",
           "w_down", "ple_norm", "w_ple_gate", "w_ple"]


def kernel(x, p, positions, attn_pre_norm, attn_post_norm, w_in, b_gate, sinks, q_a_norm, w_uq, kv_a_norm, w_ukv, w_branch_a, w_branch_b, w_out, mlp_pre_norm, mlp_post_norm, w_up, conv_w, conv_b, w_down, ple_norm, w_ple_gate, w_ple, loss_target, m_attn_pre_norm, m_attn_post_norm, m_w_in, m_b_gate, m_sinks, m_q_a_norm, m_w_uq, m_kv_a_norm, m_w_ukv, m_w_branch_a, m_w_branch_b, m_w_out, m_mlp_pre_norm, m_mlp_post_norm, m_w_up, m_conv_w, m_conv_b, m_w_down, m_ple_norm, m_w_ple_gate, m_w_ple, v_attn_pre_norm, v_attn_post_norm, v_w_in, v_b_gate, v_sinks, v_q_a_norm, v_w_uq, v_kv_a_norm, v_w_ukv, v_w_branch_a, v_w_branch_b, v_w_out, v_mlp_pre_norm, v_mlp_post_norm, v_w_up, v_conv_w, v_conv_b, v_w_down, v_ple_norm, v_w_ple_gate, v_w_ple):
    given = dict(locals())
    s = x.shape[1]
    wts = {n: given[n][0] if given[n].ndim == 3 else given[n] for n in WEIGHTS}
    tables = _rope_tables(positions, s)
    local_loss, (grads, grad_x) = jax.value_and_grad(_local_loss, argnums=(0, 1))(
        wts, x[0], p[0, 0], tables, loss_target[0])
    loss = lax.psum(local_loss, AXES)

    outs = {"grad": [], "delta": [], "m": [], "v": []}
    for n in WEIGHTS:
        shape = given[n].shape
        w2 = wts[n].reshape(-1, shape[-1])
        g2 = grads[n].reshape(w2.shape)
        delta, new_m, new_v = _adamw(w2, g2, given["m_" + n].reshape(w2.shape), given["v_" + n].reshape(w2.shape),
                                     "adamw_" + n)
        outs["grad"].append(g2.reshape(shape))
        outs["delta"].append(delta.reshape(shape))
        outs["m"].append(new_m.reshape(shape))
        outs["v"].append(new_v.reshape(shape))
    return (loss, grad_x[None], *outs["grad"], *outs["delta"], *outs["m"], *outs["v"])
```

```python
import functools

import numpy as np
import jax
import jax.numpy as jnp
from jax import lax
from jax.experimental import pallas as pl
from jax.experimental.pallas import tpu as pltpu

F32 = jnp.float32
BF16 = jnp.bfloat16
MESH_ID = pl.DeviceIdType.MESH
AXES = ("x", "y", "c")
N_DEV = 8

D_MODEL = 1024
RMS_EPS = 1e-6
ROPE_THETA = 10000.0
SWA_BLOCK = 128
A_HEADS, A_KV_HEADS, A_HEAD_DIM = 8, 2, 64
A_GROUP = A_HEADS // A_KV_HEADS
B_HEADS, Q_LORA, KV_LORA, NOPE_DIM, ROPE_DIM, V_DIM = 8, 256, 128, 64, 32, 64
D_FF = 2816
CONV_W = 3
HEAD_PAD = 128

ADAM_LR, ADAM_B1, ADAM_B2, ADAM_EPS, ADAM_WD, ADAM_STEP = 0.001, 0.9, 0.999, 1e-08, 0.01, 10

VMEM_LIMIT = 48 * 1024 * 1024
MM_TM, MM_TN, MM_TK_TOKENS = 512, 1408, 1024
MM_VMEM_BUDGET = 36 * 1024 * 1024
FLASH_T = 1024
CONV_TS = 128
CONV_CHUNK = 256


def _params(*sem):
    return pltpu.CompilerParams(dimension_semantics=sem, vmem_limit_bytes=VMEM_LIMIT)


def _pick(dim, cap, mult):
    best = None
    for t in range(mult, min(dim, cap) + 1, mult):
        if dim % t == 0:
            best = t
    return dim if best is None else best


def _divisors(dim, mult):
    return [t for t in range(mult, dim + 1, mult) if dim % t == 0] or [dim]


def _matmul_tiles(m, n, kdim, form, sizes):
    sa, sb, so = sizes
    tk = _pick(kdim, MM_TK_TOKENS, 128) if form == "tn" else kdim
    cap_m = MM_TN if form == "tn" else MM_TM
    best = None
    for tm in _divisors(m, 128):
        for tn in _divisors(n, 128):
            need = 2 * (tm * tk * sa + tk * tn * sb + tm * tn * so) + (tm * tn * 4 if tk != kdim else 0)
            if tm > cap_m or tn > MM_TN or need > MM_VMEM_BUDGET:
                continue
            if best is None or (tm * tn, tm) > (best[0] * best[1], best[0]):
                best = (tm, tn)
    return best[0], best[1], tk


def _matmul(a, b, form, *, out_dtype=F32, name):
    if form == "tn":
        (kdim, m), n = a.shape, b.shape[1]
    else:
        (m, kdim), n = a.shape, (b.shape[1] if form == "nn" else b.shape[0])
    sizes = (a.dtype.itemsize, b.dtype.itemsize, jnp.dtype(out_dtype).itemsize)
    tm, tn, tk = _matmul_tiles(m, n, kdim, form, sizes)
    nk = kdim // tk
    rows_outer = nk > 1 or (m // tm) * b.size * sizes[1] <= (n // tn) * a.size * sizes[0]

    def ij(fn):
        return (lambda i, j, k: fn(i, j, k)) if rows_outer else (lambda j, i, k: fn(i, j, k))

    a_spec = (pl.BlockSpec((tk, tm), ij(lambda i, j, k: (k, i))) if form == "tn"
              else pl.BlockSpec((tm, tk), ij(lambda i, j, k: (i, k))))
    b_spec = (pl.BlockSpec((tn, tk), ij(lambda i, j, k: (j, k))) if form == "nt"
              else pl.BlockSpec((tk, tn), ij(lambda i, j, k: (k, j))))
    dims = (((0 if form == "tn" else 1,), (1 if form == "nt" else 0,)), ((), ()))

    def product(a_ref, b_ref):
        return lax.dot_general(a_ref[...].astype(BF16), b_ref[...].astype(BF16), dims, preferred_element_type=F32)

    if nk == 1:
        def body(a_ref, b_ref, o_ref):
            o_ref[...] = product(a_ref, b_ref).astype(o_ref.dtype)

        scratch = []
    else:
        def body(a_ref, b_ref, o_ref, acc_ref):
            k = pl.program_id(2)

            @pl.when(k == 0)
            def _():
                acc_ref[...] = jnp.zeros_like(acc_ref)

            acc_ref[...] += product(a_ref, b_ref)

            @pl.when(k == nk - 1)
            def _():
                o_ref[...] = acc_ref[...].astype(o_ref.dtype)

        scratch = [pltpu.VMEM((tm, tn), F32)]

    return pl.pallas_call(
        body, name=name, grid=(m // tm, n // tn, nk) if rows_outer else (n // tn, m // tm, nk),
        in_specs=[a_spec, b_spec],
        out_specs=pl.BlockSpec((tm, tn), ij(lambda i, j, k: (i, j))),
        out_shape=jax.ShapeDtypeStruct((m, n), out_dtype),
        scratch_shapes=scratch,
        compiler_params=_params("parallel", "parallel", "arbitrary"),
    )(a, b)


def _pairs(bounds):
    return list(zip(bounds[:-1], bounds[1:]))


def _split(v, bounds):
    return [v[:, a:b] for a, b in _pairs(bounds)]


def _stage_build(name, f, tiled, params, consts, splits, ts, out_dtypes, ct_dtypes=None):
    n_t, n_p, n_c = len(tiled), len(params), len(consts)
    ct_dtypes = [t.dtype for t in tiled] if ct_dtypes is None else ct_dtypes
    s = tiled[0].shape[0]
    ts = min(ts, s)
    grid = (s // ts,)
    if splits is None:
        splits = [None] * n_t
    in_bounds = [(0, t.shape[1]) if b is None else tuple(b) for t, b in zip(tiled, splits)]

    def tile_aval(arr):
        return jax.ShapeDtypeStruct((ts, arr.shape[1]), arr.dtype)

    slab_avals = [[jax.ShapeDtypeStruct((ts, e - a), F32) for a, e in _pairs(b)]
                  for t, b in zip(tiled, in_bounds)]
    out_avals = jax.eval_shape(f, slab_avals, list(params), [tile_aval(c) for c in consts])
    out_bounds = [tuple(np.cumsum([0] + [o.shape[1] for o in slabs]).tolist()) for slabs in out_avals]
    out_dtypes = [F32] * len(out_bounds) if out_dtypes is None else out_dtypes
    out_shapes = [jax.ShapeDtypeStruct((s, b[-1]), d) for b, d in zip(out_bounds, out_dtypes)]

    def row_spec(width):
        return pl.BlockSpec((ts, width), lambda i: (i, 0))

    def par_spec(arr):
        return pl.BlockSpec(arr.shape, lambda i: (0, 0))

    in_specs = ([row_spec(t.shape[1]) for t in tiled] + [par_spec(p) for p in params]
                + [row_spec(c.shape[1]) for c in consts])

    def load(refs):
        t = [_split(r[...].astype(F32), b) for r, b in zip(refs[:n_t], in_bounds)]
        p = [r[...] for r in refs[n_t:n_t + n_p]]
        c = [r[...] for r in refs[n_t + n_p:n_t + n_p + n_c]]
        return t, p, c

    def store(refs, values, bounds):
        for ref, slabs, b in zip(refs, values, bounds):
            for v, (a, e) in zip(slabs, _pairs(b)):
                ref[:, a:e] = v.astype(ref.dtype)

    def run_fwd(tiled, params, consts):
        def body(*refs):
            t, p, c = load(refs)
            store(refs[n_t + n_p + n_c:], f(t, p, c), out_bounds)

        return pl.pallas_call(
            body, name=name + "_fwd", grid=grid, in_specs=in_specs,
            out_specs=[row_spec(b[-1]) for b in out_bounds], out_shape=out_shapes,
            compiler_params=_params("parallel"),
        )(*tiled, *params, *consts)

    def run_bwd(tiled, params, consts, cts):
        n_in = n_t + n_p + n_c
        n_o = len(out_bounds)

        def body(*refs):
            t, p, c = load(refs)
            g = [_split(r[...].astype(F32), b) for r, b in zip(refs[n_in:n_in + n_o], out_bounds)]
            _, pull = jax.vjp(lambda t_, p_: f(t_, p_, c), t, p)
            dt, dp = pull(g)
            store(refs[n_in + n_o:n_in + n_o + n_t], dt, in_bounds)
            first = pl.program_id(0) == 0
            for ref, d in zip(refs[n_in + n_o + n_t:], dp):
                @pl.when(first)
                def _(ref=ref):
                    ref[...] = jnp.zeros_like(ref)

                ref[...] += d

        res = pl.pallas_call(
            body, name=name + "_bwd", grid=grid,
            in_specs=in_specs + [row_spec(b[-1]) for b in out_bounds],
            out_specs=[row_spec(t.shape[1]) for t in tiled] + [par_spec(p) for p in params],
            out_shape=[jax.ShapeDtypeStruct(t.shape, d) for t, d in zip(tiled, ct_dtypes)]
                      + [jax.ShapeDtypeStruct(p.shape, F32) for p in params],
            compiler_params=_params("arbitrary"),
        )(*tiled, *params, *consts, *cts)
        return tuple(res[:n_t]), tuple(res[n_t:])

    return run_fwd, run_bwd


def proj_stage(name, f, projections, extra=(), params=(), consts=(), splits=None, ts=256, out_dtypes=None):
    n_z = len(projections)
    forms = [pr[2] for pr in projections]
    names = [pr[3] for pr in projections]
    need_da = [pr[4] for pr in projections]
    store = [pr[5] for pr in projections]
    extra, params, consts = tuple(extra), tuple(params), tuple(consts)

    def matmuls(a_list, w_list):
        return tuple(_matmul(a, w, form, out_dtype=dt, name=n + "_fwd")
                     for a, w, form, n, dt in zip(a_list, w_list, forms, names, store))

    def build(zs, ct=False):
        ct_dtypes = [BF16] * n_z + [e.dtype for e in extra] if ct else None
        return _stage_build(name, f, tuple(zs) + extra, params, consts, splits, ts, out_dtypes, ct_dtypes)

    @jax.custom_vjp
    def op(a_list, w_list, extra, params, consts):
        zs = matmuls(a_list, w_list)
        return tuple(build(zs)[0](zs + extra, params, consts))

    def op_fwd(a_list, w_list, extra, params, consts):
        zs = matmuls(a_list, w_list)
        return tuple(build(zs)[0](zs + extra, params, consts)), (a_list, w_list, zs, extra, params, consts)

    def op_bwd(res, cts):
        a_list, w_list, zs, extra, params, consts = res
        dt, dp = build(zs, ct=True)[1](zs + extra, params, consts, cts)
        da_list, dw_list = [], []
        for a, w, dz, form, n, want in zip(a_list, w_list, dt[:n_z], forms, names, need_da):
            if form == "nn":
                da = _matmul(dz, w, "nt", out_dtype=a.dtype, name=n + "_da") if want else jnp.zeros_like(a)
                dw = _matmul(a, dz, "tn", out_dtype=w.dtype, name=n + "_dw")
            else:
                da = _matmul(dz, w, "nn", out_dtype=a.dtype, name=n + "_da") if want else jnp.zeros_like(a)
                dw = _matmul(dz, a, "tn", out_dtype=w.dtype, name=n + "_dw")
            da_list.append(da)
            dw_list.append(dw)
        return tuple(da_list), tuple(dw_list), tuple(dt[n_z:]), dp, tuple(jnp.zeros_like(c) for c in consts)

    op.defvjp(op_fwd, op_bwd)
    return op(tuple(pr[0] for pr in projections), tuple(pr[1] for pr in projections), extra, params, consts)


def _rms(t, g):
    return t * lax.rsqrt(jnp.mean(t * t, axis=-1, keepdims=True) + RMS_EPS) * g


@functools.partial(jax.custom_vjp, nondiff_argnums=(1,))
def _lane_roll(t, shift):
    return pltpu.roll(t, shift % t.shape[-1], t.ndim - 1)


def _lane_roll_fwd(t, shift):
    return _lane_roll(t, shift), None


def _lane_roll_bwd(shift, _, ct):
    return (pltpu.roll(ct, (-shift) % ct.shape[-1], ct.ndim - 1),)


_lane_roll.defvjp(_lane_roll_fwd, _lane_roll_bwd)


def _rope_lanes(t, tables, half):
    reps = t.shape[1] // tables[0].shape[1]
    c, s_lo, s_hi = [jnp.concatenate([tb] * reps, axis=1) if reps > 1 else tb for tb in tables]
    return t * c + _lane_roll(t, -half) * s_lo + _lane_roll(t, half) * s_hi


PRENORM_TS = 256


def _prenorm_fwd_call(x, g, shards):
    s, width = x.shape
    ts = min(PRENORM_TS, s)
    nt = s // ts
    n_arr = len(shards)

    def body(*refs):
        x_ref, g_ref = refs[:2]
        o_ref = refs[2 + n_arr]
        i = pl.program_id(0)
        ag_start, ag_forward, ag_finish = _allgather_phases(refs[2:2 + n_arr], refs[3 + n_arr:3 + 2 * n_arr],
                                                            *refs[3 + 2 * n_arr:])

        @pl.when(i == 0)
        def _():
            ag_start()

        @pl.when(i == nt // 2)
        def _():
            ag_forward()

        o_ref[...] = _rms(x_ref[...], g_ref[...]).astype(o_ref.dtype)

        @pl.when(i == nt - 1)
        def _():
            ag_finish()

    return pl.pallas_call(
        body, name="prenorm_fwd", grid=(nt,),
        in_specs=[pl.BlockSpec((ts, width), lambda i: (i, 0)), pl.BlockSpec(g.shape, lambda i: (0, 0))]
                 + [HBM_SPEC] * n_arr,
        out_specs=[pl.BlockSpec((ts, width), lambda i: (i, 0))] + [HBM_SPEC] * n_arr,
        out_shape=[jax.ShapeDtypeStruct(x.shape, BF16)] + _allgather_out_shapes(shards),
        scratch_shapes=_allgather_sems(n_arr),
        compiler_params=_params("arbitrary"),
    )(x, g, *shards)


def _prenorm_bwd_call(x, g, dh, parts):
    s, width = x.shape
    ts = min(PRENORM_TS, s)
    nt = s // ts
    n_arr = len(parts)

    def body(*refs):
        x_ref, g_ref, dh_ref = refs[:3]
        dx_ref, dg_ref = refs[3 + n_arr:5 + n_arr]
        i = pl.program_id(0)
        exchange_start, exchange_finish = _exchange_chips_phases(
            refs[3:3 + n_arr], refs[5 + n_arr:5 + 2 * n_arr], *refs[5 + 2 * n_arr:])

        @pl.when(i == 0)
        def _():
            exchange_start()
            dg_ref[...] = jnp.zeros_like(dg_ref)

        _, pull = jax.vjp(_rms, x_ref[...], g_ref[...])
        dx, dg = pull(dh_ref[...].astype(F32))
        dx_ref[...] = dx
        dg_ref[...] += dg

        @pl.when(i == nt - 1)
        def _():
            exchange_finish()

    row = pl.BlockSpec((ts, width), lambda i: (i, 0))
    par = pl.BlockSpec(g.shape, lambda i: (0, 0))
    return pl.pallas_call(
        body, name="prenorm_bwd", grid=(nt,),
        in_specs=[row, par, row] + [HBM_SPEC] * n_arr,
        out_specs=[row, par] + [HBM_SPEC] * n_arr,
        out_shape=[jax.ShapeDtypeStruct(x.shape, F32), jax.ShapeDtypeStruct(g.shape, F32)]
                  + [jax.ShapeDtypeStruct(p.shape, p.dtype) for p in parts],
        scratch_shapes=_exchange_chips_sems(n_arr),
        compiler_params=_params("arbitrary"),
    )(x, g, dh, *parts)


@functools.partial(jax.custom_vjp, nondiff_argnums=(3,))
def prenorm_gather(x, g, shards, wire_dtypes):
    out = _prenorm_fwd_call(x, g, [s.astype(d) for s, d in zip(shards, wire_dtypes)])
    return out[0], tuple(out[1:])


def _prenorm_gather_fwd(x, g, shards, wire_dtypes):
    return prenorm_gather(x, g, shards, wire_dtypes), (x, g)


def _prenorm_gather_bwd(wire_dtypes, res, cts):
    x, g = res
    dh, d_gathered = cts
    out = _prenorm_bwd_call(x, g, dh, _reduce_scatter_head(d_gathered, "grads"))
    return out[0], out[1], _reduce_scatter_tail(out[2:], "grads")


prenorm_gather.defvjp(_prenorm_gather_fwd, _prenorm_gather_bwd)


def _f_prep(t, p, c):
    qa, ka, va, cq, ckv, kr = t[0]
    return [[_rope_lanes(qa, c[0:3], A_HEAD_DIM // 2)], [_rope_lanes(ka, c[0:3], A_HEAD_DIM // 2)], [va],
            [_rms(cq, p[0])], [_rms(ckv, p[1])], [_rope_lanes(kr, c[3:6], ROPE_DIM // 2)]]


def _f_qrope(t, p, c):
    return [[_rope_lanes(t[0][0], c, ROPE_DIM // 2)]]


def _f_kv(t, p, c):
    (k_nope, v), (k_pe,) = t
    return [[k_nope + jnp.concatenate([k_pe] * B_HEADS, axis=1)], [v]]


def _f_gate(t, p, c):
    (ga, gb), (pa,), (pb,) = t
    ba, bb = p
    return [[jax.nn.sigmoid(ga + ba) * pa + jax.nn.sigmoid(gb + bb) * pb]]


def _f_post(t, p, c):
    (branch,), (residual,) = t
    x1 = residual + _rms(branch, p[0])
    return [[x1], [_rms(x1, p[1])]]


def _f_out(t, p, c):
    (gate,), (emb,), (x2,) = t
    y = x2 + jax.nn.sigmoid(gate) * emb
    err = y - c[0]
    return [[0.5 * jnp.mean(err * err, axis=-1, keepdims=True)]]


def _shift_down(cur, prev, has_prev):
    full = jnp.concatenate([prev * has_prev, cur], axis=0)
    return pltpu.roll(full, 1, 0)[HALO:], pltpu.roll(full, 2, 0)[HALO:]


GELU_C = float(np.sqrt(2.0 / np.pi))
GELU_A = 0.044715
HALO = 8
HALO_BLOCK = 16


def _halo_before(ref, cols):
    return ref[:, cols].astype(F32)[HALO_BLOCK - HALO:]


def _halo_after(ref, cols):
    return ref[:, cols].astype(F32)[:HALO]


def _gelu_tanh(x):
    x2 = x * x
    th = jnp.tanh(x * (GELU_C + (GELU_C * GELU_A) * x2))
    half = 0.5 + 0.5 * th
    return x * half, half + x * (0.5 - 0.5 * (th * th)) * (GELU_C + (3.0 * GELU_C * GELU_A) * x2)


def _row_sum(t):
    return jnp.sum(t, axis=0, keepdims=True)


def _conv3(cur, prev, w_ref, b_ref, has_prev):
    u1, u2 = _shift_down(cur, prev, has_prev)
    return w_ref[2:3, :] * cur + w_ref[1:2, :] * u1 + w_ref[0:1, :] * u2 + b_ref[...], u1, u2


def _mlp_act_specs(s):
    ts = min(CONV_TS, s)
    hb = ts // HALO_BLOCK

    def half_specs(h):
        return [pl.BlockSpec((ts, D_FF), lambda i: (i, h)),
                pl.BlockSpec((HALO_BLOCK, D_FF), lambda i: (jnp.maximum(i * hb - 1, 0), h))]

    def par_specs(h):
        return [pl.BlockSpec((CONV_W, D_FF), lambda i: (0, h)), pl.BlockSpec((1, D_FF), lambda i: (0, h))]

    return ts, hb, half_specs, par_specs


def _mlp_act_fwd_call(up, conv_w, conv_b):
    s = up.shape[0]
    ts, hb, half_specs, par_specs = _mlp_act_specs(s)

    def body(g_ref, gp_ref, v_ref, vp_ref, wg_ref, bg_ref, wv_ref, bv_ref, o_ref):
        has_prev = (pl.program_id(0) > 0).astype(F32)

        def chunk(cidx, carry):
            cols = pl.ds(pl.multiple_of(cidx * CONV_CHUNK, CONV_CHUNK), CONV_CHUNK)
            u_g, _, _ = _conv3(g_ref[:, cols].astype(F32), _halo_before(gp_ref, cols), wg_ref.at[:, cols],
                               bg_ref.at[:, cols], has_prev)
            u_v, _, _ = _conv3(v_ref[:, cols].astype(F32), _halo_before(vp_ref, cols), wv_ref.at[:, cols],
                               bv_ref.at[:, cols], has_prev)
            o_ref[:, cols] = (_gelu_tanh(u_g)[0] * u_v).astype(o_ref.dtype)
            return carry

        lax.fori_loop(0, D_FF // CONV_CHUNK, chunk, 0)

    return pl.pallas_call(
        body, name="mlp_act_fwd", grid=(s // ts,),
        in_specs=half_specs(0) + half_specs(1) + par_specs(0) + par_specs(1),
        out_specs=pl.BlockSpec((ts, D_FF), lambda i: (i, 0)),
        out_shape=jax.ShapeDtypeStruct((s, D_FF), BF16),
        compiler_params=_params("parallel"),
    )(up, up, up, up, conv_w, conv_b, conv_w, conv_b)


def _mlp_act_bwd_call(up, conv_w, conv_b, dact):
    s = up.shape[0]
    ts, hb, half_specs, par_specs = _mlp_act_specs(s)
    nt = s // ts
    ext = ts + HALO

    def next_spec(rows, h):
        return pl.BlockSpec((rows, D_FF), lambda i: (jnp.minimum((i + 1) * (ts // rows), s // rows - 1), h))

    def body(g_ref, gp_ref, gn_ref, v_ref, vp_ref, vn_ref, wg_ref, bg_ref, wv_ref, bv_ref, da_ref, dan_ref,
             dup_ref, dwg_ref, dbg_ref, dwv_ref, dbv_ref):
        i = pl.program_id(0)
        has_prev, has_next = (i > 0).astype(F32), (i < nt - 1).astype(F32)

        @pl.when(i == 0)
        def _():
            for ref in (dwg_ref, dbg_ref, dwv_ref, dbv_ref):
                ref[...] = jnp.zeros_like(ref)

        def chunk(cidx, carry):
            cols = pl.ds(pl.multiple_of(cidx * CONV_CHUNK, CONV_CHUNK), CONV_CHUNK)
            g_ext = jnp.concatenate([g_ref[:, cols].astype(F32), _halo_after(gn_ref, cols)], axis=0)
            v_ext = jnp.concatenate([v_ref[:, cols].astype(F32), _halo_after(vn_ref, cols)], axis=0)
            u_g, g1, g2 = _conv3(g_ext, _halo_before(gp_ref, cols), wg_ref.at[:, cols], bg_ref.at[:, cols], has_prev)
            u_v, v1, v2 = _conv3(v_ext, _halo_before(vp_ref, cols), wv_ref.at[:, cols], bv_ref.at[:, cols], has_prev)
            da_ext = jnp.concatenate([da_ref[:, cols].astype(F32), _halo_after(dan_ref, cols) * has_next], axis=0)
            act_g, dact_g = _gelu_tanh(u_g)
            du_g = da_ext * u_v * dact_g
            du_v = da_ext * act_g
            for du, w_ref, x0, x1, x2, dw_ref, db_ref, lo in ((du_g, wg_ref, g_ext, g1, g2, dwg_ref, dbg_ref, 0),
                                                          (du_v, wv_ref, v_ext, v1, v2, dwv_ref, dbv_ref, D_FF)):
                d1 = pltpu.roll(du, ext - 1, 0)
                d2 = pltpu.roll(du, ext - 2, 0)
                dup = w_ref[2:3, cols] * du + w_ref[1:2, cols] * d1 + w_ref[0:1, cols] * d2
                out_cols = pl.ds(pl.multiple_of(lo + cidx * CONV_CHUNK, CONV_CHUNK), CONV_CHUNK)
                dup_ref[:, out_cols] = dup[0:ts].astype(dup_ref.dtype)
                own = du[0:ts]
                dw_ref[0:1, cols] += _row_sum(own * x2[0:ts])
                dw_ref[1:2, cols] += _row_sum(own * x1[0:ts])
                dw_ref[2:3, cols] += _row_sum(own * x0[0:ts])
                db_ref[:, cols] += _row_sum(own)
            return carry

        lax.fori_loop(0, D_FF // CONV_CHUNK, chunk, 0)

    par_out = [pl.BlockSpec((CONV_W, D_FF), lambda i: (0, 0)), pl.BlockSpec((1, D_FF), lambda i: (0, 0))]
    par_shapes = [jax.ShapeDtypeStruct((CONV_W, D_FF), F32), jax.ShapeDtypeStruct((1, D_FF), F32)]
    return pl.pallas_call(
        body, name="mlp_act_bwd", grid=(nt,),
        in_specs=(half_specs(0) + [next_spec(HALO_BLOCK, 0)] + half_specs(1) + [next_spec(HALO_BLOCK, 1)]
                  + par_specs(0) + par_specs(1)
                  + [pl.BlockSpec((ts, D_FF), lambda i: (i, 0)), next_spec(HALO_BLOCK, 0)]),
        out_specs=[pl.BlockSpec((ts, 2 * D_FF), lambda i: (i, 0))] + par_out + par_out,
        out_shape=[jax.ShapeDtypeStruct((s, 2 * D_FF), BF16)] + par_shapes + par_shapes,
        compiler_params=_params("arbitrary"),
    )(up, up, up, up, up, up, conv_w, conv_b, conv_w, conv_b, dact, dact)


@jax.custom_vjp
def mlp_up(h2, w_up_t, conv_w, conv_b):
    return _mlp_act_fwd_call(_matmul(h2, w_up_t, "nt", out_dtype=BF16, name="w_up_fwd"), conv_w, conv_b)


def _mlp_up_fwd(h2, w_up_t, conv_w, conv_b):
    up = _matmul(h2, w_up_t, "nt", out_dtype=BF16, name="w_up_fwd")
    return _mlp_act_fwd_call(up, conv_w, conv_b), (h2, w_up_t, up, conv_w, conv_b)


def _mlp_up_bwd(res, dact):
    h2, w_up_t, up, conv_w, conv_b = res
    dup, dwg, dbg, dwv, dbv = _mlp_act_bwd_call(up, conv_w, conv_b, dact)
    dh2 = _matmul(dup, w_up_t, "nn", out_dtype=h2.dtype, name="w_up_da")
    dw = _matmul(dup, h2, "tn", out_dtype=w_up_t.dtype, name="w_up_dw")
    return dh2, dw, jnp.concatenate([dwg, dwv], axis=1), jnp.concatenate([dbg, dbv], axis=1)


mlp_up.defvjp(_mlp_up_fwd, _mlp_up_bwd)


SWA_ROWS = A_GROUP * SWA_BLOCK


def _swa_sink_rows(sink_ref, g):
    return jnp.concatenate([jnp.full((SWA_BLOCK, 1), sink_ref[g * A_GROUP + h], F32) for h in range(A_GROUP)], axis=0)


def _swa_operands(q_ref, kp_ref, kc_ref, vp_ref, vc_ref, sink_ref):
    groups = []
    for g in range(A_KV_HEADS):
        groups.append((_swa_stack_heads(q_ref, g), _dup_half(kp_ref[...], g), _dup_half(kc_ref[...], g),
                       _dup_half(vp_ref[...], g), _dup_half(vc_ref[...], g)))
    return groups, jnp.concatenate([_swa_sink_rows(sink_ref, g) for g in range(A_KV_HEADS)], axis=0)


def _swa_probs(groups, sink, prev_off):
    scale = A_HEAD_DIM ** -0.5
    sp = jnp.concatenate([lax.dot_general(gr[0], gr[1], NT_DIMS, preferred_element_type=F32) for gr in groups], axis=0)
    sc = jnp.concatenate([lax.dot_general(gr[0], gr[2], NT_DIMS, preferred_element_type=F32) for gr in groups], axis=0)
    qi = lax.broadcasted_iota(jnp.int32, sp.shape, 0) & (SWA_BLOCK - 1)
    kj = lax.broadcasted_iota(jnp.int32, sp.shape, 1)
    in_cur = kj <= qi
    sw = jnp.where(in_cur, sc, jnp.where(kj > qi + prev_off, sp, -jnp.inf)) * scale
    m = jnp.maximum(jnp.max(sw, axis=-1, keepdims=True), sink)
    e, es = jnp.exp(sw - m), jnp.exp(sink - m)
    den = jnp.sum(e, axis=-1, keepdims=True) + es
    return e / den, in_cur, es / den


def _swa_split(t, in_cur):
    cur = jnp.where(in_cur, t, 0.0)
    return t - cur, cur


MLA_SCALE = (NOPE_DIM + ROPE_DIM) ** -0.5
EXP2_SCALE = MLA_SCALE * float(np.log2(np.e))
NT_DIMS = (((1,), (1,)), ((), ()))
TN_DIMS = (((0,), (0,)), ((), ()))


LANES = 128
HALF = LANES // 2


def _low_half(shape):
    return lax.broadcasted_iota(jnp.int32, shape, len(shape) - 1) < HALF


def _dup_half(x, g):
    xf = x.astype(F32)
    keep = _low_half(xf.shape) if g == 0 else jnp.logical_not(_low_half(xf.shape))
    xm = jnp.where(keep, xf, 0.0)
    return (xm + pltpu.roll(xm, HALF, 1)).astype(x.dtype)


def _fold_half(r, g):
    total = r + pltpu.roll(r, HALF, 1)
    keep = _low_half(r.shape) if g == 0 else jnp.logical_not(_low_half(r.shape))
    return jnp.where(keep, total, 0.0)


def _swa_stack_heads(ref, g):
    parts = []
    for tile in range(2):
        slab = ref[:, (2 * g + tile) * LANES:(2 * g + tile + 1) * LANES]
        low = _low_half(slab.shape)
        parts += [jnp.where(low, slab, jnp.zeros_like(slab)), jnp.where(low, jnp.zeros_like(slab), slab)]
    return jnp.concatenate(parts, axis=0)


def _swa_unstack_heads(ref, g, rows):
    for tile in range(2):
        a = rows[(2 * tile) * SWA_BLOCK:(2 * tile + 1) * SWA_BLOCK]
        b = rows[(2 * tile + 1) * SWA_BLOCK:(2 * tile + 2) * SWA_BLOCK]
        ref[:, (2 * g + tile) * LANES:(2 * g + tile + 1) * LANES] = jnp.where(_low_half(a.shape), a, b).astype(ref.dtype)


def _swa_nat_specs():
    blk = SWA_BLOCK
    q_spec = pl.BlockSpec((blk, A_HEADS * A_HEAD_DIM), lambda n: (n, 0))
    prev_spec = pl.BlockSpec((blk, LANES), lambda n: (jnp.maximum(n - 1, 0), 0))
    cur_spec = pl.BlockSpec((blk, LANES), lambda n: (n, 0))
    return q_spec, prev_spec, cur_spec, pl.BlockSpec(memory_space=pltpu.SMEM)


def _swa_nat_fwd_call(q, k, v, sinks, shards):
    s = q.shape[0]
    nblk = s // SWA_BLOCK
    n_arr = len(shards)
    q_spec, prev_spec, cur_spec, sink_spec = _swa_nat_specs()

    def body(*refs):
        q_ref, kp_ref, kc_ref, vp_ref, vc_ref, sink_ref = refs[:6]
        o_ref = refs[6 + n_arr]
        n = pl.program_id(0)
        ag_start, ag_forward, ag_finish = _allgather_phases(refs[6:6 + n_arr], refs[7 + n_arr:7 + 2 * n_arr],
                                                            *refs[7 + 2 * n_arr:])

        @pl.when(n == 0)
        def _():
            ag_start()

        @pl.when(n == nblk // 2)
        def _():
            ag_forward()

        prev_off = jnp.where(n > 0, 0, SWA_BLOCK)
        groups, sink = _swa_operands(q_ref, kp_ref, kc_ref, vp_ref, vc_ref, sink_ref)
        p, in_cur, _ = _swa_probs(groups, sink, prev_off)
        ppb, pcb = [t.astype(BF16) for t in _swa_split(p, in_cur)]
        for g, (_, _, _, vp, vc) in enumerate(groups):
            rows = slice(g * SWA_ROWS, (g + 1) * SWA_ROWS)
            out = (jnp.dot(ppb[rows], vp, preferred_element_type=F32)
                   + jnp.dot(pcb[rows], vc, preferred_element_type=F32))
            _swa_unstack_heads(o_ref, g, out)

        @pl.when(n == nblk - 1)
        def _():
            ag_finish()

    return pl.pallas_call(
        body, name="swa_fwd", grid=(nblk,),
        in_specs=[q_spec, prev_spec, cur_spec, prev_spec, cur_spec, sink_spec] + [HBM_SPEC] * n_arr,
        out_specs=[q_spec] + [HBM_SPEC] * n_arr,
        out_shape=[jax.ShapeDtypeStruct(q.shape, BF16)] + _allgather_out_shapes(shards),
        scratch_shapes=_allgather_sems(n_arr),
        compiler_params=_params("arbitrary"),
    )(q, k, k, v, v, sinks, *shards)


def _swa_nat_bwd_call(q, k, v, sinks, do, parts):
    s = q.shape[0]
    nblk = s // SWA_BLOCK
    n_arr = len(parts)
    q_spec, prev_spec, cur_spec, sink_spec = _swa_nat_specs()
    scale = A_HEAD_DIM ** -0.5
    dsink_spec = pl.BlockSpec((A_KV_HEADS, SWA_ROWS, 1), lambda n: (0, 0, 0))

    def body(*refs):
        q_ref, kp_ref, kc_ref, vp_ref, vc_ref, sink_ref, do_ref = refs[:7]
        dq_ref, dkp_ref, dkc_ref, dvp_ref, dvc_ref, dsink_ref = refs[7 + n_arr:13 + n_arr]
        n = pl.program_id(0)
        exchange_start, exchange_finish = _exchange_chips_phases(
            refs[7:7 + n_arr], refs[13 + n_arr:13 + 2 * n_arr], *refs[13 + 2 * n_arr:])

        @pl.when(n == 0)
        def _():
            exchange_start()
        prev_off = jnp.where(n > 0, 0, SWA_BLOCK)

        @pl.when(n == 0)
        def _():
            dsink_ref[...] = jnp.zeros_like(dsink_ref)

        groups, sink = _swa_operands(q_ref, kp_ref, kc_ref, vp_ref, vc_ref, sink_ref)
        dobs = [_swa_stack_heads(do_ref, g) for g in range(A_KV_HEADS)]
        p, in_cur, ps = _swa_probs(groups, sink, prev_off)
        ppb, pcb = [t.astype(BF16) for t in _swa_split(p, in_cur)]

        def per_group(fn):
            return jnp.concatenate([fn(g, slice(g * SWA_ROWS, (g + 1) * SWA_ROWS)) for g in range(A_KV_HEADS)], axis=0)

        out = per_group(lambda g, rows: jnp.dot(ppb[rows], groups[g][3], preferred_element_type=F32)
                        + jnp.dot(pcb[rows], groups[g][4], preferred_element_type=F32))
        delta = jnp.sum(jnp.concatenate(dobs, axis=0).astype(F32) * out, axis=-1, keepdims=True)
        dp = jnp.where(in_cur,
                       per_group(lambda g, rows: lax.dot_general(dobs[g], groups[g][4], NT_DIMS,
                                                                 preferred_element_type=F32)),
                       per_group(lambda g, rows: lax.dot_general(dobs[g], groups[g][3], NT_DIMS,
                                                                 preferred_element_type=F32)))
        dsp, dsc = [t.astype(BF16) for t in _swa_split(p * (dp - delta), in_cur)]
        dsink_ref[...] += (-ps * delta).reshape(dsink_ref.shape)
        totals = [jnp.zeros((SWA_BLOCK, LANES), F32) for _ in range(4)]
        for g, (qb, kp, kc, _, _) in enumerate(groups):
            rows = slice(g * SWA_ROWS, (g + 1) * SWA_ROWS)
            dq = (jnp.dot(dsp[rows], kp, preferred_element_type=F32)
                  + jnp.dot(dsc[rows], kc, preferred_element_type=F32)) * scale
            _swa_unstack_heads(dq_ref, g, dq)
            pieces = [lax.dot_general(dsp[rows], qb, TN_DIMS, preferred_element_type=F32) * scale,
                      lax.dot_general(dsc[rows], qb, TN_DIMS, preferred_element_type=F32) * scale,
                      lax.dot_general(ppb[rows], dobs[g], TN_DIMS, preferred_element_type=F32),
                      lax.dot_general(pcb[rows], dobs[g], TN_DIMS, preferred_element_type=F32)]
            totals = [tot + _fold_half(r, g) for tot, r in zip(totals, pieces)]
        dkp_ref[...], dkc_ref[...], dvp_ref[...], dvc_ref[...] = totals

        @pl.when(n == nblk - 1)
        def _():
            exchange_finish()

    kv_shape = jax.ShapeDtypeStruct(k.shape, F32)
    return pl.pallas_call(
        body, name="swa_bwd", grid=(nblk,),
        in_specs=[q_spec, prev_spec, cur_spec, prev_spec, cur_spec, sink_spec, q_spec] + [HBM_SPEC] * n_arr,
        out_specs=[q_spec, cur_spec, cur_spec, cur_spec, cur_spec, dsink_spec] + [HBM_SPEC] * n_arr,
        out_shape=[jax.ShapeDtypeStruct(q.shape, q.dtype), kv_shape, kv_shape, kv_shape, kv_shape,
                   jax.ShapeDtypeStruct((A_KV_HEADS, SWA_ROWS, 1), F32)]
                  + [jax.ShapeDtypeStruct(p.shape, p.dtype) for p in parts],
        scratch_shapes=_exchange_chips_sems(n_arr),
        compiler_params=_params("arbitrary"),
    )(q, k, k, v, v, sinks, do, *parts)


@jax.custom_vjp
def swa_nat(q, k, v, sinks, shards):
    out = _swa_nat_fwd_call(q, k, v, sinks, [s.astype(BF16) for s in shards])
    return out[0], tuple(out[1:])


def _swa_nat_fwd(q, k, v, sinks, shards):
    out = _swa_nat_fwd_call(q, k, v, sinks, [s.astype(BF16) for s in shards])
    return (out[0], tuple(out[1:])), (q, k, v, sinks)


def _swa_nat_bwd(res, cts):
    q, k, v, sinks = res
    do, d_gathered = cts
    out = _swa_nat_bwd_call(q, k, v, sinks, do, _reduce_scatter_head(d_gathered, "mid_grads"))
    dq, dkp, dkc, dvp, dvc, dsink = out[:6]

    def fold(prev_part, cur_part):
        shifted = jnp.concatenate([prev_part[SWA_BLOCK:], jnp.zeros_like(prev_part[:SWA_BLOCK])], axis=0)
        return (cur_part + shifted).astype(k.dtype)

    dsinks = jnp.sum(dsink.reshape(A_HEADS, SWA_BLOCK), axis=1)
    return dq, fold(dkp, dkc), fold(dvp, dvc), dsinks, _reduce_scatter_tail(out[6:], "mid_grads")


swa_nat.defvjp(_swa_nat_fwd, _swa_nat_bwd)

N_PAIR = B_HEADS // 2


def _flash_nat_fwd_call(q, k, v, shards):
    s = q.shape[0]
    t = min(FLASH_T, s)
    nb = s // t
    d = LANES
    n_arr = len(shards)

    def body(*refs):
        q_ref, k_ref, v_ref = refs[:3]
        shard_refs = refs[3:3 + n_arr]
        o_ref, lse_ref = refs[3 + n_arr:5 + n_arr]
        gathered_refs = refs[5 + n_arr:5 + 2 * n_arr]
        vt_ref, m_ref, l_ref, acc_ref = refs[5 + 2 * n_arr:9 + 2 * n_arr]
        pair, i = pl.program_id(0), pl.program_id(1)
        ag_start, ag_forward, ag_finish = _allgather_phases(shard_refs, gathered_refs, *refs[9 + 2 * n_arr:])

        @pl.when((pair == 0) & (i == 0))
        def _():
            ag_start()

        @pl.when((pair == N_PAIR // 2) & (i == 0))
        def _():
            ag_forward()

        @pl.when(i == 0)
        def _():
            for hh in range(2):
                for chunk in range(nb):
                    rows = slice(chunk * t, (chunk + 1) * t)
                    vt_ref[hh, :, rows] = v_ref[rows, hh * d:(hh + 1) * d].T

        m_ref[...] = jnp.full_like(m_ref, -jnp.inf)
        l_ref[...] = jnp.zeros_like(l_ref)
        acc_ref[...] = jnp.zeros_like(acc_ref)

        def step(j, on_diagonal):
            keys = pl.ds(pl.multiple_of(j * t, t), t)
            scores = [lax.dot_general(k_ref[keys, hh * d:(hh + 1) * d], q_ref[:, hh * d:(hh + 1) * d], NT_DIMS,
                                      preferred_element_type=F32) for hh in range(2)]
            for hh in range(2):
                sc_t = scores[hh]
                if on_diagonal:
                    key = lax.broadcasted_iota(jnp.int32, (t, t), 0)
                    qry = lax.broadcasted_iota(jnp.int32, (t, t), 1)
                    sc_t = jnp.where(qry >= key, sc_t, -jnp.inf)
                m_old = m_ref[hh]
                m_new = jnp.maximum(m_old, jnp.max(sc_t, axis=0, keepdims=True))
                alpha = jnp.exp2((m_old - m_new) * EXP2_SCALE)
                p_t = jnp.exp2((sc_t - m_new) * EXP2_SCALE)
                l_ref[hh] = alpha * l_ref[hh] + jnp.sum(p_t, axis=0, keepdims=True)
                acc_ref[hh] = alpha * acc_ref[hh] + jnp.dot(vt_ref[hh, :, keys], p_t.astype(BF16),
                                                            preferred_element_type=F32)
                m_ref[hh] = m_new

        def below(j, carry):
            step(j, False)
            return carry

        lax.fori_loop(0, i, below, 0)
        step(i, True)
        outs =[(acc_ref[hh] / l_ref[hh]).T for hh in range(2)]
        for hh in range(2):
            lse_ref[hh] = m_ref[hh] * EXP2_SCALE + jnp.log2(l_ref[hh])
        o_ref[...] = (outs[0] + pltpu.roll(outs[1], HALF, 1)).astype(o_ref.dtype)

        @pl.when((pair == N_PAIR - 1) & (i == nb - 1))
        def _():
            ag_finish()

    return pl.pallas_call(
        body, name="mla_fwd", grid=(N_PAIR, nb),
        in_specs=[pl.BlockSpec((t, 2 * d), lambda p, i: (i, p)),
                  pl.BlockSpec((s, 2 * d), lambda p, i: (0, p)),
                  pl.BlockSpec((s, 2 * d), lambda p, i: (0, p))] + [HBM_SPEC] * n_arr,
        out_specs=[pl.BlockSpec((t, d), lambda p, i: (i, p)),
                   pl.BlockSpec((2, 1, t), lambda p, i: (p, 0, i))] + [HBM_SPEC] * n_arr,
        out_shape=[jax.ShapeDtypeStruct((s, N_PAIR * d), BF16), jax.ShapeDtypeStruct((B_HEADS, 1, s), F32)]
                  + _allgather_out_shapes(shards),
        scratch_shapes=[pltpu.VMEM((2, d, s), BF16), pltpu.VMEM((2, 1, t), F32), pltpu.VMEM((2, 1, t), F32),
                        pltpu.VMEM((2, d, t), F32)] + _allgather_sems(n_arr),
        compiler_params=_params("arbitrary", "arbitrary"),
    )(q, k, v, *shards)


def _flash_nat_delta_call(o, do):
    s, w = o.shape
    t = min(FLASH_T, s)

    def body(o_ref, do_ref, out_ref):
        prod = o_ref[...].astype(F32) * do_ref[...].astype(F32)
        lane = lax.broadcasted_iota(jnp.int32, (w, LANES), 0) // V_DIM
        head = lax.broadcasted_iota(jnp.int32, (w, LANES), 1)
        out_ref[...] = jnp.dot(prod, (lane == head).astype(F32), precision=lax.Precision.HIGHEST,
                               preferred_element_type=F32)

    spec = pl.BlockSpec((t, w), lambda i: (i, 0))
    return pl.pallas_call(
        body, name="mla_delta", grid=(s // t,), in_specs=[spec, spec],
        out_specs=pl.BlockSpec((t, LANES), lambda i: (i, 0)),
        out_shape=jax.ShapeDtypeStruct((s, LANES), F32), compiler_params=_params("parallel"),
    )(o, do)


def _flash_nat_bwd_call(q, k, v, lse_row, delta_row, do, parts):
    s = q.shape[0]
    t = min(FLASH_T, s)
    nb = s // t
    d = LANES
    n_arr = len(parts)

    def body(*refs):
        q_ref, k_ref, v_ref, lse_ref, delta_ref, do_ref = refs[:6]
        part_refs = refs[6:6 + n_arr]
        dq_ref, dk_ref, dv_ref = refs[6 + n_arr:9 + n_arr]
        received_refs = refs[9 + n_arr:9 + 2 * n_arr]
        dq_acc, dk_acc, dv_acc = refs[9 + 2 * n_arr:12 + 2 * n_arr]
        pair, j = pl.program_id(0), pl.program_id(1)
        exchange_start, exchange_finish = _exchange_chips_phases(part_refs, received_refs, *refs[12 + 2 * n_arr:])

        @pl.when((pair == 0) & (j == 0))
        def _():
            exchange_start()

        @pl.when(j == 0)
        def _():
            dq_acc[...] = jnp.zeros_like(dq_acc)

        for hh in range(2):
            kb, vb = k_ref[:, hh * d:(hh + 1) * d], v_ref[:, hh * d:(hh + 1) * d]
            dk_acc[...] = jnp.zeros_like(dk_acc)
            dv_acc[...] = jnp.zeros_like(dv_acc)

            def step(i, on_diagonal, hh=hh, kb=kb, vb=vb):
                rows = pl.ds(pl.multiple_of(i * t, t), t)
                qb = q_ref[rows, hh * d:(hh + 1) * d]
                do_pair = do_ref[rows, :].astype(F32)
                do_h = do_pair if hh == 0 else pltpu.roll(do_pair, HALF, 1)
                dob = jnp.where(_low_half(do_h.shape), do_h, 0.0).astype(BF16)
                sc_t = lax.dot_general(kb, qb, NT_DIMS, preferred_element_type=F32)
                p_t = jnp.exp2(sc_t * EXP2_SCALE - lse_ref[hh, :, rows])
                if on_diagonal:
                    key = lax.broadcasted_iota(jnp.int32, (t, t), 0)
                    qry = lax.broadcasted_iota(jnp.int32, (t, t), 1)
                    p_t = jnp.where(qry >= key, p_t, 0.0)
                dp_t = lax.dot_general(vb, dob, NT_DIMS, preferred_element_type=F32)
                ds_t = (p_t * (dp_t - delta_ref[hh, :, rows])).astype(BF16)
                dv_acc[...] += jnp.dot(p_t.astype(BF16), dob, preferred_element_type=F32)
                dk_acc[...] += jnp.dot(ds_t, qb, preferred_element_type=F32)
                dq_acc[hh, rows, :] += lax.dot_general(ds_t, kb, TN_DIMS, preferred_element_type=F32)

            def above(i, carry, step=step):
                step(i, False)
                return carry

            step(j, True)
            lax.fori_loop(j + 1, nb, above, 0)
            dk_ref[:, hh * d:(hh + 1) * d] = (dk_acc[...] * MLA_SCALE).astype(dk_ref.dtype)
            dv_ref[:, hh * d:(hh + 1) * d] = dv_acc[...].astype(dv_ref.dtype)

        @pl.when(j == nb - 1)
        def _():
            for hh in range(2):
                dq_ref[:, hh * d:(hh + 1) * d] = (dq_acc[hh] * MLA_SCALE).astype(dq_ref.dtype)

        @pl.when((pair == N_PAIR - 1) & (j == nb - 1))
        def _():
            exchange_finish()

    full_spec = pl.BlockSpec((s, 2 * d), lambda p, j: (0, p))
    tile_spec = pl.BlockSpec((t, 2 * d), lambda p, j: (j, p))
    row_spec = pl.BlockSpec((2, 1, s), lambda p, j: (p, 0, 0))
    return pl.pallas_call(
        body, name="mla_bwd", grid=(N_PAIR, nb),
        in_specs=[full_spec, tile_spec, tile_spec, row_spec, row_spec, pl.BlockSpec((s, d), lambda p, j: (0, p))]
                 + [HBM_SPEC] * n_arr,
        out_specs=[full_spec, tile_spec, tile_spec] + [HBM_SPEC] * n_arr,
        out_shape=[jax.ShapeDtypeStruct(q.shape, q.dtype)] * 3 + [jax.ShapeDtypeStruct(p.shape, p.dtype) for p in parts],
        scratch_shapes=[pltpu.VMEM((2, s, d), F32), pltpu.VMEM((t, d), F32), pltpu.VMEM((t, d), F32)]
                       + _exchange_chips_sems(n_arr),
        compiler_params=_params("arbitrary", "arbitrary"),
    )(q, k, v, lse_row, delta_row, do, *parts)


def _reduce_scatter_head(cts, tag):
    received = _exchange_sibling(list(cts), tag + "_exchange_sibling")
    my_c = lax.axis_index("c").astype(jnp.int32).reshape(1)
    return [_pair_add(m, r, my_c, "%s_pair_add_%d" % (tag, i)) for i, (m, r) in enumerate(zip(cts, received))]


def _reduce_scatter_tail(chip_parts, tag):
    return tuple(_sum_blocks(r, "%s_sum_%d" % (tag, i)) for i, r in enumerate(chip_parts))


@jax.custom_vjp
def flash_nat(q, k, v, shards):
    out = _flash_nat_fwd_call(q, k, v, [s.astype(BF16) for s in shards])
    return out[0], tuple(out[2:])


def _flash_nat_fwd(q, k, v, shards):
    out = _flash_nat_fwd_call(q, k, v, [s.astype(BF16) for s in shards])
    return (out[0], tuple(out[2:])), (q, k, v, out[0], out[1])


def _flash_nat_bwd(res, cts):
    q, k, v, o, lse = res
    do, d_gathered = cts
    delta = _flash_nat_delta_call(o, do)[:, :B_HEADS].T.reshape(B_HEADS, 1, q.shape[0])
    out = _flash_nat_bwd_call(q, k, v, lse, delta, do, _reduce_scatter_head(d_gathered, "mlp_grads"))
    return out[0], out[1], out[2], _reduce_scatter_tail(out[3:], "mlp_grads")


flash_nat.defvjp(_flash_nat_fwd, _flash_nat_bwd)


HBM_SPEC = pl.BlockSpec(memory_space=pltpu.HBM)


def _allgather(shards, name):
    n_arr = len(shards)

    def body(*refs):
        start, forward, finish = _allgather_phases(refs[:n_arr], refs[n_arr:2 * n_arr], *refs[2 * n_arr:])
        start()
        forward()
        finish()

    return pl.pallas_call(
        body, name=name, out_shape=_allgather_out_shapes(shards),
        in_specs=[HBM_SPEC] * n_arr, out_specs=[HBM_SPEC] * n_arr,
        scratch_shapes=_allgather_sems(n_arr),
    )(*shards)


def _allgather_out_shapes(shards):
    return [jax.ShapeDtypeStruct((N_DEV,) + s.shape, s.dtype) for s in shards]


def _allgather_sems(n_arr):
    return [pltpu.SemaphoreType.DMA((7, n_arr)), pltpu.SemaphoreType.DMA((7, n_arr)), pltpu.SemaphoreType.DMA((n_arr,))]


def _allgather_phases(x_refs, out_refs, send_sems, recv_sems, local_sems):
    arrays = range(len(x_refs))
    x, y, c = lax.axis_index("x"), lax.axis_index("y"), lax.axis_index("c")
    me, sibling = (x, y, c), (x, y, 1 - c)
    chips = [(1 - x, y), (x, 1 - y), (1 - x, 1 - y)]

    def rows(a, px, py, pc):
        return out_refs[a].at[4 * px + 2 * py + pc]

    def copy(a, k, block, to, src=None):
        return pltpu.make_async_remote_copy(
            src_ref=rows(a, *block) if src is None else src, dst_ref=rows(a, *block),
            send_sem=send_sems.at[k, a], recv_sem=recv_sems.at[k, a], device_id=to, device_id_type=MESH_ID)

    def mine():
        return [pltpu.make_async_copy(x_refs[a], rows(a, *me), local_sems.at[a]) for a in arrays]

    def first():
        return [cp for a in arrays for cp in
                [copy(a, 0, me, sibling, src=x_refs[a])]
                + [copy(a, 1 + j, me, (*chip, c), src=x_refs[a]) for j, chip in enumerate(chips)]]

    def passed():
        return [copy(a, 4 + j, (*chip, c), sibling) for j, chip in enumerate(chips) for a in arrays]

    def start():
        for cp in mine() + first():
            cp.start()

    def forward():
        for j, chip in enumerate(chips):
            for a in arrays:
                copy(a, 1 + j, (*chip, c), me).wait_recv()
                copy(a, 4 + j, (*chip, c), sibling).start()

    def finish():
        for a in arrays:
            copy(a, 0, sibling, me).wait_recv()
        for j, chip in enumerate(chips):
            for a in arrays:
                copy(a, 4 + j, (*chip, 1 - c), me).wait_recv()
        for cp in first() + passed():
            cp.wait_send()
        for cp in mine():
            cp.wait()

    return start, forward, finish


N_CHIP = 4


def _exchange_sibling(parts, name):
    n_arr = len(parts)

    def body(*refs):
        in_refs, recv_refs = refs[:n_arr], refs[n_arr:2 * n_arr]
        send_sems, recv_sems = refs[2 * n_arr:]
        x, y, c = lax.axis_index("x"), lax.axis_index("y"), lax.axis_index("c")
        copies = []
        for a in range(n_arr):
            for q in range(N_CHIP):
                copies.append(pltpu.make_async_remote_copy(
                    src_ref=in_refs[a].at[2 * q + 1 - c], dst_ref=recv_refs[a].at[q],
                    send_sem=send_sems.at[q, a], recv_sem=recv_sems.at[q, a],
                    device_id=(x, y, 1 - c), device_id_type=MESH_ID))
        for cp in copies:
            cp.start()
        for cp in copies:
            cp.wait()

    return pl.pallas_call(
        body, name=name, out_shape=[jax.ShapeDtypeStruct((N_CHIP,) + p.shape[1:], p.dtype) for p in parts],
        in_specs=[HBM_SPEC] * n_arr, out_specs=[HBM_SPEC] * n_arr,
        scratch_shapes=[pltpu.SemaphoreType.DMA((N_CHIP, n_arr)), pltpu.SemaphoreType.DMA((N_CHIP, n_arr))],
    )(*parts)


def _exchange_chips_sems(n_arr):
    return [pltpu.SemaphoreType.DMA((N_CHIP - 1, n_arr)), pltpu.SemaphoreType.DMA((N_CHIP - 1, n_arr)),
            pltpu.SemaphoreType.DMA((n_arr,))]


def _exchange_chips_phases(in_refs, out_refs, send_sems, recv_sems, local_sems):
    n_arr = len(in_refs)
    x, y, c = lax.axis_index("x"), lax.axis_index("y"), lax.axis_index("c")
    me = 2 * x + y

    def copies():
        out = [pltpu.make_async_copy(in_refs[a].at[me], out_refs[a].at[me], local_sems.at[a]) for a in range(n_arr)]
        for k in range(1, N_CHIP):
            px = 1 - x if k & 2 else x
            py = 1 - y if k & 1 else y
            for a in range(n_arr):
                out.append(pltpu.make_async_remote_copy(
                    src_ref=in_refs[a].at[2 * px + py], dst_ref=out_refs[a].at[me],
                    send_sem=send_sems.at[k - 1, a], recv_sem=recv_sems.at[k - 1, a],
                    device_id=(px, py, c), device_id_type=MESH_ID))
        return out

    def start():
        for cp in copies():
            cp.start()

    def finish():
        for cp in copies():
            cp.wait()

    return start, finish


def _row_tile(r, ccols, blocks):
    cap = max(16, (2 * 1024 * 1024) // (4 * ccols * blocks))
    return _pick(r, cap, 16)


def _pair_add(mine, theirs, my_c, name):
    _, r, ccols = mine.shape
    tr = _row_tile(r, ccols, 1)

    def body(c_ref, a_ref, b_ref, o_ref):
        o_ref[...] = (a_ref[...].astype(F32) + b_ref[...].astype(F32)).astype(o_ref.dtype)

    spec = pl.BlockSpec((None, tr, ccols), lambda q, i, c_ref: (q, i, 0))
    return pl.pallas_call(
        body, name=name,
        grid_spec=pltpu.PrefetchScalarGridSpec(
            num_scalar_prefetch=1, grid=(N_CHIP, r // tr),
            in_specs=[pl.BlockSpec((None, tr, ccols), lambda q, i, c_ref: (2 * q + c_ref[0], i, 0)), spec],
            out_specs=spec),
        out_shape=jax.ShapeDtypeStruct(theirs.shape, theirs.dtype),
        compiler_params=_params("parallel", "parallel"),
    )(my_c, mine, theirs)


def _sum_blocks(parts, name):
    nb, r, ccols = parts.shape
    tr = _row_tile(r, ccols, nb)

    def body(p_ref, o_ref):
        acc = p_ref[0].astype(F32)
        for i in range(1, nb):
            acc = acc + p_ref[i].astype(F32)
        o_ref[...] = acc

    return pl.pallas_call(
        body, name=name, grid=(r // tr,),
        in_specs=[pl.BlockSpec((nb, tr, ccols), lambda i: (0, i, 0))],
        out_specs=pl.BlockSpec((tr, ccols), lambda i: (i, 0)),
        out_shape=jax.ShapeDtypeStruct((r, ccols), F32),
        compiler_params=_params("parallel"),
    )(parts)


@jax.custom_vjp
def replicated(vec):
    return vec


def _replicated_fwd(vec):
    return vec, None


def _replicated_bwd(_, ct):
    return (_sum_blocks(_allgather([ct], "small_grad_allgather")[0], "small_grad_sum"),)


replicated.defvjp(_replicated_fwd, _replicated_bwd)


def _adamw(w, g, m, v, name):
    rows, cols = w.shape
    tr = _pick(rows, 256, 8) if rows % 8 == 0 else rows

    def body(w_ref, g_ref, m_ref, v_ref, d_ref, nm_ref, nv_ref):
        g_ = g_ref[...]
        m_ = ADAM_B1 * m_ref[...] + (1.0 - ADAM_B1) * g_
        v_ = ADAM_B2 * v_ref[...] + (1.0 - ADAM_B2) * jnp.square(g_)
        m_hat = m_ / (1.0 - ADAM_B1 ** ADAM_STEP)
        v_hat = v_ / (1.0 - ADAM_B2 ** ADAM_STEP)
        d_ref[...] = -ADAM_LR * (m_hat / (jnp.sqrt(v_hat) + ADAM_EPS) + ADAM_WD * w_ref[...])
        nm_ref[...] = m_
        nv_ref[...] = v_

    spec = pl.BlockSpec((tr, cols), lambda i: (i, 0))
    return pl.pallas_call(
        body, name=name, grid=(rows // tr,), in_specs=[spec] * 4, out_specs=[spec] * 3,
        out_shape=[jax.ShapeDtypeStruct(w.shape, F32)] * 3, compiler_params=_params("parallel"),
    )(w, g, m, v)


COL_SHARDED = ("w_in", "w_uq", "w_ukv", "w_branch_a", "w_branch_b", "w_up", "w_ple")
EARLY = ("w_in",)
MID = ("w_uq", "w_ukv", "w_branch_a", "w_branch_b", "w_out")
LATE = ("w_up", "w_down", "w_ple_gate", "w_ple")
SMALL = ("attn_pre_norm", "attn_post_norm", "b_gate", "q_a_norm", "kv_a_norm", "mlp_pre_norm", "mlp_post_norm",
         "conv_b", "ple_norm", "sinks")
SMALL_COLS = 128


def _pack_rows(arrays, cols, row_mult):
    flat = jnp.concatenate([a.reshape(-1) for a in arrays])
    pad = (-flat.shape[0]) % (cols * row_mult)
    return jnp.pad(flat, (0, pad)).reshape(-1, cols)


def _unpack_small(vec, shapes):
    flat = vec.reshape(-1)
    out, off = {}, 0
    for name in SMALL:
        n = shapes[name]
        out[name] = flat[off:off + n].reshape(1, n)
        off += n + (-n) % SMALL_COLS
    return out


def _pad_lanes(t, width):
    return jnp.pad(t, [(0, 0)] * (t.ndim - 1) + [(0, width - t.shape[-1])])


def _pad_rows(t, rows):
    return jnp.pad(t, [(0, 0)] * (t.ndim - 2) + [(0, rows - t.shape[-2]), (0, 0)])


FRONT_SIZES = (512, 128, 128, 256, 128)
FRONT_BOUNDS = (0, 512, 640, 768, 1024, 1152, 1280)
PE_LANE = NOPE_DIM


def _arrange_w_in_t(wt):
    k = wt.shape[1]
    n_front = sum(FRONT_SIZES)
    front, kr, gates = wt[:n_front], wt[n_front:n_front + ROPE_DIM], wt[n_front + ROPE_DIM:]
    kr_slab = jnp.concatenate([jnp.zeros((PE_LANE, k), wt.dtype), kr,
                               jnp.zeros((HEAD_PAD - PE_LANE - ROPE_DIM, k), wt.dtype)], axis=0)
    return jnp.concatenate([front, kr_slab], axis=0), gates


def _arrange_w_uq_t(wt):
    k = wt.shape[1]
    return _pad_rows(wt.reshape(B_HEADS, NOPE_DIM + ROPE_DIM, k), HEAD_PAD).reshape(B_HEADS * HEAD_PAD, k)


def _arrange_w_ukv_t(wt):
    k = wt.shape[1]
    w = wt.reshape(B_HEADS, 2, NOPE_DIM, k)
    slabs = [_pad_rows(w[:, part], HEAD_PAD).reshape(B_HEADS * HEAD_PAD, k) for part in range(2)]
    return jnp.concatenate(slabs, axis=0)


def _rope_tables(positions, s):
    pos = positions.reshape(s, 1).astype(F32)

    def angles(dim):
        return pos * ROPE_THETA ** (-(jnp.arange(0, dim, 2, dtype=F32) / dim))

    cos_a, sin_a = jnp.cos(angles(A_HEAD_DIM)), jnp.sin(angles(A_HEAD_DIM))
    zero_a = jnp.zeros_like(sin_a)
    tables_a = [jnp.tile(jnp.concatenate(pair, axis=1), (1, LANES // A_HEAD_DIM))
                for pair in ((cos_a, cos_a), (-sin_a, zero_a), (zero_a, sin_a))]
    cos_b, sin_b = jnp.cos(angles(ROPE_DIM)), jnp.sin(angles(ROPE_DIM))
    zero_b = jnp.zeros_like(sin_b)

    def slab(first, second, fill):
        return jnp.concatenate([jnp.full((s, PE_LANE), fill, F32), first, second,
                                jnp.full((s, HEAD_PAD - PE_LANE - ROPE_DIM), fill, F32)], axis=1)

    tables_b = [slab(cos_b, cos_b, 1.0), slab(-sin_b, zero_b, 0.0), slab(zero_b, sin_b, 0.0)]
    return tables_a + tables_b


def _local_loss(wts, x, p, tables, target):
    s = x.shape[0]
    small_shapes = {n: wts[n].shape[-1] for n in SMALL}
    small_vec = _pack_rows([_pad_lanes(wts[n].reshape(1, -1), small_shapes[n] + (-small_shapes[n]) % SMALL_COLS)
                            for n in SMALL], SMALL_COLS, 8)
    sm = _unpack_small(replicated(small_vec), small_shapes)
    def shard(n):
        return wts[n].T if n in COL_SHARDED else wts[n]

    h1, gathered = prenorm_gather(
        x, sm["attn_pre_norm"], tuple([shard(n) for n in EARLY] + [_pack_rows([wts["conv_w"]], SMALL_COLS, 8)]),
        (BF16,) * len(EARLY) + (F32,))
    big = {n: g.reshape(-1, g.shape[2]) for n, g in zip(EARLY, gathered)}
    ch = wts["conv_w"].shape[1]
    conv_w = gathered[-1].reshape(N_DEV, -1)[:, :CONV_W * ch].reshape(N_DEV, CONV_W, ch)
    conv_w = conv_w.transpose(1, 0, 2).reshape(CONV_W, N_DEV * ch)

    w_front_t, w_gates_t = _arrange_w_in_t(big["w_in"])
    tables_a, tables_b = tables[:3], tables[3:]

    qa, ka, va, cqn, ckvn, kpe = proj_stage(
        "prep", _f_prep, [(h1, w_front_t, "nt", "w_front", True, F32)], params=[sm["q_a_norm"], sm["kv_a_norm"]],
        consts=tables, splits=[FRONT_BOUNDS], ts=512, out_dtypes=[BF16, BF16, BF16, BF16, BF16, F32])
    ya, mid = swa_nat(qa, ka, va, sm["sinks"].reshape(-1), tuple(shard(n) for n in MID))
    big.update({n: g.reshape(-1, g.shape[2]) for n, g in zip(MID, mid)})

    (q2,) = proj_stage("qrope", _f_qrope, [(cqn, _arrange_w_uq_t(big["w_uq"]), "nt", "w_uq", True, BF16)],
                       consts=tables_b, ts=512, out_dtypes=[BF16])
    k2, v2 = proj_stage("kv", _f_kv, [(ckvn, _arrange_w_ukv_t(big["w_ukv"]), "nt", "w_ukv", True, BF16)],
                        extra=[kpe], splits=[(0, B_HEADS * HEAD_PAD, 2 * B_HEADS * HEAD_PAD), None], ts=512,
                        out_dtypes=[BF16, BF16])
    yb, late = flash_nat(q2, k2, v2, tuple(shard(n) for n in LATE))
    big.update({n: g.reshape(-1, g.shape[2]) for n, g in zip(LATE, late)})

    (mixed,) = proj_stage(
        "gate", _f_gate, [(h1, w_gates_t, "nt", "w_gates", True, F32),
                          (ya, big["w_branch_a"], "nt", "w_branch_a", True, BF16),
                          (yb, big["w_branch_b"], "nt", "w_branch_b", True, BF16)],
        params=[sm["b_gate"][:, :D_MODEL], sm["b_gate"][:, D_MODEL:]],
        splits=[(0, D_MODEL, 2 * D_MODEL), None, None], out_dtypes=[BF16])
    x1, h2 = proj_stage("post_attn", _f_post, [(mixed, big["w_out"], "nn", "w_out", True, F32)], extra=[x],
                        params=[sm["attn_post_norm"], sm["mlp_pre_norm"]], ts=512, out_dtypes=[F32, BF16])

    act = mlp_up(h2, big["w_up"], conv_w, sm["conv_b"])
    x2, h3 = proj_stage("post_mlp", _f_post, [(act, big["w_down"], "nn", "w_down", True, F32)], extra=[x1],
                        params=[sm["mlp_post_norm"], sm["ple_norm"]], ts=512, out_dtypes=[F32, BF16])

    (rowloss,) = proj_stage("loss", _f_out, [(h3, big["w_ple_gate"], "nn", "w_ple_gate", True, F32),
                                             (p, big["w_ple"], "nt", "w_ple", False, BF16)], extra=[x2],
                            consts=[target], ts=512)
    return jnp.sum(rowloss)


WEIGHTS = ["attn_pre_norm", "attn_post_norm", "w_in", "b_gate", "sinks", "q_a_norm", "w_uq", "kv_a_norm", "w_ukv",
           "w_branch_a", "w_branch_b", "w_out", "mlp_pre_norm", "mlp_post_norm", "w_up", "conv_w", "conv_b",
           "w_down", "ple_norm", "w_ple_gate", "w_ple"]


def kernel(x, p, positions, attn_pre_norm, attn_post_norm, w_in, b_gate, sinks, q_a_norm, w_uq, kv_a_norm, w_ukv, w_branch_a, w_branch_b, w_out, mlp_pre_norm, mlp_post_norm, w_up, conv_w, conv_b, w_down, ple_norm, w_ple_gate, w_ple, loss_target, m_attn_pre_norm, m_attn_post_norm, m_w_in, m_b_gate, m_sinks, m_q_a_norm, m_w_uq, m_kv_a_norm, m_w_ukv, m_w_branch_a, m_w_branch_b, m_w_out, m_mlp_pre_norm, m_mlp_post_norm, m_w_up, m_conv_w, m_conv_b, m_w_down, m_ple_norm, m_w_ple_gate, m_w_ple, v_attn_pre_norm, v_attn_post_norm, v_w_in, v_b_gate, v_sinks, v_q_a_norm, v_w_uq, v_kv_a_norm, v_w_ukv, v_w_branch_a, v_w_branch_b, v_w_out, v_mlp_pre_norm, v_mlp_post_norm, v_w_up, v_conv_w, v_conv_b, v_w_down, v_ple_norm, v_w_ple_gate, v_w_ple):
    given = dict(locals())
    s = x.shape[1]
    wts = {n: given[n][0] if given[n].ndim == 3 else given[n] for n in WEIGHTS}
    tables = _rope_tables(positions, s)
    local_loss, (grads, grad_x) = jax.value_and_grad(_local_loss, argnums=(0, 1))(
        wts, x[0], p[0, 0], tables, loss_target[0])
    loss = lax.psum(local_loss, AXES)

    outs = {"grad": [], "delta": [], "m": [], "v": []}
    for n in WEIGHTS:
        shape = given[n].shape
        w2 = wts[n].reshape(-1, shape[-1])
        g2 = grads[n].reshape(w2.shape)
        delta, new_m, new_v = _adamw(w2, g2, given["m_" + n].reshape(w2.shape), given["v_" + n].reshape(w2.shape),
                                     "adamw_" + n)
        outs["grad"].append(g2.reshape(shape))
        outs["delta"].append(delta.reshape(shape))
        outs["m"].append(new_m.reshape(shape))
        outs["v"].append(new_v.reshape(shape))
    return (loss, grad_x[None], *outs["grad"], *outs["delta"], *outs["m"], *outs["v"])
```

```python
import functools

import numpy as np
import jax
import jax.numpy as jnp
from jax import lax
from jax.experimental import pallas as pl
from jax.experimental.pallas import tpu as pltpu

F32 = jnp.float32
BF16 = jnp.bfloat16
MESH_ID = pl.DeviceIdType.MESH
AXES = ("x", "y", "c")
N_DEV = 8

D_MODEL = 1024
RMS_EPS = 1e-6
ROPE_THETA = 10000.0
SWA_BLOCK = 128
A_HEADS, A_KV_HEADS, A_HEAD_DIM = 8, 2, 64
A_GROUP = A_HEADS // A_KV_HEADS
B_HEADS, Q_LORA, KV_LORA, NOPE_DIM, ROPE_DIM, V_DIM = 8, 256, 128, 64, 32, 64
D_FF = 2816
CONV_W = 3
HEAD_PAD = 128

ADAM_LR, ADAM_B1, ADAM_B2, ADAM_EPS, ADAM_WD, ADAM_STEP = 0.001, 0.9, 0.999, 1e-08, 0.01, 10

VMEM_LIMIT = 48 * 1024 * 1024
MM_TM, MM_TN, MM_TK_TOKENS = 512, 1408, 1024
MM_VMEM_BUDGET = 36 * 1024 * 1024
FLASH_T = 1024
CONV_TS = 256
CONV_CHUNK = 256


def _params(*sem):
    return pltpu.CompilerParams(dimension_semantics=sem, vmem_limit_bytes=VMEM_LIMIT)


def _pick(dim, cap, mult):
    best = None
    for t in range(mult, min(dim, cap) + 1, mult):
        if dim % t == 0:
            best = t
    return dim if best is None else best


def _divisors(dim, mult):
    return [t for t in range(mult, dim + 1, mult) if dim % t == 0] or [dim]


def _matmul_tiles(m, n, kdim, form, sizes):
    sa, sb, so = sizes
    tk = _pick(kdim, MM_TK_TOKENS, 128) if form == "tn" else kdim
    cap_m = MM_TN if form == "tn" else MM_TM
    best = None
    for tm in _divisors(m, 128):
        for tn in _divisors(n, 128):
            need = 2 * (tm * tk * sa + tk * tn * sb + tm * tn * so) + (tm * tn * 4 if tk != kdim else 0)
            if tm > cap_m or tn > MM_TN or need > MM_VMEM_BUDGET:
                continue
            if best is None or (tm * tn, tm) > (best[0] * best[1], best[0]):
                best = (tm, tn)
    return best[0], best[1], tk


def _matmul(a, b, form, *, out_dtype=F32, name):
    if form == "tn":
        (kdim, m), n = a.shape, b.shape[1]
    else:
        (m, kdim), n = a.shape, (b.shape[1] if form == "nn" else b.shape[0])
    sizes = (a.dtype.itemsize, b.dtype.itemsize, jnp.dtype(out_dtype).itemsize)
    tm, tn, tk = _matmul_tiles(m, n, kdim, form, sizes)
    nk = kdim // tk
    rows_outer = nk > 1 or (m // tm) * b.size * sizes[1] <= (n // tn) * a.size * sizes[0]

    def ij(fn):
        return (lambda i, j, k: fn(i, j, k)) if rows_outer else (lambda j, i, k: fn(i, j, k))

    a_spec = (pl.BlockSpec((tk, tm), ij(lambda i, j, k: (k, i))) if form == "tn"
              else pl.BlockSpec((tm, tk), ij(lambda i, j, k: (i, k))))
    b_spec = (pl.BlockSpec((tn, tk), ij(lambda i, j, k: (j, k))) if form == "nt"
              else pl.BlockSpec((tk, tn), ij(lambda i, j, k: (k, j))))
    dims = (((0 if form == "tn" else 1,), (1 if form == "nt" else 0,)), ((), ()))

    def product(a_ref, b_ref):
        return lax.dot_general(a_ref[...].astype(BF16), b_ref[...].astype(BF16), dims, preferred_element_type=F32)

    if nk == 1:
        def body(a_ref, b_ref, o_ref):
            o_ref[...] = product(a_ref, b_ref).astype(o_ref.dtype)

        scratch = []
    else:
        def body(a_ref, b_ref, o_ref, acc_ref):
            k = pl.program_id(2)

            @pl.when(k == 0)
            def _():
                acc_ref[...] = jnp.zeros_like(acc_ref)

            acc_ref[...] += product(a_ref, b_ref)

            @pl.when(k == nk - 1)
            def _():
                o_ref[...] = acc_ref[...].astype(o_ref.dtype)

        scratch = [pltpu.VMEM((tm, tn), F32)]

    return pl.pallas_call(
        body, name=name, grid=(m // tm, n // tn, nk) if rows_outer else (n // tn, m // tm, nk),
        in_specs=[a_spec, b_spec],
        out_specs=pl.BlockSpec((tm, tn), ij(lambda i, j, k: (i, j))),
        out_shape=jax.ShapeDtypeStruct((m, n), out_dtype),
        scratch_shapes=scratch,
        compiler_params=_params("parallel", "parallel", "arbitrary"),
    )(a, b)


def _pairs(bounds):
    return list(zip(bounds[:-1], bounds[1:]))


def _split(v, bounds):
    return [v[:, a:b] for a, b in _pairs(bounds)]


def _stage_build(name, f, tiled, params, consts, splits, ts, out_dtypes, ct_dtypes=None):
    n_t, n_p, n_c = len(tiled), len(params), len(consts)
    ct_dtypes = [t.dtype for t in tiled] if ct_dtypes is None else ct_dtypes
    s = tiled[0].shape[0]
    ts = min(ts, s)
    grid = (s // ts,)
    if splits is None:
        splits = [None] * n_t
    in_bounds = [(0, t.shape[1]) if b is None else tuple(b) for t, b in zip(tiled, splits)]

    def tile_aval(arr):
        return jax.ShapeDtypeStruct((ts, arr.shape[1]), arr.dtype)

    slab_avals = [[jax.ShapeDtypeStruct((ts, e - a), F32) for a, e in _pairs(b)]
                  for t, b in zip(tiled, in_bounds)]
    out_avals = jax.eval_shape(f, slab_avals, list(params), [tile_aval(c) for c in consts])
    out_bounds = [tuple(np.cumsum([0] + [o.shape[1] for o in slabs]).tolist()) for slabs in out_avals]
    out_dtypes = [F32] * len(out_bounds) if out_dtypes is None else out_dtypes
    out_shapes = [jax.ShapeDtypeStruct((s, b[-1]), d) for b, d in zip(out_bounds, out_dtypes)]

    def row_spec(width):
        return pl.BlockSpec((ts, width), lambda i: (i, 0))

    def par_spec(arr):
        return pl.BlockSpec(arr.shape, lambda i: (0, 0))

    in_specs = ([row_spec(t.shape[1]) for t in tiled] + [par_spec(p) for p in params]
                + [row_spec(c.shape[1]) for c in consts])

    def load(refs):
        t = [_split(r[...].astype(F32), b) for r, b in zip(refs[:n_t], in_bounds)]
        p = [r[...] for r in refs[n_t:n_t + n_p]]
        c = [r[...] for r in refs[n_t + n_p:n_t + n_p + n_c]]
        return t, p, c

    def store(refs, values, bounds):
        for ref, slabs, b in zip(refs, values, bounds):
            for v, (a, e) in zip(slabs, _pairs(b)):
                ref[:, a:e] = v.astype(ref.dtype)

    def run_fwd(tiled, params, consts):
        def body(*refs):
            t, p, c = load(refs)
            store(refs[n_t + n_p + n_c:], f(t, p, c), out_bounds)

        return pl.pallas_call(
            body, name=name + "_fwd", grid=grid, in_specs=in_specs,
            out_specs=[row_spec(b[-1]) for b in out_bounds], out_shape=out_shapes,
            compiler_params=_params("parallel"),
        )(*tiled, *params, *consts)

    def run_bwd(tiled, params, consts, cts):
        n_in = n_t + n_p + n_c
        n_o = len(out_bounds)

        def body(*refs):
            t, p, c = load(refs)
            g = [_split(r[...].astype(F32), b) for r, b in zip(refs[n_in:n_in + n_o], out_bounds)]
            _, pull = jax.vjp(lambda t_, p_: f(t_, p_, c), t, p)
            dt, dp = pull(g)
            store(refs[n_in + n_o:n_in + n_o + n_t], dt, in_bounds)
            first = pl.program_id(0) == 0
            for ref, d in zip(refs[n_in + n_o + n_t:], dp):
                @pl.when(first)
                def _(ref=ref):
                    ref[...] = jnp.zeros_like(ref)

                ref[...] += d

        res = pl.pallas_call(
            body, name=name + "_bwd", grid=grid,
            in_specs=in_specs + [row_spec(b[-1]) for b in out_bounds],
            out_specs=[row_spec(t.shape[1]) for t in tiled] + [par_spec(p) for p in params],
            out_shape=[jax.ShapeDtypeStruct(t.shape, d) for t, d in zip(tiled, ct_dtypes)]
                      + [jax.ShapeDtypeStruct(p.shape, F32) for p in params],
            compiler_params=_params("arbitrary"),
        )(*tiled, *params, *consts, *cts)
        return tuple(res[:n_t]), tuple(res[n_t:])

    return run_fwd, run_bwd


def proj_stage(name, f, projections, extra=(), params=(), consts=(), splits=None, ts=256, out_dtypes=None):
    n_z = len(projections)
    forms = [pr[2] for pr in projections]
    names = [pr[3] for pr in projections]
    need_da = [pr[4] for pr in projections]
    store = [pr[5] for pr in projections]
    extra, params, consts = tuple(extra), tuple(params), tuple(consts)

    def matmuls(a_list, w_list):
        return tuple(_matmul(a, w, form, out_dtype=dt, name=n + "_fwd")
                     for a, w, form, n, dt in zip(a_list, w_list, forms, names, store))

    def build(zs, ct=False):
        ct_dtypes = [BF16] * n_z + [e.dtype for e in extra] if ct else None
        return _stage_build(name, f, tuple(zs) + extra, params, consts, splits, ts, out_dtypes, ct_dtypes)

    @jax.custom_vjp
    def op(a_list, w_list, extra, params, consts):
        zs = matmuls(a_list, w_list)
        return tuple(build(zs)[0](zs + extra, params, consts))

    def op_fwd(a_list, w_list, extra, params, consts):
        zs = matmuls(a_list, w_list)
        return tuple(build(zs)[0](zs + extra, params, consts)), (a_list, w_list, zs, extra, params, consts)

    def op_bwd(res, cts):
        a_list, w_list, zs, extra, params, consts = res
        dt, dp = build(zs, ct=True)[1](zs + extra, params, consts, cts)
        da_list, dw_list = [], []
        for a, w, dz, form, n, want in zip(a_list, w_list, dt[:n_z], forms, names, need_da):
            if form == "nn":
                da = _matmul(dz, w, "nt", out_dtype=a.dtype, name=n + "_da") if want else jnp.zeros_like(a)
                dw = _matmul(a, dz, "tn", out_dtype=w.dtype, name=n + "_dw")
            else:
                da = _matmul(dz, w, "nn", out_dtype=a.dtype, name=n + "_da") if want else jnp.zeros_like(a)
                dw = _matmul(dz, a, "tn", out_dtype=w.dtype, name=n + "_dw")
            da_list.append(da)
            dw_list.append(dw)
        return tuple(da_list), tuple(dw_list), tuple(dt[n_z:]), dp, tuple(jnp.zeros_like(c) for c in consts)

    op.defvjp(op_fwd, op_bwd)
    return op(tuple(pr[0] for pr in projections), tuple(pr[1] for pr in projections), extra, params, consts)


def _rms(t, g):
    return t * lax.rsqrt(jnp.mean(t * t, axis=-1, keepdims=True) + RMS_EPS) * g


@functools.partial(jax.custom_vjp, nondiff_argnums=(1,))
def _lane_roll(t, shift):
    return pltpu.roll(t, shift % t.shape[-1], t.ndim - 1)


def _lane_roll_fwd(t, shift):
    return _lane_roll(t, shift), None


def _lane_roll_bwd(shift, _, ct):
    return (pltpu.roll(ct, (-shift) % ct.shape[-1], ct.ndim - 1),)


_lane_roll.defvjp(_lane_roll_fwd, _lane_roll_bwd)


def _rope_lanes(t, tables, half):
    reps = t.shape[1] // tables[0].shape[1]
    c, s_lo, s_hi = [jnp.concatenate([tb] * reps, axis=1) if reps > 1 else tb for tb in tables]
    return t * c + _lane_roll(t, -half) * s_lo + _lane_roll(t, half) * s_hi


PRENORM_TS = 256


def _prenorm_fwd_call(x, g, shards):
    s, width = x.shape
    ts = min(PRENORM_TS, s)
    nt = s // ts
    n_arr = len(shards)

    def body(*refs):
        x_ref, g_ref = refs[:2]
        o_ref = refs[2 + n_arr]
        i = pl.program_id(0)
        ag_start, ag_forward, ag_finish = _allgather_phases(refs[2:2 + n_arr], refs[3 + n_arr:3 + 2 * n_arr],
                                                            *refs[3 + 2 * n_arr:])

        @pl.when(i == 0)
        def _():
            ag_start()

        @pl.when(i == nt // 2)
        def _():
            ag_forward()

        o_ref[...] = _rms(x_ref[...], g_ref[...]).astype(o_ref.dtype)

        @pl.when(i == nt - 1)
        def _():
            ag_finish()

    return pl.pallas_call(
        body, name="prenorm_fwd", grid=(nt,),
        in_specs=[pl.BlockSpec((ts, width), lambda i: (i, 0)), pl.BlockSpec(g.shape, lambda i: (0, 0))]
                 + [HBM_SPEC] * n_arr,
        out_specs=[pl.BlockSpec((ts, width), lambda i: (i, 0))] + [HBM_SPEC] * n_arr,
        out_shape=[jax.ShapeDtypeStruct(x.shape, BF16)] + _allgather_out_shapes(shards),
        scratch_shapes=_allgather_sems(n_arr),
        compiler_params=_params("arbitrary"),
    )(x, g, *shards)


def _prenorm_bwd_call(x, g, dh_a, dh_b, dx_res, parts):
    s, width = x.shape
    ts = min(PRENORM_TS, s)
    nt = s // ts
    n_arr = len(parts)

    def body(*refs):
        x_ref, g_ref, dha_ref, dhb_ref, dxr_ref = refs[:5]
        dx_ref, dg_ref = refs[5 + n_arr:7 + n_arr]
        i = pl.program_id(0)
        exchange_start, exchange_finish = _exchange_chips_phases(
            refs[5:5 + n_arr], refs[7 + n_arr:7 + 2 * n_arr], *refs[7 + 2 * n_arr:])

        @pl.when(i == 0)
        def _():
            exchange_start()
            dg_ref[...] = jnp.zeros_like(dg_ref)

        _, pull = jax.vjp(_rms, x_ref[...], g_ref[...])
        dx, dg = pull(dha_ref[...].astype(F32) + dhb_ref[...].astype(F32))
        dx_ref[...] = dx + dxr_ref[...]
        dg_ref[...] += dg

        @pl.when(i == nt - 1)
        def _():
            exchange_finish()

    row = pl.BlockSpec((ts, width), lambda i: (i, 0))
    par = pl.BlockSpec(g.shape, lambda i: (0, 0))
    return pl.pallas_call(
        body, name="prenorm_bwd", grid=(nt,),
        in_specs=[row, par, row, row, row] + [HBM_SPEC] * n_arr,
        out_specs=[row, par] + [HBM_SPEC] * n_arr,
        out_shape=[jax.ShapeDtypeStruct(x.shape, F32), jax.ShapeDtypeStruct(g.shape, F32)]
                  + [jax.ShapeDtypeStruct(p.shape, p.dtype) for p in parts],
        scratch_shapes=_exchange_chips_sems(n_arr),
        compiler_params=_params("arbitrary"),
    )(x, g, dh_a, dh_b, dx_res, *parts)


@functools.partial(jax.custom_vjp, nondiff_argnums=(3,))
def prenorm_gather(x, g, shards, wire_dtypes):
    out = _prenorm_fwd_call(x, g, [s.astype(d) for s, d in zip(shards, wire_dtypes)])
    return out[0], out[0], x, tuple(out[1:])


def _prenorm_gather_fwd(x, g, shards, wire_dtypes):
    return prenorm_gather(x, g, shards, wire_dtypes), (x, g)


def _prenorm_gather_bwd(wire_dtypes, res, cts):
    x, g = res
    dh_a, dh_b, dx_res, d_gathered = cts
    out = _prenorm_bwd_call(x, g, dh_a, dh_b, dx_res, _reduce_scatter_head(d_gathered, "grads"))
    return out[0], out[1], _reduce_scatter_tail(out[2:], "grads")


prenorm_gather.defvjp(_prenorm_gather_fwd, _prenorm_gather_bwd)


def _f_prep(t, p, c):
    qa, ka, va, cq, ckv, kr = t[0]
    return [[_rope_lanes(qa, c[0:3], A_HEAD_DIM // 2)], [_rope_lanes(ka, c[0:3], A_HEAD_DIM // 2)], [va],
            [_rms(cq, p[0])], [_rms(ckv, p[1])], [_rope_lanes(kr, c[3:6], ROPE_DIM // 2)]]


def _f_qrope(t, p, c):
    return [[_rope_lanes(t[0][0], c, ROPE_DIM // 2)]]


def _f_kv(t, p, c):
    (k_nope, v), (k_pe,) = t
    return [[k_nope + jnp.concatenate([k_pe] * B_HEADS, axis=1)], [v]]


def _f_gate(t, p, c):
    (ga, gb), (pa,), (pb,) = t
    ba, bb = p
    return [[jax.nn.sigmoid(ga + ba) * pa + jax.nn.sigmoid(gb + bb) * pb]]


def _f_post(t, p, c):
    (branch,), (residual,) = t
    x1 = residual + _rms(branch, p[0])
    return [[x1], [_rms(x1, p[1])]]


def _f_out(t, p, c):
    (gate,), (emb,), (x2,) = t
    y = x2 + jax.nn.sigmoid(gate) * emb
    err = y - c[0]
    return [[0.5 * jnp.mean(err * err, axis=-1, keepdims=True)]]


def _shift_down(cur, prev, has_prev):
    full = jnp.concatenate([prev * has_prev, cur], axis=0)
    return pltpu.roll(full, 1, 0)[HALO:], pltpu.roll(full, 2, 0)[HALO:]


GELU_C = float(np.sqrt(2.0 / np.pi))
GELU_A = 0.044715
HALO = 8


def _gelu_tanh(x):
    x2 = x * x
    th = jnp.tanh(x * (GELU_C + (GELU_C * GELU_A) * x2))
    half = 0.5 + 0.5 * th
    return x * half, half + x * (0.5 - 0.5 * (th * th)) * (GELU_C + (3.0 * GELU_C * GELU_A) * x2)


def _row_sum(t):
    return jnp.sum(t, axis=0, keepdims=True)


def _conv3(cur, prev, w_ref, b_ref, has_prev):
    u1, u2 = _shift_down(cur, prev, has_prev)
    return w_ref[2:3, :] * cur + w_ref[1:2, :] * u1 + w_ref[0:1, :] * u2 + b_ref[...], u1, u2


def _mlp_act_specs(s):
    ts = min(CONV_TS, s)
    hb = ts // HALO

    def half_specs(h):
        return [pl.BlockSpec((ts, D_FF), lambda i: (i, h)),
                pl.BlockSpec((HALO, D_FF), lambda i: (jnp.maximum(i * hb - 1, 0), h))]

    def par_specs(h):
        return [pl.BlockSpec((CONV_W, D_FF), lambda i: (0, h)), pl.BlockSpec((1, D_FF), lambda i: (0, h))]

    return ts, hb, half_specs, par_specs


def _mlp_act_fwd_call(up, conv_w, conv_b):
    s = up.shape[0]
    ts, hb, half_specs, par_specs = _mlp_act_specs(s)

    def body(g_ref, gp_ref, v_ref, vp_ref, wg_ref, bg_ref, wv_ref, bv_ref, o_ref):
        has_prev = (pl.program_id(0) > 0).astype(F32)

        def chunk(cidx, carry):
            cols = pl.ds(pl.multiple_of(cidx * CONV_CHUNK, CONV_CHUNK), CONV_CHUNK)
            u_g, _, _ = _conv3(g_ref[:, cols], gp_ref[:, cols], wg_ref.at[:, cols], bg_ref.at[:, cols], has_prev)
            u_v, _, _ = _conv3(v_ref[:, cols], vp_ref[:, cols], wv_ref.at[:, cols], bv_ref.at[:, cols], has_prev)
            o_ref[:, cols] = (_gelu_tanh(u_g)[0] * u_v).astype(o_ref.dtype)
            return carry

        lax.fori_loop(0, D_FF // CONV_CHUNK, chunk, 0)

    return pl.pallas_call(
        body, name="mlp_act_fwd", grid=(s // ts,),
        in_specs=half_specs(0) + half_specs(1) + par_specs(0) + par_specs(1),
        out_specs=pl.BlockSpec((ts, D_FF), lambda i: (i, 0)),
        out_shape=jax.ShapeDtypeStruct((s, D_FF), BF16),
        compiler_params=_params("parallel"),
    )(up, up, up, up, conv_w, conv_b, conv_w, conv_b)


def _mlp_act_bwd_call(up, conv_w, conv_b, dact):
    s = up.shape[0]
    ts, hb, half_specs, par_specs = _mlp_act_specs(s)
    nt = s // ts
    ext = ts + HALO
    bf16_rows = 2 * HALO

    def next_spec(rows, h):
        return pl.BlockSpec((rows, D_FF), lambda i: (jnp.minimum((i + 1) * (ts // rows), s // rows - 1), h))

    def body(g_ref, gp_ref, gn_ref, v_ref, vp_ref, vn_ref, wg_ref, bg_ref, wv_ref, bv_ref, da_ref, dan_ref,
             dup_ref, dwg_ref, dbg_ref, dwv_ref, dbv_ref):
        i = pl.program_id(0)
        has_prev, has_next = (i > 0).astype(F32), (i < nt - 1).astype(F32)

        @pl.when(i == 0)
        def _():
            for ref in (dwg_ref, dbg_ref, dwv_ref, dbv_ref):
                ref[...] = jnp.zeros_like(ref)

        def chunk(cidx, carry):
            cols = pl.ds(pl.multiple_of(cidx * CONV_CHUNK, CONV_CHUNK), CONV_CHUNK)
            g_ext = jnp.concatenate([g_ref[:, cols], gn_ref[:, cols]], axis=0)
            v_ext = jnp.concatenate([v_ref[:, cols], vn_ref[:, cols]], axis=0)
            u_g, g1, g2 = _conv3(g_ext, gp_ref[:, cols], wg_ref.at[:, cols], bg_ref.at[:, cols], has_prev)
            u_v, v1, v2 = _conv3(v_ext, vp_ref[:, cols], wv_ref.at[:, cols], bv_ref.at[:, cols], has_prev)
            da_ext = jnp.concatenate([da_ref[:, cols].astype(F32),
                                      dan_ref[:, cols].astype(F32)[0:HALO] * has_next], axis=0)
            act_g, dact_g = _gelu_tanh(u_g)
            du_g = da_ext * u_v * dact_g
            du_v = da_ext * act_g
            for du, w_ref, x0, x1, x2, dw_ref, db_ref, lo in ((du_g, wg_ref, g_ext, g1, g2, dwg_ref, dbg_ref, 0),
                                                          (du_v, wv_ref, v_ext, v1, v2, dwv_ref, dbv_ref, D_FF)):
                d1 = pltpu.roll(du, ext - 1, 0)
                d2 = pltpu.roll(du, ext - 2, 0)
                dup = w_ref[2:3, cols] * du + w_ref[1:2, cols] * d1 + w_ref[0:1, cols] * d2
                out_cols = pl.ds(pl.multiple_of(lo + cidx * CONV_CHUNK, CONV_CHUNK), CONV_CHUNK)
                dup_ref[:, out_cols] = dup[0:ts].astype(dup_ref.dtype)
                own = du[0:ts]
                dw_ref[0:1, cols] += _row_sum(own * x2[0:ts])
                dw_ref[1:2, cols] += _row_sum(own * x1[0:ts])
                dw_ref[2:3, cols] += _row_sum(own * x0[0:ts])
                db_ref[:, cols] += _row_sum(own)
            return carry

        lax.fori_loop(0, D_FF // CONV_CHUNK, chunk, 0)

    par_out = [pl.BlockSpec((CONV_W, D_FF), lambda i: (0, 0)), pl.BlockSpec((1, D_FF), lambda i: (0, 0))]
    par_shapes = [jax.ShapeDtypeStruct((CONV_W, D_FF), F32), jax.ShapeDtypeStruct((1, D_FF), F32)]
    return pl.pallas_call(
        body, name="mlp_act_bwd", grid=(nt,),
        in_specs=(half_specs(0) + [next_spec(HALO, 0)] + half_specs(1) + [next_spec(HALO, 1)]
                  + par_specs(0) + par_specs(1)
                  + [pl.BlockSpec((ts, D_FF), lambda i: (i, 0)), next_spec(bf16_rows, 0)]),
        out_specs=[pl.BlockSpec((ts, 2 * D_FF), lambda i: (i, 0))] + par_out + par_out,
        out_shape=[jax.ShapeDtypeStruct((s, 2 * D_FF), BF16)] + par_shapes + par_shapes,
        compiler_params=_params("arbitrary"),
    )(up, up, up, up, up, up, conv_w, conv_b, conv_w, conv_b, dact, dact)


@jax.custom_vjp
def mlp_up(h2, w_up_t, conv_w, conv_b):
    return _mlp_act_fwd_call(_matmul(h2, w_up_t, "nt", out_dtype=F32, name="w_up_fwd"), conv_w, conv_b)


def _mlp_up_fwd(h2, w_up_t, conv_w, conv_b):
    up = _matmul(h2, w_up_t, "nt", out_dtype=F32, name="w_up_fwd")
    return _mlp_act_fwd_call(up, conv_w, conv_b), (h2, w_up_t, up, conv_w, conv_b)


def _mlp_up_bwd(res, dact):
    h2, w_up_t, up, conv_w, conv_b = res
    dup, dwg, dbg, dwv, dbv = _mlp_act_bwd_call(up, conv_w, conv_b, dact)
    dh2 = _matmul(dup, w_up_t, "nn", out_dtype=h2.dtype, name="w_up_da")
    dw = _matmul(dup, h2, "tn", out_dtype=w_up_t.dtype, name="w_up_dw")
    return dh2, dw, jnp.concatenate([dwg, dwv], axis=1), jnp.concatenate([dbg, dbv], axis=1)


mlp_up.defvjp(_mlp_up_fwd, _mlp_up_bwd)


SWA_ROWS = A_GROUP * SWA_BLOCK


def _swa_sink_rows(sink_ref, g):
    return jnp.concatenate([jnp.full((SWA_BLOCK, 1), sink_ref[g * A_GROUP + h], F32) for h in range(A_GROUP)], axis=0)


def _swa_operands(q_ref, kp_ref, kc_ref, vp_ref, vc_ref, sink_ref):
    groups = []
    for g in range(A_KV_HEADS):
        groups.append((_swa_stack_heads(q_ref, g), _dup_half(kp_ref[...], g), _dup_half(kc_ref[...], g),
                       _dup_half(vp_ref[...], g), _dup_half(vc_ref[...], g)))
    return groups, jnp.concatenate([_swa_sink_rows(sink_ref, g) for g in range(A_KV_HEADS)], axis=0)


def _swa_probs(groups, sink, prev_off):
    scale = A_HEAD_DIM ** -0.5
    sp = jnp.concatenate([lax.dot_general(gr[0], gr[1], NT_DIMS, preferred_element_type=F32) for gr in groups], axis=0)
    sc = jnp.concatenate([lax.dot_general(gr[0], gr[2], NT_DIMS, preferred_element_type=F32) for gr in groups], axis=0)
    qi = lax.broadcasted_iota(jnp.int32, sp.shape, 0) & (SWA_BLOCK - 1)
    kj = lax.broadcasted_iota(jnp.int32, sp.shape, 1)
    in_cur = kj <= qi
    sw = jnp.where(in_cur, sc, jnp.where(kj > qi + prev_off, sp, -jnp.inf)) * scale
    m = jnp.maximum(jnp.max(sw, axis=-1, keepdims=True), sink)
    e, es = jnp.exp(sw - m), jnp.exp(sink - m)
    den = jnp.sum(e, axis=-1, keepdims=True) + es
    return e / den, in_cur, es / den


def _swa_split(t, in_cur):
    cur = jnp.where(in_cur, t, 0.0)
    return t - cur, cur


MLA_SCALE = (NOPE_DIM + ROPE_DIM) ** -0.5
EXP2_SCALE = MLA_SCALE * float(np.log2(np.e))
NT_DIMS = (((1,), (1,)), ((), ()))
TN_DIMS = (((0,), (0,)), ((), ()))


LANES = 128
HALF = LANES // 2


def _low_half(shape):
    return lax.broadcasted_iota(jnp.int32, shape, len(shape) - 1) < HALF


def _dup_half(x, g):
    xf = x.astype(F32)
    keep = _low_half(xf.shape) if g == 0 else jnp.logical_not(_low_half(xf.shape))
    xm = jnp.where(keep, xf, 0.0)
    return (xm + pltpu.roll(xm, HALF, 1)).astype(x.dtype)


def _fold_half(r, g):
    total = r + pltpu.roll(r, HALF, 1)
    keep = _low_half(r.shape) if g == 0 else jnp.logical_not(_low_half(r.shape))
    return jnp.where(keep, total, 0.0)


def _swa_stack_heads(ref, g):
    parts = []
    for tile in range(2):
        slab = ref[:, (2 * g + tile) * LANES:(2 * g + tile + 1) * LANES]
        low = _low_half(slab.shape)
        parts += [jnp.where(low, slab, jnp.zeros_like(slab)), jnp.where(low, jnp.zeros_like(slab), slab)]
    return jnp.concatenate(parts, axis=0)


def _swa_unstack_heads(ref, g, rows):
    for tile in range(2):
        a = rows[(2 * tile) * SWA_BLOCK:(2 * tile + 1) * SWA_BLOCK]
        b = rows[(2 * tile + 1) * SWA_BLOCK:(2 * tile + 2) * SWA_BLOCK]
        ref[:, (2 * g + tile) * LANES:(2 * g + tile + 1) * LANES] = jnp.where(_low_half(a.shape), a, b).astype(ref.dtype)


def _swa_nat_specs():
    blk = SWA_BLOCK
    q_spec = pl.BlockSpec((blk, A_HEADS * A_HEAD_DIM), lambda n: (n, 0))
    prev_spec = pl.BlockSpec((blk, LANES), lambda n: (jnp.maximum(n - 1, 0), 0))
    cur_spec = pl.BlockSpec((blk, LANES), lambda n: (n, 0))
    return q_spec, prev_spec, cur_spec, pl.BlockSpec(memory_space=pltpu.SMEM)


def _swa_nat_fwd_call(q, k, v, sinks, shards):
    s = q.shape[0]
    nblk = s // SWA_BLOCK
    n_arr = len(shards)
    q_spec, prev_spec, cur_spec, sink_spec = _swa_nat_specs()

    def body(*refs):
        q_ref, kp_ref, kc_ref, vp_ref, vc_ref, sink_ref = refs[:6]
        o_ref = refs[6 + n_arr]
        n = pl.program_id(0)
        ag_start, ag_forward, ag_finish = _allgather_phases(refs[6:6 + n_arr], refs[7 + n_arr:7 + 2 * n_arr],
                                                            *refs[7 + 2 * n_arr:])

        @pl.when(n == 0)
        def _():
            ag_start()

        @pl.when(n == nblk // 2)
        def _():
            ag_forward()

        prev_off = jnp.where(n > 0, 0, SWA_BLOCK)
        groups, sink = _swa_operands(q_ref, kp_ref, kc_ref, vp_ref, vc_ref, sink_ref)
        p, in_cur, _ = _swa_probs(groups, sink, prev_off)
        ppb, pcb = [t.astype(BF16) for t in _swa_split(p, in_cur)]
        for g, (_, _, _, vp, vc) in enumerate(groups):
            rows = slice(g * SWA_ROWS, (g + 1) * SWA_ROWS)
            out = (jnp.dot(ppb[rows], vp, preferred_element_type=F32)
                   + jnp.dot(pcb[rows], vc, preferred_element_type=F32))
            _swa_unstack_heads(o_ref, g, out)

        @pl.when(n == nblk - 1)
        def _():
            ag_finish()

    return pl.pallas_call(
        body, name="swa_fwd", grid=(nblk,),
        in_specs=[q_spec, prev_spec, cur_spec, prev_spec, cur_spec, sink_spec] + [HBM_SPEC] * n_arr,
        out_specs=[q_spec] + [HBM_SPEC] * n_arr,
        out_shape=[jax.ShapeDtypeStruct(q.shape, BF16)] + _allgather_out_shapes(shards),
        scratch_shapes=_allgather_sems(n_arr),
        compiler_params=_params("arbitrary"),
    )(q, k, k, v, v, sinks, *shards)


def _swa_nat_bwd_call(q, k, v, sinks, do, parts):
    s = q.shape[0]
    nblk = s // SWA_BLOCK
    n_arr = len(parts)
    q_spec, prev_spec, cur_spec, sink_spec = _swa_nat_specs()
    scale = A_HEAD_DIM ** -0.5
    dsink_spec = pl.BlockSpec((A_KV_HEADS, SWA_ROWS, 1), lambda n: (0, 0, 0))

    def body(*refs):
        q_ref, kp_ref, kc_ref, vp_ref, vc_ref, sink_ref, do_ref = refs[:7]
        dq_ref, dkp_ref, dkc_ref, dvp_ref, dvc_ref, dsink_ref = refs[7 + n_arr:13 + n_arr]
        n = pl.program_id(0)
        exchange_start, exchange_finish = _exchange_chips_phases(
            refs[7:7 + n_arr], refs[13 + n_arr:13 + 2 * n_arr], *refs[13 + 2 * n_arr:])

        @pl.when(n == 0)
        def _():
            exchange_start()
        prev_off = jnp.where(n > 0, 0, SWA_BLOCK)

        @pl.when(n == 0)
        def _():
            dsink_ref[...] = jnp.zeros_like(dsink_ref)

        groups, sink = _swa_operands(q_ref, kp_ref, kc_ref, vp_ref, vc_ref, sink_ref)
        dobs = [_swa_stack_heads(do_ref, g) for g in range(A_KV_HEADS)]
        p, in_cur, ps = _swa_probs(groups, sink, prev_off)
        ppb, pcb = [t.astype(BF16) for t in _swa_split(p, in_cur)]

        def per_group(fn):
            return jnp.concatenate([fn(g, slice(g * SWA_ROWS, (g + 1) * SWA_ROWS)) for g in range(A_KV_HEADS)], axis=0)

        out = per_group(lambda g, rows: jnp.dot(ppb[rows], groups[g][3], preferred_element_type=F32)
                        + jnp.dot(pcb[rows], groups[g][4], preferred_element_type=F32))
        delta = jnp.sum(jnp.concatenate(dobs, axis=0).astype(F32) * out, axis=-1, keepdims=True)
        dp = jnp.where(in_cur,
                       per_group(lambda g, rows: lax.dot_general(dobs[g], groups[g][4], NT_DIMS,
                                                                 preferred_element_type=F32)),
                       per_group(lambda g, rows: lax.dot_general(dobs[g], groups[g][3], NT_DIMS,
                                                                 preferred_element_type=F32)))
        dsp, dsc = [t.astype(BF16) for t in _swa_split(p * (dp - delta), in_cur)]
        dsink_ref[...] += (-ps * delta).reshape(dsink_ref.shape)
        totals = [jnp.zeros((SWA_BLOCK, LANES), F32) for _ in range(4)]
        for g, (qb, kp, kc, _, _) in enumerate(groups):
            rows = slice(g * SWA_ROWS, (g + 1) * SWA_ROWS)
            dq = (jnp.dot(dsp[rows], kp, preferred_element_type=F32)
                  + jnp.dot(dsc[rows], kc, preferred_element_type=F32)) * scale
            _swa_unstack_heads(dq_ref, g, dq)
            pieces = [lax.dot_general(dsp[rows], qb, TN_DIMS, preferred_element_type=F32) * scale,
                      lax.dot_general(dsc[rows], qb, TN_DIMS, preferred_element_type=F32) * scale,
                      lax.dot_general(ppb[rows], dobs[g], TN_DIMS, preferred_element_type=F32),
                      lax.dot_general(pcb[rows], dobs[g], TN_DIMS, preferred_element_type=F32)]
            totals = [tot + _fold_half(r, g) for tot, r in zip(totals, pieces)]
        dkp_ref[...], dkc_ref[...], dvp_ref[...], dvc_ref[...] = totals

        @pl.when(n == nblk - 1)
        def _():
            exchange_finish()

    kv_shape = jax.ShapeDtypeStruct(k.shape, F32)
    return pl.pallas_call(
        body, name="swa_bwd", grid=(nblk,),
        in_specs=[q_spec, prev_spec, cur_spec, prev_spec, cur_spec, sink_spec, q_spec] + [HBM_SPEC] * n_arr,
        out_specs=[q_spec, cur_spec, cur_spec, cur_spec, cur_spec, dsink_spec] + [HBM_SPEC] * n_arr,
        out_shape=[jax.ShapeDtypeStruct(q.shape, q.dtype), kv_shape, kv_shape, kv_shape, kv_shape,
                   jax.ShapeDtypeStruct((A_KV_HEADS, SWA_ROWS, 1), F32)]
                  + [jax.ShapeDtypeStruct(p.shape, p.dtype) for p in parts],
        scratch_shapes=_exchange_chips_sems(n_arr),
        compiler_params=_params("arbitrary"),
    )(q, k, k, v, v, sinks, do, *parts)


@jax.custom_vjp
def swa_nat(q, k, v, sinks, shards):
    out = _swa_nat_fwd_call(q, k, v, sinks, [s.astype(BF16) for s in shards])
    return out[0], tuple(out[1:])


def _swa_nat_fwd(q, k, v, sinks, shards):
    out = _swa_nat_fwd_call(q, k, v, sinks, [s.astype(BF16) for s in shards])
    return (out[0], tuple(out[1:])), (q, k, v, sinks)


def _swa_nat_bwd(res, cts):
    q, k, v, sinks = res
    do, d_gathered = cts
    out = _swa_nat_bwd_call(q, k, v, sinks, do, _reduce_scatter_head(d_gathered, "mid_grads"))
    dq, dkp, dkc, dvp, dvc, dsink = out[:6]

    def fold(prev_part, cur_part):
        shifted = jnp.concatenate([prev_part[SWA_BLOCK:], jnp.zeros_like(prev_part[:SWA_BLOCK])], axis=0)
        return (cur_part + shifted).astype(k.dtype)

    dsinks = jnp.sum(dsink.reshape(A_HEADS, SWA_BLOCK), axis=1)
    return dq, fold(dkp, dkc), fold(dvp, dvc), dsinks, _reduce_scatter_tail(out[6:], "mid_grads")


swa_nat.defvjp(_swa_nat_fwd, _swa_nat_bwd)

N_PAIR = B_HEADS // 2


def _flash_nat_fwd_call(q, k, v, shards):
    s = q.shape[0]
    t = min(FLASH_T, s)
    nb = s // t
    d = LANES
    n_arr = len(shards)

    def body(*refs):
        q_ref, k_ref, v_ref = refs[:3]
        shard_refs = refs[3:3 + n_arr]
        o_ref, lse_ref = refs[3 + n_arr:5 + n_arr]
        gathered_refs = refs[5 + n_arr:5 + 2 * n_arr]
        vt_ref, m_ref, l_ref, acc_ref = refs[5 + 2 * n_arr:9 + 2 * n_arr]
        pair, i = pl.program_id(0), pl.program_id(1)
        ag_start, ag_forward, ag_finish = _allgather_phases(shard_refs, gathered_refs, *refs[9 + 2 * n_arr:])

        @pl.when((pair == 0) & (i == 0))
        def _():
            ag_start()

        @pl.when((pair == N_PAIR // 2) & (i == 0))
        def _():
            ag_forward()

        @pl.when(i == 0)
        def _():
            for hh in range(2):
                for chunk in range(nb):
                    rows = slice(chunk * t, (chunk + 1) * t)
                    vt_ref[hh, :, rows] = v_ref[rows, hh * d:(hh + 1) * d].T

        m_ref[...] = jnp.full_like(m_ref, -jnp.inf)
        l_ref[...] = jnp.zeros_like(l_ref)
        acc_ref[...] = jnp.zeros_like(acc_ref)

        def step(j, on_diagonal):
            keys = pl.ds(pl.multiple_of(j * t, t), t)
            scores = [lax.dot_general(k_ref[keys, hh * d:(hh + 1) * d], q_ref[:, hh * d:(hh + 1) * d], NT_DIMS,
                                      preferred_element_type=F32) for hh in range(2)]
            for hh in range(2):
                sc_t = scores[hh]
                if on_diagonal:
                    key = lax.broadcasted_iota(jnp.int32, (t, t), 0)
                    qry = lax.broadcasted_iota(jnp.int32, (t, t), 1)
                    sc_t = jnp.where(qry >= key, sc_t, -jnp.inf)
                m_old = m_ref[hh]
                m_new = jnp.maximum(m_old, jnp.max(sc_t, axis=0, keepdims=True))
                alpha = jnp.exp2((m_old - m_new) * EXP2_SCALE)
                p_t = jnp.exp2((sc_t - m_new) * EXP2_SCALE)
                l_ref[hh] = alpha * l_ref[hh] + jnp.sum(p_t, axis=0, keepdims=True)
                acc_ref[hh] = alpha * acc_ref[hh] + jnp.dot(vt_ref[hh, :, keys], p_t.astype(BF16),
                                                            preferred_element_type=F32)
                m_ref[hh] = m_new

        def below(j, carry):
            step(j, False)
            return carry

        lax.fori_loop(0, i, below, 0)
        step(i, True)
        outs =[(acc_ref[hh] / l_ref[hh]).T for hh in range(2)]
        for hh in range(2):
            lse_ref[hh] = m_ref[hh] * EXP2_SCALE + jnp.log2(l_ref[hh])
        o_ref[...] = (outs[0] + pltpu.roll(outs[1], HALF, 1)).astype(o_ref.dtype)

        @pl.when((pair == N_PAIR - 1) & (i == nb - 1))
        def _():
            ag_finish()

    return pl.pallas_call(
        body, name="mla_fwd", grid=(N_PAIR, nb),
        in_specs=[pl.BlockSpec((t, 2 * d), lambda p, i: (i, p)),
                  pl.BlockSpec((s, 2 * d), lambda p, i: (0, p)),
                  pl.BlockSpec((s, 2 * d), lambda p, i: (0, p))] + [HBM_SPEC] * n_arr,
        out_specs=[pl.BlockSpec((t, d), lambda p, i: (i, p)),
                   pl.BlockSpec((2, 1, t), lambda p, i: (p, 0, i))] + [HBM_SPEC] * n_arr,
        out_shape=[jax.ShapeDtypeStruct((s, N_PAIR * d), BF16), jax.ShapeDtypeStruct((B_HEADS, 1, s), F32)]
                  + _allgather_out_shapes(shards),
        scratch_shapes=[pltpu.VMEM((2, d, s), BF16), pltpu.VMEM((2, 1, t), F32), pltpu.VMEM((2, 1, t), F32),
                        pltpu.VMEM((2, d, t), F32)] + _allgather_sems(n_arr),
        compiler_params=_params("arbitrary", "arbitrary"),
    )(q, k, v, *shards)


def _flash_nat_delta_call(o, do):
    s, w = o.shape
    t = min(FLASH_T, s)

    def body(o_ref, do_ref, out_ref):
        prod = o_ref[...].astype(F32) * do_ref[...].astype(F32)
        lane = lax.broadcasted_iota(jnp.int32, (w, LANES), 0) // V_DIM
        head = lax.broadcasted_iota(jnp.int32, (w, LANES), 1)
        out_ref[...] = jnp.dot(prod, (lane == head).astype(F32), precision=lax.Precision.HIGHEST,
                               preferred_element_type=F32)

    spec = pl.BlockSpec((t, w), lambda i: (i, 0))
    return pl.pallas_call(
        body, name="mla_delta", grid=(s // t,), in_specs=[spec, spec],
        out_specs=pl.BlockSpec((t, LANES), lambda i: (i, 0)),
        out_shape=jax.ShapeDtypeStruct((s, LANES), F32), compiler_params=_params("parallel"),
    )(o, do)


def _flash_nat_bwd_call(q, k, v, lse_row, delta_row, do, parts):
    s = q.shape[0]
    t = min(FLASH_T, s)
    nb = s // t
    d = LANES
    n_arr = len(parts)

    def body(*refs):
        q_ref, k_ref, v_ref, lse_ref, delta_ref, do_ref = refs[:6]
        part_refs = refs[6:6 + n_arr]
        dq_ref, dk_ref, dv_ref = refs[6 + n_arr:9 + n_arr]
        received_refs = refs[9 + n_arr:9 + 2 * n_arr]
        dq_acc, dk_acc, dv_acc = refs[9 + 2 * n_arr:12 + 2 * n_arr]
        pair, j = pl.program_id(0), pl.program_id(1)
        exchange_start, exchange_finish = _exchange_chips_phases(part_refs, received_refs, *refs[12 + 2 * n_arr:])

        @pl.when((pair == 0) & (j == 0))
        def _():
            exchange_start()

        @pl.when(j == 0)
        def _():
            dq_acc[...] = jnp.zeros_like(dq_acc)

        for hh in range(2):
            kb, vb = k_ref[:, hh * d:(hh + 1) * d], v_ref[:, hh * d:(hh + 1) * d]
            dk_acc[...] = jnp.zeros_like(dk_acc)
            dv_acc[...] = jnp.zeros_like(dv_acc)

            def step(i, on_diagonal, hh=hh, kb=kb, vb=vb):
                rows = pl.ds(pl.multiple_of(i * t, t), t)
                qb = q_ref[rows, hh * d:(hh + 1) * d]
                do_pair = do_ref[rows, :].astype(F32)
                do_h = do_pair if hh == 0 else pltpu.roll(do_pair, HALF, 1)
                dob = jnp.where(_low_half(do_h.shape), do_h, 0.0).astype(BF16)
                sc_t = lax.dot_general(kb, qb, NT_DIMS, preferred_element_type=F32)
                p_t = jnp.exp2(sc_t * EXP2_SCALE - lse_ref[hh, :, rows])
                if on_diagonal:
                    key = lax.broadcasted_iota(jnp.int32, (t, t), 0)
                    qry = lax.broadcasted_iota(jnp.int32, (t, t), 1)
                    p_t = jnp.where(qry >= key, p_t, 0.0)
                dp_t = lax.dot_general(vb, dob, NT_DIMS, preferred_element_type=F32)
                ds_t = (p_t * (dp_t - delta_ref[hh, :, rows])).astype(BF16)
                dv_acc[...] += jnp.dot(p_t.astype(BF16), dob, preferred_element_type=F32)
                dk_acc[...] += jnp.dot(ds_t, qb, preferred_element_type=F32)
                dq_acc[hh, rows, :] += lax.dot_general(ds_t, kb, TN_DIMS, preferred_element_type=F32)

            def above(i, carry, step=step):
                step(i, False)
                return carry

            step(j, True)
            lax.fori_loop(j + 1, nb, above, 0)
            dk_ref[:, hh * d:(hh + 1) * d] = (dk_acc[...] * MLA_SCALE).astype(dk_ref.dtype)
            dv_ref[:, hh * d:(hh + 1) * d] = dv_acc[...].astype(dv_ref.dtype)

        @pl.when(j == nb - 1)
        def _():
            for hh in range(2):
                dq_ref[:, hh * d:(hh + 1) * d] = (dq_acc[hh] * MLA_SCALE).astype(dq_ref.dtype)

        @pl.when((pair == N_PAIR - 1) & (j == nb - 1))
        def _():
            exchange_finish()

    full_spec = pl.BlockSpec((s, 2 * d), lambda p, j: (0, p))
    tile_spec = pl.BlockSpec((t, 2 * d), lambda p, j: (j, p))
    row_spec = pl.BlockSpec((2, 1, s), lambda p, j: (p, 0, 0))
    return pl.pallas_call(
        body, name="mla_bwd", grid=(N_PAIR, nb),
        in_specs=[full_spec, tile_spec, tile_spec, row_spec, row_spec, pl.BlockSpec((s, d), lambda p, j: (0, p))]
                 + [HBM_SPEC] * n_arr,
        out_specs=[full_spec, tile_spec, tile_spec] + [HBM_SPEC] * n_arr,
        out_shape=[jax.ShapeDtypeStruct(q.shape, q.dtype)] * 3 + [jax.ShapeDtypeStruct(p.shape, p.dtype) for p in parts],
        scratch_shapes=[pltpu.VMEM((2, s, d), F32), pltpu.VMEM((t, d), F32), pltpu.VMEM((t, d), F32)]
                       + _exchange_chips_sems(n_arr),
        compiler_params=_params("arbitrary", "arbitrary"),
    )(q, k, v, lse_row, delta_row, do, *parts)


def _reduce_scatter_head(cts, tag):
    received = _exchange_sibling(list(cts), tag + "_exchange_sibling")
    my_c = lax.axis_index("c").astype(jnp.int32).reshape(1)
    return [_pair_add(m, r, my_c, "%s_pair_add_%d" % (tag, i)) for i, (m, r) in enumerate(zip(cts, received))]


def _reduce_scatter_tail(chip_parts, tag):
    return tuple(_sum_blocks(r, "%s_sum_%d" % (tag, i)) for i, r in enumerate(chip_parts))


@jax.custom_vjp
def flash_nat(q, k, v, shards):
    out = _flash_nat_fwd_call(q, k, v, [s.astype(BF16) for s in shards])
    return out[0], tuple(out[2:])


def _flash_nat_fwd(q, k, v, shards):
    out = _flash_nat_fwd_call(q, k, v, [s.astype(BF16) for s in shards])
    return (out[0], tuple(out[2:])), (q, k, v, out[0], out[1])


def _flash_nat_bwd(res, cts):
    q, k, v, o, lse = res
    do, d_gathered = cts
    delta = _flash_nat_delta_call(o, do)[:, :B_HEADS].T.reshape(B_HEADS, 1, q.shape[0])
    out = _flash_nat_bwd_call(q, k, v, lse, delta, do, _reduce_scatter_head(d_gathered, "mlp_grads"))
    return out[0], out[1], out[2], _reduce_scatter_tail(out[3:], "mlp_grads")


flash_nat.defvjp(_flash_nat_fwd, _flash_nat_bwd)


HBM_SPEC = pl.BlockSpec(memory_space=pltpu.HBM)


def _allgather(shards, name):
    n_arr = len(shards)

    def body(*refs):
        start, forward, finish = _allgather_phases(refs[:n_arr], refs[n_arr:2 * n_arr], *refs[2 * n_arr:])
        start()
        forward()
        finish()

    return pl.pallas_call(
        body, name=name, out_shape=_allgather_out_shapes(shards),
        in_specs=[HBM_SPEC] * n_arr, out_specs=[HBM_SPEC] * n_arr,
        scratch_shapes=_allgather_sems(n_arr),
    )(*shards)


def _allgather_out_shapes(shards):
    return [jax.ShapeDtypeStruct((N_DEV,) + s.shape, s.dtype) for s in shards]


def _allgather_sems(n_arr):
    return [pltpu.SemaphoreType.DMA((7, n_arr)), pltpu.SemaphoreType.DMA((7, n_arr)), pltpu.SemaphoreType.DMA((n_arr,))]


def _allgather_phases(x_refs, out_refs, send_sems, recv_sems, local_sems):
    arrays = range(len(x_refs))
    x, y, c = lax.axis_index("x"), lax.axis_index("y"), lax.axis_index("c")
    me, sibling = (x, y, c), (x, y, 1 - c)
    chips = [(1 - x, y), (x, 1 - y), (1 - x, 1 - y)]

    def rows(a, px, py, pc):
        return out_refs[a].at[4 * px + 2 * py + pc]

    def copy(a, k, block, to, src=None):
        return pltpu.make_async_remote_copy(
            src_ref=rows(a, *block) if src is None else src, dst_ref=rows(a, *block),
            send_sem=send_sems.at[k, a], recv_sem=recv_sems.at[k, a], device_id=to, device_id_type=MESH_ID)

    def mine():
        return [pltpu.make_async_copy(x_refs[a], rows(a, *me), local_sems.at[a]) for a in arrays]

    def first():
        return [cp for a in arrays for cp in
                [copy(a, 0, me, sibling, src=x_refs[a])]
                + [copy(a, 1 + j, me, (*chip, c), src=x_refs[a]) for j, chip in enumerate(chips)]]

    def passed():
        return [copy(a, 4 + j, (*chip, c), sibling) for j, chip in enumerate(chips) for a in arrays]

    def start():
        for cp in mine() + first():
            cp.start()

    def forward():
        for j, chip in enumerate(chips):
            for a in arrays:
                copy(a, 1 + j, (*chip, c), me).wait_recv()
                copy(a, 4 + j, (*chip, c), sibling).start()

    def finish():
        for a in arrays:
            copy(a, 0, sibling, me).wait_recv()
        for j, chip in enumerate(chips):
            for a in arrays:
                copy(a, 4 + j, (*chip, 1 - c), me).wait_recv()
        for cp in first() + passed():
            cp.wait_send()
        for cp in mine():
            cp.wait()

    return start, forward, finish


N_CHIP = 4


def _exchange_sibling(parts, name):
    n_arr = len(parts)

    def body(*refs):
        in_refs, recv_refs = refs[:n_arr], refs[n_arr:2 * n_arr]
        send_sems, recv_sems = refs[2 * n_arr:]
        x, y, c = lax.axis_index("x"), lax.axis_index("y"), lax.axis_index("c")
        copies = []
        for a in range(n_arr):
            for q in range(N_CHIP):
                copies.append(pltpu.make_async_remote_copy(
                    src_ref=in_refs[a].at[2 * q + 1 - c], dst_ref=recv_refs[a].at[q],
                    send_sem=send_sems.at[q, a], recv_sem=recv_sems.at[q, a],
                    device_id=(x, y, 1 - c), device_id_type=MESH_ID))
        for cp in copies:
            cp.start()
        for cp in copies:
            cp.wait()

    return pl.pallas_call(
        body, name=name, out_shape=[jax.ShapeDtypeStruct((N_CHIP,) + p.shape[1:], p.dtype) for p in parts],
        in_specs=[HBM_SPEC] * n_arr, out_specs=[HBM_SPEC] * n_arr,
        scratch_shapes=[pltpu.SemaphoreType.DMA((N_CHIP, n_arr)), pltpu.SemaphoreType.DMA((N_CHIP, n_arr))],
    )(*parts)


def _exchange_chips_sems(n_arr):
    return [pltpu.SemaphoreType.DMA((N_CHIP - 1, n_arr)), pltpu.SemaphoreType.DMA((N_CHIP - 1, n_arr)),
            pltpu.SemaphoreType.DMA((n_arr,))]


def _exchange_chips_phases(in_refs, out_refs, send_sems, recv_sems, local_sems):
    n_arr = len(in_refs)
    x, y, c = lax.axis_index("x"), lax.axis_index("y"), lax.axis_index("c")
    me = 2 * x + y

    def copies():
        out = [pltpu.make_async_copy(in_refs[a].at[me], out_refs[a].at[me], local_sems.at[a]) for a in range(n_arr)]
        for k in range(1, N_CHIP):
            px = 1 - x if k & 2 else x
            py = 1 - y if k & 1 else y
            for a in range(n_arr):
                out.append(pltpu.make_async_remote_copy(
                    src_ref=in_refs[a].at[2 * px + py], dst_ref=out_refs[a].at[me],
                    send_sem=send_sems.at[k - 1, a], recv_sem=recv_sems.at[k - 1, a],
                    device_id=(px, py, c), device_id_type=MESH_ID))
        return out

    def start():
        for cp in copies():
            cp.start()

    def finish():
        for cp in copies():
            cp.wait()

    return start, finish


def _row_tile(r, ccols, blocks):
    cap = max(16, (2 * 1024 * 1024) // (4 * ccols * blocks))
    return _pick(r, cap, 16)


def _pair_add(mine, theirs, my_c, name):
    _, r, ccols = mine.shape
    tr = _row_tile(r, ccols, 1)

    def body(c_ref, a_ref, b_ref, o_ref):
        o_ref[...] = (a_ref[...].astype(F32) + b_ref[...].astype(F32)).astype(o_ref.dtype)

    spec = pl.BlockSpec((None, tr, ccols), lambda q, i, c_ref: (q, i, 0))
    return pl.pallas_call(
        body, name=name,
        grid_spec=pltpu.PrefetchScalarGridSpec(
            num_scalar_prefetch=1, grid=(N_CHIP, r // tr),
            in_specs=[pl.BlockSpec((None, tr, ccols), lambda q, i, c_ref: (2 * q + c_ref[0], i, 0)), spec],
            out_specs=spec),
        out_shape=jax.ShapeDtypeStruct(theirs.shape, theirs.dtype),
        compiler_params=_params("parallel", "parallel"),
    )(my_c, mine, theirs)


def _sum_blocks(parts, name):
    nb, r, ccols = parts.shape
    tr = _row_tile(r, ccols, nb)

    def body(p_ref, o_ref):
        acc = p_ref[0].astype(F32)
        for i in range(1, nb):
            acc = acc + p_ref[i].astype(F32)
        o_ref[...] = acc

    return pl.pallas_call(
        body, name=name, grid=(r // tr,),
        in_specs=[pl.BlockSpec((nb, tr, ccols), lambda i: (0, i, 0))],
        out_specs=pl.BlockSpec((tr, ccols), lambda i: (i, 0)),
        out_shape=jax.ShapeDtypeStruct((r, ccols), F32),
        compiler_params=_params("parallel"),
    )(parts)


@jax.custom_vjp
def replicated(vec):
    return vec


def _replicated_fwd(vec):
    return vec, None


def _replicated_bwd(_, ct):
    return (_sum_blocks(_allgather([ct], "small_grad_allgather")[0], "small_grad_sum"),)


replicated.defvjp(_replicated_fwd, _replicated_bwd)


def _adamw(w, g, m, v, name):
    rows, cols = w.shape
    tr = _pick(rows, 256, 8) if rows % 8 == 0 else rows

    def body(w_ref, g_ref, m_ref, v_ref, d_ref, nm_ref, nv_ref):
        g_ = g_ref[...]
        m_ = ADAM_B1 * m_ref[...] + (1.0 - ADAM_B1) * g_
        v_ = ADAM_B2 * v_ref[...] + (1.0 - ADAM_B2) * jnp.square(g_)
        m_hat = m_ / (1.0 - ADAM_B1 ** ADAM_STEP)
        v_hat = v_ / (1.0 - ADAM_B2 ** ADAM_STEP)
        d_ref[...] = -ADAM_LR * (m_hat / (jnp.sqrt(v_hat) + ADAM_EPS) + ADAM_WD * w_ref[...])
        nm_ref[...] = m_
        nv_ref[...] = v_

    spec = pl.BlockSpec((tr, cols), lambda i: (i, 0))
    return pl.pallas_call(
        body, name=name, grid=(rows // tr,), in_specs=[spec] * 4, out_specs=[spec] * 3,
        out_shape=[jax.ShapeDtypeStruct(w.shape, F32)] * 3, compiler_params=_params("parallel"),
    )(w, g, m, v)


COL_SHARDED = ("w_in", "w_uq", "w_ukv", "w_branch_a", "w_branch_b", "w_up", "w_ple")
EARLY = ("w_in",)
MID = ("w_uq", "w_ukv", "w_branch_a", "w_branch_b", "w_out")
LATE = ("w_up", "w_down", "w_ple_gate", "w_ple")
SMALL = ("attn_pre_norm", "attn_post_norm", "b_gate", "q_a_norm", "kv_a_norm", "mlp_pre_norm", "mlp_post_norm",
         "conv_b", "ple_norm", "sinks")
SMALL_COLS = 128


def _pack_rows(arrays, cols, row_mult):
    flat = jnp.concatenate([a.reshape(-1) for a in arrays])
    pad = (-flat.shape[0]) % (cols * row_mult)
    return jnp.pad(flat, (0, pad)).reshape(-1, cols)


def _unpack_small(vec, shapes):
    flat = vec.reshape(-1)
    out, off = {}, 0
    for name in SMALL:
        n = shapes[name]
        out[name] = flat[off:off + n].reshape(1, n)
        off += n + (-n) % SMALL_COLS
    return out


def _pad_lanes(t, width):
    return jnp.pad(t, [(0, 0)] * (t.ndim - 1) + [(0, width - t.shape[-1])])


def _pad_rows(t, rows):
    return jnp.pad(t, [(0, 0)] * (t.ndim - 2) + [(0, rows - t.shape[-2]), (0, 0)])


FRONT_SIZES = (512, 128, 128, 256, 128)
FRONT_BOUNDS = (0, 512, 640, 768, 1024, 1152, 1280)
PE_LANE = NOPE_DIM


def _arrange_w_in_t(wt):
    k = wt.shape[1]
    n_front = sum(FRONT_SIZES)
    front, kr, gates = wt[:n_front], wt[n_front:n_front + ROPE_DIM], wt[n_front + ROPE_DIM:]
    kr_slab = jnp.concatenate([jnp.zeros((PE_LANE, k), wt.dtype), kr,
                               jnp.zeros((HEAD_PAD - PE_LANE - ROPE_DIM, k), wt.dtype)], axis=0)
    return jnp.concatenate([front, kr_slab], axis=0), gates


def _arrange_w_uq_t(wt):
    k = wt.shape[1]
    return _pad_rows(wt.reshape(B_HEADS, NOPE_DIM + ROPE_DIM, k), HEAD_PAD).reshape(B_HEADS * HEAD_PAD, k)


def _arrange_w_ukv_t(wt):
    k = wt.shape[1]
    w = wt.reshape(B_HEADS, 2, NOPE_DIM, k)
    slabs = [_pad_rows(w[:, part], HEAD_PAD).reshape(B_HEADS * HEAD_PAD, k) for part in range(2)]
    return jnp.concatenate(slabs, axis=0)


def _rope_tables(positions, s):
    pos = positions.reshape(s, 1).astype(F32)

    def angles(dim):
        return pos * ROPE_THETA ** (-(jnp.arange(0, dim, 2, dtype=F32) / dim))

    cos_a, sin_a = jnp.cos(angles(A_HEAD_DIM)), jnp.sin(angles(A_HEAD_DIM))
    zero_a = jnp.zeros_like(sin_a)
    tables_a = [jnp.tile(jnp.concatenate(pair, axis=1), (1, LANES // A_HEAD_DIM))
                for pair in ((cos_a, cos_a), (-sin_a, zero_a), (zero_a, sin_a))]
    cos_b, sin_b = jnp.cos(angles(ROPE_DIM)), jnp.sin(angles(ROPE_DIM))
    zero_b = jnp.zeros_like(sin_b)

    def slab(first, second, fill):
        return jnp.concatenate([jnp.full((s, PE_LANE), fill, F32), first, second,
                                jnp.full((s, HEAD_PAD - PE_LANE - ROPE_DIM), fill, F32)], axis=1)

    tables_b = [slab(cos_b, cos_b, 1.0), slab(-sin_b, zero_b, 0.0), slab(zero_b, sin_b, 0.0)]
    return tables_a + tables_b


def _local_loss(wts, x, p, tables, target):
    s = x.shape[0]
    small_shapes = {n: wts[n].shape[-1] for n in SMALL}
    small_vec = _pack_rows([_pad_lanes(wts[n].reshape(1, -1), small_shapes[n] + (-small_shapes[n]) % SMALL_COLS)
                            for n in SMALL], SMALL_COLS, 8)
    sm = _unpack_small(replicated(small_vec), small_shapes)
    def shard(n):
        return wts[n].T if n in COL_SHARDED else wts[n]

    h1_front, h1_gates, x_res, gathered = prenorm_gather(
        x, sm["attn_pre_norm"], tuple([shard(n) for n in EARLY] + [_pack_rows([wts["conv_w"]], SMALL_COLS, 8)]),
        (BF16,) * len(EARLY) + (F32,))
    big = {n: g.reshape(-1, g.shape[2]) for n, g in zip(EARLY, gathered)}
    ch = wts["conv_w"].shape[1]
    conv_w = gathered[-1].reshape(N_DEV, -1)[:, :CONV_W * ch].reshape(N_DEV, CONV_W, ch)
    conv_w = conv_w.transpose(1, 0, 2).reshape(CONV_W, N_DEV * ch)

    w_front_t, w_gates_t = _arrange_w_in_t(big["w_in"])
    tables_a, tables_b = tables[:3], tables[3:]

    qa, ka, va, cqn, ckvn, kpe = proj_stage(
        "prep", _f_prep, [(h1_front, w_front_t, "nt", "w_front", True, F32)], params=[sm["q_a_norm"], sm["kv_a_norm"]],
        consts=tables, splits=[FRONT_BOUNDS], ts=512, out_dtypes=[BF16, BF16, BF16, BF16, BF16, F32])
    ya, mid = swa_nat(qa, ka, va, sm["sinks"].reshape(-1), tuple(shard(n) for n in MID))
    big.update({n: g.reshape(-1, g.shape[2]) for n, g in zip(MID, mid)})

    (q2,) = proj_stage("qrope", _f_qrope, [(cqn, _arrange_w_uq_t(big["w_uq"]), "nt", "w_uq", True, BF16)],
                       consts=tables_b, ts=512, out_dtypes=[BF16])
    k2, v2 = proj_stage("kv", _f_kv, [(ckvn, _arrange_w_ukv_t(big["w_ukv"]), "nt", "w_ukv", True, BF16)],
                        extra=[kpe], splits=[(0, B_HEADS * HEAD_PAD, 2 * B_HEADS * HEAD_PAD), None], ts=512,
                        out_dtypes=[BF16, BF16])
    yb, late = flash_nat(q2, k2, v2, tuple(shard(n) for n in LATE))
    big.update({n: g.reshape(-1, g.shape[2]) for n, g in zip(LATE, late)})

    (mixed,) = proj_stage(
        "gate", _f_gate, [(h1_gates, w_gates_t, "nt", "w_gates", True, F32),
                          (ya, big["w_branch_a"], "nt", "w_branch_a", True, BF16),
                          (yb, big["w_branch_b"], "nt", "w_branch_b", True, BF16)],
        params=[sm["b_gate"][:, :D_MODEL], sm["b_gate"][:, D_MODEL:]],
        splits=[(0, D_MODEL, 2 * D_MODEL), None, None], out_dtypes=[BF16])
    x1, h2 = proj_stage("post_attn", _f_post, [(mixed, big["w_out"], "nn", "w_out", True, F32)], extra=[x_res],
                        params=[sm["attn_post_norm"], sm["mlp_pre_norm"]], ts=512, out_dtypes=[F32, BF16])

    act = mlp_up(h2, big["w_up"], conv_w, sm["conv_b"])
    x2, h3 = proj_stage("post_mlp", _f_post, [(act, big["w_down"], "nn", "w_down", True, F32)], extra=[x1],
                        params=[sm["mlp_post_norm"], sm["ple_norm"]], ts=512, out_dtypes=[F32, BF16])

    (rowloss,) = proj_stage("loss", _f_out, [(h3, big["w_ple_gate"], "nn", "w_ple_gate", True, F32),
                                             (p, big["w_ple"], "nt", "w_ple", False, BF16)], extra=[x2],
                            consts=[target], ts=512)
    return jnp.sum(rowloss)


WEIGHTS = ["attn_pre_norm", "attn_post_norm", "w_in", "b_gate", "sinks", "q_a_norm", "w_uq", "kv_a_norm", "w_ukv",
           "w_branch_a", "w_branch_b", "w_out", "mlp_pre_norm", "mlp_post_norm", "w_up", "conv_w", "conv_b",
           "w_down", "ple_norm", "w_ple_gate", "w_ple"]


def kernel(x, p, positions, attn_pre_norm, attn_post_norm, w_in, b_gate, sinks, q_a_norm, w_uq, kv_a_norm, w_ukv, w_branch_a, w_branch_b, w_out, mlp_pre_norm, mlp_post_norm, w_up, conv_w, conv_b, w_down, ple_norm, w_ple_gate, w_ple, loss_target, m_attn_pre_norm, m_attn_post_norm, m_w_in, m_b_gate, m_sinks, m_q_a_norm, m_w_uq, m_kv_a_norm, m_w_ukv, m_w_branch_a, m_w_branch_b, m_w_out, m_mlp_pre_norm, m_mlp_post_norm, m_w_up, m_conv_w, m_conv_b, m_w_down, m_ple_norm, m_w_ple_gate, m_w_ple, v_attn_pre_norm, v_attn_post_norm, v_w_in, v_b_gate, v_sinks, v_q_a_norm, v_w_uq, v_kv_a_norm, v_w_ukv, v_w_branch_a, v_w_branch_b, v_w_out, v_mlp_pre_norm, v_mlp_post_norm, v_w_up, v_conv_w, v_conv_b, v_w_down, v_ple_norm, v_w_ple_gate, v_w_ple):
    given = dict(locals())
    s = x.shape[1]
    wts = {n: given[n][0] if given[n].ndim == 3 else given[n] for n in WEIGHTS}
    tables = _rope_tables(positions, s)
    local_loss, (grads, grad_x) = jax.value_and_grad(_local_loss, argnums=(0, 1))(
        wts, x[0], p[0, 0], tables, loss_target[0])
    loss = lax.psum(local_loss, AXES)

    outs = {"grad": [], "delta": [], "m": [], "v": []}
    for n in WEIGHTS:
        shape = given[n].shape
        w2 = wts[n].reshape(-1, shape[-1])
        g2 = grads[n].reshape(w2.shape)
        delta, new_m, new_v = _adamw(w2, g2, given["m_" + n].reshape(w2.shape), given["v_" + n].reshape(w2.shape),
                                     "adamw_" + n)
        outs["grad"].append(g2.reshape(shape))
        outs["delta"].append(delta.reshape(shape))
        outs["m"].append(new_m.reshape(shape))
        outs["v"].append(new_v.reshape(shape))
    return (loss, grad_x[None], *outs["grad"], *outs["delta"], *outs["m"], *outs["v"])
```

```python
import functools

import numpy as np
import jax
import jax.numpy as jnp
from jax import lax
from jax.experimental import pallas as pl
from jax.experimental.pallas import tpu as pltpu

F32 = jnp.float32
BF16 = jnp.bfloat16
MESH_ID = pl.DeviceIdType.MESH
AXES = ("x", "y", "c")
N_DEV = 8

D_MODEL = 1024
RMS_EPS = 1e-6
ROPE_THETA = 10000.0
SWA_BLOCK = 128
A_HEADS, A_KV_HEADS, A_HEAD_DIM = 8, 2, 64
A_GROUP = A_HEADS // A_KV_HEADS
B_HEADS, Q_LORA, KV_LORA, NOPE_DIM, ROPE_DIM, V_DIM = 8, 256, 128, 64, 32, 64
D_FF = 2816
CONV_W = 3
HEAD_PAD = 128

ADAM_LR, ADAM_B1, ADAM_B2, ADAM_EPS, ADAM_WD, ADAM_STEP = 0.001, 0.9, 0.999, 1e-08, 0.01, 10

VMEM_LIMIT = 48 * 1024 * 1024
MM_TM, MM_TN, MM_TK_TOKENS = 1024, 1408, 1024
MM_VMEM_BUDGET = 36 * 1024 * 1024
FLASH_T = 1024
CONV_TS = 256
CONV_CHUNK = 256


def _params(*sem):
    return pltpu.CompilerParams(dimension_semantics=sem, vmem_limit_bytes=VMEM_LIMIT)


def _pick(dim, cap, mult):
    best = None
    for t in range(mult, min(dim, cap) + 1, mult):
        if dim % t == 0:
            best = t
    return dim if best is None else best


def _divisors(dim, mult):
    return [t for t in range(mult, dim + 1, mult) if dim % t == 0] or [dim]


def _matmul_tiles(m, n, kdim, form, sizes):
    sa, sb, so = sizes
    tk = _pick(kdim, MM_TK_TOKENS, 128) if form == "tn" else kdim
    cap_m = MM_TN if form == "tn" else MM_TM
    best = None
    for tm in _divisors(m, 128):
        for tn in _divisors(n, 128):
            need = 2 * (tm * tk * sa + tk * tn * sb + tm * tn * so) + (tm * tn * 4 if tk != kdim else 0)
            if tm > cap_m or tn > MM_TN or need > MM_VMEM_BUDGET:
                continue
            if best is None or (tm * tn, tm) > (best[0] * best[1], best[0]):
                best = (tm, tn)
    return best[0], best[1], tk


def _matmul(a, b, form, *, out_dtype=F32, name):
    if form == "tn":
        (kdim, m), n = a.shape, b.shape[1]
    else:
        (m, kdim), n = a.shape, (b.shape[1] if form == "nn" else b.shape[0])
    sizes = (a.dtype.itemsize, b.dtype.itemsize, jnp.dtype(out_dtype).itemsize)
    tm, tn, tk = _matmul_tiles(m, n, kdim, form, sizes)
    nk = kdim // tk
    rows_outer = nk > 1 or (m // tm) * b.size * sizes[1] <= (n // tn) * a.size * sizes[0]

    def ij(fn):
        return (lambda i, j, k: fn(i, j, k)) if rows_outer else (lambda j, i, k: fn(i, j, k))

    a_spec = (pl.BlockSpec((tk, tm), ij(lambda i, j, k: (k, i))) if form == "tn"
              else pl.BlockSpec((tm, tk), ij(lambda i, j, k: (i, k))))
    b_spec = (pl.BlockSpec((tn, tk), ij(lambda i, j, k: (j, k))) if form == "nt"
              else pl.BlockSpec((tk, tn), ij(lambda i, j, k: (k, j))))
    dims = (((0 if form == "tn" else 1,), (1 if form == "nt" else 0,)), ((), ()))

    def product(a_ref, b_ref):
        return lax.dot_general(a_ref[...].astype(BF16), b_ref[...].astype(BF16), dims, preferred_element_type=F32)

    if nk == 1:
        def body(a_ref, b_ref, o_ref):
            o_ref[...] = product(a_ref, b_ref).astype(o_ref.dtype)

        scratch = []
    else:
        def body(a_ref, b_ref, o_ref, acc_ref):
            k = pl.program_id(2)

            @pl.when(k == 0)
            def _():
                acc_ref[...] = jnp.zeros_like(acc_ref)

            acc_ref[...] += product(a_ref, b_ref)

            @pl.when(k == nk - 1)
            def _():
                o_ref[...] = acc_ref[...].astype(o_ref.dtype)

        scratch = [pltpu.VMEM((tm, tn), F32)]

    return pl.pallas_call(
        body, name=name, grid=(m // tm, n // tn, nk) if rows_outer else (n // tn, m // tm, nk),
        in_specs=[a_spec, b_spec],
        out_specs=pl.BlockSpec((tm, tn), ij(lambda i, j, k: (i, j))),
        out_shape=jax.ShapeDtypeStruct((m, n), out_dtype),
        scratch_shapes=scratch,
        compiler_params=_params("parallel", "parallel", "arbitrary"),
    )(a, b)


def _pairs(bounds):
    return list(zip(bounds[:-1], bounds[1:]))


def _split(v, bounds):
    return [v[:, a:b] for a, b in _pairs(bounds)]


def _stage_build(name, f, tiled, params, consts, splits, ts, out_dtypes, ct_dtypes=None):
    n_t, n_p, n_c = len(tiled), len(params), len(consts)
    ct_dtypes = [t.dtype for t in tiled] if ct_dtypes is None else ct_dtypes
    s = tiled[0].shape[0]
    ts = min(ts, s)
    grid = (s // ts,)
    if splits is None:
        splits = [None] * n_t
    in_bounds = [(0, t.shape[1]) if b is None else tuple(b) for t, b in zip(tiled, splits)]

    def tile_aval(arr):
        return jax.ShapeDtypeStruct((ts, arr.shape[1]), arr.dtype)

    slab_avals = [[jax.ShapeDtypeStruct((ts, e - a), F32) for a, e in _pairs(b)]
                  for t, b in zip(tiled, in_bounds)]
    out_avals = jax.eval_shape(f, slab_avals, list(params), [tile_aval(c) for c in consts])
    out_bounds = [tuple(np.cumsum([0] + [o.shape[1] for o in slabs]).tolist()) for slabs in out_avals]
    out_dtypes = [F32] * len(out_bounds) if out_dtypes is None else out_dtypes
    out_shapes = [jax.ShapeDtypeStruct((s, b[-1]), d) for b, d in zip(out_bounds, out_dtypes)]

    def row_spec(width):
        return pl.BlockSpec((ts, width), lambda i: (i, 0))

    def par_spec(arr):
        return pl.BlockSpec(arr.shape, lambda i: (0, 0))

    in_specs = ([row_spec(t.shape[1]) for t in tiled] + [par_spec(p) for p in params]
                + [row_spec(c.shape[1]) for c in consts])

    def load(refs):
        t = [_split(r[...].astype(F32), b) for r, b in zip(refs[:n_t], in_bounds)]
        p = [r[...] for r in refs[n_t:n_t + n_p]]
        c = [r[...] for r in refs[n_t + n_p:n_t + n_p + n_c]]
        return t, p, c

    def store(refs, values, bounds):
        for ref, slabs, b in zip(refs, values, bounds):
            for v, (a, e) in zip(slabs, _pairs(b)):
                ref[:, a:e] = v.astype(ref.dtype)

    def run_fwd(tiled, params, consts):
        def body(*refs):
            t, p, c = load(refs)
            store(refs[n_t + n_p + n_c:], f(t, p, c), out_bounds)

        return pl.pallas_call(
            body, name=name + "_fwd", grid=grid, in_specs=in_specs,
            out_specs=[row_spec(b[-1]) for b in out_bounds], out_shape=out_shapes,
            compiler_params=_params("parallel"),
        )(*tiled, *params, *consts)

    def run_bwd(tiled, params, consts, cts):
        n_in = n_t + n_p + n_c
        n_o = len(out_bounds)

        def body(*refs):
            t, p, c = load(refs)
            g = [_split(r[...].astype(F32), b) for r, b in zip(refs[n_in:n_in + n_o], out_bounds)]
            _, pull = jax.vjp(lambda t_, p_: f(t_, p_, c), t, p)
            dt, dp = pull(g)
            store(refs[n_in + n_o:n_in + n_o + n_t], dt, in_bounds)
            first = pl.program_id(0) == 0
            for ref, d in zip(refs[n_in + n_o + n_t:], dp):
                @pl.when(first)
                def _(ref=ref):
                    ref[...] = jnp.zeros_like(ref)

                ref[...] += d

        res = pl.pallas_call(
            body, name=name + "_bwd", grid=grid,
            in_specs=in_specs + [row_spec(b[-1]) for b in out_bounds],
            out_specs=[row_spec(t.shape[1]) for t in tiled] + [par_spec(p) for p in params],
            out_shape=[jax.ShapeDtypeStruct(t.shape, d) for t, d in zip(tiled, ct_dtypes)]
                      + [jax.ShapeDtypeStruct(p.shape, F32) for p in params],
            compiler_params=_params("arbitrary"),
        )(*tiled, *params, *consts, *cts)
        return tuple(res[:n_t]), tuple(res[n_t:])

    return run_fwd, run_bwd


def proj_stage(name, f, projections, extra=(), params=(), consts=(), splits=None, ts=256, out_dtypes=None):
    n_z = len(projections)
    forms = [pr[2] for pr in projections]
    names = [pr[3] for pr in projections]
    need_da = [pr[4] for pr in projections]
    store = [pr[5] for pr in projections]
    extra, params, consts = tuple(extra), tuple(params), tuple(consts)

    def matmuls(a_list, w_list):
        return tuple(_matmul(a, w, form, out_dtype=dt, name=n + "_fwd")
                     for a, w, form, n, dt in zip(a_list, w_list, forms, names, store))

    def build(zs, ct=False):
        ct_dtypes = [BF16] * n_z + [e.dtype for e in extra] if ct else None
        return _stage_build(name, f, tuple(zs) + extra, params, consts, splits, ts, out_dtypes, ct_dtypes)

    @jax.custom_vjp
    def op(a_list, w_list, extra, params, consts):
        zs = matmuls(a_list, w_list)
        return tuple(build(zs)[0](zs + extra, params, consts))

    def op_fwd(a_list, w_list, extra, params, consts):
        zs = matmuls(a_list, w_list)
        return tuple(build(zs)[0](zs + extra, params, consts)), (a_list, w_list, zs, extra, params, consts)

    def op_bwd(res, cts):
        a_list, w_list, zs, extra, params, consts = res
        dt, dp = build(zs, ct=True)[1](zs + extra, params, consts, cts)
        da_list, dw_list = [], []
        for a, w, dz, form, n, want in zip(a_list, w_list, dt[:n_z], forms, names, need_da):
            if form == "nn":
                da = _matmul(dz, w, "nt", out_dtype=a.dtype, name=n + "_da") if want else jnp.zeros_like(a)
                dw = _matmul(a, dz, "tn", out_dtype=w.dtype, name=n + "_dw")
            else:
                da = _matmul(dz, w, "nn", out_dtype=a.dtype, name=n + "_da") if want else jnp.zeros_like(a)
                dw = _matmul(dz, a, "tn", out_dtype=w.dtype, name=n + "_dw")
            da_list.append(da)
            dw_list.append(dw)
        return tuple(da_list), tuple(dw_list), tuple(dt[n_z:]), dp, tuple(jnp.zeros_like(c) for c in consts)

    op.defvjp(op_fwd, op_bwd)
    return op(tuple(pr[0] for pr in projections), tuple(pr[1] for pr in projections), extra, params, consts)


def _rms(t, g):
    return t * lax.rsqrt(jnp.mean(t * t, axis=-1, keepdims=True) + RMS_EPS) * g


@functools.partial(jax.custom_vjp, nondiff_argnums=(1,))
def _lane_roll(t, shift):
    return pltpu.roll(t, shift % t.shape[-1], t.ndim - 1)


def _lane_roll_fwd(t, shift):
    return _lane_roll(t, shift), None


def _lane_roll_bwd(shift, _, ct):
    return (pltpu.roll(ct, (-shift) % ct.shape[-1], ct.ndim - 1),)


_lane_roll.defvjp(_lane_roll_fwd, _lane_roll_bwd)


def _rope_lanes(t, tables, half):
    reps = t.shape[1] // tables[0].shape[1]
    c, s_lo, s_hi = [jnp.concatenate([tb] * reps, axis=1) if reps > 1 else tb for tb in tables]
    return t * c + _lane_roll(t, -half) * s_lo + _lane_roll(t, half) * s_hi


PRENORM_TS = 256


def _prenorm_fwd_call(x, g, shards):
    s, width = x.shape
    ts = min(PRENORM_TS, s)
    nt = s // ts
    n_arr = len(shards)

    def body(*refs):
        x_ref, g_ref = refs[:2]
        o_ref = refs[2 + n_arr]
        i = pl.program_id(0)
        ag_start, ag_forward, ag_finish = _allgather_phases(refs[2:2 + n_arr], refs[3 + n_arr:3 + 2 * n_arr],
                                                            *refs[3 + 2 * n_arr:])

        @pl.when(i == 0)
        def _():
            ag_start()

        @pl.when(i == nt // 2)
        def _():
            ag_forward()

        o_ref[...] = _rms(x_ref[...], g_ref[...]).astype(o_ref.dtype)

        @pl.when(i == nt - 1)
        def _():
            ag_finish()

    return pl.pallas_call(
        body, name="prenorm_fwd", grid=(nt,),
        in_specs=[pl.BlockSpec((ts, width), lambda i: (i, 0)), pl.BlockSpec(g.shape, lambda i: (0, 0))]
                 + [HBM_SPEC] * n_arr,
        out_specs=[pl.BlockSpec((ts, width), lambda i: (i, 0))] + [HBM_SPEC] * n_arr,
        out_shape=[jax.ShapeDtypeStruct(x.shape, BF16)] + _allgather_out_shapes(shards),
        scratch_shapes=_allgather_sems(n_arr),
        compiler_params=_params("arbitrary"),
    )(x, g, *shards)


def _prenorm_bwd_call(x, g, dh_a, dh_b, dx_res, parts):
    s, width = x.shape
    ts = min(PRENORM_TS, s)
    nt = s // ts
    n_arr = len(parts)

    def body(*refs):
        x_ref, g_ref, dha_ref, dhb_ref, dxr_ref = refs[:5]
        dx_ref, dg_ref = refs[5 + n_arr:7 + n_arr]
        i = pl.program_id(0)
        exchange_start, exchange_finish = _exchange_chips_phases(
            refs[5:5 + n_arr], refs[7 + n_arr:7 + 2 * n_arr], *refs[7 + 2 * n_arr:])

        @pl.when(i == 0)
        def _():
            exchange_start()
            dg_ref[...] = jnp.zeros_like(dg_ref)

        _, pull = jax.vjp(_rms, x_ref[...], g_ref[...])
        dx, dg = pull(dha_ref[...].astype(F32) + dhb_ref[...].astype(F32))
        dx_ref[...] = dx + dxr_ref[...]
        dg_ref[...] += dg

        @pl.when(i == nt - 1)
        def _():
            exchange_finish()

    row = pl.BlockSpec((ts, width), lambda i: (i, 0))
    par = pl.BlockSpec(g.shape, lambda i: (0, 0))
    return pl.pallas_call(
        body, name="prenorm_bwd", grid=(nt,),
        in_specs=[row, par, row, row, row] + [HBM_SPEC] * n_arr,
        out_specs=[row, par] + [HBM_SPEC] * n_arr,
        out_shape=[jax.ShapeDtypeStruct(x.shape, F32), jax.ShapeDtypeStruct(g.shape, F32)]
                  + [jax.ShapeDtypeStruct(p.shape, p.dtype) for p in parts],
        scratch_shapes=_exchange_chips_sems(n_arr),
        compiler_params=_params("arbitrary"),
    )(x, g, dh_a, dh_b, dx_res, *parts)


@functools.partial(jax.custom_vjp, nondiff_argnums=(3,))
def prenorm_gather(x, g, shards, wire_dtypes):
    out = _prenorm_fwd_call(x, g, [s.astype(d) for s, d in zip(shards, wire_dtypes)])
    return out[0], out[0], x, tuple(out[1:])


def _prenorm_gather_fwd(x, g, shards, wire_dtypes):
    return prenorm_gather(x, g, shards, wire_dtypes), (x, g)


def _prenorm_gather_bwd(wire_dtypes, res, cts):
    x, g = res
    dh_a, dh_b, dx_res, d_gathered = cts
    out = _prenorm_bwd_call(x, g, dh_a, dh_b, dx_res, _reduce_scatter_head(d_gathered, "grads"))
    return out[0], out[1], _reduce_scatter_tail(out[2:], "grads")


prenorm_gather.defvjp(_prenorm_gather_fwd, _prenorm_gather_bwd)


def _f_prep(t, p, c):
    qa, ka, va, cq, ckv, kr = t[0]
    return [[_rope_lanes(qa, c[0:3], A_HEAD_DIM // 2)], [_rope_lanes(ka, c[0:3], A_HEAD_DIM // 2)], [va],
            [_rms(cq, p[0])], [_rms(ckv, p[1])], [_rope_lanes(kr, c[3:6], ROPE_DIM // 2)]]


def _f_qrope(t, p, c):
    return [[_rope_lanes(t[0][0], c, ROPE_DIM // 2)]]


def _f_kv(t, p, c):
    (k_nope, v), (k_pe,) = t
    return [[k_nope + jnp.concatenate([k_pe] * B_HEADS, axis=1)], [v]]


def _f_gate(t, p, c):
    (ga, gb), (pa,), (pb,) = t
    ba, bb = p
    return [[jax.nn.sigmoid(ga + ba) * pa + jax.nn.sigmoid(gb + bb) * pb]]


def _f_post(t, p, c):
    (branch,), (residual,) = t
    x1 = residual + _rms(branch, p[0])
    return [[x1], [_rms(x1, p[1])]]


def _f_out(t, p, c):
    (gate,), (emb,), (x2,) = t
    y = x2 + jax.nn.sigmoid(gate) * emb
    err = y - c[0]
    return [[0.5 * jnp.mean(err * err, axis=-1, keepdims=True)]]


def _shift_down(cur, prev, has_prev):
    full = jnp.concatenate([prev * has_prev, cur], axis=0)
    return pltpu.roll(full, 1, 0)[HALO:], pltpu.roll(full, 2, 0)[HALO:]


GELU_C = float(np.sqrt(2.0 / np.pi))
GELU_A = 0.044715
HALO = 8


def _gelu_tanh(x):
    x2 = x * x
    th = jnp.tanh(x * (GELU_C + (GELU_C * GELU_A) * x2))
    half = 0.5 + 0.5 * th
    return x * half, half + x * (0.5 - 0.5 * (th * th)) * (GELU_C + (3.0 * GELU_C * GELU_A) * x2)


def _row_sum(t):
    return jnp.sum(t, axis=0, keepdims=True)


def _conv3(cur, prev, w_ref, b_ref, has_prev):
    u1, u2 = _shift_down(cur, prev, has_prev)
    return w_ref[2:3, :] * cur + w_ref[1:2, :] * u1 + w_ref[0:1, :] * u2 + b_ref[...], u1, u2


def _mlp_act_specs(s):
    ts = min(CONV_TS, s)
    hb = ts // HALO

    def half_specs(h):
        return [pl.BlockSpec((ts, D_FF), lambda i: (i, h)),
                pl.BlockSpec((HALO, D_FF), lambda i: (jnp.maximum(i * hb - 1, 0), h))]

    def par_specs(h):
        return [pl.BlockSpec((CONV_W, D_FF), lambda i: (0, h)), pl.BlockSpec((1, D_FF), lambda i: (0, h))]

    return ts, hb, half_specs, par_specs


def _mlp_act_fwd_call(up, conv_w, conv_b):
    s = up.shape[0]
    ts, hb, half_specs, par_specs = _mlp_act_specs(s)

    def body(g_ref, gp_ref, v_ref, vp_ref, wg_ref, bg_ref, wv_ref, bv_ref, o_ref):
        has_prev = (pl.program_id(0) > 0).astype(F32)

        def chunk(cidx, carry):
            cols = pl.ds(pl.multiple_of(cidx * CONV_CHUNK, CONV_CHUNK), CONV_CHUNK)
            u_g, _, _ = _conv3(g_ref[:, cols], gp_ref[:, cols], wg_ref.at[:, cols], bg_ref.at[:, cols], has_prev)
            u_v, _, _ = _conv3(v_ref[:, cols], vp_ref[:, cols], wv_ref.at[:, cols], bv_ref.at[:, cols], has_prev)
            o_ref[:, cols] = (_gelu_tanh(u_g)[0] * u_v).astype(o_ref.dtype)
            return carry

        lax.fori_loop(0, D_FF // CONV_CHUNK, chunk, 0)

    return pl.pallas_call(
        body, name="mlp_act_fwd", grid=(s // ts,),
        in_specs=half_specs(0) + half_specs(1) + par_specs(0) + par_specs(1),
        out_specs=pl.BlockSpec((ts, D_FF), lambda i: (i, 0)),
        out_shape=jax.ShapeDtypeStruct((s, D_FF), BF16),
        compiler_params=_params("parallel"),
    )(up, up, up, up, conv_w, conv_b, conv_w, conv_b)


def _mlp_act_bwd_call(up, conv_w, conv_b, dact):
    s = up.shape[0]
    ts, hb, half_specs, par_specs = _mlp_act_specs(s)
    nt = s // ts
    ext = ts + HALO
    bf16_rows = 2 * HALO

    def next_spec(rows, h):
        return pl.BlockSpec((rows, D_FF), lambda i: (jnp.minimum((i + 1) * (ts // rows), s // rows - 1), h))

    def body(g_ref, gp_ref, gn_ref, v_ref, vp_ref, vn_ref, wg_ref, bg_ref, wv_ref, bv_ref, da_ref, dan_ref,
             dup_ref, dwg_ref, dbg_ref, dwv_ref, dbv_ref):
        i = pl.program_id(0)
        has_prev, has_next = (i > 0).astype(F32), (i < nt - 1).astype(F32)

        @pl.when(i == 0)
        def _():
            for ref in (dwg_ref, dbg_ref, dwv_ref, dbv_ref):
                ref[...] = jnp.zeros_like(ref)

        def chunk(cidx, carry):
            cols = pl.ds(pl.multiple_of(cidx * CONV_CHUNK, CONV_CHUNK), CONV_CHUNK)
            g_ext = jnp.concatenate([g_ref[:, cols], gn_ref[:, cols]], axis=0)
            v_ext = jnp.concatenate([v_ref[:, cols], vn_ref[:, cols]], axis=0)
            u_g, g1, g2 = _conv3(g_ext, gp_ref[:, cols], wg_ref.at[:, cols], bg_ref.at[:, cols], has_prev)
            u_v, v1, v2 = _conv3(v_ext, vp_ref[:, cols], wv_ref.at[:, cols], bv_ref.at[:, cols], has_prev)
            da_ext = jnp.concatenate([da_ref[:, cols].astype(F32),
                                      dan_ref[:, cols].astype(F32)[0:HALO] * has_next], axis=0)
            act_g, dact_g = _gelu_tanh(u_g)
            du_g = da_ext * u_v * dact_g
            du_v = da_ext * act_g
            for du, w_ref, x0, x1, x2, dw_ref, db_ref, lo in ((du_g, wg_ref, g_ext, g1, g2, dwg_ref, dbg_ref, 0),
                                                          (du_v, wv_ref, v_ext, v1, v2, dwv_ref, dbv_ref, D_FF)):
                d1 = pltpu.roll(du, ext - 1, 0)
                d2 = pltpu.roll(du, ext - 2, 0)
                dup = w_ref[2:3, cols] * du + w_ref[1:2, cols] * d1 + w_ref[0:1, cols] * d2
                out_cols = pl.ds(pl.multiple_of(lo + cidx * CONV_CHUNK, CONV_CHUNK), CONV_CHUNK)
                dup_ref[:, out_cols] = dup[0:ts].astype(dup_ref.dtype)
                own = du[0:ts]
                dw_ref[0:1, cols] += _row_sum(own * x2[0:ts])
                dw_ref[1:2, cols] += _row_sum(own * x1[0:ts])
                dw_ref[2:3, cols] += _row_sum(own * x0[0:ts])
                db_ref[:, cols] += _row_sum(own)
            return carry

        lax.fori_loop(0, D_FF // CONV_CHUNK, chunk, 0)

    par_out = [pl.BlockSpec((CONV_W, D_FF), lambda i: (0, 0)), pl.BlockSpec((1, D_FF), lambda i: (0, 0))]
    par_shapes = [jax.ShapeDtypeStruct((CONV_W, D_FF), F32), jax.ShapeDtypeStruct((1, D_FF), F32)]
    return pl.pallas_call(
        body, name="mlp_act_bwd", grid=(nt,),
        in_specs=(half_specs(0) + [next_spec(HALO, 0)] + half_specs(1) + [next_spec(HALO, 1)]
                  + par_specs(0) + par_specs(1)
                  + [pl.BlockSpec((ts, D_FF), lambda i: (i, 0)), next_spec(bf16_rows, 0)]),
        out_specs=[pl.BlockSpec((ts, 2 * D_FF), lambda i: (i, 0))] + par_out + par_out,
        out_shape=[jax.ShapeDtypeStruct((s, 2 * D_FF), BF16)] + par_shapes + par_shapes,
        compiler_params=_params("arbitrary"),
    )(up, up, up, up, up, up, conv_w, conv_b, conv_w, conv_b, dact, dact)


@jax.custom_vjp
def mlp_up(h2, w_up_t, conv_w, conv_b):
    return _mlp_act_fwd_call(_matmul(h2, w_up_t, "nt", out_dtype=F32, name="w_up_fwd"), conv_w, conv_b)


def _mlp_up_fwd(h2, w_up_t, conv_w, conv_b):
    up = _matmul(h2, w_up_t, "nt", out_dtype=F32, name="w_up_fwd")
    return _mlp_act_fwd_call(up, conv_w, conv_b), (h2, w_up_t, up, conv_w, conv_b)


def _mlp_up_bwd(res, dact):
    h2, w_up_t, up, conv_w, conv_b = res
    dup, dwg, dbg, dwv, dbv = _mlp_act_bwd_call(up, conv_w, conv_b, dact)
    dh2 = _matmul(dup, w_up_t, "nn", out_dtype=h2.dtype, name="w_up_da")
    dw = _matmul(dup, h2, "tn", out_dtype=w_up_t.dtype, name="w_up_dw")
    return dh2, dw, jnp.concatenate([dwg, dwv], axis=1), jnp.concatenate([dbg, dbv], axis=1)


mlp_up.defvjp(_mlp_up_fwd, _mlp_up_bwd)


SWA_ROWS = A_GROUP * SWA_BLOCK


def _swa_sink_rows(sink_ref, g):
    return jnp.concatenate([jnp.full((SWA_BLOCK, 1), sink_ref[g * A_GROUP + h], F32) for h in range(A_GROUP)], axis=0)


def _swa_operands(q_ref, kp_ref, kc_ref, vp_ref, vc_ref, sink_ref):
    groups = []
    for g in range(A_KV_HEADS):
        groups.append((_swa_stack_heads(q_ref, g), _dup_half(kp_ref[...], g), _dup_half(kc_ref[...], g),
                       _dup_half(vp_ref[...], g), _dup_half(vc_ref[...], g)))
    return groups, jnp.concatenate([_swa_sink_rows(sink_ref, g) for g in range(A_KV_HEADS)], axis=0)


def _swa_probs(groups, sink, prev_off):
    scale = A_HEAD_DIM ** -0.5
    sp = jnp.concatenate([lax.dot_general(gr[0], gr[1], NT_DIMS, preferred_element_type=F32) for gr in groups], axis=0)
    sc = jnp.concatenate([lax.dot_general(gr[0], gr[2], NT_DIMS, preferred_element_type=F32) for gr in groups], axis=0)
    qi = lax.broadcasted_iota(jnp.int32, sp.shape, 0) & (SWA_BLOCK - 1)
    kj = lax.broadcasted_iota(jnp.int32, sp.shape, 1)
    in_cur = kj <= qi
    sw = jnp.where(in_cur, sc, jnp.where(kj > qi + prev_off, sp, -jnp.inf)) * scale
    m = jnp.maximum(jnp.max(sw, axis=-1, keepdims=True), sink)
    e, es = jnp.exp(sw - m), jnp.exp(sink - m)
    den = jnp.sum(e, axis=-1, keepdims=True) + es
    return e / den, in_cur, es / den


def _swa_split(t, in_cur):
    cur = jnp.where(in_cur, t, 0.0)
    return t - cur, cur


MLA_SCALE = (NOPE_DIM + ROPE_DIM) ** -0.5
EXP2_SCALE = MLA_SCALE * float(np.log2(np.e))
NT_DIMS = (((1,), (1,)), ((), ()))
TN_DIMS = (((0,), (0,)), ((), ()))


LANES = 128
HALF = LANES // 2


def _low_half(shape):
    return lax.broadcasted_iota(jnp.int32, shape, len(shape) - 1) < HALF


def _dup_half(x, g):
    xf = x.astype(F32)
    keep = _low_half(xf.shape) if g == 0 else jnp.logical_not(_low_half(xf.shape))
    xm = jnp.where(keep, xf, 0.0)
    return (xm + pltpu.roll(xm, HALF, 1)).astype(x.dtype)


def _fold_half(r, g):
    total = r + pltpu.roll(r, HALF, 1)
    keep = _low_half(r.shape) if g == 0 else jnp.logical_not(_low_half(r.shape))
    return jnp.where(keep, total, 0.0)


def _swa_stack_heads(ref, g):
    parts = []
    for tile in range(2):
        slab = ref[:, (2 * g + tile) * LANES:(2 * g + tile + 1) * LANES]
        low = _low_half(slab.shape)
        parts += [jnp.where(low, slab, jnp.zeros_like(slab)), jnp.where(low, jnp.zeros_like(slab), slab)]
    return jnp.concatenate(parts, axis=0)


def _swa_unstack_heads(ref, g, rows):
    for tile in range(2):
        a = rows[(2 * tile) * SWA_BLOCK:(2 * tile + 1) * SWA_BLOCK]
        b = rows[(2 * tile + 1) * SWA_BLOCK:(2 * tile + 2) * SWA_BLOCK]
        ref[:, (2 * g + tile) * LANES:(2 * g + tile + 1) * LANES] = jnp.where(_low_half(a.shape), a, b).astype(ref.dtype)


def _swa_nat_specs():
    blk = SWA_BLOCK
    q_spec = pl.BlockSpec((blk, A_HEADS * A_HEAD_DIM), lambda n: (n, 0))
    prev_spec = pl.BlockSpec((blk, LANES), lambda n: (jnp.maximum(n - 1, 0), 0))
    cur_spec = pl.BlockSpec((blk, LANES), lambda n: (n, 0))
    return q_spec, prev_spec, cur_spec, pl.BlockSpec(memory_space=pltpu.SMEM)


def _swa_nat_fwd_call(q, k, v, sinks, shards):
    s = q.shape[0]
    nblk = s // SWA_BLOCK
    n_arr = len(shards)
    q_spec, prev_spec, cur_spec, sink_spec = _swa_nat_specs()

    def body(*refs):
        q_ref, kp_ref, kc_ref, vp_ref, vc_ref, sink_ref = refs[:6]
        o_ref = refs[6 + n_arr]
        n = pl.program_id(0)
        ag_start, ag_forward, ag_finish = _allgather_phases(refs[6:6 + n_arr], refs[7 + n_arr:7 + 2 * n_arr],
                                                            *refs[7 + 2 * n_arr:])

        @pl.when(n == 0)
        def _():
            ag_start()

        @pl.when(n == nblk // 2)
        def _():
            ag_forward()

        prev_off = jnp.where(n > 0, 0, SWA_BLOCK)
        groups, sink = _swa_operands(q_ref, kp_ref, kc_ref, vp_ref, vc_ref, sink_ref)
        p, in_cur, _ = _swa_probs(groups, sink, prev_off)
        ppb, pcb = [t.astype(BF16) for t in _swa_split(p, in_cur)]
        for g, (_, _, _, vp, vc) in enumerate(groups):
            rows = slice(g * SWA_ROWS, (g + 1) * SWA_ROWS)
            out = (jnp.dot(ppb[rows], vp, preferred_element_type=F32)
                   + jnp.dot(pcb[rows], vc, preferred_element_type=F32))
            _swa_unstack_heads(o_ref, g, out)

        @pl.when(n == nblk - 1)
        def _():
            ag_finish()

    return pl.pallas_call(
        body, name="swa_fwd", grid=(nblk,),
        in_specs=[q_spec, prev_spec, cur_spec, prev_spec, cur_spec, sink_spec] + [HBM_SPEC] * n_arr,
        out_specs=[q_spec] + [HBM_SPEC] * n_arr,
        out_shape=[jax.ShapeDtypeStruct(q.shape, BF16)] + _allgather_out_shapes(shards),
        scratch_shapes=_allgather_sems(n_arr),
        compiler_params=_params("arbitrary"),
    )(q, k, k, v, v, sinks, *shards)


def _swa_nat_bwd_call(q, k, v, sinks, do, parts):
    s = q.shape[0]
    nblk = s // SWA_BLOCK
    n_arr = len(parts)
    q_spec, prev_spec, cur_spec, sink_spec = _swa_nat_specs()
    scale = A_HEAD_DIM ** -0.5
    dsink_spec = pl.BlockSpec((A_KV_HEADS, SWA_ROWS, 1), lambda n: (0, 0, 0))

    def body(*refs):
        q_ref, kp_ref, kc_ref, vp_ref, vc_ref, sink_ref, do_ref = refs[:7]
        dq_ref, dkp_ref, dkc_ref, dvp_ref, dvc_ref, dsink_ref = refs[7 + n_arr:13 + n_arr]
        n = pl.program_id(0)
        exchange_start, exchange_finish = _exchange_chips_phases(
            refs[7:7 + n_arr], refs[13 + n_arr:13 + 2 * n_arr], *refs[13 + 2 * n_arr:])

        @pl.when(n == 0)
        def _():
            exchange_start()
        prev_off = jnp.where(n > 0, 0, SWA_BLOCK)

        @pl.when(n == 0)
        def _():
            dsink_ref[...] = jnp.zeros_like(dsink_ref)

        groups, sink = _swa_operands(q_ref, kp_ref, kc_ref, vp_ref, vc_ref, sink_ref)
        dobs = [_swa_stack_heads(do_ref, g) for g in range(A_KV_HEADS)]
        p, in_cur, ps = _swa_probs(groups, sink, prev_off)
        ppb, pcb = [t.astype(BF16) for t in _swa_split(p, in_cur)]

        def per_group(fn):
            return jnp.concatenate([fn(g, slice(g * SWA_ROWS, (g + 1) * SWA_ROWS)) for g in range(A_KV_HEADS)], axis=0)

        out = per_group(lambda g, rows: jnp.dot(ppb[rows], groups[g][3], preferred_element_type=F32)
                        + jnp.dot(pcb[rows], groups[g][4], preferred_element_type=F32))
        delta = jnp.sum(jnp.concatenate(dobs, axis=0).astype(F32) * out, axis=-1, keepdims=True)
        dp = jnp.where(in_cur,
                       per_group(lambda g, rows: lax.dot_general(dobs[g], groups[g][4], NT_DIMS,
                                                                 preferred_element_type=F32)),
                       per_group(lambda g, rows: lax.dot_general(dobs[g], groups[g][3], NT_DIMS,
                                                                 preferred_element_type=F32)))
        dsp, dsc = [t.astype(BF16) for t in _swa_split(p * (dp - delta), in_cur)]
        dsink_ref[...] += (-ps * delta).reshape(dsink_ref.shape)
        totals = [jnp.zeros((SWA_BLOCK, LANES), F32) for _ in range(4)]
        for g, (qb, kp, kc, _, _) in enumerate(groups):
            rows = slice(g * SWA_ROWS, (g + 1) * SWA_ROWS)
            dq = (jnp.dot(dsp[rows], kp, preferred_element_type=F32)
                  + jnp.dot(dsc[rows], kc, preferred_element_type=F32)) * scale
            _swa_unstack_heads(dq_ref, g, dq)
            pieces = [lax.dot_general(dsp[rows], qb, TN_DIMS, preferred_element_type=F32) * scale,
                      lax.dot_general(dsc[rows], qb, TN_DIMS, preferred_element_type=F32) * scale,
                      lax.dot_general(ppb[rows], dobs[g], TN_DIMS, preferred_element_type=F32),
                      lax.dot_general(pcb[rows], dobs[g], TN_DIMS, preferred_element_type=F32)]
            totals = [tot + _fold_half(r, g) for tot, r in zip(totals, pieces)]
        dkp_ref[...], dkc_ref[...], dvp_ref[...], dvc_ref[...] = totals

        @pl.when(n == nblk - 1)
        def _():
            exchange_finish()

    kv_shape = jax.ShapeDtypeStruct(k.shape, F32)
    return pl.pallas_call(
        body, name="swa_bwd", grid=(nblk,),
        in_specs=[q_spec, prev_spec, cur_spec, prev_spec, cur_spec, sink_spec, q_spec] + [HBM_SPEC] * n_arr,
        out_specs=[q_spec, cur_spec, cur_spec, cur_spec, cur_spec, dsink_spec] + [HBM_SPEC] * n_arr,
        out_shape=[jax.ShapeDtypeStruct(q.shape, q.dtype), kv_shape, kv_shape, kv_shape, kv_shape,
                   jax.ShapeDtypeStruct((A_KV_HEADS, SWA_ROWS, 1), F32)]
                  + [jax.ShapeDtypeStruct(p.shape, p.dtype) for p in parts],
        scratch_shapes=_exchange_chips_sems(n_arr),
        compiler_params=_params("arbitrary"),
    )(q, k, k, v, v, sinks, do, *parts)


@jax.custom_vjp
def swa_nat(q, k, v, sinks, shards):
    out = _swa_nat_fwd_call(q, k, v, sinks, [s.astype(BF16) for s in shards])
    return out[0], tuple(out[1:])


def _swa_nat_fwd(q, k, v, sinks, shards):
    out = _swa_nat_fwd_call(q, k, v, sinks, [s.astype(BF16) for s in shards])
    return (out[0], tuple(out[1:])), (q, k, v, sinks)


def _swa_nat_bwd(res, cts):
    q, k, v, sinks = res
    do, d_gathered = cts
    out = _swa_nat_bwd_call(q, k, v, sinks, do, _reduce_scatter_head(d_gathered, "mid_grads"))
    dq, dkp, dkc, dvp, dvc, dsink = out[:6]

    def fold(prev_part, cur_part):
        shifted = jnp.concatenate([prev_part[SWA_BLOCK:], jnp.zeros_like(prev_part[:SWA_BLOCK])], axis=0)
        return (cur_part + shifted).astype(k.dtype)

    dsinks = jnp.sum(dsink.reshape(A_HEADS, SWA_BLOCK), axis=1)
    return dq, fold(dkp, dkc), fold(dvp, dvc), dsinks, _reduce_scatter_tail(out[6:], "mid_grads")


swa_nat.defvjp(_swa_nat_fwd, _swa_nat_bwd)

N_PAIR = B_HEADS // 2


def _flash_nat_fwd_call(q, k, v, shards):
    s = q.shape[0]
    t = min(FLASH_T, s)
    nb = s // t
    d = LANES
    n_arr = len(shards)

    def body(*refs):
        q_ref, k_ref, v_ref = refs[:3]
        shard_refs = refs[3:3 + n_arr]
        o_ref, lse_ref = refs[3 + n_arr:5 + n_arr]
        gathered_refs = refs[5 + n_arr:5 + 2 * n_arr]
        vt_ref, m_ref, l_ref, acc_ref = refs[5 + 2 * n_arr:9 + 2 * n_arr]
        pair, i = pl.program_id(0), pl.program_id(1)
        ag_start, ag_forward, ag_finish = _allgather_phases(shard_refs, gathered_refs, *refs[9 + 2 * n_arr:])

        @pl.when((pair == 0) & (i == 0))
        def _():
            ag_start()

        @pl.when((pair == N_PAIR // 2) & (i == 0))
        def _():
            ag_forward()

        @pl.when(i == 0)
        def _():
            for hh in range(2):
                for chunk in range(nb):
                    rows = slice(chunk * t, (chunk + 1) * t)
                    vt_ref[hh, :, rows] = v_ref[rows, hh * d:(hh + 1) * d].T

        m_ref[...] = jnp.full_like(m_ref, -jnp.inf)
        l_ref[...] = jnp.zeros_like(l_ref)
        acc_ref[...] = jnp.zeros_like(acc_ref)

        def step(j, on_diagonal):
            keys = pl.ds(pl.multiple_of(j * t, t), t)
            scores = [lax.dot_general(k_ref[keys, hh * d:(hh + 1) * d], q_ref[:, hh * d:(hh + 1) * d], NT_DIMS,
                                      preferred_element_type=F32) for hh in range(2)]
            for hh in range(2):
                sc_t = scores[hh]
                if on_diagonal:
                    key = lax.broadcasted_iota(jnp.int32, (t, t), 0)
                    qry = lax.broadcasted_iota(jnp.int32, (t, t), 1)
                    sc_t = jnp.where(qry >= key, sc_t, -jnp.inf)
                m_old = m_ref[hh]
                m_new = jnp.maximum(m_old, jnp.max(sc_t, axis=0, keepdims=True))
                alpha = jnp.exp2((m_old - m_new) * EXP2_SCALE)
                p_t = jnp.exp2((sc_t - m_new) * EXP2_SCALE)
                l_ref[hh] = alpha * l_ref[hh] + jnp.sum(p_t, axis=0, keepdims=True)
                acc_ref[hh] = alpha * acc_ref[hh] + jnp.dot(vt_ref[hh, :, keys], p_t.astype(BF16),
                                                            preferred_element_type=F32)
                m_ref[hh] = m_new

        def below(j, carry):
            step(j, False)
            return carry

        lax.fori_loop(0, i, below, 0)
        step(i, True)
        outs =[(acc_ref[hh] / l_ref[hh]).T for hh in range(2)]
        for hh in range(2):
            lse_ref[hh] = m_ref[hh] * EXP2_SCALE + jnp.log2(l_ref[hh])
        o_ref[...] = (outs[0] + pltpu.roll(outs[1], HALF, 1)).astype(o_ref.dtype)

        @pl.when((pair == N_PAIR - 1) & (i == nb - 1))
        def _():
            ag_finish()

    return pl.pallas_call(
        body, name="mla_fwd", grid=(N_PAIR, nb),
        in_specs=[pl.BlockSpec((t, 2 * d), lambda p, i: (i, p)),
                  pl.BlockSpec((s, 2 * d), lambda p, i: (0, p)),
                  pl.BlockSpec((s, 2 * d), lambda p, i: (0, p))] + [HBM_SPEC] * n_arr,
        out_specs=[pl.BlockSpec((t, d), lambda p, i: (i, p)),
                   pl.BlockSpec((2, 1, t), lambda p, i: (p, 0, i))] + [HBM_SPEC] * n_arr,
        out_shape=[jax.ShapeDtypeStruct((s, N_PAIR * d), BF16), jax.ShapeDtypeStruct((B_HEADS, 1, s), F32)]
                  + _allgather_out_shapes(shards),
        scratch_shapes=[pltpu.VMEM((2, d, s), BF16), pltpu.VMEM((2, 1, t), F32), pltpu.VMEM((2, 1, t), F32),
                        pltpu.VMEM((2, d, t), F32)] + _allgather_sems(n_arr),
        compiler_params=_params("arbitrary", "arbitrary"),
    )(q, k, v, *shards)


def _flash_nat_delta_call(o, do):
    s, w = o.shape
    t = min(FLASH_T, s)

    def body(o_ref, do_ref, out_ref):
        prod = o_ref[...].astype(F32) * do_ref[...].astype(F32)
        lane = lax.broadcasted_iota(jnp.int32, (w, LANES), 0) // V_DIM
        head = lax.broadcasted_iota(jnp.int32, (w, LANES), 1)
        out_ref[...] = jnp.dot(prod, (lane == head).astype(F32), precision=lax.Precision.HIGHEST,
                               preferred_element_type=F32)

    spec = pl.BlockSpec((t, w), lambda i: (i, 0))
    return pl.pallas_call(
        body, name="mla_delta", grid=(s // t,), in_specs=[spec, spec],
        out_specs=pl.BlockSpec((t, LANES), lambda i: (i, 0)),
        out_shape=jax.ShapeDtypeStruct((s, LANES), F32), compiler_params=_params("parallel"),
    )(o, do)


def _flash_nat_bwd_call(q, k, v, lse_row, delta_row, do, parts):
    s = q.shape[0]
    t = min(FLASH_T, s)
    nb = s // t
    d = LANES
    n_arr = len(parts)

    def body(*refs):
        q_ref, k_ref, v_ref, lse_ref, delta_ref, do_ref = refs[:6]
        part_refs = refs[6:6 + n_arr]
        dq_ref, dk_ref, dv_ref = refs[6 + n_arr:9 + n_arr]
        received_refs = refs[9 + n_arr:9 + 2 * n_arr]
        dq_acc, dk_acc, dv_acc = refs[9 + 2 * n_arr:12 + 2 * n_arr]
        pair, j = pl.program_id(0), pl.program_id(1)
        exchange_start, exchange_finish = _exchange_chips_phases(part_refs, received_refs, *refs[12 + 2 * n_arr:])

        @pl.when((pair == 0) & (j == 0))
        def _():
            exchange_start()

        @pl.when(j == 0)
        def _():
            dq_acc[...] = jnp.zeros_like(dq_acc)

        for hh in range(2):
            kb, vb = k_ref[:, hh * d:(hh + 1) * d], v_ref[:, hh * d:(hh + 1) * d]
            dk_acc[...] = jnp.zeros_like(dk_acc)
            dv_acc[...] = jnp.zeros_like(dv_acc)

            def step(i, on_diagonal, hh=hh, kb=kb, vb=vb):
                rows = pl.ds(pl.multiple_of(i * t, t), t)
                qb = q_ref[rows, hh * d:(hh + 1) * d]
                do_pair = do_ref[rows, :].astype(F32)
                do_h = do_pair if hh == 0 else pltpu.roll(do_pair, HALF, 1)
                dob = jnp.where(_low_half(do_h.shape), do_h, 0.0).astype(BF16)
                sc_t = lax.dot_general(kb, qb, NT_DIMS, preferred_element_type=F32)
                p_t = jnp.exp2(sc_t * EXP2_SCALE - lse_ref[hh, :, rows])
                if on_diagonal:
                    key = lax.broadcasted_iota(jnp.int32, (t, t), 0)
                    qry = lax.broadcasted_iota(jnp.int32, (t, t), 1)
                    p_t = jnp.where(qry >= key, p_t, 0.0)
                dp_t = lax.dot_general(vb, dob, NT_DIMS, preferred_element_type=F32)
                ds_t = (p_t * (dp_t - delta_ref[hh, :, rows])).astype(BF16)
                dv_acc[...] += jnp.dot(p_t.astype(BF16), dob, preferred_element_type=F32)
                dk_acc[...] += jnp.dot(ds_t, qb, preferred_element_type=F32)
                dq_acc[hh, rows, :] += lax.dot_general(ds_t, kb, TN_DIMS, preferred_element_type=F32)

            def above(i, carry, step=step):
                step(i, False)
                return carry

            step(j, True)
            lax.fori_loop(j + 1, nb, above, 0)
            dk_ref[:, hh * d:(hh + 1) * d] = (dk_acc[...] * MLA_SCALE).astype(dk_ref.dtype)
            dv_ref[:, hh * d:(hh + 1) * d] = dv_acc[...].astype(dv_ref.dtype)

        @pl.when(j == nb - 1)
        def _():
            for hh in range(2):
                dq_ref[:, hh * d:(hh + 1) * d] = (dq_acc[hh] * MLA_SCALE).astype(dq_ref.dtype)

        @pl.when((pair == N_PAIR - 1) & (j == nb - 1))
        def _():
            exchange_finish()

    full_spec = pl.BlockSpec((s, 2 * d), lambda p, j: (0, p))
    tile_spec = pl.BlockSpec((t, 2 * d), lambda p, j: (j, p))
    row_spec = pl.BlockSpec((2, 1, s), lambda p, j: (p, 0, 0))
    return pl.pallas_call(
        body, name="mla_bwd", grid=(N_PAIR, nb),
        in_specs=[full_spec, tile_spec, tile_spec, row_spec, row_spec, pl.BlockSpec((s, d), lambda p, j: (0, p))]
                 + [HBM_SPEC] * n_arr,
        out_specs=[full_spec, tile_spec, tile_spec] + [HBM_SPEC] * n_arr,
        out_shape=[jax.ShapeDtypeStruct(q.shape, q.dtype)] * 3 + [jax.ShapeDtypeStruct(p.shape, p.dtype) for p in parts],
        scratch_shapes=[pltpu.VMEM((2, s, d), F32), pltpu.VMEM((t, d), F32), pltpu.VMEM((t, d), F32)]
                       + _exchange_chips_sems(n_arr),
        compiler_params=_params("arbitrary", "arbitrary"),
    )(q, k, v, lse_row, delta_row, do, *parts)


def _reduce_scatter_head(cts, tag):
    received = _exchange_sibling(list(cts), tag + "_exchange_sibling")
    my_c = lax.axis_index("c").astype(jnp.int32).reshape(1)
    return [_pair_add(m, r, my_c, "%s_pair_add_%d" % (tag, i)) for i, (m, r) in enumerate(zip(cts, received))]


def _reduce_scatter_tail(chip_parts, tag):
    return tuple(_sum_blocks(r, "%s_sum_%d" % (tag, i)) for i, r in enumerate(chip_parts))


@jax.custom_vjp
def flash_nat(q, k, v, shards):
    out = _flash_nat_fwd_call(q, k, v, [s.astype(BF16) for s in shards])
    return out[0], tuple(out[2:])


def _flash_nat_fwd(q, k, v, shards):
    out = _flash_nat_fwd_call(q, k, v, [s.astype(BF16) for s in shards])
    return (out[0], tuple(out[2:])), (q, k, v, out[0], out[1])


def _flash_nat_bwd(res, cts):
    q, k, v, o, lse = res
    do, d_gathered = cts
    delta = _flash_nat_delta_call(o, do)[:, :B_HEADS].T.reshape(B_HEADS, 1, q.shape[0])
    out = _flash_nat_bwd_call(q, k, v, lse, delta, do, _reduce_scatter_head(d_gathered, "mlp_grads"))
    return out[0], out[1], out[2], _reduce_scatter_tail(out[3:], "mlp_grads")


flash_nat.defvjp(_flash_nat_fwd, _flash_nat_bwd)


HBM_SPEC = pl.BlockSpec(memory_space=pltpu.HBM)


def _allgather(shards, name):
    n_arr = len(shards)

    def body(*refs):
        start, forward, finish = _allgather_phases(refs[:n_arr], refs[n_arr:2 * n_arr], *refs[2 * n_arr:])
        start()
        forward()
        finish()

    return pl.pallas_call(
        body, name=name, out_shape=_allgather_out_shapes(shards),
        in_specs=[HBM_SPEC] * n_arr, out_specs=[HBM_SPEC] * n_arr,
        scratch_shapes=_allgather_sems(n_arr),
    )(*shards)


def _allgather_out_shapes(shards):
    return [jax.ShapeDtypeStruct((N_DEV,) + s.shape, s.dtype) for s in shards]


def _allgather_sems(n_arr):
    return [pltpu.SemaphoreType.DMA((7, n_arr)), pltpu.SemaphoreType.DMA((7, n_arr)), pltpu.SemaphoreType.DMA((n_arr,))]


def _allgather_phases(x_refs, out_refs, send_sems, recv_sems, local_sems):
    arrays = range(len(x_refs))
    x, y, c = lax.axis_index("x"), lax.axis_index("y"), lax.axis_index("c")
    me, sibling = (x, y, c), (x, y, 1 - c)
    chips = [(1 - x, y), (x, 1 - y), (1 - x, 1 - y)]

    def rows(a, px, py, pc):
        return out_refs[a].at[4 * px + 2 * py + pc]

    def copy(a, k, block, to, src=None):
        return pltpu.make_async_remote_copy(
            src_ref=rows(a, *block) if src is None else src, dst_ref=rows(a, *block),
            send_sem=send_sems.at[k, a], recv_sem=recv_sems.at[k, a], device_id=to, device_id_type=MESH_ID)

    def mine():
        return [pltpu.make_async_copy(x_refs[a], rows(a, *me), local_sems.at[a]) for a in arrays]

    def first():
        return [cp for a in arrays for cp in
                [copy(a, 0, me, sibling, src=x_refs[a])]
                + [copy(a, 1 + j, me, (*chip, c), src=x_refs[a]) for j, chip in enumerate(chips)]]

    def passed():
        return [copy(a, 4 + j, (*chip, c), sibling) for j, chip in enumerate(chips) for a in arrays]

    def start():
        for cp in mine() + first():
            cp.start()

    def forward():
        for j, chip in enumerate(chips):
            for a in arrays:
                copy(a, 1 + j, (*chip, c), me).wait_recv()
                copy(a, 4 + j, (*chip, c), sibling).start()

    def finish():
        for a in arrays:
            copy(a, 0, sibling, me).wait_recv()
        for j, chip in enumerate(chips):
            for a in arrays:
                copy(a, 4 + j, (*chip, 1 - c), me).wait_recv()
        for cp in first() + passed():
            cp.wait_send()
        for cp in mine():
            cp.wait()

    return start, forward, finish


N_CHIP = 4


def _exchange_sibling(parts, name):
    n_arr = len(parts)

    def body(*refs):
        in_refs, recv_refs = refs[:n_arr], refs[n_arr:2 * n_arr]
        send_sems, recv_sems = refs[2 * n_arr:]
        x, y, c = lax.axis_index("x"), lax.axis_index("y"), lax.axis_index("c")
        copies = []
        for a in range(n_arr):
            for q in range(N_CHIP):
                copies.append(pltpu.make_async_remote_copy(
                    src_ref=in_refs[a].at[2 * q + 1 - c], dst_ref=recv_refs[a].at[q],
                    send_sem=send_sems.at[q, a], recv_sem=recv_sems.at[q, a],
                    device_id=(x, y, 1 - c), device_id_type=MESH_ID))
        for cp in copies:
            cp.start()
        for cp in copies:
            cp.wait()

    return pl.pallas_call(
        body, name=name, out_shape=[jax.ShapeDtypeStruct((N_CHIP,) + p.shape[1:], p.dtype) for p in parts],
        in_specs=[HBM_SPEC] * n_arr, out_specs=[HBM_SPEC] * n_arr,
        scratch_shapes=[pltpu.SemaphoreType.DMA((N_CHIP, n_arr)), pltpu.SemaphoreType.DMA((N_CHIP, n_arr))],
    )(*parts)


def _exchange_chips_sems(n_arr):
    return [pltpu.SemaphoreType.DMA((N_CHIP - 1, n_arr)), pltpu.SemaphoreType.DMA((N_CHIP - 1, n_arr)),
            pltpu.SemaphoreType.DMA((n_arr,))]


def _exchange_chips_phases(in_refs, out_refs, send_sems, recv_sems, local_sems):
    n_arr = len(in_refs)
    x, y, c = lax.axis_index("x"), lax.axis_index("y"), lax.axis_index("c")
    me = 2 * x + y

    def copies():
        out = [pltpu.make_async_copy(in_refs[a].at[me], out_refs[a].at[me], local_sems.at[a]) for a in range(n_arr)]
        for k in range(1, N_CHIP):
            px = 1 - x if k & 2 else x
            py = 1 - y if k & 1 else y
            for a in range(n_arr):
                out.append(pltpu.make_async_remote_copy(
                    src_ref=in_refs[a].at[2 * px + py], dst_ref=out_refs[a].at[me],
                    send_sem=send_sems.at[k - 1, a], recv_sem=recv_sems.at[k - 1, a],
                    device_id=(px, py, c), device_id_type=MESH_ID))
        return out

    def start():
        for cp in copies():
            cp.start()

    def finish():
        for cp in copies():
            cp.wait()

    return start, finish


def _row_tile(r, ccols, blocks):
    cap = max(16, (2 * 1024 * 1024) // (4 * ccols * blocks))
    return _pick(r, cap, 16)


def _pair_add(mine, theirs, my_c, name):
    _, r, ccols = mine.shape
    tr = _row_tile(r, ccols, 1)

    def body(c_ref, a_ref, b_ref, o_ref):
        o_ref[...] = (a_ref[...].astype(F32) + b_ref[...].astype(F32)).astype(o_ref.dtype)

    spec = pl.BlockSpec((None, tr, ccols), lambda q, i, c_ref: (q, i, 0))
    return pl.pallas_call(
        body, name=name,
        grid_spec=pltpu.PrefetchScalarGridSpec(
            num_scalar_prefetch=1, grid=(N_CHIP, r // tr),
            in_specs=[pl.BlockSpec((None, tr, ccols), lambda q, i, c_ref: (2 * q + c_ref[0], i, 0)), spec],
            out_specs=spec),
        out_shape=jax.ShapeDtypeStruct(theirs.shape, theirs.dtype),
        compiler_params=_params("parallel", "parallel"),
    )(my_c, mine, theirs)


def _sum_blocks(parts, name):
    nb, r, ccols = parts.shape
    tr = _row_tile(r, ccols, nb)

    def body(p_ref, o_ref):
        acc = p_ref[0].astype(F32)
        for i in range(1, nb):
            acc = acc + p_ref[i].astype(F32)
        o_ref[...] = acc

    return pl.pallas_call(
        body, name=name, grid=(r // tr,),
        in_specs=[pl.BlockSpec((nb, tr, ccols), lambda i: (0, i, 0))],
        out_specs=pl.BlockSpec((tr, ccols), lambda i: (i, 0)),
        out_shape=jax.ShapeDtypeStruct((r, ccols), F32),
        compiler_params=_params("parallel"),
    )(parts)


@jax.custom_vjp
def replicated(vec):
    return vec


def _replicated_fwd(vec):
    return vec, None


def _replicated_bwd(_, ct):
    return (_sum_blocks(_allgather([ct], "small_grad_allgather")[0], "small_grad_sum"),)


replicated.defvjp(_replicated_fwd, _replicated_bwd)


def _adamw(w, g, m, v, name):
    rows, cols = w.shape
    tr = _pick(rows, 256, 8) if rows % 8 == 0 else rows

    def body(w_ref, g_ref, m_ref, v_ref, d_ref, nm_ref, nv_ref):
        g_ = g_ref[...]
        m_ = ADAM_B1 * m_ref[...] + (1.0 - ADAM_B1) * g_
        v_ = ADAM_B2 * v_ref[...] + (1.0 - ADAM_B2) * jnp.square(g_)
        m_hat = m_ / (1.0 - ADAM_B1 ** ADAM_STEP)
        v_hat = v_ / (1.0 - ADAM_B2 ** ADAM_STEP)
        d_ref[...] = -ADAM_LR * (m_hat / (jnp.sqrt(v_hat) + ADAM_EPS) + ADAM_WD * w_ref[...])
        nm_ref[...] = m_
        nv_ref[...] = v_

    spec = pl.BlockSpec((tr, cols), lambda i: (i, 0))
    return pl.pallas_call(
        body, name=name, grid=(rows // tr,), in_specs=[spec] * 4, out_specs=[spec] * 3,
        out_shape=[jax.ShapeDtypeStruct(w.shape, F32)] * 3, compiler_params=_params("parallel"),
    )(w, g, m, v)


COL_SHARDED = ("w_in", "w_uq", "w_ukv", "w_branch_a", "w_branch_b", "w_up", "w_ple")
EARLY = ("w_in",)
MID = ("w_uq", "w_ukv", "w_branch_a", "w_branch_b", "w_out")
LATE = ("w_up", "w_down", "w_ple_gate", "w_ple")
SMALL = ("attn_pre_norm", "attn_post_norm", "b_gate", "q_a_norm", "kv_a_norm", "mlp_pre_norm", "mlp_post_norm",
         "conv_b", "ple_norm", "sinks")
SMALL_COLS = 128


def _pack_rows(arrays, cols, row_mult):
    flat = jnp.concatenate([a.reshape(-1) for a in arrays])
    pad = (-flat.shape[0]) % (cols * row_mult)
    return jnp.pad(flat, (0, pad)).reshape(-1, cols)


def _unpack_small(vec, shapes):
    flat = vec.reshape(-1)
    out, off = {}, 0
    for name in SMALL:
        n = shapes[name]
        out[name] = flat[off:off + n].reshape(1, n)
        off += n + (-n) % SMALL_COLS
    return out


def _pad_lanes(t, width):
    return jnp.pad(t, [(0, 0)] * (t.ndim - 1) + [(0, width - t.shape[-1])])


def _pad_rows(t, rows):
    return jnp.pad(t, [(0, 0)] * (t.ndim - 2) + [(0, rows - t.shape[-2]), (0, 0)])


FRONT_SIZES = (512, 128, 128, 256, 128)
FRONT_BOUNDS = (0, 512, 640, 768, 1024, 1152, 1280)
PE_LANE = NOPE_DIM


def _arrange_w_in_t(wt):
    k = wt.shape[1]
    n_front = sum(FRONT_SIZES)
    front, kr, gates = wt[:n_front], wt[n_front:n_front + ROPE_DIM], wt[n_front + ROPE_DIM:]
    kr_slab = jnp.concatenate([jnp.zeros((PE_LANE, k), wt.dtype), kr,
                               jnp.zeros((HEAD_PAD - PE_LANE - ROPE_DIM, k), wt.dtype)], axis=0)
    return jnp.concatenate([front, kr_slab], axis=0), gates


def _arrange_w_uq_t(wt):
    k = wt.shape[1]
    return _pad_rows(wt.reshape(B_HEADS, NOPE_DIM + ROPE_DIM, k), HEAD_PAD).reshape(B_HEADS * HEAD_PAD, k)


def _arrange_w_ukv_t(wt):
    k = wt.shape[1]
    w = wt.reshape(B_HEADS, 2, NOPE_DIM, k)
    slabs = [_pad_rows(w[:, part], HEAD_PAD).reshape(B_HEADS * HEAD_PAD, k) for part in range(2)]
    return jnp.concatenate(slabs, axis=0)


def _rope_tables(positions, s):
    pos = positions.reshape(s, 1).astype(F32)

    def angles(dim):
        return pos * ROPE_THETA ** (-(jnp.arange(0, dim, 2, dtype=F32) / dim))

    cos_a, sin_a = jnp.cos(angles(A_HEAD_DIM)), jnp.sin(angles(A_HEAD_DIM))
    zero_a = jnp.zeros_like(sin_a)
    tables_a = [jnp.tile(jnp.concatenate(pair, axis=1), (1, LANES // A_HEAD_DIM))
                for pair in ((cos_a, cos_a), (-sin_a, zero_a), (zero_a, sin_a))]
    cos_b, sin_b = jnp.cos(angles(ROPE_DIM)), jnp.sin(angles(ROPE_DIM))
    zero_b = jnp.zeros_like(sin_b)

    def slab(first, second, fill):
        return jnp.concatenate([jnp.full((s, PE_LANE), fill, F32), first, second,
                                jnp.full((s, HEAD_PAD - PE_LANE - ROPE_DIM), fill, F32)], axis=1)

    tables_b = [slab(cos_b, cos_b, 1.0), slab(-sin_b, zero_b, 0.0), slab(zero_b, sin_b, 0.0)]
    return tables_a + tables_b


def _local_loss(wts, x, p, tables, target):
    s = x.shape[0]
    small_shapes = {n: wts[n].shape[-1] for n in SMALL}
    small_vec = _pack_rows([_pad_lanes(wts[n].reshape(1, -1), small_shapes[n] + (-small_shapes[n]) % SMALL_COLS)
                            for n in SMALL], SMALL_COLS, 8)
    sm = _unpack_small(replicated(small_vec), small_shapes)
    def shard(n):
        return wts[n].T if n in COL_SHARDED else wts[n]

    h1_front, h1_gates, x_res, gathered = prenorm_gather(
        x, sm["attn_pre_norm"], tuple([shard(n) for n in EARLY] + [_pack_rows([wts["conv_w"]], SMALL_COLS, 8)]),
        (BF16,) * len(EARLY) + (F32,))
    big = {n: g.reshape(-1, g.shape[2]) for n, g in zip(EARLY, gathered)}
    ch = wts["conv_w"].shape[1]
    conv_w = gathered[-1].reshape(N_DEV, -1)[:, :CONV_W * ch].reshape(N_DEV, CONV_W, ch)
    conv_w = conv_w.transpose(1, 0, 2).reshape(CONV_W, N_DEV * ch)

    w_front_t, w_gates_t = _arrange_w_in_t(big["w_in"])
    tables_a, tables_b = tables[:3], tables[3:]

    qa, ka, va, cqn, ckvn, kpe = proj_stage(
        "prep", _f_prep, [(h1_front, w_front_t, "nt", "w_front", True, F32)], params=[sm["q_a_norm"], sm["kv_a_norm"]],
        consts=tables, splits=[FRONT_BOUNDS], ts=512, out_dtypes=[BF16, BF16, BF16, BF16, BF16, F32])
    ya, mid = swa_nat(qa, ka, va, sm["sinks"].reshape(-1), tuple(shard(n) for n in MID))
    big.update({n: g.reshape(-1, g.shape[2]) for n, g in zip(MID, mid)})

    (q2,) = proj_stage("qrope", _f_qrope, [(cqn, _arrange_w_uq_t(big["w_uq"]), "nt", "w_uq", True, BF16)],
                       consts=tables_b, ts=512, out_dtypes=[BF16])
    k2, v2 = proj_stage("kv", _f_kv, [(ckvn, _arrange_w_ukv_t(big["w_ukv"]), "nt", "w_ukv", True, BF16)],
                        extra=[kpe], splits=[(0, B_HEADS * HEAD_PAD, 2 * B_HEADS * HEAD_PAD), None], ts=512,
                        out_dtypes=[BF16, BF16])
    yb, late = flash_nat(q2, k2, v2, tuple(shard(n) for n in LATE))
    big.update({n: g.reshape(-1, g.shape[2]) for n, g in zip(LATE, late)})

    (mixed,) = proj_stage(
        "gate", _f_gate, [(h1_gates, w_gates_t, "nt", "w_gates", True, F32),
                          (ya, big["w_branch_a"], "nt", "w_branch_a", True, BF16),
                          (yb, big["w_branch_b"], "nt", "w_branch_b", True, BF16)],
        params=[sm["b_gate"][:, :D_MODEL], sm["b_gate"][:, D_MODEL:]],
        splits=[(0, D_MODEL, 2 * D_MODEL), None, None], out_dtypes=[BF16])
    x1, h2 = proj_stage("post_attn", _f_post, [(mixed, big["w_out"], "nn", "w_out", True, F32)], extra=[x_res],
                        params=[sm["attn_post_norm"], sm["mlp_pre_norm"]], ts=512, out_dtypes=[F32, BF16])

    act = mlp_up(h2, big["w_up"], conv_w, sm["conv_b"])
    x2, h3 = proj_stage("post_mlp", _f_post, [(act, big["w_down"], "nn", "w_down", True, F32)], extra=[x1],
                        params=[sm["mlp_post_norm"], sm["ple_norm"]], ts=512, out_dtypes=[F32, BF16])

    (rowloss,) = proj_stage("loss", _f_out, [(h3, big["w_ple_gate"], "nn", "w_ple_gate", True, F32),
                                             (p, big["w_ple"], "nt", "w_ple", False, BF16)], extra=[x2],
                            consts=[target], ts=512)
    return jnp.sum(rowloss)


WEIGHTS = ["attn_pre_norm", "attn_post_norm", "w_in", "b_gate", "sinks", "q_a_norm", "w_uq", "kv_a_norm", "w_ukv",
           "w_branch_a", "w_branch_b", "w_out", "mlp_pre_norm", "mlp_post_norm", "w_up", "conv_w", "conv_b",
           "w_down", "ple_norm", "w_ple_gate", "w_ple"]


def kernel(x, p, positions, attn_pre_norm, attn_post_norm, w_in, b_gate, sinks, q_a_norm, w_uq, kv_a_norm, w_ukv, w_branch_a, w_branch_b, w_out, mlp_pre_norm, mlp_post_norm, w_up, conv_w, conv_b, w_down, ple_norm, w_ple_gate, w_ple, loss_target, m_attn_pre_norm, m_attn_post_norm, m_w_in, m_b_gate, m_sinks, m_q_a_norm, m_w_uq, m_kv_a_norm, m_w_ukv, m_w_branch_a, m_w_branch_b, m_w_out, m_mlp_pre_norm, m_mlp_post_norm, m_w_up, m_conv_w, m_conv_b, m_w_down, m_ple_norm, m_w_ple_gate, m_w_ple, v_attn_pre_norm, v_attn_post_norm, v_w_in, v_b_gate, v_sinks, v_q_a_norm, v_w_uq, v_kv_a_norm, v_w_ukv, v_w_branch_a, v_w_branch_b, v_w_out, v_mlp_pre_norm, v_mlp_post_norm, v_w_up, v_conv_w, v_conv_b, v_w_down, v_ple_norm, v_w_ple_gate, v_w_ple):
    given = dict(locals())
    s = x.shape[1]
    wts = {n: given[n][0] if given[n].ndim == 3 else given[n] for n in WEIGHTS}
    tables = _rope_tables(positions, s)
    local_loss, (grads, grad_x) = jax.value_and_grad(_local_loss, argnums=(0, 1))(
        wts, x[0], p[0, 0], tables, loss_target[0])
    loss = lax.psum(local_loss, AXES)

    outs = {"grad": [], "delta": [], "m": [], "v": []}
    for n in WEIGHTS:
        shape = given[n].shape
        w2 = wts[n].reshape(-1, shape[-1])
        g2 = grads[n].reshape(w2.shape)
        delta, new_m, new_v = _adamw(w2, g2, given["m_" + n].reshape(w2.shape), given["v_" + n].reshape(w2.shape),
                                     "adamw_" + n)
        outs["grad"].append(g2.reshape(shape))
        outs["delta"].append(delta.reshape(shape))
        outs["m"].append(new_m.reshape(shape))
        outs["v"].append(new_v.reshape(shape))
    return (loss, grad_x[None], *outs["grad"], *outs["delta"], *outs["m"], *outs["v"])
```

```python
import functools

import numpy as np
import jax
import jax.numpy as jnp
from jax import lax
from jax.experimental import pallas as pl
from jax.experimental.pallas import tpu as pltpu

F32 = jnp.float32
BF16 = jnp.bfloat16
MESH_ID = pl.DeviceIdType.MESH
AXES = ("x", "y", "c")
N_DEV = 8

D_MODEL = 1024
RMS_EPS = 1e-6
ROPE_THETA = 10000.0
SWA_BLOCK = 128
A_HEADS, A_KV_HEADS, A_HEAD_DIM = 8, 2, 64
A_GROUP = A_HEADS // A_KV_HEADS
B_HEADS, Q_LORA, KV_LORA, NOPE_DIM, ROPE_DIM, V_DIM = 8, 256, 128, 64, 32, 64
D_FF = 2816
CONV_W = 3
HEAD_PAD = 128

ADAM_LR, ADAM_B1, ADAM_B2, ADAM_EPS, ADAM_WD, ADAM_STEP = 0.001, 0.9, 0.999, 1e-08, 0.01, 10

VMEM_LIMIT = 48 * 1024 * 1024
MM_TM, MM_TN, MM_TK_TOKENS = 1024, 1408, 2048
MM_VMEM_BUDGET = 36 * 1024 * 1024
FLASH_T = 1024
CONV_TS = 256
CONV_CHUNK = 256


def _params(*sem):
    return pltpu.CompilerParams(dimension_semantics=sem, vmem_limit_bytes=VMEM_LIMIT)


def _pick(dim, cap, mult):
    best = None
    for t in range(mult, min(dim, cap) + 1, mult):
        if dim % t == 0:
            best = t
    return dim if best is None else best


def _divisors(dim, mult):
    return [t for t in range(mult, dim + 1, mult) if dim % t == 0] or [dim]


def _matmul_tiles(m, n, kdim, form, sizes):
    sa, sb, so = sizes
    tk = _pick(kdim, MM_TK_TOKENS, 128) if form == "tn" else kdim
    cap_m = MM_TN if form == "tn" else MM_TM
    best = None
    for tm in _divisors(m, 128):
        for tn in _divisors(n, 128):
            need = 2 * (tm * tk * sa + tk * tn * sb + tm * tn * so) + (tm * tn * 4 if tk != kdim else 0)
            if tm > cap_m or tn > MM_TN or need > MM_VMEM_BUDGET:
                continue
            if best is None or (tm * tn, tm) > (best[0] * best[1], best[0]):
                best = (tm, tn)
    return best[0], best[1], tk


def _matmul(a, b, form, *, out_dtype=F32, name):
    if form == "tn":
        (kdim, m), n = a.shape, b.shape[1]
    else:
        (m, kdim), n = a.shape, (b.shape[1] if form == "nn" else b.shape[0])
    sizes = (a.dtype.itemsize, b.dtype.itemsize, jnp.dtype(out_dtype).itemsize)
    tm, tn, tk = _matmul_tiles(m, n, kdim, form, sizes)
    nk = kdim // tk
    rows_outer = nk > 1 or (m // tm) * b.size * sizes[1] <= (n // tn) * a.size * sizes[0]

    def ij(fn):
        return (lambda i, j, k: fn(i, j, k)) if rows_outer else (lambda j, i, k: fn(i, j, k))

    a_spec = (pl.BlockSpec((tk, tm), ij(lambda i, j, k: (k, i))) if form == "tn"
              else pl.BlockSpec((tm, tk), ij(lambda i, j, k: (i, k))))
    b_spec = (pl.BlockSpec((tn, tk), ij(lambda i, j, k: (j, k))) if form == "nt"
              else pl.BlockSpec((tk, tn), ij(lambda i, j, k: (k, j))))
    dims = (((0 if form == "tn" else 1,), (1 if form == "nt" else 0,)), ((), ()))

    def product(a_ref, b_ref):
        return lax.dot_general(a_ref[...].astype(BF16), b_ref[...].astype(BF16), dims, preferred_element_type=F32)

    if nk == 1:
        def body(a_ref, b_ref, o_ref):
            o_ref[...] = product(a_ref, b_ref).astype(o_ref.dtype)

        scratch = []
    else:
        def body(a_ref, b_ref, o_ref, acc_ref):
            k = pl.program_id(2)

            @pl.when(k == 0)
            def _():
                acc_ref[...] = jnp.zeros_like(acc_ref)

            acc_ref[...] += product(a_ref, b_ref)

            @pl.when(k == nk - 1)
            def _():
                o_ref[...] = acc_ref[...].astype(o_ref.dtype)

        scratch = [pltpu.VMEM((tm, tn), F32)]

    return pl.pallas_call(
        body, name=name, grid=(m // tm, n // tn, nk) if rows_outer else (n // tn, m // tm, nk),
        in_specs=[a_spec, b_spec],
        out_specs=pl.BlockSpec((tm, tn), ij(lambda i, j, k: (i, j))),
        out_shape=jax.ShapeDtypeStruct((m, n), out_dtype),
        scratch_shapes=scratch,
        compiler_params=_params("parallel", "parallel", "arbitrary"),
    )(a, b)


def _pairs(bounds):
    return list(zip(bounds[:-1], bounds[1:]))


def _split(v, bounds):
    return [v[:, a:b] for a, b in _pairs(bounds)]


def _stage_build(name, f, tiled, params, consts, splits, ts, out_dtypes, ct_dtypes=None):
    n_t, n_p, n_c = len(tiled), len(params), len(consts)
    ct_dtypes = [t.dtype for t in tiled] if ct_dtypes is None else ct_dtypes
    s = tiled[0].shape[0]
    ts = min(ts, s)
    grid = (s // ts,)
    if splits is None:
        splits = [None] * n_t
    in_bounds = [(0, t.shape[1]) if b is None else tuple(b) for t, b in zip(tiled, splits)]

    def tile_aval(arr):
        return jax.ShapeDtypeStruct((ts, arr.shape[1]), arr.dtype)

    slab_avals = [[jax.ShapeDtypeStruct((ts, e - a), F32) for a, e in _pairs(b)]
                  for t, b in zip(tiled, in_bounds)]
    out_avals = jax.eval_shape(f, slab_avals, list(params), [tile_aval(c) for c in consts])
    out_bounds = [tuple(np.cumsum([0] + [o.shape[1] for o in slabs]).tolist()) for slabs in out_avals]
    out_dtypes = [F32] * len(out_bounds) if out_dtypes is None else out_dtypes
    out_shapes = [jax.ShapeDtypeStruct((s, b[-1]), d) for b, d in zip(out_bounds, out_dtypes)]

    def row_spec(width):
        return pl.BlockSpec((ts, width), lambda i: (i, 0))

    def par_spec(arr):
        return pl.BlockSpec(arr.shape, lambda i: (0, 0))

    in_specs = ([row_spec(t.shape[1]) for t in tiled] + [par_spec(p) for p in params]
                + [row_spec(c.shape[1]) for c in consts])

    def load(refs):
        t = [_split(r[...].astype(F32), b) for r, b in zip(refs[:n_t], in_bounds)]
        p = [r[...] for r in refs[n_t:n_t + n_p]]
        c = [r[...] for r in refs[n_t + n_p:n_t + n_p + n_c]]
        return t, p, c

    def store(refs, values, bounds):
        for ref, slabs, b in zip(refs, values, bounds):
            for v, (a, e) in zip(slabs, _pairs(b)):
                ref[:, a:e] = v.astype(ref.dtype)

    def run_fwd(tiled, params, consts):
        def body(*refs):
            t, p, c = load(refs)
            store(refs[n_t + n_p + n_c:], f(t, p, c), out_bounds)

        return pl.pallas_call(
            body, name=name + "_fwd", grid=grid, in_specs=in_specs,
            out_specs=[row_spec(b[-1]) for b in out_bounds], out_shape=out_shapes,
            compiler_params=_params("parallel"),
        )(*tiled, *params, *consts)

    def run_bwd(tiled, params, consts, cts):
        n_in = n_t + n_p + n_c
        n_o = len(out_bounds)

        def body(*refs):
            t, p, c = load(refs)
            g = [_split(r[...].astype(F32), b) for r, b in zip(refs[n_in:n_in + n_o], out_bounds)]
            _, pull = jax.vjp(lambda t_, p_: f(t_, p_, c), t, p)
            dt, dp = pull(g)
            store(refs[n_in + n_o:n_in + n_o + n_t], dt, in_bounds)
            first = pl.program_id(0) == 0
            for ref, d in zip(refs[n_in + n_o + n_t:], dp):
                @pl.when(first)
                def _(ref=ref):
                    ref[...] = jnp.zeros_like(ref)

                ref[...] += d

        res = pl.pallas_call(
            body, name=name + "_bwd", grid=grid,
            in_specs=in_specs + [row_spec(b[-1]) for b in out_bounds],
            out_specs=[row_spec(t.shape[1]) for t in tiled] + [par_spec(p) for p in params],
            out_shape=[jax.ShapeDtypeStruct(t.shape, d) for t, d in zip(tiled, ct_dtypes)]
                      + [jax.ShapeDtypeStruct(p.shape, F32) for p in params],
            compiler_params=_params("arbitrary"),
        )(*tiled, *params, *consts, *cts)
        return tuple(res[:n_t]), tuple(res[n_t:])

    return run_fwd, run_bwd


def proj_stage(name, f, projections, extra=(), params=(), consts=(), splits=None, ts=256, out_dtypes=None):
    n_z = len(projections)
    forms = [pr[2] for pr in projections]
    names = [pr[3] for pr in projections]
    need_da = [pr[4] for pr in projections]
    store = [pr[5] for pr in projections]
    extra, params, consts = tuple(extra), tuple(params), tuple(consts)

    def matmuls(a_list, w_list):
        return tuple(_matmul(a, w, form, out_dtype=dt, name=n + "_fwd")
                     for a, w, form, n, dt in zip(a_list, w_list, forms, names, store))

    def build(zs, ct=False):
        ct_dtypes = [BF16] * n_z + [e.dtype for e in extra] if ct else None
        return _stage_build(name, f, tuple(zs) + extra, params, consts, splits, ts, out_dtypes, ct_dtypes)

    @jax.custom_vjp
    def op(a_list, w_list, extra, params, consts):
        zs = matmuls(a_list, w_list)
        return tuple(build(zs)[0](zs + extra, params, consts))

    def op_fwd(a_list, w_list, extra, params, consts):
        zs = matmuls(a_list, w_list)
        return tuple(build(zs)[0](zs + extra, params, consts)), (a_list, w_list, zs, extra, params, consts)

    def op_bwd(res, cts):
        a_list, w_list, zs, extra, params, consts = res
        dt, dp = build(zs, ct=True)[1](zs + extra, params, consts, cts)
        da_list, dw_list = [], []
        for a, w, dz, form, n, want in zip(a_list, w_list, dt[:n_z], forms, names, need_da):
            if form == "nn":
                da = _matmul(dz, w, "nt", out_dtype=a.dtype, name=n + "_da") if want else jnp.zeros_like(a)
                dw = _matmul(a, dz, "tn", out_dtype=w.dtype, name=n + "_dw")
            else:
                da = _matmul(dz, w, "nn", out_dtype=a.dtype, name=n + "_da") if want else jnp.zeros_like(a)
                dw = _matmul(dz, a, "tn", out_dtype=w.dtype, name=n + "_dw")
            da_list.append(da)
            dw_list.append(dw)
        return tuple(da_list), tuple(dw_list), tuple(dt[n_z:]), dp, tuple(jnp.zeros_like(c) for c in consts)

    op.defvjp(op_fwd, op_bwd)
    return op(tuple(pr[0] for pr in projections), tuple(pr[1] for pr in projections), extra, params, consts)


def _rms(t, g):
    return t * lax.rsqrt(jnp.mean(t * t, axis=-1, keepdims=True) + RMS_EPS) * g


@functools.partial(jax.custom_vjp, nondiff_argnums=(1,))
def _lane_roll(t, shift):
    return pltpu.roll(t, shift % t.shape[-1], t.ndim - 1)


def _lane_roll_fwd(t, shift):
    return _lane_roll(t, shift), None


def _lane_roll_bwd(shift, _, ct):
    return (pltpu.roll(ct, (-shift) % ct.shape[-1], ct.ndim - 1),)


_lane_roll.defvjp(_lane_roll_fwd, _lane_roll_bwd)


def _rope_lanes(t, tables, half):
    reps = t.shape[1] // tables[0].shape[1]
    c, s_lo, s_hi = [jnp.concatenate([tb] * reps, axis=1) if reps > 1 else tb for tb in tables]
    return t * c + _lane_roll(t, -half) * s_lo + _lane_roll(t, half) * s_hi


PRENORM_TS = 256


def _prenorm_fwd_call(x, g, shards):
    s, width = x.shape
    ts = min(PRENORM_TS, s)
    nt = s // ts
    n_arr = len(shards)

    def body(*refs):
        x_ref, g_ref = refs[:2]
        o_ref = refs[2 + n_arr]
        i = pl.program_id(0)
        ag_start, ag_forward, ag_finish = _allgather_phases(refs[2:2 + n_arr], refs[3 + n_arr:3 + 2 * n_arr],
                                                            *refs[3 + 2 * n_arr:])

        @pl.when(i == 0)
        def _():
            ag_start()

        @pl.when(i == nt // 2)
        def _():
            ag_forward()

        o_ref[...] = _rms(x_ref[...], g_ref[...]).astype(o_ref.dtype)

        @pl.when(i == nt - 1)
        def _():
            ag_finish()

    return pl.pallas_call(
        body, name="prenorm_fwd", grid=(nt,),
        in_specs=[pl.BlockSpec((ts, width), lambda i: (i, 0)), pl.BlockSpec(g.shape, lambda i: (0, 0))]
                 + [HBM_SPEC] * n_arr,
        out_specs=[pl.BlockSpec((ts, width), lambda i: (i, 0))] + [HBM_SPEC] * n_arr,
        out_shape=[jax.ShapeDtypeStruct(x.shape, BF16)] + _allgather_out_shapes(shards),
        scratch_shapes=_allgather_sems(n_arr),
        compiler_params=_params("arbitrary"),
    )(x, g, *shards)


def _prenorm_bwd_call(x, g, dh_a, dh_b, dx_res, parts):
    s, width = x.shape
    ts = min(PRENORM_TS, s)
    nt = s // ts
    n_arr = len(parts)

    def body(*refs):
        x_ref, g_ref, dha_ref, dhb_ref, dxr_ref = refs[:5]
        dx_ref, dg_ref = refs[5 + n_arr:7 + n_arr]
        i = pl.program_id(0)
        exchange_start, exchange_finish = _exchange_chips_phases(
            refs[5:5 + n_arr], refs[7 + n_arr:7 + 2 * n_arr], *refs[7 + 2 * n_arr:])

        @pl.when(i == 0)
        def _():
            exchange_start()
            dg_ref[...] = jnp.zeros_like(dg_ref)

        _, pull = jax.vjp(_rms, x_ref[...], g_ref[...])
        dx, dg = pull(dha_ref[...].astype(F32) + dhb_ref[...].astype(F32))
        dx_ref[...] = dx + dxr_ref[...]
        dg_ref[...] += dg

        @pl.when(i == nt - 1)
        def _():
            exchange_finish()

    row = pl.BlockSpec((ts, width), lambda i: (i, 0))
    par = pl.BlockSpec(g.shape, lambda i: (0, 0))
    return pl.pallas_call(
        body, name="prenorm_bwd", grid=(nt,),
        in_specs=[row, par, row, row, row] + [HBM_SPEC] * n_arr,
        out_specs=[row, par] + [HBM_SPEC] * n_arr,
        out_shape=[jax.ShapeDtypeStruct(x.shape, F32), jax.ShapeDtypeStruct(g.shape, F32)]
                  + [jax.ShapeDtypeStruct(p.shape, p.dtype) for p in parts],
        scratch_shapes=_exchange_chips_sems(n_arr),
        compiler_params=_params("arbitrary"),
    )(x, g, dh_a, dh_b, dx_res, *parts)


@functools.partial(jax.custom_vjp, nondiff_argnums=(3,))
def prenorm_gather(x, g, shards, wire_dtypes):
    out = _prenorm_fwd_call(x, g, [s.astype(d) for s, d in zip(shards, wire_dtypes)])
    return out[0], out[0], x, tuple(out[1:])


def _prenorm_gather_fwd(x, g, shards, wire_dtypes):
    return prenorm_gather(x, g, shards, wire_dtypes), (x, g)


def _prenorm_gather_bwd(wire_dtypes, res, cts):
    x, g = res
    dh_a, dh_b, dx_res, d_gathered = cts
    out = _prenorm_bwd_call(x, g, dh_a, dh_b, dx_res, _reduce_scatter_head(d_gathered, "grads"))
    return out[0], out[1], _reduce_scatter_tail(out[2:], "grads")


prenorm_gather.defvjp(_prenorm_gather_fwd, _prenorm_gather_bwd)


def _f_prep(t, p, c):
    qa, ka, va, cq, ckv, kr = t[0]
    return [[_rope_lanes(qa, c[0:3], A_HEAD_DIM // 2)], [_rope_lanes(ka, c[0:3], A_HEAD_DIM // 2)], [va],
            [_rms(cq, p[0])], [_rms(ckv, p[1])], [_rope_lanes(kr, c[3:6], ROPE_DIM // 2)]]


def _f_qrope(t, p, c):
    return [[_rope_lanes(t[0][0], c, ROPE_DIM // 2)]]


def _f_kv(t, p, c):
    (k_nope, v), (k_pe,) = t
    return [[k_nope + jnp.concatenate([k_pe] * B_HEADS, axis=1)], [v]]


def _f_gate(t, p, c):
    (ga, gb), (pa,), (pb,) = t
    ba, bb = p
    return [[jax.nn.sigmoid(ga + ba) * pa + jax.nn.sigmoid(gb + bb) * pb]]


def _f_post(t, p, c):
    (branch,), (residual,) = t
    x1 = residual + _rms(branch, p[0])
    return [[x1], [_rms(x1, p[1])]]


def _f_out(t, p, c):
    (gate,), (emb,), (x2,) = t
    y = x2 + jax.nn.sigmoid(gate) * emb
    err = y - c[0]
    return [[0.5 * jnp.mean(err * err, axis=-1, keepdims=True)]]


def _shift_down(cur, prev, has_prev):
    full = jnp.concatenate([prev * has_prev, cur], axis=0)
    return pltpu.roll(full, 1, 0)[HALO:], pltpu.roll(full, 2, 0)[HALO:]


GELU_C = float(np.sqrt(2.0 / np.pi))
GELU_A = 0.044715
HALO = 8


def _gelu_tanh(x):
    x2 = x * x
    th = jnp.tanh(x * (GELU_C + (GELU_C * GELU_A) * x2))
    half = 0.5 + 0.5 * th
    return x * half, half + x * (0.5 - 0.5 * (th * th)) * (GELU_C + (3.0 * GELU_C * GELU_A) * x2)


def _row_sum(t):
    return jnp.sum(t, axis=0, keepdims=True)


def _conv3(cur, prev, w_ref, b_ref, has_prev):
    u1, u2 = _shift_down(cur, prev, has_prev)
    return w_ref[2:3, :] * cur + w_ref[1:2, :] * u1 + w_ref[0:1, :] * u2 + b_ref[...], u1, u2


def _mlp_act_specs(s):
    ts = min(CONV_TS, s)
    hb = ts // HALO

    def half_specs(h):
        return [pl.BlockSpec((ts, D_FF), lambda i: (i, h)),
                pl.BlockSpec((HALO, D_FF), lambda i: (jnp.maximum(i * hb - 1, 0), h))]

    def par_specs(h):
        return [pl.BlockSpec((CONV_W, D_FF), lambda i: (0, h)), pl.BlockSpec((1, D_FF), lambda i: (0, h))]

    return ts, hb, half_specs, par_specs


def _mlp_act_fwd_call(up, conv_w, conv_b):
    s = up.shape[0]
    ts, hb, half_specs, par_specs = _mlp_act_specs(s)

    def body(g_ref, gp_ref, v_ref, vp_ref, wg_ref, bg_ref, wv_ref, bv_ref, o_ref):
        has_prev = (pl.program_id(0) > 0).astype(F32)

        def chunk(cidx, carry):
            cols = pl.ds(pl.multiple_of(cidx * CONV_CHUNK, CONV_CHUNK), CONV_CHUNK)
            u_g, _, _ = _conv3(g_ref[:, cols], gp_ref[:, cols], wg_ref.at[:, cols], bg_ref.at[:, cols], has_prev)
            u_v, _, _ = _conv3(v_ref[:, cols], vp_ref[:, cols], wv_ref.at[:, cols], bv_ref.at[:, cols], has_prev)
            o_ref[:, cols] = (_gelu_tanh(u_g)[0] * u_v).astype(o_ref.dtype)
            return carry

        lax.fori_loop(0, D_FF // CONV_CHUNK, chunk, 0)

    return pl.pallas_call(
        body, name="mlp_act_fwd", grid=(s // ts,),
        in_specs=half_specs(0) + half_specs(1) + par_specs(0) + par_specs(1),
        out_specs=pl.BlockSpec((ts, D_FF), lambda i: (i, 0)),
        out_shape=jax.ShapeDtypeStruct((s, D_FF), BF16),
        compiler_params=_params("parallel"),
    )(up, up, up, up, conv_w, conv_b, conv_w, conv_b)


def _mlp_act_bwd_call(up, conv_w, conv_b, dact):
    s = up.shape[0]
    ts, hb, half_specs, par_specs = _mlp_act_specs(s)
    nt = s // ts
    ext = ts + HALO
    bf16_rows = 2 * HALO

    def next_spec(rows, h):
        return pl.BlockSpec((rows, D_FF), lambda i: (jnp.minimum((i + 1) * (ts // rows), s // rows - 1), h))

    def body(g_ref, gp_ref, gn_ref, v_ref, vp_ref, vn_ref, wg_ref, bg_ref, wv_ref, bv_ref, da_ref, dan_ref,
             dup_ref, dwg_ref, dbg_ref, dwv_ref, dbv_ref):
        i = pl.program_id(0)
        has_prev, has_next = (i > 0).astype(F32), (i < nt - 1).astype(F32)

        @pl.when(i == 0)
        def _():
            for ref in (dwg_ref, dbg_ref, dwv_ref, dbv_ref):
                ref[...] = jnp.zeros_like(ref)

        def chunk(cidx, carry):
            cols = pl.ds(pl.multiple_of(cidx * CONV_CHUNK, CONV_CHUNK), CONV_CHUNK)
            g_ext = jnp.concatenate([g_ref[:, cols], gn_ref[:, cols]], axis=0)
            v_ext = jnp.concatenate([v_ref[:, cols], vn_ref[:, cols]], axis=0)
            u_g, g1, g2 = _conv3(g_ext, gp_ref[:, cols], wg_ref.at[:, cols], bg_ref.at[:, cols], has_prev)
            u_v, v1, v2 = _conv3(v_ext, vp_ref[:, cols], wv_ref.at[:, cols], bv_ref.at[:, cols], has_prev)
            da_ext = jnp.concatenate([da_ref[:, cols].astype(F32),
                                      dan_ref[:, cols].astype(F32)[0:HALO] * has_next], axis=0)
            act_g, dact_g = _gelu_tanh(u_g)
            du_g = da_ext * u_v * dact_g
            du_v = da_ext * act_g
            for du, w_ref, x0, x1, x2, dw_ref, db_ref, lo in ((du_g, wg_ref, g_ext, g1, g2, dwg_ref, dbg_ref, 0),
                                                          (du_v, wv_ref, v_ext, v1, v2, dwv_ref, dbv_ref, D_FF)):
                d1 = pltpu.roll(du, ext - 1, 0)
                d2 = pltpu.roll(du, ext - 2, 0)
                dup = w_ref[2:3, cols] * du + w_ref[1:2, cols] * d1 + w_ref[0:1, cols] * d2
                out_cols = pl.ds(pl.multiple_of(lo + cidx * CONV_CHUNK, CONV_CHUNK), CONV_CHUNK)
                dup_ref[:, out_cols] = dup[0:ts].astype(dup_ref.dtype)
                own = du[0:ts]
                dw_ref[0:1, cols] += _row_sum(own * x2[0:ts])
                dw_ref[1:2, cols] += _row_sum(own * x1[0:ts])
                dw_ref[2:3, cols] += _row_sum(own * x0[0:ts])
                db_ref[:, cols] += _row_sum(own)
            return carry

        lax.fori_loop(0, D_FF // CONV_CHUNK, chunk, 0)

    par_out = [pl.BlockSpec((CONV_W, D_FF), lambda i: (0, 0)), pl.BlockSpec((1, D_FF), lambda i: (0, 0))]
    par_shapes = [jax.ShapeDtypeStruct((CONV_W, D_FF), F32), jax.ShapeDtypeStruct((1, D_FF), F32)]
    return pl.pallas_call(
        body, name="mlp_act_bwd", grid=(nt,),
        in_specs=(half_specs(0) + [next_spec(HALO, 0)] + half_specs(1) + [next_spec(HALO, 1)]
                  + par_specs(0) + par_specs(1)
                  + [pl.BlockSpec((ts, D_FF), lambda i: (i, 0)), next_spec(bf16_rows, 0)]),
        out_specs=[pl.BlockSpec((ts, 2 * D_FF), lambda i: (i, 0))] + par_out + par_out,
        out_shape=[jax.ShapeDtypeStruct((s, 2 * D_FF), BF16)] + par_shapes + par_shapes,
        compiler_params=_params("arbitrary"),
    )(up, up, up, up, up, up, conv_w, conv_b, conv_w, conv_b, dact, dact)


@jax.custom_vjp
def mlp_up(h2, w_up_t, conv_w, conv_b):
    return _mlp_act_fwd_call(_matmul(h2, w_up_t, "nt", out_dtype=F32, name="w_up_fwd"), conv_w, conv_b)


def _mlp_up_fwd(h2, w_up_t, conv_w, conv_b):
    up = _matmul(h2, w_up_t, "nt", out_dtype=F32, name="w_up_fwd")
    return _mlp_act_fwd_call(up, conv_w, conv_b), (h2, w_up_t, up, conv_w, conv_b)


def _mlp_up_bwd(res, dact):
    h2, w_up_t, up, conv_w, conv_b = res
    dup, dwg, dbg, dwv, dbv = _mlp_act_bwd_call(up, conv_w, conv_b, dact)
    dh2 = _matmul(dup, w_up_t, "nn", out_dtype=h2.dtype, name="w_up_da")
    dw = _matmul(dup, h2, "tn", out_dtype=w_up_t.dtype, name="w_up_dw")
    return dh2, dw, jnp.concatenate([dwg, dwv], axis=1), jnp.concatenate([dbg, dbv], axis=1)


mlp_up.defvjp(_mlp_up_fwd, _mlp_up_bwd)


SWA_ROWS = A_GROUP * SWA_BLOCK


def _swa_sink_rows(sink_ref, g):
    return jnp.concatenate([jnp.full((SWA_BLOCK, 1), sink_ref[g * A_GROUP + h], F32) for h in range(A_GROUP)], axis=0)


def _swa_operands(q_ref, kp_ref, kc_ref, vp_ref, vc_ref, sink_ref):
    groups = []
    for g in range(A_KV_HEADS):
        groups.append((_swa_stack_heads(q_ref, g), _dup_half(kp_ref[...], g), _dup_half(kc_ref[...], g),
                       _dup_half(vp_ref[...], g), _dup_half(vc_ref[...], g)))
    return groups, jnp.concatenate([_swa_sink_rows(sink_ref, g) for g in range(A_KV_HEADS)], axis=0)


def _swa_probs(groups, sink, prev_off):
    scale = A_HEAD_DIM ** -0.5
    sp = jnp.concatenate([lax.dot_general(gr[0], gr[1], NT_DIMS, preferred_element_type=F32) for gr in groups], axis=0)
    sc = jnp.concatenate([lax.dot_general(gr[0], gr[2], NT_DIMS, preferred_element_type=F32) for gr in groups], axis=0)
    qi = lax.broadcasted_iota(jnp.int32, sp.shape, 0) & (SWA_BLOCK - 1)
    kj = lax.broadcasted_iota(jnp.int32, sp.shape, 1)
    in_cur = kj <= qi
    sw = jnp.where(in_cur, sc, jnp.where(kj > qi + prev_off, sp, -jnp.inf)) * scale
    m = jnp.maximum(jnp.max(sw, axis=-1, keepdims=True), sink)
    e, es = jnp.exp(sw - m), jnp.exp(sink - m)
    den = jnp.sum(e, axis=-1, keepdims=True) + es
    return e / den, in_cur, es / den


def _swa_split(t, in_cur):
    cur = jnp.where(in_cur, t, 0.0)
    return t - cur, cur


MLA_SCALE = (NOPE_DIM + ROPE_DIM) ** -0.5
EXP2_SCALE = MLA_SCALE * float(np.log2(np.e))
NT_DIMS = (((1,), (1,)), ((), ()))
TN_DIMS = (((0,), (0,)), ((), ()))


LANES = 128
HALF = LANES // 2


def _low_half(shape):
    return lax.broadcasted_iota(jnp.int32, shape, len(shape) - 1) < HALF


def _dup_half(x, g):
    xf = x.astype(F32)
    keep = _low_half(xf.shape) if g == 0 else jnp.logical_not(_low_half(xf.shape))
    xm = jnp.where(keep, xf, 0.0)
    return (xm + pltpu.roll(xm, HALF, 1)).astype(x.dtype)


def _fold_half(r, g):
    total = r + pltpu.roll(r, HALF, 1)
    keep = _low_half(r.shape) if g == 0 else jnp.logical_not(_low_half(r.shape))
    return jnp.where(keep, total, 0.0)


def _swa_stack_heads(ref, g):
    parts = []
    for tile in range(2):
        slab = ref[:, (2 * g + tile) * LANES:(2 * g + tile + 1) * LANES]
        low = _low_half(slab.shape)
        parts += [jnp.where(low, slab, jnp.zeros_like(slab)), jnp.where(low, jnp.zeros_like(slab), slab)]
    return jnp.concatenate(parts, axis=0)


def _swa_unstack_heads(ref, g, rows):
    for tile in range(2):
        a = rows[(2 * tile) * SWA_BLOCK:(2 * tile + 1) * SWA_BLOCK]
        b = rows[(2 * tile + 1) * SWA_BLOCK:(2 * tile + 2) * SWA_BLOCK]
        ref[:, (2 * g + tile) * LANES:(2 * g + tile + 1) * LANES] = jnp.where(_low_half(a.shape), a, b).astype(ref.dtype)


def _swa_nat_specs():
    blk = SWA_BLOCK
    q_spec = pl.BlockSpec((blk, A_HEADS * A_HEAD_DIM), lambda n: (n, 0))
    prev_spec = pl.BlockSpec((blk, LANES), lambda n: (jnp.maximum(n - 1, 0), 0))
    cur_spec = pl.BlockSpec((blk, LANES), lambda n: (n, 0))
    return q_spec, prev_spec, cur_spec, pl.BlockSpec(memory_space=pltpu.SMEM)


def _swa_nat_fwd_call(q, k, v, sinks, shards):
    s = q.shape[0]
    nblk = s // SWA_BLOCK
    n_arr = len(shards)
    q_spec, prev_spec, cur_spec, sink_spec = _swa_nat_specs()

    def body(*refs):
        q_ref, kp_ref, kc_ref, vp_ref, vc_ref, sink_ref = refs[:6]
        o_ref = refs[6 + n_arr]
        n = pl.program_id(0)
        ag_start, ag_forward, ag_finish = _allgather_phases(refs[6:6 + n_arr], refs[7 + n_arr:7 + 2 * n_arr],
                                                            *refs[7 + 2 * n_arr:])

        @pl.when(n == 0)
        def _():
            ag_start()

        @pl.when(n == nblk // 2)
        def _():
            ag_forward()

        prev_off = jnp.where(n > 0, 0, SWA_BLOCK)
        groups, sink = _swa_operands(q_ref, kp_ref, kc_ref, vp_ref, vc_ref, sink_ref)
        p, in_cur, _ = _swa_probs(groups, sink, prev_off)
        ppb, pcb = [t.astype(BF16) for t in _swa_split(p, in_cur)]
        for g, (_, _, _, vp, vc) in enumerate(groups):
            rows = slice(g * SWA_ROWS, (g + 1) * SWA_ROWS)
            out = (jnp.dot(ppb[rows], vp, preferred_element_type=F32)
                   + jnp.dot(pcb[rows], vc, preferred_element_type=F32))
            _swa_unstack_heads(o_ref, g, out)

        @pl.when(n == nblk - 1)
        def _():
            ag_finish()

    return pl.pallas_call(
        body, name="swa_fwd", grid=(nblk,),
        in_specs=[q_spec, prev_spec, cur_spec, prev_spec, cur_spec, sink_spec] + [HBM_SPEC] * n_arr,
        out_specs=[q_spec] + [HBM_SPEC] * n_arr,
        out_shape=[jax.ShapeDtypeStruct(q.shape, BF16)] + _allgather_out_shapes(shards),
        scratch_shapes=_allgather_sems(n_arr),
        compiler_params=_params("arbitrary"),
    )(q, k, k, v, v, sinks, *shards)


def _swa_nat_bwd_call(q, k, v, sinks, do, parts):
    s = q.shape[0]
    nblk = s // SWA_BLOCK
    n_arr = len(parts)
    q_spec, prev_spec, cur_spec, sink_spec = _swa_nat_specs()
    scale = A_HEAD_DIM ** -0.5
    dsink_spec = pl.BlockSpec((A_KV_HEADS, SWA_ROWS, 1), lambda n: (0, 0, 0))

    def body(*refs):
        q_ref, kp_ref, kc_ref, vp_ref, vc_ref, sink_ref, do_ref = refs[:7]
        dq_ref, dkp_ref, dkc_ref, dvp_ref, dvc_ref, dsink_ref = refs[7 + n_arr:13 + n_arr]
        n = pl.program_id(0)
        exchange_start, exchange_finish = _exchange_chips_phases(
            refs[7:7 + n_arr], refs[13 + n_arr:13 + 2 * n_arr], *refs[13 + 2 * n_arr:])

        @pl.when(n == 0)
        def _():
            exchange_start()
        prev_off = jnp.where(n > 0, 0, SWA_BLOCK)

        @pl.when(n == 0)
        def _():
            dsink_ref[...] = jnp.zeros_like(dsink_ref)

        groups, sink = _swa_operands(q_ref, kp_ref, kc_ref, vp_ref, vc_ref, sink_ref)
        dobs = [_swa_stack_heads(do_ref, g) for g in range(A_KV_HEADS)]
        p, in_cur, ps = _swa_probs(groups, sink, prev_off)
        ppb, pcb = [t.astype(BF16) for t in _swa_split(p, in_cur)]

        def per_group(fn):
            return jnp.concatenate([fn(g, slice(g * SWA_ROWS, (g + 1) * SWA_ROWS)) for g in range(A_KV_HEADS)], axis=0)

        out = per_group(lambda g, rows: jnp.dot(ppb[rows], groups[g][3], preferred_element_type=F32)
                        + jnp.dot(pcb[rows], groups[g][4], preferred_element_type=F32))
        delta = jnp.sum(jnp.concatenate(dobs, axis=0).astype(F32) * out, axis=-1, keepdims=True)
        dp = jnp.where(in_cur,
                       per_group(lambda g, rows: lax.dot_general(dobs[g], groups[g][4], NT_DIMS,
                                                                 preferred_element_type=F32)),
                       per_group(lambda g, rows: lax.dot_general(dobs[g], groups[g][3], NT_DIMS,
                                                                 preferred_element_type=F32)))
        dsp, dsc = [t.astype(BF16) for t in _swa_split(p * (dp - delta), in_cur)]
        dsink_ref[...] += (-ps * delta).reshape(dsink_ref.shape)
        totals = [jnp.zeros((SWA_BLOCK, LANES), F32) for _ in range(4)]
        for g, (qb, kp, kc, _, _) in enumerate(groups):
            rows = slice(g * SWA_ROWS, (g + 1) * SWA_ROWS)
            dq = (jnp.dot(dsp[rows], kp, preferred_element_type=F32)
                  + jnp.dot(dsc[rows], kc, preferred_element_type=F32)) * scale
            _swa_unstack_heads(dq_ref, g, dq)
            pieces = [lax.dot_general(dsp[rows], qb, TN_DIMS, preferred_element_type=F32) * scale,
                      lax.dot_general(dsc[rows], qb, TN_DIMS, preferred_element_type=F32) * scale,
                      lax.dot_general(ppb[rows], dobs[g], TN_DIMS, preferred_element_type=F32),
                      lax.dot_general(pcb[rows], dobs[g], TN_DIMS, preferred_element_type=F32)]
            totals = [tot + _fold_half(r, g) for tot, r in zip(totals, pieces)]
        dkp_ref[...], dkc_ref[...], dvp_ref[...], dvc_ref[...] = totals

        @pl.when(n == nblk - 1)
        def _():
            exchange_finish()

    kv_shape = jax.ShapeDtypeStruct(k.shape, F32)
    return pl.pallas_call(
        body, name="swa_bwd", grid=(nblk,),
        in_specs=[q_spec, prev_spec, cur_spec, prev_spec, cur_spec, sink_spec, q_spec] + [HBM_SPEC] * n_arr,
        out_specs=[q_spec, cur_spec, cur_spec, cur_spec, cur_spec, dsink_spec] + [HBM_SPEC] * n_arr,
        out_shape=[jax.ShapeDtypeStruct(q.shape, q.dtype), kv_shape, kv_shape, kv_shape, kv_shape,
                   jax.ShapeDtypeStruct((A_KV_HEADS, SWA_ROWS, 1), F32)]
                  + [jax.ShapeDtypeStruct(p.shape, p.dtype) for p in parts],
        scratch_shapes=_exchange_chips_sems(n_arr),
        compiler_params=_params("arbitrary"),
    )(q, k, k, v, v, sinks, do, *parts)


@jax.custom_vjp
def swa_nat(q, k, v, sinks, shards):
    out = _swa_nat_fwd_call(q, k, v, sinks, [s.astype(BF16) for s in shards])
    return out[0], tuple(out[1:])


def _swa_nat_fwd(q, k, v, sinks, shards):
    out = _swa_nat_fwd_call(q, k, v, sinks, [s.astype(BF16) for s in shards])
    return (out[0], tuple(out[1:])), (q, k, v, sinks)


def _swa_nat_bwd(res, cts):
    q, k, v, sinks = res
    do, d_gathered = cts
    out = _swa_nat_bwd_call(q, k, v, sinks, do, _reduce_scatter_head(d_gathered, "mid_grads"))
    dq, dkp, dkc, dvp, dvc, dsink = out[:6]

    def fold(prev_part, cur_part):
        shifted = jnp.concatenate([prev_part[SWA_BLOCK:], jnp.zeros_like(prev_part[:SWA_BLOCK])], axis=0)
        return (cur_part + shifted).astype(k.dtype)

    dsinks = jnp.sum(dsink.reshape(A_HEADS, SWA_BLOCK), axis=1)
    return dq, fold(dkp, dkc), fold(dvp, dvc), dsinks, _reduce_scatter_tail(out[6:], "mid_grads")


swa_nat.defvjp(_swa_nat_fwd, _swa_nat_bwd)

N_PAIR = B_HEADS // 2


def _flash_nat_fwd_call(q, k, v, shards):
    s = q.shape[0]
    t = min(FLASH_T, s)
    nb = s // t
    d = LANES
    n_arr = len(shards)

    def body(*refs):
        q_ref, k_ref, v_ref = refs[:3]
        shard_refs = refs[3:3 + n_arr]
        o_ref, lse_ref = refs[3 + n_arr:5 + n_arr]
        gathered_refs = refs[5 + n_arr:5 + 2 * n_arr]
        vt_ref, m_ref, l_ref, acc_ref = refs[5 + 2 * n_arr:9 + 2 * n_arr]
        pair, i = pl.program_id(0), pl.program_id(1)
        ag_start, ag_forward, ag_finish = _allgather_phases(shard_refs, gathered_refs, *refs[9 + 2 * n_arr:])

        @pl.when((pair == 0) & (i == 0))
        def _():
            ag_start()

        @pl.when((pair == N_PAIR // 2) & (i == 0))
        def _():
            ag_forward()

        @pl.when(i == 0)
        def _():
            for hh in range(2):
                for chunk in range(nb):
                    rows = slice(chunk * t, (chunk + 1) * t)
                    vt_ref[hh, :, rows] = v_ref[rows, hh * d:(hh + 1) * d].T

        m_ref[...] = jnp.full_like(m_ref, -jnp.inf)
        l_ref[...] = jnp.zeros_like(l_ref)
        acc_ref[...] = jnp.zeros_like(acc_ref)

        def step(j, on_diagonal):
            keys = pl.ds(pl.multiple_of(j * t, t), t)
            scores = [lax.dot_general(k_ref[keys, hh * d:(hh + 1) * d], q_ref[:, hh * d:(hh + 1) * d], NT_DIMS,
                                      preferred_element_type=F32) for hh in range(2)]
            for hh in range(2):
                sc_t = scores[hh]
                if on_diagonal:
                    key = lax.broadcasted_iota(jnp.int32, (t, t), 0)
                    qry = lax.broadcasted_iota(jnp.int32, (t, t), 1)
                    sc_t = jnp.where(qry >= key, sc_t, -jnp.inf)
                m_old = m_ref[hh]
                m_new = jnp.maximum(m_old, jnp.max(sc_t, axis=0, keepdims=True))
                alpha = jnp.exp2((m_old - m_new) * EXP2_SCALE)
                p_t = jnp.exp2((sc_t - m_new) * EXP2_SCALE)
                l_ref[hh] = alpha * l_ref[hh] + jnp.sum(p_t, axis=0, keepdims=True)
                acc_ref[hh] = alpha * acc_ref[hh] + jnp.dot(vt_ref[hh, :, keys], p_t.astype(BF16),
                                                            preferred_element_type=F32)
                m_ref[hh] = m_new

        def below(j, carry):
            step(j, False)
            return carry

        lax.fori_loop(0, i, below, 0)
        step(i, True)
        outs =[(acc_ref[hh] / l_ref[hh]).T for hh in range(2)]
        for hh in range(2):
            lse_ref[hh] = m_ref[hh] * EXP2_SCALE + jnp.log2(l_ref[hh])
        o_ref[...] = (outs[0] + pltpu.roll(outs[1], HALF, 1)).astype(o_ref.dtype)

        @pl.when((pair == N_PAIR - 1) & (i == nb - 1))
        def _():
            ag_finish()

    return pl.pallas_call(
        body, name="mla_fwd", grid=(N_PAIR, nb),
        in_specs=[pl.BlockSpec((t, 2 * d), lambda p, i: (i, p)),
                  pl.BlockSpec((s, 2 * d), lambda p, i: (0, p)),
                  pl.BlockSpec((s, 2 * d), lambda p, i: (0, p))] + [HBM_SPEC] * n_arr,
        out_specs=[pl.BlockSpec((t, d), lambda p, i: (i, p)),
                   pl.BlockSpec((2, 1, t), lambda p, i: (p, 0, i))] + [HBM_SPEC] * n_arr,
        out_shape=[jax.ShapeDtypeStruct((s, N_PAIR * d), BF16), jax.ShapeDtypeStruct((B_HEADS, 1, s), F32)]
                  + _allgather_out_shapes(shards),
        scratch_shapes=[pltpu.VMEM((2, d, s), BF16), pltpu.VMEM((2, 1, t), F32), pltpu.VMEM((2, 1, t), F32),
                        pltpu.VMEM((2, d, t), F32)] + _allgather_sems(n_arr),
        compiler_params=_params("arbitrary", "arbitrary"),
    )(q, k, v, *shards)


def _flash_nat_delta_call(o, do):
    s, w = o.shape
    t = min(FLASH_T, s)

    def body(o_ref, do_ref, out_ref):
        prod = o_ref[...].astype(F32) * do_ref[...].astype(F32)
        lane = lax.broadcasted_iota(jnp.int32, (w, LANES), 0) // V_DIM
        head = lax.broadcasted_iota(jnp.int32, (w, LANES), 1)
        out_ref[...] = jnp.dot(prod, (lane == head).astype(F32), precision=lax.Precision.HIGHEST,
                               preferred_element_type=F32)

    spec = pl.BlockSpec((t, w), lambda i: (i, 0))
    return pl.pallas_call(
        body, name="mla_delta", grid=(s // t,), in_specs=[spec, spec],
        out_specs=pl.BlockSpec((t, LANES), lambda i: (i, 0)),
        out_shape=jax.ShapeDtypeStruct((s, LANES), F32), compiler_params=_params("parallel"),
    )(o, do)


def _flash_nat_bwd_call(q, k, v, lse_row, delta_row, do, parts):
    s = q.shape[0]
    t = min(FLASH_T, s)
    nb = s // t
    d = LANES
    n_arr = len(parts)

    def body(*refs):
        q_ref, k_ref, v_ref, lse_ref, delta_ref, do_ref = refs[:6]
        part_refs = refs[6:6 + n_arr]
        dq_ref, dk_ref, dv_ref = refs[6 + n_arr:9 + n_arr]
        received_refs = refs[9 + n_arr:9 + 2 * n_arr]
        dq_acc, dk_acc, dv_acc = refs[9 + 2 * n_arr:12 + 2 * n_arr]
        pair, j = pl.program_id(0), pl.program_id(1)
        exchange_start, exchange_finish = _exchange_chips_phases(part_refs, received_refs, *refs[12 + 2 * n_arr:])

        @pl.when((pair == 0) & (j == 0))
        def _():
            exchange_start()

        @pl.when(j == 0)
        def _():
            dq_acc[...] = jnp.zeros_like(dq_acc)

        for hh in range(2):
            kb, vb = k_ref[:, hh * d:(hh + 1) * d], v_ref[:, hh * d:(hh + 1) * d]
            dk_acc[...] = jnp.zeros_like(dk_acc)
            dv_acc[...] = jnp.zeros_like(dv_acc)

            def step(i, on_diagonal, hh=hh, kb=kb, vb=vb):
                rows = pl.ds(pl.multiple_of(i * t, t), t)
                qb = q_ref[rows, hh * d:(hh + 1) * d]
                do_pair = do_ref[rows, :].astype(F32)
                do_h = do_pair if hh == 0 else pltpu.roll(do_pair, HALF, 1)
                dob = jnp.where(_low_half(do_h.shape), do_h, 0.0).astype(BF16)
                sc_t = lax.dot_general(kb, qb, NT_DIMS, preferred_element_type=F32)
                p_t = jnp.exp2(sc_t * EXP2_SCALE - lse_ref[hh, :, rows])
                if on_diagonal:
                    key = lax.broadcasted_iota(jnp.int32, (t, t), 0)
                    qry = lax.broadcasted_iota(jnp.int32, (t, t), 1)
                    p_t = jnp.where(qry >= key, p_t, 0.0)
                dp_t = lax.dot_general(vb, dob, NT_DIMS, preferred_element_type=F32)
                ds_t = (p_t * (dp_t - delta_ref[hh, :, rows])).astype(BF16)
                dv_acc[...] += jnp.dot(p_t.astype(BF16), dob, preferred_element_type=F32)
                dk_acc[...] += jnp.dot(ds_t, qb, preferred_element_type=F32)
                dq_acc[hh, rows, :] += lax.dot_general(ds_t, kb, TN_DIMS, preferred_element_type=F32)

            def above(i, carry, step=step):
                step(i, False)
                return carry

            step(j, True)
            lax.fori_loop(j + 1, nb, above, 0)
            dk_ref[:, hh * d:(hh + 1) * d] = (dk_acc[...] * MLA_SCALE).astype(dk_ref.dtype)
            dv_ref[:, hh * d:(hh + 1) * d] = dv_acc[...].astype(dv_ref.dtype)

        @pl.when(j == nb - 1)
        def _():
            for hh in range(2):
                dq_ref[:, hh * d:(hh + 1) * d] = (dq_acc[hh] * MLA_SCALE).astype(dq_ref.dtype)

        @pl.when((pair == N_PAIR - 1) & (j == nb - 1))
        def _():
            exchange_finish()

    full_spec = pl.BlockSpec((s, 2 * d), lambda p, j: (0, p))
    tile_spec = pl.BlockSpec((t, 2 * d), lambda p, j: (j, p))
    row_spec = pl.BlockSpec((2, 1, s), lambda p, j: (p, 0, 0))
    return pl.pallas_call(
        body, name="mla_bwd", grid=(N_PAIR, nb),
        in_specs=[full_spec, tile_spec, tile_spec, row_spec, row_spec, pl.BlockSpec((s, d), lambda p, j: (0, p))]
                 + [HBM_SPEC] * n_arr,
        out_specs=[full_spec, tile_spec, tile_spec] + [HBM_SPEC] * n_arr,
        out_shape=[jax.ShapeDtypeStruct(q.shape, q.dtype)] * 3 + [jax.ShapeDtypeStruct(p.shape, p.dtype) for p in parts],
        scratch_shapes=[pltpu.VMEM((2, s, d), F32), pltpu.VMEM((t, d), F32), pltpu.VMEM((t, d), F32)]
                       + _exchange_chips_sems(n_arr),
        compiler_params=_params("arbitrary", "arbitrary"),
    )(q, k, v, lse_row, delta_row, do, *parts)


def _reduce_scatter_head(cts, tag):
    received = _exchange_sibling(list(cts), tag + "_exchange_sibling")
    my_c = lax.axis_index("c").astype(jnp.int32).reshape(1)
    return [_pair_add(m, r, my_c, "%s_pair_add_%d" % (tag, i)) for i, (m, r) in enumerate(zip(cts, received))]


def _reduce_scatter_tail(chip_parts, tag):
    return tuple(_sum_blocks(r, "%s_sum_%d" % (tag, i)) for i, r in enumerate(chip_parts))


@jax.custom_vjp
def flash_nat(q, k, v, shards):
    out = _flash_nat_fwd_call(q, k, v, [s.astype(BF16) for s in shards])
    return out[0], tuple(out[2:])


def _flash_nat_fwd(q, k, v, shards):
    out = _flash_nat_fwd_call(q, k, v, [s.astype(BF16) for s in shards])
    return (out[0], tuple(out[2:])), (q, k, v, out[0], out[1])


def _flash_nat_bwd(res, cts):
    q, k, v, o, lse = res
    do, d_gathered = cts
    delta = _flash_nat_delta_call(o, do)[:, :B_HEADS].T.reshape(B_HEADS, 1, q.shape[0])
    out = _flash_nat_bwd_call(q, k, v, lse, delta, do, _reduce_scatter_head(d_gathered, "mlp_grads"))
    return out[0], out[1], out[2], _reduce_scatter_tail(out[3:], "mlp_grads")


flash_nat.defvjp(_flash_nat_fwd, _flash_nat_bwd)


HBM_SPEC = pl.BlockSpec(memory_space=pltpu.HBM)


def _allgather(shards, name):
    n_arr = len(shards)

    def body(*refs):
        start, forward, finish = _allgather_phases(refs[:n_arr], refs[n_arr:2 * n_arr], *refs[2 * n_arr:])
        start()
        forward()
        finish()

    return pl.pallas_call(
        body, name=name, out_shape=_allgather_out_shapes(shards),
        in_specs=[HBM_SPEC] * n_arr, out_specs=[HBM_SPEC] * n_arr,
        scratch_shapes=_allgather_sems(n_arr),
    )(*shards)


def _allgather_out_shapes(shards):
    return [jax.ShapeDtypeStruct((N_DEV,) + s.shape, s.dtype) for s in shards]


def _allgather_sems(n_arr):
    return [pltpu.SemaphoreType.DMA((7, n_arr)), pltpu.SemaphoreType.DMA((7, n_arr)), pltpu.SemaphoreType.DMA((n_arr,))]


def _allgather_phases(x_refs, out_refs, send_sems, recv_sems, local_sems):
    arrays = range(len(x_refs))
    x, y, c = lax.axis_index("x"), lax.axis_index("y"), lax.axis_index("c")
    me, sibling = (x, y, c), (x, y, 1 - c)
    chips = [(1 - x, y), (x, 1 - y), (1 - x, 1 - y)]

    def rows(a, px, py, pc):
        return out_refs[a].at[4 * px + 2 * py + pc]

    def copy(a, k, block, to, src=None):
        return pltpu.make_async_remote_copy(
            src_ref=rows(a, *block) if src is None else src, dst_ref=rows(a, *block),
            send_sem=send_sems.at[k, a], recv_sem=recv_sems.at[k, a], device_id=to, device_id_type=MESH_ID)

    def mine():
        return [pltpu.make_async_copy(x_refs[a], rows(a, *me), local_sems.at[a]) for a in arrays]

    def first():
        return [cp for a in arrays for cp in
                [copy(a, 0, me, sibling, src=x_refs[a])]
                + [copy(a, 1 + j, me, (*chip, c), src=x_refs[a]) for j, chip in enumerate(chips)]]

    def passed():
        return [copy(a, 4 + j, (*chip, c), sibling) for j, chip in enumerate(chips) for a in arrays]

    def start():
        for cp in mine() + first():
            cp.start()

    def forward():
        for j, chip in enumerate(chips):
            for a in arrays:
                copy(a, 1 + j, (*chip, c), me).wait_recv()
                copy(a, 4 + j, (*chip, c), sibling).start()

    def finish():
        for a in arrays:
            copy(a, 0, sibling, me).wait_recv()
        for j, chip in enumerate(chips):
            for a in arrays:
                copy(a, 4 + j, (*chip, 1 - c), me).wait_recv()
        for cp in first() + passed():
            cp.wait_send()
        for cp in mine():
            cp.wait()

    return start, forward, finish


N_CHIP = 4


def _exchange_sibling(parts, name):
    n_arr = len(parts)

    def body(*refs):
        in_refs, recv_refs = refs[:n_arr], refs[n_arr:2 * n_arr]
        send_sems, recv_sems = refs[2 * n_arr:]
        x, y, c = lax.axis_index("x"), lax.axis_index("y"), lax.axis_index("c")
        copies = []
        for a in range(n_arr):
            for q in range(N_CHIP):
                copies.append(pltpu.make_async_remote_copy(
                    src_ref=in_refs[a].at[2 * q + 1 - c], dst_ref=recv_refs[a].at[q],
                    send_sem=send_sems.at[q, a], recv_sem=recv_sems.at[q, a],
                    device_id=(x, y, 1 - c), device_id_type=MESH_ID))
        for cp in copies:
            cp.start()
        for cp in copies:
            cp.wait()

    return pl.pallas_call(
        body, name=name, out_shape=[jax.ShapeDtypeStruct((N_CHIP,) + p.shape[1:], p.dtype) for p in parts],
        in_specs=[HBM_SPEC] * n_arr, out_specs=[HBM_SPEC] * n_arr,
        scratch_shapes=[pltpu.SemaphoreType.DMA((N_CHIP, n_arr)), pltpu.SemaphoreType.DMA((N_CHIP, n_arr))],
    )(*parts)


def _exchange_chips_sems(n_arr):
    return [pltpu.SemaphoreType.DMA((N_CHIP - 1, n_arr)), pltpu.SemaphoreType.DMA((N_CHIP - 1, n_arr)),
            pltpu.SemaphoreType.DMA((n_arr,))]


def _exchange_chips_phases(in_refs, out_refs, send_sems, recv_sems, local_sems):
    n_arr = len(in_refs)
    x, y, c = lax.axis_index("x"), lax.axis_index("y"), lax.axis_index("c")
    me = 2 * x + y

    def copies():
        out = [pltpu.make_async_copy(in_refs[a].at[me], out_refs[a].at[me], local_sems.at[a]) for a in range(n_arr)]
        for k in range(1, N_CHIP):
            px = 1 - x if k & 2 else x
            py = 1 - y if k & 1 else y
            for a in range(n_arr):
                out.append(pltpu.make_async_remote_copy(
                    src_ref=in_refs[a].at[2 * px + py], dst_ref=out_refs[a].at[me],
                    send_sem=send_sems.at[k - 1, a], recv_sem=recv_sems.at[k - 1, a],
                    device_id=(px, py, c), device_id_type=MESH_ID))
        return out

    def start():
        for cp in copies():
            cp.start()

    def finish():
        for cp in copies():
            cp.wait()

    return start, finish


def _row_tile(r, ccols, blocks):
    cap = max(16, (2 * 1024 * 1024) // (4 * ccols * blocks))
    return _pick(r, cap, 16)


def _pair_add(mine, theirs, my_c, name):
    _, r, ccols = mine.shape
    tr = _row_tile(r, ccols, 1)

    def body(c_ref, a_ref, b_ref, o_ref):
        o_ref[...] = (a_ref[...].astype(F32) + b_ref[...].astype(F32)).astype(o_ref.dtype)

    spec = pl.BlockSpec((None, tr, ccols), lambda q, i, c_ref: (q, i, 0))
    return pl.pallas_call(
        body, name=name,
        grid_spec=pltpu.PrefetchScalarGridSpec(
            num_scalar_prefetch=1, grid=(N_CHIP, r // tr),
            in_specs=[pl.BlockSpec((None, tr, ccols), lambda q, i, c_ref: (2 * q + c_ref[0], i, 0)), spec],
            out_specs=spec),
        out_shape=jax.ShapeDtypeStruct(theirs.shape, theirs.dtype),
        compiler_params=_params("parallel", "parallel"),
    )(my_c, mine, theirs)


def _sum_blocks(parts, name):
    nb, r, ccols = parts.shape
    tr = _row_tile(r, ccols, nb)

    def body(p_ref, o_ref):
        acc = p_ref[0].astype(F32)
        for i in range(1, nb):
            acc = acc + p_ref[i].astype(F32)
        o_ref[...] = acc

    return pl.pallas_call(
        body, name=name, grid=(r // tr,),
        in_specs=[pl.BlockSpec((nb, tr, ccols), lambda i: (0, i, 0))],
        out_specs=pl.BlockSpec((tr, ccols), lambda i: (i, 0)),
        out_shape=jax.ShapeDtypeStruct((r, ccols), F32),
        compiler_params=_params("parallel"),
    )(parts)


@jax.custom_vjp
def replicated(vec):
    return vec


def _replicated_fwd(vec):
    return vec, None


def _replicated_bwd(_, ct):
    return (_sum_blocks(_allgather([ct], "small_grad_allgather")[0], "small_grad_sum"),)


replicated.defvjp(_replicated_fwd, _replicated_bwd)


def _adamw(w, g, m, v, name):
    rows, cols = w.shape
    tr = _pick(rows, 256, 8) if rows % 8 == 0 else rows

    def body(w_ref, g_ref, m_ref, v_ref, d_ref, nm_ref, nv_ref):
        g_ = g_ref[...]
        m_ = ADAM_B1 * m_ref[...] + (1.0 - ADAM_B1) * g_
        v_ = ADAM_B2 * v_ref[...] + (1.0 - ADAM_B2) * jnp.square(g_)
        m_hat = m_ / (1.0 - ADAM_B1 ** ADAM_STEP)
        v_hat = v_ / (1.0 - ADAM_B2 ** ADAM_STEP)
        d_ref[...] = -ADAM_LR * (m_hat / (jnp.sqrt(v_hat) + ADAM_EPS) + ADAM_WD * w_ref[...])
        nm_ref[...] = m_
        nv_ref[...] = v_

    spec = pl.BlockSpec((tr, cols), lambda i: (i, 0))
    return pl.pallas_call(
        body, name=name, grid=(rows // tr,), in_specs=[spec] * 4, out_specs=[spec] * 3,
        out_shape=[jax.ShapeDtypeStruct(w.shape, F32)] * 3, compiler_params=_params("parallel"),
    )(w, g, m, v)


COL_SHARDED = ("w_in", "w_uq", "w_ukv", "w_branch_a", "w_branch_b", "w_up", "w_ple")
EARLY = ("w_in",)
MID = ("w_uq", "w_ukv", "w_branch_a", "w_branch_b", "w_out")
LATE = ("w_up", "w_down", "w_ple_gate", "w_ple")
SMALL = ("attn_pre_norm", "attn_post_norm", "b_gate", "q_a_norm", "kv_a_norm", "mlp_pre_norm", "mlp_post_norm",
         "conv_b", "ple_norm", "sinks")
SMALL_COLS = 128


def _pack_rows(arrays, cols, row_mult):
    flat = jnp.concatenate([a.reshape(-1) for a in arrays])
    pad = (-flat.shape[0]) % (cols * row_mult)
    return jnp.pad(flat, (0, pad)).reshape(-1, cols)


def _unpack_small(vec, shapes):
    flat = vec.reshape(-1)
    out, off = {}, 0
    for name in SMALL:
        n = shapes[name]
        out[name] = flat[off:off + n].reshape(1, n)
        off += n + (-n) % SMALL_COLS
    return out


def _pad_lanes(t, width):
    return jnp.pad(t, [(0, 0)] * (t.ndim - 1) + [(0, width - t.shape[-1])])


def _pad_rows(t, rows):
    return jnp.pad(t, [(0, 0)] * (t.ndim - 2) + [(0, rows - t.shape[-2]), (0, 0)])


FRONT_SIZES = (512, 128, 128, 256, 128)
FRONT_BOUNDS = (0, 512, 640, 768, 1024, 1152, 1280)
PE_LANE = NOPE_DIM


def _arrange_w_in_t(wt):
    k = wt.shape[1]
    n_front = sum(FRONT_SIZES)
    front, kr, gates = wt[:n_front], wt[n_front:n_front + ROPE_DIM], wt[n_front + ROPE_DIM:]
    kr_slab = jnp.concatenate([jnp.zeros((PE_LANE, k), wt.dtype), kr,
                               jnp.zeros((HEAD_PAD - PE_LANE - ROPE_DIM, k), wt.dtype)], axis=0)
    return jnp.concatenate([front, kr_slab], axis=0), gates


def _arrange_w_uq_t(wt):
    k = wt.shape[1]
    return _pad_rows(wt.reshape(B_HEADS, NOPE_DIM + ROPE_DIM, k), HEAD_PAD).reshape(B_HEADS * HEAD_PAD, k)


def _arrange_w_ukv_t(wt):
    k = wt.shape[1]
    w = wt.reshape(B_HEADS, 2, NOPE_DIM, k)
    slabs = [_pad_rows(w[:, part], HEAD_PAD).reshape(B_HEADS * HEAD_PAD, k) for part in range(2)]
    return jnp.concatenate(slabs, axis=0)


def _rope_tables(positions, s):
    pos = positions.reshape(s, 1).astype(F32)

    def angles(dim):
        return pos * ROPE_THETA ** (-(jnp.arange(0, dim, 2, dtype=F32) / dim))

    cos_a, sin_a = jnp.cos(angles(A_HEAD_DIM)), jnp.sin(angles(A_HEAD_DIM))
    zero_a = jnp.zeros_like(sin_a)
    tables_a = [jnp.tile(jnp.concatenate(pair, axis=1), (1, LANES // A_HEAD_DIM))
                for pair in ((cos_a, cos_a), (-sin_a, zero_a), (zero_a, sin_a))]
    cos_b, sin_b = jnp.cos(angles(ROPE_DIM)), jnp.sin(angles(ROPE_DIM))
    zero_b = jnp.zeros_like(sin_b)

    def slab(first, second, fill):
        return jnp.concatenate([jnp.full((s, PE_LANE), fill, F32), first, second,
                                jnp.full((s, HEAD_PAD - PE_LANE - ROPE_DIM), fill, F32)], axis=1)

    tables_b = [slab(cos_b, cos_b, 1.0), slab(-sin_b, zero_b, 0.0), slab(zero_b, sin_b, 0.0)]
    return tables_a + tables_b


def _local_loss(wts, x, p, tables, target):
    s = x.shape[0]
    small_shapes = {n: wts[n].shape[-1] for n in SMALL}
    small_vec = _pack_rows([_pad_lanes(wts[n].reshape(1, -1), small_shapes[n] + (-small_shapes[n]) % SMALL_COLS)
                            for n in SMALL], SMALL_COLS, 8)
    sm = _unpack_small(replicated(small_vec), small_shapes)
    def shard(n):
        return wts[n].T if n in COL_SHARDED else wts[n]

    h1_front, h1_gates, x_res, gathered = prenorm_gather(
        x, sm["attn_pre_norm"], tuple([shard(n) for n in EARLY] + [_pack_rows([wts["conv_w"]], SMALL_COLS, 8)]),
        (BF16,) * len(EARLY) + (F32,))
    big = {n: g.reshape(-1, g.shape[2]) for n, g in zip(EARLY, gathered)}
    ch = wts["conv_w"].shape[1]
    conv_w = gathered[-1].reshape(N_DEV, -1)[:, :CONV_W * ch].reshape(N_DEV, CONV_W, ch)
    conv_w = conv_w.transpose(1, 0, 2).reshape(CONV_W, N_DEV * ch)

    w_front_t, w_gates_t = _arrange_w_in_t(big["w_in"])
    tables_a, tables_b = tables[:3], tables[3:]

    qa, ka, va, cqn, ckvn, kpe = proj_stage(
        "prep", _f_prep, [(h1_front, w_front_t, "nt", "w_front", True, F32)], params=[sm["q_a_norm"], sm["kv_a_norm"]],
        consts=tables, splits=[FRONT_BOUNDS], ts=512, out_dtypes=[BF16, BF16, BF16, BF16, BF16, F32])
    ya, mid = swa_nat(qa, ka, va, sm["sinks"].reshape(-1), tuple(shard(n) for n in MID))
    big.update({n: g.reshape(-1, g.shape[2]) for n, g in zip(MID, mid)})

    (q2,) = proj_stage("qrope", _f_qrope, [(cqn, _arrange_w_uq_t(big["w_uq"]), "nt", "w_uq", True, BF16)],
                       consts=tables_b, ts=512, out_dtypes=[BF16])
    k2, v2 = proj_stage("kv", _f_kv, [(ckvn, _arrange_w_ukv_t(big["w_ukv"]), "nt", "w_ukv", True, BF16)],
                        extra=[kpe], splits=[(0, B_HEADS * HEAD_PAD, 2 * B_HEADS * HEAD_PAD), None], ts=512,
                        out_dtypes=[BF16, BF16])
    yb, late = flash_nat(q2, k2, v2, tuple(shard(n) for n in LATE))
    big.update({n: g.reshape(-1, g.shape[2]) for n, g in zip(LATE, late)})

    (mixed,) = proj_stage(
        "gate", _f_gate, [(h1_gates, w_gates_t, "nt", "w_gates", True, F32),
                          (ya, big["w_branch_a"], "nt", "w_branch_a", True, BF16),
                          (yb, big["w_branch_b"], "nt", "w_branch_b", True, BF16)],
        params=[sm["b_gate"][:, :D_MODEL], sm["b_gate"][:, D_MODEL:]],
        splits=[(0, D_MODEL, 2 * D_MODEL), None, None], out_dtypes=[BF16])
    x1, h2 = proj_stage("post_attn", _f_post, [(mixed, big["w_out"], "nn", "w_out", True, F32)], extra=[x_res],
                        params=[sm["attn_post_norm"], sm["mlp_pre_norm"]], ts=512, out_dtypes=[F32, BF16])

    act = mlp_up(h2, big["w_up"], conv_w, sm["conv_b"])
    x2, h3 = proj_stage("post_mlp", _f_post, [(act, big["w_down"], "nn", "w_down", True, F32)], extra=[x1],
                        params=[sm["mlp_post_norm"], sm["ple_norm"]], ts=512, out_dtypes=[F32, BF16])

    (rowloss,) = proj_stage("loss", _f_out, [(h3, big["w_ple_gate"], "nn", "w_ple_gate", True, F32),
                                             (p, big["w_ple"], "nt", "w_ple", False, BF16)], extra=[x2],
                            consts=[target], ts=512)
    return jnp.sum(rowloss)


WEIGHTS = ["attn_pre_norm", "attn_post_norm", "w_in", "b_gate", "sinks", "q_a_norm", "w_uq", "kv_a_norm", "w_ukv",
           "w_branch_a", "w_branch_b", "w_out", "mlp_pre_norm", "mlp_post_norm", "w_up", "conv_w", "conv_b",
           "w_down", "ple_norm", "w_ple_gate", "w_ple"]


def kernel(x, p, positions, attn_pre_norm, attn_post_norm, w_in, b_gate, sinks, q_a_norm, w_uq, kv_a_norm, w_ukv, w_branch_a, w_branch_b, w_out, mlp_pre_norm, mlp_post_norm, w_up, conv_w, conv_b, w_down, ple_norm, w_ple_gate, w_ple, loss_target, m_attn_pre_norm, m_attn_post_norm, m_w_in, m_b_gate, m_sinks, m_q_a_norm, m_w_uq, m_kv_a_norm, m_w_ukv, m_w_branch_a, m_w_branch_b, m_w_out, m_mlp_pre_norm, m_mlp_post_norm, m_w_up, m_conv_w, m_conv_b, m_w_down, m_ple_norm, m_w_ple_gate, m_w_ple, v_attn_pre_norm, v_attn_post_norm, v_w_in, v_b_gate, v_sinks, v_q_a_norm, v_w_uq, v_kv_a_norm, v_w_ukv, v_w_branch_a, v_w_branch_b, v_w_out, v_mlp_pre_norm, v_mlp_post_norm, v_w_up, v_conv_w, v_conv_b, v_w_down, v_ple_norm, v_w_ple_gate, v_w_ple):
    given = dict(locals())
    s = x.shape[1]
    wts = {n: given[n][0] if given[n].ndim == 3 else given[n] for n in WEIGHTS}
    tables = _rope_tables(positions, s)
    local_loss, (grads, grad_x) = jax.value_and_grad(_local_loss, argnums=(0, 1))(
        wts, x[0], p[0, 0], tables, loss_target[0])
    loss = lax.psum(local_loss, AXES)

    outs = {"grad": [], "delta": [], "m": [], "v": []}
    for n in WEIGHTS:
        shape = given[n].shape
        w2 = wts[n].reshape(-1, shape[-1])
        g2 = grads[n].reshape(w2.shape)
        delta, new_m, new_v = _adamw(w2, g2, given["m_" + n].reshape(w2.shape), given["v_" + n].reshape(w2.shape),
                                     "adamw_" + n)
        outs["grad"].append(g2.reshape(shape))
        outs["delta"].append(delta.reshape(shape))
        outs["m"].append(new_m.reshape(shape))
        outs["v"].append(new_v.reshape(shape))
    return (loss, grad_x[None], *outs["grad"], *outs["delta"], *outs["m"], *outs["v"])
```

```python
import functools

import numpy as np
import jax
import jax.numpy as jnp
from jax import lax
from jax.experimental import pallas as pl
from jax.experimental.pallas import tpu as pltpu

F32 = jnp.float32
BF16 = jnp.bfloat16
MESH_ID = pl.DeviceIdType.MESH
AXES = ("x", "y", "c")
N_DEV = 8

D_MODEL = 1024
RMS_EPS = 1e-6
ROPE_THETA = 10000.0
SWA_BLOCK = 128
A_HEADS, A_KV_HEADS, A_HEAD_DIM = 8, 2, 64
A_GROUP = A_HEADS // A_KV_HEADS
B_HEADS, Q_LORA, KV_LORA, NOPE_DIM, ROPE_DIM, V_DIM = 8, 256, 128, 64, 32, 64
D_FF = 2816
CONV_W = 3
HEAD_PAD = 128

ADAM_LR, ADAM_B1, ADAM_B2, ADAM_EPS, ADAM_WD, ADAM_STEP = 0.001, 0.9, 0.999, 1e-08, 0.01, 10

VMEM_LIMIT = 48 * 1024 * 1024
MM_TM, MM_TN, MM_TK_TOKENS = 1024, 1408, 2048
MM_VMEM_BUDGET = 36 * 1024 * 1024
FLASH_T = 1024
CONV_TS = 256
CONV_CHUNK = 256


def _params(*sem):
    return pltpu.CompilerParams(dimension_semantics=sem, vmem_limit_bytes=VMEM_LIMIT)


def _pick(dim, cap, mult):
    best = None
    for t in range(mult, min(dim, cap) + 1, mult):
        if dim % t == 0:
            best = t
    return dim if best is None else best


def _divisors(dim, mult):
    return [t for t in range(mult, dim + 1, mult) if dim % t == 0] or [dim]


def _matmul_tiles(m, n, kdim, form, sizes):
    sa, sb, so = sizes
    tk = _pick(kdim, MM_TK_TOKENS, 128) if form == "tn" else kdim
    cap_m = MM_TN if form == "tn" else MM_TM
    best = None
    for tm in _divisors(m, 128):
        for tn in _divisors(n, 128):
            need = 2 * (tm * tk * sa + tk * tn * sb + tm * tn * so) + (tm * tn * 4 if tk != kdim else 0)
            if tm > cap_m or tn > MM_TN or need > MM_VMEM_BUDGET:
                continue
            if best is None or (tm * tn, tm) > (best[0] * best[1], best[0]):
                best = (tm, tn)
    return best[0], best[1], tk


def _matmul(a, b, form, *, out_dtype=F32, name):
    if form == "tn":
        (kdim, m), n = a.shape, b.shape[1]
    else:
        (m, kdim), n = a.shape, (b.shape[1] if form == "nn" else b.shape[0])
    sizes = (a.dtype.itemsize, b.dtype.itemsize, jnp.dtype(out_dtype).itemsize)
    tm, tn, tk = _matmul_tiles(m, n, kdim, form, sizes)
    nk = kdim // tk
    rows_outer = nk > 1 or (m // tm) * b.size * sizes[1] <= (n // tn) * a.size * sizes[0]

    def ij(fn):
        return (lambda i, j, k: fn(i, j, k)) if rows_outer else (lambda j, i, k: fn(i, j, k))

    a_spec = (pl.BlockSpec((tk, tm), ij(lambda i, j, k: (k, i))) if form == "tn"
              else pl.BlockSpec((tm, tk), ij(lambda i, j, k: (i, k))))
    b_spec = (pl.BlockSpec((tn, tk), ij(lambda i, j, k: (j, k))) if form == "nt"
              else pl.BlockSpec((tk, tn), ij(lambda i, j, k: (k, j))))
    dims = (((0 if form == "tn" else 1,), (1 if form == "nt" else 0,)), ((), ()))

    def product(a_ref, b_ref):
        return lax.dot_general(a_ref[...].astype(BF16), b_ref[...].astype(BF16), dims, preferred_element_type=F32)

    if nk == 1:
        def body(a_ref, b_ref, o_ref):
            o_ref[...] = product(a_ref, b_ref).astype(o_ref.dtype)

        scratch = []
    else:
        def body(a_ref, b_ref, o_ref, acc_ref):
            k = pl.program_id(2)

            @pl.when(k == 0)
            def _():
                acc_ref[...] = jnp.zeros_like(acc_ref)

            acc_ref[...] += product(a_ref, b_ref)

            @pl.when(k == nk - 1)
            def _():
                o_ref[...] = acc_ref[...].astype(o_ref.dtype)

        scratch = [pltpu.VMEM((tm, tn), F32)]

    return pl.pallas_call(
        body, name=name, grid=(m // tm, n // tn, nk) if rows_outer else (n // tn, m // tm, nk),
        in_specs=[a_spec, b_spec],
        out_specs=pl.BlockSpec((tm, tn), ij(lambda i, j, k: (i, j))),
        out_shape=jax.ShapeDtypeStruct((m, n), out_dtype),
        scratch_shapes=scratch,
        compiler_params=_params("parallel", "parallel", "arbitrary"),
    )(a, b)


def _pairs(bounds):
    return list(zip(bounds[:-1], bounds[1:]))


def _split(v, bounds):
    return [v[:, a:b] for a, b in _pairs(bounds)]


def _stage_build(name, f, tiled, params, consts, splits, ts, out_dtypes, ct_dtypes=None):
    n_t, n_p, n_c = len(tiled), len(params), len(consts)
    ct_dtypes = [t.dtype for t in tiled] if ct_dtypes is None else ct_dtypes
    s = tiled[0].shape[0]
    ts = min(ts, s)
    grid = (s // ts,)
    if splits is None:
        splits = [None] * n_t
    in_bounds = [(0, t.shape[1]) if b is None else tuple(b) for t, b in zip(tiled, splits)]

    def tile_aval(arr):
        return jax.ShapeDtypeStruct((ts, arr.shape[1]), arr.dtype)

    slab_avals = [[jax.ShapeDtypeStruct((ts, e - a), F32) for a, e in _pairs(b)]
                  for t, b in zip(tiled, in_bounds)]
    out_avals = jax.eval_shape(f, slab_avals, list(params), [tile_aval(c) for c in consts])
    out_bounds = [tuple(np.cumsum([0] + [o.shape[1] for o in slabs]).tolist()) for slabs in out_avals]
    out_dtypes = [F32] * len(out_bounds) if out_dtypes is None else out_dtypes
    out_shapes = [jax.ShapeDtypeStruct((s, b[-1]), d) for b, d in zip(out_bounds, out_dtypes)]

    def row_spec(width):
        return pl.BlockSpec((ts, width), lambda i: (i, 0))

    def par_spec(arr):
        return pl.BlockSpec(arr.shape, lambda i: (0, 0))

    in_specs = ([row_spec(t.shape[1]) for t in tiled] + [par_spec(p) for p in params]
                + [row_spec(c.shape[1]) for c in consts])

    def load(refs):
        t = [_split(r[...].astype(F32), b) for r, b in zip(refs[:n_t], in_bounds)]
        p = [r[...] for r in refs[n_t:n_t + n_p]]
        c = [r[...] for r in refs[n_t + n_p:n_t + n_p + n_c]]
        return t, p, c

    def store(refs, values, bounds):
        for ref, slabs, b in zip(refs, values, bounds):
            for v, (a, e) in zip(slabs, _pairs(b)):
                ref[:, a:e] = v.astype(ref.dtype)

    def run_fwd(tiled, params, consts):
        def body(*refs):
            t, p, c = load(refs)
            store(refs[n_t + n_p + n_c:], f(t, p, c), out_bounds)

        return pl.pallas_call(
            body, name=name + "_fwd", grid=grid, in_specs=in_specs,
            out_specs=[row_spec(b[-1]) for b in out_bounds], out_shape=out_shapes,
            compiler_params=_params("parallel"),
        )(*tiled, *params, *consts)

    def run_bwd(tiled, params, consts, cts):
        n_in = n_t + n_p + n_c
        n_o = len(out_bounds)

        def body(*refs):
            t, p, c = load(refs)
            g = [_split(r[...].astype(F32), b) for r, b in zip(refs[n_in:n_in + n_o], out_bounds)]
            _, pull = jax.vjp(lambda t_, p_: f(t_, p_, c), t, p)
            dt, dp = pull(g)
            store(refs[n_in + n_o:n_in + n_o + n_t], dt, in_bounds)
            first = pl.program_id(0) == 0
            for ref, d in zip(refs[n_in + n_o + n_t:], dp):
                @pl.when(first)
                def _(ref=ref):
                    ref[...] = jnp.zeros_like(ref)

                ref[...] += d

        res = pl.pallas_call(
            body, name=name + "_bwd", grid=grid,
            in_specs=in_specs + [row_spec(b[-1]) for b in out_bounds],
            out_specs=[row_spec(t.shape[1]) for t in tiled] + [par_spec(p) for p in params],
            out_shape=[jax.ShapeDtypeStruct(t.shape, d) for t, d in zip(tiled, ct_dtypes)]
                      + [jax.ShapeDtypeStruct(p.shape, F32) for p in params],
            compiler_params=_params("arbitrary"),
        )(*tiled, *params, *consts, *cts)
        return tuple(res[:n_t]), tuple(res[n_t:])

    return run_fwd, run_bwd


def proj_stage(name, f, projections, extra=(), params=(), consts=(), splits=None, ts=256, out_dtypes=None):
    n_z = len(projections)
    forms = [pr[2] for pr in projections]
    names = [pr[3] for pr in projections]
    need_da = [pr[4] for pr in projections]
    store = [pr[5] for pr in projections]
    extra, params, consts = tuple(extra), tuple(params), tuple(consts)

    def matmuls(a_list, w_list):
        return tuple(_matmul(a, w, form, out_dtype=dt, name=n + "_fwd")
                     for a, w, form, n, dt in zip(a_list, w_list, forms, names, store))

    def build(zs, ct=False):
        ct_dtypes = [BF16] * n_z + [e.dtype for e in extra] if ct else None
        return _stage_build(name, f, tuple(zs) + extra, params, consts, splits, ts, out_dtypes, ct_dtypes)

    @jax.custom_vjp
    def op(a_list, w_list, extra, params, consts):
        zs = matmuls(a_list, w_list)
        return tuple(build(zs)[0](zs + extra, params, consts))

    def op_fwd(a_list, w_list, extra, params, consts):
        zs = matmuls(a_list, w_list)
        return tuple(build(zs)[0](zs + extra, params, consts)), (a_list, w_list, zs, extra, params, consts)

    def op_bwd(res, cts):
        a_list, w_list, zs, extra, params, consts = res
        dt, dp = build(zs, ct=True)[1](zs + extra, params, consts, cts)
        da_list, dw_list = [], []
        for a, w, dz, form, n, want in zip(a_list, w_list, dt[:n_z], forms, names, need_da):
            if form == "nn":
                da = _matmul(dz, w, "nt", out_dtype=a.dtype, name=n + "_da") if want else jnp.zeros_like(a)
                dw = _matmul(a, dz, "tn", out_dtype=w.dtype, name=n + "_dw")
            else:
                da = _matmul(dz, w, "nn", out_dtype=a.dtype, name=n + "_da") if want else jnp.zeros_like(a)
                dw = _matmul(dz, a, "tn", out_dtype=w.dtype, name=n + "_dw")
            da_list.append(da)
            dw_list.append(dw)
        return tuple(da_list), tuple(dw_list), tuple(dt[n_z:]), dp, tuple(jnp.zeros_like(c) for c in consts)

    op.defvjp(op_fwd, op_bwd)
    return op(tuple(pr[0] for pr in projections), tuple(pr[1] for pr in projections), extra, params, consts)


def _rms(t, g):
    return t * lax.rsqrt(jnp.mean(t * t, axis=-1, keepdims=True) + RMS_EPS) * g


@functools.partial(jax.custom_vjp, nondiff_argnums=(1,))
def _lane_roll(t, shift):
    return pltpu.roll(t, shift % t.shape[-1], t.ndim - 1)


def _lane_roll_fwd(t, shift):
    return _lane_roll(t, shift), None


def _lane_roll_bwd(shift, _, ct):
    return (pltpu.roll(ct, (-shift) % ct.shape[-1], ct.ndim - 1),)


_lane_roll.defvjp(_lane_roll_fwd, _lane_roll_bwd)


def _rope_lanes(t, tables, half):
    reps = t.shape[1] // tables[0].shape[1]
    c, s_lo, s_hi = [jnp.concatenate([tb] * reps, axis=1) if reps > 1 else tb for tb in tables]
    return t * c + _lane_roll(t, -half) * s_lo + _lane_roll(t, half) * s_hi


PRENORM_TS = 256


def _prenorm_fwd_call(x, g, shards):
    s, width = x.shape
    ts = min(PRENORM_TS, s)
    nt = s // ts
    n_arr = len(shards)

    def body(*refs):
        x_ref, g_ref = refs[:2]
        o_ref = refs[2 + n_arr]
        i = pl.program_id(0)
        ag_start, ag_forward, ag_finish = _allgather_phases(refs[2:2 + n_arr], refs[3 + n_arr:3 + 2 * n_arr],
                                                            *refs[3 + 2 * n_arr:])

        @pl.when(i == 0)
        def _():
            ag_start()

        @pl.when(i == nt // 2)
        def _():
            ag_forward()

        o_ref[...] = _rms(x_ref[...], g_ref[...]).astype(o_ref.dtype)

        @pl.when(i == nt - 1)
        def _():
            ag_finish()

    return pl.pallas_call(
        body, name="prenorm_fwd", grid=(nt,),
        in_specs=[pl.BlockSpec((ts, width), lambda i: (i, 0)), pl.BlockSpec(g.shape, lambda i: (0, 0))]
                 + [HBM_SPEC] * n_arr,
        out_specs=[pl.BlockSpec((ts, width), lambda i: (i, 0))] + [HBM_SPEC] * n_arr,
        out_shape=[jax.ShapeDtypeStruct(x.shape, BF16)] + _allgather_out_shapes(shards),
        scratch_shapes=_allgather_sems(n_arr),
        compiler_params=_params("arbitrary"),
    )(x, g, *shards)


def _prenorm_bwd_call(x, g, dh_a, dh_b, dx_res, parts):
    s, width = x.shape
    ts = min(PRENORM_TS, s)
    nt = s // ts
    n_arr = len(parts)

    def body(*refs):
        x_ref, g_ref, dha_ref, dhb_ref, dxr_ref = refs[:5]
        dx_ref, dg_ref = refs[5 + n_arr:7 + n_arr]
        i = pl.program_id(0)
        exchange_start, exchange_finish = _exchange_chips_phases(
            refs[5:5 + n_arr], refs[7 + n_arr:7 + 2 * n_arr], *refs[7 + 2 * n_arr:])

        @pl.when(i == 0)
        def _():
            exchange_start()
            dg_ref[...] = jnp.zeros_like(dg_ref)

        _, pull = jax.vjp(_rms, x_ref[...], g_ref[...])
        dx, dg = pull(dha_ref[...].astype(F32) + dhb_ref[...].astype(F32))
        dx_ref[...] = dx + dxr_ref[...]
        dg_ref[...] += dg

        @pl.when(i == nt - 1)
        def _():
            exchange_finish()

    row = pl.BlockSpec((ts, width), lambda i: (i, 0))
    par = pl.BlockSpec(g.shape, lambda i: (0, 0))
    return pl.pallas_call(
        body, name="prenorm_bwd", grid=(nt,),
        in_specs=[row, par, row, row, row] + [HBM_SPEC] * n_arr,
        out_specs=[row, par] + [HBM_SPEC] * n_arr,
        out_shape=[jax.ShapeDtypeStruct(x.shape, F32), jax.ShapeDtypeStruct(g.shape, F32)]
                  + [jax.ShapeDtypeStruct(p.shape, p.dtype) for p in parts],
        scratch_shapes=_exchange_chips_sems(n_arr),
        compiler_params=_params("arbitrary"),
    )(x, g, dh_a, dh_b, dx_res, *parts)


@functools.partial(jax.custom_vjp, nondiff_argnums=(3,))
def prenorm_gather(x, g, shards, wire_dtypes):
    out = _prenorm_fwd_call(x, g, [s.astype(d) for s, d in zip(shards, wire_dtypes)])
    return out[0], out[0], x, tuple(out[1:])


def _prenorm_gather_fwd(x, g, shards, wire_dtypes):
    return prenorm_gather(x, g, shards, wire_dtypes), (x, g)


def _prenorm_gather_bwd(wire_dtypes, res, cts):
    x, g = res
    dh_a, dh_b, dx_res, d_gathered = cts
    out = _prenorm_bwd_call(x, g, dh_a, dh_b, dx_res, _reduce_scatter_head(d_gathered, "grads"))
    return out[0], out[1], _reduce_scatter_tail(out[2:], "grads")


prenorm_gather.defvjp(_prenorm_gather_fwd, _prenorm_gather_bwd)


def _dup_heads(x):
    low = lax.broadcasted_iota(jnp.int32, x.shape, 1) < x.shape[1] // 2
    tiles = []
    for keep in (low, jnp.logical_not(low)):
        xm = jnp.where(keep, x, 0.0)
        tiles.append(xm + _lane_roll(xm, x.shape[1] // 2))
    return jnp.concatenate(tiles, axis=1)


def _f_prep(t, p, c):
    qa, ka, va, cq, ckv, kr = t[0]
    return [[_rope_lanes(qa, c[0:3], A_HEAD_DIM // 2)], [_dup_heads(_rope_lanes(ka, c[0:3], A_HEAD_DIM // 2))],
            [_dup_heads(va)], [_rms(cq, p[0])], [_rms(ckv, p[1])], [_rope_lanes(kr, c[3:6], ROPE_DIM // 2)]]


def _f_qrope(t, p, c):
    return [[_rope_lanes(t[0][0], c, ROPE_DIM // 2)]]


def _f_kv(t, p, c):
    (k_nope, v), (k_pe,) = t
    return [[k_nope + jnp.concatenate([k_pe] * B_HEADS, axis=1)], [v]]


def _f_gate(t, p, c):
    (ga, gb), (pa,), (pb,) = t
    ba, bb = p
    return [[jax.nn.sigmoid(ga + ba) * pa + jax.nn.sigmoid(gb + bb) * pb]]


def _f_post(t, p, c):
    (branch,), (residual,) = t
    x1 = residual + _rms(branch, p[0])
    return [[x1], [_rms(x1, p[1])]]


def _f_out(t, p, c):
    (gate,), (emb,), (x2,) = t
    y = x2 + jax.nn.sigmoid(gate) * emb
    err = y - c[0]
    return [[0.5 * jnp.mean(err * err, axis=-1, keepdims=True)]]


def _shift_down(cur, prev, has_prev):
    full = jnp.concatenate([prev * has_prev, cur], axis=0)
    return pltpu.roll(full, 1, 0)[HALO:], pltpu.roll(full, 2, 0)[HALO:]


GELU_C = float(np.sqrt(2.0 / np.pi))
GELU_A = 0.044715
HALO = 8


def _gelu_tanh(x):
    x2 = x * x
    th = jnp.tanh(x * (GELU_C + (GELU_C * GELU_A) * x2))
    half = 0.5 + 0.5 * th
    return x * half, half + x * (0.5 - 0.5 * (th * th)) * (GELU_C + (3.0 * GELU_C * GELU_A) * x2)


def _row_sum(t):
    return jnp.sum(t, axis=0, keepdims=True)


def _conv3(cur, prev, w_ref, b_ref, has_prev):
    u1, u2 = _shift_down(cur, prev, has_prev)
    return w_ref[2:3, :] * cur + w_ref[1:2, :] * u1 + w_ref[0:1, :] * u2 + b_ref[...], u1, u2


def _mlp_act_specs(s):
    ts = min(CONV_TS, s)
    hb = ts // HALO

    def half_specs(h):
        return [pl.BlockSpec((ts, D_FF), lambda i: (i, h)),
                pl.BlockSpec((HALO, D_FF), lambda i: (jnp.maximum(i * hb - 1, 0), h))]

    def par_specs(h):
        return [pl.BlockSpec((CONV_W, D_FF), lambda i: (0, h)), pl.BlockSpec((1, D_FF), lambda i: (0, h))]

    return ts, hb, half_specs, par_specs


def _mlp_act_fwd_call(up, conv_w, conv_b):
    s = up.shape[0]
    ts, hb, half_specs, par_specs = _mlp_act_specs(s)

    def body(g_ref, gp_ref, v_ref, vp_ref, wg_ref, bg_ref, wv_ref, bv_ref, o_ref):
        has_prev = (pl.program_id(0) > 0).astype(F32)

        def chunk(cidx, carry):
            cols = pl.ds(pl.multiple_of(cidx * CONV_CHUNK, CONV_CHUNK), CONV_CHUNK)
            u_g, _, _ = _conv3(g_ref[:, cols], gp_ref[:, cols], wg_ref.at[:, cols], bg_ref.at[:, cols], has_prev)
            u_v, _, _ = _conv3(v_ref[:, cols], vp_ref[:, cols], wv_ref.at[:, cols], bv_ref.at[:, cols], has_prev)
            o_ref[:, cols] = (_gelu_tanh(u_g)[0] * u_v).astype(o_ref.dtype)
            return carry

        lax.fori_loop(0, D_FF // CONV_CHUNK, chunk, 0)

    return pl.pallas_call(
        body, name="mlp_act_fwd", grid=(s // ts,),
        in_specs=half_specs(0) + half_specs(1) + par_specs(0) + par_specs(1),
        out_specs=pl.BlockSpec((ts, D_FF), lambda i: (i, 0)),
        out_shape=jax.ShapeDtypeStruct((s, D_FF), BF16),
        compiler_params=_params("parallel"),
    )(up, up, up, up, conv_w, conv_b, conv_w, conv_b)


def _mlp_act_bwd_call(up, conv_w, conv_b, dact):
    s = up.shape[0]
    ts, hb, half_specs, par_specs = _mlp_act_specs(s)
    nt = s // ts
    ext = ts + HALO
    bf16_rows = 2 * HALO

    def next_spec(rows, h):
        return pl.BlockSpec((rows, D_FF), lambda i: (jnp.minimum((i + 1) * (ts // rows), s // rows - 1), h))

    def body(g_ref, gp_ref, gn_ref, v_ref, vp_ref, vn_ref, wg_ref, bg_ref, wv_ref, bv_ref, da_ref, dan_ref,
             dup_ref, dwg_ref, dbg_ref, dwv_ref, dbv_ref):
        i = pl.program_id(0)
        has_prev, has_next = (i > 0).astype(F32), (i < nt - 1).astype(F32)

        @pl.when(i == 0)
        def _():
            for ref in (dwg_ref, dbg_ref, dwv_ref, dbv_ref):
                ref[...] = jnp.zeros_like(ref)

        def chunk(cidx, carry):
            cols = pl.ds(pl.multiple_of(cidx * CONV_CHUNK, CONV_CHUNK), CONV_CHUNK)
            g_ext = jnp.concatenate([g_ref[:, cols], gn_ref[:, cols]], axis=0)
            v_ext = jnp.concatenate([v_ref[:, cols], vn_ref[:, cols]], axis=0)
            u_g, g1, g2 = _conv3(g_ext, gp_ref[:, cols], wg_ref.at[:, cols], bg_ref.at[:, cols], has_prev)
            u_v, v1, v2 = _conv3(v_ext, vp_ref[:, cols], wv_ref.at[:, cols], bv_ref.at[:, cols], has_prev)
            da_ext = jnp.concatenate([da_ref[:, cols].astype(F32),
                                      dan_ref[:, cols].astype(F32)[0:HALO] * has_next], axis=0)
            act_g, dact_g = _gelu_tanh(u_g)
            du_g = da_ext * u_v * dact_g
            du_v = da_ext * act_g
            for du, w_ref, x0, x1, x2, dw_ref, db_ref, lo in ((du_g, wg_ref, g_ext, g1, g2, dwg_ref, dbg_ref, 0),
                                                          (du_v, wv_ref, v_ext, v1, v2, dwv_ref, dbv_ref, D_FF)):
                d1 = pltpu.roll(du, ext - 1, 0)
                d2 = pltpu.roll(du, ext - 2, 0)
                dup = w_ref[2:3, cols] * du + w_ref[1:2, cols] * d1 + w_ref[0:1, cols] * d2
                out_cols = pl.ds(pl.multiple_of(lo + cidx * CONV_CHUNK, CONV_CHUNK), CONV_CHUNK)
                dup_ref[:, out_cols] = dup[0:ts].astype(dup_ref.dtype)
                own = du[0:ts]
                dw_ref[0:1, cols] += _row_sum(own * x2[0:ts])
                dw_ref[1:2, cols] += _row_sum(own * x1[0:ts])
                dw_ref[2:3, cols] += _row_sum(own * x0[0:ts])
                db_ref[:, cols] += _row_sum(own)
            return carry

        lax.fori_loop(0, D_FF // CONV_CHUNK, chunk, 0)

    par_out = [pl.BlockSpec((CONV_W, D_FF), lambda i: (0, 0)), pl.BlockSpec((1, D_FF), lambda i: (0, 0))]
    par_shapes = [jax.ShapeDtypeStruct((CONV_W, D_FF), F32), jax.ShapeDtypeStruct((1, D_FF), F32)]
    return pl.pallas_call(
        body, name="mlp_act_bwd", grid=(nt,),
        in_specs=(half_specs(0) + [next_spec(HALO, 0)] + half_specs(1) + [next_spec(HALO, 1)]
                  + par_specs(0) + par_specs(1)
                  + [pl.BlockSpec((ts, D_FF), lambda i: (i, 0)), next_spec(bf16_rows, 0)]),
        out_specs=[pl.BlockSpec((ts, 2 * D_FF), lambda i: (i, 0))] + par_out + par_out,
        out_shape=[jax.ShapeDtypeStruct((s, 2 * D_FF), BF16)] + par_shapes + par_shapes,
        compiler_params=_params("arbitrary"),
    )(up, up, up, up, up, up, conv_w, conv_b, conv_w, conv_b, dact, dact)


@jax.custom_vjp
def mlp_up(h2, w_up_t, conv_w, conv_b):
    return _mlp_act_fwd_call(_matmul(h2, w_up_t, "nt", out_dtype=F32, name="w_up_fwd"), conv_w, conv_b)


def _mlp_up_fwd(h2, w_up_t, conv_w, conv_b):
    up = _matmul(h2, w_up_t, "nt", out_dtype=F32, name="w_up_fwd")
    return _mlp_act_fwd_call(up, conv_w, conv_b), (h2, w_up_t, up, conv_w, conv_b)


def _mlp_up_bwd(res, dact):
    h2, w_up_t, up, conv_w, conv_b = res
    dup, dwg, dbg, dwv, dbv = _mlp_act_bwd_call(up, conv_w, conv_b, dact)
    dh2 = _matmul(dup, w_up_t, "nn", out_dtype=h2.dtype, name="w_up_da")
    dw = _matmul(dup, h2, "tn", out_dtype=w_up_t.dtype, name="w_up_dw")
    return dh2, dw, jnp.concatenate([dwg, dwv], axis=1), jnp.concatenate([dbg, dbv], axis=1)


mlp_up.defvjp(_mlp_up_fwd, _mlp_up_bwd)


SWA_ROWS = A_GROUP * SWA_BLOCK


def _swa_sink_rows(sink_ref, g):
    return jnp.concatenate([jnp.full((SWA_BLOCK, 1), sink_ref[g * A_GROUP + h], F32) for h in range(A_GROUP)], axis=0)


def _swa_operands(q_ref, kp_ref, kc_ref, vp_ref, vc_ref, sink_ref):
    groups = []
    for g in range(A_KV_HEADS):
        lanes = slice(g * LANES, (g + 1) * LANES)
        groups.append((_swa_stack_heads(q_ref, g), kp_ref[:, lanes], kc_ref[:, lanes], vp_ref[:, lanes],
                       vc_ref[:, lanes]))
    return groups, jnp.concatenate([_swa_sink_rows(sink_ref, g) for g in range(A_KV_HEADS)], axis=0)


def _swa_probs(groups, sink, prev_off):
    scale = A_HEAD_DIM ** -0.5
    sp = jnp.concatenate([lax.dot_general(gr[0], gr[1], NT_DIMS, preferred_element_type=F32) for gr in groups], axis=0)
    sc = jnp.concatenate([lax.dot_general(gr[0], gr[2], NT_DIMS, preferred_element_type=F32) for gr in groups], axis=0)
    qi = lax.broadcasted_iota(jnp.int32, sp.shape, 0) & (SWA_BLOCK - 1)
    kj = lax.broadcasted_iota(jnp.int32, sp.shape, 1)
    in_cur = kj <= qi
    sw = jnp.where(in_cur, sc, jnp.where(kj > qi + prev_off, sp, -jnp.inf)) * scale
    m = jnp.maximum(jnp.max(sw, axis=-1, keepdims=True), sink)
    e, es = jnp.exp(sw - m), jnp.exp(sink - m)
    den = jnp.sum(e, axis=-1, keepdims=True) + es
    return e / den, in_cur, es / den


def _swa_split(t, in_cur):
    cur = jnp.where(in_cur, t, 0.0)
    return t - cur, cur


MLA_SCALE = (NOPE_DIM + ROPE_DIM) ** -0.5
EXP2_SCALE = MLA_SCALE * float(np.log2(np.e))
NT_DIMS = (((1,), (1,)), ((), ()))
TN_DIMS = (((0,), (0,)), ((), ()))


LANES = 128
HALF = LANES // 2


def _low_half(shape):
    return lax.broadcasted_iota(jnp.int32, shape, len(shape) - 1) < HALF


def _swa_stack_heads(ref, g):
    parts = []
    for tile in range(2):
        slab = ref[:, (2 * g + tile) * LANES:(2 * g + tile + 1) * LANES]
        low = _low_half(slab.shape)
        parts += [jnp.where(low, slab, jnp.zeros_like(slab)), jnp.where(low, jnp.zeros_like(slab), slab)]
    return jnp.concatenate(parts, axis=0)


def _swa_unstack_heads(ref, g, rows):
    for tile in range(2):
        a = rows[(2 * tile) * SWA_BLOCK:(2 * tile + 1) * SWA_BLOCK]
        b = rows[(2 * tile + 1) * SWA_BLOCK:(2 * tile + 2) * SWA_BLOCK]
        ref[:, (2 * g + tile) * LANES:(2 * g + tile + 1) * LANES] = jnp.where(_low_half(a.shape), a, b).astype(ref.dtype)


def _swa_nat_specs():
    blk = SWA_BLOCK
    q_spec = pl.BlockSpec((blk, A_HEADS * A_HEAD_DIM), lambda n: (n, 0))
    prev_spec = pl.BlockSpec((blk, A_KV_HEADS * LANES), lambda n: (jnp.maximum(n - 1, 0), 0))
    cur_spec = pl.BlockSpec((blk, A_KV_HEADS * LANES), lambda n: (n, 0))
    return q_spec, prev_spec, cur_spec, pl.BlockSpec(memory_space=pltpu.SMEM)


def _swa_nat_fwd_call(q, k, v, sinks, shards):
    s = q.shape[0]
    nblk = s // SWA_BLOCK
    n_arr = len(shards)
    q_spec, prev_spec, cur_spec, sink_spec = _swa_nat_specs()

    def body(*refs):
        q_ref, kp_ref, kc_ref, vp_ref, vc_ref, sink_ref = refs[:6]
        o_ref = refs[6 + n_arr]
        n = pl.program_id(0)
        ag_start, ag_forward, ag_finish = _allgather_phases(refs[6:6 + n_arr], refs[7 + n_arr:7 + 2 * n_arr],
                                                            *refs[7 + 2 * n_arr:])

        @pl.when(n == 0)
        def _():
            ag_start()

        @pl.when(n == nblk // 2)
        def _():
            ag_forward()

        prev_off = jnp.where(n > 0, 0, SWA_BLOCK)
        groups, sink = _swa_operands(q_ref, kp_ref, kc_ref, vp_ref, vc_ref, sink_ref)
        p, in_cur, _ = _swa_probs(groups, sink, prev_off)
        ppb, pcb = [t.astype(BF16) for t in _swa_split(p, in_cur)]
        for g, (_, _, _, vp, vc) in enumerate(groups):
            rows = slice(g * SWA_ROWS, (g + 1) * SWA_ROWS)
            out = (jnp.dot(ppb[rows], vp, preferred_element_type=F32)
                   + jnp.dot(pcb[rows], vc, preferred_element_type=F32))
            _swa_unstack_heads(o_ref, g, out)

        @pl.when(n == nblk - 1)
        def _():
            ag_finish()

    return pl.pallas_call(
        body, name="swa_fwd", grid=(nblk,),
        in_specs=[q_spec, prev_spec, cur_spec, prev_spec, cur_spec, sink_spec] + [HBM_SPEC] * n_arr,
        out_specs=[q_spec] + [HBM_SPEC] * n_arr,
        out_shape=[jax.ShapeDtypeStruct(q.shape, BF16)] + _allgather_out_shapes(shards),
        scratch_shapes=_allgather_sems(n_arr),
        compiler_params=_params("arbitrary"),
    )(q, k, k, v, v, sinks, *shards)


def _swa_nat_bwd_call(q, k, v, sinks, do, parts):
    s = q.shape[0]
    nblk = s // SWA_BLOCK
    n_arr = len(parts)
    q_spec, prev_spec, cur_spec, sink_spec = _swa_nat_specs()
    scale = A_HEAD_DIM ** -0.5
    dsink_spec = pl.BlockSpec((A_KV_HEADS, SWA_ROWS, 1), lambda n: (0, 0, 0))

    def body(*refs):
        q_ref, kp_ref, kc_ref, vp_ref, vc_ref, sink_ref, do_ref = refs[:7]
        dq_ref, dkp_ref, dkc_ref, dvp_ref, dvc_ref, dsink_ref = refs[7 + n_arr:13 + n_arr]
        n = pl.program_id(0)
        exchange_start, exchange_finish = _exchange_chips_phases(
            refs[7:7 + n_arr], refs[13 + n_arr:13 + 2 * n_arr], *refs[13 + 2 * n_arr:])

        @pl.when(n == 0)
        def _():
            exchange_start()
        prev_off = jnp.where(n > 0, 0, SWA_BLOCK)

        @pl.when(n == 0)
        def _():
            dsink_ref[...] = jnp.zeros_like(dsink_ref)

        groups, sink = _swa_operands(q_ref, kp_ref, kc_ref, vp_ref, vc_ref, sink_ref)
        dobs = [_swa_stack_heads(do_ref, g) for g in range(A_KV_HEADS)]
        p, in_cur, ps = _swa_probs(groups, sink, prev_off)
        ppb, pcb = [t.astype(BF16) for t in _swa_split(p, in_cur)]

        def per_group(fn):
            return jnp.concatenate([fn(g, slice(g * SWA_ROWS, (g + 1) * SWA_ROWS)) for g in range(A_KV_HEADS)], axis=0)

        out = per_group(lambda g, rows: jnp.dot(ppb[rows], groups[g][3], preferred_element_type=F32)
                        + jnp.dot(pcb[rows], groups[g][4], preferred_element_type=F32))
        delta = jnp.sum(jnp.concatenate(dobs, axis=0).astype(F32) * out, axis=-1, keepdims=True)
        dp = jnp.where(in_cur,
                       per_group(lambda g, rows: lax.dot_general(dobs[g], groups[g][4], NT_DIMS,
                                                                 preferred_element_type=F32)),
                       per_group(lambda g, rows: lax.dot_general(dobs[g], groups[g][3], NT_DIMS,
                                                                 preferred_element_type=F32)))
        dsp, dsc = [t.astype(BF16) for t in _swa_split(p * (dp - delta), in_cur)]
        dsink_ref[...] += (-ps * delta).reshape(dsink_ref.shape)
        for g, (qb, kp, kc, _, _) in enumerate(groups):
            rows = slice(g * SWA_ROWS, (g + 1) * SWA_ROWS)
            lanes = slice(g * LANES, (g + 1) * LANES)
            dq = (jnp.dot(dsp[rows], kp, preferred_element_type=F32)
                  + jnp.dot(dsc[rows], kc, preferred_element_type=F32)) * scale
            _swa_unstack_heads(dq_ref, g, dq)
            dkp_ref[:, lanes] = lax.dot_general(dsp[rows], qb, TN_DIMS, preferred_element_type=F32) * scale
            dkc_ref[:, lanes] = lax.dot_general(dsc[rows], qb, TN_DIMS, preferred_element_type=F32) * scale
            dvp_ref[:, lanes] = lax.dot_general(ppb[rows], dobs[g], TN_DIMS, preferred_element_type=F32)
            dvc_ref[:, lanes] = lax.dot_general(pcb[rows], dobs[g], TN_DIMS, preferred_element_type=F32)

        @pl.when(n == nblk - 1)
        def _():
            exchange_finish()

    kv_shape = jax.ShapeDtypeStruct(k.shape, F32)
    return pl.pallas_call(
        body, name="swa_bwd", grid=(nblk,),
        in_specs=[q_spec, prev_spec, cur_spec, prev_spec, cur_spec, sink_spec, q_spec] + [HBM_SPEC] * n_arr,
        out_specs=[q_spec, cur_spec, cur_spec, cur_spec, cur_spec, dsink_spec] + [HBM_SPEC] * n_arr,
        out_shape=[jax.ShapeDtypeStruct(q.shape, q.dtype), kv_shape, kv_shape, kv_shape, kv_shape,
                   jax.ShapeDtypeStruct((A_KV_HEADS, SWA_ROWS, 1), F32)]
                  + [jax.ShapeDtypeStruct(p.shape, p.dtype) for p in parts],
        scratch_shapes=_exchange_chips_sems(n_arr),
        compiler_params=_params("arbitrary"),
    )(q, k, k, v, v, sinks, do, *parts)


@jax.custom_vjp
def swa_nat(q, k, v, sinks, shards):
    out = _swa_nat_fwd_call(q, k, v, sinks, [s.astype(BF16) for s in shards])
    return out[0], tuple(out[1:])


def _swa_nat_fwd(q, k, v, sinks, shards):
    out = _swa_nat_fwd_call(q, k, v, sinks, [s.astype(BF16) for s in shards])
    return (out[0], tuple(out[1:])), (q, k, v, sinks)


def _swa_nat_bwd(res, cts):
    q, k, v, sinks = res
    do, d_gathered = cts
    out = _swa_nat_bwd_call(q, k, v, sinks, do, _reduce_scatter_head(d_gathered, "mid_grads"))
    dq, dkp, dkc, dvp, dvc, dsink = out[:6]

    def fold(prev_part, cur_part):
        shifted = jnp.concatenate([prev_part[SWA_BLOCK:], jnp.zeros_like(prev_part[:SWA_BLOCK])], axis=0)
        return (cur_part + shifted).astype(k.dtype)

    dsinks = jnp.sum(dsink.reshape(A_HEADS, SWA_BLOCK), axis=1)
    return dq, fold(dkp, dkc), fold(dvp, dvc), dsinks, _reduce_scatter_tail(out[6:], "mid_grads")


swa_nat.defvjp(_swa_nat_fwd, _swa_nat_bwd)

N_PAIR = B_HEADS // 2


def _flash_nat_fwd_call(q, k, v, shards):
    s = q.shape[0]
    t = min(FLASH_T, s)
    nb = s // t
    d = LANES
    n_arr = len(shards)

    def body(*refs):
        q_ref, k_ref, v_ref = refs[:3]
        shard_refs = refs[3:3 + n_arr]
        o_ref, lse_ref = refs[3 + n_arr:5 + n_arr]
        gathered_refs = refs[5 + n_arr:5 + 2 * n_arr]
        vt_ref, m_ref, l_ref, acc_ref = refs[5 + 2 * n_arr:9 + 2 * n_arr]
        pair, i = pl.program_id(0), pl.program_id(1)
        ag_start, ag_forward, ag_finish = _allgather_phases(shard_refs, gathered_refs, *refs[9 + 2 * n_arr:])

        @pl.when((pair == 0) & (i == 0))
        def _():
            ag_start()

        @pl.when((pair == N_PAIR // 2) & (i == 0))
        def _():
            ag_forward()

        @pl.when(i == 0)
        def _():
            for hh in range(2):
                for chunk in range(nb):
                    rows = slice(chunk * t, (chunk + 1) * t)
                    vt_ref[hh, :, rows] = v_ref[rows, hh * d:(hh + 1) * d].T

        m_ref[...] = jnp.full_like(m_ref, -jnp.inf)
        l_ref[...] = jnp.zeros_like(l_ref)
        acc_ref[...] = jnp.zeros_like(acc_ref)

        def step(j, on_diagonal):
            keys = pl.ds(pl.multiple_of(j * t, t), t)
            scores = [lax.dot_general(k_ref[keys, hh * d:(hh + 1) * d], q_ref[:, hh * d:(hh + 1) * d], NT_DIMS,
                                      preferred_element_type=F32) for hh in range(2)]
            for hh in range(2):
                sc_t = scores[hh]
                if on_diagonal:
                    key = lax.broadcasted_iota(jnp.int32, (t, t), 0)
                    qry = lax.broadcasted_iota(jnp.int32, (t, t), 1)
                    sc_t = jnp.where(qry >= key, sc_t, -jnp.inf)
                m_old = m_ref[hh]
                m_new = jnp.maximum(m_old, jnp.max(sc_t, axis=0, keepdims=True))
                alpha = jnp.exp2((m_old - m_new) * EXP2_SCALE)
                p_t = jnp.exp2((sc_t - m_new) * EXP2_SCALE)
                l_ref[hh] = alpha * l_ref[hh] + jnp.sum(p_t, axis=0, keepdims=True)
                acc_ref[hh] = alpha * acc_ref[hh] + jnp.dot(vt_ref[hh, :, keys], p_t.astype(BF16),
                                                            preferred_element_type=F32)
                m_ref[hh] = m_new

        def below(j, carry):
            step(j, False)
            return carry

        lax.fori_loop(0, i, below, 0)
        step(i, True)
        outs =[(acc_ref[hh] / l_ref[hh]).T for hh in range(2)]
        for hh in range(2):
            lse_ref[hh] = m_ref[hh] * EXP2_SCALE + jnp.log2(l_ref[hh])
        o_ref[...] = (outs[0] + pltpu.roll(outs[1], HALF, 1)).astype(o_ref.dtype)

        @pl.when((pair == N_PAIR - 1) & (i == nb - 1))
        def _():
            ag_finish()

    return pl.pallas_call(
        body, name="mla_fwd", grid=(N_PAIR, nb),
        in_specs=[pl.BlockSpec((t, 2 * d), lambda p, i: (i, p)),
                  pl.BlockSpec((s, 2 * d), lambda p, i: (0, p)),
                  pl.BlockSpec((s, 2 * d), lambda p, i: (0, p))] + [HBM_SPEC] * n_arr,
        out_specs=[pl.BlockSpec((t, d), lambda p, i: (i, p)),
                   pl.BlockSpec((2, 1, t), lambda p, i: (p, 0, i))] + [HBM_SPEC] * n_arr,
        out_shape=[jax.ShapeDtypeStruct((s, N_PAIR * d), BF16), jax.ShapeDtypeStruct((B_HEADS, 1, s), F32)]
                  + _allgather_out_shapes(shards),
        scratch_shapes=[pltpu.VMEM((2, d, s), BF16), pltpu.VMEM((2, 1, t), F32), pltpu.VMEM((2, 1, t), F32),
                        pltpu.VMEM((2, d, t), F32)] + _allgather_sems(n_arr),
        compiler_params=_params("arbitrary", "arbitrary"),
    )(q, k, v, *shards)


def _flash_nat_delta_call(o, do):
    s, w = o.shape
    t = min(FLASH_T, s)

    def body(o_ref, do_ref, out_ref):
        prod = o_ref[...].astype(F32) * do_ref[...].astype(F32)
        lane = lax.broadcasted_iota(jnp.int32, (w, LANES), 0) // V_DIM
        head = lax.broadcasted_iota(jnp.int32, (w, LANES), 1)
        out_ref[...] = jnp.dot(prod, (lane == head).astype(F32), precision=lax.Precision.HIGHEST,
                               preferred_element_type=F32)

    spec = pl.BlockSpec((t, w), lambda i: (i, 0))
    return pl.pallas_call(
        body, name="mla_delta", grid=(s // t,), in_specs=[spec, spec],
        out_specs=pl.BlockSpec((t, LANES), lambda i: (i, 0)),
        out_shape=jax.ShapeDtypeStruct((s, LANES), F32), compiler_params=_params("parallel"),
    )(o, do)


def _flash_nat_bwd_call(q, k, v, lse_row, delta_row, do, parts):
    s = q.shape[0]
    t = min(FLASH_T, s)
    nb = s // t
    d = LANES
    n_arr = len(parts)

    def body(*refs):
        q_ref, k_ref, v_ref, lse_ref, delta_ref, do_ref = refs[:6]
        part_refs = refs[6:6 + n_arr]
        dq_ref, dk_ref, dv_ref = refs[6 + n_arr:9 + n_arr]
        received_refs = refs[9 + n_arr:9 + 2 * n_arr]
        dq_acc, dk_acc, dv_acc = refs[9 + 2 * n_arr:12 + 2 * n_arr]
        pair, j = pl.program_id(0), pl.program_id(1)
        exchange_start, exchange_finish = _exchange_chips_phases(part_refs, received_refs, *refs[12 + 2 * n_arr:])

        @pl.when((pair == 0) & (j == 0))
        def _():
            exchange_start()

        @pl.when(j == 0)
        def _():
            dq_acc[...] = jnp.zeros_like(dq_acc)

        for hh in range(2):
            kb, vb = k_ref[:, hh * d:(hh + 1) * d], v_ref[:, hh * d:(hh + 1) * d]
            dk_acc[...] = jnp.zeros_like(dk_acc)
            dv_acc[...] = jnp.zeros_like(dv_acc)

            def step(i, on_diagonal, hh=hh, kb=kb, vb=vb):
                rows = pl.ds(pl.multiple_of(i * t, t), t)
                qb = q_ref[rows, hh * d:(hh + 1) * d]
                do_pair = do_ref[rows, :].astype(F32)
                do_h = do_pair if hh == 0 else pltpu.roll(do_pair, HALF, 1)
                dob = jnp.where(_low_half(do_h.shape), do_h, 0.0).astype(BF16)
                sc_t = lax.dot_general(kb, qb, NT_DIMS, preferred_element_type=F32)
                p_t = jnp.exp2(sc_t * EXP2_SCALE - lse_ref[hh, :, rows])
                if on_diagonal:
                    key = lax.broadcasted_iota(jnp.int32, (t, t), 0)
                    qry = lax.broadcasted_iota(jnp.int32, (t, t), 1)
                    p_t = jnp.where(qry >= key, p_t, 0.0)
                dp_t = lax.dot_general(vb, dob, NT_DIMS, preferred_element_type=F32)
                ds_t = (p_t * (dp_t - delta_ref[hh, :, rows])).astype(BF16)
                dv_acc[...] += jnp.dot(p_t.astype(BF16), dob, preferred_element_type=F32)
                dk_acc[...] += jnp.dot(ds_t, qb, preferred_element_type=F32)
                dq_acc[hh, rows, :] += lax.dot_general(ds_t, kb, TN_DIMS, preferred_element_type=F32)

            def above(i, carry, step=step):
                step(i, False)
                return carry

            step(j, True)
            lax.fori_loop(j + 1, nb, above, 0)
            dk_ref[:, hh * d:(hh + 1) * d] = (dk_acc[...] * MLA_SCALE).astype(dk_ref.dtype)
            dv_ref[:, hh * d:(hh + 1) * d] = dv_acc[...].astype(dv_ref.dtype)

        @pl.when(j == nb - 1)
        def _():
            for hh in range(2):
                dq_ref[:, hh * d:(hh + 1) * d] = (dq_acc[hh] * MLA_SCALE).astype(dq_ref.dtype)

        @pl.when((pair == N_PAIR - 1) & (j == nb - 1))
        def _():
            exchange_finish()

    full_spec = pl.BlockSpec((s, 2 * d), lambda p, j: (0, p))
    tile_spec = pl.BlockSpec((t, 2 * d), lambda p, j: (j, p))
    row_spec = pl.BlockSpec((2, 1, s), lambda p, j: (p, 0, 0))
    return pl.pallas_call(
        body, name="mla_bwd", grid=(N_PAIR, nb),
        in_specs=[full_spec, tile_spec, tile_spec, row_spec, row_spec, pl.BlockSpec((s, d), lambda p, j: (0, p))]
                 + [HBM_SPEC] * n_arr,
        out_specs=[full_spec, tile_spec, tile_spec] + [HBM_SPEC] * n_arr,
        out_shape=[jax.ShapeDtypeStruct(q.shape, q.dtype)] * 3 + [jax.ShapeDtypeStruct(p.shape, p.dtype) for p in parts],
        scratch_shapes=[pltpu.VMEM((2, s, d), F32), pltpu.VMEM((t, d), F32), pltpu.VMEM((t, d), F32)]
                       + _exchange_chips_sems(n_arr),
        compiler_params=_params("arbitrary", "arbitrary"),
    )(q, k, v, lse_row, delta_row, do, *parts)


def _reduce_scatter_head(cts, tag):
    received = _exchange_sibling(list(cts), tag + "_exchange_sibling")
    my_c = lax.axis_index("c").astype(jnp.int32).reshape(1)
    return [_pair_add(m, r, my_c, "%s_pair_add_%d" % (tag, i)) for i, (m, r) in enumerate(zip(cts, received))]


def _reduce_scatter_tail(chip_parts, tag):
    return tuple(_sum_blocks(r, "%s_sum_%d" % (tag, i)) for i, r in enumerate(chip_parts))


@jax.custom_vjp
def flash_nat(q, k, v, shards):
    out = _flash_nat_fwd_call(q, k, v, [s.astype(BF16) for s in shards])
    return out[0], tuple(out[2:])


def _flash_nat_fwd(q, k, v, shards):
    out = _flash_nat_fwd_call(q, k, v, [s.astype(BF16) for s in shards])
    return (out[0], tuple(out[2:])), (q, k, v, out[0], out[1])


def _flash_nat_bwd(res, cts):
    q, k, v, o, lse = res
    do, d_gathered = cts
    delta = _flash_nat_delta_call(o, do)[:, :B_HEADS].T.reshape(B_HEADS, 1, q.shape[0])
    out = _flash_nat_bwd_call(q, k, v, lse, delta, do, _reduce_scatter_head(d_gathered, "mlp_grads"))
    return out[0], out[1], out[2], _reduce_scatter_tail(out[3:], "mlp_grads")


flash_nat.defvjp(_flash_nat_fwd, _flash_nat_bwd)


HBM_SPEC = pl.BlockSpec(memory_space=pltpu.HBM)


def _allgather(shards, name):
    n_arr = len(shards)

    def body(*refs):
        start, forward, finish = _allgather_phases(refs[:n_arr], refs[n_arr:2 * n_arr], *refs[2 * n_arr:])
        start()
        forward()
        finish()

    return pl.pallas_call(
        body, name=name, out_shape=_allgather_out_shapes(shards),
        in_specs=[HBM_SPEC] * n_arr, out_specs=[HBM_SPEC] * n_arr,
        scratch_shapes=_allgather_sems(n_arr),
    )(*shards)


def _allgather_out_shapes(shards):
    return [jax.ShapeDtypeStruct((N_DEV,) + s.shape, s.dtype) for s in shards]


def _allgather_sems(n_arr):
    return [pltpu.SemaphoreType.DMA((7, n_arr)), pltpu.SemaphoreType.DMA((7, n_arr)), pltpu.SemaphoreType.DMA((n_arr,))]


def _allgather_phases(x_refs, out_refs, send_sems, recv_sems, local_sems):
    arrays = range(len(x_refs))
    x, y, c = lax.axis_index("x"), lax.axis_index("y"), lax.axis_index("c")
    me, sibling = (x, y, c), (x, y, 1 - c)
    chips = [(1 - x, y), (x, 1 - y), (1 - x, 1 - y)]

    def rows(a, px, py, pc):
        return out_refs[a].at[4 * px + 2 * py + pc]

    def copy(a, k, block, to, src=None):
        return pltpu.make_async_remote_copy(
            src_ref=rows(a, *block) if src is None else src, dst_ref=rows(a, *block),
            send_sem=send_sems.at[k, a], recv_sem=recv_sems.at[k, a], device_id=to, device_id_type=MESH_ID)

    def mine():
        return [pltpu.make_async_copy(x_refs[a], rows(a, *me), local_sems.at[a]) for a in arrays]

    def first():
        return [cp for a in arrays for cp in
                [copy(a, 0, me, sibling, src=x_refs[a])]
                + [copy(a, 1 + j, me, (*chip, c), src=x_refs[a]) for j, chip in enumerate(chips)]]

    def passed():
        return [copy(a, 4 + j, (*chip, c), sibling) for j, chip in enumerate(chips) for a in arrays]

    def start():
        for cp in mine() + first():
            cp.start()

    def forward():
        for j, chip in enumerate(chips):
            for a in arrays:
                copy(a, 1 + j, (*chip, c), me).wait_recv()
                copy(a, 4 + j, (*chip, c), sibling).start()

    def finish():
        for a in arrays:
            copy(a, 0, sibling, me).wait_recv()
        for j, chip in enumerate(chips):
            for a in arrays:
                copy(a, 4 + j, (*chip, 1 - c), me).wait_recv()
        for cp in first() + passed():
            cp.wait_send()
        for cp in mine():
            cp.wait()

    return start, forward, finish


N_CHIP = 4


def _exchange_sibling(parts, name):
    n_arr = len(parts)

    def body(*refs):
        in_refs, recv_refs = refs[:n_arr], refs[n_arr:2 * n_arr]
        send_sems, recv_sems = refs[2 * n_arr:]
        x, y, c = lax.axis_index("x"), lax.axis_index("y"), lax.axis_index("c")
        copies = []
        for a in range(n_arr):
            for q in range(N_CHIP):
                copies.append(pltpu.make_async_remote_copy(
                    src_ref=in_refs[a].at[2 * q + 1 - c], dst_ref=recv_refs[a].at[q],
                    send_sem=send_sems.at[q, a], recv_sem=recv_sems.at[q, a],
                    device_id=(x, y, 1 - c), device_id_type=MESH_ID))
        for cp in copies:
            cp.start()
        for cp in copies:
            cp.wait()

    return pl.pallas_call(
        body, name=name, out_shape=[jax.ShapeDtypeStruct((N_CHIP,) + p.shape[1:], p.dtype) for p in parts],
        in_specs=[HBM_SPEC] * n_arr, out_specs=[HBM_SPEC] * n_arr,
        scratch_shapes=[pltpu.SemaphoreType.DMA((N_CHIP, n_arr)), pltpu.SemaphoreType.DMA((N_CHIP, n_arr))],
    )(*parts)


def _exchange_chips_sems(n_arr):
    return [pltpu.SemaphoreType.DMA((N_CHIP - 1, n_arr)), pltpu.SemaphoreType.DMA((N_CHIP - 1, n_arr)),
            pltpu.SemaphoreType.DMA((n_arr,))]


def _exchange_chips_phases(in_refs, out_refs, send_sems, recv_sems, local_sems):
    n_arr = len(in_refs)
    x, y, c = lax.axis_index("x"), lax.axis_index("y"), lax.axis_index("c")
    me = 2 * x + y

    def copies():
        out = [pltpu.make_async_copy(in_refs[a].at[me], out_refs[a].at[me], local_sems.at[a]) for a in range(n_arr)]
        for k in range(1, N_CHIP):
            px = 1 - x if k & 2 else x
            py = 1 - y if k & 1 else y
            for a in range(n_arr):
                out.append(pltpu.make_async_remote_copy(
                    src_ref=in_refs[a].at[2 * px + py], dst_ref=out_refs[a].at[me],
                    send_sem=send_sems.at[k - 1, a], recv_sem=recv_sems.at[k - 1, a],
                    device_id=(px, py, c), device_id_type=MESH_ID))
        return out

    def start():
        for cp in copies():
            cp.start()

    def finish():
        for cp in copies():
            cp.wait()

    return start, finish


def _row_tile(r, ccols, blocks):
    cap = max(16, (2 * 1024 * 1024) // (4 * ccols * blocks))
    return _pick(r, cap, 16)


def _pair_add(mine, theirs, my_c, name):
    _, r, ccols = mine.shape
    tr = _row_tile(r, ccols, 1)

    def body(c_ref, a_ref, b_ref, o_ref):
        o_ref[...] = (a_ref[...].astype(F32) + b_ref[...].astype(F32)).astype(o_ref.dtype)

    spec = pl.BlockSpec((None, tr, ccols), lambda q, i, c_ref: (q, i, 0))
    return pl.pallas_call(
        body, name=name,
        grid_spec=pltpu.PrefetchScalarGridSpec(
            num_scalar_prefetch=1, grid=(N_CHIP, r // tr),
            in_specs=[pl.BlockSpec((None, tr, ccols), lambda q, i, c_ref: (2 * q + c_ref[0], i, 0)), spec],
            out_specs=spec),
        out_shape=jax.ShapeDtypeStruct(theirs.shape, theirs.dtype),
        compiler_params=_params("parallel", "parallel"),
    )(my_c, mine, theirs)


def _sum_blocks(parts, name):
    nb, r, ccols = parts.shape
    tr = _row_tile(r, ccols, nb)

    def body(p_ref, o_ref):
        acc = p_ref[0].astype(F32)
        for i in range(1, nb):
            acc = acc + p_ref[i].astype(F32)
        o_ref[...] = acc

    return pl.pallas_call(
        body, name=name, grid=(r // tr,),
        in_specs=[pl.BlockSpec((nb, tr, ccols), lambda i: (0, i, 0))],
        out_specs=pl.BlockSpec((tr, ccols), lambda i: (i, 0)),
        out_shape=jax.ShapeDtypeStruct((r, ccols), F32),
        compiler_params=_params("parallel"),
    )(parts)


@jax.custom_vjp
def replicated(vec):
    return vec


def _replicated_fwd(vec):
    return vec, None


def _replicated_bwd(_, ct):
    return (_sum_blocks(_allgather([ct], "small_grad_allgather")[0], "small_grad_sum"),)


replicated.defvjp(_replicated_fwd, _replicated_bwd)


def _adamw(w, g, m, v, name):
    rows, cols = w.shape
    tr = _pick(rows, 256, 8) if rows % 8 == 0 else rows

    def body(w_ref, g_ref, m_ref, v_ref, d_ref, nm_ref, nv_ref):
        g_ = g_ref[...]
        m_ = ADAM_B1 * m_ref[...] + (1.0 - ADAM_B1) * g_
        v_ = ADAM_B2 * v_ref[...] + (1.0 - ADAM_B2) * jnp.square(g_)
        m_hat = m_ / (1.0 - ADAM_B1 ** ADAM_STEP)
        v_hat = v_ / (1.0 - ADAM_B2 ** ADAM_STEP)
        d_ref[...] = -ADAM_LR * (m_hat / (jnp.sqrt(v_hat) + ADAM_EPS) + ADAM_WD * w_ref[...])
        nm_ref[...] = m_
        nv_ref[...] = v_

    spec = pl.BlockSpec((tr, cols), lambda i: (i, 0))
    return pl.pallas_call(
        body, name=name, grid=(rows // tr,), in_specs=[spec] * 4, out_specs=[spec] * 3,
        out_shape=[jax.ShapeDtypeStruct(w.shape, F32)] * 3, compiler_params=_params("parallel"),
    )(w, g, m, v)


COL_SHARDED = ("w_in", "w_uq", "w_ukv", "w_branch_a", "w_branch_b", "w_up", "w_ple")
EARLY = ("w_in",)
MID = ("w_uq", "w_ukv", "w_branch_a", "w_branch_b", "w_out")
LATE = ("w_up", "w_down", "w_ple_gate", "w_ple")
SMALL = ("attn_pre_norm", "attn_post_norm", "b_gate", "q_a_norm", "kv_a_norm", "mlp_pre_norm", "mlp_post_norm",
         "conv_b", "ple_norm", "sinks")
SMALL_COLS = 128


def _pack_rows(arrays, cols, row_mult):
    flat = jnp.concatenate([a.reshape(-1) for a in arrays])
    pad = (-flat.shape[0]) % (cols * row_mult)
    return jnp.pad(flat, (0, pad)).reshape(-1, cols)


def _unpack_small(vec, shapes):
    flat = vec.reshape(-1)
    out, off = {}, 0
    for name in SMALL:
        n = shapes[name]
        out[name] = flat[off:off + n].reshape(1, n)
        off += n + (-n) % SMALL_COLS
    return out


def _pad_lanes(t, width):
    return jnp.pad(t, [(0, 0)] * (t.ndim - 1) + [(0, width - t.shape[-1])])


def _pad_rows(t, rows):
    return jnp.pad(t, [(0, 0)] * (t.ndim - 2) + [(0, rows - t.shape[-2]), (0, 0)])


FRONT_SIZES = (512, 128, 128, 256, 128)
FRONT_BOUNDS = (0, 512, 640, 768, 1024, 1152, 1280)
PE_LANE = NOPE_DIM


def _arrange_w_in_t(wt):
    k = wt.shape[1]
    n_front = sum(FRONT_SIZES)
    front, kr, gates = wt[:n_front], wt[n_front:n_front + ROPE_DIM], wt[n_front + ROPE_DIM:]
    kr_slab = jnp.concatenate([jnp.zeros((PE_LANE, k), wt.dtype), kr,
                               jnp.zeros((HEAD_PAD - PE_LANE - ROPE_DIM, k), wt.dtype)], axis=0)
    return jnp.concatenate([front, kr_slab], axis=0), gates


def _arrange_w_uq_t(wt):
    k = wt.shape[1]
    return _pad_rows(wt.reshape(B_HEADS, NOPE_DIM + ROPE_DIM, k), HEAD_PAD).reshape(B_HEADS * HEAD_PAD, k)


def _arrange_w_ukv_t(wt):
    k = wt.shape[1]
    w = wt.reshape(B_HEADS, 2, NOPE_DIM, k)
    slabs = [_pad_rows(w[:, part], HEAD_PAD).reshape(B_HEADS * HEAD_PAD, k) for part in range(2)]
    return jnp.concatenate(slabs, axis=0)


def _rope_tables(positions, s):
    pos = positions.reshape(s, 1).astype(F32)

    def angles(dim):
        return pos * ROPE_THETA ** (-(jnp.arange(0, dim, 2, dtype=F32) / dim))

    cos_a, sin_a = jnp.cos(angles(A_HEAD_DIM)), jnp.sin(angles(A_HEAD_DIM))
    zero_a = jnp.zeros_like(sin_a)
    tables_a = [jnp.tile(jnp.concatenate(pair, axis=1), (1, LANES // A_HEAD_DIM))
                for pair in ((cos_a, cos_a), (-sin_a, zero_a), (zero_a, sin_a))]
    cos_b, sin_b = jnp.cos(angles(ROPE_DIM)), jnp.sin(angles(ROPE_DIM))
    zero_b = jnp.zeros_like(sin_b)

    def slab(first, second, fill):
        return jnp.concatenate([jnp.full((s, PE_LANE), fill, F32), first, second,
                                jnp.full((s, HEAD_PAD - PE_LANE - ROPE_DIM), fill, F32)], axis=1)

    tables_b = [slab(cos_b, cos_b, 1.0), slab(-sin_b, zero_b, 0.0), slab(zero_b, sin_b, 0.0)]
    return tables_a + tables_b


def _local_loss(wts, x, p, tables, target):
    s = x.shape[0]
    small_shapes = {n: wts[n].shape[-1] for n in SMALL}
    small_vec = _pack_rows([_pad_lanes(wts[n].reshape(1, -1), small_shapes[n] + (-small_shapes[n]) % SMALL_COLS)
                            for n in SMALL], SMALL_COLS, 8)
    sm = _unpack_small(replicated(small_vec), small_shapes)
    def shard(n):
        return wts[n].T if n in COL_SHARDED else wts[n]

    h1_front, h1_gates, x_res, gathered = prenorm_gather(
        x, sm["attn_pre_norm"], tuple([shard(n) for n in EARLY] + [_pack_rows([wts["conv_w"]], SMALL_COLS, 8)]),
        (BF16,) * len(EARLY) + (F32,))
    big = {n: g.reshape(-1, g.shape[2]) for n, g in zip(EARLY, gathered)}
    ch = wts["conv_w"].shape[1]
    conv_w = gathered[-1].reshape(N_DEV, -1)[:, :CONV_W * ch].reshape(N_DEV, CONV_W, ch)
    conv_w = conv_w.transpose(1, 0, 2).reshape(CONV_W, N_DEV * ch)

    w_front_t, w_gates_t = _arrange_w_in_t(big["w_in"])
    tables_a, tables_b = tables[:3], tables[3:]

    qa, ka, va, cqn, ckvn, kpe = proj_stage(
        "prep", _f_prep, [(h1_front, w_front_t, "nt", "w_front", True, F32)], params=[sm["q_a_norm"], sm["kv_a_norm"]],
        consts=tables, splits=[FRONT_BOUNDS], ts=512, out_dtypes=[BF16, BF16, BF16, BF16, BF16, F32])
    ya, mid = swa_nat(qa, ka, va, sm["sinks"].reshape(-1), tuple(shard(n) for n in MID))
    big.update({n: g.reshape(-1, g.shape[2]) for n, g in zip(MID, mid)})

    (q2,) = proj_stage("qrope", _f_qrope, [(cqn, _arrange_w_uq_t(big["w_uq"]), "nt", "w_uq", True, BF16)],
                       consts=tables_b, ts=512, out_dtypes=[BF16])
    k2, v2 = proj_stage("kv", _f_kv, [(ckvn, _arrange_w_ukv_t(big["w_ukv"]), "nt", "w_ukv", True, BF16)],
                        extra=[kpe], splits=[(0, B_HEADS * HEAD_PAD, 2 * B_HEADS * HEAD_PAD), None], ts=512,
                        out_dtypes=[BF16, BF16])
    yb, late = flash_nat(q2, k2, v2, tuple(shard(n) for n in LATE))
    big.update({n: g.reshape(-1, g.shape[2]) for n, g in zip(LATE, late)})

    (mixed,) = proj_stage(
        "gate", _f_gate, [(h1_gates, w_gates_t, "nt", "w_gates", True, F32),
                          (ya, big["w_branch_a"], "nt", "w_branch_a", True, BF16),
                          (yb, big["w_branch_b"], "nt", "w_branch_b", True, BF16)],
        params=[sm["b_gate"][:, :D_MODEL], sm["b_gate"][:, D_MODEL:]],
        splits=[(0, D_MODEL, 2 * D_MODEL), None, None], out_dtypes=[BF16])
    x1, h2 = proj_stage("post_attn", _f_post, [(mixed, big["w_out"], "nn", "w_out", True, F32)], extra=[x_res],
                        params=[sm["attn_post_norm"], sm["mlp_pre_norm"]], ts=512, out_dtypes=[F32, BF16])

    act = mlp_up(h2, big["w_up"], conv_w, sm["conv_b"])
    x2, h3 = proj_stage("post_mlp", _f_post, [(act, big["w_down"], "nn", "w_down", True, F32)], extra=[x1],
                        params=[sm["mlp_post_norm"], sm["ple_norm"]], ts=512, out_dtypes=[F32, BF16])

    (rowloss,) = proj_stage("loss", _f_out, [(h3, big["w_ple_gate"], "nn", "w_ple_gate", True, F32),
                                             (p, big["w_ple"], "nt", "w_ple", False, BF16)], extra=[x2],
                            consts=[target], ts=512)
    return jnp.sum(rowloss)


WEIGHTS = ["attn_pre_norm", "attn_post_norm", "w_in", "b_gate", "sinks", "q_a_norm", "w_uq", "kv_a_norm", "w_ukv",
           "w_branch_a", "w_branch_b", "w_out", "mlp_pre_norm", "mlp_post_norm", "w_up", "conv_w", "conv_b",
           "w_down", "ple_norm", "w_ple_gate", "w_ple"]


def kernel(x, p, positions, attn_pre_norm, attn_post_norm, w_in, b_gate, sinks, q_a_norm, w_uq, kv_a_norm, w_ukv, w_branch_a, w_branch_b, w_out, mlp_pre_norm, mlp_post_norm, w_up, conv_w, conv_b, w_down, ple_norm, w_ple_gate, w_ple, loss_target, m_attn_pre_norm, m_attn_post_norm, m_w_in, m_b_gate, m_sinks, m_q_a_norm, m_w_uq, m_kv_a_norm, m_w_ukv, m_w_branch_a, m_w_branch_b, m_w_out, m_mlp_pre_norm, m_mlp_post_norm, m_w_up, m_conv_w, m_conv_b, m_w_down, m_ple_norm, m_w_ple_gate, m_w_ple, v_attn_pre_norm, v_attn_post_norm, v_w_in, v_b_gate, v_sinks, v_q_a_norm, v_w_uq, v_kv_a_norm, v_w_ukv, v_w_branch_a, v_w_branch_b, v_w_out, v_mlp_pre_norm, v_mlp_post_norm, v_w_up, v_conv_w, v_conv_b, v_w_down, v_ple_norm, v_w_ple_gate, v_w_ple):
    given = dict(locals())
    s = x.shape[1]
    wts = {n: given[n][0] if given[n].ndim == 3 else given[n] for n in WEIGHTS}
    tables = _rope_tables(positions, s)
    local_loss, (grads, grad_x) = jax.value_and_grad(_local_loss, argnums=(0, 1))(
        wts, x[0], p[0, 0], tables, loss_target[0])
    loss = lax.psum(local_loss, AXES)

    outs = {"grad": [], "delta": [], "m": [], "v": []}
    for n in WEIGHTS:
        shape = given[n].shape
        w2 = wts[n].reshape(-1, shape[-1])
        g2 = grads[n].reshape(w2.shape)
        delta, new_m, new_v = _adamw(w2, g2, given["m_" + n].reshape(w2.shape), given["v_" + n].reshape(w2.shape),
                                     "adamw_" + n)
        outs["grad"].append(g2.reshape(shape))
        outs["delta"].append(delta.reshape(shape))
        outs["m"].append(new_m.reshape(shape))
        outs["v"].append(new_v.reshape(shape))
    return (loss, grad_x[None], *outs["grad"], *outs["delta"], *outs["m"], *outs["v"])
```

```python
import functools

import numpy as np
import jax
import jax.numpy as jnp
from jax import lax
from jax.experimental import pallas as pl
from jax.experimental.pallas import tpu as pltpu

F32 = jnp.float32
BF16 = jnp.bfloat16
MESH_ID = pl.DeviceIdType.MESH
AXES = ("x", "y", "c")
N_DEV = 8

D_MODEL = 1024
RMS_EPS = 1e-6
ROPE_THETA = 10000.0
SWA_BLOCK = 128
A_HEADS, A_KV_HEADS, A_HEAD_DIM = 8, 2, 64
A_GROUP = A_HEADS // A_KV_HEADS
B_HEADS, Q_LORA, KV_LORA, NOPE_DIM, ROPE_DIM, V_DIM = 8, 256, 128, 64, 32, 64
D_FF = 2816
CONV_W = 3
HEAD_PAD = 128

ADAM_LR, ADAM_B1, ADAM_B2, ADAM_EPS, ADAM_WD, ADAM_STEP = 0.001, 0.9, 0.999, 1e-08, 0.01, 10

VMEM_LIMIT = 48 * 1024 * 1024
MM_TM, MM_TN, MM_TK_TOKENS = 1024, 1408, 2048
MM_VMEM_BUDGET = 36 * 1024 * 1024
FLASH_T = 1024
CONV_TS = 256
CONV_CHUNK = 256


def _params(*sem):
    return pltpu.CompilerParams(dimension_semantics=sem, vmem_limit_bytes=VMEM_LIMIT)


def _pick(dim, cap, mult):
    best = None
    for t in range(mult, min(dim, cap) + 1, mult):
        if dim % t == 0:
            best = t
    return dim if best is None else best


def _divisors(dim, mult):
    return [t for t in range(mult, dim + 1, mult) if dim % t == 0] or [dim]


def _matmul_tiles(m, n, kdim, form, sizes):
    sa, sb, so = sizes
    tk = _pick(kdim, MM_TK_TOKENS, 128) if form == "tn" else kdim
    cap_m = MM_TN if form == "tn" else MM_TM
    best = None
    for tm in _divisors(m, 128):
        for tn in _divisors(n, 128):
            need = 2 * (tm * tk * sa + tk * tn * sb + tm * tn * so) + (tm * tn * 4 if tk != kdim else 0)
            if tm > cap_m or tn > MM_TN or need > MM_VMEM_BUDGET:
                continue
            if best is None or (tm * tn, tm) > (best[0] * best[1], best[0]):
                best = (tm, tn)
    return best[0], best[1], tk


def _matmul(a, b, form, *, out_dtype=F32, name):
    if form == "tn":
        (kdim, m), n = a.shape, b.shape[1]
    else:
        (m, kdim), n = a.shape, (b.shape[1] if form == "nn" else b.shape[0])
    sizes = (a.dtype.itemsize, b.dtype.itemsize, jnp.dtype(out_dtype).itemsize)
    tm, tn, tk = _matmul_tiles(m, n, kdim, form, sizes)
    nk = kdim // tk
    rows_outer = nk > 1 or (m // tm) * b.size * sizes[1] <= (n // tn) * a.size * sizes[0]

    def ij(fn):
        return (lambda i, j, k: fn(i, j, k)) if rows_outer else (lambda j, i, k: fn(i, j, k))

    a_spec = (pl.BlockSpec((tk, tm), ij(lambda i, j, k: (k, i))) if form == "tn"
              else pl.BlockSpec((tm, tk), ij(lambda i, j, k: (i, k))))
    b_spec = (pl.BlockSpec((tn, tk), ij(lambda i, j, k: (j, k))) if form == "nt"
              else pl.BlockSpec((tk, tn), ij(lambda i, j, k: (k, j))))
    dims = (((0 if form == "tn" else 1,), (1 if form == "nt" else 0,)), ((), ()))

    def product(a_ref, b_ref):
        return lax.dot_general(a_ref[...].astype(BF16), b_ref[...].astype(BF16), dims, preferred_element_type=F32)

    if nk == 1:
        def body(a_ref, b_ref, o_ref):
            o_ref[...] = product(a_ref, b_ref).astype(o_ref.dtype)

        scratch = []
    else:
        def body(a_ref, b_ref, o_ref, acc_ref):
            k = pl.program_id(2)

            @pl.when(k == 0)
            def _():
                acc_ref[...] = jnp.zeros_like(acc_ref)

            acc_ref[...] += product(a_ref, b_ref)

            @pl.when(k == nk - 1)
            def _():
                o_ref[...] = acc_ref[...].astype(o_ref.dtype)

        scratch = [pltpu.VMEM((tm, tn), F32)]

    return pl.pallas_call(
        body, name=name, grid=(m // tm, n // tn, nk) if rows_outer else (n // tn, m // tm, nk),
        in_specs=[a_spec, b_spec],
        out_specs=pl.BlockSpec((tm, tn), ij(lambda i, j, k: (i, j))),
        out_shape=jax.ShapeDtypeStruct((m, n), out_dtype),
        scratch_shapes=scratch,
        compiler_params=_params("parallel", "parallel", "arbitrary"),
    )(a, b)


def _pairs(bounds):
    return list(zip(bounds[:-1], bounds[1:]))


def _split(v, bounds):
    return [v[:, a:b] for a, b in _pairs(bounds)]


def _stage_build(name, f, tiled, params, consts, splits, ts, out_dtypes, ct_dtypes=None):
    n_t, n_p, n_c = len(tiled), len(params), len(consts)
    ct_dtypes = [t.dtype for t in tiled] if ct_dtypes is None else ct_dtypes
    s = tiled[0].shape[0]
    ts = min(ts, s)
    grid = (s // ts,)
    if splits is None:
        splits = [None] * n_t
    in_bounds = [(0, t.shape[1]) if b is None else tuple(b) for t, b in zip(tiled, splits)]

    def tile_aval(arr):
        return jax.ShapeDtypeStruct((ts, arr.shape[1]), arr.dtype)

    slab_avals = [[jax.ShapeDtypeStruct((ts, e - a), F32) for a, e in _pairs(b)]
                  for t, b in zip(tiled, in_bounds)]
    out_avals = jax.eval_shape(f, slab_avals, list(params), [tile_aval(c) for c in consts])
    out_bounds = [tuple(np.cumsum([0] + [o.shape[1] for o in slabs]).tolist()) for slabs in out_avals]
    out_dtypes = [F32] * len(out_bounds) if out_dtypes is None else out_dtypes
    out_shapes = [jax.ShapeDtypeStruct((s, b[-1]), d) for b, d in zip(out_bounds, out_dtypes)]

    def row_spec(width):
        return pl.BlockSpec((ts, width), lambda i: (i, 0))

    def par_spec(arr):
        return pl.BlockSpec(arr.shape, lambda i: (0, 0))

    in_specs = ([row_spec(t.shape[1]) for t in tiled] + [par_spec(p) for p in params]
                + [row_spec(c.shape[1]) for c in consts])

    def load(refs):
        t = [_split(r[...].astype(F32), b) for r, b in zip(refs[:n_t], in_bounds)]
        p = [r[...] for r in refs[n_t:n_t + n_p]]
        c = [r[...] for r in refs[n_t + n_p:n_t + n_p + n_c]]
        return t, p, c

    def store(refs, values, bounds):
        for ref, slabs, b in zip(refs, values, bounds):
            for v, (a, e) in zip(slabs, _pairs(b)):
                ref[:, a:e] = v.astype(ref.dtype)

    def run_fwd(tiled, params, consts):
        def body(*refs):
            t, p, c = load(refs)
            store(refs[n_t + n_p + n_c:], f(t, p, c), out_bounds)

        return pl.pallas_call(
            body, name=name + "_fwd", grid=grid, in_specs=in_specs,
            out_specs=[row_spec(b[-1]) for b in out_bounds], out_shape=out_shapes,
            compiler_params=_params("parallel"),
        )(*tiled, *params, *consts)

    def run_bwd(tiled, params, consts, cts):
        n_in = n_t + n_p + n_c
        n_o = len(out_bounds)

        def body(*refs):
            t, p, c = load(refs)
            g = [_split(r[...].astype(F32), b) for r, b in zip(refs[n_in:n_in + n_o], out_bounds)]
            _, pull = jax.vjp(lambda t_, p_: f(t_, p_, c), t, p)
            dt, dp = pull(g)
            store(refs[n_in + n_o:n_in + n_o + n_t], dt, in_bounds)
            first = pl.program_id(0) == 0
            for ref, d in zip(refs[n_in + n_o + n_t:], dp):
                @pl.when(first)
                def _(ref=ref):
                    ref[...] = jnp.zeros_like(ref)

                ref[...] += d

        res = pl.pallas_call(
            body, name=name + "_bwd", grid=grid,
            in_specs=in_specs + [row_spec(b[-1]) for b in out_bounds],
            out_specs=[row_spec(t.shape[1]) for t in tiled] + [par_spec(p) for p in params],
            out_shape=[jax.ShapeDtypeStruct(t.shape, d) for t, d in zip(tiled, ct_dtypes)]
                      + [jax.ShapeDtypeStruct(p.shape, F32) for p in params],
            compiler_params=_params("arbitrary"),
        )(*tiled, *params, *consts, *cts)
        return tuple(res[:n_t]), tuple(res[n_t:])

    return run_fwd, run_bwd


def proj_stage(name, f, projections, extra=(), params=(), consts=(), splits=None, ts=256, out_dtypes=None):
    n_z = len(projections)
    forms = [pr[2] for pr in projections]
    names = [pr[3] for pr in projections]
    need_da = [pr[4] for pr in projections]
    store = [pr[5] for pr in projections]
    extra, params, consts = tuple(extra), tuple(params), tuple(consts)

    def matmuls(a_list, w_list):
        return tuple(_matmul(a, w, form, out_dtype=dt, name=n + "_fwd")
                     for a, w, form, n, dt in zip(a_list, w_list, forms, names, store))

    def build(zs, ct=False):
        ct_dtypes = [BF16] * n_z + [e.dtype for e in extra] if ct else None
        return _stage_build(name, f, tuple(zs) + extra, params, consts, splits, ts, out_dtypes, ct_dtypes)

    @jax.custom_vjp
    def op(a_list, w_list, extra, params, consts):
        zs = matmuls(a_list, w_list)
        return tuple(build(zs)[0](zs + extra, params, consts))

    def op_fwd(a_list, w_list, extra, params, consts):
        zs = matmuls(a_list, w_list)
        return tuple(build(zs)[0](zs + extra, params, consts)), (a_list, w_list, zs, extra, params, consts)

    def op_bwd(res, cts):
        a_list, w_list, zs, extra, params, consts = res
        dt, dp = build(zs, ct=True)[1](zs + extra, params, consts, cts)
        da_list, dw_list = [], []
        for a, w, dz, form, n, want in zip(a_list, w_list, dt[:n_z], forms, names, need_da):
            if form == "nn":
                da = _matmul(dz, w, "nt", out_dtype=a.dtype, name=n + "_da") if want else jnp.zeros_like(a)
                dw = _matmul(a, dz, "tn", out_dtype=w.dtype, name=n + "_dw")
            else:
                da = _matmul(dz, w, "nn", out_dtype=a.dtype, name=n + "_da") if want else jnp.zeros_like(a)
                dw = _matmul(dz, a, "tn", out_dtype=w.dtype, name=n + "_dw")
            da_list.append(da)
            dw_list.append(dw)
        return tuple(da_list), tuple(dw_list), tuple(dt[n_z:]), dp, tuple(jnp.zeros_like(c) for c in consts)

    op.defvjp(op_fwd, op_bwd)
    return op(tuple(pr[0] for pr in projections), tuple(pr[1] for pr in projections), extra, params, consts)


def _rms(t, g):
    return t * lax.rsqrt(jnp.mean(t * t, axis=-1, keepdims=True) + RMS_EPS) * g


@functools.partial(jax.custom_vjp, nondiff_argnums=(1,))
def _lane_roll(t, shift):
    return pltpu.roll(t, shift % t.shape[-1], t.ndim - 1)


def _lane_roll_fwd(t, shift):
    return _lane_roll(t, shift), None


def _lane_roll_bwd(shift, _, ct):
    return (pltpu.roll(ct, (-shift) % ct.shape[-1], ct.ndim - 1),)


_lane_roll.defvjp(_lane_roll_fwd, _lane_roll_bwd)


def _rope_lanes(t, tables, half):
    reps = t.shape[1] // tables[0].shape[1]
    c, s_lo, s_hi = [jnp.concatenate([tb] * reps, axis=1) if reps > 1 else tb for tb in tables]
    return t * c + _lane_roll(t, -half) * s_lo + _lane_roll(t, half) * s_hi


PRENORM_TS = 256


def _prenorm_fwd_call(x, g, shards):
    s, width = x.shape
    ts = min(PRENORM_TS, s)
    nt = s // ts
    n_arr = len(shards)

    def body(*refs):
        x_ref, g_ref = refs[:2]
        o_ref = refs[2 + n_arr]
        i = pl.program_id(0)
        ag_start, ag_forward, ag_finish = _allgather_phases(refs[2:2 + n_arr], refs[3 + n_arr:3 + 2 * n_arr],
                                                            *refs[3 + 2 * n_arr:])

        @pl.when(i == 0)
        def _():
            ag_start()

        @pl.when(i == nt // 2)
        def _():
            ag_forward()

        o_ref[...] = _rms(x_ref[...], g_ref[...]).astype(o_ref.dtype)

        @pl.when(i == nt - 1)
        def _():
            ag_finish()

    return pl.pallas_call(
        body, name="prenorm_fwd", grid=(nt,),
        in_specs=[pl.BlockSpec((ts, width), lambda i: (i, 0)), pl.BlockSpec(g.shape, lambda i: (0, 0))]
                 + [HBM_SPEC] * n_arr,
        out_specs=[pl.BlockSpec((ts, width), lambda i: (i, 0))] + [HBM_SPEC] * n_arr,
        out_shape=[jax.ShapeDtypeStruct(x.shape, BF16)] + _allgather_out_shapes(shards),
        scratch_shapes=_allgather_sems(n_arr),
        compiler_params=_params("arbitrary"),
    )(x, g, *shards)


def _prenorm_bwd_call(x, g, dh_a, dh_b, dx_res, parts):
    s, width = x.shape
    ts = min(PRENORM_TS, s)
    nt = s // ts
    n_arr = len(parts)

    def body(*refs):
        x_ref, g_ref, dha_ref, dhb_ref, dxr_ref = refs[:5]
        dx_ref, dg_ref = refs[5 + n_arr:7 + n_arr]
        i = pl.program_id(0)
        exchange_start, exchange_finish = _exchange_chips_phases(
            refs[5:5 + n_arr], refs[7 + n_arr:7 + 2 * n_arr], *refs[7 + 2 * n_arr:])

        @pl.when(i == 0)
        def _():
            exchange_start()
            dg_ref[...] = jnp.zeros_like(dg_ref)

        _, pull = jax.vjp(_rms, x_ref[...], g_ref[...])
        dx, dg = pull(dha_ref[...].astype(F32) + dhb_ref[...].astype(F32))
        dx_ref[...] = dx + dxr_ref[...]
        dg_ref[...] += dg

        @pl.when(i == nt - 1)
        def _():
            exchange_finish()

    row = pl.BlockSpec((ts, width), lambda i: (i, 0))
    par = pl.BlockSpec(g.shape, lambda i: (0, 0))
    return pl.pallas_call(
        body, name="prenorm_bwd", grid=(nt,),
        in_specs=[row, par, row, row, row] + [HBM_SPEC] * n_arr,
        out_specs=[row, par] + [HBM_SPEC] * n_arr,
        out_shape=[jax.ShapeDtypeStruct(x.shape, F32), jax.ShapeDtypeStruct(g.shape, F32)]
                  + [jax.ShapeDtypeStruct(p.shape, p.dtype) for p in parts],
        scratch_shapes=_exchange_chips_sems(n_arr),
        compiler_params=_params("arbitrary"),
    )(x, g, dh_a, dh_b, dx_res, *parts)


@functools.partial(jax.custom_vjp, nondiff_argnums=(3,))
def prenorm_gather(x, g, shards, wire_dtypes):
    out = _prenorm_fwd_call(x, g, [s.astype(d) for s, d in zip(shards, wire_dtypes)])
    return out[0], out[0], x, tuple(out[1:])


def _prenorm_gather_fwd(x, g, shards, wire_dtypes):
    return prenorm_gather(x, g, shards, wire_dtypes), (x, g)


def _prenorm_gather_bwd(wire_dtypes, res, cts):
    x, g = res
    dh_a, dh_b, dx_res, d_gathered = cts
    out = _prenorm_bwd_call(x, g, dh_a, dh_b, dx_res, _reduce_scatter_head(d_gathered, "grads"))
    return out[0], out[1], _reduce_scatter_tail(out[2:], "grads")


prenorm_gather.defvjp(_prenorm_gather_fwd, _prenorm_gather_bwd)


def _f_prep(t, p, c):
    qa, ka, va, cq, ckv, kr = t[0]
    return [[_rope_lanes(qa, c[0:3], A_HEAD_DIM // 2)], [_rope_lanes(ka, c[0:3], A_HEAD_DIM // 2)], [va],
            [_rms(cq, p[0])], [_rms(ckv, p[1])], [_rope_lanes(kr, c[3:6], ROPE_DIM // 2)]]


def _f_qrope(t, p, c):
    return [[_rope_lanes(t[0][0], c, ROPE_DIM // 2)]]


def _f_kv(t, p, c):
    (k_nope, v), (k_pe,) = t
    return [[k_nope + jnp.concatenate([k_pe] * B_HEADS, axis=1)], [v]]


def _f_gate(t, p, c):
    (ga, gb), (pa,), (pb,) = t
    ba, bb = p
    return [[jax.nn.sigmoid(ga + ba) * pa + jax.nn.sigmoid(gb + bb) * pb]]


def _f_post(t, p, c):
    (branch,), (residual,) = t
    x1 = residual + _rms(branch, p[0])
    return [[x1], [_rms(x1, p[1])]]


def _f_out(t, p, c):
    (gate,), (emb,), (x2,) = t
    y = x2 + jax.nn.sigmoid(gate) * emb
    err = y - c[0]
    return [[0.5 * jnp.mean(err * err, axis=-1, keepdims=True)]]


def _shift_down(cur, prev, has_prev):
    full = jnp.concatenate([prev * has_prev, cur], axis=0)
    return pltpu.roll(full, 1, 0)[HALO:], pltpu.roll(full, 2, 0)[HALO:]


GELU_C = float(np.sqrt(2.0 / np.pi))
GELU_A = 0.044715
HALO = 8


def _gelu_tanh(x):
    x2 = x * x
    th = jnp.tanh(x * (GELU_C + (GELU_C * GELU_A) * x2))
    half = 0.5 + 0.5 * th
    return x * half, half + x * (0.5 - 0.5 * (th * th)) * (GELU_C + (3.0 * GELU_C * GELU_A) * x2)


def _row_sum(t):
    return jnp.sum(t, axis=0, keepdims=True)


def _conv3(cur, prev, w_ref, b_ref, has_prev):
    u1, u2 = _shift_down(cur, prev, has_prev)
    return w_ref[2:3, :] * cur + w_ref[1:2, :] * u1 + w_ref[0:1, :] * u2 + b_ref[...], u1, u2


def _mlp_act_specs(s):
    ts = min(CONV_TS, s)
    hb = ts // HALO

    def half_specs(h):
        return [pl.BlockSpec((ts, D_FF), lambda i: (i, h)),
                pl.BlockSpec((HALO, D_FF), lambda i: (jnp.maximum(i * hb - 1, 0), h))]

    def par_specs(h):
        return [pl.BlockSpec((CONV_W, D_FF), lambda i: (0, h)), pl.BlockSpec((1, D_FF), lambda i: (0, h))]

    return ts, hb, half_specs, par_specs


def _mlp_act_fwd_call(up, conv_w, conv_b):
    s = up.shape[0]
    ts, hb, half_specs, par_specs = _mlp_act_specs(s)

    def body(g_ref, gp_ref, v_ref, vp_ref, wg_ref, bg_ref, wv_ref, bv_ref, o_ref):
        has_prev = (pl.program_id(0) > 0).astype(F32)

        def chunk(cidx, carry):
            cols = pl.ds(pl.multiple_of(cidx * CONV_CHUNK, CONV_CHUNK), CONV_CHUNK)
            u_g, _, _ = _conv3(g_ref[:, cols], gp_ref[:, cols], wg_ref.at[:, cols], bg_ref.at[:, cols], has_prev)
            u_v, _, _ = _conv3(v_ref[:, cols], vp_ref[:, cols], wv_ref.at[:, cols], bv_ref.at[:, cols], has_prev)
            o_ref[:, cols] = (_gelu_tanh(u_g)[0] * u_v).astype(o_ref.dtype)
            return carry

        lax.fori_loop(0, D_FF // CONV_CHUNK, chunk, 0)

    return pl.pallas_call(
        body, name="mlp_act_fwd", grid=(s // ts,),
        in_specs=half_specs(0) + half_specs(1) + par_specs(0) + par_specs(1),
        out_specs=pl.BlockSpec((ts, D_FF), lambda i: (i, 0)),
        out_shape=jax.ShapeDtypeStruct((s, D_FF), BF16),
        compiler_params=_params("parallel"),
    )(up, up, up, up, conv_w, conv_b, conv_w, conv_b)


def _mlp_act_bwd_call(up, conv_w, conv_b, dact):
    s = up.shape[0]
    ts, hb, half_specs, par_specs = _mlp_act_specs(s)
    nt = s // ts
    ext = ts + HALO
    bf16_rows = 2 * HALO

    def next_spec(rows, h):
        return pl.BlockSpec((rows, D_FF), lambda i: (jnp.minimum((i + 1) * (ts // rows), s // rows - 1), h))

    def body(g_ref, gp_ref, gn_ref, v_ref, vp_ref, vn_ref, wg_ref, bg_ref, wv_ref, bv_ref, da_ref, dan_ref,
             dup_ref, dwg_ref, dbg_ref, dwv_ref, dbv_ref):
        i = pl.program_id(0)
        has_prev, has_next = (i > 0).astype(F32), (i < nt - 1).astype(F32)

        @pl.when(i == 0)
        def _():
            for ref in (dwg_ref, dbg_ref, dwv_ref, dbv_ref):
                ref[...] = jnp.zeros_like(ref)

        def chunk(cidx, carry):
            cols = pl.ds(pl.multiple_of(cidx * CONV_CHUNK, CONV_CHUNK), CONV_CHUNK)
            g_ext = jnp.concatenate([g_ref[:, cols], gn_ref[:, cols]], axis=0)
            v_ext = jnp.concatenate([v_ref[:, cols], vn_ref[:, cols]], axis=0)
            u_g, g1, g2 = _conv3(g_ext, gp_ref[:, cols], wg_ref.at[:, cols], bg_ref.at[:, cols], has_prev)
            u_v, v1, v2 = _conv3(v_ext, vp_ref[:, cols], wv_ref.at[:, cols], bv_ref.at[:, cols], has_prev)
            da_ext = jnp.concatenate([da_ref[:, cols].astype(F32),
                                      dan_ref[:, cols].astype(F32)[0:HALO] * has_next], axis=0)
            act_g, dact_g = _gelu_tanh(u_g)
            du_g = da_ext * u_v * dact_g
            du_v = da_ext * act_g
            for du, w_ref, x0, x1, x2, dw_ref, db_ref, lo in ((du_g, wg_ref, g_ext, g1, g2, dwg_ref, dbg_ref, 0),
                                                          (du_v, wv_ref, v_ext, v1, v2, dwv_ref, dbv_ref, D_FF)):
                d1 = pltpu.roll(du, ext - 1, 0)
                d2 = pltpu.roll(du, ext - 2, 0)
                dup = w_ref[2:3, cols] * du + w_ref[1:2, cols] * d1 + w_ref[0:1, cols] * d2
                out_cols = pl.ds(pl.multiple_of(lo + cidx * CONV_CHUNK, CONV_CHUNK), CONV_CHUNK)
                dup_ref[:, out_cols] = dup[0:ts].astype(dup_ref.dtype)
                own = du[0:ts]
                dw_ref[0:1, cols] += _row_sum(own * x2[0:ts])
                dw_ref[1:2, cols] += _row_sum(own * x1[0:ts])
                dw_ref[2:3, cols] += _row_sum(own * x0[0:ts])
                db_ref[:, cols] += _row_sum(own)
            return carry

        lax.fori_loop(0, D_FF // CONV_CHUNK, chunk, 0)

    par_out = [pl.BlockSpec((CONV_W, D_FF), lambda i: (0, 0)), pl.BlockSpec((1, D_FF), lambda i: (0, 0))]
    par_shapes = [jax.ShapeDtypeStruct((CONV_W, D_FF), F32), jax.ShapeDtypeStruct((1, D_FF), F32)]
    return pl.pallas_call(
        body, name="mlp_act_bwd", grid=(nt,),
        in_specs=(half_specs(0) + [next_spec(HALO, 0)] + half_specs(1) + [next_spec(HALO, 1)]
                  + par_specs(0) + par_specs(1)
                  + [pl.BlockSpec((ts, D_FF), lambda i: (i, 0)), next_spec(bf16_rows, 0)]),
        out_specs=[pl.BlockSpec((ts, 2 * D_FF), lambda i: (i, 0))] + par_out + par_out,
        out_shape=[jax.ShapeDtypeStruct((s, 2 * D_FF), BF16)] + par_shapes + par_shapes,
        compiler_params=_params("arbitrary"),
    )(up, up, up, up, up, up, conv_w, conv_b, conv_w, conv_b, dact, dact)


@jax.custom_vjp
def mlp_up(h2, w_up_t, conv_w, conv_b):
    return _mlp_act_fwd_call(_matmul(h2, w_up_t, "nt", out_dtype=F32, name="w_up_fwd"), conv_w, conv_b)


def _mlp_up_fwd(h2, w_up_t, conv_w, conv_b):
    up = _matmul(h2, w_up_t, "nt", out_dtype=F32, name="w_up_fwd")
    return _mlp_act_fwd_call(up, conv_w, conv_b), (h2, w_up_t, up, conv_w, conv_b)


def _mlp_up_bwd(res, dact):
    h2, w_up_t, up, conv_w, conv_b = res
    dup, dwg, dbg, dwv, dbv = _mlp_act_bwd_call(up, conv_w, conv_b, dact)
    dh2 = _matmul(dup, w_up_t, "nn", out_dtype=h2.dtype, name="w_up_da")
    dw = _matmul(dup, h2, "tn", out_dtype=w_up_t.dtype, name="w_up_dw")
    return dh2, dw, jnp.concatenate([dwg, dwv], axis=1), jnp.concatenate([dbg, dbv], axis=1)


mlp_up.defvjp(_mlp_up_fwd, _mlp_up_bwd)


SWA_ROWS = A_GROUP * SWA_BLOCK


def _swa_sink_rows(sink_ref, g):
    return jnp.concatenate([jnp.full((SWA_BLOCK, 1), sink_ref[g * A_GROUP + h], F32) for h in range(A_GROUP)], axis=0)


def _swa_operands(q_ref, kp_ref, kc_ref, vp_ref, vc_ref, sink_ref):
    groups = []
    for g in range(A_KV_HEADS):
        groups.append((_swa_stack_heads(q_ref, g), _dup_half(kp_ref[...], g), _dup_half(kc_ref[...], g),
                       _dup_half(vp_ref[...], g), _dup_half(vc_ref[...], g)))
    return groups, jnp.concatenate([_swa_sink_rows(sink_ref, g) for g in range(A_KV_HEADS)], axis=0)


def _swa_probs(groups, sink, prev_off):
    scale = A_HEAD_DIM ** -0.5
    sp = jnp.concatenate([lax.dot_general(gr[0], gr[1], NT_DIMS, preferred_element_type=F32) for gr in groups], axis=0)
    sc = jnp.concatenate([lax.dot_general(gr[0], gr[2], NT_DIMS, preferred_element_type=F32) for gr in groups], axis=0)
    qi = lax.broadcasted_iota(jnp.int32, sp.shape, 0) & (SWA_BLOCK - 1)
    kj = lax.broadcasted_iota(jnp.int32, sp.shape, 1)
    in_cur = kj <= qi
    sw = jnp.where(in_cur, sc, jnp.where(kj > qi + prev_off, sp, -jnp.inf)) * scale
    m = jnp.maximum(jnp.max(sw, axis=-1, keepdims=True), sink)
    e, es = jnp.exp(sw - m), jnp.exp(sink - m)
    den = jnp.sum(e, axis=-1, keepdims=True) + es
    return e / den, in_cur, es / den


def _swa_split(t, in_cur):
    cur = jnp.where(in_cur, t, 0.0)
    return t - cur, cur


MLA_SCALE = (NOPE_DIM + ROPE_DIM) ** -0.5
EXP2_SCALE = MLA_SCALE * float(np.log2(np.e))
NT_DIMS = (((1,), (1,)), ((), ()))
TN_DIMS = (((0,), (0,)), ((), ()))


LANES = 128
HALF = LANES // 2


def _low_half(shape):
    return lax.broadcasted_iota(jnp.int32, shape, len(shape) - 1) < HALF


def _dup_half(x, g):
    xf = x.astype(F32)
    keep = _low_half(xf.shape) if g == 0 else jnp.logical_not(_low_half(xf.shape))
    xm = jnp.where(keep, xf, 0.0)
    return (xm + pltpu.roll(xm, HALF, 1)).astype(x.dtype)


def _fold_half(r, g):
    total = r + pltpu.roll(r, HALF, 1)
    keep = _low_half(r.shape) if g == 0 else jnp.logical_not(_low_half(r.shape))
    return jnp.where(keep, total, 0.0)


def _swa_stack_heads(ref, g):
    parts = []
    for tile in range(2):
        slab = ref[:, (2 * g + tile) * LANES:(2 * g + tile + 1) * LANES]
        low = _low_half(slab.shape)
        parts += [jnp.where(low, slab, jnp.zeros_like(slab)), jnp.where(low, jnp.zeros_like(slab), slab)]
    return jnp.concatenate(parts, axis=0)


def _swa_unstack_heads(ref, g, rows):
    for tile in range(2):
        a = rows[(2 * tile) * SWA_BLOCK:(2 * tile + 1) * SWA_BLOCK]
        b = rows[(2 * tile + 1) * SWA_BLOCK:(2 * tile + 2) * SWA_BLOCK]
        ref[:, (2 * g + tile) * LANES:(2 * g + tile + 1) * LANES] = jnp.where(_low_half(a.shape), a, b).astype(ref.dtype)


def _swa_nat_specs():
    blk = SWA_BLOCK
    q_spec = pl.BlockSpec((blk, A_HEADS * A_HEAD_DIM), lambda n: (n, 0))
    prev_spec = pl.BlockSpec((blk, LANES), lambda n: (jnp.maximum(n - 1, 0), 0))
    cur_spec = pl.BlockSpec((blk, LANES), lambda n: (n, 0))
    return q_spec, prev_spec, cur_spec, pl.BlockSpec(memory_space=pltpu.SMEM)


def _swa_nat_fwd_call(q, k, v, sinks, shards):
    s = q.shape[0]
    nblk = s // SWA_BLOCK
    n_arr = len(shards)
    q_spec, prev_spec, cur_spec, sink_spec = _swa_nat_specs()

    def body(*refs):
        q_ref, kp_ref, kc_ref, vp_ref, vc_ref, sink_ref = refs[:6]
        o_ref = refs[6 + n_arr]
        n = pl.program_id(0)
        ag_start, ag_forward, ag_finish = _allgather_phases(refs[6:6 + n_arr], refs[7 + n_arr:7 + 2 * n_arr],
                                                            *refs[7 + 2 * n_arr:])

        @pl.when(n == 0)
        def _():
            ag_start()

        @pl.when(n == nblk // 2)
        def _():
            ag_forward()

        prev_off = jnp.where(n > 0, 0, SWA_BLOCK)
        groups, sink = _swa_operands(q_ref, kp_ref, kc_ref, vp_ref, vc_ref, sink_ref)
        p, in_cur, _ = _swa_probs(groups, sink, prev_off)
        ppb, pcb = [t.astype(BF16) for t in _swa_split(p, in_cur)]
        for g, (_, _, _, vp, vc) in enumerate(groups):
            rows = slice(g * SWA_ROWS, (g + 1) * SWA_ROWS)
            out = (jnp.dot(ppb[rows], vp, preferred_element_type=F32)
                   + jnp.dot(pcb[rows], vc, preferred_element_type=F32))
            _swa_unstack_heads(o_ref, g, out)

        @pl.when(n == nblk - 1)
        def _():
            ag_finish()

    return pl.pallas_call(
        body, name="swa_fwd", grid=(nblk,),
        in_specs=[q_spec, prev_spec, cur_spec, prev_spec, cur_spec, sink_spec] + [HBM_SPEC] * n_arr,
        out_specs=[q_spec] + [HBM_SPEC] * n_arr,
        out_shape=[jax.ShapeDtypeStruct(q.shape, BF16)] + _allgather_out_shapes(shards),
        scratch_shapes=_allgather_sems(n_arr),
        compiler_params=_params("arbitrary"),
    )(q, k, k, v, v, sinks, *shards)


def _swa_nat_bwd_call(q, k, v, sinks, do, parts):
    s = q.shape[0]
    nblk = s // SWA_BLOCK
    n_arr = len(parts)
    q_spec, prev_spec, cur_spec, sink_spec = _swa_nat_specs()
    scale = A_HEAD_DIM ** -0.5
    dsink_spec = pl.BlockSpec((A_KV_HEADS, SWA_ROWS, 1), lambda n: (0, 0, 0))

    def body(*refs):
        q_ref, kp_ref, kc_ref, vp_ref, vc_ref, sink_ref, do_ref = refs[:7]
        dq_ref, dkp_ref, dkc_ref, dvp_ref, dvc_ref, dsink_ref = refs[7 + n_arr:13 + n_arr]
        n = pl.program_id(0)
        exchange_start, exchange_finish = _exchange_chips_phases(
            refs[7:7 + n_arr], refs[13 + n_arr:13 + 2 * n_arr], *refs[13 + 2 * n_arr:])

        @pl.when(n == 0)
        def _():
            exchange_start()
        prev_off = jnp.where(n > 0, 0, SWA_BLOCK)

        @pl.when(n == 0)
        def _():
            dsink_ref[...] = jnp.zeros_like(dsink_ref)

        groups, sink = _swa_operands(q_ref, kp_ref, kc_ref, vp_ref, vc_ref, sink_ref)
        dobs = [_swa_stack_heads(do_ref, g) for g in range(A_KV_HEADS)]
        p, in_cur, ps = _swa_probs(groups, sink, prev_off)
        ppb, pcb = [t.astype(BF16) for t in _swa_split(p, in_cur)]

        def per_group(fn):
            return jnp.concatenate([fn(g, slice(g * SWA_ROWS, (g + 1) * SWA_ROWS)) for g in range(A_KV_HEADS)], axis=0)

        out = per_group(lambda g, rows: jnp.dot(ppb[rows], groups[g][3], preferred_element_type=F32)
                        + jnp.dot(pcb[rows], groups[g][4], preferred_element_type=F32))
        delta = jnp.sum(jnp.concatenate(dobs, axis=0).astype(F32) * out, axis=-1, keepdims=True)
        dp = jnp.where(in_cur,
                       per_group(lambda g, rows: lax.dot_general(dobs[g], groups[g][4], NT_DIMS,
                                                                 preferred_element_type=F32)),
                       per_group(lambda g, rows: lax.dot_general(dobs[g], groups[g][3], NT_DIMS,
                                                                 preferred_element_type=F32)))
        dsp, dsc = [t.astype(BF16) for t in _swa_split(p * (dp - delta), in_cur)]
        dsink_ref[...] += (-ps * delta).reshape(dsink_ref.shape)
        totals = [jnp.zeros((SWA_BLOCK, LANES), F32) for _ in range(4)]
        for g, (qb, kp, kc, _, _) in enumerate(groups):
            rows = slice(g * SWA_ROWS, (g + 1) * SWA_ROWS)
            dq = (jnp.dot(dsp[rows], kp, preferred_element_type=F32)
                  + jnp.dot(dsc[rows], kc, preferred_element_type=F32)) * scale
            _swa_unstack_heads(dq_ref, g, dq)
            pieces = [lax.dot_general(dsp[rows], qb, TN_DIMS, preferred_element_type=F32) * scale,
                      lax.dot_general(dsc[rows], qb, TN_DIMS, preferred_element_type=F32) * scale,
                      lax.dot_general(ppb[rows], dobs[g], TN_DIMS, preferred_element_type=F32),
                      lax.dot_general(pcb[rows], dobs[g], TN_DIMS, preferred_element_type=F32)]
            totals = [tot + _fold_half(r, g) for tot, r in zip(totals, pieces)]
        dkp_ref[...], dkc_ref[...], dvp_ref[...], dvc_ref[...] = totals

        @pl.when(n == nblk - 1)
        def _():
            exchange_finish()

    kv_shape = jax.ShapeDtypeStruct(k.shape, F32)
    return pl.pallas_call(
        body, name="swa_bwd", grid=(nblk,),
        in_specs=[q_spec, prev_spec, cur_spec, prev_spec, cur_spec, sink_spec, q_spec] + [HBM_SPEC] * n_arr,
        out_specs=[q_spec, cur_spec, cur_spec, cur_spec, cur_spec, dsink_spec] + [HBM_SPEC] * n_arr,
        out_shape=[jax.ShapeDtypeStruct(q.shape, q.dtype), kv_shape, kv_shape, kv_shape, kv_shape,
                   jax.ShapeDtypeStruct((A_KV_HEADS, SWA_ROWS, 1), F32)]
                  + [jax.ShapeDtypeStruct(p.shape, p.dtype) for p in parts],
        scratch_shapes=_exchange_chips_sems(n_arr),
        compiler_params=_params("arbitrary"),
    )(q, k, k, v, v, sinks, do, *parts)


@jax.custom_vjp
def swa_nat(q, k, v, sinks, shards):
    out = _swa_nat_fwd_call(q, k, v, sinks, [s.astype(BF16) for s in shards])
    return out[0], tuple(out[1:])


def _swa_nat_fwd(q, k, v, sinks, shards):
    out = _swa_nat_fwd_call(q, k, v, sinks, [s.astype(BF16) for s in shards])
    return (out[0], tuple(out[1:])), (q, k, v, sinks)


def _swa_nat_bwd(res, cts):
    q, k, v, sinks = res
    do, d_gathered = cts
    out = _swa_nat_bwd_call(q, k, v, sinks, do, _reduce_scatter_head(d_gathered, "mid_grads"))
    dq, dkp, dkc, dvp, dvc, dsink = out[:6]

    def fold(prev_part, cur_part):
        shifted = jnp.concatenate([prev_part[SWA_BLOCK:], jnp.zeros_like(prev_part[:SWA_BLOCK])], axis=0)
        return (cur_part + shifted).astype(k.dtype)

    dsinks = jnp.sum(dsink.reshape(A_HEADS, SWA_BLOCK), axis=1)
    return dq, fold(dkp, dkc), fold(dvp, dvc), dsinks, _reduce_scatter_tail(out[6:], "mid_grads")


swa_nat.defvjp(_swa_nat_fwd, _swa_nat_bwd)

N_PAIR = B_HEADS // 2


def _flash_nat_fwd_call(q, k, v, shards):
    s = q.shape[0]
    t = min(FLASH_T, s)
    nb = s // t
    d = LANES
    n_arr = len(shards)

    def body(*refs):
        q_ref, k_ref, v_ref = refs[:3]
        shard_refs = refs[3:3 + n_arr]
        o_ref, lse_ref = refs[3 + n_arr:5 + n_arr]
        gathered_refs = refs[5 + n_arr:5 + 2 * n_arr]
        vt_ref, m_ref, l_ref, acc_ref = refs[5 + 2 * n_arr:9 + 2 * n_arr]
        pair, i = pl.program_id(0), pl.program_id(1)
        ag_start, ag_forward, ag_finish = _allgather_phases(shard_refs, gathered_refs, *refs[9 + 2 * n_arr:])

        @pl.when((pair == 0) & (i == 0))
        def _():
            ag_start()

        @pl.when((pair == N_PAIR - 1) & (i == 0))
        def _():
            ag_forward()

        @pl.when(i == 0)
        def _():
            for hh in range(2):
                for chunk in range(nb):
                    rows = slice(chunk * t, (chunk + 1) * t)
                    vt_ref[hh, :, rows] = v_ref[rows, hh * d:(hh + 1) * d].T

        m_ref[...] = jnp.full_like(m_ref, -jnp.inf)
        l_ref[...] = jnp.zeros_like(l_ref)
        acc_ref[...] = jnp.zeros_like(acc_ref)

        def step(j, on_diagonal):
            keys = pl.ds(pl.multiple_of(j * t, t), t)
            scores = [lax.dot_general(k_ref[keys, hh * d:(hh + 1) * d], q_ref[:, hh * d:(hh + 1) * d], NT_DIMS,
                                      preferred_element_type=F32) for hh in range(2)]
            for hh in range(2):
                sc_t = scores[hh]
                if on_diagonal:
                    key = lax.broadcasted_iota(jnp.int32, (t, t), 0)
                    qry = lax.broadcasted_iota(jnp.int32, (t, t), 1)
                    sc_t = jnp.where(qry >= key, sc_t, -jnp.inf)
                m_old = m_ref[hh]
                m_new = jnp.maximum(m_old, jnp.max(sc_t, axis=0, keepdims=True))
                alpha = jnp.exp2((m_old - m_new) * EXP2_SCALE)
                p_t = jnp.exp2((sc_t - m_new) * EXP2_SCALE)
                l_ref[hh] = alpha * l_ref[hh] + jnp.sum(p_t, axis=0, keepdims=True)
                acc_ref[hh] = alpha * acc_ref[hh] + jnp.dot(vt_ref[hh, :, keys], p_t.astype(BF16),
                                                            preferred_element_type=F32)
                m_ref[hh] = m_new

        def below(j, carry):
            step(j, False)
            return carry

        lax.fori_loop(0, i, below, 0)
        step(i, True)
        outs =[(acc_ref[hh] / l_ref[hh]).T for hh in range(2)]
        for hh in range(2):
            lse_ref[hh] = m_ref[hh] * EXP2_SCALE + jnp.log2(l_ref[hh])
        o_ref[...] = (outs[0] + pltpu.roll(outs[1], HALF, 1)).astype(o_ref.dtype)

        @pl.when((pair == N_PAIR - 1) & (i == nb - 1))
        def _():
            ag_finish()

    return pl.pallas_call(
        body, name="mla_fwd", grid=(N_PAIR, nb),
        in_specs=[pl.BlockSpec((t, 2 * d), lambda p, i: (i, p)),
                  pl.BlockSpec((s, 2 * d), lambda p, i: (0, p)),
                  pl.BlockSpec((s, 2 * d), lambda p, i: (0, p))] + [HBM_SPEC] * n_arr,
        out_specs=[pl.BlockSpec((t, d), lambda p, i: (i, p)),
                   pl.BlockSpec((2, 1, t), lambda p, i: (p, 0, i))] + [HBM_SPEC] * n_arr,
        out_shape=[jax.ShapeDtypeStruct((s, N_PAIR * d), BF16), jax.ShapeDtypeStruct((B_HEADS, 1, s), F32)]
                  + _allgather_out_shapes(shards),
        scratch_shapes=[pltpu.VMEM((2, d, s), BF16), pltpu.VMEM((2, 1, t), F32), pltpu.VMEM((2, 1, t), F32),
                        pltpu.VMEM((2, d, t), F32)] + _allgather_sems(n_arr),
        compiler_params=_params("arbitrary", "arbitrary"),
    )(q, k, v, *shards)


def _flash_nat_delta_call(o, do):
    s, w = o.shape
    t = min(FLASH_T, s)

    def body(o_ref, do_ref, out_ref):
        prod = o_ref[...].astype(F32) * do_ref[...].astype(F32)
        lane = lax.broadcasted_iota(jnp.int32, (w, LANES), 0) // V_DIM
        head = lax.broadcasted_iota(jnp.int32, (w, LANES), 1)
        out_ref[...] = jnp.dot(prod, (lane == head).astype(F32), precision=lax.Precision.HIGHEST,
                               preferred_element_type=F32)

    spec = pl.BlockSpec((t, w), lambda i: (i, 0))
    return pl.pallas_call(
        body, name="mla_delta", grid=(s // t,), in_specs=[spec, spec],
        out_specs=pl.BlockSpec((t, LANES), lambda i: (i, 0)),
        out_shape=jax.ShapeDtypeStruct((s, LANES), F32), compiler_params=_params("parallel"),
    )(o, do)


def _flash_nat_bwd_call(q, k, v, lse_row, delta_row, do, parts):
    s = q.shape[0]
    t = min(FLASH_T, s)
    nb = s // t
    d = LANES
    n_arr = len(parts)

    def body(*refs):
        q_ref, k_ref, v_ref, lse_ref, delta_ref, do_ref = refs[:6]
        part_refs = refs[6:6 + n_arr]
        dq_ref, dk_ref, dv_ref = refs[6 + n_arr:9 + n_arr]
        received_refs = refs[9 + n_arr:9 + 2 * n_arr]
        dq_acc, dk_acc, dv_acc = refs[9 + 2 * n_arr:12 + 2 * n_arr]
        pair, j = pl.program_id(0), pl.program_id(1)
        exchange_start, exchange_finish = _exchange_chips_phases(part_refs, received_refs, *refs[12 + 2 * n_arr:])

        @pl.when((pair == 0) & (j == 0))
        def _():
            exchange_start()

        @pl.when(j == 0)
        def _():
            dq_acc[...] = jnp.zeros_like(dq_acc)

        for hh in range(2):
            kb, vb = k_ref[:, hh * d:(hh + 1) * d], v_ref[:, hh * d:(hh + 1) * d]
            dk_acc[...] = jnp.zeros_like(dk_acc)
            dv_acc[...] = jnp.zeros_like(dv_acc)

            def step(i, on_diagonal, hh=hh, kb=kb, vb=vb):
                rows = pl.ds(pl.multiple_of(i * t, t), t)
                qb = q_ref[rows, hh * d:(hh + 1) * d]
                do_pair = do_ref[rows, :].astype(F32)
                do_h = do_pair if hh == 0 else pltpu.roll(do_pair, HALF, 1)
                dob = jnp.where(_low_half(do_h.shape), do_h, 0.0).astype(BF16)
                sc_t = lax.dot_general(kb, qb, NT_DIMS, preferred_element_type=F32)
                p_t = jnp.exp2(sc_t * EXP2_SCALE - lse_ref[hh, :, rows])
                if on_diagonal:
                    key = lax.broadcasted_iota(jnp.int32, (t, t), 0)
                    qry = lax.broadcasted_iota(jnp.int32, (t, t), 1)
                    p_t = jnp.where(qry >= key, p_t, 0.0)
                dp_t = lax.dot_general(vb, dob, NT_DIMS, preferred_element_type=F32)
                ds_t = (p_t * (dp_t - delta_ref[hh, :, rows])).astype(BF16)
                dv_acc[...] += jnp.dot(p_t.astype(BF16), dob, preferred_element_type=F32)
                dk_acc[...] += jnp.dot(ds_t, qb, preferred_element_type=F32)
                dq_acc[hh, rows, :] += lax.dot_general(ds_t, kb, TN_DIMS, preferred_element_type=F32)

            def above(i, carry, step=step):
                step(i, False)
                return carry

            step(j, True)
            lax.fori_loop(j + 1, nb, above, 0)
            dk_ref[:, hh * d:(hh + 1) * d] = (dk_acc[...] * MLA_SCALE).astype(dk_ref.dtype)
            dv_ref[:, hh * d:(hh + 1) * d] = dv_acc[...].astype(dv_ref.dtype)

        @pl.when(j == nb - 1)
        def _():
            for hh in range(2):
                dq_ref[:, hh * d:(hh + 1) * d] = (dq_acc[hh] * MLA_SCALE).astype(dq_ref.dtype)

        @pl.when((pair == N_PAIR - 1) & (j == nb - 1))
        def _():
            exchange_finish()

    full_spec = pl.BlockSpec((s, 2 * d), lambda p, j: (0, p))
    tile_spec = pl.BlockSpec((t, 2 * d), lambda p, j: (j, p))
    row_spec = pl.BlockSpec((2, 1, s), lambda p, j: (p, 0, 0))
    return pl.pallas_call(
        body, name="mla_bwd", grid=(N_PAIR, nb),
        in_specs=[full_spec, tile_spec, tile_spec, row_spec, row_spec, pl.BlockSpec((s, d), lambda p, j: (0, p))]
                 + [HBM_SPEC] * n_arr,
        out_specs=[full_spec, tile_spec, tile_spec] + [HBM_SPEC] * n_arr,
        out_shape=[jax.ShapeDtypeStruct(q.shape, q.dtype)] * 3 + [jax.ShapeDtypeStruct(p.shape, p.dtype) for p in parts],
        scratch_shapes=[pltpu.VMEM((2, s, d), F32), pltpu.VMEM((t, d), F32), pltpu.VMEM((t, d), F32)]
                       + _exchange_chips_sems(n_arr),
        compiler_params=_params("arbitrary", "arbitrary"),
    )(q, k, v, lse_row, delta_row, do, *parts)


def _reduce_scatter_head(cts, tag):
    received = _exchange_sibling(list(cts), tag + "_exchange_sibling")
    my_c = lax.axis_index("c").astype(jnp.int32).reshape(1)
    return [_pair_add(m, r, my_c, "%s_pair_add_%d" % (tag, i)) for i, (m, r) in enumerate(zip(cts, received))]


def _reduce_scatter_tail(chip_parts, tag):
    return tuple(_sum_blocks(r, "%s_sum_%d" % (tag, i)) for i, r in enumerate(chip_parts))


@jax.custom_vjp
def flash_nat(q, k, v, shards):
    out = _flash_nat_fwd_call(q, k, v, [s.astype(BF16) for s in shards])
    return out[0], tuple(out[2:])


def _flash_nat_fwd(q, k, v, shards):
    out = _flash_nat_fwd_call(q, k, v, [s.astype(BF16) for s in shards])
    return (out[0], tuple(out[2:])), (q, k, v, out[0], out[1])


def _flash_nat_bwd(res, cts):
    q, k, v, o, lse = res
    do, d_gathered = cts
    delta = _flash_nat_delta_call(o, do)[:, :B_HEADS].T.reshape(B_HEADS, 1, q.shape[0])
    out = _flash_nat_bwd_call(q, k, v, lse, delta, do, _reduce_scatter_head(d_gathered, "mlp_grads"))
    return out[0], out[1], out[2], _reduce_scatter_tail(out[3:], "mlp_grads")


flash_nat.defvjp(_flash_nat_fwd, _flash_nat_bwd)


HBM_SPEC = pl.BlockSpec(memory_space=pltpu.HBM)


def _allgather(shards, name):
    n_arr = len(shards)

    def body(*refs):
        start, forward, finish = _allgather_phases(refs[:n_arr], refs[n_arr:2 * n_arr], *refs[2 * n_arr:])
        start()
        forward()
        finish()

    return pl.pallas_call(
        body, name=name, out_shape=_allgather_out_shapes(shards),
        in_specs=[HBM_SPEC] * n_arr, out_specs=[HBM_SPEC] * n_arr,
        scratch_shapes=_allgather_sems(n_arr),
    )(*shards)


def _allgather_out_shapes(shards):
    return [jax.ShapeDtypeStruct((N_DEV,) + s.shape, s.dtype) for s in shards]


def _allgather_sems(n_arr):
    return [pltpu.SemaphoreType.DMA((7, n_arr)), pltpu.SemaphoreType.DMA((7, n_arr)), pltpu.SemaphoreType.DMA((n_arr,))]


def _allgather_phases(x_refs, out_refs, send_sems, recv_sems, local_sems):
    arrays = range(len(x_refs))
    x, y, c = lax.axis_index("x"), lax.axis_index("y"), lax.axis_index("c")
    me, sibling = (x, y, c), (x, y, 1 - c)
    chips = [(1 - x, y), (x, 1 - y), (1 - x, 1 - y)]

    def rows(a, px, py, pc):
        return out_refs[a].at[4 * px + 2 * py + pc]

    def copy(a, k, block, to, src=None):
        return pltpu.make_async_remote_copy(
            src_ref=rows(a, *block) if src is None else src, dst_ref=rows(a, *block),
            send_sem=send_sems.at[k, a], recv_sem=recv_sems.at[k, a], device_id=to, device_id_type=MESH_ID)

    def mine():
        return [pltpu.make_async_copy(x_refs[a], rows(a, *me), local_sems.at[a]) for a in arrays]

    def first():
        return [cp for a in arrays for cp in
                [copy(a, 0, me, sibling, src=x_refs[a])]
                + [copy(a, 1 + j, me, (*chip, c), src=x_refs[a]) for j, chip in enumerate(chips)]]

    def passed():
        return [copy(a, 4 + j, (*chip, c), sibling) for j, chip in enumerate(chips) for a in arrays]

    def start():
        for cp in mine() + first():
            cp.start()

    def forward():
        for j, chip in enumerate(chips):
            for a in arrays:
                copy(a, 1 + j, (*chip, c), me).wait_recv()
                copy(a, 4 + j, (*chip, c), sibling).start()

    def finish():
        for a in arrays:
            copy(a, 0, sibling, me).wait_recv()
        for j, chip in enumerate(chips):
            for a in arrays:
                copy(a, 4 + j, (*chip, 1 - c), me).wait_recv()
        for cp in first() + passed():
            cp.wait_send()
        for cp in mine():
            cp.wait()

    return start, forward, finish


N_CHIP = 4


def _exchange_sibling(parts, name):
    n_arr = len(parts)

    def body(*refs):
        in_refs, recv_refs = refs[:n_arr], refs[n_arr:2 * n_arr]
        send_sems, recv_sems = refs[2 * n_arr:]
        x, y, c = lax.axis_index("x"), lax.axis_index("y"), lax.axis_index("c")
        copies = []
        for a in range(n_arr):
            for q in range(N_CHIP):
                copies.append(pltpu.make_async_remote_copy(
                    src_ref=in_refs[a].at[2 * q + 1 - c], dst_ref=recv_refs[a].at[q],
                    send_sem=send_sems.at[q, a], recv_sem=recv_sems.at[q, a],
                    device_id=(x, y, 1 - c), device_id_type=MESH_ID))
        for cp in copies:
            cp.start()
        for cp in copies:
            cp.wait()

    return pl.pallas_call(
        body, name=name, out_shape=[jax.ShapeDtypeStruct((N_CHIP,) + p.shape[1:], p.dtype) for p in parts],
        in_specs=[HBM_SPEC] * n_arr, out_specs=[HBM_SPEC] * n_arr,
        scratch_shapes=[pltpu.SemaphoreType.DMA((N_CHIP, n_arr)), pltpu.SemaphoreType.DMA((N_CHIP, n_arr))],
    )(*parts)


def _exchange_chips_sems(n_arr):
    return [pltpu.SemaphoreType.DMA((N_CHIP - 1, n_arr)), pltpu.SemaphoreType.DMA((N_CHIP - 1, n_arr)),
            pltpu.SemaphoreType.DMA((n_arr,))]


def _exchange_chips_phases(in_refs, out_refs, send_sems, recv_sems, local_sems):
    n_arr = len(in_refs)
    x, y, c = lax.axis_index("x"), lax.axis_index("y"), lax.axis_index("c")
    me = 2 * x + y

    def copies():
        out = [pltpu.make_async_copy(in_refs[a].at[me], out_refs[a].at[me], local_sems.at[a]) for a in range(n_arr)]
        for k in range(1, N_CHIP):
            px = 1 - x if k & 2 else x
            py = 1 - y if k & 1 else y
            for a in range(n_arr):
                out.append(pltpu.make_async_remote_copy(
                    src_ref=in_refs[a].at[2 * px + py], dst_ref=out_refs[a].at[me],
                    send_sem=send_sems.at[k - 1, a], recv_sem=recv_sems.at[k - 1, a],
                    device_id=(px, py, c), device_id_type=MESH_ID))
        return out

    def start():
        for cp in copies():
            cp.start()

    def finish():
        for cp in copies():
            cp.wait()

    return start, finish


def _row_tile(r, ccols, blocks):
    cap = max(16, (2 * 1024 * 1024) // (4 * ccols * blocks))
    return _pick(r, cap, 16)


def _pair_add(mine, theirs, my_c, name):
    _, r, ccols = mine.shape
    tr = _row_tile(r, ccols, 1)

    def body(c_ref, a_ref, b_ref, o_ref):
        o_ref[...] = (a_ref[...].astype(F32) + b_ref[...].astype(F32)).astype(o_ref.dtype)

    spec = pl.BlockSpec((None, tr, ccols), lambda q, i, c_ref: (q, i, 0))
    return pl.pallas_call(
        body, name=name,
        grid_spec=pltpu.PrefetchScalarGridSpec(
            num_scalar_prefetch=1, grid=(N_CHIP, r // tr),
            in_specs=[pl.BlockSpec((None, tr, ccols), lambda q, i, c_ref: (2 * q + c_ref[0], i, 0)), spec],
            out_specs=spec),
        out_shape=jax.ShapeDtypeStruct(theirs.shape, theirs.dtype),
        compiler_params=_params("parallel", "parallel"),
    )(my_c, mine, theirs)


def _sum_blocks(parts, name):
    nb, r, ccols = parts.shape
    tr = _row_tile(r, ccols, nb)

    def body(p_ref, o_ref):
        acc = p_ref[0].astype(F32)
        for i in range(1, nb):
            acc = acc + p_ref[i].astype(F32)
        o_ref[...] = acc

    return pl.pallas_call(
        body, name=name, grid=(r // tr,),
        in_specs=[pl.BlockSpec((nb, tr, ccols), lambda i: (0, i, 0))],
        out_specs=pl.BlockSpec((tr, ccols), lambda i: (i, 0)),
        out_shape=jax.ShapeDtypeStruct((r, ccols), F32),
        compiler_params=_params("parallel"),
    )(parts)


@jax.custom_vjp
def replicated(vec):
    return vec


def _replicated_fwd(vec):
    return vec, None


def _replicated_bwd(_, ct):
    return (_sum_blocks(_allgather([ct], "small_grad_allgather")[0], "small_grad_sum"),)


replicated.defvjp(_replicated_fwd, _replicated_bwd)


def _adamw(w, g, m, v, name):
    rows, cols = w.shape
    tr = _pick(rows, 256, 8) if rows % 8 == 0 else rows

    def body(w_ref, g_ref, m_ref, v_ref, d_ref, nm_ref, nv_ref):
        g_ = g_ref[...]
        m_ = ADAM_B1 * m_ref[...] + (1.0 - ADAM_B1) * g_
        v_ = ADAM_B2 * v_ref[...] + (1.0 - ADAM_B2) * jnp.square(g_)
        m_hat = m_ / (1.0 - ADAM_B1 ** ADAM_STEP)
        v_hat = v_ / (1.0 - ADAM_B2 ** ADAM_STEP)
        d_ref[...] = -ADAM_LR * (m_hat / (jnp.sqrt(v_hat) + ADAM_EPS) + ADAM_WD * w_ref[...])
        nm_ref[...] = m_
        nv_ref[...] = v_

    spec = pl.BlockSpec((tr, cols), lambda i: (i, 0))
    return pl.pallas_call(
        body, name=name, grid=(rows // tr,), in_specs=[spec] * 4, out_specs=[spec] * 3,
        out_shape=[jax.ShapeDtypeStruct(w.shape, F32)] * 3, compiler_params=_params("parallel"),
    )(w, g, m, v)


COL_SHARDED = ("w_in", "w_uq", "w_ukv", "w_branch_a", "w_branch_b", "w_up", "w_ple")
EARLY = ("w_in",)
MID = ("w_uq", "w_ukv", "w_branch_a", "w_branch_b", "w_out")
LATE = ("w_up", "w_down", "w_ple_gate", "w_ple")
SMALL = ("attn_pre_norm", "attn_post_norm", "b_gate", "q_a_norm", "kv_a_norm", "mlp_pre_norm", "mlp_post_norm",
         "conv_b", "ple_norm", "sinks")
SMALL_COLS = 128


def _pack_rows(arrays, cols, row_mult):
    flat = jnp.concatenate([a.reshape(-1) for a in arrays])
    pad = (-flat.shape[0]) % (cols * row_mult)
    return jnp.pad(flat, (0, pad)).reshape(-1, cols)


def _unpack_small(vec, shapes):
    flat = vec.reshape(-1)
    out, off = {}, 0
    for name in SMALL:
        n = shapes[name]
        out[name] = flat[off:off + n].reshape(1, n)
        off += n + (-n) % SMALL_COLS
    return out


def _pad_lanes(t, width):
    return jnp.pad(t, [(0, 0)] * (t.ndim - 1) + [(0, width - t.shape[-1])])


def _pad_rows(t, rows):
    return jnp.pad(t, [(0, 0)] * (t.ndim - 2) + [(0, rows - t.shape[-2]), (0, 0)])


FRONT_SIZES = (512, 128, 128, 256, 128)
FRONT_BOUNDS = (0, 512, 640, 768, 1024, 1152, 1280)
PE_LANE = NOPE_DIM


def _arrange_w_in_t(wt):
    k = wt.shape[1]
    n_front = sum(FRONT_SIZES)
    front, kr, gates = wt[:n_front], wt[n_front:n_front + ROPE_DIM], wt[n_front + ROPE_DIM:]
    kr_slab = jnp.concatenate([jnp.zeros((PE_LANE, k), wt.dtype), kr,
                               jnp.zeros((HEAD_PAD - PE_LANE - ROPE_DIM, k), wt.dtype)], axis=0)
    return jnp.concatenate([front, kr_slab], axis=0), gates


def _arrange_w_uq_t(wt):
    k = wt.shape[1]
    return _pad_rows(wt.reshape(B_HEADS, NOPE_DIM + ROPE_DIM, k), HEAD_PAD).reshape(B_HEADS * HEAD_PAD, k)


def _arrange_w_ukv_t(wt):
    k = wt.shape[1]
    w = wt.reshape(B_HEADS, 2, NOPE_DIM, k)
    slabs = [_pad_rows(w[:, part], HEAD_PAD).reshape(B_HEADS * HEAD_PAD, k) for part in range(2)]
    return jnp.concatenate(slabs, axis=0)


def _rope_tables(positions, s):
    pos = positions.reshape(s, 1).astype(F32)

    def angles(dim):
        return pos * ROPE_THETA ** (-(jnp.arange(0, dim, 2, dtype=F32) / dim))

    cos_a, sin_a = jnp.cos(angles(A_HEAD_DIM)), jnp.sin(angles(A_HEAD_DIM))
    zero_a = jnp.zeros_like(sin_a)
    tables_a = [jnp.tile(jnp.concatenate(pair, axis=1), (1, LANES // A_HEAD_DIM))
                for pair in ((cos_a, cos_a), (-sin_a, zero_a), (zero_a, sin_a))]
    cos_b, sin_b = jnp.cos(angles(ROPE_DIM)), jnp.sin(angles(ROPE_DIM))
    zero_b = jnp.zeros_like(sin_b)

    def slab(first, second, fill):
        return jnp.concatenate([jnp.full((s, PE_LANE), fill, F32), first, second,
                                jnp.full((s, HEAD_PAD - PE_LANE - ROPE_DIM), fill, F32)], axis=1)

    tables_b = [slab(cos_b, cos_b, 1.0), slab(-sin_b, zero_b, 0.0), slab(zero_b, sin_b, 0.0)]
    return tables_a + tables_b


def _local_loss(wts, x, p, tables, target):
    s = x.shape[0]
    small_shapes = {n: wts[n].shape[-1] for n in SMALL}
    small_vec = _pack_rows([_pad_lanes(wts[n].reshape(1, -1), small_shapes[n] + (-small_shapes[n]) % SMALL_COLS)
                            for n in SMALL], SMALL_COLS, 8)
    sm = _unpack_small(replicated(small_vec), small_shapes)
    def shard(n):
        return wts[n].T if n in COL_SHARDED else wts[n]

    h1_front, h1_gates, x_res, gathered = prenorm_gather(
        x, sm["attn_pre_norm"], tuple([shard(n) for n in EARLY] + [_pack_rows([wts["conv_w"]], SMALL_COLS, 8)]),
        (BF16,) * len(EARLY) + (F32,))
    big = {n: g.reshape(-1, g.shape[2]) for n, g in zip(EARLY, gathered)}
    ch = wts["conv_w"].shape[1]
    conv_w = gathered[-1].reshape(N_DEV, -1)[:, :CONV_W * ch].reshape(N_DEV, CONV_W, ch)
    conv_w = conv_w.transpose(1, 0, 2).reshape(CONV_W, N_DEV * ch)

    w_front_t, w_gates_t = _arrange_w_in_t(big["w_in"])
    tables_a, tables_b = tables[:3], tables[3:]

    qa, ka, va, cqn, ckvn, kpe = proj_stage(
        "prep", _f_prep, [(h1_front, w_front_t, "nt", "w_front", True, F32)], params=[sm["q_a_norm"], sm["kv_a_norm"]],
        consts=tables, splits=[FRONT_BOUNDS], ts=512, out_dtypes=[BF16, BF16, BF16, BF16, BF16, F32])
    ya, mid = swa_nat(qa, ka, va, sm["sinks"].reshape(-1), tuple(shard(n) for n in MID))
    big.update({n: g.reshape(-1, g.shape[2]) for n, g in zip(MID, mid)})

    (q2,) = proj_stage("qrope", _f_qrope, [(cqn, _arrange_w_uq_t(big["w_uq"]), "nt", "w_uq", True, BF16)],
                       consts=tables_b, ts=512, out_dtypes=[BF16])
    k2, v2 = proj_stage("kv", _f_kv, [(ckvn, _arrange_w_ukv_t(big["w_ukv"]), "nt", "w_ukv", True, BF16)],
                        extra=[kpe], splits=[(0, B_HEADS * HEAD_PAD, 2 * B_HEADS * HEAD_PAD), None], ts=512,
                        out_dtypes=[BF16, BF16])
    yb, late = flash_nat(q2, k2, v2, tuple(shard(n) for n in LATE))
    big.update({n: g.reshape(-1, g.shape[2]) for n, g in zip(LATE, late)})

    (mixed,) = proj_stage(
        "gate", _f_gate, [(h1_gates, w_gates_t, "nt", "w_gates", True, F32),
                          (ya, big["w_branch_a"], "nt", "w_branch_a", True, BF16),
                          (yb, big["w_branch_b"], "nt", "w_branch_b", True, BF16)],
        params=[sm["b_gate"][:, :D_MODEL], sm["b_gate"][:, D_MODEL:]],
        splits=[(0, D_MODEL, 2 * D_MODEL), None, None], out_dtypes=[BF16])
    x1, h2 = proj_stage("post_attn", _f_post, [(mixed, big["w_out"], "nn", "w_out", True, F32)], extra=[x_res],
                        params=[sm["attn_post_norm"], sm["mlp_pre_norm"]], ts=512, out_dtypes=[F32, BF16])

    act = mlp_up(h2, big["w_up"], conv_w, sm["conv_b"])
    x2, h3 = proj_stage("post_mlp", _f_post, [(act, big["w_down"], "nn", "w_down", True, F32)], extra=[x1],
                        params=[sm["mlp_post_norm"], sm["ple_norm"]], ts=512, out_dtypes=[F32, BF16])

    (rowloss,) = proj_stage("loss", _f_out, [(h3, big["w_ple_gate"], "nn", "w_ple_gate", True, F32),
                                             (p, big["w_ple"], "nt", "w_ple", False, BF16)], extra=[x2],
                            consts=[target], ts=512)
    return jnp.sum(rowloss)


WEIGHTS = ["attn_pre_norm", "attn_post_norm", "w_in", "b_gate", "sinks", "q_a_norm", "w_uq", "kv_a_norm", "w_ukv",
           "w_branch_a", "w_branch_b", "w_out", "mlp_pre_norm", "mlp_post_norm", "w_up", "conv_w", "conv_b",
           "w_down", "ple_norm", "w_ple_gate", "w_ple"]


def kernel(x, p, positions, attn_pre_norm, attn_post_norm, w_in, b_gate, sinks, q_a_norm, w_uq, kv_a_norm, w_ukv, w_branch_a, w_branch_b, w_out, mlp_pre_norm, mlp_post_norm, w_up, conv_w, conv_b, w_down, ple_norm, w_ple_gate, w_ple, loss_target, m_attn_pre_norm, m_attn_post_norm, m_w_in, m_b_gate, m_sinks, m_q_a_norm, m_w_uq, m_kv_a_norm, m_w_ukv, m_w_branch_a, m_w_branch_b, m_w_out, m_mlp_pre_norm, m_mlp_post_norm, m_w_up, m_conv_w, m_conv_b, m_w_down, m_ple_norm, m_w_ple_gate, m_w_ple, v_attn_pre_norm, v_attn_post_norm, v_w_in, v_b_gate, v_sinks, v_q_a_norm, v_w_uq, v_kv_a_norm, v_w_ukv, v_w_branch_a, v_w_branch_b, v_w_out, v_mlp_pre_norm, v_mlp_post_norm, v_w_up, v_conv_w, v_conv_b, v_w_down, v_ple_norm, v_w_ple_gate, v_w_ple):
    given = dict(locals())
    s = x.shape[1]
    wts = {n: given[n][0] if given[n].ndim == 3 else given[n] for n in WEIGHTS}
    tables = _rope_tables(positions, s)
    local_loss, (grads, grad_x) = jax.value_and_grad(_local_loss, argnums=(0, 1))(
        wts, x[0], p[0, 0], tables, loss_target[0])
    loss = lax.psum(local_loss, AXES)

    outs = {"grad": [], "delta": [], "m": [], "v": []}
    for n in WEIGHTS:
        shape = given[n].shape
        w2 = wts[n].reshape(-1, shape[-1])
        g2 = grads[n].reshape(w2.shape)
        delta, new_m, new_v = _adamw(w2, g2, given["m_" + n].reshape(w2.shape), given["v_" + n].reshape(w2.shape),
                                     "adamw_" + n)
        outs["grad"].append(g2.reshape(shape))
        outs["delta"].append(delta.reshape(shape))
        outs["m"].append(new_m.reshape(shape))
        outs["v"].append(new_v.reshape(shape))
    return (loss, grad_x[None], *outs["grad"], *outs["delta"], *outs["m"], *outs["v"])
```

```python
import functools

import numpy as np
import jax
import jax.numpy as jnp
from jax import lax
from jax.experimental import pallas as pl
from jax.experimental.pallas import tpu as pltpu

F32 = jnp.float32
BF16 = jnp.bfloat16
MESH_ID = pl.DeviceIdType.MESH
AXES = ("x", "y", "c")
N_DEV = 8

D_MODEL = 1024
RMS_EPS = 1e-6
ROPE_THETA = 10000.0
SWA_BLOCK = 128
A_HEADS, A_KV_HEADS, A_HEAD_DIM = 8, 2, 64
A_GROUP = A_HEADS // A_KV_HEADS
B_HEADS, Q_LORA, KV_LORA, NOPE_DIM, ROPE_DIM, V_DIM = 8, 256, 128, 64, 32, 64
D_FF = 2816
CONV_W = 3
HEAD_PAD = 128

ADAM_LR, ADAM_B1, ADAM_B2, ADAM_EPS, ADAM_WD, ADAM_STEP = 0.001, 0.9, 0.999, 1e-08, 0.01, 10

VMEM_LIMIT = 48 * 1024 * 1024
MM_TM, MM_TN, MM_TK_TOKENS = 1024, 1408, 2048
MM_VMEM_BUDGET = 36 * 1024 * 1024
FLASH_T = 1024
CONV_TS = 256
CONV_CHUNK = 256


def _params(*sem):
    return pltpu.CompilerParams(dimension_semantics=sem, vmem_limit_bytes=VMEM_LIMIT)


def _pick(dim, cap, mult):
    best = None
    for t in range(mult, min(dim, cap) + 1, mult):
        if dim % t == 0:
            best = t
    return dim if best is None else best


def _divisors(dim, mult):
    return [t for t in range(mult, dim + 1, mult) if dim % t == 0] or [dim]


def _matmul_tiles(m, n, kdim, form, sizes):
    sa, sb, so = sizes
    tk = _pick(kdim, MM_TK_TOKENS, 128) if form == "tn" else kdim
    cap_m = MM_TN if form == "tn" else MM_TM
    best = None
    for tm in _divisors(m, 128):
        for tn in _divisors(n, 128):
            need = 2 * (tm * tk * sa + tk * tn * sb + tm * tn * so) + (tm * tn * 4 if tk != kdim else 0)
            if tm > cap_m or tn > MM_TN or need > MM_VMEM_BUDGET:
                continue
            if best is None or (tm * tn, tm) > (best[0] * best[1], best[0]):
                best = (tm, tn)
    return best[0], best[1], tk


def _matmul(a, b, form, *, out_dtype=F32, name):
    if form == "tn":
        (kdim, m), n = a.shape, b.shape[1]
    else:
        (m, kdim), n = a.shape, (b.shape[1] if form == "nn" else b.shape[0])
    sizes = (a.dtype.itemsize, b.dtype.itemsize, jnp.dtype(out_dtype).itemsize)
    tm, tn, tk = _matmul_tiles(m, n, kdim, form, sizes)
    nk = kdim // tk
    rows_outer = nk > 1 or (m // tm) * b.size * sizes[1] <= (n // tn) * a.size * sizes[0]

    def ij(fn):
        return (lambda i, j, k: fn(i, j, k)) if rows_outer else (lambda j, i, k: fn(i, j, k))

    a_spec = (pl.BlockSpec((tk, tm), ij(lambda i, j, k: (k, i))) if form == "tn"
              else pl.BlockSpec((tm, tk), ij(lambda i, j, k: (i, k))))
    b_spec = (pl.BlockSpec((tn, tk), ij(lambda i, j, k: (j, k))) if form == "nt"
              else pl.BlockSpec((tk, tn), ij(lambda i, j, k: (k, j))))
    dims = (((0 if form == "tn" else 1,), (1 if form == "nt" else 0,)), ((), ()))

    def product(a_ref, b_ref):
        return lax.dot_general(a_ref[...].astype(BF16), b_ref[...].astype(BF16), dims, preferred_element_type=F32)

    if nk == 1:
        def body(a_ref, b_ref, o_ref):
            o_ref[...] = product(a_ref, b_ref).astype(o_ref.dtype)

        scratch = []
    else:
        def body(a_ref, b_ref, o_ref, acc_ref):
            k = pl.program_id(2)

            @pl.when(k == 0)
            def _():
                acc_ref[...] = jnp.zeros_like(acc_ref)

            acc_ref[...] += product(a_ref, b_ref)

            @pl.when(k == nk - 1)
            def _():
                o_ref[...] = acc_ref[...].astype(o_ref.dtype)

        scratch = [pltpu.VMEM((tm, tn), F32)]

    return pl.pallas_call(
        body, name=name, grid=(m // tm, n // tn, nk) if rows_outer else (n // tn, m // tm, nk),
        in_specs=[a_spec, b_spec],
        out_specs=pl.BlockSpec((tm, tn), ij(lambda i, j, k: (i, j))),
        out_shape=jax.ShapeDtypeStruct((m, n), out_dtype),
        scratch_shapes=scratch,
        compiler_params=_params("parallel", "parallel", "arbitrary"),
    )(a, b)


def _pairs(bounds):
    return list(zip(bounds[:-1], bounds[1:]))


def _split(v, bounds):
    return [v[:, a:b] for a, b in _pairs(bounds)]


def _stage_build(name, f, tiled, params, consts, splits, ts, out_dtypes, ct_dtypes=None):
    n_t, n_p, n_c = len(tiled), len(params), len(consts)
    ct_dtypes = [t.dtype for t in tiled] if ct_dtypes is None else ct_dtypes
    s = tiled[0].shape[0]
    ts = min(ts, s)
    grid = (s // ts,)
    if splits is None:
        splits = [None] * n_t
    in_bounds = [(0, t.shape[1]) if b is None else tuple(b) for t, b in zip(tiled, splits)]

    def tile_aval(arr):
        return jax.ShapeDtypeStruct((ts, arr.shape[1]), arr.dtype)

    slab_avals = [[jax.ShapeDtypeStruct((ts, e - a), F32) for a, e in _pairs(b)]
                  for t, b in zip(tiled, in_bounds)]
    out_avals = jax.eval_shape(f, slab_avals, list(params), [tile_aval(c) for c in consts])
    out_bounds = [tuple(np.cumsum([0] + [o.shape[1] for o in slabs]).tolist()) for slabs in out_avals]
    out_dtypes = [F32] * len(out_bounds) if out_dtypes is None else out_dtypes
    out_shapes = [jax.ShapeDtypeStruct((s, b[-1]), d) for b, d in zip(out_bounds, out_dtypes)]

    def row_spec(width):
        return pl.BlockSpec((ts, width), lambda i: (i, 0))

    def par_spec(arr):
        return pl.BlockSpec(arr.shape, lambda i: (0, 0))

    in_specs = ([row_spec(t.shape[1]) for t in tiled] + [par_spec(p) for p in params]
                + [row_spec(c.shape[1]) for c in consts])

    def load(refs):
        t = [_split(r[...].astype(F32), b) for r, b in zip(refs[:n_t], in_bounds)]
        p = [r[...] for r in refs[n_t:n_t + n_p]]
        c = [r[...] for r in refs[n_t + n_p:n_t + n_p + n_c]]
        return t, p, c

    def store(refs, values, bounds):
        for ref, slabs, b in zip(refs, values, bounds):
            for v, (a, e) in zip(slabs, _pairs(b)):
                ref[:, a:e] = v.astype(ref.dtype)

    def run_fwd(tiled, params, consts):
        def body(*refs):
            t, p, c = load(refs)
            store(refs[n_t + n_p + n_c:], f(t, p, c), out_bounds)

        return pl.pallas_call(
            body, name=name + "_fwd", grid=grid, in_specs=in_specs,
            out_specs=[row_spec(b[-1]) for b in out_bounds], out_shape=out_shapes,
            compiler_params=_params("parallel"),
        )(*tiled, *params, *consts)

    def run_bwd(tiled, params, consts, cts):
        n_in = n_t + n_p + n_c
        n_o = len(out_bounds)

        def body(*refs):
            t, p, c = load(refs)
            g = [_split(r[...].astype(F32), b) for r, b in zip(refs[n_in:n_in + n_o], out_bounds)]
            _, pull = jax.vjp(lambda t_, p_: f(t_, p_, c), t, p)
            dt, dp = pull(g)
            store(refs[n_in + n_o:n_in + n_o + n_t], dt, in_bounds)
            first = pl.program_id(0) == 0
            for ref, d in zip(refs[n_in + n_o + n_t:], dp):
                @pl.when(first)
                def _(ref=ref):
                    ref[...] = jnp.zeros_like(ref)

                ref[...] += d

        res = pl.pallas_call(
            body, name=name + "_bwd", grid=grid,
            in_specs=in_specs + [row_spec(b[-1]) for b in out_bounds],
            out_specs=[row_spec(t.shape[1]) for t in tiled] + [par_spec(p) for p in params],
            out_shape=[jax.ShapeDtypeStruct(t.shape, d) for t, d in zip(tiled, ct_dtypes)]
                      + [jax.ShapeDtypeStruct(p.shape, F32) for p in params],
            compiler_params=_params("arbitrary"),
        )(*tiled, *params, *consts, *cts)
        return tuple(res[:n_t]), tuple(res[n_t:])

    return run_fwd, run_bwd


def proj_stage(name, f, projections, extra=(), params=(), consts=(), splits=None, ts=256, out_dtypes=None):
    n_z = len(projections)
    forms = [pr[2] for pr in projections]
    names = [pr[3] for pr in projections]
    need_da = [pr[4] for pr in projections]
    store = [pr[5] for pr in projections]
    extra, params, consts = tuple(extra), tuple(params), tuple(consts)

    def matmuls(a_list, w_list):
        return tuple(_matmul(a, w, form, out_dtype=dt, name=n + "_fwd")
                     for a, w, form, n, dt in zip(a_list, w_list, forms, names, store))

    def build(zs, ct=False):
        ct_dtypes = [BF16] * n_z + [e.dtype for e in extra] if ct else None
        return _stage_build(name, f, tuple(zs) + extra, params, consts, splits, ts, out_dtypes, ct_dtypes)

    @jax.custom_vjp
    def op(a_list, w_list, extra, params, consts):
        zs = matmuls(a_list, w_list)
        return tuple(build(zs)[0](zs + extra, params, consts))

    def op_fwd(a_list, w_list, extra, params, consts):
        zs = matmuls(a_list, w_list)
        return tuple(build(zs)[0](zs + extra, params, consts)), (a_list, w_list, zs, extra, params, consts)

    def op_bwd(res, cts):
        a_list, w_list, zs, extra, params, consts = res
        dt, dp = build(zs, ct=True)[1](zs + extra, params, consts, cts)
        da_list, dw_list = [], []
        for a, w, dz, form, n, want in zip(a_list, w_list, dt[:n_z], forms, names, need_da):
            if form == "nn":
                da = _matmul(dz, w, "nt", out_dtype=a.dtype, name=n + "_da") if want else jnp.zeros_like(a)
                dw = _matmul(a, dz, "tn", out_dtype=w.dtype, name=n + "_dw")
            else:
                da = _matmul(dz, w, "nn", out_dtype=a.dtype, name=n + "_da") if want else jnp.zeros_like(a)
                dw = _matmul(dz, a, "tn", out_dtype=w.dtype, name=n + "_dw")
            da_list.append(da)
            dw_list.append(dw)
        return tuple(da_list), tuple(dw_list), tuple(dt[n_z:]), dp, tuple(jnp.zeros_like(c) for c in consts)

    op.defvjp(op_fwd, op_bwd)
    return op(tuple(pr[0] for pr in projections), tuple(pr[1] for pr in projections), extra, params, consts)


def _rms(t, g):
    return t * lax.rsqrt(jnp.mean(t * t, axis=-1, keepdims=True) + RMS_EPS) * g


@functools.partial(jax.custom_vjp, nondiff_argnums=(1,))
def _lane_roll(t, shift):
    return pltpu.roll(t, shift % t.shape[-1], t.ndim - 1)


def _lane_roll_fwd(t, shift):
    return _lane_roll(t, shift), None


def _lane_roll_bwd(shift, _, ct):
    return (pltpu.roll(ct, (-shift) % ct.shape[-1], ct.ndim - 1),)


_lane_roll.defvjp(_lane_roll_fwd, _lane_roll_bwd)


def _rope_lanes(t, tables, half):
    reps = t.shape[1] // tables[0].shape[1]
    c, s_lo, s_hi = [jnp.concatenate([tb] * reps, axis=1) if reps > 1 else tb for tb in tables]
    return t * c + _lane_roll(t, -half) * s_lo + _lane_roll(t, half) * s_hi


PRENORM_TS = 256


def _prenorm_fwd_call(x, g, shards):
    s, width = x.shape
    ts = min(PRENORM_TS, s)
    nt = s // ts
    n_arr = len(shards)

    def body(*refs):
        x_ref, g_ref = refs[:2]
        o_ref = refs[2 + n_arr]
        i = pl.program_id(0)
        ag_start, ag_forward, ag_finish = _allgather_phases(refs[2:2 + n_arr], refs[3 + n_arr:3 + 2 * n_arr],
                                                            *refs[3 + 2 * n_arr:])

        @pl.when(i == 0)
        def _():
            ag_start()

        @pl.when(i == nt // 2)
        def _():
            ag_forward()

        o_ref[...] = _rms(x_ref[...], g_ref[...]).astype(o_ref.dtype)

        @pl.when(i == nt - 1)
        def _():
            ag_finish()

    return pl.pallas_call(
        body, name="prenorm_fwd", grid=(nt,),
        in_specs=[pl.BlockSpec((ts, width), lambda i: (i, 0)), pl.BlockSpec(g.shape, lambda i: (0, 0))]
                 + [HBM_SPEC] * n_arr,
        out_specs=[pl.BlockSpec((ts, width), lambda i: (i, 0))] + [HBM_SPEC] * n_arr,
        out_shape=[jax.ShapeDtypeStruct(x.shape, BF16)] + _allgather_out_shapes(shards),
        scratch_shapes=_allgather_sems(n_arr),
        compiler_params=_params("arbitrary"),
    )(x, g, *shards)


def _prenorm_bwd_call(x, g, dh_a, dh_b, dx_res, parts):
    s, width = x.shape
    ts = min(PRENORM_TS, s)
    nt = s // ts
    n_arr = len(parts)

    def body(*refs):
        x_ref, g_ref, dha_ref, dhb_ref, dxr_ref = refs[:5]
        dx_ref, dg_ref = refs[5 + n_arr:7 + n_arr]
        i = pl.program_id(0)
        exchange_start, exchange_finish = _exchange_chips_phases(
            refs[5:5 + n_arr], refs[7 + n_arr:7 + 2 * n_arr], *refs[7 + 2 * n_arr:])

        @pl.when(i == 0)
        def _():
            exchange_start()
            dg_ref[...] = jnp.zeros_like(dg_ref)

        _, pull = jax.vjp(_rms, x_ref[...], g_ref[...])
        dx, dg = pull(dha_ref[...].astype(F32) + dhb_ref[...].astype(F32))
        dx_ref[...] = dx + dxr_ref[...]
        dg_ref[...] += dg

        @pl.when(i == nt - 1)
        def _():
            exchange_finish()

    row = pl.BlockSpec((ts, width), lambda i: (i, 0))
    par = pl.BlockSpec(g.shape, lambda i: (0, 0))
    return pl.pallas_call(
        body, name="prenorm_bwd", grid=(nt,),
        in_specs=[row, par, row, row, row] + [HBM_SPEC] * n_arr,
        out_specs=[row, par] + [HBM_SPEC] * n_arr,
        out_shape=[jax.ShapeDtypeStruct(x.shape, F32), jax.ShapeDtypeStruct(g.shape, F32)]
                  + [jax.ShapeDtypeStruct(p.shape, p.dtype) for p in parts],
        scratch_shapes=_exchange_chips_sems(n_arr),
        compiler_params=_params("arbitrary"),
    )(x, g, dh_a, dh_b, dx_res, *parts)


@functools.partial(jax.custom_vjp, nondiff_argnums=(3,))
def prenorm_gather(x, g, shards, wire_dtypes):
    out = _prenorm_fwd_call(x, g, [s.astype(d) for s, d in zip(shards, wire_dtypes)])
    return out[0], out[0], x, tuple(out[1:])


def _prenorm_gather_fwd(x, g, shards, wire_dtypes):
    return prenorm_gather(x, g, shards, wire_dtypes), (x, g)


def _prenorm_gather_bwd(wire_dtypes, res, cts):
    x, g = res
    dh_a, dh_b, dx_res, d_gathered = cts
    out = _prenorm_bwd_call(x, g, dh_a, dh_b, dx_res, _reduce_scatter_head(d_gathered, "grads"))
    return out[0], out[1], _reduce_scatter_tail(out[2:], "grads")


prenorm_gather.defvjp(_prenorm_gather_fwd, _prenorm_gather_bwd)


def _f_prep(t, p, c):
    qa, ka, va, cq, ckv, kr = t[0]
    return [[_rope_lanes(qa, c[0:3], A_HEAD_DIM // 2)], [_rope_lanes(ka, c[0:3], A_HEAD_DIM // 2)], [va],
            [_rms(cq, p[0])], [_rms(ckv, p[1])], [_rope_lanes(kr, c[3:6], ROPE_DIM // 2)]]


def _f_qrope(t, p, c):
    return [[_rope_lanes(t[0][0], c, ROPE_DIM // 2)]]


def _f_kv(t, p, c):
    (k_nope, v), (k_pe,) = t
    return [[k_nope + jnp.concatenate([k_pe] * B_HEADS, axis=1)], [v]]


def _f_gate(t, p, c):
    (ga, gb), (pa,), (pb,) = t
    ba, bb = p
    return [[jax.nn.sigmoid(ga + ba) * pa + jax.nn.sigmoid(gb + bb) * pb]]


def _f_post(t, p, c):
    (branch,), (residual,) = t
    x1 = residual + _rms(branch, p[0])
    return [[x1], [_rms(x1, p[1])]]


def _f_out(t, p, c):
    (gate,), (emb,), (x2,) = t
    y = x2 + jax.nn.sigmoid(gate) * emb
    err = y - c[0]
    return [[0.5 * jnp.mean(err * err, axis=-1, keepdims=True)]]


def _shift_down(cur, prev, has_prev):
    full = jnp.concatenate([prev * has_prev, cur], axis=0)
    return pltpu.roll(full, 1, 0)[HALO:], pltpu.roll(full, 2, 0)[HALO:]


GELU_C = float(np.sqrt(2.0 / np.pi))
GELU_A = 0.044715
HALO = 8


def _gelu_tanh(x):
    x2 = x * x
    th = jnp.tanh(x * (GELU_C + (GELU_C * GELU_A) * x2))
    half = 0.5 + 0.5 * th
    return x * half, half + x * (0.5 - 0.5 * (th * th)) * (GELU_C + (3.0 * GELU_C * GELU_A) * x2)


def _row_sum(t):
    return jnp.sum(t, axis=0, keepdims=True)


def _conv3(cur, prev, w_ref, b_ref, has_prev):
    u1, u2 = _shift_down(cur, prev, has_prev)
    return w_ref[2:3, :] * cur + w_ref[1:2, :] * u1 + w_ref[0:1, :] * u2 + b_ref[...], u1, u2


def _mlp_act_specs(s):
    ts = min(CONV_TS, s)
    hb = ts // HALO

    def half_specs(h):
        return [pl.BlockSpec((ts, D_FF), lambda i: (i, h)),
                pl.BlockSpec((HALO, D_FF), lambda i: (jnp.maximum(i * hb - 1, 0), h))]

    def par_specs(h):
        return [pl.BlockSpec((CONV_W, D_FF), lambda i: (0, h)), pl.BlockSpec((1, D_FF), lambda i: (0, h))]

    return ts, hb, half_specs, par_specs


def _mlp_act_fwd_call(up, conv_w, conv_b):
    s = up.shape[0]
    ts, hb, half_specs, par_specs = _mlp_act_specs(s)

    def body(g_ref, gp_ref, v_ref, vp_ref, wg_ref, bg_ref, wv_ref, bv_ref, o_ref):
        has_prev = (pl.program_id(0) > 0).astype(F32)

        def chunk(cidx, carry):
            cols = pl.ds(pl.multiple_of(cidx * CONV_CHUNK, CONV_CHUNK), CONV_CHUNK)
            u_g, _, _ = _conv3(g_ref[:, cols], gp_ref[:, cols], wg_ref.at[:, cols], bg_ref.at[:, cols], has_prev)
            u_v, _, _ = _conv3(v_ref[:, cols], vp_ref[:, cols], wv_ref.at[:, cols], bv_ref.at[:, cols], has_prev)
            o_ref[:, cols] = (_gelu_tanh(u_g)[0] * u_v).astype(o_ref.dtype)
            return carry

        lax.fori_loop(0, D_FF // CONV_CHUNK, chunk, 0)

    return pl.pallas_call(
        body, name="mlp_act_fwd", grid=(s // ts,),
        in_specs=half_specs(0) + half_specs(1) + par_specs(0) + par_specs(1),
        out_specs=pl.BlockSpec((ts, D_FF), lambda i: (i, 0)),
        out_shape=jax.ShapeDtypeStruct((s, D_FF), BF16),
        compiler_params=_params("parallel"),
    )(up, up, up, up, conv_w, conv_b, conv_w, conv_b)


def _mlp_act_bwd_call(up, conv_w, conv_b, dact):
    s = up.shape[0]
    ts, hb, half_specs, par_specs = _mlp_act_specs(s)
    nt = s // ts
    ext = ts + HALO
    bf16_rows = 2 * HALO

    def next_spec(rows, h):
        return pl.BlockSpec((rows, D_FF), lambda i: (jnp.minimum((i + 1) * (ts // rows), s // rows - 1), h))

    def body(g_ref, gp_ref, gn_ref, v_ref, vp_ref, vn_ref, wg_ref, bg_ref, wv_ref, bv_ref, da_ref, dan_ref,
             dup_ref, dwg_ref, dbg_ref, dwv_ref, dbv_ref):
        i = pl.program_id(0)
        has_prev, has_next = (i > 0).astype(F32), (i < nt - 1).astype(F32)

        @pl.when(i == 0)
        def _():
            for ref in (dwg_ref, dbg_ref, dwv_ref, dbv_ref):
                ref[...] = jnp.zeros_like(ref)

        def chunk(cidx, carry):
            cols = pl.ds(pl.multiple_of(cidx * CONV_CHUNK, CONV_CHUNK), CONV_CHUNK)
            g_ext = jnp.concatenate([g_ref[:, cols], gn_ref[:, cols]], axis=0)
            v_ext = jnp.concatenate([v_ref[:, cols], vn_ref[:, cols]], axis=0)
            u_g, g1, g2 = _conv3(g_ext, gp_ref[:, cols], wg_ref.at[:, cols], bg_ref.at[:, cols], has_prev)
            u_v, v1, v2 = _conv3(v_ext, vp_ref[:, cols], wv_ref.at[:, cols], bv_ref.at[:, cols], has_prev)
            da_ext = jnp.concatenate([da_ref[:, cols].astype(F32),
                                      dan_ref[:, cols].astype(F32)[0:HALO] * has_next], axis=0)
            act_g, dact_g = _gelu_tanh(u_g)
            du_g = da_ext * u_v * dact_g
            du_v = da_ext * act_g
            for du, w_ref, x0, x1, x2, dw_ref, db_ref, lo in ((du_g, wg_ref, g_ext, g1, g2, dwg_ref, dbg_ref, 0),
                                                          (du_v, wv_ref, v_ext, v1, v2, dwv_ref, dbv_ref, D_FF)):
                d1 = pltpu.roll(du, ext - 1, 0)
                d2 = pltpu.roll(du, ext - 2, 0)
                dup = w_ref[2:3, cols] * du + w_ref[1:2, cols] * d1 + w_ref[0:1, cols] * d2
                out_cols = pl.ds(pl.multiple_of(lo + cidx * CONV_CHUNK, CONV_CHUNK), CONV_CHUNK)
                dup_ref[:, out_cols] = dup[0:ts].astype(dup_ref.dtype)
                own = du[0:ts]
                dw_ref[0:1, cols] += _row_sum(own * x2[0:ts])
                dw_ref[1:2, cols] += _row_sum(own * x1[0:ts])
                dw_ref[2:3, cols] += _row_sum(own * x0[0:ts])
                db_ref[:, cols] += _row_sum(own)
            return carry

        lax.fori_loop(0, D_FF // CONV_CHUNK, chunk, 0)

    par_out = [pl.BlockSpec((CONV_W, D_FF), lambda i: (0, 0)), pl.BlockSpec((1, D_FF), lambda i: (0, 0))]
    par_shapes = [jax.ShapeDtypeStruct((CONV_W, D_FF), F32), jax.ShapeDtypeStruct((1, D_FF), F32)]
    return pl.pallas_call(
        body, name="mlp_act_bwd", grid=(nt,),
        in_specs=(half_specs(0) + [next_spec(HALO, 0)] + half_specs(1) + [next_spec(HALO, 1)]
                  + par_specs(0) + par_specs(1)
                  + [pl.BlockSpec((ts, D_FF), lambda i: (i, 0)), next_spec(bf16_rows, 0)]),
        out_specs=[pl.BlockSpec((ts, 2 * D_FF), lambda i: (i, 0))] + par_out + par_out,
        out_shape=[jax.ShapeDtypeStruct((s, 2 * D_FF), BF16)] + par_shapes + par_shapes,
        compiler_params=_params("arbitrary"),
    )(up, up, up, up, up, up, conv_w, conv_b, conv_w, conv_b, dact, dact)


@jax.custom_vjp
def mlp_up(h2, w_up_t, conv_w, conv_b):
    return _mlp_act_fwd_call(_matmul(h2, w_up_t, "nt", out_dtype=F32, name="w_up_fwd"), conv_w, conv_b)


def _mlp_up_fwd(h2, w_up_t, conv_w, conv_b):
    up = _matmul(h2, w_up_t, "nt", out_dtype=F32, name="w_up_fwd")
    return _mlp_act_fwd_call(up, conv_w, conv_b), (h2, w_up_t, up, conv_w, conv_b)


def _mlp_up_bwd(res, dact):
    h2, w_up_t, up, conv_w, conv_b = res
    dup, dwg, dbg, dwv, dbv = _mlp_act_bwd_call(up, conv_w, conv_b, dact)
    dh2 = _matmul(dup, w_up_t, "nn", out_dtype=h2.dtype, name="w_up_da")
    dw = _matmul(dup, h2, "tn", out_dtype=w_up_t.dtype, name="w_up_dw")
    return dh2, dw, jnp.concatenate([dwg, dwv], axis=1), jnp.concatenate([dbg, dbv], axis=1)


mlp_up.defvjp(_mlp_up_fwd, _mlp_up_bwd)


SWA_ROWS = A_GROUP * SWA_BLOCK


def _swa_sink_rows(sink_ref, g):
    return jnp.concatenate([jnp.full((SWA_BLOCK, 1), sink_ref[g * A_GROUP + h], F32) for h in range(A_GROUP)], axis=0)


def _swa_operands(q_ref, kp_ref, kc_ref, vp_ref, vc_ref, sink_ref):
    groups = []
    for g in range(A_KV_HEADS):
        groups.append((_swa_stack_heads(q_ref, g), _dup_half(kp_ref[...], g), _dup_half(kc_ref[...], g),
                       _dup_half(vp_ref[...], g), _dup_half(vc_ref[...], g)))
    return groups, jnp.concatenate([_swa_sink_rows(sink_ref, g) for g in range(A_KV_HEADS)], axis=0)


def _swa_probs(groups, sink, prev_off):
    scale = A_HEAD_DIM ** -0.5
    sp = jnp.concatenate([lax.dot_general(gr[0], gr[1], NT_DIMS, preferred_element_type=F32) for gr in groups], axis=0)
    sc = jnp.concatenate([lax.dot_general(gr[0], gr[2], NT_DIMS, preferred_element_type=F32) for gr in groups], axis=0)
    qi = lax.broadcasted_iota(jnp.int32, sp.shape, 0) & (SWA_BLOCK - 1)
    kj = lax.broadcasted_iota(jnp.int32, sp.shape, 1)
    in_cur = kj <= qi
    sw = jnp.where(in_cur, sc, jnp.where(kj > qi + prev_off, sp, -jnp.inf)) * scale
    m = jnp.maximum(jnp.max(sw, axis=-1, keepdims=True), sink)
    e, es = jnp.exp(sw - m), jnp.exp(sink - m)
    den = jnp.sum(e, axis=-1, keepdims=True) + es
    return e / den, in_cur, es / den


def _swa_split(t, in_cur):
    cur = jnp.where(in_cur, t, 0.0)
    return t - cur, cur


MLA_SCALE = (NOPE_DIM + ROPE_DIM) ** -0.5
EXP2_SCALE = MLA_SCALE * float(np.log2(np.e))
NT_DIMS = (((1,), (1,)), ((), ()))
TN_DIMS = (((0,), (0,)), ((), ()))


LANES = 128
HALF = LANES // 2


def _low_half(shape):
    return lax.broadcasted_iota(jnp.int32, shape, len(shape) - 1) < HALF


def _dup_half(x, g):
    xf = x.astype(F32)
    keep = _low_half(xf.shape) if g == 0 else jnp.logical_not(_low_half(xf.shape))
    xm = jnp.where(keep, xf, 0.0)
    return (xm + pltpu.roll(xm, HALF, 1)).astype(x.dtype)


def _fold_half(r, g):
    total = r + pltpu.roll(r, HALF, 1)
    keep = _low_half(r.shape) if g == 0 else jnp.logical_not(_low_half(r.shape))
    return jnp.where(keep, total, 0.0)


def _swa_stack_heads(ref, g):
    parts = []
    for tile in range(2):
        slab = ref[:, (2 * g + tile) * LANES:(2 * g + tile + 1) * LANES]
        low = _low_half(slab.shape)
        parts += [jnp.where(low, slab, jnp.zeros_like(slab)), jnp.where(low, jnp.zeros_like(slab), slab)]
    return jnp.concatenate(parts, axis=0)


def _swa_unstack_heads(ref, g, rows):
    for tile in range(2):
        a = rows[(2 * tile) * SWA_BLOCK:(2 * tile + 1) * SWA_BLOCK]
        b = rows[(2 * tile + 1) * SWA_BLOCK:(2 * tile + 2) * SWA_BLOCK]
        ref[:, (2 * g + tile) * LANES:(2 * g + tile + 1) * LANES] = jnp.where(_low_half(a.shape), a, b).astype(ref.dtype)


def _swa_nat_specs():
    blk = SWA_BLOCK
    q_spec = pl.BlockSpec((blk, A_HEADS * A_HEAD_DIM), lambda n: (n, 0))
    prev_spec = pl.BlockSpec((blk, LANES), lambda n: (jnp.maximum(n - 1, 0), 0))
    cur_spec = pl.BlockSpec((blk, LANES), lambda n: (n, 0))
    return q_spec, prev_spec, cur_spec, pl.BlockSpec(memory_space=pltpu.SMEM)


def _swa_nat_fwd_call(q, k, v, sinks, shards):
    s = q.shape[0]
    nblk = s // SWA_BLOCK
    n_arr = len(shards)
    q_spec, prev_spec, cur_spec, sink_spec = _swa_nat_specs()

    def body(*refs):
        q_ref, kp_ref, kc_ref, vp_ref, vc_ref, sink_ref = refs[:6]
        o_ref = refs[6 + n_arr]
        n = pl.program_id(0)
        ag_start, ag_forward, ag_finish = _allgather_phases(refs[6:6 + n_arr], refs[7 + n_arr:7 + 2 * n_arr],
                                                            *refs[7 + 2 * n_arr:])

        @pl.when(n == 0)
        def _():
            ag_start()

        @pl.when(n == (3 * nblk) // 4)
        def _():
            ag_forward()

        prev_off = jnp.where(n > 0, 0, SWA_BLOCK)
        groups, sink = _swa_operands(q_ref, kp_ref, kc_ref, vp_ref, vc_ref, sink_ref)
        p, in_cur, _ = _swa_probs(groups, sink, prev_off)
        ppb, pcb = [t.astype(BF16) for t in _swa_split(p, in_cur)]
        for g, (_, _, _, vp, vc) in enumerate(groups):
            rows = slice(g * SWA_ROWS, (g + 1) * SWA_ROWS)
            out = (jnp.dot(ppb[rows], vp, preferred_element_type=F32)
                   + jnp.dot(pcb[rows], vc, preferred_element_type=F32))
            _swa_unstack_heads(o_ref, g, out)

        @pl.when(n == nblk - 1)
        def _():
            ag_finish()

    return pl.pallas_call(
        body, name="swa_fwd", grid=(nblk,),
        in_specs=[q_spec, prev_spec, cur_spec, prev_spec, cur_spec, sink_spec] + [HBM_SPEC] * n_arr,
        out_specs=[q_spec] + [HBM_SPEC] * n_arr,
        out_shape=[jax.ShapeDtypeStruct(q.shape, BF16)] + _allgather_out_shapes(shards),
        scratch_shapes=_allgather_sems(n_arr),
        compiler_params=_params("arbitrary"),
    )(q, k, k, v, v, sinks, *shards)


def _swa_nat_bwd_call(q, k, v, sinks, do, parts):
    s = q.shape[0]
    nblk = s // SWA_BLOCK
    n_arr = len(parts)
    q_spec, prev_spec, cur_spec, sink_spec = _swa_nat_specs()
    scale = A_HEAD_DIM ** -0.5
    dsink_spec = pl.BlockSpec((A_KV_HEADS, SWA_ROWS, 1), lambda n: (0, 0, 0))

    def body(*refs):
        q_ref, kp_ref, kc_ref, vp_ref, vc_ref, sink_ref, do_ref = refs[:7]
        dq_ref, dkp_ref, dkc_ref, dvp_ref, dvc_ref, dsink_ref = refs[7 + n_arr:13 + n_arr]
        n = pl.program_id(0)
        exchange_start, exchange_finish = _exchange_chips_phases(
            refs[7:7 + n_arr], refs[13 + n_arr:13 + 2 * n_arr], *refs[13 + 2 * n_arr:])

        @pl.when(n == 0)
        def _():
            exchange_start()
        prev_off = jnp.where(n > 0, 0, SWA_BLOCK)

        @pl.when(n == 0)
        def _():
            dsink_ref[...] = jnp.zeros_like(dsink_ref)

        groups, sink = _swa_operands(q_ref, kp_ref, kc_ref, vp_ref, vc_ref, sink_ref)
        dobs = [_swa_stack_heads(do_ref, g) for g in range(A_KV_HEADS)]
        p, in_cur, ps = _swa_probs(groups, sink, prev_off)
        ppb, pcb = [t.astype(BF16) for t in _swa_split(p, in_cur)]

        def per_group(fn):
            return jnp.concatenate([fn(g, slice(g * SWA_ROWS, (g + 1) * SWA_ROWS)) for g in range(A_KV_HEADS)], axis=0)

        out = per_group(lambda g, rows: jnp.dot(ppb[rows], groups[g][3], preferred_element_type=F32)
                        + jnp.dot(pcb[rows], groups[g][4], preferred_element_type=F32))
        delta = jnp.sum(jnp.concatenate(dobs, axis=0).astype(F32) * out, axis=-1, keepdims=True)
        dp = jnp.where(in_cur,
                       per_group(lambda g, rows: lax.dot_general(dobs[g], groups[g][4], NT_DIMS,
                                                                 preferred_element_type=F32)),
                       per_group(lambda g, rows: lax.dot_general(dobs[g], groups[g][3], NT_DIMS,
                                                                 preferred_element_type=F32)))
        dsp, dsc = [t.astype(BF16) for t in _swa_split(p * (dp - delta), in_cur)]
        dsink_ref[...] += (-ps * delta).reshape(dsink_ref.shape)
        totals = [jnp.zeros((SWA_BLOCK, LANES), F32) for _ in range(4)]
        for g, (qb, kp, kc, _, _) in enumerate(groups):
            rows = slice(g * SWA_ROWS, (g + 1) * SWA_ROWS)
            dq = (jnp.dot(dsp[rows], kp, preferred_element_type=F32)
                  + jnp.dot(dsc[rows], kc, preferred_element_type=F32)) * scale
            _swa_unstack_heads(dq_ref, g, dq)
            pieces = [lax.dot_general(dsp[rows], qb, TN_DIMS, preferred_element_type=F32) * scale,
                      lax.dot_general(dsc[rows], qb, TN_DIMS, preferred_element_type=F32) * scale,
                      lax.dot_general(ppb[rows], dobs[g], TN_DIMS, preferred_element_type=F32),
                      lax.dot_general(pcb[rows], dobs[g], TN_DIMS, preferred_element_type=F32)]
            totals = [tot + _fold_half(r, g) for tot, r in zip(totals, pieces)]
        dkp_ref[...], dkc_ref[...], dvp_ref[...], dvc_ref[...] = totals

        @pl.when(n == nblk - 1)
        def _():
            exchange_finish()

    kv_shape = jax.ShapeDtypeStruct(k.shape, F32)
    return pl.pallas_call(
        body, name="swa_bwd", grid=(nblk,),
        in_specs=[q_spec, prev_spec, cur_spec, prev_spec, cur_spec, sink_spec, q_spec] + [HBM_SPEC] * n_arr,
        out_specs=[q_spec, cur_spec, cur_spec, cur_spec, cur_spec, dsink_spec] + [HBM_SPEC] * n_arr,
        out_shape=[jax.ShapeDtypeStruct(q.shape, q.dtype), kv_shape, kv_shape, kv_shape, kv_shape,
                   jax.ShapeDtypeStruct((A_KV_HEADS, SWA_ROWS, 1), F32)]
                  + [jax.ShapeDtypeStruct(p.shape, p.dtype) for p in parts],
        scratch_shapes=_exchange_chips_sems(n_arr),
        compiler_params=_params("arbitrary"),
    )(q, k, k, v, v, sinks, do, *parts)


@jax.custom_vjp
def swa_nat(q, k, v, sinks, shards):
    out = _swa_nat_fwd_call(q, k, v, sinks, [s.astype(BF16) for s in shards])
    return out[0], tuple(out[1:])


def _swa_nat_fwd(q, k, v, sinks, shards):
    out = _swa_nat_fwd_call(q, k, v, sinks, [s.astype(BF16) for s in shards])
    return (out[0], tuple(out[1:])), (q, k, v, sinks)


def _swa_nat_bwd(res, cts):
    q, k, v, sinks = res
    do, d_gathered = cts
    out = _swa_nat_bwd_call(q, k, v, sinks, do, _reduce_scatter_head(d_gathered, "mid_grads"))
    dq, dkp, dkc, dvp, dvc, dsink = out[:6]

    def fold(prev_part, cur_part):
        shifted = jnp.concatenate([prev_part[SWA_BLOCK:], jnp.zeros_like(prev_part[:SWA_BLOCK])], axis=0)
        return (cur_part + shifted).astype(k.dtype)

    dsinks = jnp.sum(dsink.reshape(A_HEADS, SWA_BLOCK), axis=1)
    return dq, fold(dkp, dkc), fold(dvp, dvc), dsinks, _reduce_scatter_tail(out[6:], "mid_grads")


swa_nat.defvjp(_swa_nat_fwd, _swa_nat_bwd)

N_PAIR = B_HEADS // 2


def _flash_nat_fwd_call(q, k, v, shards):
    s = q.shape[0]
    t = min(FLASH_T, s)
    nb = s // t
    d = LANES
    n_arr = len(shards)

    def body(*refs):
        q_ref, k_ref, v_ref = refs[:3]
        shard_refs = refs[3:3 + n_arr]
        o_ref, lse_ref = refs[3 + n_arr:5 + n_arr]
        gathered_refs = refs[5 + n_arr:5 + 2 * n_arr]
        vt_ref, m_ref, l_ref, acc_ref = refs[5 + 2 * n_arr:9 + 2 * n_arr]
        pair, i = pl.program_id(0), pl.program_id(1)
        ag_start, ag_forward, ag_finish = _allgather_phases(shard_refs, gathered_refs, *refs[9 + 2 * n_arr:])

        @pl.when((pair == 0) & (i == 0))
        def _():
            ag_start()

        @pl.when((pair == N_PAIR - 1) & (i == nb // 2))
        def _():
            ag_forward()

        @pl.when(i == 0)
        def _():
            for hh in range(2):
                for chunk in range(nb):
                    rows = slice(chunk * t, (chunk + 1) * t)
                    vt_ref[hh, :, rows] = v_ref[rows, hh * d:(hh + 1) * d].T

        m_ref[...] = jnp.full_like(m_ref, -jnp.inf)
        l_ref[...] = jnp.zeros_like(l_ref)
        acc_ref[...] = jnp.zeros_like(acc_ref)

        def step(j, on_diagonal):
            keys = pl.ds(pl.multiple_of(j * t, t), t)
            scores = [lax.dot_general(k_ref[keys, hh * d:(hh + 1) * d], q_ref[:, hh * d:(hh + 1) * d], NT_DIMS,
                                      preferred_element_type=F32) for hh in range(2)]
            for hh in range(2):
                sc_t = scores[hh]
                if on_diagonal:
                    key = lax.broadcasted_iota(jnp.int32, (t, t), 0)
                    qry = lax.broadcasted_iota(jnp.int32, (t, t), 1)
                    sc_t = jnp.where(qry >= key, sc_t, -jnp.inf)
                m_old = m_ref[hh]
                m_new = jnp.maximum(m_old, jnp.max(sc_t, axis=0, keepdims=True))
                alpha = jnp.exp2((m_old - m_new) * EXP2_SCALE)
                p_t = jnp.exp2((sc_t - m_new) * EXP2_SCALE)
                l_ref[hh] = alpha * l_ref[hh] + jnp.sum(p_t, axis=0, keepdims=True)
                acc_ref[hh] = alpha * acc_ref[hh] + jnp.dot(vt_ref[hh, :, keys], p_t.astype(BF16),
                                                            preferred_element_type=F32)
                m_ref[hh] = m_new

        def below(j, carry):
            step(j, False)
            return carry

        lax.fori_loop(0, i, below, 0)
        step(i, True)
        outs =[(acc_ref[hh] / l_ref[hh]).T for hh in range(2)]
        for hh in range(2):
            lse_ref[hh] = m_ref[hh] * EXP2_SCALE + jnp.log2(l_ref[hh])
        o_ref[...] = (outs[0] + pltpu.roll(outs[1], HALF, 1)).astype(o_ref.dtype)

        @pl.when((pair == N_PAIR - 1) & (i == nb - 1))
        def _():
            ag_finish()

    return pl.pallas_call(
        body, name="mla_fwd", grid=(N_PAIR, nb),
        in_specs=[pl.BlockSpec((t, 2 * d), lambda p, i: (i, p)),
                  pl.BlockSpec((s, 2 * d), lambda p, i: (0, p)),
                  pl.BlockSpec((s, 2 * d), lambda p, i: (0, p))] + [HBM_SPEC] * n_arr,
        out_specs=[pl.BlockSpec((t, d), lambda p, i: (i, p)),
                   pl.BlockSpec((2, 1, t), lambda p, i: (p, 0, i))] + [HBM_SPEC] * n_arr,
        out_shape=[jax.ShapeDtypeStruct((s, N_PAIR * d), BF16), jax.ShapeDtypeStruct((B_HEADS, 1, s), F32)]
                  + _allgather_out_shapes(shards),
        scratch_shapes=[pltpu.VMEM((2, d, s), BF16), pltpu.VMEM((2, 1, t), F32), pltpu.VMEM((2, 1, t), F32),
                        pltpu.VMEM((2, d, t), F32)] + _allgather_sems(n_arr),
        compiler_params=_params("arbitrary", "arbitrary"),
    )(q, k, v, *shards)


def _flash_nat_delta_call(o, do):
    s, w = o.shape
    t = min(FLASH_T, s)

    def body(o_ref, do_ref, out_ref):
        prod = o_ref[...].astype(F32) * do_ref[...].astype(F32)
        lane = lax.broadcasted_iota(jnp.int32, (w, LANES), 0) // V_DIM
        head = lax.broadcasted_iota(jnp.int32, (w, LANES), 1)
        out_ref[...] = jnp.dot(prod, (lane == head).astype(F32), precision=lax.Precision.HIGHEST,
                               preferred_element_type=F32)

    spec = pl.BlockSpec((t, w), lambda i: (i, 0))
    return pl.pallas_call(
        body, name="mla_delta", grid=(s // t,), in_specs=[spec, spec],
        out_specs=pl.BlockSpec((t, LANES), lambda i: (i, 0)),
        out_shape=jax.ShapeDtypeStruct((s, LANES), F32), compiler_params=_params("parallel"),
    )(o, do)


def _flash_nat_bwd_call(q, k, v, lse_row, delta_row, do, parts):
    s = q.shape[0]
    t = min(FLASH_T, s)
    nb = s // t
    d = LANES
    n_arr = len(parts)

    def body(*refs):
        q_ref, k_ref, v_ref, lse_ref, delta_ref, do_ref = refs[:6]
        part_refs = refs[6:6 + n_arr]
        dq_ref, dk_ref, dv_ref = refs[6 + n_arr:9 + n_arr]
        received_refs = refs[9 + n_arr:9 + 2 * n_arr]
        dq_acc, dk_acc, dv_acc = refs[9 + 2 * n_arr:12 + 2 * n_arr]
        pair, j = pl.program_id(0), pl.program_id(1)
        exchange_start, exchange_finish = _exchange_chips_phases(part_refs, received_refs, *refs[12 + 2 * n_arr:])

        @pl.when((pair == 0) & (j == 0))
        def _():
            exchange_start()

        @pl.when(j == 0)
        def _():
            dq_acc[...] = jnp.zeros_like(dq_acc)

        for hh in range(2):
            kb, vb = k_ref[:, hh * d:(hh + 1) * d], v_ref[:, hh * d:(hh + 1) * d]
            dk_acc[...] = jnp.zeros_like(dk_acc)
            dv_acc[...] = jnp.zeros_like(dv_acc)

            def step(i, on_diagonal, hh=hh, kb=kb, vb=vb):
                rows = pl.ds(pl.multiple_of(i * t, t), t)
                qb = q_ref[rows, hh * d:(hh + 1) * d]
                do_pair = do_ref[rows, :].astype(F32)
                do_h = do_pair if hh == 0 else pltpu.roll(do_pair, HALF, 1)
                dob = jnp.where(_low_half(do_h.shape), do_h, 0.0).astype(BF16)
                sc_t = lax.dot_general(kb, qb, NT_DIMS, preferred_element_type=F32)
                p_t = jnp.exp2(sc_t * EXP2_SCALE - lse_ref[hh, :, rows])
                if on_diagonal:
                    key = lax.broadcasted_iota(jnp.int32, (t, t), 0)
                    qry = lax.broadcasted_iota(jnp.int32, (t, t), 1)
                    p_t = jnp.where(qry >= key, p_t, 0.0)
                dp_t = lax.dot_general(vb, dob, NT_DIMS, preferred_element_type=F32)
                ds_t = (p_t * (dp_t - delta_ref[hh, :, rows])).astype(BF16)
                dv_acc[...] += jnp.dot(p_t.astype(BF16), dob, preferred_element_type=F32)
                dk_acc[...] += jnp.dot(ds_t, qb, preferred_element_type=F32)
                dq_acc[hh, rows, :] += lax.dot_general(ds_t, kb, TN_DIMS, preferred_element_type=F32)

            def above(i, carry, step=step):
                step(i, False)
                return carry

            step(j, True)
            lax.fori_loop(j + 1, nb, above, 0)
            dk_ref[:, hh * d:(hh + 1) * d] = (dk_acc[...] * MLA_SCALE).astype(dk_ref.dtype)
            dv_ref[:, hh * d:(hh + 1) * d] = dv_acc[...].astype(dv_ref.dtype)

        @pl.when(j == nb - 1)
        def _():
            for hh in range(2):
                dq_ref[:, hh * d:(hh + 1) * d] = (dq_acc[hh] * MLA_SCALE).astype(dq_ref.dtype)

        @pl.when((pair == N_PAIR - 1) & (j == nb - 1))
        def _():
            exchange_finish()

    full_spec = pl.BlockSpec((s, 2 * d), lambda p, j: (0, p))
    tile_spec = pl.BlockSpec((t, 2 * d), lambda p, j: (j, p))
    row_spec = pl.BlockSpec((2, 1, s), lambda p, j: (p, 0, 0))
    return pl.pallas_call(
        body, name="mla_bwd", grid=(N_PAIR, nb),
        in_specs=[full_spec, tile_spec, tile_spec, row_spec, row_spec, pl.BlockSpec((s, d), lambda p, j: (0, p))]
                 + [HBM_SPEC] * n_arr,
        out_specs=[full_spec, tile_spec, tile_spec] + [HBM_SPEC] * n_arr,
        out_shape=[jax.ShapeDtypeStruct(q.shape, q.dtype)] * 3 + [jax.ShapeDtypeStruct(p.shape, p.dtype) for p in parts],
        scratch_shapes=[pltpu.VMEM((2, s, d), F32), pltpu.VMEM((t, d), F32), pltpu.VMEM((t, d), F32)]
                       + _exchange_chips_sems(n_arr),
        compiler_params=_params("arbitrary", "arbitrary"),
    )(q, k, v, lse_row, delta_row, do, *parts)


def _reduce_scatter_head(cts, tag):
    received = _exchange_sibling(list(cts), tag + "_exchange_sibling")
    my_c = lax.axis_index("c").astype(jnp.int32).reshape(1)
    return [_pair_add(m, r, my_c, "%s_pair_add_%d" % (tag, i)) for i, (m, r) in enumerate(zip(cts, received))]


def _reduce_scatter_tail(chip_parts, tag):
    return tuple(_sum_blocks(r, "%s_sum_%d" % (tag, i)) for i, r in enumerate(chip_parts))


@jax.custom_vjp
def flash_nat(q, k, v, shards):
    out = _flash_nat_fwd_call(q, k, v, [s.astype(BF16) for s in shards])
    return out[0], tuple(out[2:])


def _flash_nat_fwd(q, k, v, shards):
    out = _flash_nat_fwd_call(q, k, v, [s.astype(BF16) for s in shards])
    return (out[0], tuple(out[2:])), (q, k, v, out[0], out[1])


def _flash_nat_bwd(res, cts):
    q, k, v, o, lse = res
    do, d_gathered = cts
    delta = _flash_nat_delta_call(o, do)[:, :B_HEADS].T.reshape(B_HEADS, 1, q.shape[0])
    out = _flash_nat_bwd_call(q, k, v, lse, delta, do, _reduce_scatter_head(d_gathered, "mlp_grads"))
    return out[0], out[1], out[2], _reduce_scatter_tail(out[3:], "mlp_grads")


flash_nat.defvjp(_flash_nat_fwd, _flash_nat_bwd)


HBM_SPEC = pl.BlockSpec(memory_space=pltpu.HBM)


def _allgather(shards, name):
    n_arr = len(shards)

    def body(*refs):
        start, forward, finish = _allgather_phases(refs[:n_arr], refs[n_arr:2 * n_arr], *refs[2 * n_arr:])
        start()
        forward()
        finish()

    return pl.pallas_call(
        body, name=name, out_shape=_allgather_out_shapes(shards),
        in_specs=[HBM_SPEC] * n_arr, out_specs=[HBM_SPEC] * n_arr,
        scratch_shapes=_allgather_sems(n_arr),
    )(*shards)


def _allgather_out_shapes(shards):
    return [jax.ShapeDtypeStruct((N_DEV,) + s.shape, s.dtype) for s in shards]


def _allgather_sems(n_arr):
    return [pltpu.SemaphoreType.DMA((7, n_arr)), pltpu.SemaphoreType.DMA((7, n_arr)), pltpu.SemaphoreType.DMA((n_arr,))]


def _allgather_phases(x_refs, out_refs, send_sems, recv_sems, local_sems):
    arrays = range(len(x_refs))
    x, y, c = lax.axis_index("x"), lax.axis_index("y"), lax.axis_index("c")
    me, sibling = (x, y, c), (x, y, 1 - c)
    chips = [(1 - x, y), (x, 1 - y), (1 - x, 1 - y)]

    def rows(a, px, py, pc):
        return out_refs[a].at[4 * px + 2 * py + pc]

    def copy(a, k, block, to, src=None):
        return pltpu.make_async_remote_copy(
            src_ref=rows(a, *block) if src is None else src, dst_ref=rows(a, *block),
            send_sem=send_sems.at[k, a], recv_sem=recv_sems.at[k, a], device_id=to, device_id_type=MESH_ID)

    def mine():
        return [pltpu.make_async_copy(x_refs[a], rows(a, *me), local_sems.at[a]) for a in arrays]

    def first():
        return [cp for a in arrays for cp in
                [copy(a, 0, me, sibling, src=x_refs[a])]
                + [copy(a, 1 + j, me, (*chip, c), src=x_refs[a]) for j, chip in enumerate(chips)]]

    def passed():
        return [copy(a, 4 + j, (*chip, c), sibling) for j, chip in enumerate(chips) for a in arrays]

    def start():
        for cp in mine() + first():
            cp.start()

    def forward():
        for j, chip in enumerate(chips):
            for a in arrays:
                copy(a, 1 + j, (*chip, c), me).wait_recv()
                copy(a, 4 + j, (*chip, c), sibling).start()

    def finish():
        for a in arrays:
            copy(a, 0, sibling, me).wait_recv()
        for j, chip in enumerate(chips):
            for a in arrays:
                copy(a, 4 + j, (*chip, 1 - c), me).wait_recv()
        for cp in first() + passed():
            cp.wait_send()
        for cp in mine():
            cp.wait()

    return start, forward, finish


N_CHIP = 4


def _exchange_sibling(parts, name):
    n_arr = len(parts)

    def body(*refs):
        in_refs, recv_refs = refs[:n_arr], refs[n_arr:2 * n_arr]
        send_sems, recv_sems = refs[2 * n_arr:]
        x, y, c = lax.axis_index("x"), lax.axis_index("y"), lax.axis_index("c")
        copies = []
        for a in range(n_arr):
            for q in range(N_CHIP):
                copies.append(pltpu.make_async_remote_copy(
                    src_ref=in_refs[a].at[2 * q + 1 - c], dst_ref=recv_refs[a].at[q],
                    send_sem=send_sems.at[q, a], recv_sem=recv_sems.at[q, a],
                    device_id=(x, y, 1 - c), device_id_type=MESH_ID))
        for cp in copies:
            cp.start()
        for cp in copies:
            cp.wait()

    return pl.pallas_call(
        body, name=name, out_shape=[jax.ShapeDtypeStruct((N_CHIP,) + p.shape[1:], p.dtype) for p in parts],
        in_specs=[HBM_SPEC] * n_arr, out_specs=[HBM_SPEC] * n_arr,
        scratch_shapes=[pltpu.SemaphoreType.DMA((N_CHIP, n_arr)), pltpu.SemaphoreType.DMA((N_CHIP, n_arr))],
    )(*parts)


def _exchange_chips_sems(n_arr):
    return [pltpu.SemaphoreType.DMA((N_CHIP - 1, n_arr)), pltpu.SemaphoreType.DMA((N_CHIP - 1, n_arr)),
            pltpu.SemaphoreType.DMA((n_arr,))]


def _exchange_chips_phases(in_refs, out_refs, send_sems, recv_sems, local_sems):
    n_arr = len(in_refs)
    x, y, c = lax.axis_index("x"), lax.axis_index("y"), lax.axis_index("c")
    me = 2 * x + y

    def copies():
        out = [pltpu.make_async_copy(in_refs[a].at[me], out_refs[a].at[me], local_sems.at[a]) for a in range(n_arr)]
        for k in range(1, N_CHIP):
            px = 1 - x if k & 2 else x
            py = 1 - y if k & 1 else y
            for a in range(n_arr):
                out.append(pltpu.make_async_remote_copy(
                    src_ref=in_refs[a].at[2 * px + py], dst_ref=out_refs[a].at[me],
                    send_sem=send_sems.at[k - 1, a], recv_sem=recv_sems.at[k - 1, a],
                    device_id=(px, py, c), device_id_type=MESH_ID))
        return out

    def start():
        for cp in copies():
            cp.start()

    def finish():
        for cp in copies():
            cp.wait()

    return start, finish


def _row_tile(r, ccols, blocks):
    cap = max(16, (2 * 1024 * 1024) // (4 * ccols * blocks))
    return _pick(r, cap, 16)


def _pair_add(mine, theirs, my_c, name):
    _, r, ccols = mine.shape
    tr = _row_tile(r, ccols, 1)

    def body(c_ref, a_ref, b_ref, o_ref):
        o_ref[...] = (a_ref[...].astype(F32) + b_ref[...].astype(F32)).astype(o_ref.dtype)

    spec = pl.BlockSpec((None, tr, ccols), lambda q, i, c_ref: (q, i, 0))
    return pl.pallas_call(
        body, name=name,
        grid_spec=pltpu.PrefetchScalarGridSpec(
            num_scalar_prefetch=1, grid=(N_CHIP, r // tr),
            in_specs=[pl.BlockSpec((None, tr, ccols), lambda q, i, c_ref: (2 * q + c_ref[0], i, 0)), spec],
            out_specs=spec),
        out_shape=jax.ShapeDtypeStruct(theirs.shape, theirs.dtype),
        compiler_params=_params("parallel", "parallel"),
    )(my_c, mine, theirs)


def _sum_blocks(parts, name):
    nb, r, ccols = parts.shape
    tr = _row_tile(r, ccols, nb)

    def body(p_ref, o_ref):
        acc = p_ref[0].astype(F32)
        for i in range(1, nb):
            acc = acc + p_ref[i].astype(F32)
        o_ref[...] = acc

    return pl.pallas_call(
        body, name=name, grid=(r // tr,),
        in_specs=[pl.BlockSpec((nb, tr, ccols), lambda i: (0, i, 0))],
        out_specs=pl.BlockSpec((tr, ccols), lambda i: (i, 0)),
        out_shape=jax.ShapeDtypeStruct((r, ccols), F32),
        compiler_params=_params("parallel"),
    )(parts)


@jax.custom_vjp
def replicated(vec):
    return vec


def _replicated_fwd(vec):
    return vec, None


def _replicated_bwd(_, ct):
    return (_sum_blocks(_allgather([ct], "small_grad_allgather")[0], "small_grad_sum"),)


replicated.defvjp(_replicated_fwd, _replicated_bwd)


def _adamw(w, g, m, v, name):
    rows, cols = w.shape
    tr = _pick(rows, 256, 8) if rows % 8 == 0 else rows

    def body(w_ref, g_ref, m_ref, v_ref, d_ref, nm_ref, nv_ref):
        g_ = g_ref[...]
        m_ = ADAM_B1 * m_ref[...] + (1.0 - ADAM_B1) * g_
        v_ = ADAM_B2 * v_ref[...] + (1.0 - ADAM_B2) * jnp.square(g_)
        m_hat = m_ / (1.0 - ADAM_B1 ** ADAM_STEP)
        v_hat = v_ / (1.0 - ADAM_B2 ** ADAM_STEP)
        d_ref[...] = -ADAM_LR * (m_hat / (jnp.sqrt(v_hat) + ADAM_EPS) + ADAM_WD * w_ref[...])
        nm_ref[...] = m_
        nv_ref[...] = v_

    spec = pl.BlockSpec((tr, cols), lambda i: (i, 0))
    return pl.pallas_call(
        body, name=name, grid=(rows // tr,), in_specs=[spec] * 4, out_specs=[spec] * 3,
        out_shape=[jax.ShapeDtypeStruct(w.shape, F32)] * 3, compiler_params=_params("parallel"),
    )(w, g, m, v)


COL_SHARDED = ("w_in", "w_uq", "w_ukv", "w_branch_a", "w_branch_b", "w_up", "w_ple")
EARLY = ("w_in",)
MID = ("w_uq", "w_ukv", "w_branch_a", "w_branch_b", "w_out")
LATE = ("w_up", "w_down", "w_ple_gate", "w_ple")
SMALL = ("attn_pre_norm", "attn_post_norm", "b_gate", "q_a_norm", "kv_a_norm", "mlp_pre_norm", "mlp_post_norm",
         "conv_b", "ple_norm", "sinks")
SMALL_COLS = 128


def _pack_rows(arrays, cols, row_mult):
    flat = jnp.concatenate([a.reshape(-1) for a in arrays])
    pad = (-flat.shape[0]) % (cols * row_mult)
    return jnp.pad(flat, (0, pad)).reshape(-1, cols)


def _unpack_small(vec, shapes):
    flat = vec.reshape(-1)
    out, off = {}, 0
    for name in SMALL:
        n = shapes[name]
        out[name] = flat[off:off + n].reshape(1, n)
        off += n + (-n) % SMALL_COLS
    return out


def _pad_lanes(t, width):
    return jnp.pad(t, [(0, 0)] * (t.ndim - 1) + [(0, width - t.shape[-1])])


def _pad_rows(t, rows):
    return jnp.pad(t, [(0, 0)] * (t.ndim - 2) + [(0, rows - t.shape[-2]), (0, 0)])


FRONT_SIZES = (512, 128, 128, 256, 128)
FRONT_BOUNDS = (0, 512, 640, 768, 1024, 1152, 1280)
PE_LANE = NOPE_DIM


def _arrange_w_in_t(wt):
    k = wt.shape[1]
    n_front = sum(FRONT_SIZES)
    front, kr, gates = wt[:n_front], wt[n_front:n_front + ROPE_DIM], wt[n_front + ROPE_DIM:]
    kr_slab = jnp.concatenate([jnp.zeros((PE_LANE, k), wt.dtype), kr,
                               jnp.zeros((HEAD_PAD - PE_LANE - ROPE_DIM, k), wt.dtype)], axis=0)
    return jnp.concatenate([front, kr_slab], axis=0), gates


def _arrange_w_uq_t(wt):
    k = wt.shape[1]
    return _pad_rows(wt.reshape(B_HEADS, NOPE_DIM + ROPE_DIM, k), HEAD_PAD).reshape(B_HEADS * HEAD_PAD, k)


def _arrange_w_ukv_t(wt):
    k = wt.shape[1]
    w = wt.reshape(B_HEADS, 2, NOPE_DIM, k)
    slabs = [_pad_rows(w[:, part], HEAD_PAD).reshape(B_HEADS * HEAD_PAD, k) for part in range(2)]
    return jnp.concatenate(slabs, axis=0)


def _rope_tables(positions, s):
    pos = positions.reshape(s, 1).astype(F32)

    def angles(dim):
        return pos * ROPE_THETA ** (-(jnp.arange(0, dim, 2, dtype=F32) / dim))

    cos_a, sin_a = jnp.cos(angles(A_HEAD_DIM)), jnp.sin(angles(A_HEAD_DIM))
    zero_a = jnp.zeros_like(sin_a)
    tables_a = [jnp.tile(jnp.concatenate(pair, axis=1), (1, LANES // A_HEAD_DIM))
                for pair in ((cos_a, cos_a), (-sin_a, zero_a), (zero_a, sin_a))]
    cos_b, sin_b = jnp.cos(angles(ROPE_DIM)), jnp.sin(angles(ROPE_DIM))
    zero_b = jnp.zeros_like(sin_b)

    def slab(first, second, fill):
        return jnp.concatenate([jnp.full((s, PE_LANE), fill, F32), first, second,
                                jnp.full((s, HEAD_PAD - PE_LANE - ROPE_DIM), fill, F32)], axis=1)

    tables_b = [slab(cos_b, cos_b, 1.0), slab(-sin_b, zero_b, 0.0), slab(zero_b, sin_b, 0.0)]
    return tables_a + tables_b


def _local_loss(wts, x, p, tables, target):
    s = x.shape[0]
    small_shapes = {n: wts[n].shape[-1] for n in SMALL}
    small_vec = _pack_rows([_pad_lanes(wts[n].reshape(1, -1), small_shapes[n] + (-small_shapes[n]) % SMALL_COLS)
                            for n in SMALL], SMALL_COLS, 8)
    sm = _unpack_small(replicated(small_vec), small_shapes)
    def shard(n):
        return wts[n].T if n in COL_SHARDED else wts[n]

    h1_front, h1_gates, x_res, gathered = prenorm_gather(
        x, sm["attn_pre_norm"], tuple([shard(n) for n in EARLY] + [_pack_rows([wts["conv_w"]], SMALL_COLS, 8)]),
        (BF16,) * len(EARLY) + (F32,))
    big = {n: g.reshape(-1, g.shape[2]) for n, g in zip(EARLY, gathered)}
    ch = wts["conv_w"].shape[1]
    conv_w = gathered[-1].reshape(N_DEV, -1)[:, :CONV_W * ch].reshape(N_DEV, CONV_W, ch)
    conv_w = conv_w.transpose(1, 0, 2).reshape(CONV_W, N_DEV * ch)

    w_front_t, w_gates_t = _arrange_w_in_t(big["w_in"])
    tables_a, tables_b = tables[:3], tables[3:]

    qa, ka, va, cqn, ckvn, kpe = proj_stage(
        "prep", _f_prep, [(h1_front, w_front_t, "nt", "w_front", True, F32)], params=[sm["q_a_norm"], sm["kv_a_norm"]],
        consts=tables, splits=[FRONT_BOUNDS], ts=512, out_dtypes=[BF16, BF16, BF16, BF16, BF16, F32])
    ya, mid = swa_nat(qa, ka, va, sm["sinks"].reshape(-1), tuple(shard(n) for n in MID))
    big.update({n: g.reshape(-1, g.shape[2]) for n, g in zip(MID, mid)})

    (q2,) = proj_stage("qrope", _f_qrope, [(cqn, _arrange_w_uq_t(big["w_uq"]), "nt", "w_uq", True, BF16)],
                       consts=tables_b, ts=512, out_dtypes=[BF16])
    k2, v2 = proj_stage("kv", _f_kv, [(ckvn, _arrange_w_ukv_t(big["w_ukv"]), "nt", "w_ukv", True, BF16)],
                        extra=[kpe], splits=[(0, B_HEADS * HEAD_PAD, 2 * B_HEADS * HEAD_PAD), None], ts=512,
                        out_dtypes=[BF16, BF16])
    yb, late = flash_nat(q2, k2, v2, tuple(shard(n) for n in LATE))
    big.update({n: g.reshape(-1, g.shape[2]) for n, g in zip(LATE, late)})

    (mixed,) = proj_stage(
        "gate", _f_gate, [(h1_gates, w_gates_t, "nt", "w_gates", True, F32),
                          (ya, big["w_branch_a"], "nt", "w_branch_a", True, BF16),
                          (yb, big["w_branch_b"], "nt", "w_branch_b", True, BF16)],
        params=[sm["b_gate"][:, :D_MODEL], sm["b_gate"][:, D_MODEL:]],
        splits=[(0, D_MODEL, 2 * D_MODEL), None, None], out_dtypes=[BF16])
    x1, h2 = proj_stage("post_attn", _f_post, [(mixed, big["w_out"], "nn", "w_out", True, F32)], extra=[x_res],
                        params=[sm["attn_post_norm"], sm["mlp_pre_norm"]], ts=512, out_dtypes=[F32, BF16])

    act = mlp_up(h2, big["w_up"], conv_w, sm["conv_b"])
    x2, h3 = proj_stage("post_mlp", _f_post, [(act, big["w_down"], "nn", "w_down", True, F32)], extra=[x1],
                        params=[sm["mlp_post_norm"], sm["ple_norm"]], ts=512, out_dtypes=[F32, BF16])

    (rowloss,) = proj_stage("loss", _f_out, [(h3, big["w_ple_gate"], "nn", "w_ple_gate", True, F32),
                                             (p, big["w_ple"], "nt", "w_ple", False, BF16)], extra=[x2],
                            consts=[target], ts=512)
    return jnp.sum(rowloss)


WEIGHTS = ["attn_pre_norm", "attn_post_norm", "w_in", "b_gate", "sinks", "q_a_norm", "w_uq", "kv_a_norm", "w_ukv",
           "w_branch_a", "w_branch_b", "w_out", "mlp_pre_norm", "mlp_post_norm", "w_up", "conv_w", "conv_b",
           "w_down", "ple_norm", "w_ple_gate", "w_ple"]


def kernel(x, p, positions, attn_pre_norm, attn_post_norm, w_in, b_gate, sinks, q_a_norm, w_uq, kv_a_norm, w_ukv, w_branch_a, w_branch_b, w_out, mlp_pre_norm, mlp_post_norm, w_up, conv_w, conv_b, w_down, ple_norm, w_ple_gate, w_ple, loss_target, m_attn_pre_norm, m_attn_post_norm, m_w_in, m_b_gate, m_sinks, m_q_a_norm, m_w_uq, m_kv_a_norm, m_w_ukv, m_w_branch_a, m_w_branch_b, m_w_out, m_mlp_pre_norm, m_mlp_post_norm, m_w_up, m_conv_w, m_conv_b, m_w_down, m_ple_norm, m_w_ple_gate, m_w_ple, v_attn_pre_norm, v_attn_post_norm, v_w_in, v_b_gate, v_sinks, v_q_a_norm, v_w_uq, v_kv_a_norm, v_w_ukv, v_w_branch_a, v_w_branch_b, v_w_out, v_mlp_pre_norm, v_mlp_post_norm, v_w_up, v_conv_w, v_conv_b, v_w_down, v_ple_norm, v_w_ple_gate, v_w_ple):
    given = dict(locals())
    s = x.shape[1]
    wts = {n: given[n][0] if given[n].ndim == 3 else given[n] for n in WEIGHTS}
    tables = _rope_tables(positions, s)
    local_loss, (grads, grad_x) = jax.value_and_grad(_local_loss, argnums=(0, 1))(
        wts, x[0], p[0, 0], tables, loss_target[0])
    loss = lax.psum(local_loss, AXES)

    outs = {"grad": [], "delta": [], "m": [], "v": []}
    for n in WEIGHTS:
        shape = given[n].shape
        w2 = wts[n].reshape(-1, shape[-1])
        g2 = grads[n].reshape(w2.shape)
        delta, new_m, new_v = _adamw(w2, g2, given["m_" + n].reshape(w2.shape), given["v_" + n].reshape(w2.shape),
                                     "adamw_" + n)
        outs["grad"].append(g2.reshape(shape))
        outs["delta"].append(delta.reshape(shape))
        outs["m"].append(new_m.reshape(shape))
        outs["v"].append(new_v.reshape(shape))
    return (loss, grad_x[None], *outs["grad"], *outs["delta"], *outs["m"], *outs["v"])
```

```python
import functools

import numpy as np
import jax
import jax.numpy as jnp
from jax import lax
from jax.experimental import pallas as pl
from jax.experimental.pallas import tpu as pltpu

F32 = jnp.float32
BF16 = jnp.bfloat16
MESH_ID = pl.DeviceIdType.MESH
AXES = ("x", "y", "c")
N_DEV = 8

D_MODEL = 1024
RMS_EPS = 1e-6
ROPE_THETA = 10000.0
SWA_BLOCK = 128
A_HEADS, A_KV_HEADS, A_HEAD_DIM = 8, 2, 64
A_GROUP = A_HEADS // A_KV_HEADS
B_HEADS, Q_LORA, KV_LORA, NOPE_DIM, ROPE_DIM, V_DIM = 8, 256, 128, 64, 32, 64
D_FF = 2816
CONV_W = 3
HEAD_PAD = 128

ADAM_LR, ADAM_B1, ADAM_B2, ADAM_EPS, ADAM_WD, ADAM_STEP = 0.001, 0.9, 0.999, 1e-08, 0.01, 10

VMEM_LIMIT = 48 * 1024 * 1024
MM_TM, MM_TN, MM_TK_TOKENS = 1024, 1408, 2048
MM_VMEM_BUDGET = 36 * 1024 * 1024
FLASH_T = 1024
CONV_TS = 256
CONV_CHUNK = 256


def _params(*sem):
    return pltpu.CompilerParams(dimension_semantics=sem, vmem_limit_bytes=VMEM_LIMIT)


def _pick(dim, cap, mult):
    best = None
    for t in range(mult, min(dim, cap) + 1, mult):
        if dim % t == 0:
            best = t
    return dim if best is None else best


def _divisors(dim, mult):
    return [t for t in range(mult, dim + 1, mult) if dim % t == 0] or [dim]


def _matmul_tiles(m, n, kdim, form, sizes):
    sa, sb, so = sizes
    tk = _pick(kdim, MM_TK_TOKENS, 128) if form == "tn" else kdim
    cap_m = MM_TN if form == "tn" else MM_TM
    best = None
    for tm in _divisors(m, 128):
        for tn in _divisors(n, 128):
            need = 2 * (tm * tk * sa + tk * tn * sb + tm * tn * so) + (tm * tn * 4 if tk != kdim else 0)
            if tm > cap_m or tn > MM_TN or need > MM_VMEM_BUDGET:
                continue
            if best is None or (tm * tn, tm) > (best[0] * best[1], best[0]):
                best = (tm, tn)
    return best[0], best[1], tk


def _matmul(a, b, form, *, out_dtype=F32, name):
    if form == "tn":
        (kdim, m), n = a.shape, b.shape[1]
    else:
        (m, kdim), n = a.shape, (b.shape[1] if form == "nn" else b.shape[0])
    sizes = (a.dtype.itemsize, b.dtype.itemsize, jnp.dtype(out_dtype).itemsize)
    tm, tn, tk = _matmul_tiles(m, n, kdim, form, sizes)
    nk = kdim // tk
    rows_outer = nk > 1 or (m // tm) * b.size * sizes[1] <= (n // tn) * a.size * sizes[0]

    def ij(fn):
        return (lambda i, j, k: fn(i, j, k)) if rows_outer else (lambda j, i, k: fn(i, j, k))

    a_spec = (pl.BlockSpec((tk, tm), ij(lambda i, j, k: (k, i))) if form == "tn"
              else pl.BlockSpec((tm, tk), ij(lambda i, j, k: (i, k))))
    b_spec = (pl.BlockSpec((tn, tk), ij(lambda i, j, k: (j, k))) if form == "nt"
              else pl.BlockSpec((tk, tn), ij(lambda i, j, k: (k, j))))
    dims = (((0 if form == "tn" else 1,), (1 if form == "nt" else 0,)), ((), ()))

    def product(a_ref, b_ref):
        return lax.dot_general(a_ref[...].astype(BF16), b_ref[...].astype(BF16), dims, preferred_element_type=F32)

    if nk == 1:
        def body(a_ref, b_ref, o_ref):
            o_ref[...] = product(a_ref, b_ref).astype(o_ref.dtype)

        scratch = []
    else:
        def body(a_ref, b_ref, o_ref, acc_ref):
            k = pl.program_id(2)

            @pl.when(k == 0)
            def _():
                acc_ref[...] = jnp.zeros_like(acc_ref)

            acc_ref[...] += product(a_ref, b_ref)

            @pl.when(k == nk - 1)
            def _():
                o_ref[...] = acc_ref[...].astype(o_ref.dtype)

        scratch = [pltpu.VMEM((tm, tn), F32)]

    return pl.pallas_call(
        body, name=name, grid=(m // tm, n // tn, nk) if rows_outer else (n // tn, m // tm, nk),
        in_specs=[a_spec, b_spec],
        out_specs=pl.BlockSpec((tm, tn), ij(lambda i, j, k: (i, j))),
        out_shape=jax.ShapeDtypeStruct((m, n), out_dtype),
        scratch_shapes=scratch,
        compiler_params=_params("parallel", "parallel", "arbitrary"),
    )(a, b)


def _pairs(bounds):
    return list(zip(bounds[:-1], bounds[1:]))


def _split(v, bounds):
    return [v[:, a:b] for a, b in _pairs(bounds)]


def _stage_build(name, f, tiled, params, consts, splits, ts, out_dtypes, ct_dtypes=None):
    n_t, n_p, n_c = len(tiled), len(params), len(consts)
    ct_dtypes = [t.dtype for t in tiled] if ct_dtypes is None else ct_dtypes
    s = tiled[0].shape[0]
    ts = min(ts, s)
    grid = (s // ts,)
    if splits is None:
        splits = [None] * n_t
    in_bounds = [(0, t.shape[1]) if b is None else tuple(b) for t, b in zip(tiled, splits)]

    def tile_aval(arr):
        return jax.ShapeDtypeStruct((ts, arr.shape[1]), arr.dtype)

    slab_avals = [[jax.ShapeDtypeStruct((ts, e - a), F32) for a, e in _pairs(b)]
                  for t, b in zip(tiled, in_bounds)]
    out_avals = jax.eval_shape(f, slab_avals, list(params), [tile_aval(c) for c in consts])
    out_bounds = [tuple(np.cumsum([0] + [o.shape[1] for o in slabs]).tolist()) for slabs in out_avals]
    out_dtypes = [F32] * len(out_bounds) if out_dtypes is None else out_dtypes
    out_shapes = [jax.ShapeDtypeStruct((s, b[-1]), d) for b, d in zip(out_bounds, out_dtypes)]

    def row_spec(width):
        return pl.BlockSpec((ts, width), lambda i: (i, 0))

    def par_spec(arr):
        return pl.BlockSpec(arr.shape, lambda i: (0, 0))

    in_specs = ([row_spec(t.shape[1]) for t in tiled] + [par_spec(p) for p in params]
                + [row_spec(c.shape[1]) for c in consts])

    def load(refs):
        t = [_split(r[...].astype(F32), b) for r, b in zip(refs[:n_t], in_bounds)]
        p = [r[...] for r in refs[n_t:n_t + n_p]]
        c = [r[...] for r in refs[n_t + n_p:n_t + n_p + n_c]]
        return t, p, c

    def store(refs, values, bounds):
        for ref, slabs, b in zip(refs, values, bounds):
            for v, (a, e) in zip(slabs, _pairs(b)):
                ref[:, a:e] = v.astype(ref.dtype)

    def run_fwd(tiled, params, consts):
        def body(*refs):
            t, p, c = load(refs)
            store(refs[n_t + n_p + n_c:], f(t, p, c), out_bounds)

        return pl.pallas_call(
            body, name=name + "_fwd", grid=grid, in_specs=in_specs,
            out_specs=[row_spec(b[-1]) for b in out_bounds], out_shape=out_shapes,
            compiler_params=_params("parallel"),
        )(*tiled, *params, *consts)

    def run_bwd(tiled, params, consts, cts):
        n_in = n_t + n_p + n_c
        n_o = len(out_bounds)

        def body(*refs):
            t, p, c = load(refs)
            g = [_split(r[...].astype(F32), b) for r, b in zip(refs[n_in:n_in + n_o], out_bounds)]
            _, pull = jax.vjp(lambda t_, p_: f(t_, p_, c), t, p)
            dt, dp = pull(g)
            store(refs[n_in + n_o:n_in + n_o + n_t], dt, in_bounds)
            first = pl.program_id(0) == 0
            for ref, d in zip(refs[n_in + n_o + n_t:], dp):
                @pl.when(first)
                def _(ref=ref):
                    ref[...] = jnp.zeros_like(ref)

                ref[...] += d

        res = pl.pallas_call(
            body, name=name + "_bwd", grid=grid,
            in_specs=in_specs + [row_spec(b[-1]) for b in out_bounds],
            out_specs=[row_spec(t.shape[1]) for t in tiled] + [par_spec(p) for p in params],
            out_shape=[jax.ShapeDtypeStruct(t.shape, d) for t, d in zip(tiled, ct_dtypes)]
                      + [jax.ShapeDtypeStruct(p.shape, F32) for p in params],
            compiler_params=_params("arbitrary"),
        )(*tiled, *params, *consts, *cts)
        return tuple(res[:n_t]), tuple(res[n_t:])

    return run_fwd, run_bwd


def proj_stage(name, f, projections, extra=(), params=(), consts=(), splits=None, ts=256, out_dtypes=None):
    n_z = len(projections)
    forms = [pr[2] for pr in projections]
    names = [pr[3] for pr in projections]
    need_da = [pr[4] for pr in projections]
    store = [pr[5] for pr in projections]
    extra, params, consts = tuple(extra), tuple(params), tuple(consts)

    def matmuls(a_list, w_list):
        return tuple(_matmul(a, w, form, out_dtype=dt, name=n + "_fwd")
                     for a, w, form, n, dt in zip(a_list, w_list, forms, names, store))

    def build(zs, ct=False):
        ct_dtypes = [BF16] * n_z + [e.dtype for e in extra] if ct else None
        return _stage_build(name, f, tuple(zs) + extra, params, consts, splits, ts, out_dtypes, ct_dtypes)

    @jax.custom_vjp
    def op(a_list, w_list, extra, params, consts):
        zs = matmuls(a_list, w_list)
        return tuple(build(zs)[0](zs + extra, params, consts))

    def op_fwd(a_list, w_list, extra, params, consts):
        zs = matmuls(a_list, w_list)
        return tuple(build(zs)[0](zs + extra, params, consts)), (a_list, w_list, zs, extra, params, consts)

    def op_bwd(res, cts):
        a_list, w_list, zs, extra, params, consts = res
        dt, dp = build(zs, ct=True)[1](zs + extra, params, consts, cts)
        da_list, dw_list = [], []
        for a, w, dz, form, n, want in zip(a_list, w_list, dt[:n_z], forms, names, need_da):
            if form == "nn":
                da = _matmul(dz, w, "nt", out_dtype=a.dtype, name=n + "_da") if want else jnp.zeros_like(a)
                dw = _matmul(a, dz, "tn", out_dtype=w.dtype, name=n + "_dw")
            else:
                da = _matmul(dz, w, "nn", out_dtype=a.dtype, name=n + "_da") if want else jnp.zeros_like(a)
                dw = _matmul(dz, a, "tn", out_dtype=w.dtype, name=n + "_dw")
            da_list.append(da)
            dw_list.append(dw)
        return tuple(da_list), tuple(dw_list), tuple(dt[n_z:]), dp, tuple(jnp.zeros_like(c) for c in consts)

    op.defvjp(op_fwd, op_bwd)
    return op(tuple(pr[0] for pr in projections), tuple(pr[1] for pr in projections), extra, params, consts)


def _rms(t, g):
    return t * lax.rsqrt(jnp.mean(t * t, axis=-1, keepdims=True) + RMS_EPS) * g


@functools.partial(jax.custom_vjp, nondiff_argnums=(1,))
def _lane_roll(t, shift):
    return pltpu.roll(t, shift % t.shape[-1], t.ndim - 1)


def _lane_roll_fwd(t, shift):
    return _lane_roll(t, shift), None


def _lane_roll_bwd(shift, _, ct):
    return (pltpu.roll(ct, (-shift) % ct.shape[-1], ct.ndim - 1),)


_lane_roll.defvjp(_lane_roll_fwd, _lane_roll_bwd)


def _rope_lanes(t, tables, half):
    reps = t.shape[1] // tables[0].shape[1]
    c, s_lo, s_hi = [jnp.concatenate([tb] * reps, axis=1) if reps > 1 else tb for tb in tables]
    return t * c + _lane_roll(t, -half) * s_lo + _lane_roll(t, half) * s_hi


PRENORM_TS = 256


def _prenorm_fwd_call(x, g, shards):
    s, width = x.shape
    ts = min(PRENORM_TS, s)
    nt = s // ts
    n_arr = len(shards)

    def body(*refs):
        x_ref, g_ref = refs[:2]
        o_ref = refs[2 + n_arr]
        i = pl.program_id(0)
        ag_start, ag_forward, ag_finish = _allgather_phases(refs[2:2 + n_arr], refs[3 + n_arr:3 + 2 * n_arr],
                                                            *refs[3 + 2 * n_arr:])

        @pl.when(i == 0)
        def _():
            ag_start()

        @pl.when(i == nt - 1)
        def _():
            ag_forward()

        o_ref[...] = _rms(x_ref[...], g_ref[...]).astype(o_ref.dtype)

        @pl.when(i == nt - 1)
        def _():
            ag_finish()

    return pl.pallas_call(
        body, name="prenorm_fwd", grid=(nt,),
        in_specs=[pl.BlockSpec((ts, width), lambda i: (i, 0)), pl.BlockSpec(g.shape, lambda i: (0, 0))]
                 + [HBM_SPEC] * n_arr,
        out_specs=[pl.BlockSpec((ts, width), lambda i: (i, 0))] + [HBM_SPEC] * n_arr,
        out_shape=[jax.ShapeDtypeStruct(x.shape, BF16)] + _allgather_out_shapes(shards),
        scratch_shapes=_allgather_sems(n_arr),
        compiler_params=_params("arbitrary"),
    )(x, g, *shards)


def _prenorm_bwd_call(x, g, dh_a, dh_b, dx_res, parts):
    s, width = x.shape
    ts = min(PRENORM_TS, s)
    nt = s // ts
    n_arr = len(parts)

    def body(*refs):
        x_ref, g_ref, dha_ref, dhb_ref, dxr_ref = refs[:5]
        dx_ref, dg_ref = refs[5 + n_arr:7 + n_arr]
        i = pl.program_id(0)
        exchange_start, exchange_finish = _exchange_chips_phases(
            refs[5:5 + n_arr], refs[7 + n_arr:7 + 2 * n_arr], *refs[7 + 2 * n_arr:])

        @pl.when(i == 0)
        def _():
            exchange_start()
            dg_ref[...] = jnp.zeros_like(dg_ref)

        _, pull = jax.vjp(_rms, x_ref[...], g_ref[...])
        dx, dg = pull(dha_ref[...].astype(F32) + dhb_ref[...].astype(F32))
        dx_ref[...] = dx + dxr_ref[...]
        dg_ref[...] += dg

        @pl.when(i == nt - 1)
        def _():
            exchange_finish()

    row = pl.BlockSpec((ts, width), lambda i: (i, 0))
    par = pl.BlockSpec(g.shape, lambda i: (0, 0))
    return pl.pallas_call(
        body, name="prenorm_bwd", grid=(nt,),
        in_specs=[row, par, row, row, row] + [HBM_SPEC] * n_arr,
        out_specs=[row, par] + [HBM_SPEC] * n_arr,
        out_shape=[jax.ShapeDtypeStruct(x.shape, F32), jax.ShapeDtypeStruct(g.shape, F32)]
                  + [jax.ShapeDtypeStruct(p.shape, p.dtype) for p in parts],
        scratch_shapes=_exchange_chips_sems(n_arr),
        compiler_params=_params("arbitrary"),
    )(x, g, dh_a, dh_b, dx_res, *parts)


@functools.partial(jax.custom_vjp, nondiff_argnums=(3,))
def prenorm_gather(x, g, shards, wire_dtypes):
    out = _prenorm_fwd_call(x, g, [s.astype(d) for s, d in zip(shards, wire_dtypes)])
    return out[0], out[0], x, tuple(out[1:])


def _prenorm_gather_fwd(x, g, shards, wire_dtypes):
    return prenorm_gather(x, g, shards, wire_dtypes), (x, g)


def _prenorm_gather_bwd(wire_dtypes, res, cts):
    x, g = res
    dh_a, dh_b, dx_res, d_gathered = cts
    out = _prenorm_bwd_call(x, g, dh_a, dh_b, dx_res, _reduce_scatter_head(d_gathered, "grads"))
    return out[0], out[1], _reduce_scatter_tail(out[2:], "grads")


prenorm_gather.defvjp(_prenorm_gather_fwd, _prenorm_gather_bwd)


def _f_prep(t, p, c):
    qa, ka, va, cq, ckv, kr = t[0]
    return [[_rope_lanes(qa, c[0:3], A_HEAD_DIM // 2)], [_rope_lanes(ka, c[0:3], A_HEAD_DIM // 2)], [va],
            [_rms(cq, p[0])], [_rms(ckv, p[1])], [_rope_lanes(kr, c[3:6], ROPE_DIM // 2)]]


def _f_qrope(t, p, c):
    return [[_rope_lanes(t[0][0], c, ROPE_DIM // 2)]]


def _f_kv(t, p, c):
    (k_nope, v), (k_pe,) = t
    return [[k_nope + jnp.concatenate([k_pe] * B_HEADS, axis=1)], [v]]


def _f_gate(t, p, c):
    (ga, gb), (pa,), (pb,) = t
    ba, bb = p
    return [[jax.nn.sigmoid(ga + ba) * pa + jax.nn.sigmoid(gb + bb) * pb]]


def _f_post(t, p, c):
    (branch,), (residual,) = t
    x1 = residual + _rms(branch, p[0])
    return [[x1], [_rms(x1, p[1])]]


def _f_out(t, p, c):
    (gate,), (emb,), (x2,) = t
    y = x2 + jax.nn.sigmoid(gate) * emb
    err = y - c[0]
    return [[0.5 * jnp.mean(err * err, axis=-1, keepdims=True)]]


def _shift_down(cur, prev, has_prev):
    full = jnp.concatenate([prev * has_prev, cur], axis=0)
    return pltpu.roll(full, 1, 0)[HALO:], pltpu.roll(full, 2, 0)[HALO:]


GELU_C = float(np.sqrt(2.0 / np.pi))
GELU_A = 0.044715
HALO = 8


def _gelu_tanh(x):
    x2 = x * x
    th = jnp.tanh(x * (GELU_C + (GELU_C * GELU_A) * x2))
    half = 0.5 + 0.5 * th
    return x * half, half + x * (0.5 - 0.5 * (th * th)) * (GELU_C + (3.0 * GELU_C * GELU_A) * x2)


def _row_sum(t):
    return jnp.sum(t, axis=0, keepdims=True)


def _conv3(cur, prev, w_ref, b_ref, has_prev):
    u1, u2 = _shift_down(cur, prev, has_prev)
    return w_ref[2:3, :] * cur + w_ref[1:2, :] * u1 + w_ref[0:1, :] * u2 + b_ref[...], u1, u2


def _mlp_act_specs(s):
    ts = min(CONV_TS, s)
    hb = ts // HALO

    def half_specs(h):
        return [pl.BlockSpec((ts, D_FF), lambda i: (i, h)),
                pl.BlockSpec((HALO, D_FF), lambda i: (jnp.maximum(i * hb - 1, 0), h))]

    def par_specs(h):
        return [pl.BlockSpec((CONV_W, D_FF), lambda i: (0, h)), pl.BlockSpec((1, D_FF), lambda i: (0, h))]

    return ts, hb, half_specs, par_specs


def _mlp_act_fwd_call(up, conv_w, conv_b):
    s = up.shape[0]
    ts, hb, half_specs, par_specs = _mlp_act_specs(s)

    def body(g_ref, gp_ref, v_ref, vp_ref, wg_ref, bg_ref, wv_ref, bv_ref, o_ref):
        has_prev = (pl.program_id(0) > 0).astype(F32)

        def chunk(cidx, carry):
            cols = pl.ds(pl.multiple_of(cidx * CONV_CHUNK, CONV_CHUNK), CONV_CHUNK)
            u_g, _, _ = _conv3(g_ref[:, cols], gp_ref[:, cols], wg_ref.at[:, cols], bg_ref.at[:, cols], has_prev)
            u_v, _, _ = _conv3(v_ref[:, cols], vp_ref[:, cols], wv_ref.at[:, cols], bv_ref.at[:, cols], has_prev)
            o_ref[:, cols] = (_gelu_tanh(u_g)[0] * u_v).astype(o_ref.dtype)
            return carry

        lax.fori_loop(0, D_FF // CONV_CHUNK, chunk, 0)

    return pl.pallas_call(
        body, name="mlp_act_fwd", grid=(s // ts,),
        in_specs=half_specs(0) + half_specs(1) + par_specs(0) + par_specs(1),
        out_specs=pl.BlockSpec((ts, D_FF), lambda i: (i, 0)),
        out_shape=jax.ShapeDtypeStruct((s, D_FF), BF16),
        compiler_params=_params("parallel"),
    )(up, up, up, up, conv_w, conv_b, conv_w, conv_b)


def _mlp_act_bwd_call(up, conv_w, conv_b, dact):
    s = up.shape[0]
    ts, hb, half_specs, par_specs = _mlp_act_specs(s)
    nt = s // ts
    ext = ts + HALO
    bf16_rows = 2 * HALO

    def next_spec(rows, h):
        return pl.BlockSpec((rows, D_FF), lambda i: (jnp.minimum((i + 1) * (ts // rows), s // rows - 1), h))

    def body(g_ref, gp_ref, gn_ref, v_ref, vp_ref, vn_ref, wg_ref, bg_ref, wv_ref, bv_ref, da_ref, dan_ref,
             dup_ref, dwg_ref, dbg_ref, dwv_ref, dbv_ref):
        i = pl.program_id(0)
        has_prev, has_next = (i > 0).astype(F32), (i < nt - 1).astype(F32)

        @pl.when(i == 0)
        def _():
            for ref in (dwg_ref, dbg_ref, dwv_ref, dbv_ref):
                ref[...] = jnp.zeros_like(ref)

        def chunk(cidx, carry):
            cols = pl.ds(pl.multiple_of(cidx * CONV_CHUNK, CONV_CHUNK), CONV_CHUNK)
            g_ext = jnp.concatenate([g_ref[:, cols], gn_ref[:, cols]], axis=0)
            v_ext = jnp.concatenate([v_ref[:, cols], vn_ref[:, cols]], axis=0)
            u_g, g1, g2 = _conv3(g_ext, gp_ref[:, cols], wg_ref.at[:, cols], bg_ref.at[:, cols], has_prev)
            u_v, v1, v2 = _conv3(v_ext, vp_ref[:, cols], wv_ref.at[:, cols], bv_ref.at[:, cols], has_prev)
            da_ext = jnp.concatenate([da_ref[:, cols].astype(F32),
                                      dan_ref[:, cols].astype(F32)[0:HALO] * has_next], axis=0)
            act_g, dact_g = _gelu_tanh(u_g)
            du_g = da_ext * u_v * dact_g
            du_v = da_ext * act_g
            for du, w_ref, x0, x1, x2, dw_ref, db_ref, lo in ((du_g, wg_ref, g_ext, g1, g2, dwg_ref, dbg_ref, 0),
                                                          (du_v, wv_ref, v_ext, v1, v2, dwv_ref, dbv_ref, D_FF)):
                d1 = pltpu.roll(du, ext - 1, 0)
                d2 = pltpu.roll(du, ext - 2, 0)
                dup = w_ref[2:3, cols] * du + w_ref[1:2, cols] * d1 + w_ref[0:1, cols] * d2
                out_cols = pl.ds(pl.multiple_of(lo + cidx * CONV_CHUNK, CONV_CHUNK), CONV_CHUNK)
                dup_ref[:, out_cols] = dup[0:ts].astype(dup_ref.dtype)
                own = du[0:ts]
                dw_ref[0:1, cols] += _row_sum(own * x2[0:ts])
                dw_ref[1:2, cols] += _row_sum(own * x1[0:ts])
                dw_ref[2:3, cols] += _row_sum(own * x0[0:ts])
                db_ref[:, cols] += _row_sum(own)
            return carry

        lax.fori_loop(0, D_FF // CONV_CHUNK, chunk, 0)

    par_out = [pl.BlockSpec((CONV_W, D_FF), lambda i: (0, 0)), pl.BlockSpec((1, D_FF), lambda i: (0, 0))]
    par_shapes = [jax.ShapeDtypeStruct((CONV_W, D_FF), F32), jax.ShapeDtypeStruct((1, D_FF), F32)]
    return pl.pallas_call(
        body, name="mlp_act_bwd", grid=(nt,),
        in_specs=(half_specs(0) + [next_spec(HALO, 0)] + half_specs(1) + [next_spec(HALO, 1)]
                  + par_specs(0) + par_specs(1)
                  + [pl.BlockSpec((ts, D_FF), lambda i: (i, 0)), next_spec(bf16_rows, 0)]),
        out_specs=[pl.BlockSpec((ts, 2 * D_FF), lambda i: (i, 0))] + par_out + par_out,
        out_shape=[jax.ShapeDtypeStruct((s, 2 * D_FF), BF16)] + par_shapes + par_shapes,
        compiler_params=_params("arbitrary"),
    )(up, up, up, up, up, up, conv_w, conv_b, conv_w, conv_b, dact, dact)


@jax.custom_vjp
def mlp_up(h2, w_up_t, conv_w, conv_b):
    return _mlp_act_fwd_call(_matmul(h2, w_up_t, "nt", out_dtype=F32, name="w_up_fwd"), conv_w, conv_b)


def _mlp_up_fwd(h2, w_up_t, conv_w, conv_b):
    up = _matmul(h2, w_up_t, "nt", out_dtype=F32, name="w_up_fwd")
    return _mlp_act_fwd_call(up, conv_w, conv_b), (h2, w_up_t, up, conv_w, conv_b)


def _mlp_up_bwd(res, dact):
    h2, w_up_t, up, conv_w, conv_b = res
    dup, dwg, dbg, dwv, dbv = _mlp_act_bwd_call(up, conv_w, conv_b, dact)
    dh2 = _matmul(dup, w_up_t, "nn", out_dtype=h2.dtype, name="w_up_da")
    dw = _matmul(dup, h2, "tn", out_dtype=w_up_t.dtype, name="w_up_dw")
    return dh2, dw, jnp.concatenate([dwg, dwv], axis=1), jnp.concatenate([dbg, dbv], axis=1)


mlp_up.defvjp(_mlp_up_fwd, _mlp_up_bwd)


SWA_ROWS = A_GROUP * SWA_BLOCK


def _swa_sink_rows(sink_ref, g):
    return jnp.concatenate([jnp.full((SWA_BLOCK, 1), sink_ref[g * A_GROUP + h], F32) for h in range(A_GROUP)], axis=0)


def _swa_operands(q_ref, kp_ref, kc_ref, vp_ref, vc_ref, sink_ref):
    groups = []
    for g in range(A_KV_HEADS):
        groups.append((_swa_stack_heads(q_ref, g), _dup_half(kp_ref[...], g), _dup_half(kc_ref[...], g),
                       _dup_half(vp_ref[...], g), _dup_half(vc_ref[...], g)))
    return groups, jnp.concatenate([_swa_sink_rows(sink_ref, g) for g in range(A_KV_HEADS)], axis=0)


def _swa_probs(groups, sink, prev_off):
    scale = A_HEAD_DIM ** -0.5
    sp = jnp.concatenate([lax.dot_general(gr[0], gr[1], NT_DIMS, preferred_element_type=F32) for gr in groups], axis=0)
    sc = jnp.concatenate([lax.dot_general(gr[0], gr[2], NT_DIMS, preferred_element_type=F32) for gr in groups], axis=0)
    qi = lax.broadcasted_iota(jnp.int32, sp.shape, 0) & (SWA_BLOCK - 1)
    kj = lax.broadcasted_iota(jnp.int32, sp.shape, 1)
    in_cur = kj <= qi
    sw = jnp.where(in_cur, sc, jnp.where(kj > qi + prev_off, sp, -jnp.inf)) * scale
    m = jnp.maximum(jnp.max(sw, axis=-1, keepdims=True), sink)
    e, es = jnp.exp(sw - m), jnp.exp(sink - m)
    den = jnp.sum(e, axis=-1, keepdims=True) + es
    return e / den, in_cur, es / den


def _swa_split(t, in_cur):
    cur = jnp.where(in_cur, t, 0.0)
    return t - cur, cur


MLA_SCALE = (NOPE_DIM + ROPE_DIM) ** -0.5
EXP2_SCALE = MLA_SCALE * float(np.log2(np.e))
NT_DIMS = (((1,), (1,)), ((), ()))
TN_DIMS = (((0,), (0,)), ((), ()))


LANES = 128
HALF = LANES // 2


def _low_half(shape):
    return lax.broadcasted_iota(jnp.int32, shape, len(shape) - 1) < HALF


def _dup_half(x, g):
    xf = x.astype(F32)
    keep = _low_half(xf.shape) if g == 0 else jnp.logical_not(_low_half(xf.shape))
    xm = jnp.where(keep, xf, 0.0)
    return (xm + pltpu.roll(xm, HALF, 1)).astype(x.dtype)


def _fold_half(r, g):
    total = r + pltpu.roll(r, HALF, 1)
    keep = _low_half(r.shape) if g == 0 else jnp.logical_not(_low_half(r.shape))
    return jnp.where(keep, total, 0.0)


def _swa_stack_heads(ref, g):
    parts = []
    for tile in range(2):
        slab = ref[:, (2 * g + tile) * LANES:(2 * g + tile + 1) * LANES]
        low = _low_half(slab.shape)
        parts += [jnp.where(low, slab, jnp.zeros_like(slab)), jnp.where(low, jnp.zeros_like(slab), slab)]
    return jnp.concatenate(parts, axis=0)


def _swa_unstack_heads(ref, g, rows):
    for tile in range(2):
        a = rows[(2 * tile) * SWA_BLOCK:(2 * tile + 1) * SWA_BLOCK]
        b = rows[(2 * tile + 1) * SWA_BLOCK:(2 * tile + 2) * SWA_BLOCK]
        ref[:, (2 * g + tile) * LANES:(2 * g + tile + 1) * LANES] = jnp.where(_low_half(a.shape), a, b).astype(ref.dtype)


def _swa_nat_specs():
    blk = SWA_BLOCK
    q_spec = pl.BlockSpec((blk, A_HEADS * A_HEAD_DIM), lambda n: (n, 0))
    prev_spec = pl.BlockSpec((blk, LANES), lambda n: (jnp.maximum(n - 1, 0), 0))
    cur_spec = pl.BlockSpec((blk, LANES), lambda n: (n, 0))
    return q_spec, prev_spec, cur_spec, pl.BlockSpec(memory_space=pltpu.SMEM)


def _swa_nat_fwd_call(q, k, v, sinks, shards):
    s = q.shape[0]
    nblk = s // SWA_BLOCK
    n_arr = len(shards)
    q_spec, prev_spec, cur_spec, sink_spec = _swa_nat_specs()

    def body(*refs):
        q_ref, kp_ref, kc_ref, vp_ref, vc_ref, sink_ref = refs[:6]
        o_ref = refs[6 + n_arr]
        n = pl.program_id(0)
        ag_start, ag_forward, ag_finish = _allgather_phases(refs[6:6 + n_arr], refs[7 + n_arr:7 + 2 * n_arr],
                                                            *refs[7 + 2 * n_arr:])

        @pl.when(n == 0)
        def _():
            ag_start()

        @pl.when(n == (3 * nblk) // 4)
        def _():
            ag_forward()

        prev_off = jnp.where(n > 0, 0, SWA_BLOCK)
        groups, sink = _swa_operands(q_ref, kp_ref, kc_ref, vp_ref, vc_ref, sink_ref)
        p, in_cur, _ = _swa_probs(groups, sink, prev_off)
        ppb, pcb = [t.astype(BF16) for t in _swa_split(p, in_cur)]
        for g, (_, _, _, vp, vc) in enumerate(groups):
            rows = slice(g * SWA_ROWS, (g + 1) * SWA_ROWS)
            out = (jnp.dot(ppb[rows], vp, preferred_element_type=F32)
                   + jnp.dot(pcb[rows], vc, preferred_element_type=F32))
            _swa_unstack_heads(o_ref, g, out)

        @pl.when(n == nblk - 1)
        def _():
            ag_finish()

    return pl.pallas_call(
        body, name="swa_fwd", grid=(nblk,),
        in_specs=[q_spec, prev_spec, cur_spec, prev_spec, cur_spec, sink_spec] + [HBM_SPEC] * n_arr,
        out_specs=[q_spec] + [HBM_SPEC] * n_arr,
        out_shape=[jax.ShapeDtypeStruct(q.shape, BF16)] + _allgather_out_shapes(shards),
        scratch_shapes=_allgather_sems(n_arr),
        compiler_params=_params("arbitrary"),
    )(q, k, k, v, v, sinks, *shards)


def _swa_nat_bwd_call(q, k, v, sinks, do, parts):
    s = q.shape[0]
    nblk = s // SWA_BLOCK
    n_arr = len(parts)
    q_spec, prev_spec, cur_spec, sink_spec = _swa_nat_specs()
    scale = A_HEAD_DIM ** -0.5
    dsink_spec = pl.BlockSpec((A_KV_HEADS, SWA_ROWS, 1), lambda n: (0, 0, 0))

    def body(*refs):
        q_ref, kp_ref, kc_ref, vp_ref, vc_ref, sink_ref, do_ref = refs[:7]
        dq_ref, dkp_ref, dkc_ref, dvp_ref, dvc_ref, dsink_ref = refs[7 + n_arr:13 + n_arr]
        n = pl.program_id(0)
        exchange_start, exchange_finish = _exchange_chips_phases(
            refs[7:7 + n_arr], refs[13 + n_arr:13 + 2 * n_arr], *refs[13 + 2 * n_arr:])

        @pl.when(n == 0)
        def _():
            exchange_start()
        prev_off = jnp.where(n > 0, 0, SWA_BLOCK)

        @pl.when(n == 0)
        def _():
            dsink_ref[...] = jnp.zeros_like(dsink_ref)

        groups, sink = _swa_operands(q_ref, kp_ref, kc_ref, vp_ref, vc_ref, sink_ref)
        dobs = [_swa_stack_heads(do_ref, g) for g in range(A_KV_HEADS)]
        p, in_cur, ps = _swa_probs(groups, sink, prev_off)
        ppb, pcb = [t.astype(BF16) for t in _swa_split(p, in_cur)]

        def per_group(fn):
            return jnp.concatenate([fn(g, slice(g * SWA_ROWS, (g + 1) * SWA_ROWS)) for g in range(A_KV_HEADS)], axis=0)

        out = per_group(lambda g, rows: jnp.dot(ppb[rows], groups[g][3], preferred_element_type=F32)
                        + jnp.dot(pcb[rows], groups[g][4], preferred_element_type=F32))
        delta = jnp.sum(jnp.concatenate(dobs, axis=0).astype(F32) * out, axis=-1, keepdims=True)
        dp = jnp.where(in_cur,
                       per_group(lambda g, rows: lax.dot_general(dobs[g], groups[g][4], NT_DIMS,
                                                                 preferred_element_type=F32)),
                       per_group(lambda g, rows: lax.dot_general(dobs[g], groups[g][3], NT_DIMS,
                                                                 preferred_element_type=F32)))
        dsp, dsc = [t.astype(BF16) for t in _swa_split(p * (dp - delta), in_cur)]
        dsink_ref[...] += (-ps * delta).reshape(dsink_ref.shape)
        totals = [jnp.zeros((SWA_BLOCK, LANES), F32) for _ in range(4)]
        for g, (qb, kp, kc, _, _) in enumerate(groups):
            rows = slice(g * SWA_ROWS, (g + 1) * SWA_ROWS)
            dq = (jnp.dot(dsp[rows], kp, preferred_element_type=F32)
                  + jnp.dot(dsc[rows], kc, preferred_element_type=F32)) * scale
            _swa_unstack_heads(dq_ref, g, dq)
            pieces = [lax.dot_general(dsp[rows], qb, TN_DIMS, preferred_element_type=F32) * scale,
                      lax.dot_general(dsc[rows], qb, TN_DIMS, preferred_element_type=F32) * scale,
                      lax.dot_general(ppb[rows], dobs[g], TN_DIMS, preferred_element_type=F32),
                      lax.dot_general(pcb[rows], dobs[g], TN_DIMS, preferred_element_type=F32)]
            totals = [tot + _fold_half(r, g) for tot, r in zip(totals, pieces)]
        dkp_ref[...], dkc_ref[...], dvp_ref[...], dvc_ref[...] = totals

        @pl.when(n == nblk - 1)
        def _():
            exchange_finish()

    kv_shape = jax.ShapeDtypeStruct(k.shape, F32)
    return pl.pallas_call(
        body, name="swa_bwd", grid=(nblk,),
        in_specs=[q_spec, prev_spec, cur_spec, prev_spec, cur_spec, sink_spec, q_spec] + [HBM_SPEC] * n_arr,
        out_specs=[q_spec, cur_spec, cur_spec, cur_spec, cur_spec, dsink_spec] + [HBM_SPEC] * n_arr,
        out_shape=[jax.ShapeDtypeStruct(q.shape, q.dtype), kv_shape, kv_shape, kv_shape, kv_shape,
                   jax.ShapeDtypeStruct((A_KV_HEADS, SWA_ROWS, 1), F32)]
                  + [jax.ShapeDtypeStruct(p.shape, p.dtype) for p in parts],
        scratch_shapes=_exchange_chips_sems(n_arr),
        compiler_params=_params("arbitrary"),
    )(q, k, k, v, v, sinks, do, *parts)


@jax.custom_vjp
def swa_nat(q, k, v, sinks, shards):
    out = _swa_nat_fwd_call(q, k, v, sinks, [s.astype(BF16) for s in shards])
    return out[0], tuple(out[1:])


def _swa_nat_fwd(q, k, v, sinks, shards):
    out = _swa_nat_fwd_call(q, k, v, sinks, [s.astype(BF16) for s in shards])
    return (out[0], tuple(out[1:])), (q, k, v, sinks)


def _swa_nat_bwd(res, cts):
    q, k, v, sinks = res
    do, d_gathered = cts
    out = _swa_nat_bwd_call(q, k, v, sinks, do, _reduce_scatter_head(d_gathered, "mid_grads"))
    dq, dkp, dkc, dvp, dvc, dsink = out[:6]

    def fold(prev_part, cur_part):
        shifted = jnp.concatenate([prev_part[SWA_BLOCK:], jnp.zeros_like(prev_part[:SWA_BLOCK])], axis=0)
        return (cur_part + shifted).astype(k.dtype)

    dsinks = jnp.sum(dsink.reshape(A_HEADS, SWA_BLOCK), axis=1)
    return dq, fold(dkp, dkc), fold(dvp, dvc), dsinks, _reduce_scatter_tail(out[6:], "mid_grads")


swa_nat.defvjp(_swa_nat_fwd, _swa_nat_bwd)

N_PAIR = B_HEADS // 2


def _flash_nat_fwd_call(q, k, v, shards):
    s = q.shape[0]
    t = min(FLASH_T, s)
    nb = s // t
    d = LANES
    n_arr = len(shards)

    def body(*refs):
        q_ref, k_ref, v_ref = refs[:3]
        shard_refs = refs[3:3 + n_arr]
        o_ref, lse_ref = refs[3 + n_arr:5 + n_arr]
        gathered_refs = refs[5 + n_arr:5 + 2 * n_arr]
        vt_ref, m_ref, l_ref, acc_ref = refs[5 + 2 * n_arr:9 + 2 * n_arr]
        pair, i = pl.program_id(0), pl.program_id(1)
        ag_start, ag_forward, ag_finish = _allgather_phases(shard_refs, gathered_refs, *refs[9 + 2 * n_arr:])

        @pl.when((pair == 0) & (i == 0))
        def _():
            ag_start()

        @pl.when((pair == N_PAIR - 1) & (i == nb // 2))
        def _():
            ag_forward()

        @pl.when(i == 0)
        def _():
            for hh in range(2):
                for chunk in range(nb):
                    rows = slice(chunk * t, (chunk + 1) * t)
                    vt_ref[hh, :, rows] = v_ref[rows, hh * d:(hh + 1) * d].T

        m_ref[...] = jnp.full_like(m_ref, -jnp.inf)
        l_ref[...] = jnp.zeros_like(l_ref)
        acc_ref[...] = jnp.zeros_like(acc_ref)

        def step(j, on_diagonal):
            keys = pl.ds(pl.multiple_of(j * t, t), t)
            scores = [lax.dot_general(k_ref[keys, hh * d:(hh + 1) * d], q_ref[:, hh * d:(hh + 1) * d], NT_DIMS,
                                      preferred_element_type=F32) for hh in range(2)]
            for hh in range(2):
                sc_t = scores[hh]
                if on_diagonal:
                    key = lax.broadcasted_iota(jnp.int32, (t, t), 0)
                    qry = lax.broadcasted_iota(jnp.int32, (t, t), 1)
                    sc_t = jnp.where(qry >= key, sc_t, -jnp.inf)
                m_old = m_ref[hh]
                m_new = jnp.maximum(m_old, jnp.max(sc_t, axis=0, keepdims=True))
                alpha = jnp.exp2((m_old - m_new) * EXP2_SCALE)
                p_t = jnp.exp2((sc_t - m_new) * EXP2_SCALE)
                l_ref[hh] = alpha * l_ref[hh] + jnp.sum(p_t, axis=0, keepdims=True)
                acc_ref[hh] = alpha * acc_ref[hh] + jnp.dot(vt_ref[hh, :, keys], p_t.astype(BF16),
                                                            preferred_element_type=F32)
                m_ref[hh] = m_new

        def below(j, carry):
            step(j, False)
            return carry

        lax.fori_loop(0, i, below, 0)
        step(i, True)
        outs =[(acc_ref[hh] / l_ref[hh]).T for hh in range(2)]
        for hh in range(2):
            lse_ref[hh] = m_ref[hh] * EXP2_SCALE + jnp.log2(l_ref[hh])
        o_ref[...] = (outs[0] + pltpu.roll(outs[1], HALF, 1)).astype(o_ref.dtype)

        @pl.when((pair == N_PAIR - 1) & (i == nb - 1))
        def _():
            ag_finish()

    return pl.pallas_call(
        body, name="mla_fwd", grid=(N_PAIR, nb),
        in_specs=[pl.BlockSpec((t, 2 * d), lambda p, i: (i, p)),
                  pl.BlockSpec((s, 2 * d), lambda p, i: (0, p)),
                  pl.BlockSpec((s, 2 * d), lambda p, i: (0, p))] + [HBM_SPEC] * n_arr,
        out_specs=[pl.BlockSpec((t, d), lambda p, i: (i, p)),
                   pl.BlockSpec((2, 1, t), lambda p, i: (p, 0, i))] + [HBM_SPEC] * n_arr,
        out_shape=[jax.ShapeDtypeStruct((s, N_PAIR * d), BF16), jax.ShapeDtypeStruct((B_HEADS, 1, s), F32)]
                  + _allgather_out_shapes(shards),
        scratch_shapes=[pltpu.VMEM((2, d, s), BF16), pltpu.VMEM((2, 1, t), F32), pltpu.VMEM((2, 1, t), F32),
                        pltpu.VMEM((2, d, t), F32)] + _allgather_sems(n_arr),
        compiler_params=_params("arbitrary", "arbitrary"),
    )(q, k, v, *shards)


def _flash_nat_delta_call(o, do):
    s, w = o.shape
    t = min(FLASH_T, s)

    def body(o_ref, do_ref, out_ref):
        prod = o_ref[...].astype(F32) * do_ref[...].astype(F32)
        lane = lax.broadcasted_iota(jnp.int32, (w, LANES), 0) // V_DIM
        head = lax.broadcasted_iota(jnp.int32, (w, LANES), 1)
        out_ref[...] = jnp.dot(prod, (lane == head).astype(F32), precision=lax.Precision.HIGHEST,
                               preferred_element_type=F32)

    spec = pl.BlockSpec((t, w), lambda i: (i, 0))
    return pl.pallas_call(
        body, name="mla_delta", grid=(s // t,), in_specs=[spec, spec],
        out_specs=pl.BlockSpec((t, LANES), lambda i: (i, 0)),
        out_shape=jax.ShapeDtypeStruct((s, LANES), F32), compiler_params=_params("parallel"),
    )(o, do)


def _flash_nat_bwd_call(q, k, v, lse_row, delta_row, do, parts):
    s = q.shape[0]
    t = min(FLASH_T, s)
    nb = s // t
    d = LANES
    n_arr = len(parts)

    def body(*refs):
        q_ref, k_ref, v_ref, lse_ref, delta_ref, do_ref = refs[:6]
        part_refs = refs[6:6 + n_arr]
        dq_ref, dk_ref, dv_ref = refs[6 + n_arr:9 + n_arr]
        received_refs = refs[9 + n_arr:9 + 2 * n_arr]
        dq_acc, dk_acc, dv_acc = refs[9 + 2 * n_arr:12 + 2 * n_arr]
        pair, j = pl.program_id(0), pl.program_id(1)
        exchange_start, exchange_finish = _exchange_chips_phases(part_refs, received_refs, *refs[12 + 2 * n_arr:])

        @pl.when((pair == 0) & (j == 0))
        def _():
            exchange_start()

        @pl.when(j == 0)
        def _():
            dq_acc[...] = jnp.zeros_like(dq_acc)

        for hh in range(2):
            kb, vb = k_ref[:, hh * d:(hh + 1) * d], v_ref[:, hh * d:(hh + 1) * d]
            dk_acc[...] = jnp.zeros_like(dk_acc)
            dv_acc[...] = jnp.zeros_like(dv_acc)

            def step(i, on_diagonal, hh=hh, kb=kb, vb=vb):
                rows = pl.ds(pl.multiple_of(i * t, t), t)
                qb = q_ref[rows, hh * d:(hh + 1) * d]
                do_pair = do_ref[rows, :].astype(F32)
                do_h = do_pair if hh == 0 else pltpu.roll(do_pair, HALF, 1)
                dob = jnp.where(_low_half(do_h.shape), do_h, 0.0).astype(BF16)
                sc_t = lax.dot_general(kb, qb, NT_DIMS, preferred_element_type=F32)
                p_t = jnp.exp2(sc_t * EXP2_SCALE - lse_ref[hh, :, rows])
                if on_diagonal:
                    key = lax.broadcasted_iota(jnp.int32, (t, t), 0)
                    qry = lax.broadcasted_iota(jnp.int32, (t, t), 1)
                    p_t = jnp.where(qry >= key, p_t, 0.0)
                dp_t = lax.dot_general(vb, dob, NT_DIMS, preferred_element_type=F32)
                ds_t = (p_t * (dp_t - delta_ref[hh, :, rows])).astype(BF16)
                dv_acc[...] += jnp.dot(p_t.astype(BF16), dob, preferred_element_type=F32)
                dk_acc[...] += jnp.dot(ds_t, qb, preferred_element_type=F32)
                dq_acc[hh, rows, :] += lax.dot_general(ds_t, kb, TN_DIMS, preferred_element_type=F32)

            def above(i, carry, step=step):
                step(i, False)
                return carry

            step(j, True)
            lax.fori_loop(j + 1, nb, above, 0)
            dk_ref[:, hh * d:(hh + 1) * d] = (dk_acc[...] * MLA_SCALE).astype(dk_ref.dtype)
            dv_ref[:, hh * d:(hh + 1) * d] = dv_acc[...].astype(dv_ref.dtype)

        @pl.when(j == nb - 1)
        def _():
            for hh in range(2):
                dq_ref[:, hh * d:(hh + 1) * d] = (dq_acc[hh] * MLA_SCALE).astype(dq_ref.dtype)

        @pl.when((pair == N_PAIR - 1) & (j == nb - 1))
        def _():
            exchange_finish()

    full_spec = pl.BlockSpec((s, 2 * d), lambda p, j: (0, p))
    tile_spec = pl.BlockSpec((t, 2 * d), lambda p, j: (j, p))
    row_spec = pl.BlockSpec((2, 1, s), lambda p, j: (p, 0, 0))
    return pl.pallas_call(
        body, name="mla_bwd", grid=(N_PAIR, nb),
        in_specs=[full_spec, tile_spec, tile_spec, row_spec, row_spec, pl.BlockSpec((s, d), lambda p, j: (0, p))]
                 + [HBM_SPEC] * n_arr,
        out_specs=[full_spec, tile_spec, tile_spec] + [HBM_SPEC] * n_arr,
        out_shape=[jax.ShapeDtypeStruct(q.shape, q.dtype)] * 3 + [jax.ShapeDtypeStruct(p.shape, p.dtype) for p in parts],
        scratch_shapes=[pltpu.VMEM((2, s, d), F32), pltpu.VMEM((t, d), F32), pltpu.VMEM((t, d), F32)]
                       + _exchange_chips_sems(n_arr),
        compiler_params=_params("arbitrary", "arbitrary"),
    )(q, k, v, lse_row, delta_row, do, *parts)


def _reduce_scatter_head(cts, tag):
    received = _exchange_sibling(list(cts), tag + "_exchange_sibling")
    my_c = lax.axis_index("c").astype(jnp.int32).reshape(1)
    return [_pair_add(m, r, my_c, "%s_pair_add_%d" % (tag, i)) for i, (m, r) in enumerate(zip(cts, received))]


def _reduce_scatter_tail(chip_parts, tag):
    return tuple(_sum_blocks(r, "%s_sum_%d" % (tag, i)) for i, r in enumerate(chip_parts))


@jax.custom_vjp
def flash_nat(q, k, v, shards):
    out = _flash_nat_fwd_call(q, k, v, [s.astype(BF16) for s in shards])
    return out[0], tuple(out[2:])


def _flash_nat_fwd(q, k, v, shards):
    out = _flash_nat_fwd_call(q, k, v, [s.astype(BF16) for s in shards])
    return (out[0], tuple(out[2:])), (q, k, v, out[0], out[1])


def _flash_nat_bwd(res, cts):
    q, k, v, o, lse = res
    do, d_gathered = cts
    delta = _flash_nat_delta_call(o, do)[:, :B_HEADS].T.reshape(B_HEADS, 1, q.shape[0])
    out = _flash_nat_bwd_call(q, k, v, lse, delta, do, _reduce_scatter_head(d_gathered, "mlp_grads"))
    return out[0], out[1], out[2], _reduce_scatter_tail(out[3:], "mlp_grads")


flash_nat.defvjp(_flash_nat_fwd, _flash_nat_bwd)


HBM_SPEC = pl.BlockSpec(memory_space=pltpu.HBM)


def _allgather(shards, name):
    n_arr = len(shards)

    def body(*refs):
        start, forward, finish = _allgather_phases(refs[:n_arr], refs[n_arr:2 * n_arr], *refs[2 * n_arr:])
        start()
        forward()
        finish()

    return pl.pallas_call(
        body, name=name, out_shape=_allgather_out_shapes(shards),
        in_specs=[HBM_SPEC] * n_arr, out_specs=[HBM_SPEC] * n_arr,
        scratch_shapes=_allgather_sems(n_arr),
    )(*shards)


def _allgather_out_shapes(shards):
    return [jax.ShapeDtypeStruct((N_DEV,) + s.shape, s.dtype) for s in shards]


def _allgather_sems(n_arr):
    return [pltpu.SemaphoreType.DMA((7, n_arr)), pltpu.SemaphoreType.DMA((7, n_arr)), pltpu.SemaphoreType.DMA((n_arr,))]


def _allgather_phases(x_refs, out_refs, send_sems, recv_sems, local_sems):
    arrays = range(len(x_refs))
    x, y, c = lax.axis_index("x"), lax.axis_index("y"), lax.axis_index("c")
    me, sibling = (x, y, c), (x, y, 1 - c)
    chips = [(1 - x, y), (x, 1 - y), (1 - x, 1 - y)]

    def rows(a, px, py, pc):
        return out_refs[a].at[4 * px + 2 * py + pc]

    def copy(a, k, block, to, src=None):
        return pltpu.make_async_remote_copy(
            src_ref=rows(a, *block) if src is None else src, dst_ref=rows(a, *block),
            send_sem=send_sems.at[k, a], recv_sem=recv_sems.at[k, a], device_id=to, device_id_type=MESH_ID)

    def mine():
        return [pltpu.make_async_copy(x_refs[a], rows(a, *me), local_sems.at[a]) for a in arrays]

    def first():
        return [cp for a in arrays for cp in
                [copy(a, 0, me, sibling, src=x_refs[a])]
                + [copy(a, 1 + j, me, (*chip, c), src=x_refs[a]) for j, chip in enumerate(chips)]]

    def passed():
        return [copy(a, 4 + j, (*chip, c), sibling) for j, chip in enumerate(chips) for a in arrays]

    def start():
        for cp in mine() + first():
            cp.start()

    def forward():
        for j, chip in enumerate(chips):
            for a in arrays:
                copy(a, 1 + j, (*chip, c), me).wait_recv()
                copy(a, 4 + j, (*chip, c), sibling).start()

    def finish():
        for a in arrays:
            copy(a, 0, sibling, me).wait_recv()
        for j, chip in enumerate(chips):
            for a in arrays:
                copy(a, 4 + j, (*chip, 1 - c), me).wait_recv()
        for cp in first() + passed():
            cp.wait_send()
        for cp in mine():
            cp.wait()

    return start, forward, finish


N_CHIP = 4


def _exchange_sibling(parts, name):
    n_arr = len(parts)

    def body(*refs):
        in_refs, recv_refs = refs[:n_arr], refs[n_arr:2 * n_arr]
        send_sems, recv_sems = refs[2 * n_arr:]
        x, y, c = lax.axis_index("x"), lax.axis_index("y"), lax.axis_index("c")
        copies = []
        for a in range(n_arr):
            for q in range(N_CHIP):
                copies.append(pltpu.make_async_remote_copy(
                    src_ref=in_refs[a].at[2 * q + 1 - c], dst_ref=recv_refs[a].at[q],
                    send_sem=send_sems.at[q, a], recv_sem=recv_sems.at[q, a],
                    device_id=(x, y, 1 - c), device_id_type=MESH_ID))
        for cp in copies:
            cp.start()
        for cp in copies:
            cp.wait()

    return pl.pallas_call(
        body, name=name, out_shape=[jax.ShapeDtypeStruct((N_CHIP,) + p.shape[1:], p.dtype) for p in parts],
        in_specs=[HBM_SPEC] * n_arr, out_specs=[HBM_SPEC] * n_arr,
        scratch_shapes=[pltpu.SemaphoreType.DMA((N_CHIP, n_arr)), pltpu.SemaphoreType.DMA((N_CHIP, n_arr))],
    )(*parts)


def _exchange_chips_sems(n_arr):
    return [pltpu.SemaphoreType.DMA((N_CHIP - 1, n_arr)), pltpu.SemaphoreType.DMA((N_CHIP - 1, n_arr)),
            pltpu.SemaphoreType.DMA((n_arr,))]


def _exchange_chips_phases(in_refs, out_refs, send_sems, recv_sems, local_sems):
    n_arr = len(in_refs)
    x, y, c = lax.axis_index("x"), lax.axis_index("y"), lax.axis_index("c")
    me = 2 * x + y

    def copies():
        out = [pltpu.make_async_copy(in_refs[a].at[me], out_refs[a].at[me], local_sems.at[a]) for a in range(n_arr)]
        for k in range(1, N_CHIP):
            px = 1 - x if k & 2 else x
            py = 1 - y if k & 1 else y
            for a in range(n_arr):
                out.append(pltpu.make_async_remote_copy(
                    src_ref=in_refs[a].at[2 * px + py], dst_ref=out_refs[a].at[me],
                    send_sem=send_sems.at[k - 1, a], recv_sem=recv_sems.at[k - 1, a],
                    device_id=(px, py, c), device_id_type=MESH_ID))
        return out

    def start():
        for cp in copies():
            cp.start()

    def finish():
        for cp in copies():
            cp.wait()

    return start, finish


def _row_tile(r, ccols, blocks):
    cap = max(16, (2 * 1024 * 1024) // (4 * ccols * blocks))
    return _pick(r, cap, 16)


def _pair_add(mine, theirs, my_c, name):
    _, r, ccols = mine.shape
    tr = _row_tile(r, ccols, 1)

    def body(c_ref, a_ref, b_ref, o_ref):
        o_ref[...] = (a_ref[...].astype(F32) + b_ref[...].astype(F32)).astype(o_ref.dtype)

    spec = pl.BlockSpec((None, tr, ccols), lambda q, i, c_ref: (q, i, 0))
    return pl.pallas_call(
        body, name=name,
        grid_spec=pltpu.PrefetchScalarGridSpec(
            num_scalar_prefetch=1, grid=(N_CHIP, r // tr),
            in_specs=[pl.BlockSpec((None, tr, ccols), lambda q, i, c_ref: (2 * q + c_ref[0], i, 0)), spec],
            out_specs=spec),
        out_shape=jax.ShapeDtypeStruct(theirs.shape, theirs.dtype),
        compiler_params=_params("parallel", "parallel"),
    )(my_c, mine, theirs)


def _sum_blocks(parts, name):
    nb, r, ccols = parts.shape
    tr = _row_tile(r, ccols, nb)

    def body(p_ref, o_ref):
        acc = p_ref[0].astype(F32)
        for i in range(1, nb):
            acc = acc + p_ref[i].astype(F32)
        o_ref[...] = acc

    return pl.pallas_call(
        body, name=name, grid=(r // tr,),
        in_specs=[pl.BlockSpec((nb, tr, ccols), lambda i: (0, i, 0))],
        out_specs=pl.BlockSpec((tr, ccols), lambda i: (i, 0)),
        out_shape=jax.ShapeDtypeStruct((r, ccols), F32),
        compiler_params=_params("parallel"),
    )(parts)


@jax.custom_vjp
def replicated(vec):
    return vec


def _replicated_fwd(vec):
    return vec, None


def _replicated_bwd(_, ct):
    return (_sum_blocks(_allgather([ct], "small_grad_allgather")[0], "small_grad_sum"),)


replicated.defvjp(_replicated_fwd, _replicated_bwd)


def _adamw(w, g, m, v, name):
    rows, cols = w.shape
    tr = _pick(rows, 256, 8) if rows % 8 == 0 else rows

    def body(w_ref, g_ref, m_ref, v_ref, d_ref, nm_ref, nv_ref):
        g_ = g_ref[...]
        m_ = ADAM_B1 * m_ref[...] + (1.0 - ADAM_B1) * g_
        v_ = ADAM_B2 * v_ref[...] + (1.0 - ADAM_B2) * jnp.square(g_)
        m_hat = m_ / (1.0 - ADAM_B1 ** ADAM_STEP)
        v_hat = v_ / (1.0 - ADAM_B2 ** ADAM_STEP)
        d_ref[...] = -ADAM_LR * (m_hat / (jnp.sqrt(v_hat) + ADAM_EPS) + ADAM_WD * w_ref[...])
        nm_ref[...] = m_
        nv_ref[...] = v_

    spec = pl.BlockSpec((tr, cols), lambda i: (i, 0))
    return pl.pallas_call(
        body, name=name, grid=(rows // tr,), in_specs=[spec] * 4, out_specs=[spec] * 3,
        out_shape=[jax.ShapeDtypeStruct(w.shape, F32)] * 3, compiler_params=_params("parallel"),
    )(w, g, m, v)


COL_SHARDED = ("w_in", "w_uq", "w_ukv", "w_branch_a", "w_branch_b", "w_up", "w_ple")
EARLY = ("w_in",)
MID = ("w_uq", "w_ukv", "w_branch_a", "w_branch_b", "w_out")
LATE = ("w_up", "w_down", "w_ple_gate", "w_ple")
SMALL = ("attn_pre_norm", "attn_post_norm", "b_gate", "q_a_norm", "kv_a_norm", "mlp_pre_norm", "mlp_post_norm",
         "conv_b", "ple_norm", "sinks")
SMALL_COLS = 128


def _pack_rows(arrays, cols, row_mult):
    flat = jnp.concatenate([a.reshape(-1) for a in arrays])
    pad = (-flat.shape[0]) % (cols * row_mult)
    return jnp.pad(flat, (0, pad)).reshape(-1, cols)


def _unpack_small(vec, shapes):
    flat = vec.reshape(-1)
    out, off = {}, 0
    for name in SMALL:
        n = shapes[name]
        out[name] = flat[off:off + n].reshape(1, n)
        off += n + (-n) % SMALL_COLS
    return out


def _pad_lanes(t, width):
    return jnp.pad(t, [(0, 0)] * (t.ndim - 1) + [(0, width - t.shape[-1])])


def _pad_rows(t, rows):
    return jnp.pad(t, [(0, 0)] * (t.ndim - 2) + [(0, rows - t.shape[-2]), (0, 0)])


FRONT_SIZES = (512, 128, 128, 256, 128)
FRONT_BOUNDS = (0, 512, 640, 768, 1024, 1152, 1280)
PE_LANE = NOPE_DIM


def _arrange_w_in_t(wt):
    k = wt.shape[1]
    n_front = sum(FRONT_SIZES)
    front, kr, gates = wt[:n_front], wt[n_front:n_front + ROPE_DIM], wt[n_front + ROPE_DIM:]
    kr_slab = jnp.concatenate([jnp.zeros((PE_LANE, k), wt.dtype), kr,
                               jnp.zeros((HEAD_PAD - PE_LANE - ROPE_DIM, k), wt.dtype)], axis=0)
    return jnp.concatenate([front, kr_slab], axis=0), gates


def _arrange_w_uq_t(wt):
    k = wt.shape[1]
    return _pad_rows(wt.reshape(B_HEADS, NOPE_DIM + ROPE_DIM, k), HEAD_PAD).reshape(B_HEADS * HEAD_PAD, k)


def _arrange_w_ukv_t(wt):
    k = wt.shape[1]
    w = wt.reshape(B_HEADS, 2, NOPE_DIM, k)
    slabs = [_pad_rows(w[:, part], HEAD_PAD).reshape(B_HEADS * HEAD_PAD, k) for part in range(2)]
    return jnp.concatenate(slabs, axis=0)


def _rope_tables(positions, s):
    pos = positions.reshape(s, 1).astype(F32)

    def angles(dim):
        return pos * ROPE_THETA ** (-(jnp.arange(0, dim, 2, dtype=F32) / dim))

    cos_a, sin_a = jnp.cos(angles(A_HEAD_DIM)), jnp.sin(angles(A_HEAD_DIM))
    zero_a = jnp.zeros_like(sin_a)
    tables_a = [jnp.tile(jnp.concatenate(pair, axis=1), (1, LANES // A_HEAD_DIM))
                for pair in ((cos_a, cos_a), (-sin_a, zero_a), (zero_a, sin_a))]
    cos_b, sin_b = jnp.cos(angles(ROPE_DIM)), jnp.sin(angles(ROPE_DIM))
    zero_b = jnp.zeros_like(sin_b)

    def slab(first, second, fill):
        return jnp.concatenate([jnp.full((s, PE_LANE), fill, F32), first, second,
                                jnp.full((s, HEAD_PAD - PE_LANE - ROPE_DIM), fill, F32)], axis=1)

    tables_b = [slab(cos_b, cos_b, 1.0), slab(-sin_b, zero_b, 0.0), slab(zero_b, sin_b, 0.0)]
    return tables_a + tables_b


def _local_loss(wts, x, p, tables, target):
    s = x.shape[0]
    small_shapes = {n: wts[n].shape[-1] for n in SMALL}
    small_vec = _pack_rows([_pad_lanes(wts[n].reshape(1, -1), small_shapes[n] + (-small_shapes[n]) % SMALL_COLS)
                            for n in SMALL], SMALL_COLS, 8)
    sm = _unpack_small(replicated(small_vec), small_shapes)
    def shard(n):
        return wts[n].T if n in COL_SHARDED else wts[n]

    h1_front, h1_gates, x_res, gathered = prenorm_gather(
        x, sm["attn_pre_norm"], tuple([shard(n) for n in EARLY] + [_pack_rows([wts["conv_w"]], SMALL_COLS, 8)]),
        (BF16,) * len(EARLY) + (F32,))
    big = {n: g.reshape(-1, g.shape[2]) for n, g in zip(EARLY, gathered)}
    ch = wts["conv_w"].shape[1]
    conv_w = gathered[-1].reshape(N_DEV, -1)[:, :CONV_W * ch].reshape(N_DEV, CONV_W, ch)
    conv_w = conv_w.transpose(1, 0, 2).reshape(CONV_W, N_DEV * ch)

    w_front_t, w_gates_t = _arrange_w_in_t(big["w_in"])
    tables_a, tables_b = tables[:3], tables[3:]

    qa, ka, va, cqn, ckvn, kpe = proj_stage(
        "prep", _f_prep, [(h1_front, w_front_t, "nt", "w_front", True, F32)], params=[sm["q_a_norm"], sm["kv_a_norm"]],
        consts=tables, splits=[FRONT_BOUNDS], ts=512, out_dtypes=[BF16, BF16, BF16, BF16, BF16, F32])
    ya, mid = swa_nat(qa, ka, va, sm["sinks"].reshape(-1), tuple(shard(n) for n in MID))
    big.update({n: g.reshape(-1, g.shape[2]) for n, g in zip(MID, mid)})

    (q2,) = proj_stage("qrope", _f_qrope, [(cqn, _arrange_w_uq_t(big["w_uq"]), "nt", "w_uq", True, BF16)],
                       consts=tables_b, ts=512, out_dtypes=[BF16])
    k2, v2 = proj_stage("kv", _f_kv, [(ckvn, _arrange_w_ukv_t(big["w_ukv"]), "nt", "w_ukv", True, BF16)],
                        extra=[kpe], splits=[(0, B_HEADS * HEAD_PAD, 2 * B_HEADS * HEAD_PAD), None], ts=512,
                        out_dtypes=[BF16, BF16])
    yb, late = flash_nat(q2, k2, v2, tuple(shard(n) for n in LATE))
    big.update({n: g.reshape(-1, g.shape[2]) for n, g in zip(LATE, late)})

    (mixed,) = proj_stage(
        "gate", _f_gate, [(h1_gates, w_gates_t, "nt", "w_gates", True, F32),
                          (ya, big["w_branch_a"], "nt", "w_branch_a", True, BF16),
                          (yb, big["w_branch_b"], "nt", "w_branch_b", True, BF16)],
        params=[sm["b_gate"][:, :D_MODEL], sm["b_gate"][:, D_MODEL:]],
        splits=[(0, D_MODEL, 2 * D_MODEL), None, None], out_dtypes=[BF16])
    x1, h2 = proj_stage("post_attn", _f_post, [(mixed, big["w_out"], "nn", "w_out", True, F32)], extra=[x_res],
                        params=[sm["attn_post_norm"], sm["mlp_pre_norm"]], ts=512, out_dtypes=[F32, BF16])

    act = mlp_up(h2, big["w_up"], conv_w, sm["conv_b"])
    x2, h3 = proj_stage("post_mlp", _f_post, [(act, big["w_down"], "nn", "w_down", True, F32)], extra=[x1],
                        params=[sm["mlp_post_norm"], sm["ple_norm"]], ts=512, out_dtypes=[F32, BF16])

    (rowloss,) = proj_stage("loss", _f_out, [(h3, big["w_ple_gate"], "nn", "w_ple_gate", True, F32),
                                             (p, big["w_ple"], "nt", "w_ple", False, BF16)], extra=[x2],
                            consts=[target], ts=512)
    return jnp.sum(rowloss)


WEIGHTS = ["attn_pre_norm", "attn_post_norm", "w_in", "b_gate", "sinks", "q_a_norm", "w_uq", "kv_a_norm", "w_ukv",
           "w_branch_a", "w_branch_b", "w_out", "mlp_pre_norm", "mlp_post_norm", "w_up", "conv_w", "conv_b",
           "w_down", "ple_norm", "w_ple_gate", "w_ple"]


def kernel(x, p, positions, attn_pre_norm, attn_post_norm, w_in, b_gate, sinks, q_a_norm, w_uq, kv_a_norm, w_ukv, w_branch_a, w_branch_b, w_out, mlp_pre_norm, mlp_post_norm, w_up, conv_w, conv_b, w_down, ple_norm, w_ple_gate, w_ple, loss_target, m_attn_pre_norm, m_attn_post_norm, m_w_in, m_b_gate, m_sinks, m_q_a_norm, m_w_uq, m_kv_a_norm, m_w_ukv, m_w_branch_a, m_w_branch_b, m_w_out, m_mlp_pre_norm, m_mlp_post_norm, m_w_up, m_conv_w, m_conv_b, m_w_down, m_ple_norm, m_w_ple_gate, m_w_ple, v_attn_pre_norm, v_attn_post_norm, v_w_in, v_b_gate, v_sinks, v_q_a_norm, v_w_uq, v_kv_a_norm, v_w_ukv, v_w_branch_a, v_w_branch_b, v_w_out, v_mlp_pre_norm, v_mlp_post_norm, v_w_up, v_conv_w, v_conv_b, v_w_down, v_ple_norm, v_w_ple_gate, v_w_ple):
    given = dict(locals())
    s = x.shape[1]
    wts = {n: given[n][0] if given[n].ndim == 3 else given[n] for n in WEIGHTS}
    tables = _rope_tables(positions, s)
    local_loss, (grads, grad_x) = jax.value_and_grad(_local_loss, argnums=(0, 1))(
        wts, x[0], p[0, 0], tables, loss_target[0])
    loss = lax.psum(local_loss, AXES)

    outs = {"grad": [], "delta": [], "m": [], "v": []}
    for n in WEIGHTS:
        shape = given[n].shape
        w2 = wts[n].reshape(-1, shape[-1])
        g2 = grads[n].reshape(w2.shape)
        delta, new_m, new_v = _adamw(w2, g2, given["m_" + n].reshape(w2.shape), given["v_" + n].reshape(w2.shape),
                                     "adamw_" + n)
        outs["grad"].append(g2.reshape(shape))
        outs["delta"].append(delta.reshape(shape))
        outs["m"].append(new_m.reshape(shape))
        outs["v"].append(new_v.reshape(shape))
    return (loss, grad_x[None], *outs["grad"], *outs["delta"], *outs["m"], *outs["v"])
```

```python
import functools

import numpy as np
import jax
import jax.numpy as jnp
from jax import lax
from jax.experimental import pallas as pl
from jax.experimental.pallas import tpu as pltpu

F32 = jnp.float32
BF16 = jnp.bfloat16
MESH_ID = pl.DeviceIdType.MESH
AXES = ("x", "y", "c")
N_DEV = 8

D_MODEL = 1024
RMS_EPS = 1e-6
ROPE_THETA = 10000.0
SWA_BLOCK = 128
A_HEADS, A_KV_HEADS, A_HEAD_DIM = 8, 2, 64
A_GROUP = A_HEADS // A_KV_HEADS
B_HEADS, Q_LORA, KV_LORA, NOPE_DIM, ROPE_DIM, V_DIM = 8, 256, 128, 64, 32, 64
D_FF = 2816
CONV_W = 3
HEAD_PAD = 128

ADAM_LR, ADAM_B1, ADAM_B2, ADAM_EPS, ADAM_WD, ADAM_STEP = 0.001, 0.9, 0.999, 1e-08, 0.01, 10

VMEM_LIMIT = 48 * 1024 * 1024
MM_TM, MM_TN, MM_TK_TOKENS = 1024, 1408, 2048
MM_VMEM_BUDGET = 36 * 1024 * 1024
FLASH_T = 1024
CONV_TS = 256
CONV_CHUNK = 256


def _params(*sem):
    return pltpu.CompilerParams(dimension_semantics=sem, vmem_limit_bytes=VMEM_LIMIT)


def _pick(dim, cap, mult):
    best = None
    for t in range(mult, min(dim, cap) + 1, mult):
        if dim % t == 0:
            best = t
    return dim if best is None else best


def _divisors(dim, mult):
    return [t for t in range(mult, dim + 1, mult) if dim % t == 0] or [dim]


def _matmul_tiles(m, n, kdim, form, sizes):
    sa, sb, so = sizes
    tk = _pick(kdim, MM_TK_TOKENS, 128) if form == "tn" else kdim
    cap_m = MM_TN if form == "tn" else MM_TM
    best = None
    for tm in _divisors(m, 128):
        for tn in _divisors(n, 128):
            need = 2 * (tm * tk * sa + tk * tn * sb + tm * tn * so) + (tm * tn * 4 if tk != kdim else 0)
            if tm > cap_m or tn > MM_TN or need > MM_VMEM_BUDGET:
                continue
            if best is None or (tm * tn, tm) > (best[0] * best[1], best[0]):
                best = (tm, tn)
    return best[0], best[1], tk


def _matmul(a, b, form, *, out_dtype=F32, name):
    if form == "tn":
        (kdim, m), n = a.shape, b.shape[1]
    else:
        (m, kdim), n = a.shape, (b.shape[1] if form == "nn" else b.shape[0])
    sizes = (a.dtype.itemsize, b.dtype.itemsize, jnp.dtype(out_dtype).itemsize)
    tm, tn, tk = _matmul_tiles(m, n, kdim, form, sizes)
    nk = kdim // tk
    rows_outer = nk > 1 or (m // tm) * b.size * sizes[1] <= (n // tn) * a.size * sizes[0]

    def ij(fn):
        return (lambda i, j, k: fn(i, j, k)) if rows_outer else (lambda j, i, k: fn(i, j, k))

    a_spec = (pl.BlockSpec((tk, tm), ij(lambda i, j, k: (k, i))) if form == "tn"
              else pl.BlockSpec((tm, tk), ij(lambda i, j, k: (i, k))))
    b_spec = (pl.BlockSpec((tn, tk), ij(lambda i, j, k: (j, k))) if form == "nt"
              else pl.BlockSpec((tk, tn), ij(lambda i, j, k: (k, j))))
    dims = (((0 if form == "tn" else 1,), (1 if form == "nt" else 0,)), ((), ()))

    def product(a_ref, b_ref):
        return lax.dot_general(a_ref[...].astype(BF16), b_ref[...].astype(BF16), dims, preferred_element_type=F32)

    if nk == 1:
        def body(a_ref, b_ref, o_ref):
            o_ref[...] = product(a_ref, b_ref).astype(o_ref.dtype)

        scratch = []
    else:
        def body(a_ref, b_ref, o_ref, acc_ref):
            k = pl.program_id(2)

            @pl.when(k == 0)
            def _():
                acc_ref[...] = jnp.zeros_like(acc_ref)

            acc_ref[...] += product(a_ref, b_ref)

            @pl.when(k == nk - 1)
            def _():
                o_ref[...] = acc_ref[...].astype(o_ref.dtype)

        scratch = [pltpu.VMEM((tm, tn), F32)]

    return pl.pallas_call(
        body, name=name, grid=(m // tm, n // tn, nk) if rows_outer else (n // tn, m // tm, nk),
        in_specs=[a_spec, b_spec],
        out_specs=pl.BlockSpec((tm, tn), ij(lambda i, j, k: (i, j))),
        out_shape=jax.ShapeDtypeStruct((m, n), out_dtype),
        scratch_shapes=scratch,
        compiler_params=_params("parallel", "parallel", "arbitrary"),
    )(a, b)


def _pairs(bounds):
    return list(zip(bounds[:-1], bounds[1:]))


def _split(v, bounds):
    return [v[:, a:b] for a, b in _pairs(bounds)]


def _stage_build(name, f, tiled, params, consts, splits, ts, out_dtypes, ct_dtypes=None):
    n_t, n_p, n_c = len(tiled), len(params), len(consts)
    ct_dtypes = [t.dtype for t in tiled] if ct_dtypes is None else ct_dtypes
    s = tiled[0].shape[0]
    ts = min(ts, s)
    grid = (s // ts,)
    if splits is None:
        splits = [None] * n_t
    in_bounds = [(0, t.shape[1]) if b is None else tuple(b) for t, b in zip(tiled, splits)]

    def tile_aval(arr):
        return jax.ShapeDtypeStruct((ts, arr.shape[1]), arr.dtype)

    slab_avals = [[jax.ShapeDtypeStruct((ts, e - a), F32) for a, e in _pairs(b)]
                  for t, b in zip(tiled, in_bounds)]
    out_avals = jax.eval_shape(f, slab_avals, list(params), [tile_aval(c) for c in consts])
    out_bounds = [tuple(np.cumsum([0] + [o.shape[1] for o in slabs]).tolist()) for slabs in out_avals]
    out_dtypes = [F32] * len(out_bounds) if out_dtypes is None else out_dtypes
    out_shapes = [jax.ShapeDtypeStruct((s, b[-1]), d) for b, d in zip(out_bounds, out_dtypes)]

    def row_spec(width):
        return pl.BlockSpec((ts, width), lambda i: (i, 0))

    def par_spec(arr):
        return pl.BlockSpec(arr.shape, lambda i: (0, 0))

    in_specs = ([row_spec(t.shape[1]) for t in tiled] + [par_spec(p) for p in params]
                + [row_spec(c.shape[1]) for c in consts])

    def load(refs):
        t = [_split(r[...].astype(F32), b) for r, b in zip(refs[:n_t], in_bounds)]
        p = [r[...] for r in refs[n_t:n_t + n_p]]
        c = [r[...] for r in refs[n_t + n_p:n_t + n_p + n_c]]
        return t, p, c

    def store(refs, values, bounds):
        for ref, slabs, b in zip(refs, values, bounds):
            for v, (a, e) in zip(slabs, _pairs(b)):
                ref[:, a:e] = v.astype(ref.dtype)

    def run_fwd(tiled, params, consts):
        def body(*refs):
            t, p, c = load(refs)
            store(refs[n_t + n_p + n_c:], f(t, p, c), out_bounds)

        return pl.pallas_call(
            body, name=name + "_fwd", grid=grid, in_specs=in_specs,
            out_specs=[row_spec(b[-1]) for b in out_bounds], out_shape=out_shapes,
            compiler_params=_params("parallel"),
        )(*tiled, *params, *consts)

    def run_bwd(tiled, params, consts, cts):
        n_in = n_t + n_p + n_c
        n_o = len(out_bounds)

        def body(*refs):
            t, p, c = load(refs)
            g = [_split(r[...].astype(F32), b) for r, b in zip(refs[n_in:n_in + n_o], out_bounds)]
            _, pull = jax.vjp(lambda t_, p_: f(t_, p_, c), t, p)
            dt, dp = pull(g)
            store(refs[n_in + n_o:n_in + n_o + n_t], dt, in_bounds)
            first = pl.program_id(0) == 0
            for ref, d in zip(refs[n_in + n_o + n_t:], dp):
                @pl.when(first)
                def _(ref=ref):
                    ref[...] = jnp.zeros_like(ref)

                ref[...] += d

        res = pl.pallas_call(
            body, name=name + "_bwd", grid=grid,
            in_specs=in_specs + [row_spec(b[-1]) for b in out_bounds],
            out_specs=[row_spec(t.shape[1]) for t in tiled] + [par_spec(p) for p in params],
            out_shape=[jax.ShapeDtypeStruct(t.shape, d) for t, d in zip(tiled, ct_dtypes)]
                      + [jax.ShapeDtypeStruct(p.shape, F32) for p in params],
            compiler_params=_params("arbitrary"),
        )(*tiled, *params, *consts, *cts)
        return tuple(res[:n_t]), tuple(res[n_t:])

    return run_fwd, run_bwd


def proj_stage(name, f, projections, extra=(), params=(), consts=(), splits=None, ts=256, out_dtypes=None):
    n_z = len(projections)
    forms = [pr[2] for pr in projections]
    names = [pr[3] for pr in projections]
    need_da = [pr[4] for pr in projections]
    store = [pr[5] for pr in projections]
    extra, params, consts = tuple(extra), tuple(params), tuple(consts)

    def matmuls(a_list, w_list):
        return tuple(_matmul(a, w, form, out_dtype=dt, name=n + "_fwd")
                     for a, w, form, n, dt in zip(a_list, w_list, forms, names, store))

    def build(zs, ct=False):
        ct_dtypes = [BF16] * n_z + [e.dtype for e in extra] if ct else None
        return _stage_build(name, f, tuple(zs) + extra, params, consts, splits, ts, out_dtypes, ct_dtypes)

    @jax.custom_vjp
    def op(a_list, w_list, extra, params, consts):
        zs = matmuls(a_list, w_list)
        return tuple(build(zs)[0](zs + extra, params, consts))

    def op_fwd(a_list, w_list, extra, params, consts):
        zs = matmuls(a_list, w_list)
        return tuple(build(zs)[0](zs + extra, params, consts)), (a_list, w_list, zs, extra, params, consts)

    def op_bwd(res, cts):
        a_list, w_list, zs, extra, params, consts = res
        dt, dp = build(zs, ct=True)[1](zs + extra, params, consts, cts)
        da_list, dw_list = [], []
        for a, w, dz, form, n, want in zip(a_list, w_list, dt[:n_z], forms, names, need_da):
            if form == "nn":
                da = _matmul(dz, w, "nt", out_dtype=a.dtype, name=n + "_da") if want else jnp.zeros_like(a)
                dw = _matmul(a, dz, "tn", out_dtype=w.dtype, name=n + "_dw")
            else:
                da = _matmul(dz, w, "nn", out_dtype=a.dtype, name=n + "_da") if want else jnp.zeros_like(a)
                dw = _matmul(dz, a, "tn", out_dtype=w.dtype, name=n + "_dw")
            da_list.append(da)
            dw_list.append(dw)
        return tuple(da_list), tuple(dw_list), tuple(dt[n_z:]), dp, tuple(jnp.zeros_like(c) for c in consts)

    op.defvjp(op_fwd, op_bwd)
    return op(tuple(pr[0] for pr in projections), tuple(pr[1] for pr in projections), extra, params, consts)


def _rms(t, g):
    return t * lax.rsqrt(jnp.mean(t * t, axis=-1, keepdims=True) + RMS_EPS) * g


@functools.partial(jax.custom_vjp, nondiff_argnums=(1,))
def _lane_roll(t, shift):
    return pltpu.roll(t, shift % t.shape[-1], t.ndim - 1)


def _lane_roll_fwd(t, shift):
    return _lane_roll(t, shift), None


def _lane_roll_bwd(shift, _, ct):
    return (pltpu.roll(ct, (-shift) % ct.shape[-1], ct.ndim - 1),)


_lane_roll.defvjp(_lane_roll_fwd, _lane_roll_bwd)


def _rope_lanes(t, tables, half):
    reps = t.shape[1] // tables[0].shape[1]
    c, s_lo, s_hi = [jnp.concatenate([tb] * reps, axis=1) if reps > 1 else tb for tb in tables]
    return t * c + _lane_roll(t, -half) * s_lo + _lane_roll(t, half) * s_hi


PRENORM_TS = 256


def _prenorm_fwd_call(x, g, shards):
    s, width = x.shape
    ts = min(PRENORM_TS, s)
    nt = s // ts
    n_arr = len(shards)

    def body(*refs):
        x_ref, g_ref = refs[:2]
        o_ref = refs[2 + n_arr]
        i = pl.program_id(0)
        ag_start, ag_forward, ag_finish = _allgather_phases(refs[2:2 + n_arr], refs[3 + n_arr:3 + 2 * n_arr],
                                                            *refs[3 + 2 * n_arr:])

        @pl.when(i == 0)
        def _():
            ag_start()

        @pl.when(i == nt // 2)
        def _():
            ag_forward()

        o_ref[...] = _rms(x_ref[...], g_ref[...]).astype(o_ref.dtype)

        @pl.when(i == nt - 1)
        def _():
            ag_finish()

    return pl.pallas_call(
        body, name="prenorm_fwd", grid=(nt,),
        in_specs=[pl.BlockSpec((ts, width), lambda i: (i, 0)), pl.BlockSpec(g.shape, lambda i: (0, 0))]
                 + [HBM_SPEC] * n_arr,
        out_specs=[pl.BlockSpec((ts, width), lambda i: (i, 0))] + [HBM_SPEC] * n_arr,
        out_shape=[jax.ShapeDtypeStruct(x.shape, BF16)] + _allgather_out_shapes(shards),
        scratch_shapes=_allgather_sems(n_arr),
        compiler_params=_params("arbitrary"),
    )(x, g, *shards)


def _prenorm_bwd_call(x, g, dh_a, dh_b, dx_res, parts):
    s, width = x.shape
    ts = min(PRENORM_TS, s)
    nt = s // ts
    n_arr = len(parts)

    def body(*refs):
        x_ref, g_ref, dha_ref, dhb_ref, dxr_ref = refs[:5]
        dx_ref, dg_ref = refs[5 + n_arr:7 + n_arr]
        i = pl.program_id(0)
        exchange_start, exchange_finish = _exchange_chips_phases(
            refs[5:5 + n_arr], refs[7 + n_arr:7 + 2 * n_arr], *refs[7 + 2 * n_arr:])

        @pl.when(i == 0)
        def _():
            exchange_start()
            dg_ref[...] = jnp.zeros_like(dg_ref)

        _, pull = jax.vjp(_rms, x_ref[...], g_ref[...])
        dx, dg = pull(dha_ref[...].astype(F32) + dhb_ref[...].astype(F32))
        dx_ref[...] = dx + dxr_ref[...]
        dg_ref[...] += dg

        @pl.when(i == nt - 1)
        def _():
            exchange_finish()

    row = pl.BlockSpec((ts, width), lambda i: (i, 0))
    par = pl.BlockSpec(g.shape, lambda i: (0, 0))
    return pl.pallas_call(
        body, name="prenorm_bwd", grid=(nt,),
        in_specs=[row, par, row, row, row] + [HBM_SPEC] * n_arr,
        out_specs=[row, par] + [HBM_SPEC] * n_arr,
        out_shape=[jax.ShapeDtypeStruct(x.shape, F32), jax.ShapeDtypeStruct(g.shape, F32)]
                  + [jax.ShapeDtypeStruct(p.shape, p.dtype) for p in parts],
        scratch_shapes=_exchange_chips_sems(n_arr),
        compiler_params=_params("arbitrary"),
    )(x, g, dh_a, dh_b, dx_res, *parts)


@functools.partial(jax.custom_vjp, nondiff_argnums=(3,))
def prenorm_gather(x, g, shards, wire_dtypes):
    out = _prenorm_fwd_call(x, g, [s.astype(d) for s, d in zip(shards, wire_dtypes)])
    return out[0], out[0], x, tuple(out[1:])


def _prenorm_gather_fwd(x, g, shards, wire_dtypes):
    return prenorm_gather(x, g, shards, wire_dtypes), (x, g)


def _prenorm_gather_bwd(wire_dtypes, res, cts):
    x, g = res
    dh_a, dh_b, dx_res, d_gathered = cts
    out = _prenorm_bwd_call(x, g, dh_a, dh_b, dx_res, _reduce_scatter_head(d_gathered, "grads"))
    return out[0], out[1], _reduce_scatter_tail(out[2:], "grads")


prenorm_gather.defvjp(_prenorm_gather_fwd, _prenorm_gather_bwd)


def _f_prep(t, p, c):
    qa, ka, va, cq, ckv, kr = t[0]
    return [[_rope_lanes(qa, c[0:3], A_HEAD_DIM // 2)], [_rope_lanes(ka, c[0:3], A_HEAD_DIM // 2)], [va],
            [_rms(cq, p[0])], [_rms(ckv, p[1])], [_rope_lanes(kr, c[3:6], ROPE_DIM // 2)]]


def _f_qrope(t, p, c):
    return [[_rope_lanes(t[0][0], c, ROPE_DIM // 2)]]


def _f_kv(t, p, c):
    (k_nope, v), (k_pe,) = t
    return [[k_nope + jnp.concatenate([k_pe] * B_HEADS, axis=1)], [v]]


def _f_gate(t, p, c):
    (ga, gb), (pa,), (pb,) = t
    ba, bb = p
    return [[jax.nn.sigmoid(ga + ba) * pa + jax.nn.sigmoid(gb + bb) * pb]]


def _f_post(t, p, c):
    (branch,), (residual,) = t
    x1 = residual + _rms(branch, p[0])
    return [[x1], [_rms(x1, p[1])]]


def _f_out(t, p, c):
    (gate,), (emb,), (x2,) = t
    y = x2 + jax.nn.sigmoid(gate) * emb
    err = y - c[0]
    return [[0.5 * jnp.mean(err * err, axis=-1, keepdims=True)]]


def _shift_down(cur, prev, has_prev):
    full = jnp.concatenate([prev * has_prev, cur], axis=0)
    return pltpu.roll(full, 1, 0)[HALO:], pltpu.roll(full, 2, 0)[HALO:]


GELU_C = float(np.sqrt(2.0 / np.pi))
GELU_A = 0.044715
HALO = 8


def _gelu_tanh(x):
    x2 = x * x
    th = jnp.tanh(x * (GELU_C + (GELU_C * GELU_A) * x2))
    half = 0.5 + 0.5 * th
    return x * half, half + x * (0.5 - 0.5 * (th * th)) * (GELU_C + (3.0 * GELU_C * GELU_A) * x2)


def _row_sum(t):
    return jnp.sum(t, axis=0, keepdims=True)


def _conv3(cur, prev, w_ref, b_ref, has_prev):
    u1, u2 = _shift_down(cur, prev, has_prev)
    return w_ref[2:3, :] * cur + w_ref[1:2, :] * u1 + w_ref[0:1, :] * u2 + b_ref[...], u1, u2


def _mlp_act_specs(s):
    ts = min(CONV_TS, s)
    hb = ts // HALO

    def half_specs(h):
        return [pl.BlockSpec((ts, D_FF), lambda i: (i, h)),
                pl.BlockSpec((HALO, D_FF), lambda i: (jnp.maximum(i * hb - 1, 0), h))]

    def par_specs(h):
        return [pl.BlockSpec((CONV_W, D_FF), lambda i: (0, h)), pl.BlockSpec((1, D_FF), lambda i: (0, h))]

    return ts, hb, half_specs, par_specs


def _mlp_act_fwd_call(up, conv_w, conv_b):
    s = up.shape[0]
    ts, hb, half_specs, par_specs = _mlp_act_specs(s)

    def body(g_ref, gp_ref, v_ref, vp_ref, wg_ref, bg_ref, wv_ref, bv_ref, o_ref):
        has_prev = (pl.program_id(0) > 0).astype(F32)

        def chunk(cidx, carry):
            cols = pl.ds(pl.multiple_of(cidx * CONV_CHUNK, CONV_CHUNK), CONV_CHUNK)
            u_g, _, _ = _conv3(g_ref[:, cols], gp_ref[:, cols], wg_ref.at[:, cols], bg_ref.at[:, cols], has_prev)
            u_v, _, _ = _conv3(v_ref[:, cols], vp_ref[:, cols], wv_ref.at[:, cols], bv_ref.at[:, cols], has_prev)
            o_ref[:, cols] = (_gelu_tanh(u_g)[0] * u_v).astype(o_ref.dtype)
            return carry

        lax.fori_loop(0, D_FF // CONV_CHUNK, chunk, 0)

    return pl.pallas_call(
        body, name="mlp_act_fwd", grid=(s // ts,),
        in_specs=half_specs(0) + half_specs(1) + par_specs(0) + par_specs(1),
        out_specs=pl.BlockSpec((ts, D_FF), lambda i: (i, 0)),
        out_shape=jax.ShapeDtypeStruct((s, D_FF), BF16),
        compiler_params=_params("parallel"),
    )(up, up, up, up, conv_w, conv_b, conv_w, conv_b)


def _mlp_act_bwd_call(up, conv_w, conv_b, dact):
    s = up.shape[0]
    ts, hb, half_specs, par_specs = _mlp_act_specs(s)
    nt = s // ts
    ext = ts + HALO
    bf16_rows = 2 * HALO

    def next_spec(rows, h):
        return pl.BlockSpec((rows, D_FF), lambda i: (jnp.minimum((i + 1) * (ts // rows), s // rows - 1), h))

    def body(g_ref, gp_ref, gn_ref, v_ref, vp_ref, vn_ref, wg_ref, bg_ref, wv_ref, bv_ref, da_ref, dan_ref,
             dup_ref, dwg_ref, dbg_ref, dwv_ref, dbv_ref):
        i = pl.program_id(0)
        has_prev, has_next = (i > 0).astype(F32), (i < nt - 1).astype(F32)

        @pl.when(i == 0)
        def _():
            for ref in (dwg_ref, dbg_ref, dwv_ref, dbv_ref):
                ref[...] = jnp.zeros_like(ref)

        def chunk(cidx, carry):
            cols = pl.ds(pl.multiple_of(cidx * CONV_CHUNK, CONV_CHUNK), CONV_CHUNK)
            g_ext = jnp.concatenate([g_ref[:, cols], gn_ref[:, cols]], axis=0)
            v_ext = jnp.concatenate([v_ref[:, cols], vn_ref[:, cols]], axis=0)
            u_g, g1, g2 = _conv3(g_ext, gp_ref[:, cols], wg_ref.at[:, cols], bg_ref.at[:, cols], has_prev)
            u_v, v1, v2 = _conv3(v_ext, vp_ref[:, cols], wv_ref.at[:, cols], bv_ref.at[:, cols], has_prev)
            da_ext = jnp.concatenate([da_ref[:, cols].astype(F32),
                                      dan_ref[:, cols].astype(F32)[0:HALO] * has_next], axis=0)
            act_g, dact_g = _gelu_tanh(u_g)
            du_g = da_ext * u_v * dact_g
            du_v = da_ext * act_g
            for du, w_ref, x0, x1, x2, dw_ref, db_ref, lo in ((du_g, wg_ref, g_ext, g1, g2, dwg_ref, dbg_ref, 0),
                                                          (du_v, wv_ref, v_ext, v1, v2, dwv_ref, dbv_ref, D_FF)):
                d1 = pltpu.roll(du, ext - 1, 0)
                d2 = pltpu.roll(du, ext - 2, 0)
                dup = w_ref[2:3, cols] * du + w_ref[1:2, cols] * d1 + w_ref[0:1, cols] * d2
                out_cols = pl.ds(pl.multiple_of(lo + cidx * CONV_CHUNK, CONV_CHUNK), CONV_CHUNK)
                dup_ref[:, out_cols] = dup[0:ts].astype(dup_ref.dtype)
                own = du[0:ts]
                dw_ref[0:1, cols] += _row_sum(own * x2[0:ts])
                dw_ref[1:2, cols] += _row_sum(own * x1[0:ts])
                dw_ref[2:3, cols] += _row_sum(own * x0[0:ts])
                db_ref[:, cols] += _row_sum(own)
            return carry

        lax.fori_loop(0, D_FF // CONV_CHUNK, chunk, 0)

    par_out = [pl.BlockSpec((CONV_W, D_FF), lambda i: (0, 0)), pl.BlockSpec((1, D_FF), lambda i: (0, 0))]
    par_shapes = [jax.ShapeDtypeStruct((CONV_W, D_FF), F32), jax.ShapeDtypeStruct((1, D_FF), F32)]
    return pl.pallas_call(
        body, name="mlp_act_bwd", grid=(nt,),
        in_specs=(half_specs(0) + [next_spec(HALO, 0)] + half_specs(1) + [next_spec(HALO, 1)]
                  + par_specs(0) + par_specs(1)
                  + [pl.BlockSpec((ts, D_FF), lambda i: (i, 0)), next_spec(bf16_rows, 0)]),
        out_specs=[pl.BlockSpec((ts, 2 * D_FF), lambda i: (i, 0))] + par_out + par_out,
        out_shape=[jax.ShapeDtypeStruct((s, 2 * D_FF), BF16)] + par_shapes + par_shapes,
        compiler_params=_params("arbitrary"),
    )(up, up, up, up, up, up, conv_w, conv_b, conv_w, conv_b, dact, dact)


@jax.custom_vjp
def mlp_up(h2, w_up_t, conv_w, conv_b):
    return _mlp_act_fwd_call(_matmul(h2, w_up_t, "nt", out_dtype=F32, name="w_up_fwd"), conv_w, conv_b)


def _mlp_up_fwd(h2, w_up_t, conv_w, conv_b):
    up = _matmul(h2, w_up_t, "nt", out_dtype=F32, name="w_up_fwd")
    return _mlp_act_fwd_call(up, conv_w, conv_b), (h2, w_up_t, up, conv_w, conv_b)


def _mlp_up_bwd(res, dact):
    h2, w_up_t, up, conv_w, conv_b = res
    dup, dwg, dbg, dwv, dbv = _mlp_act_bwd_call(up, conv_w, conv_b, dact)
    dh2 = _matmul(dup, w_up_t, "nn", out_dtype=h2.dtype, name="w_up_da")
    dw = _matmul(dup, h2, "tn", out_dtype=w_up_t.dtype, name="w_up_dw")
    return dh2, dw, jnp.concatenate([dwg, dwv], axis=1), jnp.concatenate([dbg, dbv], axis=1)


mlp_up.defvjp(_mlp_up_fwd, _mlp_up_bwd)


SWA_ROWS = A_GROUP * SWA_BLOCK


def _swa_sink_rows(sink_ref, g):
    return jnp.concatenate([jnp.full((SWA_BLOCK, 1), sink_ref[g * A_GROUP + h], F32) for h in range(A_GROUP)], axis=0)


def _swa_operands(q_ref, kp_ref, kc_ref, vp_ref, vc_ref, sink_ref):
    groups = []
    for g in range(A_KV_HEADS):
        groups.append((_swa_stack_heads(q_ref, g), _dup_half(kp_ref[...], g), _dup_half(kc_ref[...], g),
                       _dup_half(vp_ref[...], g), _dup_half(vc_ref[...], g)))
    return groups, jnp.concatenate([_swa_sink_rows(sink_ref, g) for g in range(A_KV_HEADS)], axis=0)


def _swa_probs(groups, sink, prev_off):
    scale = A_HEAD_DIM ** -0.5
    sp = jnp.concatenate([lax.dot_general(gr[0], gr[1], NT_DIMS, preferred_element_type=F32) for gr in groups], axis=0)
    sc = jnp.concatenate([lax.dot_general(gr[0], gr[2], NT_DIMS, preferred_element_type=F32) for gr in groups], axis=0)
    qi = lax.broadcasted_iota(jnp.int32, sp.shape, 0) & (SWA_BLOCK - 1)
    kj = lax.broadcasted_iota(jnp.int32, sp.shape, 1)
    in_cur = kj <= qi
    sw = jnp.where(in_cur, sc, jnp.where(kj > qi + prev_off, sp, -jnp.inf)) * scale
    m = jnp.maximum(jnp.max(sw, axis=-1, keepdims=True), sink)
    e, es = jnp.exp(sw - m), jnp.exp(sink - m)
    den = jnp.sum(e, axis=-1, keepdims=True) + es
    return e / den, in_cur, es / den


def _swa_split(t, in_cur):
    cur = jnp.where(in_cur, t, 0.0)
    return t - cur, cur


MLA_SCALE = (NOPE_DIM + ROPE_DIM) ** -0.5
EXP2_SCALE = MLA_SCALE * float(np.log2(np.e))
NT_DIMS = (((1,), (1,)), ((), ()))
TN_DIMS = (((0,), (0,)), ((), ()))


LANES = 128
HALF = LANES // 2


def _low_half(shape):
    return lax.broadcasted_iota(jnp.int32, shape, len(shape) - 1) < HALF


def _dup_half(x, g):
    xf = x.astype(F32)
    keep = _low_half(xf.shape) if g == 0 else jnp.logical_not(_low_half(xf.shape))
    xm = jnp.where(keep, xf, 0.0)
    return (xm + pltpu.roll(xm, HALF, 1)).astype(x.dtype)


def _fold_half(r, g):
    total = r + pltpu.roll(r, HALF, 1)
    keep = _low_half(r.shape) if g == 0 else jnp.logical_not(_low_half(r.shape))
    return jnp.where(keep, total, 0.0)


def _swa_stack_heads(ref, g):
    parts = []
    for tile in range(2):
        slab = ref[:, (2 * g + tile) * LANES:(2 * g + tile + 1) * LANES]
        low = _low_half(slab.shape)
        parts += [jnp.where(low, slab, jnp.zeros_like(slab)), jnp.where(low, jnp.zeros_like(slab), slab)]
    return jnp.concatenate(parts, axis=0)


def _swa_unstack_heads(ref, g, rows):
    for tile in range(2):
        a = rows[(2 * tile) * SWA_BLOCK:(2 * tile + 1) * SWA_BLOCK]
        b = rows[(2 * tile + 1) * SWA_BLOCK:(2 * tile + 2) * SWA_BLOCK]
        ref[:, (2 * g + tile) * LANES:(2 * g + tile + 1) * LANES] = jnp.where(_low_half(a.shape), a, b).astype(ref.dtype)


def _swa_nat_specs():
    blk = SWA_BLOCK
    q_spec = pl.BlockSpec((blk, A_HEADS * A_HEAD_DIM), lambda n: (n, 0))
    prev_spec = pl.BlockSpec((blk, LANES), lambda n: (jnp.maximum(n - 1, 0), 0))
    cur_spec = pl.BlockSpec((blk, LANES), lambda n: (n, 0))
    return q_spec, prev_spec, cur_spec, pl.BlockSpec(memory_space=pltpu.SMEM)


def _swa_nat_fwd_call(q, k, v, sinks, shards):
    s = q.shape[0]
    nblk = s // SWA_BLOCK
    n_arr = len(shards)
    q_spec, prev_spec, cur_spec, sink_spec = _swa_nat_specs()

    def body(*refs):
        q_ref, kp_ref, kc_ref, vp_ref, vc_ref, sink_ref = refs[:6]
        o_ref = refs[6 + n_arr]
        n = pl.program_id(0)
        ag_start, ag_forward, ag_finish = _allgather_phases(refs[6:6 + n_arr], refs[7 + n_arr:7 + 2 * n_arr],
                                                            *refs[7 + 2 * n_arr:])

        @pl.when(n == 0)
        def _():
            ag_start()

        @pl.when(n == (3 * nblk) // 4)
        def _():
            ag_forward()

        prev_off = jnp.where(n > 0, 0, SWA_BLOCK)
        groups, sink = _swa_operands(q_ref, kp_ref, kc_ref, vp_ref, vc_ref, sink_ref)
        p, in_cur, _ = _swa_probs(groups, sink, prev_off)
        ppb, pcb = [t.astype(BF16) for t in _swa_split(p, in_cur)]
        for g, (_, _, _, vp, vc) in enumerate(groups):
            rows = slice(g * SWA_ROWS, (g + 1) * SWA_ROWS)
            out = (jnp.dot(ppb[rows], vp, preferred_element_type=F32)
                   + jnp.dot(pcb[rows], vc, preferred_element_type=F32))
            _swa_unstack_heads(o_ref, g, out)

        @pl.when(n == nblk - 1)
        def _():
            ag_finish()

    return pl.pallas_call(
        body, name="swa_fwd", grid=(nblk,),
        in_specs=[q_spec, prev_spec, cur_spec, prev_spec, cur_spec, sink_spec] + [HBM_SPEC] * n_arr,
        out_specs=[q_spec] + [HBM_SPEC] * n_arr,
        out_shape=[jax.ShapeDtypeStruct(q.shape, BF16)] + _allgather_out_shapes(shards),
        scratch_shapes=_allgather_sems(n_arr),
        compiler_params=_params("arbitrary"),
    )(q, k, k, v, v, sinks, *shards)


def _swa_nat_bwd_call(q, k, v, sinks, do, parts):
    s = q.shape[0]
    nblk = s // SWA_BLOCK
    n_arr = len(parts)
    q_spec, prev_spec, cur_spec, sink_spec = _swa_nat_specs()
    scale = A_HEAD_DIM ** -0.5
    dsink_spec = pl.BlockSpec((A_KV_HEADS, SWA_ROWS, 1), lambda n: (0, 0, 0))

    def body(*refs):
        q_ref, kp_ref, kc_ref, vp_ref, vc_ref, sink_ref, do_ref = refs[:7]
        dq_ref, dkp_ref, dkc_ref, dvp_ref, dvc_ref, dsink_ref = refs[7 + n_arr:13 + n_arr]
        n = pl.program_id(0)
        exchange_start, exchange_finish = _exchange_chips_phases(
            refs[7:7 + n_arr], refs[13 + n_arr:13 + 2 * n_arr], *refs[13 + 2 * n_arr:])

        @pl.when(n == 0)
        def _():
            exchange_start()
        prev_off = jnp.where(n > 0, 0, SWA_BLOCK)

        @pl.when(n == 0)
        def _():
            dsink_ref[...] = jnp.zeros_like(dsink_ref)

        groups, sink = _swa_operands(q_ref, kp_ref, kc_ref, vp_ref, vc_ref, sink_ref)
        dobs = [_swa_stack_heads(do_ref, g) for g in range(A_KV_HEADS)]
        p, in_cur, ps = _swa_probs(groups, sink, prev_off)
        ppb, pcb = [t.astype(BF16) for t in _swa_split(p, in_cur)]

        def per_group(fn):
            return jnp.concatenate([fn(g, slice(g * SWA_ROWS, (g + 1) * SWA_ROWS)) for g in range(A_KV_HEADS)], axis=0)

        out = per_group(lambda g, rows: jnp.dot(ppb[rows], groups[g][3], preferred_element_type=F32)
                        + jnp.dot(pcb[rows], groups[g][4], preferred_element_type=F32))
        delta = jnp.sum(jnp.concatenate(dobs, axis=0).astype(F32) * out, axis=-1, keepdims=True)
        dp = jnp.where(in_cur,
                       per_group(lambda g, rows: lax.dot_general(dobs[g], groups[g][4], NT_DIMS,
                                                                 preferred_element_type=F32)),
                       per_group(lambda g, rows: lax.dot_general(dobs[g], groups[g][3], NT_DIMS,
                                                                 preferred_element_type=F32)))
        dsp, dsc = [t.astype(BF16) for t in _swa_split(p * (dp - delta), in_cur)]
        dsink_ref[...] += (-ps * delta).reshape(dsink_ref.shape)
        totals = [jnp.zeros((SWA_BLOCK, LANES), F32) for _ in range(4)]
        for g, (qb, kp, kc, _, _) in enumerate(groups):
            rows = slice(g * SWA_ROWS, (g + 1) * SWA_ROWS)
            dq = (jnp.dot(dsp[rows], kp, preferred_element_type=F32)
                  + jnp.dot(dsc[rows], kc, preferred_element_type=F32)) * scale
            _swa_unstack_heads(dq_ref, g, dq)
            pieces = [lax.dot_general(dsp[rows], qb, TN_DIMS, preferred_element_type=F32) * scale,
                      lax.dot_general(dsc[rows], qb, TN_DIMS, preferred_element_type=F32) * scale,
                      lax.dot_general(ppb[rows], dobs[g], TN_DIMS, preferred_element_type=F32),
                      lax.dot_general(pcb[rows], dobs[g], TN_DIMS, preferred_element_type=F32)]
            totals = [tot + _fold_half(r, g) for tot, r in zip(totals, pieces)]
        dkp_ref[...], dkc_ref[...], dvp_ref[...], dvc_ref[...] = totals

        @pl.when(n == nblk - 1)
        def _():
            exchange_finish()

    kv_shape = jax.ShapeDtypeStruct(k.shape, F32)
    return pl.pallas_call(
        body, name="swa_bwd", grid=(nblk,),
        in_specs=[q_spec, prev_spec, cur_spec, prev_spec, cur_spec, sink_spec, q_spec] + [HBM_SPEC] * n_arr,
        out_specs=[q_spec, cur_spec, cur_spec, cur_spec, cur_spec, dsink_spec] + [HBM_SPEC] * n_arr,
        out_shape=[jax.ShapeDtypeStruct(q.shape, q.dtype), kv_shape, kv_shape, kv_shape, kv_shape,
                   jax.ShapeDtypeStruct((A_KV_HEADS, SWA_ROWS, 1), F32)]
                  + [jax.ShapeDtypeStruct(p.shape, p.dtype) for p in parts],
        scratch_shapes=_exchange_chips_sems(n_arr),
        compiler_params=_params("arbitrary"),
    )(q, k, k, v, v, sinks, do, *parts)


@jax.custom_vjp
def swa_nat(q, k, v, sinks, shards):
    out = _swa_nat_fwd_call(q, k, v, sinks, [s.astype(BF16) for s in shards])
    return out[0], tuple(out[1:])


def _swa_nat_fwd(q, k, v, sinks, shards):
    out = _swa_nat_fwd_call(q, k, v, sinks, [s.astype(BF16) for s in shards])
    return (out[0], tuple(out[1:])), (q, k, v, sinks)


def _swa_nat_bwd(res, cts):
    q, k, v, sinks = res
    do, d_gathered = cts
    out = _swa_nat_bwd_call(q, k, v, sinks, do, _reduce_scatter_head(d_gathered, "mid_grads"))
    dq, dkp, dkc, dvp, dvc, dsink = out[:6]

    def fold(prev_part, cur_part):
        shifted = jnp.concatenate([prev_part[SWA_BLOCK:], jnp.zeros_like(prev_part[:SWA_BLOCK])], axis=0)
        return (cur_part + shifted).astype(k.dtype)

    dsinks = jnp.sum(dsink.reshape(A_HEADS, SWA_BLOCK), axis=1)
    return dq, fold(dkp, dkc), fold(dvp, dvc), dsinks, _reduce_scatter_tail(out[6:], "mid_grads")


swa_nat.defvjp(_swa_nat_fwd, _swa_nat_bwd)

N_PAIR = B_HEADS // 2


def _flash_nat_fwd_call(q, k, v, shards):
    s = q.shape[0]
    t = min(FLASH_T, s)
    nb = s // t
    d = LANES
    n_arr = len(shards)

    def body(*refs):
        q_ref, k_ref, v_ref = refs[:3]
        shard_refs = refs[3:3 + n_arr]
        o_ref, lse_ref = refs[3 + n_arr:5 + n_arr]
        gathered_refs = refs[5 + n_arr:5 + 2 * n_arr]
        vt_ref, m_ref, l_ref, acc_ref = refs[5 + 2 * n_arr:9 + 2 * n_arr]
        pair, i = pl.program_id(0), pl.program_id(1)
        ag_start, ag_forward, ag_finish = _allgather_phases(shard_refs, gathered_refs, *refs[9 + 2 * n_arr:])

        @pl.when((pair == 0) & (i == 0))
        def _():
            ag_start()

        @pl.when((pair == N_PAIR - 1) & (i == nb // 2))
        def _():
            ag_forward()

        @pl.when(i == 0)
        def _():
            for hh in range(2):
                for chunk in range(nb):
                    rows = slice(chunk * t, (chunk + 1) * t)
                    vt_ref[hh, :, rows] = v_ref[rows, hh * d:(hh + 1) * d].T

        m_ref[...] = jnp.full_like(m_ref, -jnp.inf)
        l_ref[...] = jnp.zeros_like(l_ref)
        acc_ref[...] = jnp.zeros_like(acc_ref)

        def step(j, on_diagonal):
            keys = pl.ds(pl.multiple_of(j * t, t), t)
            scores = [lax.dot_general(k_ref[keys, hh * d:(hh + 1) * d], q_ref[:, hh * d:(hh + 1) * d], NT_DIMS,
                                      preferred_element_type=F32) for hh in range(2)]
            for hh in range(2):
                sc_t = scores[hh]
                if on_diagonal:
                    key = lax.broadcasted_iota(jnp.int32, (t, t), 0)
                    qry = lax.broadcasted_iota(jnp.int32, (t, t), 1)
                    sc_t = jnp.where(qry >= key, sc_t, -jnp.inf)
                m_old = m_ref[hh]
                m_new = jnp.maximum(m_old, jnp.max(sc_t, axis=0, keepdims=True))
                alpha = jnp.exp2((m_old - m_new) * EXP2_SCALE)
                p_t = jnp.exp2((sc_t - m_new) * EXP2_SCALE)
                l_ref[hh] = alpha * l_ref[hh] + jnp.sum(p_t, axis=0, keepdims=True)
                acc_ref[hh] = alpha * acc_ref[hh] + jnp.dot(vt_ref[hh, :, keys], p_t.astype(BF16),
                                                            preferred_element_type=F32)
                m_ref[hh] = m_new

        def below(j, carry):
            step(j, False)
            return carry

        lax.fori_loop(0, i, below, 0)
        step(i, True)
        outs =[(acc_ref[hh] / l_ref[hh]).T for hh in range(2)]
        for hh in range(2):
            lse_ref[hh] = m_ref[hh] * EXP2_SCALE + jnp.log2(l_ref[hh])
        o_ref[...] = (outs[0] + pltpu.roll(outs[1], HALF, 1)).astype(o_ref.dtype)

        @pl.when((pair == N_PAIR - 1) & (i == nb - 1))
        def _():
            ag_finish()

    return pl.pallas_call(
        body, name="mla_fwd", grid=(N_PAIR, nb),
        in_specs=[pl.BlockSpec((t, 2 * d), lambda p, i: (i, p)),
                  pl.BlockSpec((s, 2 * d), lambda p, i: (0, p)),
                  pl.BlockSpec((s, 2 * d), lambda p, i: (0, p))] + [HBM_SPEC] * n_arr,
        out_specs=[pl.BlockSpec((t, d), lambda p, i: (i, p)),
                   pl.BlockSpec((2, 1, t), lambda p, i: (p, 0, i))] + [HBM_SPEC] * n_arr,
        out_shape=[jax.ShapeDtypeStruct((s, N_PAIR * d), BF16), jax.ShapeDtypeStruct((B_HEADS, 1, s), F32)]
                  + _allgather_out_shapes(shards),
        scratch_shapes=[pltpu.VMEM((2, d, s), BF16), pltpu.VMEM((2, 1, t), F32), pltpu.VMEM((2, 1, t), F32),
                        pltpu.VMEM((2, d, t), F32)] + _allgather_sems(n_arr),
        compiler_params=_params("arbitrary", "arbitrary"),
    )(q, k, v, *shards)


def _flash_nat_delta_call(o, do):
    s, w = o.shape
    t = min(FLASH_T, s)

    def body(o_ref, do_ref, out_ref):
        prod = o_ref[...].astype(F32) * do_ref[...].astype(F32)
        lane = lax.broadcasted_iota(jnp.int32, (w, LANES), 0) // V_DIM
        head = lax.broadcasted_iota(jnp.int32, (w, LANES), 1)
        out_ref[...] = jnp.dot(prod, (lane == head).astype(F32), precision=lax.Precision.HIGHEST,
                               preferred_element_type=F32)

    spec = pl.BlockSpec((t, w), lambda i: (i, 0))
    return pl.pallas_call(
        body, name="mla_delta", grid=(s // t,), in_specs=[spec, spec],
        out_specs=pl.BlockSpec((t, LANES), lambda i: (i, 0)),
        out_shape=jax.ShapeDtypeStruct((s, LANES), F32), compiler_params=_params("parallel"),
    )(o, do)


def _flash_nat_bwd_call(q, k, v, lse_row, delta_row, do, parts):
    s = q.shape[0]
    t = min(FLASH_T, s)
    nb = s // t
    d = LANES
    n_arr = len(parts)

    def body(*refs):
        q_ref, k_ref, v_ref, lse_ref, delta_ref, do_ref = refs[:6]
        part_refs = refs[6:6 + n_arr]
        dq_ref, dk_ref, dv_ref = refs[6 + n_arr:9 + n_arr]
        received_refs = refs[9 + n_arr:9 + 2 * n_arr]
        dq_acc, dk_acc, dv_acc = refs[9 + 2 * n_arr:12 + 2 * n_arr]
        pair, j = pl.program_id(0), pl.program_id(1)
        exchange_start, exchange_finish = _exchange_all_phases(part_refs, received_refs, *refs[12 + 2 * n_arr:])

        @pl.when((pair == 0) & (j == 0))
        def _():
            exchange_start()

        @pl.when(j == 0)
        def _():
            dq_acc[...] = jnp.zeros_like(dq_acc)

        for hh in range(2):
            kb, vb = k_ref[:, hh * d:(hh + 1) * d], v_ref[:, hh * d:(hh + 1) * d]
            dk_acc[...] = jnp.zeros_like(dk_acc)
            dv_acc[...] = jnp.zeros_like(dv_acc)

            def step(i, on_diagonal, hh=hh, kb=kb, vb=vb):
                rows = pl.ds(pl.multiple_of(i * t, t), t)
                qb = q_ref[rows, hh * d:(hh + 1) * d]
                do_pair = do_ref[rows, :].astype(F32)
                do_h = do_pair if hh == 0 else pltpu.roll(do_pair, HALF, 1)
                dob = jnp.where(_low_half(do_h.shape), do_h, 0.0).astype(BF16)
                sc_t = lax.dot_general(kb, qb, NT_DIMS, preferred_element_type=F32)
                p_t = jnp.exp2(sc_t * EXP2_SCALE - lse_ref[hh, :, rows])
                if on_diagonal:
                    key = lax.broadcasted_iota(jnp.int32, (t, t), 0)
                    qry = lax.broadcasted_iota(jnp.int32, (t, t), 1)
                    p_t = jnp.where(qry >= key, p_t, 0.0)
                dp_t = lax.dot_general(vb, dob, NT_DIMS, preferred_element_type=F32)
                ds_t = (p_t * (dp_t - delta_ref[hh, :, rows])).astype(BF16)
                dv_acc[...] += jnp.dot(p_t.astype(BF16), dob, preferred_element_type=F32)
                dk_acc[...] += jnp.dot(ds_t, qb, preferred_element_type=F32)
                dq_acc[hh, rows, :] += lax.dot_general(ds_t, kb, TN_DIMS, preferred_element_type=F32)

            def above(i, carry, step=step):
                step(i, False)
                return carry

            step(j, True)
            lax.fori_loop(j + 1, nb, above, 0)
            dk_ref[:, hh * d:(hh + 1) * d] = (dk_acc[...] * MLA_SCALE).astype(dk_ref.dtype)
            dv_ref[:, hh * d:(hh + 1) * d] = dv_acc[...].astype(dv_ref.dtype)

        @pl.when(j == nb - 1)
        def _():
            for hh in range(2):
                dq_ref[:, hh * d:(hh + 1) * d] = (dq_acc[hh] * MLA_SCALE).astype(dq_ref.dtype)

        @pl.when((pair == N_PAIR - 1) & (j == nb - 1))
        def _():
            exchange_finish()

    full_spec = pl.BlockSpec((s, 2 * d), lambda p, j: (0, p))
    tile_spec = pl.BlockSpec((t, 2 * d), lambda p, j: (j, p))
    row_spec = pl.BlockSpec((2, 1, s), lambda p, j: (p, 0, 0))
    return pl.pallas_call(
        body, name="mla_bwd", grid=(N_PAIR, nb),
        in_specs=[full_spec, tile_spec, tile_spec, row_spec, row_spec, pl.BlockSpec((s, d), lambda p, j: (0, p))]
                 + [HBM_SPEC] * n_arr,
        out_specs=[full_spec, tile_spec, tile_spec] + [HBM_SPEC] * n_arr,
        out_shape=[jax.ShapeDtypeStruct(q.shape, q.dtype)] * 3 + [jax.ShapeDtypeStruct(p.shape, p.dtype) for p in parts],
        scratch_shapes=[pltpu.VMEM((2, s, d), F32), pltpu.VMEM((t, d), F32), pltpu.VMEM((t, d), F32)]
                       + _exchange_all_sems(n_arr),
        compiler_params=_params("arbitrary", "arbitrary"),
    )(q, k, v, lse_row, delta_row, do, *parts)


def _reduce_scatter_head(cts, tag):
    received = _exchange_sibling(list(cts), tag + "_exchange_sibling")
    my_c = lax.axis_index("c").astype(jnp.int32).reshape(1)
    return [_pair_add(m, r, my_c, "%s_pair_add_%d" % (tag, i)) for i, (m, r) in enumerate(zip(cts, received))]


def _reduce_scatter_tail(chip_parts, tag):
    return tuple(_sum_blocks(r, "%s_sum_%d" % (tag, i)) for i, r in enumerate(chip_parts))


@jax.custom_vjp
def flash_nat(q, k, v, shards):
    out = _flash_nat_fwd_call(q, k, v, [s.astype(BF16) for s in shards])
    return out[0], tuple(out[2:])


def _flash_nat_fwd(q, k, v, shards):
    out = _flash_nat_fwd_call(q, k, v, [s.astype(BF16) for s in shards])
    return (out[0], tuple(out[2:])), (q, k, v, out[0], out[1])


def _flash_nat_bwd(res, cts):
    q, k, v, o, lse = res
    do, d_gathered = cts
    delta = _flash_nat_delta_call(o, do)[:, :B_HEADS].T.reshape(B_HEADS, 1, q.shape[0])
    out = _flash_nat_bwd_call(q, k, v, lse, delta, do, list(d_gathered))
    return out[0], out[1], out[2], _reduce_scatter_tail(out[3:], "mlp_grads")


flash_nat.defvjp(_flash_nat_fwd, _flash_nat_bwd)


HBM_SPEC = pl.BlockSpec(memory_space=pltpu.HBM)


def _allgather(shards, name):
    n_arr = len(shards)

    def body(*refs):
        start, forward, finish = _allgather_phases(refs[:n_arr], refs[n_arr:2 * n_arr], *refs[2 * n_arr:])
        start()
        forward()
        finish()

    return pl.pallas_call(
        body, name=name, out_shape=_allgather_out_shapes(shards),
        in_specs=[HBM_SPEC] * n_arr, out_specs=[HBM_SPEC] * n_arr,
        scratch_shapes=_allgather_sems(n_arr),
    )(*shards)


def _allgather_out_shapes(shards):
    return [jax.ShapeDtypeStruct((N_DEV,) + s.shape, s.dtype) for s in shards]


def _allgather_sems(n_arr):
    return [pltpu.SemaphoreType.DMA((7, n_arr)), pltpu.SemaphoreType.DMA((7, n_arr)), pltpu.SemaphoreType.DMA((n_arr,))]


def _allgather_phases(x_refs, out_refs, send_sems, recv_sems, local_sems):
    arrays = range(len(x_refs))
    x, y, c = lax.axis_index("x"), lax.axis_index("y"), lax.axis_index("c")
    me, sibling = (x, y, c), (x, y, 1 - c)
    chips = [(1 - x, y), (x, 1 - y), (1 - x, 1 - y)]

    def rows(a, px, py, pc):
        return out_refs[a].at[4 * px + 2 * py + pc]

    def copy(a, k, block, to, src=None):
        return pltpu.make_async_remote_copy(
            src_ref=rows(a, *block) if src is None else src, dst_ref=rows(a, *block),
            send_sem=send_sems.at[k, a], recv_sem=recv_sems.at[k, a], device_id=to, device_id_type=MESH_ID)

    def mine():
        return [pltpu.make_async_copy(x_refs[a], rows(a, *me), local_sems.at[a]) for a in arrays]

    def first():
        return [cp for a in arrays for cp in
                [copy(a, 0, me, sibling, src=x_refs[a])]
                + [copy(a, 1 + j, me, (*chip, c), src=x_refs[a]) for j, chip in enumerate(chips)]]

    def passed():
        return [copy(a, 4 + j, (*chip, c), sibling) for j, chip in enumerate(chips) for a in arrays]

    def start():
        for cp in mine() + first():
            cp.start()

    def forward():
        for j, chip in enumerate(chips):
            for a in arrays:
                copy(a, 1 + j, (*chip, c), me).wait_recv()
                copy(a, 4 + j, (*chip, c), sibling).start()

    def finish():
        for a in arrays:
            copy(a, 0, sibling, me).wait_recv()
        for j, chip in enumerate(chips):
            for a in arrays:
                copy(a, 4 + j, (*chip, 1 - c), me).wait_recv()
        for cp in first() + passed():
            cp.wait_send()
        for cp in mine():
            cp.wait()

    return start, forward, finish


N_CHIP = 4


def _exchange_sibling(parts, name):
    n_arr = len(parts)

    def body(*refs):
        in_refs, recv_refs = refs[:n_arr], refs[n_arr:2 * n_arr]
        send_sems, recv_sems = refs[2 * n_arr:]
        x, y, c = lax.axis_index("x"), lax.axis_index("y"), lax.axis_index("c")
        copies = []
        for a in range(n_arr):
            for q in range(N_CHIP):
                copies.append(pltpu.make_async_remote_copy(
                    src_ref=in_refs[a].at[2 * q + 1 - c], dst_ref=recv_refs[a].at[q],
                    send_sem=send_sems.at[q, a], recv_sem=recv_sems.at[q, a],
                    device_id=(x, y, 1 - c), device_id_type=MESH_ID))
        for cp in copies:
            cp.start()
        for cp in copies:
            cp.wait()

    return pl.pallas_call(
        body, name=name, out_shape=[jax.ShapeDtypeStruct((N_CHIP,) + p.shape[1:], p.dtype) for p in parts],
        in_specs=[HBM_SPEC] * n_arr, out_specs=[HBM_SPEC] * n_arr,
        scratch_shapes=[pltpu.SemaphoreType.DMA((N_CHIP, n_arr)), pltpu.SemaphoreType.DMA((N_CHIP, n_arr))],
    )(*parts)


def _exchange_all_sems(n_arr):
    return [pltpu.SemaphoreType.DMA((N_DEV - 1, n_arr)), pltpu.SemaphoreType.DMA((N_DEV - 1, n_arr)),
            pltpu.SemaphoreType.DMA((n_arr,))]


def _exchange_all_phases(in_refs, out_refs, send_sems, recv_sems, local_sems):
    n_arr = len(in_refs)
    x, y, c = lax.axis_index("x"), lax.axis_index("y"), lax.axis_index("c")
    me = 4 * x + 2 * y + c

    def copies():
        out = [pltpu.make_async_copy(in_refs[a].at[me], out_refs[a].at[me], local_sems.at[a]) for a in range(n_arr)]
        for k in range(1, N_DEV):
            px = 1 - x if k & 4 else x
            py = 1 - y if k & 2 else y
            pc = 1 - c if k & 1 else c
            for a in range(n_arr):
                out.append(pltpu.make_async_remote_copy(
                    src_ref=in_refs[a].at[4 * px + 2 * py + pc], dst_ref=out_refs[a].at[me],
                    send_sem=send_sems.at[k - 1, a], recv_sem=recv_sems.at[k - 1, a],
                    device_id=(px, py, pc), device_id_type=MESH_ID))
        return out

    def start():
        for cp in copies():
            cp.start()

    def finish():
        for cp in copies():
            cp.wait()

    return start, finish


def _exchange_chips_sems(n_arr):
    return [pltpu.SemaphoreType.DMA((N_CHIP - 1, n_arr)), pltpu.SemaphoreType.DMA((N_CHIP - 1, n_arr)),
            pltpu.SemaphoreType.DMA((n_arr,))]


def _exchange_chips_phases(in_refs, out_refs, send_sems, recv_sems, local_sems):
    n_arr = len(in_refs)
    x, y, c = lax.axis_index("x"), lax.axis_index("y"), lax.axis_index("c")
    me = 2 * x + y

    def copies():
        out = [pltpu.make_async_copy(in_refs[a].at[me], out_refs[a].at[me], local_sems.at[a]) for a in range(n_arr)]
        for k in range(1, N_CHIP):
            px = 1 - x if k & 2 else x
            py = 1 - y if k & 1 else y
            for a in range(n_arr):
                out.append(pltpu.make_async_remote_copy(
                    src_ref=in_refs[a].at[2 * px + py], dst_ref=out_refs[a].at[me],
                    send_sem=send_sems.at[k - 1, a], recv_sem=recv_sems.at[k - 1, a],
                    device_id=(px, py, c), device_id_type=MESH_ID))
        return out

    def start():
        for cp in copies():
            cp.start()

    def finish():
        for cp in copies():
            cp.wait()

    return start, finish


def _row_tile(r, ccols, blocks):
    cap = max(16, (2 * 1024 * 1024) // (4 * ccols * blocks))
    return _pick(r, cap, 16)


def _pair_add(mine, theirs, my_c, name):
    _, r, ccols = mine.shape
    tr = _row_tile(r, ccols, 1)

    def body(c_ref, a_ref, b_ref, o_ref):
        o_ref[...] = (a_ref[...].astype(F32) + b_ref[...].astype(F32)).astype(o_ref.dtype)

    spec = pl.BlockSpec((None, tr, ccols), lambda q, i, c_ref: (q, i, 0))
    return pl.pallas_call(
        body, name=name,
        grid_spec=pltpu.PrefetchScalarGridSpec(
            num_scalar_prefetch=1, grid=(N_CHIP, r // tr),
            in_specs=[pl.BlockSpec((None, tr, ccols), lambda q, i, c_ref: (2 * q + c_ref[0], i, 0)), spec],
            out_specs=spec),
        out_shape=jax.ShapeDtypeStruct(theirs.shape, theirs.dtype),
        compiler_params=_params("parallel", "parallel"),
    )(my_c, mine, theirs)


def _sum_blocks(parts, name):
    nb, r, ccols = parts.shape
    tr = _row_tile(r, ccols, nb)

    def body(p_ref, o_ref):
        acc = p_ref[0].astype(F32)
        for i in range(1, nb):
            acc = acc + p_ref[i].astype(F32)
        o_ref[...] = acc

    return pl.pallas_call(
        body, name=name, grid=(r // tr,),
        in_specs=[pl.BlockSpec((nb, tr, ccols), lambda i: (0, i, 0))],
        out_specs=pl.BlockSpec((tr, ccols), lambda i: (i, 0)),
        out_shape=jax.ShapeDtypeStruct((r, ccols), F32),
        compiler_params=_params("parallel"),
    )(parts)


@jax.custom_vjp
def replicated(vec):
    return vec


def _replicated_fwd(vec):
    return vec, None


def _replicated_bwd(_, ct):
    return (_sum_blocks(_allgather([ct], "small_grad_allgather")[0], "small_grad_sum"),)


replicated.defvjp(_replicated_fwd, _replicated_bwd)


def _adamw(w, g, m, v, name):
    rows, cols = w.shape
    tr = _pick(rows, 256, 8) if rows % 8 == 0 else rows

    def body(w_ref, g_ref, m_ref, v_ref, d_ref, nm_ref, nv_ref):
        g_ = g_ref[...]
        m_ = ADAM_B1 * m_ref[...] + (1.0 - ADAM_B1) * g_
        v_ = ADAM_B2 * v_ref[...] + (1.0 - ADAM_B2) * jnp.square(g_)
        m_hat = m_ / (1.0 - ADAM_B1 ** ADAM_STEP)
        v_hat = v_ / (1.0 - ADAM_B2 ** ADAM_STEP)
        d_ref[...] = -ADAM_LR * (m_hat / (jnp.sqrt(v_hat) + ADAM_EPS) + ADAM_WD * w_ref[...])
        nm_ref[...] = m_
        nv_ref[...] = v_

    spec = pl.BlockSpec((tr, cols), lambda i: (i, 0))
    return pl.pallas_call(
        body, name=name, grid=(rows // tr,), in_specs=[spec] * 4, out_specs=[spec] * 3,
        out_shape=[jax.ShapeDtypeStruct(w.shape, F32)] * 3, compiler_params=_params("parallel"),
    )(w, g, m, v)


COL_SHARDED = ("w_in", "w_uq", "w_ukv", "w_branch_a", "w_branch_b", "w_up", "w_ple")
EARLY = ("w_in",)
MID = ("w_uq", "w_ukv", "w_branch_a", "w_branch_b", "w_out")
LATE = ("w_up", "w_down", "w_ple_gate", "w_ple")
SMALL = ("attn_pre_norm", "attn_post_norm", "b_gate", "q_a_norm", "kv_a_norm", "mlp_pre_norm", "mlp_post_norm",
         "conv_b", "ple_norm", "sinks")
SMALL_COLS = 128


def _pack_rows(arrays, cols, row_mult):
    flat = jnp.concatenate([a.reshape(-1) for a in arrays])
    pad = (-flat.shape[0]) % (cols * row_mult)
    return jnp.pad(flat, (0, pad)).reshape(-1, cols)


def _unpack_small(vec, shapes):
    flat = vec.reshape(-1)
    out, off = {}, 0
    for name in SMALL:
        n = shapes[name]
        out[name] = flat[off:off + n].reshape(1, n)
        off += n + (-n) % SMALL_COLS
    return out


def _pad_lanes(t, width):
    return jnp.pad(t, [(0, 0)] * (t.ndim - 1) + [(0, width - t.shape[-1])])


def _pad_rows(t, rows):
    return jnp.pad(t, [(0, 0)] * (t.ndim - 2) + [(0, rows - t.shape[-2]), (0, 0)])


FRONT_SIZES = (512, 128, 128, 256, 128)
FRONT_BOUNDS = (0, 512, 640, 768, 1024, 1152, 1280)
PE_LANE = NOPE_DIM


def _arrange_w_in_t(wt):
    k = wt.shape[1]
    n_front = sum(FRONT_SIZES)
    front, kr, gates = wt[:n_front], wt[n_front:n_front + ROPE_DIM], wt[n_front + ROPE_DIM:]
    kr_slab = jnp.concatenate([jnp.zeros((PE_LANE, k), wt.dtype), kr,
                               jnp.zeros((HEAD_PAD - PE_LANE - ROPE_DIM, k), wt.dtype)], axis=0)
    return jnp.concatenate([front, kr_slab], axis=0), gates


def _arrange_w_uq_t(wt):
    k = wt.shape[1]
    return _pad_rows(wt.reshape(B_HEADS, NOPE_DIM + ROPE_DIM, k), HEAD_PAD).reshape(B_HEADS * HEAD_PAD, k)


def _arrange_w_ukv_t(wt):
    k = wt.shape[1]
    w = wt.reshape(B_HEADS, 2, NOPE_DIM, k)
    slabs = [_pad_rows(w[:, part], HEAD_PAD).reshape(B_HEADS * HEAD_PAD, k) for part in range(2)]
    return jnp.concatenate(slabs, axis=0)


def _rope_tables(positions, s):
    pos = positions.reshape(s, 1).astype(F32)

    def angles(dim):
        return pos * ROPE_THETA ** (-(jnp.arange(0, dim, 2, dtype=F32) / dim))

    cos_a, sin_a = jnp.cos(angles(A_HEAD_DIM)), jnp.sin(angles(A_HEAD_DIM))
    zero_a = jnp.zeros_like(sin_a)
    tables_a = [jnp.tile(jnp.concatenate(pair, axis=1), (1, LANES // A_HEAD_DIM))
                for pair in ((cos_a, cos_a), (-sin_a, zero_a), (zero_a, sin_a))]
    cos_b, sin_b = jnp.cos(angles(ROPE_DIM)), jnp.sin(angles(ROPE_DIM))
    zero_b = jnp.zeros_like(sin_b)

    def slab(first, second, fill):
        return jnp.concatenate([jnp.full((s, PE_LANE), fill, F32), first, second,
                                jnp.full((s, HEAD_PAD - PE_LANE - ROPE_DIM), fill, F32)], axis=1)

    tables_b = [slab(cos_b, cos_b, 1.0), slab(-sin_b, zero_b, 0.0), slab(zero_b, sin_b, 0.0)]
    return tables_a + tables_b


def _local_loss(wts, x, p, tables, target):
    s = x.shape[0]
    small_shapes = {n: wts[n].shape[-1] for n in SMALL}
    small_vec = _pack_rows([_pad_lanes(wts[n].reshape(1, -1), small_shapes[n] + (-small_shapes[n]) % SMALL_COLS)
                            for n in SMALL], SMALL_COLS, 8)
    sm = _unpack_small(replicated(small_vec), small_shapes)
    def shard(n):
        return wts[n].T if n in COL_SHARDED else wts[n]

    h1_front, h1_gates, x_res, gathered = prenorm_gather(
        x, sm["attn_pre_norm"], tuple([shard(n) for n in EARLY] + [_pack_rows([wts["conv_w"]], SMALL_COLS, 8)]),
        (BF16,) * len(EARLY) + (F32,))
    big = {n: g.reshape(-1, g.shape[2]) for n, g in zip(EARLY, gathered)}
    ch = wts["conv_w"].shape[1]
    conv_w = gathered[-1].reshape(N_DEV, -1)[:, :CONV_W * ch].reshape(N_DEV, CONV_W, ch)
    conv_w = conv_w.transpose(1, 0, 2).reshape(CONV_W, N_DEV * ch)

    w_front_t, w_gates_t = _arrange_w_in_t(big["w_in"])
    tables_a, tables_b = tables[:3], tables[3:]

    qa, ka, va, cqn, ckvn, kpe = proj_stage(
        "prep", _f_prep, [(h1_front, w_front_t, "nt", "w_front", True, F32)], params=[sm["q_a_norm"], sm["kv_a_norm"]],
        consts=tables, splits=[FRONT_BOUNDS], ts=512, out_dtypes=[BF16, BF16, BF16, BF16, BF16, F32])
    ya, mid = swa_nat(qa, ka, va, sm["sinks"].reshape(-1), tuple(shard(n) for n in MID))
    big.update({n: g.reshape(-1, g.shape[2]) for n, g in zip(MID, mid)})

    (q2,) = proj_stage("qrope", _f_qrope, [(cqn, _arrange_w_uq_t(big["w_uq"]), "nt", "w_uq", True, BF16)],
                       consts=tables_b, ts=512, out_dtypes=[BF16])
    k2, v2 = proj_stage("kv", _f_kv, [(ckvn, _arrange_w_ukv_t(big["w_ukv"]), "nt", "w_ukv", True, BF16)],
                        extra=[kpe], splits=[(0, B_HEADS * HEAD_PAD, 2 * B_HEADS * HEAD_PAD), None], ts=512,
                        out_dtypes=[BF16, BF16])
    yb, late = flash_nat(q2, k2, v2, tuple(shard(n) for n in LATE))
    big.update({n: g.reshape(-1, g.shape[2]) for n, g in zip(LATE, late)})

    (mixed,) = proj_stage(
        "gate", _f_gate, [(h1_gates, w_gates_t, "nt", "w_gates", True, F32),
                          (ya, big["w_branch_a"], "nt", "w_branch_a", True, BF16),
                          (yb, big["w_branch_b"], "nt", "w_branch_b", True, BF16)],
        params=[sm["b_gate"][:, :D_MODEL], sm["b_gate"][:, D_MODEL:]],
        splits=[(0, D_MODEL, 2 * D_MODEL), None, None], out_dtypes=[BF16])
    x1, h2 = proj_stage("post_attn", _f_post, [(mixed, big["w_out"], "nn", "w_out", True, F32)], extra=[x_res],
                        params=[sm["attn_post_norm"], sm["mlp_pre_norm"]], ts=512, out_dtypes=[F32, BF16])

    act = mlp_up(h2, big["w_up"], conv_w, sm["conv_b"])
    x2, h3 = proj_stage("post_mlp", _f_post, [(act, big["w_down"], "nn", "w_down", True, F32)], extra=[x1],
                        params=[sm["mlp_post_norm"], sm["ple_norm"]], ts=512, out_dtypes=[F32, BF16])

    (rowloss,) = proj_stage("loss", _f_out, [(h3, big["w_ple_gate"], "nn", "w_ple_gate", True, F32),
                                             (p, big["w_ple"], "nt", "w_ple", False, BF16)], extra=[x2],
                            consts=[target], ts=512)
    return jnp.sum(rowloss)


WEIGHTS = ["attn_pre_norm", "attn_post_norm", "w_in", "b_gate", "sinks", "q_a_norm", "w_uq", "kv_a_norm", "w_ukv",
           "w_branch_a", "w_branch_b", "w_out", "mlp_pre_norm", "mlp_post_norm", "w_up", "conv_w", "conv_b",
           "w_down", "ple_norm", "w_ple_gate", "w_ple"]


def kernel(x, p, positions, attn_pre_norm, attn_post_norm, w_in, b_gate, sinks, q_a_norm, w_uq, kv_a_norm, w_ukv, w_branch_a, w_branch_b, w_out, mlp_pre_norm, mlp_post_norm, w_up, conv_w, conv_b, w_down, ple_norm, w_ple_gate, w_ple, loss_target, m_attn_pre_norm, m_attn_post_norm, m_w_in, m_b_gate, m_sinks, m_q_a_norm, m_w_uq, m_kv_a_norm, m_w_ukv, m_w_branch_a, m_w_branch_b, m_w_out, m_mlp_pre_norm, m_mlp_post_norm, m_w_up, m_conv_w, m_conv_b, m_w_down, m_ple_norm, m_w_ple_gate, m_w_ple, v_attn_pre_norm, v_attn_post_norm, v_w_in, v_b_gate, v_sinks, v_q_a_norm, v_w_uq, v_kv_a_norm, v_w_ukv, v_w_branch_a, v_w_branch_b, v_w_out, v_mlp_pre_norm, v_mlp_post_norm, v_w_up, v_conv_w, v_conv_b, v_w_down, v_ple_norm, v_w_ple_gate, v_w_ple):
    given = dict(locals())
    s = x.shape[1]
    wts = {n: given[n][0] if given[n].ndim == 3 else given[n] for n in WEIGHTS}
    tables = _rope_tables(positions, s)
    local_loss, (grads, grad_x) = jax.value_and_grad(_local_loss, argnums=(0, 1))(
        wts, x[0], p[0, 0], tables, loss_target[0])
    loss = lax.psum(local_loss, AXES)

    outs = {"grad": [], "delta": [], "m": [], "v": []}
    for n in WEIGHTS:
        shape = given[n].shape
        w2 = wts[n].reshape(-1, shape[-1])
        g2 = grads[n].reshape(w2.shape)
        delta, new_m, new_v = _adamw(w2, g2, given["m_" + n].reshape(w2.shape), given["v_" + n].reshape(w2.shape),
                                     "adamw_" + n)
        outs["grad"].append(g2.reshape(shape))
        outs["delta"].append(delta.reshape(shape))
        outs["m"].append(new_m.reshape(shape))
        outs["v"].append(new_v.reshape(shape))
    return (loss, grad_x[None], *outs["grad"], *outs["delta"], *outs["m"], *outs["v"])
```

```python
import functools

import numpy as np
import jax
import jax.numpy as jnp
from jax import lax
from jax.experimental import pallas as pl
from jax.experimental.pallas import tpu as pltpu

F32 = jnp.float32
BF16 = jnp.bfloat16
MESH_ID = pl.DeviceIdType.MESH
AXES = ("x", "y", "c")
N_DEV = 8

D_MODEL = 1024
RMS_EPS = 1e-6
ROPE_THETA = 10000.0
SWA_BLOCK = 128
A_HEADS, A_KV_HEADS, A_HEAD_DIM = 8, 2, 64
A_GROUP = A_HEADS // A_KV_HEADS
B_HEADS, Q_LORA, KV_LORA, NOPE_DIM, ROPE_DIM, V_DIM = 8, 256, 128, 64, 32, 64
D_FF = 2816
CONV_W = 3
HEAD_PAD = 128

ADAM_LR, ADAM_B1, ADAM_B2, ADAM_EPS, ADAM_WD, ADAM_STEP = 0.001, 0.9, 0.999, 1e-08, 0.01, 10

VMEM_LIMIT = 48 * 1024 * 1024
MM_TM, MM_TN, MM_TK_TOKENS = 1024, 1408, 2048
MM_VMEM_BUDGET = 36 * 1024 * 1024
FLASH_T = 1024
CONV_TS = 256
CONV_CHUNK = 256


def _params(*sem):
    return pltpu.CompilerParams(dimension_semantics=sem, vmem_limit_bytes=VMEM_LIMIT)


def _pick(dim, cap, mult):
    best = None
    for t in range(mult, min(dim, cap) + 1, mult):
        if dim % t == 0:
            best = t
    return dim if best is None else best


def _divisors(dim, mult):
    return [t for t in range(mult, dim + 1, mult) if dim % t == 0] or [dim]


def _matmul_tiles(m, n, kdim, form, sizes):
    sa, sb, so = sizes
    tk = _pick(kdim, MM_TK_TOKENS, 128) if form == "tn" else kdim
    cap_m = MM_TN if form == "tn" else MM_TM
    best = None
    for tm in _divisors(m, 128):
        for tn in _divisors(n, 128):
            need = 2 * (tm * tk * sa + tk * tn * sb + tm * tn * so) + (tm * tn * 4 if tk != kdim else 0)
            if tm > cap_m or tn > MM_TN or need > MM_VMEM_BUDGET:
                continue
            if best is None or (tm * tn, tm) > (best[0] * best[1], best[0]):
                best = (tm, tn)
    return best[0], best[1], tk


def _matmul(a, b, form, *, out_dtype=F32, name):
    if form == "tn":
        (kdim, m), n = a.shape, b.shape[1]
    else:
        (m, kdim), n = a.shape, (b.shape[1] if form == "nn" else b.shape[0])
    sizes = (a.dtype.itemsize, b.dtype.itemsize, jnp.dtype(out_dtype).itemsize)
    tm, tn, tk = _matmul_tiles(m, n, kdim, form, sizes)
    nk = kdim // tk
    rows_outer = nk > 1 or (m // tm) * b.size * sizes[1] <= (n // tn) * a.size * sizes[0]

    def ij(fn):
        return (lambda i, j, k: fn(i, j, k)) if rows_outer else (lambda j, i, k: fn(i, j, k))

    a_spec = (pl.BlockSpec((tk, tm), ij(lambda i, j, k: (k, i))) if form == "tn"
              else pl.BlockSpec((tm, tk), ij(lambda i, j, k: (i, k))))
    b_spec = (pl.BlockSpec((tn, tk), ij(lambda i, j, k: (j, k))) if form == "nt"
              else pl.BlockSpec((tk, tn), ij(lambda i, j, k: (k, j))))
    dims = (((0 if form == "tn" else 1,), (1 if form == "nt" else 0,)), ((), ()))

    def product(a_ref, b_ref):
        return lax.dot_general(a_ref[...].astype(BF16), b_ref[...].astype(BF16), dims, preferred_element_type=F32)

    if nk == 1:
        def body(a_ref, b_ref, o_ref):
            o_ref[...] = product(a_ref, b_ref).astype(o_ref.dtype)

        scratch = []
    else:
        def body(a_ref, b_ref, o_ref, acc_ref):
            k = pl.program_id(2)

            @pl.when(k == 0)
            def _():
                acc_ref[...] = jnp.zeros_like(acc_ref)

            acc_ref[...] += product(a_ref, b_ref)

            @pl.when(k == nk - 1)
            def _():
                o_ref[...] = acc_ref[...].astype(o_ref.dtype)

        scratch = [pltpu.VMEM((tm, tn), F32)]

    return pl.pallas_call(
        body, name=name, grid=(m // tm, n // tn, nk) if rows_outer else (n // tn, m // tm, nk),
        in_specs=[a_spec, b_spec],
        out_specs=pl.BlockSpec((tm, tn), ij(lambda i, j, k: (i, j))),
        out_shape=jax.ShapeDtypeStruct((m, n), out_dtype),
        scratch_shapes=scratch,
        compiler_params=_params("parallel", "parallel", "arbitrary"),
    )(a, b)


def _pairs(bounds):
    return list(zip(bounds[:-1], bounds[1:]))


def _split(v, bounds):
    return [v[:, a:b] for a, b in _pairs(bounds)]


def _stage_build(name, f, tiled, params, consts, splits, ts, out_dtypes, ct_dtypes=None):
    n_t, n_p, n_c = len(tiled), len(params), len(consts)
    ct_dtypes = [t.dtype for t in tiled] if ct_dtypes is None else ct_dtypes
    s = tiled[0].shape[0]
    ts = min(ts, s)
    grid = (s // ts,)
    if splits is None:
        splits = [None] * n_t
    in_bounds = [(0, t.shape[1]) if b is None else tuple(b) for t, b in zip(tiled, splits)]

    def tile_aval(arr):
        return jax.ShapeDtypeStruct((ts, arr.shape[1]), arr.dtype)

    slab_avals = [[jax.ShapeDtypeStruct((ts, e - a), F32) for a, e in _pairs(b)]
                  for t, b in zip(tiled, in_bounds)]
    out_avals = jax.eval_shape(f, slab_avals, list(params), [tile_aval(c) for c in consts])
    out_bounds = [tuple(np.cumsum([0] + [o.shape[1] for o in slabs]).tolist()) for slabs in out_avals]
    out_dtypes = [F32] * len(out_bounds) if out_dtypes is None else out_dtypes
    out_shapes = [jax.ShapeDtypeStruct((s, b[-1]), d) for b, d in zip(out_bounds, out_dtypes)]

    def row_spec(width):
        return pl.BlockSpec((ts, width), lambda i: (i, 0))

    def par_spec(arr):
        return pl.BlockSpec(arr.shape, lambda i: (0, 0))

    in_specs = ([row_spec(t.shape[1]) for t in tiled] + [par_spec(p) for p in params]
                + [row_spec(c.shape[1]) for c in consts])

    def load(refs):
        t = [_split(r[...].astype(F32), b) for r, b in zip(refs[:n_t], in_bounds)]
        p = [r[...] for r in refs[n_t:n_t + n_p]]
        c = [r[...] for r in refs[n_t + n_p:n_t + n_p + n_c]]
        return t, p, c

    def store(refs, values, bounds):
        for ref, slabs, b in zip(refs, values, bounds):
            for v, (a, e) in zip(slabs, _pairs(b)):
                ref[:, a:e] = v.astype(ref.dtype)

    def run_fwd(tiled, params, consts):
        def body(*refs):
            t, p, c = load(refs)
            store(refs[n_t + n_p + n_c:], f(t, p, c), out_bounds)

        return pl.pallas_call(
            body, name=name + "_fwd", grid=grid, in_specs=in_specs,
            out_specs=[row_spec(b[-1]) for b in out_bounds], out_shape=out_shapes,
            compiler_params=_params("parallel"),
        )(*tiled, *params, *consts)

    def run_bwd(tiled, params, consts, cts):
        n_in = n_t + n_p + n_c
        n_o = len(out_bounds)

        def body(*refs):
            t, p, c = load(refs)
            g = [_split(r[...].astype(F32), b) for r, b in zip(refs[n_in:n_in + n_o], out_bounds)]
            _, pull = jax.vjp(lambda t_, p_: f(t_, p_, c), t, p)
            dt, dp = pull(g)
            store(refs[n_in + n_o:n_in + n_o + n_t], dt, in_bounds)
            first = pl.program_id(0) == 0
            for ref, d in zip(refs[n_in + n_o + n_t:], dp):
                @pl.when(first)
                def _(ref=ref):
                    ref[...] = jnp.zeros_like(ref)

                ref[...] += d

        res = pl.pallas_call(
            body, name=name + "_bwd", grid=grid,
            in_specs=in_specs + [row_spec(b[-1]) for b in out_bounds],
            out_specs=[row_spec(t.shape[1]) for t in tiled] + [par_spec(p) for p in params],
            out_shape=[jax.ShapeDtypeStruct(t.shape, d) for t, d in zip(tiled, ct_dtypes)]
                      + [jax.ShapeDtypeStruct(p.shape, F32) for p in params],
            compiler_params=_params("arbitrary"),
        )(*tiled, *params, *consts, *cts)
        return tuple(res[:n_t]), tuple(res[n_t:])

    return run_fwd, run_bwd


def proj_stage(name, f, projections, extra=(), params=(), consts=(), splits=None, ts=256, out_dtypes=None):
    n_z = len(projections)
    forms = [pr[2] for pr in projections]
    names = [pr[3] for pr in projections]
    need_da = [pr[4] for pr in projections]
    store = [pr[5] for pr in projections]
    extra, params, consts = tuple(extra), tuple(params), tuple(consts)

    def matmuls(a_list, w_list):
        return tuple(_matmul(a, w, form, out_dtype=dt, name=n + "_fwd")
                     for a, w, form, n, dt in zip(a_list, w_list, forms, names, store))

    def build(zs, ct=False):
        ct_dtypes = [BF16] * n_z + [e.dtype for e in extra] if ct else None
        return _stage_build(name, f, tuple(zs) + extra, params, consts, splits, ts, out_dtypes, ct_dtypes)

    @jax.custom_vjp
    def op(a_list, w_list, extra, params, consts):
        zs = matmuls(a_list, w_list)
        return tuple(build(zs)[0](zs + extra, params, consts))

    def op_fwd(a_list, w_list, extra, params, consts):
        zs = matmuls(a_list, w_list)
        return tuple(build(zs)[0](zs + extra, params, consts)), (a_list, w_list, zs, extra, params, consts)

    def op_bwd(res, cts):
        a_list, w_list, zs, extra, params, consts = res
        dt, dp = build(zs, ct=True)[1](zs + extra, params, consts, cts)
        da_list, dw_list = [], []
        for a, w, dz, form, n, want in zip(a_list, w_list, dt[:n_z], forms, names, need_da):
            if form == "nn":
                da = _matmul(dz, w, "nt", out_dtype=a.dtype, name=n + "_da") if want else jnp.zeros_like(a)
                dw = _matmul(a, dz, "tn", out_dtype=w.dtype, name=n + "_dw")
            else:
                da = _matmul(dz, w, "nn", out_dtype=a.dtype, name=n + "_da") if want else jnp.zeros_like(a)
                dw = _matmul(dz, a, "tn", out_dtype=w.dtype, name=n + "_dw")
            da_list.append(da)
            dw_list.append(dw)
        return tuple(da_list), tuple(dw_list), tuple(dt[n_z:]), dp, tuple(jnp.zeros_like(c) for c in consts)

    op.defvjp(op_fwd, op_bwd)
    return op(tuple(pr[0] for pr in projections), tuple(pr[1] for pr in projections), extra, params, consts)


def _rms(t, g):
    return t * lax.rsqrt(jnp.mean(t * t, axis=-1, keepdims=True) + RMS_EPS) * g


@functools.partial(jax.custom_vjp, nondiff_argnums=(1,))
def _lane_roll(t, shift):
    return pltpu.roll(t, shift % t.shape[-1], t.ndim - 1)


def _lane_roll_fwd(t, shift):
    return _lane_roll(t, shift), None


def _lane_roll_bwd(shift, _, ct):
    return (pltpu.roll(ct, (-shift) % ct.shape[-1], ct.ndim - 1),)


_lane_roll.defvjp(_lane_roll_fwd, _lane_roll_bwd)


def _rope_lanes(t, tables, half):
    reps = t.shape[1] // tables[0].shape[1]
    c, s_lo, s_hi = [jnp.concatenate([tb] * reps, axis=1) if reps > 1 else tb for tb in tables]
    return t * c + _lane_roll(t, -half) * s_lo + _lane_roll(t, half) * s_hi


PRENORM_TS = 256


def _prenorm_fwd_call(x, g, shards):
    s, width = x.shape
    ts = min(PRENORM_TS, s)
    nt = s // ts
    n_arr = len(shards)

    def body(*refs):
        x_ref, g_ref = refs[:2]
        o_ref = refs[2 + n_arr]
        i = pl.program_id(0)
        ag_start, ag_forward, ag_finish = _allgather_phases(refs[2:2 + n_arr], refs[3 + n_arr:3 + 2 * n_arr],
                                                            *refs[3 + 2 * n_arr:])

        @pl.when(i == 0)
        def _():
            ag_start()

        @pl.when(i == nt // 2)
        def _():
            ag_forward()

        o_ref[...] = _rms(x_ref[...], g_ref[...]).astype(o_ref.dtype)

        @pl.when(i == nt - 1)
        def _():
            ag_finish()

    return pl.pallas_call(
        body, name="prenorm_fwd", grid=(nt,),
        in_specs=[pl.BlockSpec((ts, width), lambda i: (i, 0)), pl.BlockSpec(g.shape, lambda i: (0, 0))]
                 + [HBM_SPEC] * n_arr,
        out_specs=[pl.BlockSpec((ts, width), lambda i: (i, 0))] + [HBM_SPEC] * n_arr,
        out_shape=[jax.ShapeDtypeStruct(x.shape, BF16)] + _allgather_out_shapes(shards),
        scratch_shapes=_allgather_sems(n_arr),
        compiler_params=_params("arbitrary"),
    )(x, g, *shards)


def _prenorm_bwd_call(x, g, dh_a, dh_b, dx_res, parts):
    s, width = x.shape
    ts = min(PRENORM_TS, s)
    nt = s // ts
    n_arr = len(parts)

    def body(*refs):
        x_ref, g_ref, dha_ref, dhb_ref, dxr_ref = refs[:5]
        dx_ref, dg_ref = refs[5 + n_arr:7 + n_arr]
        i = pl.program_id(0)
        exchange_start, exchange_finish = _exchange_chips_phases(
            refs[5:5 + n_arr], refs[7 + n_arr:7 + 2 * n_arr], *refs[7 + 2 * n_arr:])

        @pl.when(i == 0)
        def _():
            exchange_start()
            dg_ref[...] = jnp.zeros_like(dg_ref)

        _, pull = jax.vjp(_rms, x_ref[...], g_ref[...])
        dx, dg = pull(dha_ref[...].astype(F32) + dhb_ref[...].astype(F32))
        dx_ref[...] = dx + dxr_ref[...]
        dg_ref[...] += dg

        @pl.when(i == nt - 1)
        def _():
            exchange_finish()

    row = pl.BlockSpec((ts, width), lambda i: (i, 0))
    par = pl.BlockSpec(g.shape, lambda i: (0, 0))
    return pl.pallas_call(
        body, name="prenorm_bwd", grid=(nt,),
        in_specs=[row, par, row, row, row] + [HBM_SPEC] * n_arr,
        out_specs=[row, par] + [HBM_SPEC] * n_arr,
        out_shape=[jax.ShapeDtypeStruct(x.shape, F32), jax.ShapeDtypeStruct(g.shape, F32)]
                  + [jax.ShapeDtypeStruct(p.shape, p.dtype) for p in parts],
        scratch_shapes=_exchange_chips_sems(n_arr),
        compiler_params=_params("arbitrary"),
    )(x, g, dh_a, dh_b, dx_res, *parts)


@functools.partial(jax.custom_vjp, nondiff_argnums=(3,))
def prenorm_gather(x, g, shards, wire_dtypes):
    out = _prenorm_fwd_call(x, g, [s.astype(d) for s, d in zip(shards, wire_dtypes)])
    return out[0], out[0], x, tuple(out[1:])


def _prenorm_gather_fwd(x, g, shards, wire_dtypes):
    return prenorm_gather(x, g, shards, wire_dtypes), (x, g)


def _prenorm_gather_bwd(wire_dtypes, res, cts):
    x, g = res
    dh_a, dh_b, dx_res, d_gathered = cts
    out = _prenorm_bwd_call(x, g, dh_a, dh_b, dx_res, _reduce_scatter_head(d_gathered, "grads"))
    return out[0], out[1], _reduce_scatter_tail(out[2:], "grads")


prenorm_gather.defvjp(_prenorm_gather_fwd, _prenorm_gather_bwd)


def _f_prep(t, p, c):
    qa, ka, va, cq, ckv, kr = t[0]
    return [[_rope_lanes(qa, c[0:3], A_HEAD_DIM // 2)], [_rope_lanes(ka, c[0:3], A_HEAD_DIM // 2)], [va],
            [_rms(cq, p[0])], [_rms(ckv, p[1])], [_rope_lanes(kr, c[3:6], ROPE_DIM // 2)]]


def _f_qrope(t, p, c):
    return [[_rope_lanes(t[0][0], c, ROPE_DIM // 2)]]


def _f_kv(t, p, c):
    (k_nope, v), (k_pe,) = t
    return [[k_nope + jnp.concatenate([k_pe] * B_HEADS, axis=1)], [v]]


def _f_gate(t, p, c):
    (ga, gb), (pa,), (pb,) = t
    ba, bb = p
    return [[jax.nn.sigmoid(ga + ba) * pa + jax.nn.sigmoid(gb + bb) * pb]]


def _f_post(t, p, c):
    (branch,), (residual,) = t
    x1 = residual + _rms(branch, p[0])
    return [[x1], [_rms(x1, p[1])]]


def _f_out(t, p, c):
    (gate,), (emb,), (x2,) = t
    y = x2 + jax.nn.sigmoid(gate) * emb
    err = y - c[0]
    return [[0.5 * jnp.mean(err * err, axis=-1, keepdims=True)]]


def _shift_down(cur, prev, has_prev):
    full = jnp.concatenate([prev * has_prev, cur], axis=0)
    return pltpu.roll(full, 1, 0)[HALO:], pltpu.roll(full, 2, 0)[HALO:]


GELU_C = float(np.sqrt(2.0 / np.pi))
GELU_A = 0.044715
HALO = 8


def _gelu_tanh(x):
    x2 = x * x
    th = jnp.tanh(x * (GELU_C + (GELU_C * GELU_A) * x2))
    half = 0.5 + 0.5 * th
    return x * half, half + x * (0.5 - 0.5 * (th * th)) * (GELU_C + (3.0 * GELU_C * GELU_A) * x2)


def _row_sum(t):
    return jnp.sum(t, axis=0, keepdims=True)


def _conv3(cur, prev, w_ref, b_ref, has_prev):
    u1, u2 = _shift_down(cur, prev, has_prev)
    return w_ref[2:3, :] * cur + w_ref[1:2, :] * u1 + w_ref[0:1, :] * u2 + b_ref[...], u1, u2


def _mlp_act_specs(s):
    ts = min(CONV_TS, s)
    hb = ts // HALO

    def half_specs(h):
        return [pl.BlockSpec((ts, D_FF), lambda i: (i, h)),
                pl.BlockSpec((HALO, D_FF), lambda i: (jnp.maximum(i * hb - 1, 0), h))]

    def par_specs(h):
        return [pl.BlockSpec((CONV_W, D_FF), lambda i: (0, h)), pl.BlockSpec((1, D_FF), lambda i: (0, h))]

    return ts, hb, half_specs, par_specs


def _mlp_act_fwd_call(up, conv_w, conv_b):
    s = up.shape[0]
    ts, hb, half_specs, par_specs = _mlp_act_specs(s)

    def body(g_ref, gp_ref, v_ref, vp_ref, wg_ref, bg_ref, wv_ref, bv_ref, o_ref):
        has_prev = (pl.program_id(0) > 0).astype(F32)

        def chunk(cidx, carry):
            cols = pl.ds(pl.multiple_of(cidx * CONV_CHUNK, CONV_CHUNK), CONV_CHUNK)
            u_g, _, _ = _conv3(g_ref[:, cols], gp_ref[:, cols], wg_ref.at[:, cols], bg_ref.at[:, cols], has_prev)
            u_v, _, _ = _conv3(v_ref[:, cols], vp_ref[:, cols], wv_ref.at[:, cols], bv_ref.at[:, cols], has_prev)
            o_ref[:, cols] = (_gelu_tanh(u_g)[0] * u_v).astype(o_ref.dtype)
            return carry

        lax.fori_loop(0, D_FF // CONV_CHUNK, chunk, 0)

    return pl.pallas_call(
        body, name="mlp_act_fwd", grid=(s // ts,),
        in_specs=half_specs(0) + half_specs(1) + par_specs(0) + par_specs(1),
        out_specs=pl.BlockSpec((ts, D_FF), lambda i: (i, 0)),
        out_shape=jax.ShapeDtypeStruct((s, D_FF), BF16),
        compiler_params=_params("parallel"),
    )(up, up, up, up, conv_w, conv_b, conv_w, conv_b)


def _mlp_act_bwd_call(up, conv_w, conv_b, dact):
    s = up.shape[0]
    ts, hb, half_specs, par_specs = _mlp_act_specs(s)
    nt = s // ts
    ext = ts + HALO
    bf16_rows = 2 * HALO

    def next_spec(rows, h):
        return pl.BlockSpec((rows, D_FF), lambda i: (jnp.minimum((i + 1) * (ts // rows), s // rows - 1), h))

    def body(g_ref, gp_ref, gn_ref, v_ref, vp_ref, vn_ref, wg_ref, bg_ref, wv_ref, bv_ref, da_ref, dan_ref,
             dup_ref, dwg_ref, dbg_ref, dwv_ref, dbv_ref):
        i = pl.program_id(0)
        has_prev, has_next = (i > 0).astype(F32), (i < nt - 1).astype(F32)

        @pl.when(i == 0)
        def _():
            for ref in (dwg_ref, dbg_ref, dwv_ref, dbv_ref):
                ref[...] = jnp.zeros_like(ref)

        def chunk(cidx, carry):
            cols = pl.ds(pl.multiple_of(cidx * CONV_CHUNK, CONV_CHUNK), CONV_CHUNK)
            g_ext = jnp.concatenate([g_ref[:, cols], gn_ref[:, cols]], axis=0)
            v_ext = jnp.concatenate([v_ref[:, cols], vn_ref[:, cols]], axis=0)
            u_g, g1, g2 = _conv3(g_ext, gp_ref[:, cols], wg_ref.at[:, cols], bg_ref.at[:, cols], has_prev)
            u_v, v1, v2 = _conv3(v_ext, vp_ref[:, cols], wv_ref.at[:, cols], bv_ref.at[:, cols], has_prev)
            da_ext = jnp.concatenate([da_ref[:, cols].astype(F32),
                                      dan_ref[:, cols].astype(F32)[0:HALO] * has_next], axis=0)
            act_g, dact_g = _gelu_tanh(u_g)
            du_g = da_ext * u_v * dact_g
            du_v = da_ext * act_g
            for du, w_ref, x0, x1, x2, dw_ref, db_ref, lo in ((du_g, wg_ref, g_ext, g1, g2, dwg_ref, dbg_ref, 0),
                                                          (du_v, wv_ref, v_ext, v1, v2, dwv_ref, dbv_ref, D_FF)):
                d1 = pltpu.roll(du, ext - 1, 0)
                d2 = pltpu.roll(du, ext - 2, 0)
                dup = w_ref[2:3, cols] * du + w_ref[1:2, cols] * d1 + w_ref[0:1, cols] * d2
                out_cols = pl.ds(pl.multiple_of(lo + cidx * CONV_CHUNK, CONV_CHUNK), CONV_CHUNK)
                dup_ref[:, out_cols] = dup[0:ts].astype(dup_ref.dtype)
                own = du[0:ts]
                dw_ref[0:1, cols] += _row_sum(own * x2[0:ts])
                dw_ref[1:2, cols] += _row_sum(own * x1[0:ts])
                dw_ref[2:3, cols] += _row_sum(own * x0[0:ts])
                db_ref[:, cols] += _row_sum(own)
            return carry

        lax.fori_loop(0, D_FF // CONV_CHUNK, chunk, 0)

    par_out = [pl.BlockSpec((CONV_W, D_FF), lambda i: (0, 0)), pl.BlockSpec((1, D_FF), lambda i: (0, 0))]
    par_shapes = [jax.ShapeDtypeStruct((CONV_W, D_FF), F32), jax.ShapeDtypeStruct((1, D_FF), F32)]
    return pl.pallas_call(
        body, name="mlp_act_bwd", grid=(nt,),
        in_specs=(half_specs(0) + [next_spec(HALO, 0)] + half_specs(1) + [next_spec(HALO, 1)]
                  + par_specs(0) + par_specs(1)
                  + [pl.BlockSpec((ts, D_FF), lambda i: (i, 0)), next_spec(bf16_rows, 0)]),
        out_specs=[pl.BlockSpec((ts, 2 * D_FF), lambda i: (i, 0))] + par_out + par_out,
        out_shape=[jax.ShapeDtypeStruct((s, 2 * D_FF), BF16)] + par_shapes + par_shapes,
        compiler_params=_params("arbitrary"),
    )(up, up, up, up, up, up, conv_w, conv_b, conv_w, conv_b, dact, dact)


@jax.custom_vjp
def mlp_up(h2, w_up_t, conv_w, conv_b):
    return _mlp_act_fwd_call(_matmul(h2, w_up_t, "nt", out_dtype=F32, name="w_up_fwd"), conv_w, conv_b)


def _mlp_up_fwd(h2, w_up_t, conv_w, conv_b):
    up = _matmul(h2, w_up_t, "nt", out_dtype=F32, name="w_up_fwd")
    return _mlp_act_fwd_call(up, conv_w, conv_b), (h2, w_up_t, up, conv_w, conv_b)


def _mlp_up_bwd(res, dact):
    h2, w_up_t, up, conv_w, conv_b = res
    dup, dwg, dbg, dwv, dbv = _mlp_act_bwd_call(up, conv_w, conv_b, dact)
    dh2 = _matmul(dup, w_up_t, "nn", out_dtype=h2.dtype, name="w_up_da")
    dw = _matmul(dup, h2, "tn", out_dtype=w_up_t.dtype, name="w_up_dw")
    return dh2, dw, jnp.concatenate([dwg, dwv], axis=1), jnp.concatenate([dbg, dbv], axis=1)


mlp_up.defvjp(_mlp_up_fwd, _mlp_up_bwd)


SWA_ROWS = A_GROUP * SWA_BLOCK


def _swa_sink_rows(sink_ref, g):
    return jnp.concatenate([jnp.full((SWA_BLOCK, 1), sink_ref[g * A_GROUP + h], F32) for h in range(A_GROUP)], axis=0)


def _swa_operands(q_ref, kp_ref, kc_ref, vp_ref, vc_ref, sink_ref):
    groups = []
    for g in range(A_KV_HEADS):
        groups.append((_swa_stack_heads(q_ref, g), _dup_half(kp_ref[...], g), _dup_half(kc_ref[...], g),
                       _dup_half(vp_ref[...], g), _dup_half(vc_ref[...], g)))
    return groups, jnp.concatenate([_swa_sink_rows(sink_ref, g) for g in range(A_KV_HEADS)], axis=0)


def _swa_probs(groups, sink, prev_off):
    scale = A_HEAD_DIM ** -0.5
    sp = jnp.concatenate([lax.dot_general(gr[0], gr[1], NT_DIMS, preferred_element_type=F32) for gr in groups], axis=0)
    sc = jnp.concatenate([lax.dot_general(gr[0], gr[2], NT_DIMS, preferred_element_type=F32) for gr in groups], axis=0)
    qi = lax.broadcasted_iota(jnp.int32, sp.shape, 0) & (SWA_BLOCK - 1)
    kj = lax.broadcasted_iota(jnp.int32, sp.shape, 1)
    in_cur = kj <= qi
    sw = jnp.where(in_cur, sc, jnp.where(kj > qi + prev_off, sp, -jnp.inf)) * scale
    m = jnp.maximum(jnp.max(sw, axis=-1, keepdims=True), sink)
    e, es = jnp.exp(sw - m), jnp.exp(sink - m)
    den = jnp.sum(e, axis=-1, keepdims=True) + es
    return e / den, in_cur, es / den


def _swa_split(t, in_cur):
    cur = jnp.where(in_cur, t, 0.0)
    return t - cur, cur


MLA_SCALE = (NOPE_DIM + ROPE_DIM) ** -0.5
EXP2_SCALE = MLA_SCALE * float(np.log2(np.e))
NT_DIMS = (((1,), (1,)), ((), ()))
TN_DIMS = (((0,), (0,)), ((), ()))


LANES = 128
HALF = LANES // 2


def _low_half(shape):
    return lax.broadcasted_iota(jnp.int32, shape, len(shape) - 1) < HALF


def _dup_half(x, g):
    xf = x.astype(F32)
    keep = _low_half(xf.shape) if g == 0 else jnp.logical_not(_low_half(xf.shape))
    xm = jnp.where(keep, xf, 0.0)
    return (xm + pltpu.roll(xm, HALF, 1)).astype(x.dtype)


def _fold_half(r, g):
    total = r + pltpu.roll(r, HALF, 1)
    keep = _low_half(r.shape) if g == 0 else jnp.logical_not(_low_half(r.shape))
    return jnp.where(keep, total, 0.0)


def _swa_stack_heads(ref, g):
    parts = []
    for tile in range(2):
        slab = ref[:, (2 * g + tile) * LANES:(2 * g + tile + 1) * LANES]
        low = _low_half(slab.shape)
        parts += [jnp.where(low, slab, jnp.zeros_like(slab)), jnp.where(low, jnp.zeros_like(slab), slab)]
    return jnp.concatenate(parts, axis=0)


def _swa_unstack_heads(ref, g, rows):
    for tile in range(2):
        a = rows[(2 * tile) * SWA_BLOCK:(2 * tile + 1) * SWA_BLOCK]
        b = rows[(2 * tile + 1) * SWA_BLOCK:(2 * tile + 2) * SWA_BLOCK]
        ref[:, (2 * g + tile) * LANES:(2 * g + tile + 1) * LANES] = jnp.where(_low_half(a.shape), a, b).astype(ref.dtype)


def _swa_nat_specs():
    blk = SWA_BLOCK
    q_spec = pl.BlockSpec((blk, A_HEADS * A_HEAD_DIM), lambda n: (n, 0))
    prev_spec = pl.BlockSpec((blk, LANES), lambda n: (jnp.maximum(n - 1, 0), 0))
    cur_spec = pl.BlockSpec((blk, LANES), lambda n: (n, 0))
    return q_spec, prev_spec, cur_spec, pl.BlockSpec(memory_space=pltpu.SMEM)


def _swa_nat_fwd_call(q, k, v, sinks, shards):
    s = q.shape[0]
    nblk = s // SWA_BLOCK
    n_arr = len(shards)
    q_spec, prev_spec, cur_spec, sink_spec = _swa_nat_specs()

    def body(*refs):
        q_ref, kp_ref, kc_ref, vp_ref, vc_ref, sink_ref = refs[:6]
        o_ref = refs[6 + n_arr]
        n = pl.program_id(0)
        ag_start, ag_forward, ag_finish = _allgather_phases(refs[6:6 + n_arr], refs[7 + n_arr:7 + 2 * n_arr],
                                                            *refs[7 + 2 * n_arr:])

        @pl.when(n == 0)
        def _():
            ag_start()

        @pl.when(n == (3 * nblk) // 4)
        def _():
            ag_forward()

        prev_off = jnp.where(n > 0, 0, SWA_BLOCK)
        groups, sink = _swa_operands(q_ref, kp_ref, kc_ref, vp_ref, vc_ref, sink_ref)
        p, in_cur, _ = _swa_probs(groups, sink, prev_off)
        ppb, pcb = [t.astype(BF16) for t in _swa_split(p, in_cur)]
        for g, (_, _, _, vp, vc) in enumerate(groups):
            rows = slice(g * SWA_ROWS, (g + 1) * SWA_ROWS)
            out = (jnp.dot(ppb[rows], vp, preferred_element_type=F32)
                   + jnp.dot(pcb[rows], vc, preferred_element_type=F32))
            _swa_unstack_heads(o_ref, g, out)

        @pl.when(n == nblk - 1)
        def _():
            ag_finish()

    return pl.pallas_call(
        body, name="swa_fwd", grid=(nblk,),
        in_specs=[q_spec, prev_spec, cur_spec, prev_spec, cur_spec, sink_spec] + [HBM_SPEC] * n_arr,
        out_specs=[q_spec] + [HBM_SPEC] * n_arr,
        out_shape=[jax.ShapeDtypeStruct(q.shape, BF16)] + _allgather_out_shapes(shards),
        scratch_shapes=_allgather_sems(n_arr),
        compiler_params=_params("arbitrary"),
    )(q, k, k, v, v, sinks, *shards)


def _swa_nat_bwd_call(q, k, v, sinks, do, parts):
    s = q.shape[0]
    nblk = s // SWA_BLOCK
    n_arr = len(parts)
    q_spec, prev_spec, cur_spec, sink_spec = _swa_nat_specs()
    scale = A_HEAD_DIM ** -0.5
    dsink_spec = pl.BlockSpec((A_KV_HEADS, SWA_ROWS, 1), lambda n: (0, 0, 0))

    def body(*refs):
        q_ref, kp_ref, kc_ref, vp_ref, vc_ref, sink_ref, do_ref = refs[:7]
        dq_ref, dkp_ref, dkc_ref, dvp_ref, dvc_ref, dsink_ref = refs[7 + n_arr:13 + n_arr]
        n = pl.program_id(0)
        exchange_start, exchange_finish = _exchange_all_phases(
            refs[7:7 + n_arr], refs[13 + n_arr:13 + 2 * n_arr], *refs[13 + 2 * n_arr:])

        @pl.when(n == 0)
        def _():
            exchange_start()
        prev_off = jnp.where(n > 0, 0, SWA_BLOCK)

        @pl.when(n == 0)
        def _():
            dsink_ref[...] = jnp.zeros_like(dsink_ref)

        groups, sink = _swa_operands(q_ref, kp_ref, kc_ref, vp_ref, vc_ref, sink_ref)
        dobs = [_swa_stack_heads(do_ref, g) for g in range(A_KV_HEADS)]
        p, in_cur, ps = _swa_probs(groups, sink, prev_off)
        ppb, pcb = [t.astype(BF16) for t in _swa_split(p, in_cur)]

        def per_group(fn):
            return jnp.concatenate([fn(g, slice(g * SWA_ROWS, (g + 1) * SWA_ROWS)) for g in range(A_KV_HEADS)], axis=0)

        out = per_group(lambda g, rows: jnp.dot(ppb[rows], groups[g][3], preferred_element_type=F32)
                        + jnp.dot(pcb[rows], groups[g][4], preferred_element_type=F32))
        delta = jnp.sum(jnp.concatenate(dobs, axis=0).astype(F32) * out, axis=-1, keepdims=True)
        dp = jnp.where(in_cur,
                       per_group(lambda g, rows: lax.dot_general(dobs[g], groups[g][4], NT_DIMS,
                                                                 preferred_element_type=F32)),
                       per_group(lambda g, rows: lax.dot_general(dobs[g], groups[g][3], NT_DIMS,
                                                                 preferred_element_type=F32)))
        dsp, dsc = [t.astype(BF16) for t in _swa_split(p * (dp - delta), in_cur)]
        dsink_ref[...] += (-ps * delta).reshape(dsink_ref.shape)
        totals = [jnp.zeros((SWA_BLOCK, LANES), F32) for _ in range(4)]
        for g, (qb, kp, kc, _, _) in enumerate(groups):
            rows = slice(g * SWA_ROWS, (g + 1) * SWA_ROWS)
            dq = (jnp.dot(dsp[rows], kp, preferred_element_type=F32)
                  + jnp.dot(dsc[rows], kc, preferred_element_type=F32)) * scale
            _swa_unstack_heads(dq_ref, g, dq)
            pieces = [lax.dot_general(dsp[rows], qb, TN_DIMS, preferred_element_type=F32) * scale,
                      lax.dot_general(dsc[rows], qb, TN_DIMS, preferred_element_type=F32) * scale,
                      lax.dot_general(ppb[rows], dobs[g], TN_DIMS, preferred_element_type=F32),
                      lax.dot_general(pcb[rows], dobs[g], TN_DIMS, preferred_element_type=F32)]
            totals = [tot + _fold_half(r, g) for tot, r in zip(totals, pieces)]
        dkp_ref[...], dkc_ref[...], dvp_ref[...], dvc_ref[...] = totals

        @pl.when(n == nblk - 1)
        def _():
            exchange_finish()

    kv_shape = jax.ShapeDtypeStruct(k.shape, F32)
    return pl.pallas_call(
        body, name="swa_bwd", grid=(nblk,),
        in_specs=[q_spec, prev_spec, cur_spec, prev_spec, cur_spec, sink_spec, q_spec] + [HBM_SPEC] * n_arr,
        out_specs=[q_spec, cur_spec, cur_spec, cur_spec, cur_spec, dsink_spec] + [HBM_SPEC] * n_arr,
        out_shape=[jax.ShapeDtypeStruct(q.shape, q.dtype), kv_shape, kv_shape, kv_shape, kv_shape,
                   jax.ShapeDtypeStruct((A_KV_HEADS, SWA_ROWS, 1), F32)]
                  + [jax.ShapeDtypeStruct(p.shape, p.dtype) for p in parts],
        scratch_shapes=_exchange_all_sems(n_arr),
        compiler_params=_params("arbitrary"),
    )(q, k, k, v, v, sinks, do, *parts)


@jax.custom_vjp
def swa_nat(q, k, v, sinks, shards):
    out = _swa_nat_fwd_call(q, k, v, sinks, [s.astype(BF16) for s in shards])
    return out[0], tuple(out[1:])


def _swa_nat_fwd(q, k, v, sinks, shards):
    out = _swa_nat_fwd_call(q, k, v, sinks, [s.astype(BF16) for s in shards])
    return (out[0], tuple(out[1:])), (q, k, v, sinks)


def _swa_nat_bwd(res, cts):
    q, k, v, sinks = res
    do, d_gathered = cts
    out = _swa_nat_bwd_call(q, k, v, sinks, do, list(d_gathered))
    dq, dkp, dkc, dvp, dvc, dsink = out[:6]

    def fold(prev_part, cur_part):
        shifted = jnp.concatenate([prev_part[SWA_BLOCK:], jnp.zeros_like(prev_part[:SWA_BLOCK])], axis=0)
        return (cur_part + shifted).astype(k.dtype)

    dsinks = jnp.sum(dsink.reshape(A_HEADS, SWA_BLOCK), axis=1)
    return dq, fold(dkp, dkc), fold(dvp, dvc), dsinks, _reduce_scatter_tail(out[6:], "mid_grads")


swa_nat.defvjp(_swa_nat_fwd, _swa_nat_bwd)

N_PAIR = B_HEADS // 2


def _flash_nat_fwd_call(q, k, v, shards):
    s = q.shape[0]
    t = min(FLASH_T, s)
    nb = s // t
    d = LANES
    n_arr = len(shards)

    def body(*refs):
        q_ref, k_ref, v_ref = refs[:3]
        shard_refs = refs[3:3 + n_arr]
        o_ref, lse_ref = refs[3 + n_arr:5 + n_arr]
        gathered_refs = refs[5 + n_arr:5 + 2 * n_arr]
        vt_ref, m_ref, l_ref, acc_ref = refs[5 + 2 * n_arr:9 + 2 * n_arr]
        pair, i = pl.program_id(0), pl.program_id(1)
        ag_start, ag_forward, ag_finish = _allgather_phases(shard_refs, gathered_refs, *refs[9 + 2 * n_arr:])

        @pl.when((pair == 0) & (i == 0))
        def _():
            ag_start()

        @pl.when((pair == N_PAIR - 1) & (i == nb // 2))
        def _():
            ag_forward()

        @pl.when(i == 0)
        def _():
            for hh in range(2):
                for chunk in range(nb):
                    rows = slice(chunk * t, (chunk + 1) * t)
                    vt_ref[hh, :, rows] = v_ref[rows, hh * d:(hh + 1) * d].T

        m_ref[...] = jnp.full_like(m_ref, -jnp.inf)
        l_ref[...] = jnp.zeros_like(l_ref)
        acc_ref[...] = jnp.zeros_like(acc_ref)

        def step(j, on_diagonal):
            keys = pl.ds(pl.multiple_of(j * t, t), t)
            scores = [lax.dot_general(k_ref[keys, hh * d:(hh + 1) * d], q_ref[:, hh * d:(hh + 1) * d], NT_DIMS,
                                      preferred_element_type=F32) for hh in range(2)]
            for hh in range(2):
                sc_t = scores[hh]
                if on_diagonal:
                    key = lax.broadcasted_iota(jnp.int32, (t, t), 0)
                    qry = lax.broadcasted_iota(jnp.int32, (t, t), 1)
                    sc_t = jnp.where(qry >= key, sc_t, -jnp.inf)
                m_old = m_ref[hh]
                m_new = jnp.maximum(m_old, jnp.max(sc_t, axis=0, keepdims=True))
                alpha = jnp.exp2((m_old - m_new) * EXP2_SCALE)
                p_t = jnp.exp2((sc_t - m_new) * EXP2_SCALE)
                l_ref[hh] = alpha * l_ref[hh] + jnp.sum(p_t, axis=0, keepdims=True)
                acc_ref[hh] = alpha * acc_ref[hh] + jnp.dot(vt_ref[hh, :, keys], p_t.astype(BF16),
                                                            preferred_element_type=F32)
                m_ref[hh] = m_new

        def below(j, carry):
            step(j, False)
            return carry

        lax.fori_loop(0, i, below, 0)
        step(i, True)
        outs =[(acc_ref[hh] / l_ref[hh]).T for hh in range(2)]
        for hh in range(2):
            lse_ref[hh] = m_ref[hh] * EXP2_SCALE + jnp.log2(l_ref[hh])
        o_ref[...] = (outs[0] + pltpu.roll(outs[1], HALF, 1)).astype(o_ref.dtype)

        @pl.when((pair == N_PAIR - 1) & (i == nb - 1))
        def _():
            ag_finish()

    return pl.pallas_call(
        body, name="mla_fwd", grid=(N_PAIR, nb),
        in_specs=[pl.BlockSpec((t, 2 * d), lambda p, i: (i, p)),
                  pl.BlockSpec((s, 2 * d), lambda p, i: (0, p)),
                  pl.BlockSpec((s, 2 * d), lambda p, i: (0, p))] + [HBM_SPEC] * n_arr,
        out_specs=[pl.BlockSpec((t, d), lambda p, i: (i, p)),
                   pl.BlockSpec((2, 1, t), lambda p, i: (p, 0, i))] + [HBM_SPEC] * n_arr,
        out_shape=[jax.ShapeDtypeStruct((s, N_PAIR * d), BF16), jax.ShapeDtypeStruct((B_HEADS, 1, s), F32)]
                  + _allgather_out_shapes(shards),
        scratch_shapes=[pltpu.VMEM((2, d, s), BF16), pltpu.VMEM((2, 1, t), F32), pltpu.VMEM((2, 1, t), F32),
                        pltpu.VMEM((2, d, t), F32)] + _allgather_sems(n_arr),
        compiler_params=_params("arbitrary", "arbitrary"),
    )(q, k, v, *shards)


def _flash_nat_delta_call(o, do):
    s, w = o.shape
    t = min(FLASH_T, s)

    def body(o_ref, do_ref, out_ref):
        prod = o_ref[...].astype(F32) * do_ref[...].astype(F32)
        lane = lax.broadcasted_iota(jnp.int32, (w, LANES), 0) // V_DIM
        head = lax.broadcasted_iota(jnp.int32, (w, LANES), 1)
        out_ref[...] = jnp.dot(prod, (lane == head).astype(F32), precision=lax.Precision.HIGHEST,
                               preferred_element_type=F32)

    spec = pl.BlockSpec((t, w), lambda i: (i, 0))
    return pl.pallas_call(
        body, name="mla_delta", grid=(s // t,), in_specs=[spec, spec],
        out_specs=pl.BlockSpec((t, LANES), lambda i: (i, 0)),
        out_shape=jax.ShapeDtypeStruct((s, LANES), F32), compiler_params=_params("parallel"),
    )(o, do)


def _flash_nat_bwd_call(q, k, v, lse_row, delta_row, do, parts):
    s = q.shape[0]
    t = min(FLASH_T, s)
    nb = s // t
    d = LANES
    n_arr = len(parts)

    def body(*refs):
        q_ref, k_ref, v_ref, lse_ref, delta_ref, do_ref = refs[:6]
        part_refs = refs[6:6 + n_arr]
        dq_ref, dk_ref, dv_ref = refs[6 + n_arr:9 + n_arr]
        received_refs = refs[9 + n_arr:9 + 2 * n_arr]
        dq_acc, dk_acc, dv_acc = refs[9 + 2 * n_arr:12 + 2 * n_arr]
        pair, j = pl.program_id(0), pl.program_id(1)
        exchange_start, exchange_finish = _exchange_all_phases(part_refs, received_refs, *refs[12 + 2 * n_arr:])

        @pl.when((pair == 0) & (j == 0))
        def _():
            exchange_start()

        @pl.when(j == 0)
        def _():
            dq_acc[...] = jnp.zeros_like(dq_acc)

        for hh in range(2):
            kb, vb = k_ref[:, hh * d:(hh + 1) * d], v_ref[:, hh * d:(hh + 1) * d]
            dk_acc[...] = jnp.zeros_like(dk_acc)
            dv_acc[...] = jnp.zeros_like(dv_acc)

            def step(i, on_diagonal, hh=hh, kb=kb, vb=vb):
                rows = pl.ds(pl.multiple_of(i * t, t), t)
                qb = q_ref[rows, hh * d:(hh + 1) * d]
                do_pair = do_ref[rows, :].astype(F32)
                do_h = do_pair if hh == 0 else pltpu.roll(do_pair, HALF, 1)
                dob = jnp.where(_low_half(do_h.shape), do_h, 0.0).astype(BF16)
                sc_t = lax.dot_general(kb, qb, NT_DIMS, preferred_element_type=F32)
                p_t = jnp.exp2(sc_t * EXP2_SCALE - lse_ref[hh, :, rows])
                if on_diagonal:
                    key = lax.broadcasted_iota(jnp.int32, (t, t), 0)
                    qry = lax.broadcasted_iota(jnp.int32, (t, t), 1)
                    p_t = jnp.where(qry >= key, p_t, 0.0)
                dp_t = lax.dot_general(vb, dob, NT_DIMS, preferred_element_type=F32)
                ds_t = (p_t * (dp_t - delta_ref[hh, :, rows])).astype(BF16)
                dv_acc[...] += jnp.dot(p_t.astype(BF16), dob, preferred_element_type=F32)
                dk_acc[...] += jnp.dot(ds_t, qb, preferred_element_type=F32)
                dq_acc[hh, rows, :] += lax.dot_general(ds_t, kb, TN_DIMS, preferred_element_type=F32)

            def above(i, carry, step=step):
                step(i, False)
                return carry

            step(j, True)
            lax.fori_loop(j + 1, nb, above, 0)
            dk_ref[:, hh * d:(hh + 1) * d] = (dk_acc[...] * MLA_SCALE).astype(dk_ref.dtype)
            dv_ref[:, hh * d:(hh + 1) * d] = dv_acc[...].astype(dv_ref.dtype)

        @pl.when(j == nb - 1)
        def _():
            for hh in range(2):
                dq_ref[:, hh * d:(hh + 1) * d] = (dq_acc[hh] * MLA_SCALE).astype(dq_ref.dtype)

        @pl.when((pair == N_PAIR - 1) & (j == nb - 1))
        def _():
            exchange_finish()

    full_spec = pl.BlockSpec((s, 2 * d), lambda p, j: (0, p))
    tile_spec = pl.BlockSpec((t, 2 * d), lambda p, j: (j, p))
    row_spec = pl.BlockSpec((2, 1, s), lambda p, j: (p, 0, 0))
    return pl.pallas_call(
        body, name="mla_bwd", grid=(N_PAIR, nb),
        in_specs=[full_spec, tile_spec, tile_spec, row_spec, row_spec, pl.BlockSpec((s, d), lambda p, j: (0, p))]
                 + [HBM_SPEC] * n_arr,
        out_specs=[full_spec, tile_spec, tile_spec] + [HBM_SPEC] * n_arr,
        out_shape=[jax.ShapeDtypeStruct(q.shape, q.dtype)] * 3 + [jax.ShapeDtypeStruct(p.shape, p.dtype) for p in parts],
        scratch_shapes=[pltpu.VMEM((2, s, d), F32), pltpu.VMEM((t, d), F32), pltpu.VMEM((t, d), F32)]
                       + _exchange_all_sems(n_arr),
        compiler_params=_params("arbitrary", "arbitrary"),
    )(q, k, v, lse_row, delta_row, do, *parts)


def _reduce_scatter_head(cts, tag):
    received = _exchange_sibling(list(cts), tag + "_exchange_sibling")
    my_c = lax.axis_index("c").astype(jnp.int32).reshape(1)
    return [_pair_add(m, r, my_c, "%s_pair_add_%d" % (tag, i)) for i, (m, r) in enumerate(zip(cts, received))]


def _reduce_scatter_tail(chip_parts, tag):
    return tuple(_sum_blocks(r, "%s_sum_%d" % (tag, i)) for i, r in enumerate(chip_parts))


@jax.custom_vjp
def flash_nat(q, k, v, shards):
    out = _flash_nat_fwd_call(q, k, v, [s.astype(BF16) for s in shards])
    return out[0], tuple(out[2:])


def _flash_nat_fwd(q, k, v, shards):
    out = _flash_nat_fwd_call(q, k, v, [s.astype(BF16) for s in shards])
    return (out[0], tuple(out[2:])), (q, k, v, out[0], out[1])


def _flash_nat_bwd(res, cts):
    q, k, v, o, lse = res
    do, d_gathered = cts
    delta = _flash_nat_delta_call(o, do)[:, :B_HEADS].T.reshape(B_HEADS, 1, q.shape[0])
    out = _flash_nat_bwd_call(q, k, v, lse, delta, do, list(d_gathered))
    return out[0], out[1], out[2], _reduce_scatter_tail(out[3:], "mlp_grads")


flash_nat.defvjp(_flash_nat_fwd, _flash_nat_bwd)


HBM_SPEC = pl.BlockSpec(memory_space=pltpu.HBM)


def _allgather(shards, name):
    n_arr = len(shards)

    def body(*refs):
        start, forward, finish = _allgather_phases(refs[:n_arr], refs[n_arr:2 * n_arr], *refs[2 * n_arr:])
        start()
        forward()
        finish()

    return pl.pallas_call(
        body, name=name, out_shape=_allgather_out_shapes(shards),
        in_specs=[HBM_SPEC] * n_arr, out_specs=[HBM_SPEC] * n_arr,
        scratch_shapes=_allgather_sems(n_arr),
    )(*shards)


def _allgather_out_shapes(shards):
    return [jax.ShapeDtypeStruct((N_DEV,) + s.shape, s.dtype) for s in shards]


def _allgather_sems(n_arr):
    return [pltpu.SemaphoreType.DMA((7, n_arr)), pltpu.SemaphoreType.DMA((7, n_arr)), pltpu.SemaphoreType.DMA((n_arr,))]


def _allgather_phases(x_refs, out_refs, send_sems, recv_sems, local_sems):
    arrays = range(len(x_refs))
    x, y, c = lax.axis_index("x"), lax.axis_index("y"), lax.axis_index("c")
    me, sibling = (x, y, c), (x, y, 1 - c)
    chips = [(1 - x, y), (x, 1 - y), (1 - x, 1 - y)]

    def rows(a, px, py, pc):
        return out_refs[a].at[4 * px + 2 * py + pc]

    def copy(a, k, block, to, src=None):
        return pltpu.make_async_remote_copy(
            src_ref=rows(a, *block) if src is None else src, dst_ref=rows(a, *block),
            send_sem=send_sems.at[k, a], recv_sem=recv_sems.at[k, a], device_id=to, device_id_type=MESH_ID)

    def mine():
        return [pltpu.make_async_copy(x_refs[a], rows(a, *me), local_sems.at[a]) for a in arrays]

    def first():
        return [cp for a in arrays for cp in
                [copy(a, 0, me, sibling, src=x_refs[a])]
                + [copy(a, 1 + j, me, (*chip, c), src=x_refs[a]) for j, chip in enumerate(chips)]]

    def passed():
        return [copy(a, 4 + j, (*chip, c), sibling) for j, chip in enumerate(chips) for a in arrays]

    def start():
        for cp in mine() + first():
            cp.start()

    def forward():
        for j, chip in enumerate(chips):
            for a in arrays:
                copy(a, 1 + j, (*chip, c), me).wait_recv()
                copy(a, 4 + j, (*chip, c), sibling).start()

    def finish():
        for a in arrays:
            copy(a, 0, sibling, me).wait_recv()
        for j, chip in enumerate(chips):
            for a in arrays:
                copy(a, 4 + j, (*chip, 1 - c), me).wait_recv()
        for cp in first() + passed():
            cp.wait_send()
        for cp in mine():
            cp.wait()

    return start, forward, finish


N_CHIP = 4


def _exchange_sibling(parts, name):
    n_arr = len(parts)

    def body(*refs):
        in_refs, recv_refs = refs[:n_arr], refs[n_arr:2 * n_arr]
        send_sems, recv_sems = refs[2 * n_arr:]
        x, y, c = lax.axis_index("x"), lax.axis_index("y"), lax.axis_index("c")
        copies = []
        for a in range(n_arr):
            for q in range(N_CHIP):
                copies.append(pltpu.make_async_remote_copy(
                    src_ref=in_refs[a].at[2 * q + 1 - c], dst_ref=recv_refs[a].at[q],
                    send_sem=send_sems.at[q, a], recv_sem=recv_sems.at[q, a],
                    device_id=(x, y, 1 - c), device_id_type=MESH_ID))
        for cp in copies:
            cp.start()
        for cp in copies:
            cp.wait()

    return pl.pallas_call(
        body, name=name, out_shape=[jax.ShapeDtypeStruct((N_CHIP,) + p.shape[1:], p.dtype) for p in parts],
        in_specs=[HBM_SPEC] * n_arr, out_specs=[HBM_SPEC] * n_arr,
        scratch_shapes=[pltpu.SemaphoreType.DMA((N_CHIP, n_arr)), pltpu.SemaphoreType.DMA((N_CHIP, n_arr))],
    )(*parts)


def _exchange_all_sems(n_arr):
    return [pltpu.SemaphoreType.DMA((N_DEV - 1, n_arr)), pltpu.SemaphoreType.DMA((N_DEV - 1, n_arr)),
            pltpu.SemaphoreType.DMA((n_arr,))]


def _exchange_all_phases(in_refs, out_refs, send_sems, recv_sems, local_sems):
    n_arr = len(in_refs)
    x, y, c = lax.axis_index("x"), lax.axis_index("y"), lax.axis_index("c")
    me = 4 * x + 2 * y + c

    def copies():
        out = [pltpu.make_async_copy(in_refs[a].at[me], out_refs[a].at[me], local_sems.at[a]) for a in range(n_arr)]
        for k in range(1, N_DEV):
            px = 1 - x if k & 4 else x
            py = 1 - y if k & 2 else y
            pc = 1 - c if k & 1 else c
            for a in range(n_arr):
                out.append(pltpu.make_async_remote_copy(
                    src_ref=in_refs[a].at[4 * px + 2 * py + pc], dst_ref=out_refs[a].at[me],
                    send_sem=send_sems.at[k - 1, a], recv_sem=recv_sems.at[k - 1, a],
                    device_id=(px, py, pc), device_id_type=MESH_ID))
        return out

    def start():
        for cp in copies():
            cp.start()

    def finish():
        for cp in copies():
            cp.wait()

    return start, finish


def _exchange_chips_sems(n_arr):
    return [pltpu.SemaphoreType.DMA((N_CHIP - 1, n_arr)), pltpu.SemaphoreType.DMA((N_CHIP - 1, n_arr)),
            pltpu.SemaphoreType.DMA((n_arr,))]


def _exchange_chips_phases(in_refs, out_refs, send_sems, recv_sems, local_sems):
    n_arr = len(in_refs)
    x, y, c = lax.axis_index("x"), lax.axis_index("y"), lax.axis_index("c")
    me = 2 * x + y

    def copies():
        out = [pltpu.make_async_copy(in_refs[a].at[me], out_refs[a].at[me], local_sems.at[a]) for a in range(n_arr)]
        for k in range(1, N_CHIP):
            px = 1 - x if k & 2 else x
            py = 1 - y if k & 1 else y
            for a in range(n_arr):
                out.append(pltpu.make_async_remote_copy(
                    src_ref=in_refs[a].at[2 * px + py], dst_ref=out_refs[a].at[me],
                    send_sem=send_sems.at[k - 1, a], recv_sem=recv_sems.at[k - 1, a],
                    device_id=(px, py, c), device_id_type=MESH_ID))
        return out

    def start():
        for cp in copies():
            cp.start()

    def finish():
        for cp in copies():
            cp.wait()

    return start, finish


def _row_tile(r, ccols, blocks):
    cap = max(16, (2 * 1024 * 1024) // (4 * ccols * blocks))
    return _pick(r, cap, 16)


def _pair_add(mine, theirs, my_c, name):
    _, r, ccols = mine.shape
    tr = _row_tile(r, ccols, 1)

    def body(c_ref, a_ref, b_ref, o_ref):
        o_ref[...] = (a_ref[...].astype(F32) + b_ref[...].astype(F32)).astype(o_ref.dtype)

    spec = pl.BlockSpec((None, tr, ccols), lambda q, i, c_ref: (q, i, 0))
    return pl.pallas_call(
        body, name=name,
        grid_spec=pltpu.PrefetchScalarGridSpec(
            num_scalar_prefetch=1, grid=(N_CHIP, r // tr),
            in_specs=[pl.BlockSpec((None, tr, ccols), lambda q, i, c_ref: (2 * q + c_ref[0], i, 0)), spec],
            out_specs=spec),
        out_shape=jax.ShapeDtypeStruct(theirs.shape, theirs.dtype),
        compiler_params=_params("parallel", "parallel"),
    )(my_c, mine, theirs)


def _sum_blocks(parts, name):
    nb, r, ccols = parts.shape
    tr = _row_tile(r, ccols, nb)

    def body(p_ref, o_ref):
        acc = p_ref[0].astype(F32)
        for i in range(1, nb):
            acc = acc + p_ref[i].astype(F32)
        o_ref[...] = acc

    return pl.pallas_call(
        body, name=name, grid=(r // tr,),
        in_specs=[pl.BlockSpec((nb, tr, ccols), lambda i: (0, i, 0))],
        out_specs=pl.BlockSpec((tr, ccols), lambda i: (i, 0)),
        out_shape=jax.ShapeDtypeStruct((r, ccols), F32),
        compiler_params=_params("parallel"),
    )(parts)


@jax.custom_vjp
def replicated(vec):
    return vec


def _replicated_fwd(vec):
    return vec, None


def _replicated_bwd(_, ct):
    return (_sum_blocks(_allgather([ct], "small_grad_allgather")[0], "small_grad_sum"),)


replicated.defvjp(_replicated_fwd, _replicated_bwd)


def _adamw(w, g, m, v, name):
    rows, cols = w.shape
    tr = _pick(rows, 256, 8) if rows % 8 == 0 else rows

    def body(w_ref, g_ref, m_ref, v_ref, d_ref, nm_ref, nv_ref):
        g_ = g_ref[...]
        m_ = ADAM_B1 * m_ref[...] + (1.0 - ADAM_B1) * g_
        v_ = ADAM_B2 * v_ref[...] + (1.0 - ADAM_B2) * jnp.square(g_)
        m_hat = m_ / (1.0 - ADAM_B1 ** ADAM_STEP)
        v_hat = v_ / (1.0 - ADAM_B2 ** ADAM_STEP)
        d_ref[...] = -ADAM_LR * (m_hat / (jnp.sqrt(v_hat) + ADAM_EPS) + ADAM_WD * w_ref[...])
        nm_ref[...] = m_
        nv_ref[...] = v_

    spec = pl.BlockSpec((tr, cols), lambda i: (i, 0))
    return pl.pallas_call(
        body, name=name, grid=(rows // tr,), in_specs=[spec] * 4, out_specs=[spec] * 3,
        out_shape=[jax.ShapeDtypeStruct(w.shape, F32)] * 3, compiler_params=_params("parallel"),
    )(w, g, m, v)


COL_SHARDED = ("w_in", "w_uq", "w_ukv", "w_branch_a", "w_branch_b", "w_up", "w_ple")
EARLY = ("w_in",)
MID = ("w_uq", "w_ukv", "w_branch_a", "w_branch_b", "w_out")
LATE = ("w_up", "w_down", "w_ple_gate", "w_ple")
SMALL = ("attn_pre_norm", "attn_post_norm", "b_gate", "q_a_norm", "kv_a_norm", "mlp_pre_norm", "mlp_post_norm",
         "conv_b", "ple_norm", "sinks")
SMALL_COLS = 128


def _pack_rows(arrays, cols, row_mult):
    flat = jnp.concatenate([a.reshape(-1) for a in arrays])
    pad = (-flat.shape[0]) % (cols * row_mult)
    return jnp.pad(flat, (0, pad)).reshape(-1, cols)


def _unpack_small(vec, shapes):
    flat = vec.reshape(-1)
    out, off = {}, 0
    for name in SMALL:
        n = shapes[name]
        out[name] = flat[off:off + n].reshape(1, n)
        off += n + (-n) % SMALL_COLS
    return out


def _pad_lanes(t, width):
    return jnp.pad(t, [(0, 0)] * (t.ndim - 1) + [(0, width - t.shape[-1])])


def _pad_rows(t, rows):
    return jnp.pad(t, [(0, 0)] * (t.ndim - 2) + [(0, rows - t.shape[-2]), (0, 0)])


FRONT_SIZES = (512, 128, 128, 256, 128)
FRONT_BOUNDS = (0, 512, 640, 768, 1024, 1152, 1280)
PE_LANE = NOPE_DIM


def _arrange_w_in_t(wt):
    k = wt.shape[1]
    n_front = sum(FRONT_SIZES)
    front, kr, gates = wt[:n_front], wt[n_front:n_front + ROPE_DIM], wt[n_front + ROPE_DIM:]
    kr_slab = jnp.concatenate([jnp.zeros((PE_LANE, k), wt.dtype), kr,
                               jnp.zeros((HEAD_PAD - PE_LANE - ROPE_DIM, k), wt.dtype)], axis=0)
    return jnp.concatenate([front, kr_slab], axis=0), gates


def _arrange_w_uq_t(wt):
    k = wt.shape[1]
    return _pad_rows(wt.reshape(B_HEADS, NOPE_DIM + ROPE_DIM, k), HEAD_PAD).reshape(B_HEADS * HEAD_PAD, k)


def _arrange_w_ukv_t(wt):
    k = wt.shape[1]
    w = wt.reshape(B_HEADS, 2, NOPE_DIM, k)
    slabs = [_pad_rows(w[:, part], HEAD_PAD).reshape(B_HEADS * HEAD_PAD, k) for part in range(2)]
    return jnp.concatenate(slabs, axis=0)


def _rope_tables(positions, s):
    pos = positions.reshape(s, 1).astype(F32)

    def angles(dim):
        return pos * ROPE_THETA ** (-(jnp.arange(0, dim, 2, dtype=F32) / dim))

    cos_a, sin_a = jnp.cos(angles(A_HEAD_DIM)), jnp.sin(angles(A_HEAD_DIM))
    zero_a = jnp.zeros_like(sin_a)
    tables_a = [jnp.tile(jnp.concatenate(pair, axis=1), (1, LANES // A_HEAD_DIM))
                for pair in ((cos_a, cos_a), (-sin_a, zero_a), (zero_a, sin_a))]
    cos_b, sin_b = jnp.cos(angles(ROPE_DIM)), jnp.sin(angles(ROPE_DIM))
    zero_b = jnp.zeros_like(sin_b)

    def slab(first, second, fill):
        return jnp.concatenate([jnp.full((s, PE_LANE), fill, F32), first, second,
                                jnp.full((s, HEAD_PAD - PE_LANE - ROPE_DIM), fill, F32)], axis=1)

    tables_b = [slab(cos_b, cos_b, 1.0), slab(-sin_b, zero_b, 0.0), slab(zero_b, sin_b, 0.0)]
    return tables_a + tables_b


def _local_loss(wts, x, p, tables, target):
    s = x.shape[0]
    small_shapes = {n: wts[n].shape[-1] for n in SMALL}
    small_vec = _pack_rows([_pad_lanes(wts[n].reshape(1, -1), small_shapes[n] + (-small_shapes[n]) % SMALL_COLS)
                            for n in SMALL], SMALL_COLS, 8)
    sm = _unpack_small(replicated(small_vec), small_shapes)
    def shard(n):
        return wts[n].T if n in COL_SHARDED else wts[n]

    h1_front, h1_gates, x_res, gathered = prenorm_gather(
        x, sm["attn_pre_norm"], tuple([shard(n) for n in EARLY] + [_pack_rows([wts["conv_w"]], SMALL_COLS, 8)]),
        (BF16,) * len(EARLY) + (F32,))
    big = {n: g.reshape(-1, g.shape[2]) for n, g in zip(EARLY, gathered)}
    ch = wts["conv_w"].shape[1]
    conv_w = gathered[-1].reshape(N_DEV, -1)[:, :CONV_W * ch].reshape(N_DEV, CONV_W, ch)
    conv_w = conv_w.transpose(1, 0, 2).reshape(CONV_W, N_DEV * ch)

    w_front_t, w_gates_t = _arrange_w_in_t(big["w_in"])
    tables_a, tables_b = tables[:3], tables[3:]

    qa, ka, va, cqn, ckvn, kpe = proj_stage(
        "prep", _f_prep, [(h1_front, w_front_t, "nt", "w_front", True, F32)], params=[sm["q_a_norm"], sm["kv_a_norm"]],
        consts=tables, splits=[FRONT_BOUNDS], ts=512, out_dtypes=[BF16, BF16, BF16, BF16, BF16, F32])
    ya, mid = swa_nat(qa, ka, va, sm["sinks"].reshape(-1), tuple(shard(n) for n in MID))
    big.update({n: g.reshape(-1, g.shape[2]) for n, g in zip(MID, mid)})

    (q2,) = proj_stage("qrope", _f_qrope, [(cqn, _arrange_w_uq_t(big["w_uq"]), "nt", "w_uq", True, BF16)],
                       consts=tables_b, ts=512, out_dtypes=[BF16])
    k2, v2 = proj_stage("kv", _f_kv, [(ckvn, _arrange_w_ukv_t(big["w_ukv"]), "nt", "w_ukv", True, BF16)],
                        extra=[kpe], splits=[(0, B_HEADS * HEAD_PAD, 2 * B_HEADS * HEAD_PAD), None], ts=512,
                        out_dtypes=[BF16, BF16])
    yb, late = flash_nat(q2, k2, v2, tuple(shard(n) for n in LATE))
    big.update({n: g.reshape(-1, g.shape[2]) for n, g in zip(LATE, late)})

    (mixed,) = proj_stage(
        "gate", _f_gate, [(h1_gates, w_gates_t, "nt", "w_gates", True, F32),
                          (ya, big["w_branch_a"], "nt", "w_branch_a", True, BF16),
                          (yb, big["w_branch_b"], "nt", "w_branch_b", True, BF16)],
        params=[sm["b_gate"][:, :D_MODEL], sm["b_gate"][:, D_MODEL:]],
        splits=[(0, D_MODEL, 2 * D_MODEL), None, None], out_dtypes=[BF16])
    x1, h2 = proj_stage("post_attn", _f_post, [(mixed, big["w_out"], "nn", "w_out", True, F32)], extra=[x_res],
                        params=[sm["attn_post_norm"], sm["mlp_pre_norm"]], ts=512, out_dtypes=[F32, BF16])

    act = mlp_up(h2, big["w_up"], conv_w, sm["conv_b"])
    x2, h3 = proj_stage("post_mlp", _f_post, [(act, big["w_down"], "nn", "w_down", True, F32)], extra=[x1],
                        params=[sm["mlp_post_norm"], sm["ple_norm"]], ts=512, out_dtypes=[F32, BF16])

    (rowloss,) = proj_stage("loss", _f_out, [(h3, big["w_ple_gate"], "nn", "w_ple_gate", True, F32),
                                             (p, big["w_ple"], "nt", "w_ple", False, BF16)], extra=[x2],
                            consts=[target], ts=512)
    return jnp.sum(rowloss)


WEIGHTS = ["attn_pre_norm", "attn_post_norm", "w_in", "b_gate", "sinks", "q_a_norm", "w_uq", "kv_a_norm", "w_ukv",
           "w_branch_a", "w_branch_b", "w_out", "mlp_pre_norm", "mlp_post_norm", "w_up", "conv_w", "conv_b",
           "w_down", "ple_norm", "w_ple_gate", "w_ple"]


def kernel(x, p, positions, attn_pre_norm, attn_post_norm, w_in, b_gate, sinks, q_a_norm, w_uq, kv_a_norm, w_ukv, w_branch_a, w_branch_b, w_out, mlp_pre_norm, mlp_post_norm, w_up, conv_w, conv_b, w_down, ple_norm, w_ple_gate, w_ple, loss_target, m_attn_pre_norm, m_attn_post_norm, m_w_in, m_b_gate, m_sinks, m_q_a_norm, m_w_uq, m_kv_a_norm, m_w_ukv, m_w_branch_a, m_w_branch_b, m_w_out, m_mlp_pre_norm, m_mlp_post_norm, m_w_up, m_conv_w, m_conv_b, m_w_down, m_ple_norm, m_w_ple_gate, m_w_ple, v_attn_pre_norm, v_attn_post_norm, v_w_in, v_b_gate, v_sinks, v_q_a_norm, v_w_uq, v_kv_a_norm, v_w_ukv, v_w_branch_a, v_w_branch_b, v_w_out, v_mlp_pre_norm, v_mlp_post_norm, v_w_up, v_conv_w, v_conv_b, v_w_down, v_ple_norm, v_w_ple_gate, v_w_ple):
    given = dict(locals())
    s = x.shape[1]
    wts = {n: given[n][0] if given[n].ndim == 3 else given[n] for n in WEIGHTS}
    tables = _rope_tables(positions, s)
    local_loss, (grads, grad_x) = jax.value_and_grad(_local_loss, argnums=(0, 1))(
        wts, x[0], p[0, 0], tables, loss_target[0])
    loss = lax.psum(local_loss, AXES)

    outs = {"grad": [], "delta": [], "m": [], "v": []}
    for n in WEIGHTS:
        shape = given[n].shape
        w2 = wts[n].reshape(-1, shape[-1])
        g2 = grads[n].reshape(w2.shape)
        delta, new_m, new_v = _adamw(w2, g2, given["m_" + n].reshape(w2.shape), given["v_" + n].reshape(w2.shape),
                                     "adamw_" + n)
        outs["grad"].append(g2.reshape(shape))
        outs["delta"].append(delta.reshape(shape))
        outs["m"].append(new_m.reshape(shape))
        outs["v"].append(new_v.reshape(shape))
    return (loss, grad_x[None], *outs["grad"], *outs["delta"], *outs["m"], *outs["v"])
```

```python
import functools

import numpy as np
import jax
import jax.numpy as jnp
from jax import lax
from jax.experimental import pallas as pl
from jax.experimental.pallas import tpu as pltpu

F32 = jnp.float32
BF16 = jnp.bfloat16
MESH_ID = pl.DeviceIdType.MESH
AXES = ("x", "y", "c")
N_DEV = 8

D_MODEL = 1024
RMS_EPS = 1e-6
ROPE_THETA = 10000.0
SWA_BLOCK = 128
A_HEADS, A_KV_HEADS, A_HEAD_DIM = 8, 2, 64
A_GROUP = A_HEADS // A_KV_HEADS
B_HEADS, Q_LORA, KV_LORA, NOPE_DIM, ROPE_DIM, V_DIM = 8, 256, 128, 64, 32, 64
D_FF = 2816
CONV_W = 3
HEAD_PAD = 128

ADAM_LR, ADAM_B1, ADAM_B2, ADAM_EPS, ADAM_WD, ADAM_STEP = 0.001, 0.9, 0.999, 1e-08, 0.01, 10

VMEM_LIMIT = 48 * 1024 * 1024
MM_TM, MM_TN, MM_TK_TOKENS = 1024, 1408, 2048
MM_VMEM_BUDGET = 36 * 1024 * 1024
FLASH_T = 1024
CONV_TS = 256
CONV_CHUNK = 256


def _params(*sem):
    return pltpu.CompilerParams(dimension_semantics=sem, vmem_limit_bytes=VMEM_LIMIT)


def _pick(dim, cap, mult):
    best = None
    for t in range(mult, min(dim, cap) + 1, mult):
        if dim % t == 0:
            best = t
    return dim if best is None else best


def _divisors(dim, mult):
    return [t for t in range(mult, dim + 1, mult) if dim % t == 0] or [dim]


def _matmul_tiles(m, n, kdim, form, sizes):
    sa, sb, so = sizes
    tk = _pick(kdim, MM_TK_TOKENS, 128) if form == "tn" else kdim
    cap_m = MM_TN if form == "tn" else MM_TM
    best = None
    for tm in _divisors(m, 128):
        for tn in _divisors(n, 128):
            need = 2 * (tm * tk * sa + tk * tn * sb + tm * tn * so) + (tm * tn * 4 if tk != kdim else 0)
            if tm > cap_m or tn > MM_TN or need > MM_VMEM_BUDGET:
                continue
            if best is None or (tm * tn, tm) > (best[0] * best[1], best[0]):
                best = (tm, tn)
    return best[0], best[1], tk


def _matmul(a, b, form, *, out_dtype=F32, name):
    if form == "tn":
        (kdim, m), n = a.shape, b.shape[1]
    else:
        (m, kdim), n = a.shape, (b.shape[1] if form == "nn" else b.shape[0])
    sizes = (a.dtype.itemsize, b.dtype.itemsize, jnp.dtype(out_dtype).itemsize)
    tm, tn, tk = _matmul_tiles(m, n, kdim, form, sizes)
    nk = kdim // tk
    rows_outer = nk > 1 or (m // tm) * b.size * sizes[1] <= (n // tn) * a.size * sizes[0]

    def ij(fn):
        return (lambda i, j, k: fn(i, j, k)) if rows_outer else (lambda j, i, k: fn(i, j, k))

    a_spec = (pl.BlockSpec((tk, tm), ij(lambda i, j, k: (k, i))) if form == "tn"
              else pl.BlockSpec((tm, tk), ij(lambda i, j, k: (i, k))))
    b_spec = (pl.BlockSpec((tn, tk), ij(lambda i, j, k: (j, k))) if form == "nt"
              else pl.BlockSpec((tk, tn), ij(lambda i, j, k: (k, j))))
    dims = (((0 if form == "tn" else 1,), (1 if form == "nt" else 0,)), ((), ()))

    def product(a_ref, b_ref):
        return lax.dot_general(a_ref[...].astype(BF16), b_ref[...].astype(BF16), dims, preferred_element_type=F32)

    if nk == 1:
        def body(a_ref, b_ref, o_ref):
            o_ref[...] = product(a_ref, b_ref).astype(o_ref.dtype)

        scratch = []
    else:
        def body(a_ref, b_ref, o_ref, acc_ref):
            k = pl.program_id(2)

            @pl.when(k == 0)
            def _():
                acc_ref[...] = jnp.zeros_like(acc_ref)

            acc_ref[...] += product(a_ref, b_ref)

            @pl.when(k == nk - 1)
            def _():
                o_ref[...] = acc_ref[...].astype(o_ref.dtype)

        scratch = [pltpu.VMEM((tm, tn), F32)]

    return pl.pallas_call(
        body, name=name, grid=(m // tm, n // tn, nk) if rows_outer else (n // tn, m // tm, nk),
        in_specs=[a_spec, b_spec],
        out_specs=pl.BlockSpec((tm, tn), ij(lambda i, j, k: (i, j))),
        out_shape=jax.ShapeDtypeStruct((m, n), out_dtype),
        scratch_shapes=scratch,
        compiler_params=_params("parallel", "parallel", "arbitrary"),
    )(a, b)


def _pairs(bounds):
    return list(zip(bounds[:-1], bounds[1:]))


def _split(v, bounds):
    return [v[:, a:b] for a, b in _pairs(bounds)]


def _stage_build(name, f, tiled, params, consts, splits, ts, out_dtypes, ct_dtypes=None):
    n_t, n_p, n_c = len(tiled), len(params), len(consts)
    ct_dtypes = [t.dtype for t in tiled] if ct_dtypes is None else ct_dtypes
    s = tiled[0].shape[0]
    ts = min(ts, s)
    grid = (s // ts,)
    if splits is None:
        splits = [None] * n_t
    in_bounds = [(0, t.shape[1]) if b is None else tuple(b) for t, b in zip(tiled, splits)]

    def tile_aval(arr):
        return jax.ShapeDtypeStruct((ts, arr.shape[1]), arr.dtype)

    slab_avals = [[jax.ShapeDtypeStruct((ts, e - a), F32) for a, e in _pairs(b)]
                  for t, b in zip(tiled, in_bounds)]
    out_avals = jax.eval_shape(f, slab_avals, list(params), [tile_aval(c) for c in consts])
    out_bounds = [tuple(np.cumsum([0] + [o.shape[1] for o in slabs]).tolist()) for slabs in out_avals]
    out_dtypes = [F32] * len(out_bounds) if out_dtypes is None else out_dtypes
    out_shapes = [jax.ShapeDtypeStruct((s, b[-1]), d) for b, d in zip(out_bounds, out_dtypes)]

    def row_spec(width):
        return pl.BlockSpec((ts, width), lambda i: (i, 0))

    def par_spec(arr):
        return pl.BlockSpec(arr.shape, lambda i: (0, 0))

    in_specs = ([row_spec(t.shape[1]) for t in tiled] + [par_spec(p) for p in params]
                + [row_spec(c.shape[1]) for c in consts])

    def load(refs):
        t = [_split(r[...].astype(F32), b) for r, b in zip(refs[:n_t], in_bounds)]
        p = [r[...] for r in refs[n_t:n_t + n_p]]
        c = [r[...] for r in refs[n_t + n_p:n_t + n_p + n_c]]
        return t, p, c

    def store(refs, values, bounds):
        for ref, slabs, b in zip(refs, values, bounds):
            for v, (a, e) in zip(slabs, _pairs(b)):
                ref[:, a:e] = v.astype(ref.dtype)

    def run_fwd(tiled, params, consts):
        def body(*refs):
            t, p, c = load(refs)
            store(refs[n_t + n_p + n_c:], f(t, p, c), out_bounds)

        return pl.pallas_call(
            body, name=name + "_fwd", grid=grid, in_specs=in_specs,
            out_specs=[row_spec(b[-1]) for b in out_bounds], out_shape=out_shapes,
            compiler_params=_params("parallel"),
        )(*tiled, *params, *consts)

    def run_bwd(tiled, params, consts, cts):
        n_in = n_t + n_p + n_c
        n_o = len(out_bounds)

        def body(*refs):
            t, p, c = load(refs)
            g = [_split(r[...].astype(F32), b) for r, b in zip(refs[n_in:n_in + n_o], out_bounds)]
            _, pull = jax.vjp(lambda t_, p_: f(t_, p_, c), t, p)
            dt, dp = pull(g)
            store(refs[n_in + n_o:n_in + n_o + n_t], dt, in_bounds)
            first = pl.program_id(0) == 0
            for ref, d in zip(refs[n_in + n_o + n_t:], dp):
                @pl.when(first)
                def _(ref=ref):
                    ref[...] = jnp.zeros_like(ref)

                ref[...] += d

        res = pl.pallas_call(
            body, name=name + "_bwd", grid=grid,
            in_specs=in_specs + [row_spec(b[-1]) for b in out_bounds],
            out_specs=[row_spec(t.shape[1]) for t in tiled] + [par_spec(p) for p in params],
            out_shape=[jax.ShapeDtypeStruct(t.shape, d) for t, d in zip(tiled, ct_dtypes)]
                      + [jax.ShapeDtypeStruct(p.shape, F32) for p in params],
            compiler_params=_params("arbitrary"),
        )(*tiled, *params, *consts, *cts)
        return tuple(res[:n_t]), tuple(res[n_t:])

    return run_fwd, run_bwd


def proj_stage(name, f, projections, extra=(), params=(), consts=(), splits=None, ts=256, out_dtypes=None):
    n_z = len(projections)
    forms = [pr[2] for pr in projections]
    names = [pr[3] for pr in projections]
    need_da = [pr[4] for pr in projections]
    store = [pr[5] for pr in projections]
    extra, params, consts = tuple(extra), tuple(params), tuple(consts)

    def matmuls(a_list, w_list):
        return tuple(_matmul(a, w, form, out_dtype=dt, name=n + "_fwd")
                     for a, w, form, n, dt in zip(a_list, w_list, forms, names, store))

    def build(zs, ct=False):
        ct_dtypes = [BF16] * n_z + [e.dtype for e in extra] if ct else None
        return _stage_build(name, f, tuple(zs) + extra, params, consts, splits, ts, out_dtypes, ct_dtypes)

    @jax.custom_vjp
    def op(a_list, w_list, extra, params, consts):
        zs = matmuls(a_list, w_list)
        return tuple(build(zs)[0](zs + extra, params, consts))

    def op_fwd(a_list, w_list, extra, params, consts):
        zs = matmuls(a_list, w_list)
        return tuple(build(zs)[0](zs + extra, params, consts)), (a_list, w_list, zs, extra, params, consts)

    def op_bwd(res, cts):
        a_list, w_list, zs, extra, params, consts = res
        dt, dp = build(zs, ct=True)[1](zs + extra, params, consts, cts)
        da_list, dw_list = [], []
        for a, w, dz, form, n, want in zip(a_list, w_list, dt[:n_z], forms, names, need_da):
            if form == "nn":
                da = _matmul(dz, w, "nt", out_dtype=a.dtype, name=n + "_da") if want else jnp.zeros_like(a)
                dw = _matmul(a, dz, "tn", out_dtype=w.dtype, name=n + "_dw")
            else:
                da = _matmul(dz, w, "nn", out_dtype=a.dtype, name=n + "_da") if want else jnp.zeros_like(a)
                dw = _matmul(dz, a, "tn", out_dtype=w.dtype, name=n + "_dw")
            da_list.append(da)
            dw_list.append(dw)
        return tuple(da_list), tuple(dw_list), tuple(dt[n_z:]), dp, tuple(jnp.zeros_like(c) for c in consts)

    op.defvjp(op_fwd, op_bwd)
    return op(tuple(pr[0] for pr in projections), tuple(pr[1] for pr in projections), extra, params, consts)


def _rms(t, g):
    return t * lax.rsqrt(jnp.mean(t * t, axis=-1, keepdims=True) + RMS_EPS) * g


@functools.partial(jax.custom_vjp, nondiff_argnums=(1,))
def _lane_roll(t, shift):
    return pltpu.roll(t, shift % t.shape[-1], t.ndim - 1)


def _lane_roll_fwd(t, shift):
    return _lane_roll(t, shift), None


def _lane_roll_bwd(shift, _, ct):
    return (pltpu.roll(ct, (-shift) % ct.shape[-1], ct.ndim - 1),)


_lane_roll.defvjp(_lane_roll_fwd, _lane_roll_bwd)


def _rope_lanes(t, tables, half):
    reps = t.shape[1] // tables[0].shape[1]
    c, s_lo, s_hi = [jnp.concatenate([tb] * reps, axis=1) if reps > 1 else tb for tb in tables]
    return t * c + _lane_roll(t, -half) * s_lo + _lane_roll(t, half) * s_hi


PRENORM_TS = 256


def _prenorm_fwd_call(x, g, shards):
    s, width = x.shape
    ts = min(PRENORM_TS, s)
    nt = s // ts
    n_arr = len(shards)

    def body(*refs):
        x_ref, g_ref = refs[:2]
        o_ref = refs[2 + n_arr]
        i = pl.program_id(0)
        ag_start, ag_forward, ag_finish = _allgather_phases(refs[2:2 + n_arr], refs[3 + n_arr:3 + 2 * n_arr],
                                                            *refs[3 + 2 * n_arr:])

        @pl.when(i == 0)
        def _():
            ag_start()

        @pl.when(i == nt // 2)
        def _():
            ag_forward()

        o_ref[...] = _rms(x_ref[...], g_ref[...]).astype(o_ref.dtype)

        @pl.when(i == nt - 1)
        def _():
            ag_finish()

    return pl.pallas_call(
        body, name="prenorm_fwd", grid=(nt,),
        in_specs=[pl.BlockSpec((ts, width), lambda i: (i, 0)), pl.BlockSpec(g.shape, lambda i: (0, 0))]
                 + [HBM_SPEC] * n_arr,
        out_specs=[pl.BlockSpec((ts, width), lambda i: (i, 0))] + [HBM_SPEC] * n_arr,
        out_shape=[jax.ShapeDtypeStruct(x.shape, BF16)] + _allgather_out_shapes(shards),
        scratch_shapes=_allgather_sems(n_arr),
        compiler_params=_params("arbitrary"),
    )(x, g, *shards)


def _prenorm_bwd_call(x, g, dh_a, dh_b, dx_res, parts):
    s, width = x.shape
    ts = min(PRENORM_TS, s)
    nt = s // ts
    n_arr = len(parts)

    def body(*refs):
        x_ref, g_ref, dha_ref, dhb_ref, dxr_ref = refs[:5]
        dx_ref, dg_ref = refs[5 + n_arr:7 + n_arr]
        i = pl.program_id(0)
        exchange_start, exchange_finish = _exchange_chips_phases(
            refs[5:5 + n_arr], refs[7 + n_arr:7 + 2 * n_arr], *refs[7 + 2 * n_arr:])

        @pl.when(i == 0)
        def _():
            exchange_start()
            dg_ref[...] = jnp.zeros_like(dg_ref)

        _, pull = jax.vjp(_rms, x_ref[...], g_ref[...])
        dx, dg = pull(dha_ref[...].astype(F32) + dhb_ref[...].astype(F32))
        dx_ref[...] = dx + dxr_ref[...]
        dg_ref[...] += dg

        @pl.when(i == nt - 1)
        def _():
            exchange_finish()

    row = pl.BlockSpec((ts, width), lambda i: (i, 0))
    par = pl.BlockSpec(g.shape, lambda i: (0, 0))
    return pl.pallas_call(
        body, name="prenorm_bwd", grid=(nt,),
        in_specs=[row, par, row, row, row] + [HBM_SPEC] * n_arr,
        out_specs=[row, par] + [HBM_SPEC] * n_arr,
        out_shape=[jax.ShapeDtypeStruct(x.shape, F32), jax.ShapeDtypeStruct(g.shape, F32)]
                  + [jax.ShapeDtypeStruct(p.shape, p.dtype) for p in parts],
        scratch_shapes=_exchange_chips_sems(n_arr),
        compiler_params=_params("arbitrary"),
    )(x, g, dh_a, dh_b, dx_res, *parts)


@functools.partial(jax.custom_vjp, nondiff_argnums=(3,))
def prenorm_gather(x, g, shards, wire_dtypes):
    out = _prenorm_fwd_call(x, g, [s.astype(d) for s, d in zip(shards, wire_dtypes)])
    return out[0], out[0], x, tuple(out[1:])


def _prenorm_gather_fwd(x, g, shards, wire_dtypes):
    return prenorm_gather(x, g, shards, wire_dtypes), (x, g)


def _prenorm_gather_bwd(wire_dtypes, res, cts):
    x, g = res
    dh_a, dh_b, dx_res, d_gathered = cts
    out = _prenorm_bwd_call(x, g, dh_a, dh_b, dx_res, _reduce_scatter_head(d_gathered, "grads"))
    return out[0], out[1], _reduce_scatter_tail(out[2:], "grads")


prenorm_gather.defvjp(_prenorm_gather_fwd, _prenorm_gather_bwd)


def _f_prep(t, p, c):
    qa, ka, va, cq, ckv, kr = t[0]
    return [[_rope_lanes(qa, c[0:3], A_HEAD_DIM // 2)], [_rope_lanes(ka, c[0:3], A_HEAD_DIM // 2)], [va],
            [_rms(cq, p[0])], [_rms(ckv, p[1])], [_rope_lanes(kr, c[3:6], ROPE_DIM // 2)]]


def _f_qrope(t, p, c):
    return [[_rope_lanes(t[0][0], c, ROPE_DIM // 2)]]


def _f_kv(t, p, c):
    (k_nope, v), (k_pe,) = t
    return [[k_nope + jnp.concatenate([k_pe] * B_HEADS, axis=1)], [v]]


def _f_gate(t, p, c):
    (ga, gb), (pa,), (pb,) = t
    ba, bb = p
    return [[jax.nn.sigmoid(ga + ba) * pa + jax.nn.sigmoid(gb + bb) * pb]]


def _f_post(t, p, c):
    (branch,), (residual,) = t
    x1 = residual + _rms(branch, p[0])
    return [[x1], [_rms(x1, p[1])]]


def _f_out(t, p, c):
    (gate,), (emb,), (x2,) = t
    y = x2 + jax.nn.sigmoid(gate) * emb
    err = y - c[0]
    return [[0.5 * jnp.mean(err * err, axis=-1, keepdims=True)]]


def _shift_down(cur, prev, has_prev):
    full = jnp.concatenate([prev * has_prev, cur], axis=0)
    return pltpu.roll(full, 1, 0)[HALO:], pltpu.roll(full, 2, 0)[HALO:]


GELU_C = float(np.sqrt(2.0 / np.pi))
GELU_A = 0.044715
HALO = 8


def _gelu_tanh(x):
    x2 = x * x
    th = jnp.tanh(x * (GELU_C + (GELU_C * GELU_A) * x2))
    half = 0.5 + 0.5 * th
    return x * half, half + x * (0.5 - 0.5 * (th * th)) * (GELU_C + (3.0 * GELU_C * GELU_A) * x2)


def _row_sum(t):
    return jnp.sum(t, axis=0, keepdims=True)


def _conv3(cur, prev, w_ref, b_ref, has_prev):
    u1, u2 = _shift_down(cur, prev, has_prev)
    return w_ref[2:3, :] * cur + w_ref[1:2, :] * u1 + w_ref[0:1, :] * u2 + b_ref[...], u1, u2


def _mlp_act_specs(s):
    ts = min(CONV_TS, s)
    hb = ts // HALO

    def half_specs(h):
        return [pl.BlockSpec((ts, D_FF), lambda i: (i, h)),
                pl.BlockSpec((HALO, D_FF), lambda i: (jnp.maximum(i * hb - 1, 0), h))]

    def par_specs(h):
        return [pl.BlockSpec((CONV_W, D_FF), lambda i: (0, h)), pl.BlockSpec((1, D_FF), lambda i: (0, h))]

    return ts, hb, half_specs, par_specs


def _mlp_act_fwd_call(up, conv_w, conv_b):
    s = up.shape[0]
    ts, hb, half_specs, par_specs = _mlp_act_specs(s)

    def body(g_ref, gp_ref, v_ref, vp_ref, wg_ref, bg_ref, wv_ref, bv_ref, o_ref):
        has_prev = (pl.program_id(0) > 0).astype(F32)

        def chunk(cidx, carry):
            cols = pl.ds(pl.multiple_of(cidx * CONV_CHUNK, CONV_CHUNK), CONV_CHUNK)
            u_g, _, _ = _conv3(g_ref[:, cols], gp_ref[:, cols], wg_ref.at[:, cols], bg_ref.at[:, cols], has_prev)
            u_v, _, _ = _conv3(v_ref[:, cols], vp_ref[:, cols], wv_ref.at[:, cols], bv_ref.at[:, cols], has_prev)
            o_ref[:, cols] = (_gelu_tanh(u_g)[0] * u_v).astype(o_ref.dtype)
            return carry

        lax.fori_loop(0, D_FF // CONV_CHUNK, chunk, 0)

    return pl.pallas_call(
        body, name="mlp_act_fwd", grid=(s // ts,),
        in_specs=half_specs(0) + half_specs(1) + par_specs(0) + par_specs(1),
        out_specs=pl.BlockSpec((ts, D_FF), lambda i: (i, 0)),
        out_shape=jax.ShapeDtypeStruct((s, D_FF), BF16),
        compiler_params=_params("parallel"),
    )(up, up, up, up, conv_w, conv_b, conv_w, conv_b)


def _mlp_act_bwd_call(up, conv_w, conv_b, dact):
    s = up.shape[0]
    ts, hb, half_specs, par_specs = _mlp_act_specs(s)
    nt = s // ts
    ext = ts + HALO
    bf16_rows = 2 * HALO

    def next_spec(rows, h):
        return pl.BlockSpec((rows, D_FF), lambda i: (jnp.minimum((i + 1) * (ts // rows), s // rows - 1), h))

    def body(g_ref, gp_ref, gn_ref, v_ref, vp_ref, vn_ref, wg_ref, bg_ref, wv_ref, bv_ref, da_ref, dan_ref,
             dup_ref, dwg_ref, dbg_ref, dwv_ref, dbv_ref):
        i = pl.program_id(0)
        has_prev, has_next = (i > 0).astype(F32), (i < nt - 1).astype(F32)

        @pl.when(i == 0)
        def _():
            for ref in (dwg_ref, dbg_ref, dwv_ref, dbv_ref):
                ref[...] = jnp.zeros_like(ref)

        def chunk(cidx, carry):
            cols = pl.ds(pl.multiple_of(cidx * CONV_CHUNK, CONV_CHUNK), CONV_CHUNK)
            g_ext = jnp.concatenate([g_ref[:, cols], gn_ref[:, cols]], axis=0)
            v_ext = jnp.concatenate([v_ref[:, cols], vn_ref[:, cols]], axis=0)
            u_g, g1, g2 = _conv3(g_ext, gp_ref[:, cols], wg_ref.at[:, cols], bg_ref.at[:, cols], has_prev)
            u_v, v1, v2 = _conv3(v_ext, vp_ref[:, cols], wv_ref.at[:, cols], bv_ref.at[:, cols], has_prev)
            da_ext = jnp.concatenate([da_ref[:, cols].astype(F32),
                                      dan_ref[:, cols].astype(F32)[0:HALO] * has_next], axis=0)
            act_g, dact_g = _gelu_tanh(u_g)
            du_g = da_ext * u_v * dact_g
            du_v = da_ext * act_g
            for du, w_ref, x0, x1, x2, dw_ref, db_ref, lo in ((du_g, wg_ref, g_ext, g1, g2, dwg_ref, dbg_ref, 0),
                                                          (du_v, wv_ref, v_ext, v1, v2, dwv_ref, dbv_ref, D_FF)):
                d1 = pltpu.roll(du, ext - 1, 0)
                d2 = pltpu.roll(du, ext - 2, 0)
                dup = w_ref[2:3, cols] * du + w_ref[1:2, cols] * d1 + w_ref[0:1, cols] * d2
                out_cols = pl.ds(pl.multiple_of(lo + cidx * CONV_CHUNK, CONV_CHUNK), CONV_CHUNK)
                dup_ref[:, out_cols] = dup[0:ts].astype(dup_ref.dtype)
                own = du[0:ts]
                dw_ref[0:1, cols] += _row_sum(own * x2[0:ts])
                dw_ref[1:2, cols] += _row_sum(own * x1[0:ts])
                dw_ref[2:3, cols] += _row_sum(own * x0[0:ts])
                db_ref[:, cols] += _row_sum(own)
            return carry

        lax.fori_loop(0, D_FF // CONV_CHUNK, chunk, 0)

    par_out = [pl.BlockSpec((CONV_W, D_FF), lambda i: (0, 0)), pl.BlockSpec((1, D_FF), lambda i: (0, 0))]
    par_shapes = [jax.ShapeDtypeStruct((CONV_W, D_FF), F32), jax.ShapeDtypeStruct((1, D_FF), F32)]
    return pl.pallas_call(
        body, name="mlp_act_bwd", grid=(nt,),
        in_specs=(half_specs(0) + [next_spec(HALO, 0)] + half_specs(1) + [next_spec(HALO, 1)]
                  + par_specs(0) + par_specs(1)
                  + [pl.BlockSpec((ts, D_FF), lambda i: (i, 0)), next_spec(bf16_rows, 0)]),
        out_specs=[pl.BlockSpec((ts, 2 * D_FF), lambda i: (i, 0))] + par_out + par_out,
        out_shape=[jax.ShapeDtypeStruct((s, 2 * D_FF), BF16)] + par_shapes + par_shapes,
        compiler_params=_params("arbitrary"),
    )(up, up, up, up, up, up, conv_w, conv_b, conv_w, conv_b, dact, dact)


@jax.custom_vjp
def mlp_up(h2, w_up_t, conv_w, conv_b):
    return _mlp_act_fwd_call(_matmul(h2, w_up_t, "nt", out_dtype=F32, name="w_up_fwd"), conv_w, conv_b)


def _mlp_up_fwd(h2, w_up_t, conv_w, conv_b):
    up = _matmul(h2, w_up_t, "nt", out_dtype=F32, name="w_up_fwd")
    return _mlp_act_fwd_call(up, conv_w, conv_b), (h2, w_up_t, up, conv_w, conv_b)


def _mlp_up_bwd(res, dact):
    h2, w_up_t, up, conv_w, conv_b = res
    dup, dwg, dbg, dwv, dbv = _mlp_act_bwd_call(up, conv_w, conv_b, dact)
    dh2 = _matmul(dup, w_up_t, "nn", out_dtype=h2.dtype, name="w_up_da")
    dw = _matmul(dup, h2, "tn", out_dtype=w_up_t.dtype, name="w_up_dw")
    return dh2, dw, jnp.concatenate([dwg, dwv], axis=1), jnp.concatenate([dbg, dbv], axis=1)


mlp_up.defvjp(_mlp_up_fwd, _mlp_up_bwd)


SWA_ROWS = A_GROUP * SWA_BLOCK


def _swa_sink_rows(sink_ref, g):
    return jnp.concatenate([jnp.full((SWA_BLOCK, 1), sink_ref[g * A_GROUP + h], F32) for h in range(A_GROUP)], axis=0)


def _swa_operands(q_ref, kp_ref, kc_ref, vp_ref, vc_ref, sink_ref):
    groups = []
    for g in range(A_KV_HEADS):
        groups.append((_swa_stack_heads(q_ref, g), _dup_half(kp_ref[...], g), _dup_half(kc_ref[...], g),
                       _dup_half(vp_ref[...], g), _dup_half(vc_ref[...], g)))
    return groups, jnp.concatenate([_swa_sink_rows(sink_ref, g) for g in range(A_KV_HEADS)], axis=0)


def _swa_probs(groups, sink, prev_off):
    scale = A_HEAD_DIM ** -0.5
    sp = jnp.concatenate([lax.dot_general(gr[0], gr[1], NT_DIMS, preferred_element_type=F32) for gr in groups], axis=0)
    sc = jnp.concatenate([lax.dot_general(gr[0], gr[2], NT_DIMS, preferred_element_type=F32) for gr in groups], axis=0)
    qi = lax.broadcasted_iota(jnp.int32, sp.shape, 0) & (SWA_BLOCK - 1)
    kj = lax.broadcasted_iota(jnp.int32, sp.shape, 1)
    in_cur = kj <= qi
    sw = jnp.where(in_cur, sc, jnp.where(kj > qi + prev_off, sp, -jnp.inf)) * scale
    m = jnp.maximum(jnp.max(sw, axis=-1, keepdims=True), sink)
    e, es = jnp.exp(sw - m), jnp.exp(sink - m)
    den = jnp.sum(e, axis=-1, keepdims=True) + es
    return e / den, in_cur, es / den


def _swa_split(t, in_cur):
    cur = jnp.where(in_cur, t, 0.0)
    return t - cur, cur


MLA_SCALE = (NOPE_DIM + ROPE_DIM) ** -0.5
EXP2_SCALE = MLA_SCALE * float(np.log2(np.e))
NT_DIMS = (((1,), (1,)), ((), ()))
TN_DIMS = (((0,), (0,)), ((), ()))


LANES = 128
HALF = LANES // 2


def _low_half(shape):
    return lax.broadcasted_iota(jnp.int32, shape, len(shape) - 1) < HALF


def _dup_half(x, g):
    xf = x.astype(F32)
    keep = _low_half(xf.shape) if g == 0 else jnp.logical_not(_low_half(xf.shape))
    xm = jnp.where(keep, xf, 0.0)
    return (xm + pltpu.roll(xm, HALF, 1)).astype(x.dtype)


def _fold_half(r, g):
    total = r + pltpu.roll(r, HALF, 1)
    keep = _low_half(r.shape) if g == 0 else jnp.logical_not(_low_half(r.shape))
    return jnp.where(keep, total, 0.0)


def _swa_stack_heads(ref, g):
    parts = []
    for tile in range(2):
        slab = ref[:, (2 * g + tile) * LANES:(2 * g + tile + 1) * LANES]
        low = _low_half(slab.shape)
        parts += [jnp.where(low, slab, jnp.zeros_like(slab)), jnp.where(low, jnp.zeros_like(slab), slab)]
    return jnp.concatenate(parts, axis=0)


def _swa_unstack_heads(ref, g, rows):
    for tile in range(2):
        a = rows[(2 * tile) * SWA_BLOCK:(2 * tile + 1) * SWA_BLOCK]
        b = rows[(2 * tile + 1) * SWA_BLOCK:(2 * tile + 2) * SWA_BLOCK]
        ref[:, (2 * g + tile) * LANES:(2 * g + tile + 1) * LANES] = jnp.where(_low_half(a.shape), a, b).astype(ref.dtype)


def _swa_nat_specs():
    blk = SWA_BLOCK
    q_spec = pl.BlockSpec((blk, A_HEADS * A_HEAD_DIM), lambda n: (n, 0))
    prev_spec = pl.BlockSpec((blk, LANES), lambda n: (jnp.maximum(n - 1, 0), 0))
    cur_spec = pl.BlockSpec((blk, LANES), lambda n: (n, 0))
    return q_spec, prev_spec, cur_spec, pl.BlockSpec(memory_space=pltpu.SMEM)


def _swa_nat_fwd_call(q, k, v, sinks, shards):
    s = q.shape[0]
    nblk = s // SWA_BLOCK
    n_arr = len(shards)
    q_spec, prev_spec, cur_spec, sink_spec = _swa_nat_specs()

    def body(*refs):
        q_ref, kp_ref, kc_ref, vp_ref, vc_ref, sink_ref = refs[:6]
        o_ref = refs[6 + n_arr]
        n = pl.program_id(0)
        ag_start, ag_forward, ag_finish = _allgather_phases(refs[6:6 + n_arr], refs[7 + n_arr:7 + 2 * n_arr],
                                                            *refs[7 + 2 * n_arr:])

        @pl.when(n == 0)
        def _():
            ag_start()

        @pl.when(n == (3 * nblk) // 4)
        def _():
            ag_forward()

        prev_off = jnp.where(n > 0, 0, SWA_BLOCK)
        groups, sink = _swa_operands(q_ref, kp_ref, kc_ref, vp_ref, vc_ref, sink_ref)
        p, in_cur, _ = _swa_probs(groups, sink, prev_off)
        ppb, pcb = [t.astype(BF16) for t in _swa_split(p, in_cur)]
        for g, (_, _, _, vp, vc) in enumerate(groups):
            rows = slice(g * SWA_ROWS, (g + 1) * SWA_ROWS)
            out = (jnp.dot(ppb[rows], vp, preferred_element_type=F32)
                   + jnp.dot(pcb[rows], vc, preferred_element_type=F32))
            _swa_unstack_heads(o_ref, g, out)

        @pl.when(n == nblk - 1)
        def _():
            ag_finish()

    return pl.pallas_call(
        body, name="swa_fwd", grid=(nblk,),
        in_specs=[q_spec, prev_spec, cur_spec, prev_spec, cur_spec, sink_spec] + [HBM_SPEC] * n_arr,
        out_specs=[q_spec] + [HBM_SPEC] * n_arr,
        out_shape=[jax.ShapeDtypeStruct(q.shape, BF16)] + _allgather_out_shapes(shards),
        scratch_shapes=_allgather_sems(n_arr),
        compiler_params=_params("arbitrary"),
    )(q, k, k, v, v, sinks, *shards)


def _swa_nat_bwd_call(q, k, v, sinks, do, parts):
    s = q.shape[0]
    nblk = s // SWA_BLOCK
    n_arr = len(parts)
    q_spec, prev_spec, cur_spec, sink_spec = _swa_nat_specs()
    scale = A_HEAD_DIM ** -0.5
    dsink_spec = pl.BlockSpec((A_KV_HEADS, SWA_ROWS, 1), lambda n: (0, 0, 0))

    def body(*refs):
        q_ref, kp_ref, kc_ref, vp_ref, vc_ref, sink_ref, do_ref = refs[:7]
        dq_ref, dkp_ref, dkc_ref, dvp_ref, dvc_ref, dsink_ref = refs[7 + n_arr:13 + n_arr]
        n = pl.program_id(0)
        exchange_start, exchange_finish = _exchange_all_phases(
            refs[7:7 + n_arr], refs[13 + n_arr:13 + 2 * n_arr], *refs[13 + 2 * n_arr:])

        @pl.when(n == 0)
        def _():
            exchange_start()
        prev_off = jnp.where(n > 0, 0, SWA_BLOCK)

        @pl.when(n == 0)
        def _():
            dsink_ref[...] = jnp.zeros_like(dsink_ref)

        groups, sink = _swa_operands(q_ref, kp_ref, kc_ref, vp_ref, vc_ref, sink_ref)
        dobs = [_swa_stack_heads(do_ref, g) for g in range(A_KV_HEADS)]
        p, in_cur, ps = _swa_probs(groups, sink, prev_off)
        ppb, pcb = [t.astype(BF16) for t in _swa_split(p, in_cur)]

        def per_group(fn):
            return jnp.concatenate([fn(g, slice(g * SWA_ROWS, (g + 1) * SWA_ROWS)) for g in range(A_KV_HEADS)], axis=0)

        out = per_group(lambda g, rows: jnp.dot(ppb[rows], groups[g][3], preferred_element_type=F32)
                        + jnp.dot(pcb[rows], groups[g][4], preferred_element_type=F32))
        delta = jnp.sum(jnp.concatenate(dobs, axis=0).astype(F32) * out, axis=-1, keepdims=True)
        dp = jnp.where(in_cur,
                       per_group(lambda g, rows: lax.dot_general(dobs[g], groups[g][4], NT_DIMS,
                                                                 preferred_element_type=F32)),
                       per_group(lambda g, rows: lax.dot_general(dobs[g], groups[g][3], NT_DIMS,
                                                                 preferred_element_type=F32)))
        dsp, dsc = [t.astype(BF16) for t in _swa_split(p * (dp - delta), in_cur)]
        dsink_ref[...] += (-ps * delta).reshape(dsink_ref.shape)
        totals = [jnp.zeros((SWA_BLOCK, LANES), F32) for _ in range(4)]
        for g, (qb, kp, kc, _, _) in enumerate(groups):
            rows = slice(g * SWA_ROWS, (g + 1) * SWA_ROWS)
            dq = (jnp.dot(dsp[rows], kp, preferred_element_type=F32)
                  + jnp.dot(dsc[rows], kc, preferred_element_type=F32)) * scale
            _swa_unstack_heads(dq_ref, g, dq)
            pieces = [lax.dot_general(dsp[rows], qb, TN_DIMS, preferred_element_type=F32) * scale,
                      lax.dot_general(dsc[rows], qb, TN_DIMS, preferred_element_type=F32) * scale,
                      lax.dot_general(ppb[rows], dobs[g], TN_DIMS, preferred_element_type=F32),
                      lax.dot_general(pcb[rows], dobs[g], TN_DIMS, preferred_element_type=F32)]
            totals = [tot + _fold_half(r, g) for tot, r in zip(totals, pieces)]
        dkp_ref[...], dkc_ref[...], dvp_ref[...], dvc_ref[...] = totals

        @pl.when(n == nblk - 1)
        def _():
            exchange_finish()

    kv_shape = jax.ShapeDtypeStruct(k.shape, F32)
    return pl.pallas_call(
        body, name="swa_bwd", grid=(nblk,),
        in_specs=[q_spec, prev_spec, cur_spec, prev_spec, cur_spec, sink_spec, q_spec] + [HBM_SPEC] * n_arr,
        out_specs=[q_spec, cur_spec, cur_spec, cur_spec, cur_spec, dsink_spec] + [HBM_SPEC] * n_arr,
        out_shape=[jax.ShapeDtypeStruct(q.shape, q.dtype), kv_shape, kv_shape, kv_shape, kv_shape,
                   jax.ShapeDtypeStruct((A_KV_HEADS, SWA_ROWS, 1), F32)]
                  + [jax.ShapeDtypeStruct(p.shape, p.dtype) for p in parts],
        scratch_shapes=_exchange_all_sems(n_arr),
        compiler_params=_params("arbitrary"),
    )(q, k, k, v, v, sinks, do, *parts)


@jax.custom_vjp
def swa_nat(q, k, v, sinks, shards):
    out = _swa_nat_fwd_call(q, k, v, sinks, [s.astype(BF16) for s in shards])
    return out[0], tuple(out[1:])


def _swa_nat_fwd(q, k, v, sinks, shards):
    out = _swa_nat_fwd_call(q, k, v, sinks, [s.astype(BF16) for s in shards])
    return (out[0], tuple(out[1:])), (q, k, v, sinks)


def _swa_nat_bwd(res, cts):
    q, k, v, sinks = res
    do, d_gathered = cts
    out = _swa_nat_bwd_call(q, k, v, sinks, do, list(d_gathered))
    dq, dkp, dkc, dvp, dvc, dsink = out[:6]

    def fold(prev_part, cur_part):
        shifted = jnp.concatenate([prev_part[SWA_BLOCK:], jnp.zeros_like(prev_part[:SWA_BLOCK])], axis=0)
        return (cur_part + shifted).astype(k.dtype)

    dsinks = jnp.sum(dsink.reshape(A_HEADS, SWA_BLOCK), axis=1)
    return dq, fold(dkp, dkc), fold(dvp, dvc), dsinks, _reduce_scatter_tail(out[6:], "mid_grads")


swa_nat.defvjp(_swa_nat_fwd, _swa_nat_bwd)

N_PAIR = B_HEADS // 2


def _flash_nat_fwd_call(q, k, v, shards):
    s = q.shape[0]
    t = min(FLASH_T, s)
    nb = s // t
    d = LANES
    n_arr = len(shards)

    def body(*refs):
        q_ref, k_ref, v_ref = refs[:3]
        shard_refs = refs[3:3 + n_arr]
        o_ref, lse_ref = refs[3 + n_arr:5 + n_arr]
        gathered_refs = refs[5 + n_arr:5 + 2 * n_arr]
        vt_ref, m_ref, l_ref, acc_ref = refs[5 + 2 * n_arr:9 + 2 * n_arr]
        pair, i = pl.program_id(0), pl.program_id(1)
        ag_start, ag_forward, ag_finish = _allgather_phases(shard_refs, gathered_refs, *refs[9 + 2 * n_arr:])

        @pl.when((pair == 0) & (i == 0))
        def _():
            ag_start()

        @pl.when((pair == N_PAIR - 1) & (i == nb // 2))
        def _():
            ag_forward()

        @pl.when(i == 0)
        def _():
            for hh in range(2):
                for chunk in range(nb):
                    rows = slice(chunk * t, (chunk + 1) * t)
                    vt_ref[hh, :, rows] = v_ref[rows, hh * d:(hh + 1) * d].T

        m_ref[...] = jnp.full_like(m_ref, -jnp.inf)
        l_ref[...] = jnp.zeros_like(l_ref)
        acc_ref[...] = jnp.zeros_like(acc_ref)

        def step(j, on_diagonal):
            keys = pl.ds(pl.multiple_of(j * t, t), t)
            scores = [lax.dot_general(k_ref[keys, hh * d:(hh + 1) * d], q_ref[:, hh * d:(hh + 1) * d], NT_DIMS,
                                      preferred_element_type=F32) for hh in range(2)]
            for hh in range(2):
                sc_t = scores[hh]
                if on_diagonal:
                    key = lax.broadcasted_iota(jnp.int32, (t, t), 0)
                    qry = lax.broadcasted_iota(jnp.int32, (t, t), 1)
                    sc_t = jnp.where(qry >= key, sc_t, -jnp.inf)
                m_old = m_ref[hh]
                m_new = jnp.maximum(m_old, jnp.max(sc_t, axis=0, keepdims=True))
                alpha = jnp.exp2((m_old - m_new) * EXP2_SCALE)
                p_t = jnp.exp2((sc_t - m_new) * EXP2_SCALE)
                l_ref[hh] = alpha * l_ref[hh] + jnp.sum(p_t, axis=0, keepdims=True)
                acc_ref[hh] = alpha * acc_ref[hh] + jnp.dot(vt_ref[hh, :, keys], p_t.astype(BF16),
                                                            preferred_element_type=F32)
                m_ref[hh] = m_new

        def below(j, carry):
            step(j, False)
            return carry

        lax.fori_loop(0, i, below, 0)
        step(i, True)
        outs =[(acc_ref[hh] / l_ref[hh]).T for hh in range(2)]
        for hh in range(2):
            lse_ref[hh] = m_ref[hh] * EXP2_SCALE + jnp.log2(l_ref[hh])
        o_ref[...] = (outs[0] + pltpu.roll(outs[1], HALF, 1)).astype(o_ref.dtype)

        @pl.when((pair == N_PAIR - 1) & (i == nb - 1))
        def _():
            ag_finish()

    return pl.pallas_call(
        body, name="mla_fwd", grid=(N_PAIR, nb),
        in_specs=[pl.BlockSpec((t, 2 * d), lambda p, i: (i, p)),
                  pl.BlockSpec((s, 2 * d), lambda p, i: (0, p)),
                  pl.BlockSpec((s, 2 * d), lambda p, i: (0, p))] + [HBM_SPEC] * n_arr,
        out_specs=[pl.BlockSpec((t, d), lambda p, i: (i, p)),
                   pl.BlockSpec((2, 1, t), lambda p, i: (p, 0, i))] + [HBM_SPEC] * n_arr,
        out_shape=[jax.ShapeDtypeStruct((s, N_PAIR * d), BF16), jax.ShapeDtypeStruct((B_HEADS, 1, s), F32)]
                  + _allgather_out_shapes(shards),
        scratch_shapes=[pltpu.VMEM((2, d, s), BF16), pltpu.VMEM((2, 1, t), F32), pltpu.VMEM((2, 1, t), F32),
                        pltpu.VMEM((2, d, t), F32)] + _allgather_sems(n_arr),
        compiler_params=_params("arbitrary", "arbitrary"),
    )(q, k, v, *shards)


def _flash_nat_delta_call(o, do):
    s, w = o.shape
    t = min(FLASH_T, s)

    def body(o_ref, do_ref, out_ref):
        prod = o_ref[...].astype(F32) * do_ref[...].astype(F32)
        lane = lax.broadcasted_iota(jnp.int32, (w, LANES), 0) // V_DIM
        head = lax.broadcasted_iota(jnp.int32, (w, LANES), 1)
        out_ref[...] = jnp.dot(prod, (lane == head).astype(F32), precision=lax.Precision.HIGHEST,
                               preferred_element_type=F32)

    spec = pl.BlockSpec((t, w), lambda i: (i, 0))
    return pl.pallas_call(
        body, name="mla_delta", grid=(s // t,), in_specs=[spec, spec],
        out_specs=pl.BlockSpec((t, LANES), lambda i: (i, 0)),
        out_shape=jax.ShapeDtypeStruct((s, LANES), F32), compiler_params=_params("parallel"),
    )(o, do)


def _flash_nat_bwd_call(q, k, v, lse_row, delta_row, do, parts):
    s = q.shape[0]
    t = min(FLASH_T, s)
    nb = s // t
    d = LANES
    n_arr = len(parts)

    def body(*refs):
        q_ref, k_ref, v_ref, lse_ref, delta_ref, do_ref = refs[:6]
        part_refs = refs[6:6 + n_arr]
        dq_ref, dk_ref, dv_ref = refs[6 + n_arr:9 + n_arr]
        received_refs = refs[9 + n_arr:9 + 2 * n_arr]
        dq_acc, dk_acc, dv_acc = refs[9 + 2 * n_arr:12 + 2 * n_arr]
        pair, j = pl.program_id(0), pl.program_id(1)
        exchange_start, exchange_finish = _exchange_all_phases(part_refs, received_refs, *refs[12 + 2 * n_arr:])

        @pl.when((pair == 0) & (j == 0))
        def _():
            exchange_start()

        @pl.when(j == 0)
        def _():
            dq_acc[...] = jnp.zeros_like(dq_acc)

        for hh in range(2):
            kb, vb = k_ref[:, hh * d:(hh + 1) * d], v_ref[:, hh * d:(hh + 1) * d]
            dk_acc[...] = jnp.zeros_like(dk_acc)
            dv_acc[...] = jnp.zeros_like(dv_acc)

            def step(i, on_diagonal, hh=hh, kb=kb, vb=vb):
                rows = pl.ds(pl.multiple_of(i * t, t), t)
                qb = q_ref[rows, hh * d:(hh + 1) * d]
                do_pair = do_ref[rows, :].astype(F32)
                do_h = do_pair if hh == 0 else pltpu.roll(do_pair, HALF, 1)
                dob = jnp.where(_low_half(do_h.shape), do_h, 0.0).astype(BF16)
                sc_t = lax.dot_general(kb, qb, NT_DIMS, preferred_element_type=F32)
                p_t = jnp.exp2(sc_t * EXP2_SCALE - lse_ref[hh, :, rows])
                if on_diagonal:
                    key = lax.broadcasted_iota(jnp.int32, (t, t), 0)
                    qry = lax.broadcasted_iota(jnp.int32, (t, t), 1)
                    p_t = jnp.where(qry >= key, p_t, 0.0)
                dp_t = lax.dot_general(vb, dob, NT_DIMS, preferred_element_type=F32)
                ds_t = (p_t * (dp_t - delta_ref[hh, :, rows])).astype(BF16)
                dv_acc[...] += jnp.dot(p_t.astype(BF16), dob, preferred_element_type=F32)
                dk_acc[...] += jnp.dot(ds_t, qb, preferred_element_type=F32)
                dq_acc[hh, rows, :] += lax.dot_general(ds_t, kb, TN_DIMS, preferred_element_type=F32)

            def above(i, carry, step=step):
                step(i, False)
                return carry

            step(j, True)
            lax.fori_loop(j + 1, nb, above, 0)
            dk_ref[:, hh * d:(hh + 1) * d] = (dk_acc[...] * MLA_SCALE).astype(dk_ref.dtype)
            dv_ref[:, hh * d:(hh + 1) * d] = dv_acc[...].astype(dv_ref.dtype)

        @pl.when(j == nb - 1)
        def _():
            for hh in range(2):
                dq_ref[:, hh * d:(hh + 1) * d] = (dq_acc[hh] * MLA_SCALE).astype(dq_ref.dtype)

        @pl.when((pair == N_PAIR - 1) & (j == nb - 1))
        def _():
            exchange_finish()

    full_spec = pl.BlockSpec((s, 2 * d), lambda p, j: (0, p))
    tile_spec = pl.BlockSpec((t, 2 * d), lambda p, j: (j, p))
    row_spec = pl.BlockSpec((2, 1, s), lambda p, j: (p, 0, 0))
    return pl.pallas_call(
        body, name="mla_bwd", grid=(N_PAIR, nb),
        in_specs=[full_spec, tile_spec, tile_spec, row_spec, row_spec, pl.BlockSpec((s, d), lambda p, j: (0, p))]
                 + [HBM_SPEC] * n_arr,
        out_specs=[full_spec, tile_spec, tile_spec] + [HBM_SPEC] * n_arr,
        out_shape=[jax.ShapeDtypeStruct(q.shape, q.dtype)] * 3 + [jax.ShapeDtypeStruct(p.shape, p.dtype) for p in parts],
        scratch_shapes=[pltpu.VMEM((2, s, d), F32), pltpu.VMEM((t, d), F32), pltpu.VMEM((t, d), F32)]
                       + _exchange_all_sems(n_arr),
        compiler_params=_params("arbitrary", "arbitrary"),
    )(q, k, v, lse_row, delta_row, do, *parts)


def _reduce_scatter_head(cts, tag):
    received = _exchange_sibling(list(cts), tag + "_exchange_sibling")
    my_c = lax.axis_index("c").astype(jnp.int32).reshape(1)
    return [_pair_add(m, r, my_c, "%s_pair_add_%d" % (tag, i)) for i, (m, r) in enumerate(zip(cts, received))]


def _reduce_scatter_tail(chip_parts, tag):
    return tuple(_sum_blocks(r, "%s_sum_%d" % (tag, i)) for i, r in enumerate(chip_parts))


@jax.custom_vjp
def flash_nat(q, k, v, shards):
    out = _flash_nat_fwd_call(q, k, v, [s.astype(BF16) for s in shards])
    return out[0], tuple(out[2:])


def _flash_nat_fwd(q, k, v, shards):
    out = _flash_nat_fwd_call(q, k, v, [s.astype(BF16) for s in shards])
    return (out[0], tuple(out[2:])), (q, k, v, out[0], out[1])


def _flash_nat_bwd(res, cts):
    q, k, v, o, lse = res
    do, d_gathered = cts
    delta = _flash_nat_delta_call(o, do)[:, :B_HEADS].T.reshape(B_HEADS, 1, q.shape[0])
    out = _flash_nat_bwd_call(q, k, v, lse, delta, do, list(d_gathered))
    return out[0], out[1], out[2], _reduce_scatter_tail(out[3:], "mlp_grads")


flash_nat.defvjp(_flash_nat_fwd, _flash_nat_bwd)


HBM_SPEC = pl.BlockSpec(memory_space=pltpu.HBM)


def _allgather(shards, name):
    n_arr = len(shards)

    def body(*refs):
        start, forward, finish = _allgather_phases(refs[:n_arr], refs[n_arr:2 * n_arr], *refs[2 * n_arr:])
        start()
        forward()
        finish()

    return pl.pallas_call(
        body, name=name, out_shape=_allgather_out_shapes(shards),
        in_specs=[HBM_SPEC] * n_arr, out_specs=[HBM_SPEC] * n_arr,
        scratch_shapes=_allgather_sems(n_arr),
    )(*shards)


def _allgather_out_shapes(shards):
    return [jax.ShapeDtypeStruct((N_DEV,) + s.shape, s.dtype) for s in shards]


def _allgather_sems(n_arr):
    return [pltpu.SemaphoreType.DMA((7, n_arr)), pltpu.SemaphoreType.DMA((7, n_arr)), pltpu.SemaphoreType.DMA((n_arr,))]


def _allgather_phases(x_refs, out_refs, send_sems, recv_sems, local_sems):
    arrays = range(len(x_refs))
    x, y, c = lax.axis_index("x"), lax.axis_index("y"), lax.axis_index("c")
    me, sibling = (x, y, c), (x, y, 1 - c)
    chips = [(1 - x, y), (x, 1 - y), (1 - x, 1 - y)]

    def rows(a, px, py, pc):
        return out_refs[a].at[4 * px + 2 * py + pc]

    def copy(a, k, block, to, src=None):
        return pltpu.make_async_remote_copy(
            src_ref=rows(a, *block) if src is None else src, dst_ref=rows(a, *block),
            send_sem=send_sems.at[k, a], recv_sem=recv_sems.at[k, a], device_id=to, device_id_type=MESH_ID)

    def mine():
        return [pltpu.make_async_copy(x_refs[a], rows(a, *me), local_sems.at[a]) for a in arrays]

    def first():
        return [cp for a in arrays for cp in
                [copy(a, 0, me, sibling, src=x_refs[a])]
                + [copy(a, 1 + j, me, (*chip, c), src=x_refs[a]) for j, chip in enumerate(chips)]]

    def passed():
        return [copy(a, 4 + j, (*chip, c), sibling) for j, chip in enumerate(chips) for a in arrays]

    def start():
        for cp in mine() + first():
            cp.start()

    def forward():
        for j, chip in enumerate(chips):
            for a in arrays:
                copy(a, 1 + j, (*chip, c), me).wait_recv()
                copy(a, 4 + j, (*chip, c), sibling).start()

    def finish():
        for a in arrays:
            copy(a, 0, sibling, me).wait_recv()
        for j, chip in enumerate(chips):
            for a in arrays:
                copy(a, 4 + j, (*chip, 1 - c), me).wait_recv()
        for cp in first() + passed():
            cp.wait_send()
        for cp in mine():
            cp.wait()

    return start, forward, finish


N_CHIP = 4


def _exchange_sibling(parts, name):
    n_arr = len(parts)

    def body(*refs):
        in_refs, recv_refs = refs[:n_arr], refs[n_arr:2 * n_arr]
        send_sems, recv_sems = refs[2 * n_arr:]
        x, y, c = lax.axis_index("x"), lax.axis_index("y"), lax.axis_index("c")
        copies = []
        for a in range(n_arr):
            for q in range(N_CHIP):
                copies.append(pltpu.make_async_remote_copy(
                    src_ref=in_refs[a].at[2 * q + 1 - c], dst_ref=recv_refs[a].at[q],
                    send_sem=send_sems.at[q, a], recv_sem=recv_sems.at[q, a],
                    device_id=(x, y, 1 - c), device_id_type=MESH_ID))
        for cp in copies:
            cp.start()
        for cp in copies:
            cp.wait()

    return pl.pallas_call(
        body, name=name, out_shape=[jax.ShapeDtypeStruct((N_CHIP,) + p.shape[1:], p.dtype) for p in parts],
        in_specs=[HBM_SPEC] * n_arr, out_specs=[HBM_SPEC] * n_arr,
        scratch_shapes=[pltpu.SemaphoreType.DMA((N_CHIP, n_arr)), pltpu.SemaphoreType.DMA((N_CHIP, n_arr))],
    )(*parts)


def _exchange_all_sems(n_arr):
    return [pltpu.SemaphoreType.DMA((N_DEV - 1, n_arr)), pltpu.SemaphoreType.DMA((N_DEV - 1, n_arr)),
            pltpu.SemaphoreType.DMA((n_arr,))]


def _exchange_all_phases(in_refs, out_refs, send_sems, recv_sems, local_sems):
    n_arr = len(in_refs)
    x, y, c = lax.axis_index("x"), lax.axis_index("y"), lax.axis_index("c")
    me = 4 * x + 2 * y + c

    def copies():
        out = [pltpu.make_async_copy(in_refs[a].at[me], out_refs[a].at[me], local_sems.at[a]) for a in range(n_arr)]
        for k in range(1, N_DEV):
            px = 1 - x if k & 4 else x
            py = 1 - y if k & 2 else y
            pc = 1 - c if k & 1 else c
            for a in range(n_arr):
                out.append(pltpu.make_async_remote_copy(
                    src_ref=in_refs[a].at[4 * px + 2 * py + pc], dst_ref=out_refs[a].at[me],
                    send_sem=send_sems.at[k - 1, a], recv_sem=recv_sems.at[k - 1, a],
                    device_id=(px, py, pc), device_id_type=MESH_ID))
        return out

    def start():
        for cp in copies():
            cp.start()

    def finish():
        for cp in copies():
            cp.wait()

    return start, finish


def _exchange_chips_sems(n_arr):
    return [pltpu.SemaphoreType.DMA((N_CHIP - 1, n_arr)), pltpu.SemaphoreType.DMA((N_CHIP - 1, n_arr)),
            pltpu.SemaphoreType.DMA((n_arr,))]


def _exchange_chips_phases(in_refs, out_refs, send_sems, recv_sems, local_sems):
    n_arr = len(in_refs)
    x, y, c = lax.axis_index("x"), lax.axis_index("y"), lax.axis_index("c")
    me = 2 * x + y

    def copies():
        out = [pltpu.make_async_copy(in_refs[a].at[me], out_refs[a].at[me], local_sems.at[a]) for a in range(n_arr)]
        for k in range(1, N_CHIP):
            px = 1 - x if k & 2 else x
            py = 1 - y if k & 1 else y
            for a in range(n_arr):
                out.append(pltpu.make_async_remote_copy(
                    src_ref=in_refs[a].at[2 * px + py], dst_ref=out_refs[a].at[me],
                    send_sem=send_sems.at[k - 1, a], recv_sem=recv_sems.at[k - 1, a],
                    device_id=(px, py, c), device_id_type=MESH_ID))
        return out

    def start():
        for cp in copies():
            cp.start()

    def finish():
        for cp in copies():
            cp.wait()

    return start, finish


def _row_tile(r, ccols, blocks):
    cap = max(16, (2 * 1024 * 1024) // (4 * ccols * blocks))
    return _pick(r, cap, 16)


def _pair_add(mine, theirs, my_c, name):
    _, r, ccols = mine.shape
    tr = _row_tile(r, ccols, 1)

    def body(c_ref, a_ref, b_ref, o_ref):
        o_ref[...] = (a_ref[...].astype(F32) + b_ref[...].astype(F32)).astype(o_ref.dtype)

    spec = pl.BlockSpec((None, tr, ccols), lambda q, i, c_ref: (q, i, 0))
    return pl.pallas_call(
        body, name=name,
        grid_spec=pltpu.PrefetchScalarGridSpec(
            num_scalar_prefetch=1, grid=(N_CHIP, r // tr),
            in_specs=[pl.BlockSpec((None, tr, ccols), lambda q, i, c_ref: (2 * q + c_ref[0], i, 0)), spec],
            out_specs=spec),
        out_shape=jax.ShapeDtypeStruct(theirs.shape, theirs.dtype),
        compiler_params=_params("parallel", "parallel"),
    )(my_c, mine, theirs)


def _sum_blocks(parts, name):
    nb, r, ccols = parts.shape
    tr = _row_tile(r, ccols, nb)

    def body(p_ref, o_ref):
        acc = p_ref[0].astype(F32)
        for i in range(1, nb):
            acc = acc + p_ref[i].astype(F32)
        o_ref[...] = acc

    return pl.pallas_call(
        body, name=name, grid=(r // tr,),
        in_specs=[pl.BlockSpec((nb, tr, ccols), lambda i: (0, i, 0))],
        out_specs=pl.BlockSpec((tr, ccols), lambda i: (i, 0)),
        out_shape=jax.ShapeDtypeStruct((r, ccols), F32),
        compiler_params=_params("parallel"),
    )(parts)


@jax.custom_vjp
def replicated(vec):
    return vec


def _replicated_fwd(vec):
    return vec, None


def _replicated_bwd(_, ct):
    return (_sum_blocks(_allgather([ct], "small_grad_allgather")[0], "small_grad_sum"),)


replicated.defvjp(_replicated_fwd, _replicated_bwd)


def _adamw(w, g, m, v, name):
    rows, cols = w.shape
    tr = _pick(rows, 256, 8) if rows % 8 == 0 else rows

    def body(w_ref, g_ref, m_ref, v_ref, d_ref, nm_ref, nv_ref):
        g_ = g_ref[...]
        m_ = ADAM_B1 * m_ref[...] + (1.0 - ADAM_B1) * g_
        v_ = ADAM_B2 * v_ref[...] + (1.0 - ADAM_B2) * jnp.square(g_)
        m_hat = m_ / (1.0 - ADAM_B1 ** ADAM_STEP)
        v_hat = v_ / (1.0 - ADAM_B2 ** ADAM_STEP)
        d_ref[...] = -ADAM_LR * (m_hat / (jnp.sqrt(v_hat) + ADAM_EPS) + ADAM_WD * w_ref[...])
        nm_ref[...] = m_
        nv_ref[...] = v_

    spec = pl.BlockSpec((tr, cols), lambda i: (i, 0))
    return pl.pallas_call(
        body, name=name, grid=(rows // tr,), in_specs=[spec] * 4, out_specs=[spec] * 3,
        out_shape=[jax.ShapeDtypeStruct(w.shape, F32)] * 3, compiler_params=_params("parallel"),
    )(w, g, m, v)


COL_SHARDED = ("w_in", "w_uq", "w_ukv", "w_branch_a", "w_branch_b", "w_up", "w_ple")
EARLY = ("w_in",)
MID = ("w_uq", "w_ukv", "w_branch_a", "w_branch_b", "w_out")
LATE = ("w_up", "w_down", "w_ple_gate", "w_ple")
SMALL = ("attn_pre_norm", "attn_post_norm", "b_gate", "q_a_norm", "kv_a_norm", "mlp_pre_norm", "mlp_post_norm",
         "conv_b", "ple_norm", "sinks")
SMALL_COLS = 128


def _pack_rows(arrays, cols, row_mult):
    flat = jnp.concatenate([a.reshape(-1) for a in arrays])
    pad = (-flat.shape[0]) % (cols * row_mult)
    return jnp.pad(flat, (0, pad)).reshape(-1, cols)


def _unpack_small(vec, shapes):
    flat = vec.reshape(-1)
    out, off = {}, 0
    for name in SMALL:
        n = shapes[name]
        out[name] = flat[off:off + n].reshape(1, n)
        off += n + (-n) % SMALL_COLS
    return out


def _pad_lanes(t, width):
    return jnp.pad(t, [(0, 0)] * (t.ndim - 1) + [(0, width - t.shape[-1])])


def _pad_rows(t, rows):
    return jnp.pad(t, [(0, 0)] * (t.ndim - 2) + [(0, rows - t.shape[-2]), (0, 0)])


FRONT_SIZES = (512, 128, 128, 256, 128)
FRONT_BOUNDS = (0, 512, 640, 768, 1024, 1152, 1280)
PE_LANE = NOPE_DIM


def _arrange_w_in_t(wt):
    k = wt.shape[1]
    n_front = sum(FRONT_SIZES)
    front, kr, gates = wt[:n_front], wt[n_front:n_front + ROPE_DIM], wt[n_front + ROPE_DIM:]
    kr_slab = jnp.concatenate([jnp.zeros((PE_LANE, k), wt.dtype), kr,
                               jnp.zeros((HEAD_PAD - PE_LANE - ROPE_DIM, k), wt.dtype)], axis=0)
    return jnp.concatenate([front, kr_slab], axis=0), gates


def _arrange_w_uq_t(wt):
    k = wt.shape[1]
    return _pad_rows(wt.reshape(B_HEADS, NOPE_DIM + ROPE_DIM, k), HEAD_PAD).reshape(B_HEADS * HEAD_PAD, k)


def _arrange_w_ukv_t(wt):
    k = wt.shape[1]
    w = wt.reshape(B_HEADS, 2, NOPE_DIM, k)
    slabs = [_pad_rows(w[:, part], HEAD_PAD).reshape(B_HEADS * HEAD_PAD, k) for part in range(2)]
    return jnp.concatenate(slabs, axis=0)


def _rope_tables(positions, s):
    pos = positions.reshape(s, 1).astype(F32)

    def angles(dim):
        return pos * ROPE_THETA ** (-(jnp.arange(0, dim, 2, dtype=F32) / dim))

    cos_a, sin_a = jnp.cos(angles(A_HEAD_DIM)), jnp.sin(angles(A_HEAD_DIM))
    zero_a = jnp.zeros_like(sin_a)
    tables_a = [jnp.tile(jnp.concatenate(pair, axis=1), (1, LANES // A_HEAD_DIM))
                for pair in ((cos_a, cos_a), (-sin_a, zero_a), (zero_a, sin_a))]
    cos_b, sin_b = jnp.cos(angles(ROPE_DIM)), jnp.sin(angles(ROPE_DIM))
    zero_b = jnp.zeros_like(sin_b)

    def slab(first, second, fill):
        return jnp.concatenate([jnp.full((s, PE_LANE), fill, F32), first, second,
                                jnp.full((s, HEAD_PAD - PE_LANE - ROPE_DIM), fill, F32)], axis=1)

    tables_b = [slab(cos_b, cos_b, 1.0), slab(-sin_b, zero_b, 0.0), slab(zero_b, sin_b, 0.0)]
    return tables_a + tables_b


def _local_loss(wts, x, p, tables, target):
    s = x.shape[0]
    small_shapes = {n: wts[n].shape[-1] for n in SMALL}
    small_vec = _pack_rows([_pad_lanes(wts[n].reshape(1, -1), small_shapes[n] + (-small_shapes[n]) % SMALL_COLS)
                            for n in SMALL], SMALL_COLS, 8)
    sm = _unpack_small(replicated(small_vec), small_shapes)
    def shard(n):
        return wts[n].T if n in COL_SHARDED else wts[n]

    h1_front, h1_gates, x_res, gathered = prenorm_gather(
        x, sm["attn_pre_norm"], tuple([shard(n) for n in EARLY] + [_pack_rows([wts["conv_w"]], SMALL_COLS, 8)]),
        (BF16,) * len(EARLY) + (F32,))
    big = {n: g.reshape(-1, g.shape[2]) for n, g in zip(EARLY, gathered)}
    ch = wts["conv_w"].shape[1]
    conv_w = gathered[-1].reshape(N_DEV, -1)[:, :CONV_W * ch].reshape(N_DEV, CONV_W, ch)
    conv_w = conv_w.transpose(1, 0, 2).reshape(CONV_W, N_DEV * ch)

    w_front_t, w_gates_t = _arrange_w_in_t(big["w_in"])
    tables_a, tables_b = tables[:3], tables[3:]

    qa, ka, va, cqn, ckvn, kpe = proj_stage(
        "prep", _f_prep, [(h1_front, w_front_t, "nt", "w_front", True, F32)], params=[sm["q_a_norm"], sm["kv_a_norm"]],
        consts=tables, splits=[FRONT_BOUNDS], ts=512, out_dtypes=[BF16, BF16, BF16, BF16, BF16, F32])
    ya, mid = swa_nat(qa, ka, va, sm["sinks"].reshape(-1), tuple(shard(n) for n in MID))
    big.update({n: g.reshape(-1, g.shape[2]) for n, g in zip(MID, mid)})

    (q2,) = proj_stage("qrope", _f_qrope, [(cqn, _arrange_w_uq_t(big["w_uq"]), "nt", "w_uq", True, BF16)],
                       consts=tables_b, ts=512, out_dtypes=[BF16])
    k2, v2 = proj_stage("kv", _f_kv, [(ckvn, _arrange_w_ukv_t(big["w_ukv"]), "nt", "w_ukv", True, BF16)],
                        extra=[kpe], splits=[(0, B_HEADS * HEAD_PAD, 2 * B_HEADS * HEAD_PAD), None], ts=512,
                        out_dtypes=[BF16, BF16])
    yb, late = flash_nat(q2, k2, v2, tuple(shard(n) for n in LATE))
    big.update({n: g.reshape(-1, g.shape[2]) for n, g in zip(LATE, late)})

    (mixed,) = proj_stage(
        "gate", _f_gate, [(h1_gates, w_gates_t, "nt", "w_gates", True, F32),
                          (ya, big["w_branch_a"], "nt", "w_branch_a", True, BF16),
                          (yb, big["w_branch_b"], "nt", "w_branch_b", True, BF16)],
        params=[sm["b_gate"][:, :D_MODEL], sm["b_gate"][:, D_MODEL:]],
        splits=[(0, D_MODEL, 2 * D_MODEL), None, None], out_dtypes=[BF16])
    x1, h2 = proj_stage("post_attn", _f_post, [(mixed, big["w_out"], "nn", "w_out", True, F32)], extra=[x_res],
                        params=[sm["attn_post_norm"], sm["mlp_pre_norm"]], ts=512, out_dtypes=[F32, BF16])

    act = mlp_up(h2, big["w_up"], conv_w, sm["conv_b"])
    x2, h3 = proj_stage("post_mlp", _f_post, [(act, big["w_down"], "nn", "w_down", True, F32)], extra=[x1],
                        params=[sm["mlp_post_norm"], sm["ple_norm"]], ts=512, out_dtypes=[F32, BF16])

    (rowloss,) = proj_stage("loss", _f_out, [(h3, big["w_ple_gate"], "nn", "w_ple_gate", True, F32),
                                             (p, big["w_ple"], "nt", "w_ple", False, BF16)], extra=[x2],
                            consts=[target], ts=512)
    return jnp.sum(rowloss)


WEIGHTS = ["attn_pre_norm", "attn_post_norm", "w_in", "b_gate", "sinks", "q_a_norm", "w_uq", "kv_a_norm", "w_ukv",
           "w_branch_a", "w_branch_b", "w_out", "mlp_pre_norm", "mlp_post_norm", "w_up", "conv_w", "conv_b",
           "w_down", "ple_norm", "w_ple_gate", "w_ple"]


def kernel(x, p, positions, attn_pre_norm, attn_post_norm, w_in, b_gate, sinks, q_a_norm, w_uq, kv_a_norm, w_ukv, w_branch_a, w_branch_b, w_out, mlp_pre_norm, mlp_post_norm, w_up, conv_w, conv_b, w_down, ple_norm, w_ple_gate, w_ple, loss_target, m_attn_pre_norm, m_attn_post_norm, m_w_in, m_b_gate, m_sinks, m_q_a_norm, m_w_uq, m_kv_a_norm, m_w_ukv, m_w_branch_a, m_w_branch_b, m_w_out, m_mlp_pre_norm, m_mlp_post_norm, m_w_up, m_conv_w, m_conv_b, m_w_down, m_ple_norm, m_w_ple_gate, m_w_ple, v_attn_pre_norm, v_attn_post_norm, v_w_in, v_b_gate, v_sinks, v_q_a_norm, v_w_uq, v_kv_a_norm, v_w_ukv, v_w_branch_a, v_w_branch_b, v_w_out, v_mlp_pre_norm, v_mlp_post_norm, v_w_up, v_conv_w, v_conv_b, v_w_down, v_ple_norm, v_w_ple_gate, v_w_ple):
    given = dict(locals())
    s = x.shape[1]
    wts = {n: given[n][0] if given[n].ndim == 3 else given[n] for n in WEIGHTS}
    tables = _rope_tables(positions, s)
    local_loss, (grads, grad_x) = jax.value_and_grad(_local_loss, argnums=(0, 1))(
        wts, x[0], p[0, 0], tables, loss_target[0])
    loss = lax.psum(local_loss, AXES)

    packed = _adamw(_pack_rows([wts[n] for n in SMALL], SMALL_COLS, 8),
                    _pack_rows([grads[n] for n in SMALL], SMALL_COLS, 8),
                    _pack_rows([given["m_" + n] for n in SMALL], SMALL_COLS, 8),
                    _pack_rows([given["v_" + n] for n in SMALL], SMALL_COLS, 8), "adamw_small")
    small_out, off = {}, 0
    for n in SMALL:
        size = wts[n].size
        small_out[n] = [t.reshape(-1)[off:off + size] for t in packed]
        off += size

    outs = {"grad": [], "delta": [], "m": [], "v": []}
    for n in WEIGHTS:
        shape = given[n].shape
        if n in small_out:
            delta, new_m, new_v = small_out[n]
            g2 = grads[n]
        else:
            w2 = wts[n].reshape(-1, shape[-1])
            g2 = grads[n].reshape(w2.shape)
            delta, new_m, new_v = _adamw(w2, g2, given["m_" + n].reshape(w2.shape),
                                         given["v_" + n].reshape(w2.shape), "adamw_" + n)
        outs["grad"].append(g2.reshape(shape))
        outs["delta"].append(delta.reshape(shape))
        outs["m"].append(new_m.reshape(shape))
        outs["v"].append(new_v.reshape(shape))
    return (loss, grad_x[None], *outs["grad"], *outs["delta"], *outs["m"], *outs["v"])
```
